```python
import math
import jax, jax.numpy as jnp
from jax import lax
import numpy as np

D_MODEL = 1024
BATCH = 8
SEQ = 2048
DEPTH = 1

HG_HEADS = 4
HG_DK = 128
HG_DV = 128
HG_WIDTH = HG_HEADS * HG_DV
HG_CHUNK = 16
NSA_HEADS = 8
NSA_KV_HEADS = 2
NSA_HEAD_DIM = 64
NSA_WIDTH = NSA_HEADS * NSA_HEAD_DIM
KV_WIDTH = NSA_KV_HEADS * NSA_HEAD_DIM
CMP_BLOCK = 32
CMP_STRIDE = 16
CMP_HIDDEN = 256
SLC_BLOCK = 64
SLC_TOPK = 16
SLC_Q_BLOCK = 64
WINDOW = 512
Q_BLOCK = 128
ROPE_THETA = 500000.0
ROPE_DIM = NSA_HEAD_DIM // 4
D_FF = 2816
CONV_WIDTH = 3
EPS = 1e-6
MIX_WIDTH = HG_WIDTH + NSA_WIDTH
PROJ_SPLITS = (HG_WIDTH, HG_WIDTH, HG_WIDTH, HG_WIDTH, NSA_WIDTH, KV_WIDTH, KV_WIDTH, KV_WIDTH, KV_WIDTH, KV_WIDTH, KV_WIDTH, 3 * NSA_HEADS)
PROJ_WIDTH = sum(PROJ_SPLITS)
NEG_INF = -1e30
FORCE_SCORE = 1e4

kernel_name = 'hymba_hgrn2_nsa_convffn'

F32 = jnp.float32


def rms_norm(x, gain):
    xf = x.astype(F32)
    y = xf * lax.rsqrt(jnp.mean(xf * xf, axis=-1, keepdims=True) + EPS)
    return (y * gain.astype(F32)).astype(x.dtype)


def head_rms(o):
    return o * lax.rsqrt(jnp.mean(o * o, axis=-1, keepdims=True) + EPS)


def partial_rope(x, positions):
    half = ROPE_DIM // 2
    inv_freq = ROPE_THETA ** (-jnp.arange(half, dtype=F32) * 2.0 / ROPE_DIM)
    ang = positions.astype(F32)[..., None] * inv_freq
    cos = jnp.cos(ang)[:, :, None, :]
    sin = jnp.sin(ang)[:, :, None, :]
    xf = x.astype(F32)
    x1 = xf[..., :half]
    x2 = xf[..., half:ROPE_DIM]
    out = jnp.concatenate([x1 * cos - x2 * sin, x2 * cos + x1 * sin, xf[..., ROPE_DIM:]], axis=-1)
    return out.astype(x.dtype)


def masked_softmax(s, mask):
    return jax.nn.softmax(jnp.where(mask, s, NEG_INF), axis=-1)


def hgrn2_mixer(q, f_pre, i_in, g, lb, out_gain):
    B, T, _ = q.shape
    H, C = HG_HEADS, HG_CHUNK
    N = T // C
    lb = lb.astype(F32)
    f = lb + (1.0 - lb) * jax.nn.sigmoid(f_pre.astype(F32))
    log_f = jnp.log(f)
    k = 1.0 - f

    def to_chunks(t, d):
        return t.astype(F32).reshape(B, N, C, H, d).transpose(1, 0, 3, 2, 4)

    qh = to_chunks(q, HG_DK)
    kh = to_chunks(k, HG_DK)
    vh = to_chunks(i_in, HG_DV)
    b = jnp.cumsum(to_chunks(log_f, HG_DK), axis=3)
    b_last = b[:, :, :, -1:, :]
    q_dec = qh * jnp.exp(b)
    k_inv = kh * jnp.exp(-b)
    k_dec = kh * jnp.exp(b_last - b)
    chunk_decay = jnp.exp(b_last[:, :, :, 0, :])

    causal = jnp.tril(jnp.ones((C, C), dtype=bool))
    attn = jnp.where(causal, jnp.einsum('nbhcd,nbhsd->nbhcs', q_dec, k_inv), 0.0)
    o_intra = jnp.einsum('nbhcs,nbhsv->nbhcv', attn, vh)

    def step(S, xs):
        q_d, k_d, v_c, dec = xs
        o = jnp.einsum('bhcd,bhdv->bhcv', q_d, S)
        S = S * dec[..., :, None] + jnp.einsum('bhsd,bhsv->bhdv', k_d, v_c)
        return S, o

    S0 = jnp.zeros((B, H, HG_DK, HG_DV), F32)
    _, o_inter = lax.scan(step, S0, (q_dec, k_dec, vh, chunk_decay))
    o = (o_intra + o_inter).transpose(1, 0, 3, 2, 4).reshape(B, T, H, HG_DV)
    o = head_rms(o).reshape(B, T, HG_WIDTH) * out_gain.astype(F32)
    return (o * jax.nn.silu(g.astype(F32))).astype(q.dtype)


def compress_blocks(kv, pe, w1, w2):
    B, T, G, D = kv.shape
    n_cmp = (T - CMP_BLOCK) // CMP_STRIDE + 1
    tok = jnp.arange(n_cmp)[:, None] * CMP_STRIDE + jnp.arange(CMP_BLOCK)[None, :]
    blk = kv[:, tok] + pe[:, None, :].astype(kv.dtype)
    blk = blk.transpose(0, 3, 1, 2, 4).reshape(B, G, n_cmp, CMP_BLOCK * D)
    return jax.nn.silu(blk @ w1) @ w2


def nsa_mixer(q, k_cmp, v_cmp, k_slc, v_slc, k_win, v_win, gate_pre, positions,
              pe_k, pe_v, ck_w1, ck_w2, cv_w1, cv_w2, out_gain):
    B, T, _ = q.shape
    H, G, D = NSA_HEADS, NSA_KV_HEADS, NSA_HEAD_DIM
    HPG = H // G
    dt = q.dtype
    scale = D ** -0.5
    qr = partial_rope(q.reshape(B, T, H, D), positions)
    qg = qr.reshape(B, T, G, HPG, D).transpose(0, 2, 3, 1, 4)
    kv_heads = lambda t: t.reshape(B, T, G, D)
    kc_raw = partial_rope(kv_heads(k_cmp), positions)
    ks = partial_rope(kv_heads(k_slc), positions)
    kw = partial_rope(kv_heads(k_win), positions)
    t_idx = jnp.arange(T)

    kc = compress_blocks(kc_raw, pe_k, ck_w1, ck_w2)
    vc = compress_blocks(kv_heads(v_cmp), pe_v, cv_w1, cv_w2)
    n_cmp = kc.shape[2]
    cmp_start = jnp.arange(n_cmp) * CMP_STRIDE
    cmp_end = cmp_start + CMP_BLOCK - 1
    cmp_mask = cmp_end[None, :] <= t_idx[:, None]
    s_cmp = jnp.einsum('bghtd,bgnd->bghtn', qg, kc).astype(F32) * scale
    p_cmp = jnp.where(cmp_mask, masked_softmax(s_cmp, cmp_mask), 0.0)
    o_cmp = jnp.einsum('bghtn,bgnd->bghtd', p_cmp.astype(dt), vc)

    n_sel = T // SLC_BLOCK
    sel_start = jnp.arange(n_sel) * SLC_BLOCK
    overlap = ((cmp_start[:, None] <= sel_start[None, :] + SLC_BLOCK - 1)
               & (cmp_end[:, None] >= sel_start[None, :])).astype(F32)
    p_sel = jnp.einsum('bghtn,ns->bgts', p_cmp, overlap)
    cur = t_idx // SLC_BLOCK
    blk = jnp.arange(n_sel)
    forced = (blk[None, :] == 0) | (blk[None, :] == cur[:, None]) | (blk[None, :] == cur[:, None] - 1)
    score = jnp.where(forced, FORCE_SCORE, p_sel)
    score = jnp.where(blk[None, :] <= cur[:, None], score, -jnp.inf)
    k_eff = min(SLC_TOPK, n_sel)
    _, sel_idx = lax.top_k(score, k_eff)

    k_blocks = ks.transpose(0, 2, 1, 3).reshape(B, G, n_sel, SLC_BLOCK, D)
    v_blocks = kv_heads(v_slc).transpose(0, 2, 1, 3).reshape(B, G, n_sel, SLC_BLOCK, D)
    nqb = T // SLC_Q_BLOCK
    q_sb = jnp.moveaxis(qg.reshape(B, G, HPG, nqb, SLC_Q_BLOCK, D), 3, 0)
    idx_sb = jnp.moveaxis(sel_idx.reshape(B, G, nqb, SLC_Q_BLOCK, k_eff), 2, 0)
    t_sb = t_idx.reshape(nqb, SLC_Q_BLOCK)
    bi = jnp.arange(B)[:, None, None, None]
    gi = jnp.arange(G)[None, :, None, None]

    def sel_block(args):
        qb, ib, tb = args
        kb = k_blocks[bi, gi, ib].reshape(B, G, SLC_Q_BLOCK, k_eff * SLC_BLOCK, D)
        vb = v_blocks[bi, gi, ib].reshape(B, G, SLC_Q_BLOCK, k_eff * SLC_BLOCK, D)
        tok = (ib[..., None] * SLC_BLOCK + jnp.arange(SLC_BLOCK)).reshape(B, G, SLC_Q_BLOCK, k_eff * SLC_BLOCK)
        mask = (tok <= tb[None, None, :, None])[:, :, None]
        s = jnp.einsum('bghqd,bgqjd->bghqj', qb, kb).astype(F32) * scale
        p = masked_softmax(s, mask)
        return jnp.einsum('bghqj,bgqjd->bghqd', p.astype(dt), vb)

    o_slc = lax.map(sel_block, (q_sb, idx_sb, t_sb))
    o_slc = jnp.moveaxis(o_slc, 0, 3).reshape(B, G, HPG, T, D)

    nwb = T // Q_BLOCK
    span = WINDOW + Q_BLOCK
    win_tok = jnp.arange(nwb)[:, None] * Q_BLOCK + jnp.arange(span)[None, :]
    pad = ((0, 0), (WINDOW, 0), (0, 0), (0, 0))
    kw_b = jnp.pad(kw, pad)[:, win_tok]
    vw_b = jnp.pad(kv_heads(v_win), pad)[:, win_tok]
    kpos = win_tok - WINDOW
    tq = jnp.arange(nwb)[:, None] * Q_BLOCK + jnp.arange(Q_BLOCK)[None, :]
    rel = tq[:, :, None] - kpos[:, None, :]
    win_mask = (kpos[:, None, :] >= 0) & (rel >= 0) & (rel < WINDOW)
    qw = qg.reshape(B, G, HPG, nwb, Q_BLOCK, D)
    s_win = jnp.einsum('bghwqd,bwjgd->bghwqj', qw, kw_b).astype(F32) * scale
    p_win = masked_softmax(s_win, win_mask)
    o_win = jnp.einsum('bghwqj,bwjgd->bghwqd', p_win.astype(dt), vw_b).reshape(B, G, HPG, T, D)

    gates = jax.nn.sigmoid(gate_pre.astype(F32)).reshape(B, T, 3, G, HPG)
    gates = gates.transpose(2, 0, 3, 4, 1)[..., None]
    o = gates[0] * o_cmp.astype(F32) + gates[1] * o_slc.astype(F32) + gates[2] * o_win.astype(F32)
    o = head_rms(o.transpose(0, 3, 1, 2, 4)).reshape(B, T, NSA_WIDTH) * out_gain.astype(F32)
    return o.astype(dt)


def conv_ffn(h, w_gate, w_up, conv_w, conv_b, w_down):
    gate = h @ w_gate
    gate = lax.conv_general_dilated(gate, conv_w[:, None, :].astype(gate.dtype), window_strides=(1,),
                                    padding=[(CONV_WIDTH - 1, 0)],
                                    dimension_numbers=('NWC', 'WIO', 'NWC'),
                                    feature_group_count=D_FF) + conv_b
    return (jax.nn.silu(gate) * (h @ w_up)) @ w_down


def setup_inputs(seed: int = 0) -> dict:
    key = jax.random.key(seed)
    ks = jax.random.split(key, 24)

    def nrm(k, shape, scale):
        return jax.random.normal(k, shape, F32) * scale

    cmp_in = CMP_BLOCK * NSA_HEAD_DIM
    return {
        'x': nrm(ks[0], (BATCH, SEQ, D_MODEL), 1.0),
        'positions': jax.random.randint(ks[1], (BATCH, 1), 0, 1024, dtype=jnp.int32) + jnp.arange(SEQ, dtype=jnp.int32)[None, :],
        'ln1_gain': 1.0 + nrm(ks[2], (DEPTH, D_MODEL), 0.02),
        'w_in': nrm(ks[3], (DEPTH, D_MODEL, PROJ_WIDTH), D_MODEL ** -0.5),
        'hgrn_lb_param': nrm(ks[4], (DEPTH + 1, HG_WIDTH), 0.1),
        'hgrn_out_gain': 1.0 + nrm(ks[5], (DEPTH, HG_WIDTH), 0.02),
        'cmp_pe_k': nrm(ks[6], (DEPTH, CMP_BLOCK, NSA_HEAD_DIM), 0.1),
        'cmp_pe_v': nrm(ks[7], (DEPTH, CMP_BLOCK, NSA_HEAD_DIM), 0.1),
        'cmp_k_w1': nrm(ks[8], (DEPTH, cmp_in, CMP_HIDDEN), cmp_in ** -0.5),
        'cmp_k_w2': nrm(ks[9], (DEPTH, CMP_HIDDEN, NSA_HEAD_DIM), CMP_HIDDEN ** -0.5),
        'cmp_v_w1': nrm(ks[10], (DEPTH, cmp_in, CMP_HIDDEN), cmp_in ** -0.5),
        'cmp_v_w2': nrm(ks[11], (DEPTH, CMP_HIDDEN, NSA_HEAD_DIM), CMP_HIDDEN ** -0.5),
        'nsa_out_gain': 1.0 + nrm(ks[12], (DEPTH, NSA_WIDTH), 0.02),
        'w_o': nrm(ks[13], (DEPTH, MIX_WIDTH, D_MODEL), MIX_WIDTH ** -0.5),
        'ln2_gain': 1.0 + nrm(ks[14], (DEPTH, D_MODEL), 0.02),
        'ffn_w_gate': nrm(ks[15], (DEPTH, D_MODEL, D_FF), D_MODEL ** -0.5),
        'ffn_w_up': nrm(ks[16], (DEPTH, D_MODEL, D_FF), D_MODEL ** -0.5),
        'ffn_conv_w': nrm(ks[17], (DEPTH, CONV_WIDTH, D_FF), CONV_WIDTH ** -0.5),
        'ffn_conv_b': nrm(ks[18], (DEPTH, D_FF), 0.02),
        'ffn_w_down': nrm(ks[19], (DEPTH, D_FF, D_MODEL), D_FF ** -0.5),
        'final_gain': 1.0 + nrm(ks[20], (D_MODEL,), 0.02),
    }


def reference(x, positions, ln1_gain, w_in, hgrn_lb_param, hgrn_out_gain, cmp_pe_k, cmp_pe_v,
              cmp_k_w1, cmp_k_w2, cmp_v_w1, cmp_v_w2, nsa_out_gain, w_o, ln2_gain,
              ffn_w_gate, ffn_w_up, ffn_conv_w, ffn_conv_b, ffn_w_down, final_gain):
    lower_bounds = jnp.cumsum(jax.nn.softmax(hgrn_lb_param.astype(F32), axis=0), axis=0)
    offsets = np.cumsum(PROJ_SPLITS)[:-1].tolist()
    for l in range(DEPTH):
        h = rms_norm(x, ln1_gain[l])
        proj = h @ w_in[l]
        (hq, hf, hi, hg, nq, kc, vc, ksl, vsl, kwn, vwn, ngate) = jnp.split(proj, offsets, axis=-1)
        o_h = hgrn2_mixer(hq, hf, hi, hg, lower_bounds[l], hgrn_out_gain[l])
        o_n = nsa_mixer(nq, kc, vc, ksl, vsl, kwn, vwn, ngate, positions,
                        cmp_pe_k[l], cmp_pe_v[l], cmp_k_w1[l], cmp_k_w2[l], cmp_v_w1[l], cmp_v_w2[l],
                        nsa_out_gain[l])
        x = x + jnp.concatenate([o_h, o_n], axis=-1) @ w_o[l]
        h = rms_norm(x, ln2_gain[l])
        x = x + conv_ffn(h, ffn_w_gate[l], ffn_w_up[l], ffn_conv_w[l], ffn_conv_b[l], ffn_w_down[l])
    return rms_norm(x, final_gain)
```

```python
import functools
import math

import jax
import jax.numpy as jnp
import numpy as np
from jax import lax
from jax.experimental import pallas as pl
from jax.experimental.pallas import tpu as pltpu

F32 = jnp.float32
BF16 = jnp.bfloat16

D_MODEL = 1024
HG_HEADS = 4
HG_DK = 128
HG_DV = 128
HG_WIDTH = HG_HEADS * HG_DV
NSA_HEADS = 8
NSA_KV_HEADS = 2
NSA_HEAD_DIM = 64
HPG = NSA_HEADS // NSA_KV_HEADS
NSA_WIDTH = NSA_HEADS * NSA_HEAD_DIM
KV_WIDTH = NSA_KV_HEADS * NSA_HEAD_DIM
CMP_BLOCK = 32
CMP_STRIDE = 16
CMP_HIDDEN = 256
SLC_BLOCK = 64
SLC_TOPK = 16
WINDOW = 512
ROPE_THETA = 500000.0
ROPE_DIM = NSA_HEAD_DIM // 4
ROPE_HALF = ROPE_DIM // 2
D_FF = 2816
EPS = 1e-6
NEG_INF = -1e30
FORCE_SCORE = 1e4
N_GATES = 3 * NSA_HEADS

LANES = 128
VMEM_LIMIT = 56 * 1024 * 1024

PROJ_TM = 512
HG_CHUNK = 128
HG_LEVELS = (16, 32, 64)
HG_DIAG = 16
ATT_TQ = 256
ATT_TK = 256
FFN_TM = 512
FFN_TC = 256
FFN_NC = D_FF // FFN_TC


def _dot(a, b):
    return jnp.dot(a, b, preferred_element_type=F32)


def _dot_nt(a, b):
    return lax.dot_general(a, b, (((1,), (1,)), ((), ())), preferred_element_type=F32)


def _dot_tn(a, b):
    return lax.dot_general(a, b, (((0,), (0,)), ((), ())), preferred_element_type=F32)


def _split3(x):
    hi = x.astype(BF16)
    r = x - hi.astype(F32)
    mid = r.astype(BF16)
    lo = (r - mid.astype(F32)).astype(BF16)
    return hi, mid, lo


def _rms(x, gain):
    return x * lax.rsqrt(jnp.mean(x * x, axis=-1, keepdims=True) + EPS) * gain


def _inproj_body(x_ref, g_ref, wh_ref, wq_ref, wkv_ref, wg_ref, c_ref, s1_ref, s2_ref,
                 hg_ref, nq_ref, kvc_ref, kvsw_ref, gate_ref):
    hb = _rms(x_ref[...], g_ref[...]).astype(BF16)
    hg_ref[...] = _dot(hb, wh_ref[...])
    cos = c_ref[...]
    sin_hi = s1_ref[...]
    sin_lo = s2_ref[...]

    def rope(v):
        return v * cos + pltpu.roll(v, ROPE_HALF, 1) * sin_hi + pltpu.roll(v, LANES - ROPE_HALF, 1) * sin_lo

    q = _dot(hb, wq_ref[...])
    scale = NSA_HEAD_DIM ** -0.5
    for j in range(NSA_WIDTH // LANES):
        sl = slice(j * LANES, (j + 1) * LANES)
        nq_ref[:, sl] = (rope(q[:, sl]) * scale).astype(BF16)
    kv = _dot(hb, wkv_ref[...])
    kvc_ref[:, 0:LANES] = rope(kv[:, 0:LANES])
    kvc_ref[:, LANES:2 * LANES] = kv[:, LANES:2 * LANES]
    kvsw_ref[:, 0:LANES] = rope(kv[:, 2 * LANES:3 * LANES]).astype(BF16)
    kvsw_ref[:, LANES:2 * LANES] = kv[:, 3 * LANES:4 * LANES].astype(BF16)
    kvsw_ref[:, 2 * LANES:3 * LANES] = rope(kv[:, 4 * LANES:5 * LANES]).astype(BF16)
    kvsw_ref[:, 3 * LANES:4 * LANES] = kv[:, 5 * LANES:6 * LANES].astype(BF16)
    gate_ref[...] = jax.nn.sigmoid(_dot(hb, wg_ref[...]))


def _inproj_call(x2, gain, wh, wq, wkv, wg, cos_t, sin_hi_t, sin_lo_t):
    n = x2.shape[0]
    tm = PROJ_TM
    row = lambda w: pl.BlockSpec((tm, w), lambda i: (i, 0))
    full = lambda a: pl.BlockSpec(a.shape, lambda i: (0, 0))
    return pl.pallas_call(
        _inproj_body,
        grid=(n // tm,),
        in_specs=[row(D_MODEL), full(gain), full(wh), full(wq), full(wkv), full(wg),
                  row(LANES), row(LANES), row(LANES)],
        out_specs=[row(4 * HG_WIDTH), row(NSA_WIDTH), row(2 * KV_WIDTH), row(4 * KV_WIDTH),
                   row(NSA_KV_HEADS * LANES)],
        out_shape=[jax.ShapeDtypeStruct((n, 4 * HG_WIDTH), F32),
                   jax.ShapeDtypeStruct((n, NSA_WIDTH), BF16),
                   jax.ShapeDtypeStruct((n, 2 * KV_WIDTH), F32),
                   jax.ShapeDtypeStruct((n, 4 * KV_WIDTH), BF16),
                   jax.ShapeDtypeStruct((n, NSA_KV_HEADS * LANES), F32)],
        compiler_params=pltpu.CompilerParams(dimension_semantics=("arbitrary",),
                                             vmem_limit_bytes=VMEM_LIMIT),
        name="inproj",
    )(x2, gain, wh, wq, wkv, wg, cos_t, sin_hi_t, sin_lo_t)


def _hgrn_tables():
    L = HG_CHUNK
    t = np.arange(L)[:, None]
    u = np.arange(L)[None, :]
    mats = [((t // HG_DIAG) == (u // HG_DIAG)) & (u <= t)]
    level = np.where(mats[0], 1, 0)
    for li, s in enumerate(HG_LEVELS):
        same = (t // (2 * s)) == (u // (2 * s))
        mid = (t // (2 * s)) * (2 * s) + s
        right = (t % (2 * s)) >= s
        m = np.where(right, same & (u >= mid) & (u <= t), same & (u > t) & (u < mid))
        mats.append(m)
        level = np.where(same & right & ((u % (2 * s)) < s), li + 2, level)
    mats.append(u <= t)
    stack = np.concatenate([m.astype(np.float32) for m in mats], axis=0)
    return jnp.asarray(stack, BF16), jnp.asarray(level, jnp.int32)


def _hgrn_body(q_ref, f_ref, i_ref, g_ref, lb_ref, gain_ref, mst_ref, lvl_ref, o_ref, st_ref):
    L = HG_CHUNK
    n_chunks = q_ref.shape[1] // L
    st_ref[...] = jnp.zeros_like(st_ref)
    lb = lb_ref[...]
    gain = gain_ref[...]

    def chunk(c, carry):
        rows = pl.ds(pl.multiple_of(c * L, L), L)
        q = q_ref[0, rows, :]
        v = i_ref[0, rows, :]
        f = lb + (1.0 - lb) * jax.nn.sigmoid(f_ref[0, rows, :])
        logf = jnp.log(f)
        k = 1.0 - f
        hi, mid, lo = _split3(logf)
        mst = mst_ref[...]
        e = (_dot(mst, hi) + _dot(mst, mid)) + _dot(mst, lo)
        e_diag = e[0:L]
        e_full = e[(len(HG_LEVELS) + 1) * L:(len(HG_LEVELS) + 2) * L]
        lvl = lvl_ref[...]
        a = jnp.where(lvl == 1,
                      _dot_nt((q * jnp.exp(e_diag)).astype(BF16), (k * jnp.exp(-e_diag)).astype(BF16)), 0.0)
        for li in range(len(HG_LEVELS)):
            w = jnp.exp(e[(li + 1) * L:(li + 2) * L])
            a = jnp.where(lvl == li + 2, _dot_nt((q * w).astype(BF16), (k * w).astype(BF16)), a)
        vb = v.astype(BF16)
        o = _dot(a.astype(BF16), vb)
        st = st_ref[...]
        o = o + _dot_nt((q * jnp.exp(e_full)).astype(BF16), st.astype(BF16))
        b_last = e_full[L - 1:L, :]
        k_dec = (k * jnp.exp(b_last - e_full)).astype(BF16)
        st_ref[...] = st * jnp.exp(b_last) + _dot_tn(vb, k_dec)
        o = o * lax.rsqrt(jnp.mean(o * o, axis=-1, keepdims=True) + EPS) * gain
        o_ref[0, rows, :] = (o * jax.nn.silu(g_ref[0, rows, :])).astype(o_ref.dtype)
        return carry

    lax.fori_loop(0, n_chunks, chunk, 0)


def _hgrn_call(hg, lb, gain, mst, lvl):
    b, t, _ = hg.shape
    col = lambda off: pl.BlockSpec((1, t, HG_DK), lambda bi, hi: (bi, 0, off + hi))
    vec = pl.BlockSpec((1, HG_DK), lambda bi, hi: (0, hi))
    full = lambda a: pl.BlockSpec(a.shape, lambda bi, hi: (0, 0))
    return pl.pallas_call(
        _hgrn_body,
        grid=(b, HG_HEADS),
        in_specs=[col(0), col(HG_HEADS), col(2 * HG_HEADS), col(3 * HG_HEADS), vec, vec, full(mst), full(lvl)],
        out_specs=pl.BlockSpec((1, t, HG_DV), lambda bi, hi: (bi, 0, hi)),
        out_shape=jax.ShapeDtypeStruct((b, t, HG_WIDTH), BF16),
        scratch_shapes=[pltpu.VMEM((HG_DV, HG_DK), F32)],
        compiler_params=pltpu.CompilerParams(dimension_semantics=("arbitrary", "arbitrary"),
                                             vmem_limit_bytes=VMEM_LIMIT),
        name="hgrn2",
    )(hg, hg, hg, hg, lb, gain, mst, lvl)


def _cmp_body(xk_ref, xv_ref, pek_ref, pev_ref, w1k_ref, w1v_ref, w2k_ref, w2v_ref, kc_ref, vc_ref):
    nb = xk_ref.shape[2]

    def compress(x, pe_ref, w1_ref, w2_ref, out_ref):
        u = _dot((x + pe_ref[0:1, :]).astype(BF16), w1_ref[0])
        v = _dot((x + pe_ref[1:2, :]).astype(BF16), w1_ref[1])
        hidden = jax.nn.silu(u + pltpu.roll(v, nb - 1, 0)).astype(BF16)
        out_ref[0, 0, 0:nb, :] = _dot(hidden, w2_ref[0]).astype(out_ref.dtype)
        out_ref[0, 0, nb:2 * nb, :] = _dot(hidden, w2_ref[1]).astype(out_ref.dtype)

    compress(xk_ref[0, 0], pek_ref, w1k_ref, w2k_ref, kc_ref)
    compress(xv_ref[0, 0], pev_ref, w1v_ref, w2v_ref, vc_ref)


def _cmp_call(xk, xv, pek, pev, w1k, w1v, w2k, w2v):
    b, g, nb, w = xk.shape
    blk = pl.BlockSpec((1, 1, nb, w), lambda bi, gi: (bi, gi, 0, 0))
    full = lambda a: pl.BlockSpec(a.shape, lambda bi, gi: (0,) * a.ndim)
    out = pl.BlockSpec((1, 1, 2 * nb, LANES), lambda bi, gi: (bi, gi, 0, 0))
    return pl.pallas_call(
        _cmp_body,
        grid=(b, g),
        in_specs=[blk, blk, full(pek), full(pev), full(w1k), full(w1v), full(w2k), full(w2v)],
        out_specs=[out, out],
        out_shape=[jax.ShapeDtypeStruct((b, g, 2 * nb, LANES), BF16)] * 2,
        compiler_params=pltpu.CompilerParams(dimension_semantics=("arbitrary", "arbitrary"),
                                             vmem_limit_bytes=VMEM_LIMIT),
        name="nsa_compress",
    )(xk, xv, pek, pev, w1k, w1v, w2k, w2v)


def _nsa_body(q_ref, kvsw_ref, kc_ref, vc_ref, gate_ref, gain_ref, ov_ref, exp_ref, o_ref,
              ks_ref, vs_ref, kw_ref, vw_ref, m_ref, l_ref, acc_ref):
    g = pl.program_id(1)
    qi = pl.program_id(2)
    tq = ATT_TQ
    tk = ATT_TK
    t_len = kvsw_ref.shape[1]
    n_kt = t_len // tk
    n_pairs = HPG // 2

    lane = lax.broadcasted_iota(jnp.int32, (tk, LANES), 1)
    lo_half = lane < NSA_HEAD_DIM
    lo1 = lax.broadcasted_iota(jnp.int32, (1, LANES), 1) < NSA_HEAD_DIM

    @pl.when(qi == 0)
    def _build_kv():
        keep = (lane // NSA_HEAD_DIM) == g

        def build(src_col, dst_ref):
            def body(j, carry):
                rows = pl.ds(pl.multiple_of(j * tk, tk), tk)
                x = kvsw_ref[0, rows, src_col * LANES:(src_col + 1) * LANES].astype(F32)
                dup = jnp.where(keep, x, pltpu.roll(x, NSA_HEAD_DIM, 1))
                dst_ref[j, 0:tk, :] = jnp.where(lo_half, dup, 0.0).astype(BF16)
                dst_ref[j, tk:2 * tk, :] = jnp.where(lo_half, 0.0, dup).astype(BF16)
                return carry
            lax.fori_loop(0, n_kt, body, 0)

        build(0, ks_ref)
        build(1, vs_ref)
        build(2, kw_ref)
        build(3, vw_ref)

    t0 = qi * tq
    row_t = t0 + lax.broadcasted_iota(jnp.int32, (tq, tk), 0)
    col_i = lax.broadcasted_iota(jnp.int32, (tq, tk), 1)
    row128 = t0 + lax.broadcasted_iota(jnp.int32, (tq, LANES), 0)
    lane128 = lax.broadcasted_iota(jnp.int32, (tq, LANES), 1)
    q_pairs = [q_ref[0, :, p * LANES:(p + 1) * LANES] for p in range(n_pairs)]

    n_cmp_pad = kc_ref.shape[2] // 2
    cmp_ok = (lane128 * CMP_STRIDE + (CMP_BLOCK - 1)) <= row128
    kc = kc_ref[0, 0]
    vc = vc_ref[0, 0]
    p_sum = jnp.zeros((tq, n_cmp_pad), F32)
    o_cmp = []
    for p in range(n_pairs):
        s = _dot_nt(q_pairs[p], kc)
        probs = []
        for h in range(2):
            sh = jnp.where(cmp_ok, s[:, h * n_cmp_pad:(h + 1) * n_cmp_pad], NEG_INF)
            mh = jnp.max(sh, axis=-1, keepdims=True)
            eh = jnp.where(cmp_ok, jnp.exp(sh - mh), 0.0)
            den = jnp.sum(eh, axis=-1, keepdims=True)
            ph = eh / jnp.where(den > 0.0, den, 1.0)
            p_sum = p_sum + ph
            probs.append(ph)
        o_cmp.append(_dot(jnp.concatenate(probs, axis=1).astype(BF16), vc))

    hi, mid, lo = _split3(p_sum)
    ov = ov_ref[...]
    p_sel = (_dot(hi, ov) + _dot(mid, ov)) + _dot(lo, ov)
    cur = row128 // SLC_BLOCK
    forced = (lane128 == 0) | (lane128 == cur) | (lane128 == cur - 1)
    score = jnp.where(forced, FORCE_SCORE, p_sel)
    score = jnp.where(lane128 <= cur, score, -jnp.inf)
    n_sel = t_len // SLC_BLOCK
    rank = jnp.zeros((tq, LANES), jnp.int32)
    for i in range(n_sel):
        ci = score[:, i:i + 1]
        ahead = (ci > score) | ((ci == score) & (lane128 > i))
        rank = rank + jnp.where(ahead, 1, 0)
    chosen = jnp.where(rank < min(SLC_TOPK, n_sel), 1.0, 0.0).astype(BF16)

    def flash(k_ref, v_ref, j_lo, j_hi, allowed_fn):
        m_ref[...] = jnp.full_like(m_ref, -jnp.inf)
        l_ref[...] = jnp.zeros_like(l_ref)
        acc_ref[...] = jnp.zeros_like(acc_ref)

        def body(j, carry):
            allowed = allowed_fn(j)
            kt = k_ref[j]
            vt = v_ref[j]
            for p in range(n_pairs):
                s = _dot_nt(q_pairs[p], kt)
                probs = []
                alphas = []
                for h in range(2):
                    hh = 2 * p + h
                    sh = jnp.where(allowed, s[:, h * tk:(h + 1) * tk], NEG_INF)
                    m_prev = m_ref[hh]
                    m_new = jnp.maximum(m_prev, jnp.max(sh, axis=-1, keepdims=True))
                    alpha = jnp.exp(m_prev - m_new)
                    ph = jnp.exp(sh - m_new)
                    l_ref[hh] = alpha * l_ref[hh] + jnp.sum(ph, axis=-1, keepdims=True)
                    m_ref[hh] = m_new
                    probs.append(ph)
                    alphas.append(alpha)
                a_pair = jnp.where(lo1, alphas[0], alphas[1])
                acc_ref[p] = acc_ref[p] * a_pair + _dot(jnp.concatenate(probs, axis=1).astype(BF16), vt)
            return carry

        lax.fori_loop(j_lo, j_hi, body, 0)
        outs = []
        for p in range(n_pairs):
            inv = jnp.where(lo1, 1.0 / l_ref[2 * p], 1.0 / l_ref[2 * p + 1])
            outs.append(acc_ref[p] * inv)
        return outs

    def slc_allowed(j):
        picked = _dot(chosen, exp_ref[j]) > 0.5
        return picked & ((j * tk + col_i) <= row_t)

    def win_allowed(j):
        rel = row_t - (j * tk + col_i)
        return (rel >= 0) & (rel < WINDOW)

    last = (t0 + tq - 1) // tk
    o_slc = flash(ks_ref, vs_ref, 0, last + 1, slc_allowed)
    first_w = jnp.maximum(t0 - (WINDOW - 1), 0) // tk
    o_win = flash(kw_ref, vw_ref, first_w, last + 1, win_allowed)

    gates = gate_ref[0]
    gain = gain_ref[...]
    for p in range(n_pairs):
        o = jnp.zeros((tq, LANES), F32)
        for c, branch in enumerate((o_cmp[p], o_slc[p], o_win[p])):
            ga = gates[:, c * HPG + 2 * p:c * HPG + 2 * p + 1]
            gb = gates[:, c * HPG + 2 * p + 1:c * HPG + 2 * p + 2]
            o = o + jnp.where(lo1, ga, gb) * branch
        sq = o * o
        ms_lo = jnp.sum(jnp.where(lo1, sq, 0.0), axis=-1, keepdims=True)
        ms_hi = jnp.sum(jnp.where(lo1, 0.0, sq), axis=-1, keepdims=True)
        ms = jnp.where(lo1, ms_lo, ms_hi) * (1.0 / NSA_HEAD_DIM)
        o = o * lax.rsqrt(ms + EPS) * gain[:, p * LANES:(p + 1) * LANES]
        o_ref[0, :, p * LANES:(p + 1) * LANES] = o.astype(o_ref.dtype)


def _nsa_call(nq, kvsw, kc, vc, gates, gain, ov, expand):
    b, t, _ = nq.shape
    tq, tk = ATT_TQ, ATT_TK
    n_kt = t // tk
    gw = HPG * NSA_HEAD_DIM
    kv_scratch = pltpu.VMEM((n_kt, 2 * tk, LANES), BF16)
    return pl.pallas_call(
        _nsa_body,
        grid=(b, NSA_KV_HEADS, t // tq),
        in_specs=[
            pl.BlockSpec((1, tq, gw), lambda bi, gi, qi: (bi, qi, gi)),
            pl.BlockSpec((1, t, 4 * KV_WIDTH), lambda bi, gi, qi: (bi, 0, 0)),
            pl.BlockSpec((1, 1) + kc.shape[2:], lambda bi, gi, qi: (bi, gi, 0, 0)),
            pl.BlockSpec((1, 1) + vc.shape[2:], lambda bi, gi, qi: (bi, gi, 0, 0)),
            pl.BlockSpec((1, tq, LANES), lambda bi, gi, qi: (bi, qi, gi)),
            pl.BlockSpec((1, gw), lambda bi, gi, qi: (0, gi)),
            pl.BlockSpec(ov.shape, lambda bi, gi, qi: (0, 0)),
            pl.BlockSpec(expand.shape, lambda bi, gi, qi: (0, 0, 0)),
        ],
        out_specs=pl.BlockSpec((1, tq, gw), lambda bi, gi, qi: (bi, qi, gi)),
        out_shape=jax.ShapeDtypeStruct((b, t, NSA_WIDTH), BF16),
        scratch_shapes=[kv_scratch, kv_scratch, kv_scratch, kv_scratch,
                        pltpu.VMEM((HPG, tq, 1), F32), pltpu.VMEM((HPG, tq, 1), F32),
                        pltpu.VMEM((HPG // 2, tq, LANES), F32)],
        compiler_params=pltpu.CompilerParams(dimension_semantics=("arbitrary", "arbitrary", "arbitrary"),
                                             vmem_limit_bytes=VMEM_LIMIT),
        name="nsa_attention",
    )(nq, kvsw, kc, vc, gates, gain, ov, expand)


def _ffn_body(x_ref, oh_ref, on_ref, woh_ref, won_ref, g2_ref, wg_ref, wu_ref, wd_ref, cw_ref, gf_ref,
              out_ref, halo_ref, *, tiles_per_seq):
    tm = x_ref.shape[0]
    x1 = x_ref[...] + _dot(oh_ref[...], woh_ref[...]) + _dot(on_ref[...], won_ref[...])
    hb = _rms(x1, g2_ref[...]).astype(BF16)
    row = lax.broadcasted_iota(jnp.int32, (tm, FFN_TC), 0)

    @pl.when((pl.program_id(0) % tiles_per_seq) == 0)
    def _sequence_start():
        halo_ref[...] = jnp.zeros_like(halo_ref)

    def chunk(c, acc):
        gate = _dot(hb, wg_ref[c])
        halo = halo_ref[c]
        halo_ref[c] = gate[tm - 8:tm, :]
        prev1 = jnp.where(row == 0, halo[7:8, :], pltpu.roll(gate, 1, 0))
        prev2 = jnp.where(row == 0, halo[6:7, :], jnp.where(row == 1, halo[7:8, :], pltpu.roll(gate, 2, 0)))
        cw = cw_ref[c]
        y = cw[0:1, :] * prev2 + cw[1:2, :] * prev1 + cw[2:3, :] * gate + cw[3:4, :]
        act = (jax.nn.silu(y) * _dot(hb, wu_ref[c])).astype(BF16)
        return acc + _dot(act, wd_ref[c])

    acc = lax.fori_loop(0, FFN_NC, chunk, jnp.zeros((tm, D_MODEL), F32))
    out_ref[...] = _rms(x1 + acc, gf_ref[...])


def _ffn_call(x2, oh, on, woh, won, g2, wg3, wu3, wd3, cw3, gf, tiles_per_seq):
    n = x2.shape[0]
    tm = FFN_TM
    row = lambda w: pl.BlockSpec((tm, w), lambda i: (i, 0))
    full = lambda a: pl.BlockSpec(a.shape, lambda i: (0,) * a.ndim, pipeline_mode=pl.Buffered(1))
    return pl.pallas_call(
        functools.partial(_ffn_body, tiles_per_seq=tiles_per_seq),
        grid=(n // tm,),
        in_specs=[row(D_MODEL), row(HG_WIDTH), row(NSA_WIDTH), full(woh), full(won), full(g2),
                  full(wg3), full(wu3), full(wd3), full(cw3), full(gf)],
        out_specs=row(D_MODEL),
        out_shape=jax.ShapeDtypeStruct((n, D_MODEL), F32),
        scratch_shapes=[pltpu.VMEM((FFN_NC, 8, FFN_TC), F32)],
        compiler_params=pltpu.CompilerParams(dimension_semantics=("arbitrary",),
                                             vmem_limit_bytes=VMEM_LIMIT),
        name="outproj_convffn",
    )(x2, oh, on, woh, won, g2, wg3, wu3, wd3, cw3, gf)


def _rope_tables(positions):
    inv_freq = ROPE_THETA ** (-jnp.arange(ROPE_HALF, dtype=F32) * 2.0 / ROPE_DIM)
    ang = positions.astype(F32)[..., None] * inv_freq
    cos = jnp.cos(ang)
    sin = jnp.sin(ang)
    rest = NSA_HEAD_DIM - ROPE_DIM
    ones = jnp.ones(ang.shape[:-1] + (rest,), F32)
    zeros = jnp.zeros(ang.shape[:-1] + (rest,), F32)
    z8 = jnp.zeros_like(sin)
    head = lambda parts: jnp.tile(jnp.concatenate(parts, axis=-1), (1, 1, LANES // NSA_HEAD_DIM))
    cos_t = head([cos, cos, ones])
    sin_hi_t = head([z8, sin, zeros])
    sin_lo_t = head([-sin, z8, zeros])
    flat = lambda a: a.reshape(-1, LANES)
    return flat(cos_t), flat(sin_hi_t), flat(sin_lo_t)


def _layer(x, positions, ln1, w_in, lb, hg_gain, pe_k, pe_v, k_w1, k_w2, v_w1, v_w2, nsa_gain, w_o, ln2,
           w_gate, w_up, conv_w, conv_b, w_down, final_gain):
    b, t, d = x.shape
    n = b * t
    assert d == D_MODEL and t % FFN_TM == 0 and t % ATT_TQ == 0 and t % HG_CHUNK == 0
    n_grp = t // CMP_STRIDE
    assert n_grp == LANES, "compressed-block axis is laid out on exactly one lane tile"
    assert t // SLC_BLOCK <= LANES
    x2 = x.reshape(n, d)

    o_h, o_q, o_kv, o_g = 0, 4 * HG_WIDTH, 4 * HG_WIDTH + NSA_WIDTH, 4 * HG_WIDTH + NSA_WIDTH + 6 * KV_WIDTH
    wh = w_in[:, o_h:o_q].astype(BF16)
    wq = w_in[:, o_q:o_kv].astype(BF16)
    wkv = w_in[:, o_kv:o_g].astype(BF16)
    wgate = w_in[:, o_g:o_g + N_GATES].reshape(d, 3, NSA_KV_HEADS, HPG).transpose(0, 2, 1, 3)
    wgate = wgate.reshape(d, NSA_KV_HEADS, 3 * HPG)
    wgate = jnp.pad(wgate, ((0, 0), (0, 0), (0, LANES - 3 * HPG))).reshape(d, NSA_KV_HEADS * LANES).astype(BF16)
    cos_t, sin_hi_t, sin_lo_t = _rope_tables(positions)

    hg, nq, kvc, kvsw, gates = _inproj_call(x2, ln1.reshape(1, d), wh, wq, wkv, wgate, cos_t, sin_hi_t, sin_lo_t)

    mst, lvl = _hgrn_tables()
    o_hg = _hgrn_call(hg.reshape(b, t, 4 * HG_WIDTH), lb.reshape(1, HG_WIDTH).astype(F32),
                      hg_gain.reshape(1, HG_WIDTH), mst, lvl)

    grp_w = CMP_STRIDE * NSA_HEAD_DIM
    to_groups = lambda a: (a.reshape(b, n_grp, CMP_STRIDE, NSA_KV_HEADS, NSA_HEAD_DIM)
                           .transpose(0, 3, 1, 2, 4).reshape(b, NSA_KV_HEADS, n_grp, grp_w))
    kvc3 = kvc.reshape(b, t, 2 * KV_WIDTH)
    xk = to_groups(kvc3[:, :, 0:KV_WIDTH])
    xv = to_groups(kvc3[:, :, KV_WIDTH:2 * KV_WIDTH])
    zeros_w2 = jnp.zeros((CMP_HIDDEN, NSA_HEAD_DIM), F32)
    place = lambda w2: jnp.stack([jnp.concatenate([w2, zeros_w2], 1), jnp.concatenate([zeros_w2, w2], 1)]).astype(BF16)
    kc, vc = _cmp_call(xk, xv, pe_k.reshape(2, grp_w), pe_v.reshape(2, grp_w),
                       k_w1.reshape(2, grp_w, CMP_HIDDEN).astype(BF16), v_w1.reshape(2, grp_w, CMP_HIDDEN).astype(BF16),
                       place(k_w2), place(v_w2))

    n_cmp_pad = n_grp
    n_sel = t // SLC_BLOCK
    cmp_start = np.arange(n_cmp_pad) * CMP_STRIDE
    cmp_end = cmp_start + CMP_BLOCK - 1
    sel_start = np.arange(LANES) * SLC_BLOCK
    overlap = ((cmp_start[:, None] <= sel_start[None, :] + SLC_BLOCK - 1) & (cmp_end[:, None] >= sel_start[None, :])
               & (np.arange(LANES)[None, :] < n_sel) & (np.arange(n_cmp_pad)[:, None] < n_cmp_pad - 1))
    ov = jnp.asarray(overlap.astype(np.float32), BF16)
    blk_of_key = (np.arange(t) // SLC_BLOCK)
    expand = (np.arange(LANES)[:, None] == blk_of_key[None, :]).astype(np.float32)
    expand = jnp.asarray(expand.reshape(LANES, t // ATT_TK, ATT_TK).transpose(1, 0, 2), BF16)
    o_nsa = _nsa_call(nq.reshape(b, t, NSA_WIDTH), kvsw.reshape(b, t, 4 * KV_WIDTH), kc, vc,
                      gates.reshape(b, t, NSA_KV_HEADS * LANES), nsa_gain.reshape(1, NSA_WIDTH), ov, expand)

    chunks = lambda w: w.reshape(d, FFN_NC, FFN_TC).transpose(1, 0, 2).astype(BF16)
    cw3 = jnp.concatenate([conv_w, conv_b[None, :], jnp.zeros((4, D_FF), F32)], axis=0)
    cw3 = cw3.reshape(8, FFN_NC, FFN_TC).transpose(1, 0, 2)
    out = _ffn_call(x2, o_hg.reshape(n, HG_WIDTH), o_nsa.reshape(n, NSA_WIDTH),
                    w_o[:HG_WIDTH].astype(BF16), w_o[HG_WIDTH:].astype(BF16), ln2.reshape(1, d),
                    chunks(w_gate), chunks(w_up), w_down.reshape(FFN_NC, FFN_TC, d).astype(BF16), cw3,
                    final_gain.reshape(1, d), t // FFN_TM)
    return out.reshape(b, t, d)


def kernel(x, positions, ln1_gain, w_in, hgrn_lb_param, hgrn_out_gain, cmp_pe_k, cmp_pe_v, cmp_k_w1, cmp_k_w2,
           cmp_v_w1, cmp_v_w2, nsa_out_gain, w_o, ln2_gain, ffn_w_gate, ffn_w_up, ffn_conv_w, ffn_conv_b,
           ffn_w_down, final_gain):
    depth = ln1_gain.shape[0]
    assert depth == 1, "the fused final norm assumes a single layer"
    lower_bounds = jnp.cumsum(jax.nn.softmax(hgrn_lb_param.astype(F32), axis=0), axis=0)
    l = 0
    return _layer(x, positions, ln1_gain[l], w_in[l], lower_bounds[l], hgrn_out_gain[l], cmp_pe_k[l], cmp_pe_v[l],
                  cmp_k_w1[l], cmp_k_w2[l], cmp_v_w1[l], cmp_v_w2[l], nsa_out_gain[l], w_o[l], ln2_gain[l],
                  ffn_w_gate[l], ffn_w_up[l], ffn_conv_w[l], ffn_conv_b[l], ffn_w_down[l], final_gain)
```

```python
import functools

import jax
import jax.numpy as jnp
import numpy as np
from jax import lax
from jax.experimental import pallas as pl
from jax.experimental.pallas import tpu as pltpu

F32 = jnp.float32
BF16 = jnp.bfloat16

D_MODEL = 1024
HG_HEADS = 4
HG_DK = 128
HG_DV = 128
HG_WIDTH = HG_HEADS * HG_DV
NSA_HEADS = 8
NSA_KV_HEADS = 2
NSA_HEAD_DIM = 64
HPG = NSA_HEADS // NSA_KV_HEADS
NSA_WIDTH = NSA_HEADS * NSA_HEAD_DIM
KV_WIDTH = NSA_KV_HEADS * NSA_HEAD_DIM
CMP_BLOCK = 32
CMP_STRIDE = 16
CMP_HIDDEN = 256
SLC_BLOCK = 64
SLC_TOPK = 16
WINDOW = 512
ROPE_THETA = 500000.0
ROPE_DIM = NSA_HEAD_DIM // 4
ROPE_HALF = ROPE_DIM // 2
D_FF = 2816
EPS = 1e-6
NEG_INF = -1e30
FORCE_SCORE = 1e4
N_GATES = 3 * NSA_HEADS

LANES = 128
VMEM_LIMIT = 56 * 1024 * 1024

PROJ_TM = 512
HG_CHUNK = 128
HG_LEVELS = (16, 32, 64)
HG_DIAG = 16
ATT_TQ = 256
ATT_TK = 256
FFN_TM = 512
FFN_TC = 256
FFN_NC = D_FF // FFN_TC


def _dot(a, b):
    return jnp.dot(a, b, preferred_element_type=F32)


def _dot_nt(a, b):
    return lax.dot_general(a, b, (((1,), (1,)), ((), ())), preferred_element_type=F32)


def _dot_tn(a, b):
    return lax.dot_general(a, b, (((0,), (0,)), ((), ())), preferred_element_type=F32)


def _split3(x):
    hi = x.astype(BF16)
    r = x - hi.astype(F32)
    mid = r.astype(BF16)
    lo = (r - mid.astype(F32)).astype(BF16)
    return hi, mid, lo


def _rms(x, gain):
    return x * lax.rsqrt(jnp.mean(x * x, axis=-1, keepdims=True) + EPS) * gain


def _inproj_body(x_ref, g_ref, wh_ref, wk_ref, wt_ref, c_ref, s1_ref, s2_ref, ct_ref, s1t_ref, s2t_ref,
                 hg_ref, kvc_ref, ksw_ref, qt_ref, vt_ref, gt_ref):
    hb = _rms(x_ref[...], g_ref[...]).astype(BF16)
    hg_ref[...] = _dot(hb, wh_ref[...])

    def rope(v, axis, cos, sin_hi, sin_lo):
        return (v * cos + pltpu.roll(v, ROPE_HALF, axis) * sin_hi
                + pltpu.roll(v, LANES - ROPE_HALF, axis) * sin_lo)

    tab = (c_ref[...], s1_ref[...], s2_ref[...])
    kn = _dot(hb, wk_ref[...])
    kvc_ref[:, 0:LANES] = rope(kn[:, 0:LANES], 1, *tab)
    kvc_ref[:, LANES:2 * LANES] = kn[:, LANES:2 * LANES]
    ksw_ref[:, 0:LANES] = rope(kn[:, 2 * LANES:3 * LANES], 1, *tab).astype(BF16)
    ksw_ref[:, LANES:2 * LANES] = rope(kn[:, 3 * LANES:4 * LANES], 1, *tab).astype(BF16)

    rt = _dot_nt(wt_ref[...], hb)
    tab_t = (ct_ref[0], s1t_ref[0], s2t_ref[0])
    scale = NSA_HEAD_DIM ** -0.5
    for j in range(NSA_WIDTH // LANES):
        sl = slice(j * LANES, (j + 1) * LANES)
        qt_ref[0, sl, :] = (rope(rt[sl], 0, *tab_t) * scale).astype(BF16)
    vt_ref[0] = rt[NSA_WIDTH:NSA_WIDTH + 2 * KV_WIDTH].astype(BF16)
    gt_ref[0] = jax.nn.sigmoid(rt[NSA_WIDTH + 2 * KV_WIDTH:])


def _inproj_call(x2, gain, wh, wk, wt, tabs, tabs_t, tiles_per_seq):
    n = x2.shape[0]
    tm = PROJ_TM
    t = tiles_per_seq * tm
    b = n // t
    row = lambda w: pl.BlockSpec((tm, w), lambda i: (i, 0))
    col = lambda h: pl.BlockSpec((1, h, tm), lambda i: (i // tiles_per_seq, 0, i % tiles_per_seq))
    full = lambda a: pl.BlockSpec(a.shape, lambda i: (0, 0))
    gate_rows = NSA_KV_HEADS * LANES
    return pl.pallas_call(
        _inproj_body,
        grid=(n // tm,),
        in_specs=[row(D_MODEL), full(gain), full(wh), full(wk), full(wt),
                  row(LANES), row(LANES), row(LANES), col(LANES), col(LANES), col(LANES)],
        out_specs=[row(4 * HG_WIDTH), row(2 * KV_WIDTH), row(2 * KV_WIDTH),
                   col(NSA_WIDTH), col(2 * KV_WIDTH), col(gate_rows)],
        out_shape=[jax.ShapeDtypeStruct((n, 4 * HG_WIDTH), F32),
                   jax.ShapeDtypeStruct((n, 2 * KV_WIDTH), F32),
                   jax.ShapeDtypeStruct((n, 2 * KV_WIDTH), BF16),
                   jax.ShapeDtypeStruct((b, NSA_WIDTH, t), BF16),
                   jax.ShapeDtypeStruct((b, 2 * KV_WIDTH, t), BF16),
                   jax.ShapeDtypeStruct((b, gate_rows, t), F32)],
        compiler_params=pltpu.CompilerParams(dimension_semantics=("arbitrary",),
                                             vmem_limit_bytes=VMEM_LIMIT),
        name="inproj",
    )(x2, gain, wh, wk, wt, *tabs, *tabs_t)


def _hgrn_tables():
    L = HG_CHUNK
    t = np.arange(L)[:, None]
    u = np.arange(L)[None, :]
    mats = [((t // HG_DIAG) == (u // HG_DIAG)) & (u <= t)]
    level = np.where(mats[0], 1, 0)
    for li, s in enumerate(HG_LEVELS):
        same = (t // (2 * s)) == (u // (2 * s))
        mid = (t // (2 * s)) * (2 * s) + s
        right = (t % (2 * s)) >= s
        m = np.where(right, same & (u >= mid) & (u <= t), same & (u > t) & (u < mid))
        mats.append(m)
        level = np.where(same & right & ((u % (2 * s)) < s), li + 2, level)
    mats.append(u <= t)
    stack = np.concatenate([m.astype(np.float32) for m in mats], axis=0)
    return jnp.asarray(stack, BF16), jnp.asarray(level, jnp.int32)


def _hgrn_body(q_ref, f_ref, i_ref, g_ref, lb_ref, gain_ref, mst_ref, lvl_ref, o_ref, st_ref):
    L = HG_CHUNK
    n_chunks = q_ref.shape[1] // L
    st_ref[...] = jnp.zeros_like(st_ref)
    lb = lb_ref[...]
    gain = gain_ref[...]

    def chunk(c, carry):
        rows = pl.ds(pl.multiple_of(c * L, L), L)
        q = q_ref[0, rows, :]
        v = i_ref[0, rows, :]
        f = lb + (1.0 - lb) * jax.nn.sigmoid(f_ref[0, rows, :])
        logf = jnp.log(f)
        k = 1.0 - f
        hi, mid, lo = _split3(logf)
        mst = mst_ref[...]
        e = (_dot(mst, hi) + _dot(mst, mid)) + _dot(mst, lo)
        e_diag = e[0:L]
        e_full = e[(len(HG_LEVELS) + 1) * L:(len(HG_LEVELS) + 2) * L]
        lvl = lvl_ref[...]
        a = jnp.where(lvl == 1,
                      _dot_nt((q * jnp.exp(e_diag)).astype(BF16), (k * jnp.exp(-e_diag)).astype(BF16)), 0.0)
        for li in range(len(HG_LEVELS)):
            w = jnp.exp(e[(li + 1) * L:(li + 2) * L])
            a = jnp.where(lvl == li + 2, _dot_nt((q * w).astype(BF16), (k * w).astype(BF16)), a)
        vb = v.astype(BF16)
        o = _dot(a.astype(BF16), vb)
        st = st_ref[...]
        o = o + _dot_nt((q * jnp.exp(e_full)).astype(BF16), st.astype(BF16))
        b_last = e_full[L - 1:L, :]
        k_dec = (k * jnp.exp(b_last - e_full)).astype(BF16)
        st_ref[...] = st * jnp.exp(b_last) + _dot_tn(vb, k_dec)
        o = o * lax.rsqrt(jnp.mean(o * o, axis=-1, keepdims=True) + EPS) * gain
        o_ref[0, rows, :] = (o * jax.nn.silu(g_ref[0, rows, :])).astype(o_ref.dtype)
        return carry

    lax.fori_loop(0, n_chunks, chunk, 0)


def _hgrn_call(hg, lb, gain, mst, lvl):
    b, t, _ = hg.shape
    col = lambda off: pl.BlockSpec((1, t, HG_DK), lambda bi, hi: (bi, 0, off + hi))
    vec = pl.BlockSpec((1, HG_DK), lambda bi, hi: (0, hi))
    full = lambda a: pl.BlockSpec(a.shape, lambda bi, hi: (0, 0))
    return pl.pallas_call(
        _hgrn_body,
        grid=(b, HG_HEADS),
        in_specs=[col(0), col(HG_HEADS), col(2 * HG_HEADS), col(3 * HG_HEADS), vec, vec, full(mst), full(lvl)],
        out_specs=pl.BlockSpec((1, t, HG_DV), lambda bi, hi: (bi, 0, hi)),
        out_shape=jax.ShapeDtypeStruct((b, t, HG_WIDTH), BF16),
        scratch_shapes=[pltpu.VMEM((HG_DV, HG_DK), F32)],
        compiler_params=pltpu.CompilerParams(dimension_semantics=("arbitrary", "arbitrary"),
                                             vmem_limit_bytes=VMEM_LIMIT),
        name="hgrn2",
    )(hg, hg, hg, hg, lb, gain, mst, lvl)


def _cmp_body(xk_ref, xv_ref, pek_ref, pev_ref, w1k_ref, w1v_ref, w2k_ref, w2v_ref, kc_ref, vc_ref):
    nb = xk_ref.shape[2]

    def hidden(x, pe_ref, w1_ref):
        u = _dot((x + pe_ref[0:1, :]).astype(BF16), w1_ref[0])
        v = _dot((x + pe_ref[1:2, :]).astype(BF16), w1_ref[1])
        return jax.nn.silu(u + pltpu.roll(v, nb - 1, 0)).astype(BF16)

    hk = hidden(xk_ref[0, 0], pek_ref, w1k_ref)
    kc_ref[0, 0, 0:nb, :] = _dot(hk, w2k_ref[0]).astype(kc_ref.dtype)
    kc_ref[0, 0, nb:2 * nb, :] = _dot(hk, w2k_ref[1]).astype(kc_ref.dtype)
    hv = hidden(xv_ref[0, 0], pev_ref, w1v_ref)
    vc_ref[0, 0, :, 0:nb] = _dot_nt(w2v_ref[0], hv).astype(vc_ref.dtype)
    vc_ref[0, 0, :, nb:2 * nb] = _dot_nt(w2v_ref[1], hv).astype(vc_ref.dtype)


def _cmp_call(xk, xv, pek, pev, w1k, w1v, w2k, w2v):
    b, g, nb, w = xk.shape
    blk = pl.BlockSpec((1, 1, nb, w), lambda bi, gi: (bi, gi, 0, 0))
    full = lambda a: pl.BlockSpec(a.shape, lambda bi, gi: (0,) * a.ndim)
    out = lambda r, c: pl.BlockSpec((1, 1, r, c), lambda bi, gi: (bi, gi, 0, 0))
    return pl.pallas_call(
        _cmp_body,
        grid=(b, g),
        in_specs=[blk, blk, full(pek), full(pev), full(w1k), full(w1v), full(w2k), full(w2v)],
        out_specs=[out(2 * nb, LANES), out(LANES, 2 * nb)],
        out_shape=[jax.ShapeDtypeStruct((b, g, 2 * nb, LANES), BF16),
                   jax.ShapeDtypeStruct((b, g, LANES, 2 * nb), BF16)],
        compiler_params=pltpu.CompilerParams(dimension_semantics=("arbitrary", "arbitrary"),
                                             vmem_limit_bytes=VMEM_LIMIT),
        name="nsa_compress",
    )(xk, xv, pek, pev, w1k, w1v, w2k, w2v)


def _nsa_body(qt_ref, ksw_ref, vst_ref, vwt_ref, kc_ref, vct_ref, gt_ref, gain_ref, ovt_ref, expt_ref, o_ref,
              ks_ref, kw_ref, vs_ref, vw_ref, m_ref, l_ref, acc_ref):
    g = pl.program_id(1)
    qi = pl.program_id(2)
    tq = ATT_TQ
    tk = ATT_TK
    t_len = ksw_ref.shape[1]
    n_kt = t_len // tk
    n_pairs = HPG // 2
    hd = NSA_HEAD_DIM

    @pl.when(qi == 0)
    def _build_kv():
        lane = lax.broadcasted_iota(jnp.int32, (tk, LANES), 1)
        lo_lane = lane < hd
        keep = (lane // hd) == g

        def build_k(src_col, dst_ref):
            def body(j, carry):
                rows = pl.ds(pl.multiple_of(j * tk, tk), tk)
                x = ksw_ref[0, rows, src_col * LANES:(src_col + 1) * LANES].astype(F32)
                dup = jnp.where(keep, x, pltpu.roll(x, hd, 1))
                dst_ref[j, 0:tk, :] = jnp.where(lo_lane, dup, 0.0).astype(BF16)
                dst_ref[j, tk:2 * tk, :] = jnp.where(lo_lane, 0.0, dup).astype(BF16)
                return carry
            lax.fori_loop(0, n_kt, body, 0)

        def build_v(src_ref, dst_ref):
            zero = jnp.zeros((hd, tk), BF16)
            for j in range(n_kt):
                x = src_ref[0, :, j * tk:(j + 1) * tk]
                dst_ref[j, 0:hd, 0:tk] = x
                dst_ref[j, 0:hd, tk:2 * tk] = zero
                dst_ref[j, hd:2 * hd, 0:tk] = zero
                dst_ref[j, hd:2 * hd, tk:2 * tk] = x

        build_k(0, ks_ref)
        build_k(1, kw_ref)
        build_v(vst_ref, vs_ref)
        build_v(vwt_ref, vw_ref)

    t0 = qi * tq
    key_i = lax.broadcasted_iota(jnp.int32, (tk, tq), 0)
    qry_t = t0 + lax.broadcasted_iota(jnp.int32, (tk, tq), 1)
    slab_lo = lax.broadcasted_iota(jnp.int32, (LANES, tq), 0) < hd
    q_pairs = [qt_ref[0, p * LANES:(p + 1) * LANES, :] for p in range(n_pairs)]

    n_cmp_pad = kc_ref.shape[2] // 2
    blk_i = lax.broadcasted_iota(jnp.int32, (n_cmp_pad, tq), 0)
    blk_t = t0 + lax.broadcasted_iota(jnp.int32, (n_cmp_pad, tq), 1)
    cmp_ok = (blk_i * CMP_STRIDE + (CMP_BLOCK - 1)) <= blk_t
    kc = kc_ref[0, 0]
    vct = vct_ref[0, 0]
    p_sum = jnp.zeros((n_cmp_pad, tq), F32)
    o_cmp = []
    for p in range(n_pairs):
        s = _dot(kc, q_pairs[p])
        probs = []
        for h in range(2):
            sh = jnp.where(cmp_ok, s[h * n_cmp_pad:(h + 1) * n_cmp_pad], NEG_INF)
            mh = jnp.max(sh, axis=0, keepdims=True)
            eh = jnp.where(cmp_ok, jnp.exp(sh - mh), 0.0)
            den = jnp.sum(eh, axis=0, keepdims=True)
            ph = eh / jnp.where(den > 0.0, den, 1.0)
            p_sum = p_sum + ph
            probs.append(ph.astype(BF16))
        o_cmp.append(_dot(vct, jnp.concatenate(probs, axis=0)))

    n_sel = t_len // SLC_BLOCK
    hi, mid, lo = _split3(p_sum)
    ovt = ovt_ref[...]
    p_sel = ((_dot(ovt, hi) + _dot(ovt, mid)) + _dot(ovt, lo))[0:n_sel]
    sel_i = lax.broadcasted_iota(jnp.int32, (n_sel, tq), 0)
    cur = (t0 + lax.broadcasted_iota(jnp.int32, (n_sel, tq), 1)) // SLC_BLOCK
    forced = (sel_i == 0) | (sel_i == cur) | (sel_i == cur - 1)
    score = jnp.where(forced, FORCE_SCORE, p_sel)
    score = jnp.where(sel_i <= cur, score, -jnp.inf)
    rank = jnp.zeros((n_sel, tq), jnp.int32)
    for i in range(n_sel):
        ci = score[i:i + 1, :]
        ahead = (ci > score) | ((ci == score) & (sel_i > i))
        rank = rank + jnp.where(ahead, 1, 0)
    chosen = jnp.where(rank < min(SLC_TOPK, n_sel), 1.0, 0.0)
    chosen = jnp.concatenate([chosen, jnp.zeros((LANES - n_sel, tq), F32)], axis=0).astype(BF16)

    def flash(k_ref, v_ref, j_lo, j_hi, allowed_fn):
        m_ref[...] = jnp.full_like(m_ref, -jnp.inf)
        l_ref[...] = jnp.zeros_like(l_ref)
        acc_ref[...] = jnp.zeros_like(acc_ref)

        def body(j, carry):
            allowed = allowed_fn(j)
            kt = k_ref[j]
            vt = v_ref[j]
            for p in range(n_pairs):
                s = _dot(kt, q_pairs[p])
                probs = []
                alphas = []
                for h in range(2):
                    hh = 2 * p + h
                    sh = jnp.where(allowed, s[h * tk:(h + 1) * tk], NEG_INF)
                    m_prev = m_ref[hh]
                    m_new = jnp.maximum(m_prev, jnp.max(sh, axis=0, keepdims=True))
                    alpha = jnp.exp(m_prev - m_new)
                    ph = jnp.exp(sh - m_new)
                    l_ref[hh] = alpha * l_ref[hh] + jnp.sum(ph, axis=0, keepdims=True)
                    m_ref[hh] = m_new
                    probs.append(ph.astype(BF16))
                    alphas.append(alpha)
                a_pair = jnp.where(slab_lo, alphas[0], alphas[1])
                acc_ref[p] = acc_ref[p] * a_pair + _dot(vt, jnp.concatenate(probs, axis=0))
            return carry

        lax.fori_loop(j_lo, j_hi, body, 0)
        outs = []
        for p in range(n_pairs):
            inv = jnp.where(slab_lo, 1.0 / l_ref[2 * p], 1.0 / l_ref[2 * p + 1])
            outs.append(acc_ref[p] * inv)
        return outs

    def slc_allowed(j):
        picked = _dot(expt_ref[j], chosen) > 0.5
        return picked & ((j * tk + key_i) <= qry_t)

    def win_allowed(j):
        rel = qry_t - (j * tk + key_i)
        return (rel >= 0) & (rel < WINDOW)

    last = (t0 + tq - 1) // tk
    o_slc = flash(ks_ref, vs_ref, 0, last + 1, slc_allowed)
    first_w = jnp.maximum(t0 - (WINDOW - 1), 0) // tk
    o_win = flash(kw_ref, vw_ref, first_w, last + 1, win_allowed)

    gates = gt_ref[0]
    gain = gain_ref[...]
    for p in range(n_pairs):
        o = jnp.zeros((LANES, tq), F32)
        for c, branch in enumerate((o_cmp[p], o_slc[p], o_win[p])):
            r = c * HPG + 2 * p
            o = o + jnp.where(slab_lo, gates[r:r + 1, :], gates[r + 1:r + 2, :]) * branch
        sq = o * o
        ms_a = jnp.sum(sq[0:hd], axis=0, keepdims=True)
        ms_b = jnp.sum(sq[hd:2 * hd], axis=0, keepdims=True)
        ms = jnp.where(slab_lo, ms_a, ms_b) * (1.0 / hd)
        o = o * lax.rsqrt(ms + EPS)
        o_ref[0, :, p * LANES:(p + 1) * LANES] = (o.T * gain[:, p * LANES:(p + 1) * LANES]).astype(o_ref.dtype)


def _nsa_call(qt, ksw, vt, kc, vct, gt, gain, ovt, expt):
    b, _, t = qt.shape
    tq, tk = ATT_TQ, ATT_TK
    n_kt = t // tk
    gw = HPG * NSA_HEAD_DIM
    hd = NSA_HEAD_DIM
    k_scratch = pltpu.VMEM((n_kt, 2 * tk, LANES), BF16)
    v_scratch = pltpu.VMEM((n_kt, LANES, 2 * tk), BF16)
    return pl.pallas_call(
        _nsa_body,
        grid=(b, NSA_KV_HEADS, t // tq),
        in_specs=[
            pl.BlockSpec((1, gw, tq), lambda bi, gi, qi: (bi, gi, qi)),
            pl.BlockSpec((1, t, 2 * KV_WIDTH), lambda bi, gi, qi: (bi, 0, 0)),
            pl.BlockSpec((1, hd, t), lambda bi, gi, qi: (bi, gi, 0)),
            pl.BlockSpec((1, hd, t), lambda bi, gi, qi: (bi, NSA_KV_HEADS + gi, 0)),
            pl.BlockSpec((1, 1) + kc.shape[2:], lambda bi, gi, qi: (bi, gi, 0, 0)),
            pl.BlockSpec((1, 1) + vct.shape[2:], lambda bi, gi, qi: (bi, gi, 0, 0)),
            pl.BlockSpec((1, LANES, tq), lambda bi, gi, qi: (bi, gi, qi)),
            pl.BlockSpec((1, gw), lambda bi, gi, qi: (0, gi)),
            pl.BlockSpec(ovt.shape, lambda bi, gi, qi: (0, 0)),
            pl.BlockSpec(expt.shape, lambda bi, gi, qi: (0, 0, 0)),
        ],
        out_specs=pl.BlockSpec((1, tq, gw), lambda bi, gi, qi: (bi, qi, gi)),
        out_shape=jax.ShapeDtypeStruct((b, t, NSA_WIDTH), BF16),
        scratch_shapes=[k_scratch, k_scratch, v_scratch, v_scratch,
                        pltpu.VMEM((HPG, 1, tq), F32), pltpu.VMEM((HPG, 1, tq), F32),
                        pltpu.VMEM((HPG // 2, LANES, tq), F32)],
        compiler_params=pltpu.CompilerParams(dimension_semantics=("arbitrary", "arbitrary", "arbitrary"),
                                             vmem_limit_bytes=VMEM_LIMIT),
        name="nsa_attention",
    )(qt, ksw, vt, vt, kc, vct, gt, gain, ovt, expt)


def _ffn_body(x_ref, oh_ref, on_ref, woh_ref, won_ref, g2_ref, wg_ref, wu_ref, wd_ref, cw_ref, gf_ref,
              out_ref, halo_ref, *, tiles_per_seq):
    tm = x_ref.shape[0]
    x1 = x_ref[...] + _dot(oh_ref[...], woh_ref[...]) + _dot(on_ref[...], won_ref[...])
    hb = _rms(x1, g2_ref[...]).astype(BF16)
    row = lax.broadcasted_iota(jnp.int32, (tm, FFN_TC), 0)

    @pl.when((pl.program_id(0) % tiles_per_seq) == 0)
    def _sequence_start():
        halo_ref[...] = jnp.zeros_like(halo_ref)

    def chunk(c, acc):
        gate = _dot(hb, wg_ref[c])
        halo = halo_ref[c]
        halo_ref[c] = gate[tm - 8:tm, :]
        prev1 = jnp.where(row == 0, halo[7:8, :], pltpu.roll(gate, 1, 0))
        prev2 = jnp.where(row == 0, halo[6:7, :], jnp.where(row == 1, halo[7:8, :], pltpu.roll(gate, 2, 0)))
        cw = cw_ref[c]
        y = cw[0:1, :] * prev2 + cw[1:2, :] * prev1 + cw[2:3, :] * gate + cw[3:4, :]
        act = (jax.nn.silu(y) * _dot(hb, wu_ref[c])).astype(BF16)
        return acc + _dot(act, wd_ref[c])

    acc = lax.fori_loop(0, FFN_NC, chunk, jnp.zeros((tm, D_MODEL), F32))
    out_ref[...] = _rms(x1 + acc, gf_ref[...])


def _ffn_call(x2, oh, on, woh, won, g2, wg3, wu3, wd3, cw3, gf, tiles_per_seq):
    n = x2.shape[0]
    tm = FFN_TM
    row = lambda w: pl.BlockSpec((tm, w), lambda i: (i, 0))
    full = lambda a: pl.BlockSpec(a.shape, lambda i: (0,) * a.ndim, pipeline_mode=pl.Buffered(1))
    return pl.pallas_call(
        functools.partial(_ffn_body, tiles_per_seq=tiles_per_seq),
        grid=(n // tm,),
        in_specs=[row(D_MODEL), row(HG_WIDTH), row(NSA_WIDTH), full(woh), full(won), full(g2),
                  full(wg3), full(wu3), full(wd3), full(cw3), full(gf)],
        out_specs=row(D_MODEL),
        out_shape=jax.ShapeDtypeStruct((n, D_MODEL), F32),
        scratch_shapes=[pltpu.VMEM((FFN_NC, 8, FFN_TC), F32)],
        compiler_params=pltpu.CompilerParams(dimension_semantics=("arbitrary",),
                                             vmem_limit_bytes=VMEM_LIMIT),
        name="outproj_convffn",
    )(x2, oh, on, woh, won, g2, wg3, wu3, wd3, cw3, gf)


def _rope_tables(positions):
    inv_freq = ROPE_THETA ** (-jnp.arange(ROPE_HALF, dtype=F32) * 2.0 / ROPE_DIM)
    ang = positions.astype(F32)[..., None] * inv_freq
    cos = jnp.cos(ang)
    sin = jnp.sin(ang)
    rest = NSA_HEAD_DIM - ROPE_DIM
    ones = jnp.ones(ang.shape[:-1] + (rest,), F32)
    zeros = jnp.zeros(ang.shape[:-1] + (rest,), F32)
    z8 = jnp.zeros_like(sin)
    head = lambda parts: jnp.tile(jnp.concatenate(parts, axis=-1), (1, 1, LANES // NSA_HEAD_DIM))
    cos_t = head([cos, cos, ones])
    sin_hi_t = head([z8, sin, zeros])
    sin_lo_t = head([-sin, z8, zeros])
    return cos_t, sin_hi_t, sin_lo_t


def _layer(x, positions, ln1, w_in, lb, hg_gain, pe_k, pe_v, k_w1, k_w2, v_w1, v_w2, nsa_gain, w_o, ln2,
           w_gate, w_up, conv_w, conv_b, w_down, final_gain):
    b, t, d = x.shape
    n = b * t
    assert d == D_MODEL and t % FFN_TM == 0 and t % PROJ_TM == 0 and t % ATT_TQ == 0 and t % HG_CHUNK == 0
    n_grp = t // CMP_STRIDE
    assert n_grp == LANES, "compressed-block axis is laid out on exactly one lane tile"
    n_sel = t // SLC_BLOCK
    assert n_sel % 8 == 0 and n_sel <= LANES
    x2 = x.reshape(n, d)

    splits = np.cumsum([0, 4 * HG_WIDTH, NSA_WIDTH] + [KV_WIDTH] * 6 + [N_GATES])
    seg = lambda i: w_in[:, splits[i]:splits[i + 1]]
    wh = seg(0).astype(BF16)
    wk = jnp.concatenate([seg(2), seg(3), seg(4), seg(6)], axis=1).astype(BF16)
    wgate = seg(8).reshape(d, 3, NSA_KV_HEADS, HPG).transpose(0, 2, 1, 3).reshape(d, NSA_KV_HEADS, 3 * HPG)
    wgate = jnp.pad(wgate, ((0, 0), (0, 0), (0, LANES - 3 * HPG))).reshape(d, NSA_KV_HEADS * LANES)
    wt = jnp.concatenate([seg(1), seg(5), seg(7), wgate], axis=1).T.astype(BF16)
    tabs = _rope_tables(positions)
    tabs_t = tuple(a.transpose(0, 2, 1) for a in tabs)
    tabs = tuple(a.reshape(n, LANES) for a in tabs)

    hg, kvc, ksw, qt, vt, gt = _inproj_call(x2, ln1.reshape(1, d), wh, wk, wt, tabs, tabs_t, t // PROJ_TM)

    mst, lvl = _hgrn_tables()
    o_hg = _hgrn_call(hg.reshape(b, t, 4 * HG_WIDTH), lb.reshape(1, HG_WIDTH).astype(F32),
                      hg_gain.reshape(1, HG_WIDTH), mst, lvl)

    grp_w = CMP_STRIDE * NSA_HEAD_DIM
    to_groups = lambda a: (a.reshape(b, n_grp, CMP_STRIDE, NSA_KV_HEADS, NSA_HEAD_DIM)
                           .transpose(0, 3, 1, 2, 4).reshape(b, NSA_KV_HEADS, n_grp, grp_w))
    kvc3 = kvc.reshape(b, t, 2 * KV_WIDTH)
    xk = to_groups(kvc3[:, :, 0:KV_WIDTH])
    xv = to_groups(kvc3[:, :, KV_WIDTH:2 * KV_WIDTH])
    zeros_w2 = jnp.zeros((CMP_HIDDEN, NSA_HEAD_DIM), F32)
    place = lambda w2: jnp.stack([jnp.concatenate([w2, zeros_w2], 1), jnp.concatenate([zeros_w2, w2], 1)])
    kc, vct = _cmp_call(xk, xv, pe_k.reshape(2, grp_w), pe_v.reshape(2, grp_w),
                        k_w1.reshape(2, grp_w, CMP_HIDDEN).astype(BF16), v_w1.reshape(2, grp_w, CMP_HIDDEN).astype(BF16),
                        place(k_w2).astype(BF16), place(v_w2).transpose(0, 2, 1).astype(BF16))

    cmp_start = np.arange(n_grp) * CMP_STRIDE
    cmp_end = cmp_start + CMP_BLOCK - 1
    sel_start = np.arange(LANES) * SLC_BLOCK
    overlap = ((cmp_start[:, None] <= sel_start[None, :] + SLC_BLOCK - 1) & (cmp_end[:, None] >= sel_start[None, :])
               & (np.arange(LANES)[None, :] < n_sel) & (np.arange(n_grp)[:, None] < n_grp - 1))
    ovt = jnp.asarray(overlap.T.astype(np.float32), BF16)
    blk_of_key = np.arange(t) // SLC_BLOCK
    expt = (blk_of_key[:, None] == np.arange(LANES)[None, :]).astype(np.float32)
    expt = jnp.asarray(expt.reshape(t // ATT_TK, ATT_TK, LANES), BF16)
    o_nsa = _nsa_call(qt, ksw.reshape(b, t, 2 * KV_WIDTH), vt, kc, vct, gt, nsa_gain.reshape(1, NSA_WIDTH), ovt, expt)

    chunks = lambda w: w.reshape(d, FFN_NC, FFN_TC).transpose(1, 0, 2).astype(BF16)
    cw3 = jnp.concatenate([conv_w, conv_b[None, :], jnp.zeros((4, D_FF), F32)], axis=0)
    cw3 = cw3.reshape(8, FFN_NC, FFN_TC).transpose(1, 0, 2)
    out = _ffn_call(x2, o_hg.reshape(n, HG_WIDTH), o_nsa.reshape(n, NSA_WIDTH),
                    w_o[:HG_WIDTH].astype(BF16), w_o[HG_WIDTH:].astype(BF16), ln2.reshape(1, d),
                    chunks(w_gate), chunks(w_up), w_down.reshape(FFN_NC, FFN_TC, d).astype(BF16), cw3,
                    final_gain.reshape(1, d), t // FFN_TM)
    return out.reshape(b, t, d)


def kernel(x, positions, ln1_gain, w_in, hgrn_lb_param, hgrn_out_gain, cmp_pe_k, cmp_pe_v, cmp_k_w1, cmp_k_w2,
           cmp_v_w1, cmp_v_w2, nsa_out_gain, w_o, ln2_gain, ffn_w_gate, ffn_w_up, ffn_conv_w, ffn_conv_b,
           ffn_w_down, final_gain):
    depth = ln1_gain.shape[0]
    assert depth == 1, "the fused final norm assumes a single layer"
    lower_bounds = jnp.cumsum(jax.nn.softmax(hgrn_lb_param.astype(F32), axis=0), axis=0)
    l = 0
    return _layer(x, positions, ln1_gain[l], w_in[l], lower_bounds[l], hgrn_out_gain[l], cmp_pe_k[l], cmp_pe_v[l],
                  cmp_k_w1[l], cmp_k_w2[l], cmp_v_w1[l], cmp_v_w2[l], nsa_out_gain[l], w_o[l], ln2_gain[l],
                  ffn_w_gate[l], ffn_w_up[l], ffn_conv_w[l], ffn_conv_b[l], ffn_w_down[l], final_gain)
```

```python
import functools

import jax
import jax.numpy as jnp
import numpy as np
from jax import lax
from jax.experimental import pallas as pl
from jax.experimental.pallas import tpu as pltpu

F32 = jnp.float32
BF16 = jnp.bfloat16

D_MODEL = 1024
HG_HEADS = 4
HG_DK = 128
HG_DV = 128
HG_WIDTH = HG_HEADS * HG_DV
NSA_HEADS = 8
NSA_KV_HEADS = 2
NSA_HEAD_DIM = 64
HPG = NSA_HEADS // NSA_KV_HEADS
NSA_WIDTH = NSA_HEADS * NSA_HEAD_DIM
KV_WIDTH = NSA_KV_HEADS * NSA_HEAD_DIM
CMP_BLOCK = 32
CMP_STRIDE = 16
CMP_HIDDEN = 256
SLC_BLOCK = 64
SLC_TOPK = 16
WINDOW = 512
ROPE_THETA = 500000.0
ROPE_DIM = NSA_HEAD_DIM // 4
ROPE_HALF = ROPE_DIM // 2
D_FF = 2816
EPS = 1e-6
NEG_INF = -1e30
FORCE_SCORE = 1e4
N_GATES = 3 * NSA_HEADS
LOG2_E = 1.4426950408889634

LANES = 128
VMEM_LIMIT = 56 * 1024 * 1024

PROJ_TM = 512
HG_CHUNK = 128
HG_LEVELS = (16, 32, 64)
HG_DIAG = 16
HG_TT = 512
ATT_TQ = 256
ATT_TK = 256
FFN_TM = 512
FFN_TC = 256
FFN_NC = D_FF // FFN_TC


def _dot(a, b):
    return jnp.dot(a, b, preferred_element_type=F32)


def _dot_nt(a, b):
    return lax.dot_general(a, b, (((1,), (1,)), ((), ())), preferred_element_type=F32)


def _dot_tn(a, b):
    return lax.dot_general(a, b, (((0,), (0,)), ((), ())), preferred_element_type=F32)


def _split3(x):
    hi = x.astype(BF16)
    r = x - hi.astype(F32)
    mid = r.astype(BF16)
    lo = (r - mid.astype(F32)).astype(BF16)
    return hi, mid, lo


def _rms(x, gain):
    return x * lax.rsqrt(jnp.mean(x * x, axis=-1, keepdims=True) + EPS) * gain


def _inproj_body(x_ref, g_ref, wh_ref, wk_ref, wt_ref, c_ref, s1_ref, s2_ref, ct_ref, s1t_ref, s2t_ref,
                 hg_ref, kvc_ref, ksw_ref, qt_ref, vt_ref, gt_ref):
    hb = _rms(x_ref[...], g_ref[...]).astype(BF16)
    hg_ref[...] = _dot(hb, wh_ref[...])

    def rope(v, axis, cos, sin_hi, sin_lo):
        return (v * cos + pltpu.roll(v, ROPE_HALF, axis) * sin_hi
                + pltpu.roll(v, LANES - ROPE_HALF, axis) * sin_lo)

    tab = (c_ref[...], s1_ref[...], s2_ref[...])
    kn = _dot(hb, wk_ref[...])
    kvc_ref[:, 0:LANES] = rope(kn[:, 0:LANES], 1, *tab)
    kvc_ref[:, LANES:2 * LANES] = kn[:, LANES:2 * LANES]
    ksw_ref[:, 0:LANES] = rope(kn[:, 2 * LANES:3 * LANES], 1, *tab).astype(BF16)
    ksw_ref[:, LANES:2 * LANES] = rope(kn[:, 3 * LANES:4 * LANES], 1, *tab).astype(BF16)

    rt = _dot_nt(wt_ref[...], hb)
    tab_t = (ct_ref[0], s1t_ref[0], s2t_ref[0])
    scale = NSA_HEAD_DIM ** -0.5 * LOG2_E
    for j in range(NSA_WIDTH // LANES):
        sl = slice(j * LANES, (j + 1) * LANES)
        qt_ref[0, sl, :] = (rope(rt[sl], 0, *tab_t) * scale).astype(BF16)
    vt_ref[0] = rt[NSA_WIDTH:NSA_WIDTH + 2 * KV_WIDTH].astype(BF16)
    gt_ref[0] = jax.nn.sigmoid(rt[NSA_WIDTH + 2 * KV_WIDTH:])


def _inproj_call(x2, gain, wh, wk, wt, tabs, tabs_t, tiles_per_seq):
    n = x2.shape[0]
    tm = PROJ_TM
    t = tiles_per_seq * tm
    b = n // t
    row = lambda w: pl.BlockSpec((tm, w), lambda i: (i, 0))
    col = lambda h: pl.BlockSpec((1, h, tm), lambda i: (i // tiles_per_seq, 0, i % tiles_per_seq))
    full = lambda a: pl.BlockSpec(a.shape, lambda i: (0, 0))
    gate_rows = NSA_KV_HEADS * LANES
    return pl.pallas_call(
        _inproj_body,
        grid=(n // tm,),
        in_specs=[row(D_MODEL), full(gain), full(wh), full(wk), full(wt),
                  row(LANES), row(LANES), row(LANES), col(LANES), col(LANES), col(LANES)],
        out_specs=[row(4 * HG_WIDTH), row(2 * KV_WIDTH), row(2 * KV_WIDTH),
                   col(NSA_WIDTH), col(2 * KV_WIDTH), col(gate_rows)],
        out_shape=[jax.ShapeDtypeStruct((n, 4 * HG_WIDTH), F32),
                   jax.ShapeDtypeStruct((n, 2 * KV_WIDTH), F32),
                   jax.ShapeDtypeStruct((n, 2 * KV_WIDTH), BF16),
                   jax.ShapeDtypeStruct((b, NSA_WIDTH, t), BF16),
                   jax.ShapeDtypeStruct((b, 2 * KV_WIDTH, t), BF16),
                   jax.ShapeDtypeStruct((b, gate_rows, t), F32)],
        compiler_params=pltpu.CompilerParams(dimension_semantics=("arbitrary",),
                                             vmem_limit_bytes=VMEM_LIMIT),
        name="inproj",
    )(x2, gain, wh, wk, wt, *tabs, *tabs_t)


def _hgrn_tables():
    L = HG_CHUNK
    t = np.arange(L)[:, None]
    u = np.arange(L)[None, :]
    mats = [((t // HG_DIAG) == (u // HG_DIAG)) & (u <= t)]
    level = np.where(mats[0], 1, 0)
    for li, s in enumerate(HG_LEVELS):
        same = (t // (2 * s)) == (u // (2 * s))
        mid = (t // (2 * s)) * (2 * s) + s
        right = (t % (2 * s)) >= s
        m = np.where(right, same & (u >= mid) & (u <= t), same & (u > t) & (u < mid))
        mats.append(m)
        level = np.where(same & right & ((u % (2 * s)) < s), li + 2, level)
    mats.append(u <= t)
    stack = np.concatenate([m.astype(np.float32) for m in mats], axis=0)
    return jnp.asarray(stack, BF16), jnp.asarray(level, jnp.int32)


def _hgrn_body(q_ref, f_ref, i_ref, g_ref, lb_ref, gain_ref, mst_ref, lvl_ref, o_ref, st_ref):
    L = HG_CHUNK
    n_chunks = q_ref.shape[1] // L

    @pl.when(pl.program_id(1) == 0)
    def _sequence_start():
        st_ref[...] = jnp.zeros_like(st_ref)

    def chunk(c, carry):
        rows = pl.ds(pl.multiple_of(c * L, L), L)
        heads = range(HG_HEADS)
        cols = [slice(h * HG_DK, (h + 1) * HG_DK) for h in heads]
        mst = mst_ref[...]
        lvl = lvl_ref[...]
        n_lv = len(HG_LEVELS)
        q = [q_ref[0, rows, cols[h]] for h in heads]
        vb = [i_ref[0, rows, cols[h]].astype(BF16) for h in heads]
        f = [lb_ref[:, cols[h]] + (1.0 - lb_ref[:, cols[h]]) * jax.nn.sigmoid(f_ref[0, rows, cols[h]]) for h in heads]
        k = [1.0 - f[h] for h in heads]
        parts = [_split3(jnp.log(f[h])) for h in heads]
        e = [(_dot(mst, parts[h][0]) + _dot(mst, parts[h][1])) + _dot(mst, parts[h][2]) for h in heads]
        e_full = [e[h][(n_lv + 1) * L:(n_lv + 2) * L] for h in heads]
        b_last = [e_full[h][L - 1:L, :] for h in heads]
        wq = [[jnp.exp(e[h][l * L:(l + 1) * L]) for l in range(n_lv + 1)] for h in heads]
        wk = [[jnp.exp(-e[h][0:L])] + wq[h][1:] for h in heads]
        prod = [[_dot_nt((q[h] * wq[h][l]).astype(BF16), (k[h] * wk[h][l]).astype(BF16)) for l in range(n_lv + 1)]
                for h in heads]
        st = [st_ref[h] for h in heads]
        inter = [_dot_nt((q[h] * jnp.exp(e_full[h])).astype(BF16), st[h].astype(BF16)) for h in heads]
        k_dec = [(k[h] * jnp.exp(b_last[h] - e_full[h])).astype(BF16) for h in heads]
        upd = [_dot_tn(vb[h], k_dec[h]) for h in heads]
        for h in heads:
            st_ref[h] = st[h] * jnp.exp(b_last[h]) + upd[h]
        a = []
        for h in heads:
            ah = jnp.where(lvl == 1, prod[h][0], 0.0)
            for l in range(1, n_lv + 1):
                ah = jnp.where(lvl == l + 1, prod[h][l], ah)
            a.append(ah.astype(BF16))
        o = [_dot(a[h], vb[h]) + inter[h] for h in heads]
        for h in heads:
            oh = o[h] * lax.rsqrt(jnp.mean(o[h] * o[h], axis=-1, keepdims=True) + EPS) * gain_ref[:, cols[h]]
            o_ref[0, rows, cols[h]] = (oh * jax.nn.silu(g_ref[0, rows, cols[h]])).astype(o_ref.dtype)
        return carry

    lax.fori_loop(0, n_chunks, chunk, 0)


def _hgrn_call(hg, lb, gain, mst, lvl):
    b, t, _ = hg.shape
    tt = HG_TT
    col = lambda k: pl.BlockSpec((1, tt, HG_WIDTH), lambda bi, ti: (bi, ti, k))
    full = lambda a: pl.BlockSpec(a.shape, lambda bi, ti: (0, 0))
    return pl.pallas_call(
        _hgrn_body,
        grid=(b, t // tt),
        in_specs=[col(0), col(1), col(2), col(3), full(lb), full(gain), full(mst), full(lvl)],
        out_specs=pl.BlockSpec((1, tt, HG_WIDTH), lambda bi, ti: (bi, ti, 0)),
        out_shape=jax.ShapeDtypeStruct((b, t, HG_WIDTH), BF16),
        scratch_shapes=[pltpu.VMEM((HG_HEADS, HG_DV, HG_DK), F32)],
        compiler_params=pltpu.CompilerParams(dimension_semantics=("arbitrary", "arbitrary"),
                                             vmem_limit_bytes=VMEM_LIMIT),
        name="hgrn2",
    )(hg, hg, hg, hg, lb, gain, mst, lvl)


def _cmp_body(xk_ref, xv_ref, pek_ref, pev_ref, w1k_ref, w1v_ref, w2k_ref, w2v_ref, kc_ref, vc_ref):
    nb = xk_ref.shape[2]

    def hidden(x, pe_ref, w1_ref):
        u = _dot((x + pe_ref[0:1, :]).astype(BF16), w1_ref[0])
        v = _dot((x + pe_ref[1:2, :]).astype(BF16), w1_ref[1])
        return jax.nn.silu(u + pltpu.roll(v, nb - 1, 0)).astype(BF16)

    hk = hidden(xk_ref[0, 0], pek_ref, w1k_ref)
    kc_ref[0, 0, 0:nb, :] = _dot(hk, w2k_ref[0]).astype(kc_ref.dtype)
    kc_ref[0, 0, nb:2 * nb, :] = _dot(hk, w2k_ref[1]).astype(kc_ref.dtype)
    hv = hidden(xv_ref[0, 0], pev_ref, w1v_ref)
    vc_ref[0, 0, :, 0:nb] = _dot_nt(w2v_ref[0], hv).astype(vc_ref.dtype)
    vc_ref[0, 0, :, nb:2 * nb] = _dot_nt(w2v_ref[1], hv).astype(vc_ref.dtype)


def _cmp_call(xk, xv, pek, pev, w1k, w1v, w2k, w2v):
    b, g, nb, w = xk.shape
    blk = pl.BlockSpec((1, 1, nb, w), lambda bi, gi: (bi, gi, 0, 0))
    full = lambda a: pl.BlockSpec(a.shape, lambda bi, gi: (0,) * a.ndim)
    out = lambda r, c: pl.BlockSpec((1, 1, r, c), lambda bi, gi: (bi, gi, 0, 0))
    return pl.pallas_call(
        _cmp_body,
        grid=(b, g),
        in_specs=[blk, blk, full(pek), full(pev), full(w1k), full(w1v), full(w2k), full(w2v)],
        out_specs=[out(2 * nb, LANES), out(LANES, 2 * nb)],
        out_shape=[jax.ShapeDtypeStruct((b, g, 2 * nb, LANES), BF16),
                   jax.ShapeDtypeStruct((b, g, LANES, 2 * nb), BF16)],
        compiler_params=pltpu.CompilerParams(dimension_semantics=("arbitrary", "arbitrary"),
                                             vmem_limit_bytes=VMEM_LIMIT),
        name="nsa_compress",
    )(xk, xv, pek, pev, w1k, w1v, w2k, w2v)


def _nsa_body(qt_ref, ksw_ref, vst_ref, vwt_ref, kc_ref, vct_ref, gt_ref, gain_ref, ovt_ref, o_ref,
              ks_ref, kw_ref, vs_ref, vw_ref, m_ref, l_ref, acc_ref, s_ref, p_ref, a_ref, ch_ref):
    g = pl.program_id(1)
    qi = pl.program_id(2)
    tq = ATT_TQ
    tk = ATT_TK
    t_len = ksw_ref.shape[1]
    n_kt = t_len // tk
    n_pairs = HPG // 2
    hd = NSA_HEAD_DIM

    @pl.when(qi == 0)
    def _build_kv():
        lane = lax.broadcasted_iota(jnp.int32, (tk, LANES), 1)
        lo_lane = lane < hd
        keep = (lane // hd) == g

        def build_k(src_col, dst_ref):
            def body(j, carry):
                rows = pl.ds(pl.multiple_of(j * tk, tk), tk)
                x = ksw_ref[0, rows, src_col * LANES:(src_col + 1) * LANES].astype(F32)
                dup = jnp.where(keep, x, pltpu.roll(x, hd, 1))
                dst_ref[j, 0:tk, :] = jnp.where(lo_lane, dup, 0.0).astype(BF16)
                dst_ref[j, tk:2 * tk, :] = jnp.where(lo_lane, 0.0, dup).astype(BF16)
                return carry
            lax.fori_loop(0, n_kt, body, 0)

        def build_v(src_ref, dst_ref):
            zero = jnp.zeros((hd, tk), BF16)
            for j in range(n_kt):
                x = src_ref[0, :, j * tk:(j + 1) * tk]
                dst_ref[j, 0:hd, 0:tk] = x
                dst_ref[j, 0:hd, tk:2 * tk] = zero
                dst_ref[j, hd:2 * hd, 0:tk] = zero
                dst_ref[j, hd:2 * hd, tk:2 * tk] = x

        build_k(0, ks_ref)
        build_k(1, kw_ref)
        build_v(vst_ref, vs_ref)
        build_v(vwt_ref, vw_ref)

    t0 = qi * tq
    key_i = lax.broadcasted_iota(jnp.int32, (tk, tq), 0)
    qry_t = t0 + lax.broadcasted_iota(jnp.int32, (tk, tq), 1)
    slab_lo = lax.broadcasted_iota(jnp.int32, (LANES, tq), 0) < hd
    q_pairs = [qt_ref[0, p * LANES:(p + 1) * LANES, :] for p in range(n_pairs)]

    last = (t0 + tq - 1) // tk
    first_w = jnp.maximum(t0 - (WINDOW - 1), 0) // tk

    def scores(k_ref, j):
        kt = k_ref[j]
        return [_dot(kt, q_pairs[p]) for p in range(n_pairs)]

    n_cmp_pad = kc_ref.shape[2] // 2
    blk_i = lax.broadcasted_iota(jnp.int32, (n_cmp_pad, tq), 0)
    blk_t = t0 + lax.broadcasted_iota(jnp.int32, (n_cmp_pad, tq), 1)
    cmp_ok = (blk_i * CMP_STRIDE + (CMP_BLOCK - 1)) <= blk_t
    kc = kc_ref[0, 0]
    vct = vct_ref[0, 0]
    s_cmp = [_dot(kc, q_pairs[p]) for p in range(n_pairs)]
    for br, (k_ref, j0) in enumerate(((ks_ref, 0), (kw_ref, first_w))):
        s_first = scores(k_ref, j0)
        for p in range(n_pairs):
            s_ref[br, p] = s_first[p]
    p_sum = jnp.zeros((n_cmp_pad, tq), F32)
    p_cmp = []
    for p in range(n_pairs):
        probs = []
        for h in range(2):
            sh = jnp.where(cmp_ok, s_cmp[p][h * n_cmp_pad:(h + 1) * n_cmp_pad], NEG_INF)
            mh = jnp.max(sh, axis=0, keepdims=True)
            eh = jnp.where(cmp_ok, jnp.exp2(sh - mh), 0.0)
            den = jnp.sum(eh, axis=0, keepdims=True)
            ph = eh / jnp.where(den > 0.0, den, 1.0)
            p_sum = p_sum + ph
            probs.append(ph.astype(BF16))
        p_cmp.append(jnp.concatenate(probs, axis=0))
    o_cmp = [_dot(vct, p_cmp[p]) for p in range(n_pairs)]

    n_sel = t_len // SLC_BLOCK
    hi, mid, lo = _split3(p_sum)
    ovt = ovt_ref[...]
    p_sel = ((_dot(ovt, hi) + _dot(ovt, mid)) + _dot(ovt, lo))[0:n_sel]
    sel_i = lax.broadcasted_iota(jnp.int32, (n_sel, tq), 0)
    cur = (t0 + lax.broadcasted_iota(jnp.int32, (n_sel, tq), 1)) // SLC_BLOCK
    forced = (sel_i == 0) | (sel_i == cur) | (sel_i == cur - 1)
    score = jnp.where(forced, FORCE_SCORE, p_sel)
    score = jnp.where(sel_i <= cur, score, -jnp.inf)
    rank = jnp.zeros((n_sel, tq), jnp.int32)
    for i in range(n_sel):
        ci = score[i:i + 1, :]
        ahead = (ci > score) | ((ci == score) & (sel_i > i))
        rank = rank + jnp.where(ahead, 1, 0)
    chosen = jnp.where(rank < min(SLC_TOPK, n_sel), 1.0, 0.0)
    for i in range(n_sel):
        ch_ref[i] = chosen[i:i + 1, :]

    def flash(br, k_ref, v_ref, j_lo, j_hi, allowed_fn):
        m_ref[...] = jnp.full_like(m_ref, -jnp.inf)
        l_ref[...] = jnp.zeros_like(l_ref)
        acc_ref[...] = jnp.zeros_like(acc_ref)
        p_ref[...] = jnp.zeros_like(p_ref)
        a_ref[...] = jnp.ones_like(a_ref)

        def body(j, carry):
            s_cur = [s_ref[br, p] for p in range(n_pairs)]
            a_prev = [a_ref[p] for p in range(n_pairs)]
            vt_prev = v_ref[jnp.maximum(j - 1, j_lo)]
            pv_prev = [_dot(vt_prev, p_ref[p].astype(BF16)) for p in range(n_pairs)]
            s_next = scores(k_ref, jnp.minimum(j + 1, j_hi - 1))
            allowed = allowed_fn(j)
            for p in range(n_pairs):
                alphas = []
                for h in range(2):
                    hh = 2 * p + h
                    sh = jnp.where(allowed, s_cur[p][h * tk:(h + 1) * tk], NEG_INF)
                    m_prev = m_ref[hh]
                    m_new = jnp.maximum(m_prev, jnp.max(sh, axis=0, keepdims=True))
                    alpha = jnp.exp2(m_prev - m_new)
                    ph = jnp.exp2(sh - m_new)
                    l_ref[hh] = alpha * l_ref[hh] + jnp.sum(ph, axis=0, keepdims=True)
                    m_ref[hh] = m_new
                    p_ref[p, h * tk:(h + 1) * tk, :] = ph
                    alphas.append(alpha)
                a_ref[p] = jnp.where(slab_lo, alphas[0], alphas[1])
            for p in range(n_pairs):
                s_ref[br, p] = s_next[p]
                acc_ref[p] = acc_ref[p] * a_prev[p] + pv_prev[p]
            return carry

        lax.fori_loop(j_lo, j_hi, body, 0)
        vt_last = v_ref[j_hi - 1]
        outs = []
        for p in range(n_pairs):
            inv = jnp.where(slab_lo, 1.0 / l_ref[2 * p], 1.0 / l_ref[2 * p + 1])
            outs.append((acc_ref[p] * a_ref[p] + _dot(vt_last, p_ref[p].astype(BF16))) * inv)
        return outs

    def slc_allowed(j):
        per_tile = tk // SLC_BLOCK
        picked = jnp.concatenate([jnp.broadcast_to(ch_ref[j * per_tile + i], (SLC_BLOCK, tq)) for i in range(per_tile)],
                                 axis=0) > 0.5
        return picked & ((j * tk + key_i) <= qry_t)

    def win_allowed(j):
        rel = qry_t - (j * tk + key_i)
        return (rel >= 0) & (rel < WINDOW)

    o_slc = flash(0, ks_ref, vs_ref, 0, last + 1, slc_allowed)
    o_win = flash(1, kw_ref, vw_ref, first_w, last + 1, win_allowed)

    gates = gt_ref[0]
    gain = gain_ref[...]
    for p in range(n_pairs):
        o = jnp.zeros((LANES, tq), F32)
        for c, branch in enumerate((o_cmp[p], o_slc[p], o_win[p])):
            r = c * HPG + 2 * p
            o = o + jnp.where(slab_lo, gates[r:r + 1, :], gates[r + 1:r + 2, :]) * branch
        sq = o * o
        ms_a = jnp.sum(sq[0:hd], axis=0, keepdims=True)
        ms_b = jnp.sum(sq[hd:2 * hd], axis=0, keepdims=True)
        ms = jnp.where(slab_lo, ms_a, ms_b) * (1.0 / hd)
        o = o * lax.rsqrt(ms + EPS)
        o_ref[0, :, p * LANES:(p + 1) * LANES] = (o.T * gain[:, p * LANES:(p + 1) * LANES]).astype(o_ref.dtype)


def _nsa_call(qt, ksw, vt, kc, vct, gt, gain, ovt):
    b, _, t = qt.shape
    tq, tk = ATT_TQ, ATT_TK
    n_kt = t // tk
    gw = HPG * NSA_HEAD_DIM
    hd = NSA_HEAD_DIM
    k_scratch = pltpu.VMEM((n_kt, 2 * tk, LANES), BF16)
    v_scratch = pltpu.VMEM((n_kt, LANES, 2 * tk), BF16)
    return pl.pallas_call(
        _nsa_body,
        grid=(b, NSA_KV_HEADS, t // tq),
        in_specs=[
            pl.BlockSpec((1, gw, tq), lambda bi, gi, qi: (bi, gi, qi)),
            pl.BlockSpec((1, t, 2 * KV_WIDTH), lambda bi, gi, qi: (bi, 0, 0)),
            pl.BlockSpec((1, hd, t), lambda bi, gi, qi: (bi, gi, 0)),
            pl.BlockSpec((1, hd, t), lambda bi, gi, qi: (bi, NSA_KV_HEADS + gi, 0)),
            pl.BlockSpec((1, 1) + kc.shape[2:], lambda bi, gi, qi: (bi, gi, 0, 0)),
            pl.BlockSpec((1, 1) + vct.shape[2:], lambda bi, gi, qi: (bi, gi, 0, 0)),
            pl.BlockSpec((1, LANES, tq), lambda bi, gi, qi: (bi, gi, qi)),
            pl.BlockSpec((1, gw), lambda bi, gi, qi: (0, gi)),
            pl.BlockSpec(ovt.shape, lambda bi, gi, qi: (0, 0)),
        ],
        out_specs=pl.BlockSpec((1, tq, gw), lambda bi, gi, qi: (bi, qi, gi)),
        out_shape=jax.ShapeDtypeStruct((b, t, NSA_WIDTH), BF16),
        scratch_shapes=[k_scratch, k_scratch, v_scratch, v_scratch,
                        pltpu.VMEM((HPG, 1, tq), F32), pltpu.VMEM((HPG, 1, tq), F32),
                        pltpu.VMEM((HPG // 2, LANES, tq), F32), pltpu.VMEM((2, HPG // 2, 2 * tk, tq), F32),
                        pltpu.VMEM((HPG // 2, 2 * tk, tq), F32), pltpu.VMEM((HPG // 2, LANES, tq), F32),
                        pltpu.VMEM((t // SLC_BLOCK, 1, tq), F32)],
        compiler_params=pltpu.CompilerParams(dimension_semantics=("arbitrary", "arbitrary", "arbitrary"),
                                             vmem_limit_bytes=VMEM_LIMIT),
        name="nsa_attention",
    )(qt, ksw, vt, vt, kc, vct, gt, gain, ovt)


def _ffn_body(x_ref, oh_ref, on_ref, woh_ref, won_ref, g2_ref, wg_ref, wu_ref, wd_ref, cw_ref, gf_ref,
              out_ref, halo_ref, act_ref, *, tiles_per_seq):
    tm = x_ref.shape[0]
    x1 = x_ref[...] + _dot(oh_ref[...], woh_ref[...]) + _dot(on_ref[...], won_ref[...])
    hb = _rms(x1, g2_ref[...]).astype(BF16)
    row = lax.broadcasted_iota(jnp.int32, (tm, FFN_TC), 0)

    @pl.when((pl.program_id(0) % tiles_per_seq) == 0)
    def _sequence_start():
        halo_ref[...] = jnp.zeros_like(halo_ref)

    def activation(c, gate, up):
        halo = halo_ref[c]
        halo_ref[c] = gate[tm - 8:tm, :]
        prev1 = jnp.where(row == 0, halo[7:8, :], pltpu.roll(gate, 1, 0))
        prev2 = jnp.where(row == 0, halo[6:7, :], jnp.where(row == 1, halo[7:8, :], pltpu.roll(gate, 2, 0)))
        cw = cw_ref[c]
        y = cw[0:1, :] * prev2 + cw[1:2, :] * prev1 + cw[2:3, :] * gate + cw[3:4, :]
        return (jax.nn.silu(y) * up).astype(BF16)

    gate_up = (_dot(hb, wg_ref[0]), _dot(hb, wu_ref[0]))
    for c in range(FFN_NC):
        cur = gate_up
        if c + 1 < FFN_NC:
            gate_up = (_dot(hb, wg_ref[c + 1]), _dot(hb, wu_ref[c + 1]))
        act_ref[:, c * FFN_TC:(c + 1) * FFN_TC] = activation(c, *cur)
    acc = _dot(act_ref[...], wd_ref[...])
    out_ref[...] = _rms(x1 + acc, gf_ref[...])


def _ffn_call(x2, oh, on, woh, won, g2, wg3, wu3, wd, cw3, gf, tiles_per_seq):
    n = x2.shape[0]
    tm = FFN_TM
    row = lambda w: pl.BlockSpec((tm, w), lambda i: (i, 0))
    full = lambda a: pl.BlockSpec(a.shape, lambda i: (0,) * a.ndim, pipeline_mode=pl.Buffered(1))
    return pl.pallas_call(
        functools.partial(_ffn_body, tiles_per_seq=tiles_per_seq),
        grid=(n // tm,),
        in_specs=[row(D_MODEL), row(HG_WIDTH), row(NSA_WIDTH), full(woh), full(won), full(g2),
                  full(wg3), full(wu3), full(wd), full(cw3), full(gf)],
        out_specs=row(D_MODEL),
        out_shape=jax.ShapeDtypeStruct((n, D_MODEL), F32),
        scratch_shapes=[pltpu.VMEM((FFN_NC, 8, FFN_TC), F32), pltpu.VMEM((tm, D_FF), BF16)],
        compiler_params=pltpu.CompilerParams(dimension_semantics=("arbitrary",),
                                             vmem_limit_bytes=VMEM_LIMIT),
        name="outproj_convffn",
    )(x2, oh, on, woh, won, g2, wg3, wu3, wd, cw3, gf)


def _rope_tables(positions):
    inv_freq = ROPE_THETA ** (-jnp.arange(ROPE_HALF, dtype=F32) * 2.0 / ROPE_DIM)
    ang = positions.astype(F32)[..., None] * inv_freq
    cos = jnp.cos(ang)
    sin = jnp.sin(ang)
    rest = NSA_HEAD_DIM - ROPE_DIM
    ones = jnp.ones(ang.shape[:-1] + (rest,), F32)
    zeros = jnp.zeros(ang.shape[:-1] + (rest,), F32)
    z8 = jnp.zeros_like(sin)
    head = lambda parts: jnp.tile(jnp.concatenate(parts, axis=-1), (1, 1, LANES // NSA_HEAD_DIM))
    cos_t = head([cos, cos, ones])
    sin_hi_t = head([z8, sin, zeros])
    sin_lo_t = head([-sin, z8, zeros])
    return cos_t, sin_hi_t, sin_lo_t


def _layer(x, positions, ln1, w_in, lb, hg_gain, pe_k, pe_v, k_w1, k_w2, v_w1, v_w2, nsa_gain, w_o, ln2,
           w_gate, w_up, conv_w, conv_b, w_down, final_gain):
    b, t, d = x.shape
    n = b * t
    assert d == D_MODEL and t % FFN_TM == 0 and t % PROJ_TM == 0 and t % ATT_TQ == 0 and t % HG_TT == 0
    n_grp = t // CMP_STRIDE
    assert n_grp == LANES, "compressed-block axis is laid out on exactly one lane tile"
    n_sel = t // SLC_BLOCK
    assert n_sel % 8 == 0 and n_sel <= LANES and ATT_TK % SLC_BLOCK == 0
    x2 = x.reshape(n, d)

    splits = np.cumsum([0, 4 * HG_WIDTH, NSA_WIDTH] + [KV_WIDTH] * 6 + [N_GATES])
    seg = lambda i: w_in[:, splits[i]:splits[i + 1]]
    wh = seg(0).astype(BF16)
    wk = jnp.concatenate([seg(2), seg(3), seg(4), seg(6)], axis=1).astype(BF16)
    wgate = seg(8).reshape(d, 3, NSA_KV_HEADS, HPG).transpose(0, 2, 1, 3).reshape(d, NSA_KV_HEADS, 3 * HPG)
    wgate = jnp.pad(wgate, ((0, 0), (0, 0), (0, LANES - 3 * HPG))).reshape(d, NSA_KV_HEADS * LANES)
    wt = jnp.concatenate([seg(1), seg(5), seg(7), wgate], axis=1).T.astype(BF16)
    tabs = _rope_tables(positions)
    tabs_t = tuple(a.transpose(0, 2, 1) for a in tabs)
    tabs = tuple(a.reshape(n, LANES) for a in tabs)

    hg, kvc, ksw, qt, vt, gt = _inproj_call(x2, ln1.reshape(1, d), wh, wk, wt, tabs, tabs_t, t // PROJ_TM)

    mst, lvl = _hgrn_tables()
    o_hg = _hgrn_call(hg.reshape(b, t, 4 * HG_WIDTH), lb.reshape(1, HG_WIDTH).astype(F32),
                      hg_gain.reshape(1, HG_WIDTH), mst, lvl)

    grp_w = CMP_STRIDE * NSA_HEAD_DIM
    to_groups = lambda a: (a.reshape(b, n_grp, CMP_STRIDE, NSA_KV_HEADS, NSA_HEAD_DIM)
                           .transpose(0, 3, 1, 2, 4).reshape(b, NSA_KV_HEADS, n_grp, grp_w))
    kvc3 = kvc.reshape(b, t, 2 * KV_WIDTH)
    xk = to_groups(kvc3[:, :, 0:KV_WIDTH])
    xv = to_groups(kvc3[:, :, KV_WIDTH:2 * KV_WIDTH])
    zeros_w2 = jnp.zeros((CMP_HIDDEN, NSA_HEAD_DIM), F32)
    place = lambda w2: jnp.stack([jnp.concatenate([w2, zeros_w2], 1), jnp.concatenate([zeros_w2, w2], 1)])
    kc, vct = _cmp_call(xk, xv, pe_k.reshape(2, grp_w), pe_v.reshape(2, grp_w),
                        k_w1.reshape(2, grp_w, CMP_HIDDEN).astype(BF16), v_w1.reshape(2, grp_w, CMP_HIDDEN).astype(BF16),
                        place(k_w2).astype(BF16), place(v_w2).transpose(0, 2, 1).astype(BF16))

    cmp_start = np.arange(n_grp) * CMP_STRIDE
    cmp_end = cmp_start + CMP_BLOCK - 1
    sel_start = np.arange(LANES) * SLC_BLOCK
    overlap = ((cmp_start[:, None] <= sel_start[None, :] + SLC_BLOCK - 1) & (cmp_end[:, None] >= sel_start[None, :])
               & (np.arange(LANES)[None, :] < n_sel) & (np.arange(n_grp)[:, None] < n_grp - 1))
    ovt = jnp.asarray(overlap.T.astype(np.float32), BF16)
    o_nsa = _nsa_call(qt, ksw.reshape(b, t, 2 * KV_WIDTH), vt, kc, vct, gt, nsa_gain.reshape(1, NSA_WIDTH), ovt)

    chunks = lambda w: w.reshape(d, FFN_NC, FFN_TC).transpose(1, 0, 2).astype(BF16)
    cw3 = jnp.concatenate([conv_w, conv_b[None, :], jnp.zeros((4, D_FF), F32)], axis=0)
    cw3 = cw3.reshape(8, FFN_NC, FFN_TC).transpose(1, 0, 2)
    out = _ffn_call(x2, o_hg.reshape(n, HG_WIDTH), o_nsa.reshape(n, NSA_WIDTH),
                    w_o[:HG_WIDTH].astype(BF16), w_o[HG_WIDTH:].astype(BF16), ln2.reshape(1, d),
                    chunks(w_gate), chunks(w_up), w_down.astype(BF16), cw3,
                    final_gain.reshape(1, d), t // FFN_TM)
    return out.reshape(b, t, d)


def kernel(x, positions, ln1_gain, w_in, hgrn_lb_param, hgrn_out_gain, cmp_pe_k, cmp_pe_v, cmp_k_w1, cmp_k_w2,
           cmp_v_w1, cmp_v_w2, nsa_out_gain, w_o, ln2_gain, ffn_w_gate, ffn_w_up, ffn_conv_w, ffn_conv_b,
           ffn_w_down, final_gain):
    depth = ln1_gain.shape[0]
    assert depth == 1, "the fused final norm assumes a single layer"
    lower_bounds = jnp.cumsum(jax.nn.softmax(hgrn_lb_param.astype(F32), axis=0), axis=0)
    l = 0
    return _layer(x, positions, ln1_gain[l], w_in[l], lower_bounds[l], hgrn_out_gain[l], cmp_pe_k[l], cmp_pe_v[l],
                  cmp_k_w1[l], cmp_k_w2[l], cmp_v_w1[l], cmp_v_w2[l], nsa_out_gain[l], w_o[l], ln2_gain[l],
                  ffn_w_gate[l], ffn_w_up[l], ffn_conv_w[l], ffn_conv_b[l], ffn_w_down[l], final_gain)
```

```python
import functools

import jax
import jax.numpy as jnp
import numpy as np
from jax import lax
from jax.experimental import pallas as pl
from jax.experimental.pallas import tpu as pltpu

F32 = jnp.float32
BF16 = jnp.bfloat16

D_MODEL = 1024
HG_HEADS = 4
HG_DK = 128
HG_DV = 128
HG_WIDTH = HG_HEADS * HG_DV
NSA_HEADS = 8
NSA_KV_HEADS = 2
NSA_HEAD_DIM = 64
HPG = NSA_HEADS // NSA_KV_HEADS
NSA_WIDTH = NSA_HEADS * NSA_HEAD_DIM
KV_WIDTH = NSA_KV_HEADS * NSA_HEAD_DIM
CMP_BLOCK = 32
CMP_STRIDE = 16
CMP_HIDDEN = 256
SLC_BLOCK = 64
SLC_TOPK = 16
WINDOW = 512
ROPE_THETA = 500000.0
ROPE_DIM = NSA_HEAD_DIM // 4
ROPE_HALF = ROPE_DIM // 2
D_FF = 2816
EPS = 1e-6
NEG_INF = -1e30
FORCE_SCORE = 1e4
N_GATES = 3 * NSA_HEADS
LOG2_E = 1.4426950408889634

LANES = 128
VMEM_LIMIT = 56 * 1024 * 1024

PROJ_TM = 512
HG_CHUNK = 128
HG_LEVELS = (16, 32, 64)
HG_DIAG = 16
HG_TT = 512
ATT_TQ = 256
ATT_TK = 256
FFN_TM = 512
FFN_TC = 256
FFN_NC = D_FF // FFN_TC


def _dot(a, b):
    return jnp.dot(a, b, preferred_element_type=F32)


def _dot_nt(a, b):
    return lax.dot_general(a, b, (((1,), (1,)), ((), ())), preferred_element_type=F32)


def _dot_tn(a, b):
    return lax.dot_general(a, b, (((0,), (0,)), ((), ())), preferred_element_type=F32)


def _split3(x):
    hi = x.astype(BF16)
    r = x - hi.astype(F32)
    mid = r.astype(BF16)
    lo = (r - mid.astype(F32)).astype(BF16)
    return hi, mid, lo


def _rms(x, gain):
    return x * lax.rsqrt(jnp.mean(x * x, axis=-1, keepdims=True) + EPS) * gain


def _inproj_body(x_ref, g_ref, wh_ref, wk_ref, wt_ref, cs_ref,
                 hg_ref, kcn_ref, vcn_ref, ksw_ref, qt_ref, vt_ref, gt_ref):
    hb = _rms(x_ref[...], g_ref[...]).astype(BF16)
    hg_ref[...] = _dot(hb, wh_ref[...])

    def rope(v, axis, cos, sin_hi, sin_lo):
        return (v * cos + pltpu.roll(v, ROPE_HALF, axis) * sin_hi
                + pltpu.roll(v, LANES - ROPE_HALF, axis) * sin_lo)

    cos = cs_ref[0, 0:ROPE_HALF, :]
    sin = cs_ref[0, ROPE_HALF:ROPE_DIM, :]
    tm = cos.shape[1]
    zero_h = jnp.zeros((ROPE_HALF, tm), F32)
    rest = NSA_HEAD_DIM - ROPE_DIM
    slab = lambda lo, hi, fill: jnp.concatenate([lo, hi, jnp.full((rest, tm), fill, F32)] * (LANES // NSA_HEAD_DIM), axis=0)
    tab_t = (slab(cos, cos, 1.0), slab(zero_h, sin, 0.0), slab(-sin, zero_h, 0.0))
    tab = tuple(a.T for a in tab_t)
    kn = _dot(hb, wk_ref[...])
    kcn_ref[...] = rope(kn[:, 0:LANES], 1, *tab)
    vcn_ref[...] = kn[:, LANES:2 * LANES]
    ksw_ref[:, 0:LANES] = rope(kn[:, 2 * LANES:3 * LANES], 1, *tab).astype(BF16)
    ksw_ref[:, LANES:2 * LANES] = rope(kn[:, 3 * LANES:4 * LANES], 1, *tab).astype(BF16)

    rt = _dot_nt(wt_ref[...], hb)
    scale = NSA_HEAD_DIM ** -0.5 * LOG2_E
    for j in range(NSA_WIDTH // LANES):
        sl = slice(j * LANES, (j + 1) * LANES)
        qt_ref[0, sl, :] = (rope(rt[sl], 0, *tab_t) * scale).astype(BF16)
    vt_ref[0] = rt[NSA_WIDTH:NSA_WIDTH + 2 * KV_WIDTH].astype(BF16)
    gt_ref[0] = jax.nn.sigmoid(rt[NSA_WIDTH + 2 * KV_WIDTH:])


def _inproj_call(x2, gain, wh, wk, wt, cs, tiles_per_seq):
    n = x2.shape[0]
    tm = PROJ_TM
    t = tiles_per_seq * tm
    b = n // t
    row = lambda w: pl.BlockSpec((tm, w), lambda i: (i, 0))
    col = lambda h: pl.BlockSpec((1, h, tm), lambda i: (i // tiles_per_seq, 0, i % tiles_per_seq))
    full = lambda a: pl.BlockSpec(a.shape, lambda i: (0, 0))
    gate_rows = NSA_KV_HEADS * LANES
    return pl.pallas_call(
        _inproj_body,
        grid=(n // tm,),
        in_specs=[row(D_MODEL), full(gain), full(wh), full(wk), full(wt),
                  col(ROPE_DIM)],
        out_specs=[row(4 * HG_WIDTH), row(KV_WIDTH), row(KV_WIDTH), row(2 * KV_WIDTH),
                   col(NSA_WIDTH), col(2 * KV_WIDTH), col(gate_rows)],
        out_shape=[jax.ShapeDtypeStruct((n, 4 * HG_WIDTH), F32),
                   jax.ShapeDtypeStruct((n, KV_WIDTH), F32),
                   jax.ShapeDtypeStruct((n, KV_WIDTH), F32),
                   jax.ShapeDtypeStruct((n, 2 * KV_WIDTH), BF16),
                   jax.ShapeDtypeStruct((b, NSA_WIDTH, t), BF16),
                   jax.ShapeDtypeStruct((b, 2 * KV_WIDTH, t), BF16),
                   jax.ShapeDtypeStruct((b, gate_rows, t), F32)],
        compiler_params=pltpu.CompilerParams(dimension_semantics=("arbitrary",),
                                             vmem_limit_bytes=VMEM_LIMIT),
        name="inproj",
    )(x2, gain, wh, wk, wt, cs)


def _hgrn_tables():
    L = HG_CHUNK
    t = np.arange(L)[:, None]
    u = np.arange(L)[None, :]
    mats = [((t // HG_DIAG) == (u // HG_DIAG)) & (u <= t)]
    level = np.where(mats[0], 1, 0)
    for li, s in enumerate(HG_LEVELS):
        same = (t // (2 * s)) == (u // (2 * s))
        mid = (t // (2 * s)) * (2 * s) + s
        right = (t % (2 * s)) >= s
        m = np.where(right, same & (u >= mid) & (u <= t), same & (u > t) & (u < mid))
        mats.append(m)
        level = np.where(same & right & ((u % (2 * s)) < s), li + 2, level)
    mats.append(u <= t)
    stack = np.concatenate([m.astype(np.float32) for m in mats], axis=0)
    return jnp.asarray(stack, BF16), jnp.asarray(level, jnp.int32)


def _hgrn_body(q_ref, f_ref, i_ref, g_ref, lb_ref, gain_ref, mst_ref, lvl_ref, o_ref, st_ref):
    L = HG_CHUNK
    n_chunks = q_ref.shape[1] // L

    @pl.when(pl.program_id(1) == 0)
    def _sequence_start():
        st_ref[...] = jnp.zeros_like(st_ref)

    def chunk(c, carry):
        rows = pl.ds(pl.multiple_of(c * L, L), L)
        heads = range(HG_HEADS)
        cols = [slice(h * HG_DK, (h + 1) * HG_DK) for h in heads]
        mst = mst_ref[...]
        lvl = lvl_ref[...]
        n_lv = len(HG_LEVELS)
        q = [q_ref[0, rows, cols[h]] for h in heads]
        vb = [i_ref[0, rows, cols[h]].astype(BF16) for h in heads]
        f = [lb_ref[:, cols[h]] + (1.0 - lb_ref[:, cols[h]]) * jax.nn.sigmoid(f_ref[0, rows, cols[h]]) for h in heads]
        k = [1.0 - f[h] for h in heads]
        parts = [_split3(jnp.log(f[h])) for h in heads]
        e = [(_dot(mst, parts[h][0]) + _dot(mst, parts[h][1])) + _dot(mst, parts[h][2]) for h in heads]
        e_full = [e[h][(n_lv + 1) * L:(n_lv + 2) * L] for h in heads]
        b_last = [e_full[h][L - 1:L, :] for h in heads]
        wq = [[jnp.exp(e[h][l * L:(l + 1) * L]) for l in range(n_lv + 1)] for h in heads]
        wk = [[jnp.exp(-e[h][0:L])] + wq[h][1:] for h in heads]
        prod = [[_dot_nt((q[h] * wq[h][l]).astype(BF16), (k[h] * wk[h][l]).astype(BF16)) for l in range(n_lv + 1)]
                for h in heads]
        st = [st_ref[h] for h in heads]
        inter = [_dot_nt((q[h] * jnp.exp(e_full[h])).astype(BF16), st[h].astype(BF16)) for h in heads]
        k_dec = [(k[h] * jnp.exp(b_last[h] - e_full[h])).astype(BF16) for h in heads]
        upd = [_dot_tn(vb[h], k_dec[h]) for h in heads]
        for h in heads:
            st_ref[h] = st[h] * jnp.exp(b_last[h]) + upd[h]
        a = []
        for h in heads:
            ah = jnp.where(lvl == 1, prod[h][0], 0.0)
            for l in range(1, n_lv + 1):
                ah = jnp.where(lvl == l + 1, prod[h][l], ah)
            a.append(ah.astype(BF16))
        o = [_dot(a[h], vb[h]) + inter[h] for h in heads]
        for h in heads:
            oh = o[h] * lax.rsqrt(jnp.mean(o[h] * o[h], axis=-1, keepdims=True) + EPS) * gain_ref[:, cols[h]]
            o_ref[0, rows, cols[h]] = (oh * jax.nn.silu(g_ref[0, rows, cols[h]])).astype(o_ref.dtype)
        return carry

    lax.fori_loop(0, n_chunks, chunk, 0)


def _hgrn_call(hg, lb, gain, mst, lvl):
    b, t, _ = hg.shape
    tt = HG_TT
    col = lambda k: pl.BlockSpec((1, tt, HG_WIDTH), lambda bi, ti: (bi, ti, k))
    full = lambda a: pl.BlockSpec(a.shape, lambda bi, ti: (0, 0))
    return pl.pallas_call(
        _hgrn_body,
        grid=(b, t // tt),
        in_specs=[col(0), col(1), col(2), col(3), full(lb), full(gain), full(mst), full(lvl)],
        out_specs=pl.BlockSpec((1, tt, HG_WIDTH), lambda bi, ti: (bi, ti, 0)),
        out_shape=jax.ShapeDtypeStruct((b, t, HG_WIDTH), BF16),
        scratch_shapes=[pltpu.VMEM((HG_HEADS, HG_DV, HG_DK), F32)],
        compiler_params=pltpu.CompilerParams(dimension_semantics=("arbitrary", "arbitrary"),
                                             vmem_limit_bytes=VMEM_LIMIT),
        name="hgrn2",
    )(hg, hg, hg, hg, lb, gain, mst, lvl)


def _cmp_body(kcn_ref, vcn_ref, pek_ref, pev_ref, w1k_ref, w1v_ref, w2k_ref, w2v_ref, kc_ref, vc_ref):
    nb = kcn_ref.shape[1] // CMP_STRIDE
    lane_grp = (lax.broadcasted_iota(jnp.int32, (nb, CMP_STRIDE * LANES), 1) // NSA_HEAD_DIM) % NSA_KV_HEADS

    def hidden(src_ref, pe_ref, w1_ref):
        x = jnp.concatenate([src_ref[0, pl.ds(l, nb, stride=CMP_STRIDE), :]
                             for l in range(CMP_STRIDE)], axis=1)
        halves = [x + pe_ref[i] for i in range(2)]
        out = []
        for g in range(NSA_KV_HEADS):
            u, v = (_dot(jnp.where(lane_grp == g, halves[i], 0.0).astype(BF16), w1_ref[i]) for i in range(2))
            out.append(jax.nn.silu(u + pltpu.roll(v, nb - 1, 0)).astype(BF16))
        return out

    hk = hidden(kcn_ref, pek_ref, w1k_ref)
    hv = hidden(vcn_ref, pev_ref, w1v_ref)
    for g in range(NSA_KV_HEADS):
        kc_ref[0, g, 0:nb, :] = _dot(hk[g], w2k_ref[0]).astype(kc_ref.dtype)
        kc_ref[0, g, nb:2 * nb, :] = _dot(hk[g], w2k_ref[1]).astype(kc_ref.dtype)
        vc_ref[0, g, :, 0:nb] = _dot_nt(w2v_ref[0], hv[g]).astype(vc_ref.dtype)
        vc_ref[0, g, :, nb:2 * nb] = _dot_nt(w2v_ref[1], hv[g]).astype(vc_ref.dtype)


def _cmp_call(kcn, vcn, pek, pev, w1k, w1v, w2k, w2v):
    b, t, w = kcn.shape
    nb = t // CMP_STRIDE
    full = lambda a: pl.BlockSpec(a.shape, lambda bi: (0,) * a.ndim)
    out = lambda r, c: pl.BlockSpec((1, NSA_KV_HEADS, r, c), lambda bi: (bi, 0, 0, 0))
    return pl.pallas_call(
        _cmp_body,
        grid=(b,),
        in_specs=[pl.BlockSpec((1, t, w), lambda bi: (bi, 0, 0)), pl.BlockSpec((1, t, w), lambda bi: (bi, 0, 0)),
                  full(pek), full(pev), full(w1k), full(w1v), full(w2k), full(w2v)],
        out_specs=[out(2 * nb, LANES), out(LANES, 2 * nb)],
        out_shape=[jax.ShapeDtypeStruct((b, NSA_KV_HEADS, 2 * nb, LANES), BF16),
                   jax.ShapeDtypeStruct((b, NSA_KV_HEADS, LANES, 2 * nb), BF16)],
        compiler_params=pltpu.CompilerParams(dimension_semantics=("arbitrary",),
                                             vmem_limit_bytes=VMEM_LIMIT),
        name="nsa_compress",
    )(kcn, vcn, pek, pev, w1k, w1v, w2k, w2v)


def _nsa_body(qt_ref, ksw_ref, vst_ref, vwt_ref, kc_ref, vct_ref, gt_ref, gain_ref, ovt_ref, o_ref,
              ks_ref, kw_ref, vs_ref, vw_ref, m_ref, l_ref, acc_ref, s_ref, p_ref, a_ref, ch_ref):
    g = pl.program_id(1)
    qi = pl.program_id(2)
    tq = ATT_TQ
    tk = ATT_TK
    t_len = ksw_ref.shape[1]
    n_kt = t_len // tk
    n_pairs = HPG // 2
    hd = NSA_HEAD_DIM

    @pl.when(qi == 0)
    def _build_kv():
        lane = lax.broadcasted_iota(jnp.int32, (tk, LANES), 1)
        lo_lane = lane < hd
        keep = (lane // hd) == g

        def build_k(src_col, dst_ref):
            def body(j, carry):
                rows = pl.ds(pl.multiple_of(j * tk, tk), tk)
                x = ksw_ref[0, rows, src_col * LANES:(src_col + 1) * LANES].astype(F32)
                dup = jnp.where(keep, x, pltpu.roll(x, hd, 1))
                dst_ref[j, 0:tk, :] = jnp.where(lo_lane, dup, 0.0).astype(BF16)
                dst_ref[j, tk:2 * tk, :] = jnp.where(lo_lane, 0.0, dup).astype(BF16)
                return carry
            lax.fori_loop(0, n_kt, body, 0)

        def build_v(src_ref, dst_ref):
            zero = jnp.zeros((hd, tk), BF16)
            for j in range(n_kt):
                x = src_ref[0, :, j * tk:(j + 1) * tk]
                dst_ref[j, 0:hd, 0:tk] = x
                dst_ref[j, 0:hd, tk:2 * tk] = zero
                dst_ref[j, hd:2 * hd, 0:tk] = zero
                dst_ref[j, hd:2 * hd, tk:2 * tk] = x

        build_k(0, ks_ref)
        build_k(1, kw_ref)
        build_v(vst_ref, vs_ref)
        build_v(vwt_ref, vw_ref)

    t0 = qi * tq
    key_i = lax.broadcasted_iota(jnp.int32, (tk, tq), 0)
    qry_t = t0 + lax.broadcasted_iota(jnp.int32, (tk, tq), 1)
    slab_lo = lax.broadcasted_iota(jnp.int32, (LANES, tq), 0) < hd
    q_pairs = [qt_ref[0, p * LANES:(p + 1) * LANES, :] for p in range(n_pairs)]

    last = (t0 + tq - 1) // tk
    first_w = jnp.maximum(t0 - (WINDOW - 1), 0) // tk

    def scores(k_ref, j):
        kt = k_ref[j]
        return [_dot(kt, q_pairs[p]) for p in range(n_pairs)]

    n_cmp_pad = kc_ref.shape[2] // 2
    blk_i = lax.broadcasted_iota(jnp.int32, (n_cmp_pad, tq), 0)
    blk_t = t0 + lax.broadcasted_iota(jnp.int32, (n_cmp_pad, tq), 1)
    cmp_ok = (blk_i * CMP_STRIDE + (CMP_BLOCK - 1)) <= blk_t
    kc = kc_ref[0, 0]
    vct = vct_ref[0, 0]
    s_cmp = [_dot(kc, q_pairs[p]) for p in range(n_pairs)]
    for br, (k_ref, j0) in enumerate(((ks_ref, 0), (kw_ref, first_w))):
        s_first = scores(k_ref, j0)
        for p in range(n_pairs):
            s_ref[br, p] = s_first[p]
    p_sum = jnp.zeros((n_cmp_pad, tq), F32)
    p_cmp = []
    for p in range(n_pairs):
        probs = []
        for h in range(2):
            sh = jnp.where(cmp_ok, s_cmp[p][h * n_cmp_pad:(h + 1) * n_cmp_pad], NEG_INF)
            mh = jnp.max(sh, axis=0, keepdims=True)
            eh = jnp.where(cmp_ok, jnp.exp2(sh - mh), 0.0)
            den = jnp.sum(eh, axis=0, keepdims=True)
            ph = eh / jnp.where(den > 0.0, den, 1.0)
            p_sum = p_sum + ph
            probs.append(ph.astype(BF16))
        p_cmp.append(jnp.concatenate(probs, axis=0))
    o_cmp = [_dot(vct, p_cmp[p]) for p in range(n_pairs)]

    n_sel = t_len // SLC_BLOCK
    hi, mid, lo = _split3(p_sum)
    ovt = ovt_ref[...]
    p_sel = ((_dot(ovt, hi) + _dot(ovt, mid)) + _dot(ovt, lo))[0:n_sel]
    sel_i = lax.broadcasted_iota(jnp.int32, (n_sel, tq), 0)
    cur = (t0 + lax.broadcasted_iota(jnp.int32, (n_sel, tq), 1)) // SLC_BLOCK
    forced = (sel_i == 0) | (sel_i == cur) | (sel_i == cur - 1)
    score = jnp.where(forced, FORCE_SCORE, p_sel)
    score = jnp.where(sel_i <= cur, score, -jnp.inf)
    rank = jnp.zeros((n_sel, tq), jnp.int32)
    for i in range(n_sel):
        ci = score[i:i + 1, :]
        ahead = (ci > score) | ((ci == score) & (sel_i > i))
        rank = rank + jnp.where(ahead, 1, 0)
    chosen = jnp.where(rank < min(SLC_TOPK, n_sel), 1.0, 0.0)
    for i in range(n_sel):
        ch_ref[i] = chosen[i:i + 1, :]

    def flash(br, k_ref, v_ref, j_lo, j_hi, allowed_fn):
        m_ref[...] = jnp.full_like(m_ref, -jnp.inf)
        l_ref[...] = jnp.zeros_like(l_ref)
        acc_ref[...] = jnp.zeros_like(acc_ref)
        p_ref[...] = jnp.zeros_like(p_ref)
        a_ref[...] = jnp.ones_like(a_ref)

        def body(j, carry):
            s_cur = [s_ref[br, p] for p in range(n_pairs)]
            a_prev = [a_ref[p] for p in range(n_pairs)]
            vt_prev = v_ref[jnp.maximum(j - 1, j_lo)]
            pv_prev = [_dot(vt_prev, p_ref[p].astype(BF16)) for p in range(n_pairs)]
            s_next = scores(k_ref, jnp.minimum(j + 1, j_hi - 1))
            allowed = allowed_fn(j)
            for p in range(n_pairs):
                alphas = []
                for h in range(2):
                    hh = 2 * p + h
                    sh = jnp.where(allowed, s_cur[p][h * tk:(h + 1) * tk], NEG_INF)
                    m_prev = m_ref[hh]
                    m_new = jnp.maximum(m_prev, jnp.max(sh, axis=0, keepdims=True))
                    alpha = jnp.exp2(m_prev - m_new)
                    ph = jnp.exp2(sh - m_new)
                    l_ref[hh] = alpha * l_ref[hh] + jnp.sum(ph, axis=0, keepdims=True)
                    m_ref[hh] = m_new
                    p_ref[p, h * tk:(h + 1) * tk, :] = ph
                    alphas.append(alpha)
                a_ref[p] = jnp.where(slab_lo, alphas[0], alphas[1])
            for p in range(n_pairs):
                s_ref[br, p] = s_next[p]
                acc_ref[p] = acc_ref[p] * a_prev[p] + pv_prev[p]
            return carry

        lax.fori_loop(j_lo, j_hi, body, 0)
        vt_last = v_ref[j_hi - 1]
        outs = []
        for p in range(n_pairs):
            inv = jnp.where(slab_lo, 1.0 / l_ref[2 * p], 1.0 / l_ref[2 * p + 1])
            outs.append((acc_ref[p] * a_ref[p] + _dot(vt_last, p_ref[p].astype(BF16))) * inv)
        return outs

    def slc_allowed(j):
        per_tile = tk // SLC_BLOCK
        picked = jnp.concatenate([jnp.broadcast_to(ch_ref[j * per_tile + i], (SLC_BLOCK, tq)) for i in range(per_tile)],
                                 axis=0) > 0.5
        return picked & ((j * tk + key_i) <= qry_t)

    def win_allowed(j):
        rel = qry_t - (j * tk + key_i)
        return (rel >= 0) & (rel < WINDOW)

    o_slc = flash(0, ks_ref, vs_ref, 0, last + 1, slc_allowed)
    o_win = flash(1, kw_ref, vw_ref, first_w, last + 1, win_allowed)

    gates = gt_ref[0]
    gain = gain_ref[...]
    for p in range(n_pairs):
        o = jnp.zeros((LANES, tq), F32)
        for c, branch in enumerate((o_cmp[p], o_slc[p], o_win[p])):
            r = c * HPG + 2 * p
            o = o + jnp.where(slab_lo, gates[r:r + 1, :], gates[r + 1:r + 2, :]) * branch
        sq = o * o
        ms_a = jnp.sum(sq[0:hd], axis=0, keepdims=True)
        ms_b = jnp.sum(sq[hd:2 * hd], axis=0, keepdims=True)
        ms = jnp.where(slab_lo, ms_a, ms_b) * (1.0 / hd)
        o = o * lax.rsqrt(ms + EPS)
        o_ref[0, :, p * LANES:(p + 1) * LANES] = (o.T * gain[:, p * LANES:(p + 1) * LANES]).astype(o_ref.dtype)


def _nsa_call(qt, ksw, vt, kc, vct, gt, gain, ovt):
    b, _, t = qt.shape
    tq, tk = ATT_TQ, ATT_TK
    n_kt = t // tk
    gw = HPG * NSA_HEAD_DIM
    hd = NSA_HEAD_DIM
    k_scratch = pltpu.VMEM((n_kt, 2 * tk, LANES), BF16)
    v_scratch = pltpu.VMEM((n_kt, LANES, 2 * tk), BF16)
    return pl.pallas_call(
        _nsa_body,
        grid=(b, NSA_KV_HEADS, t // tq),
        in_specs=[
            pl.BlockSpec((1, gw, tq), lambda bi, gi, qi: (bi, gi, qi)),
            pl.BlockSpec((1, t, 2 * KV_WIDTH), lambda bi, gi, qi: (bi, 0, 0)),
            pl.BlockSpec((1, hd, t), lambda bi, gi, qi: (bi, gi, 0)),
            pl.BlockSpec((1, hd, t), lambda bi, gi, qi: (bi, NSA_KV_HEADS + gi, 0)),
            pl.BlockSpec((1, 1) + kc.shape[2:], lambda bi, gi, qi: (bi, gi, 0, 0)),
            pl.BlockSpec((1, 1) + vct.shape[2:], lambda bi, gi, qi: (bi, gi, 0, 0)),
            pl.BlockSpec((1, LANES, tq), lambda bi, gi, qi: (bi, gi, qi)),
            pl.BlockSpec((1, gw), lambda bi, gi, qi: (0, gi)),
            pl.BlockSpec(ovt.shape, lambda bi, gi, qi: (0, 0)),
        ],
        out_specs=pl.BlockSpec((1, tq, gw), lambda bi, gi, qi: (bi, qi, gi)),
        out_shape=jax.ShapeDtypeStruct((b, t, NSA_WIDTH), BF16),
        scratch_shapes=[k_scratch, k_scratch, v_scratch, v_scratch,
                        pltpu.VMEM((HPG, 1, tq), F32), pltpu.VMEM((HPG, 1, tq), F32),
                        pltpu.VMEM((HPG // 2, LANES, tq), F32), pltpu.VMEM((2, HPG // 2, 2 * tk, tq), F32),
                        pltpu.VMEM((HPG // 2, 2 * tk, tq), F32), pltpu.VMEM((HPG // 2, LANES, tq), F32),
                        pltpu.VMEM((t // SLC_BLOCK, 1, tq), F32)],
        compiler_params=pltpu.CompilerParams(dimension_semantics=("arbitrary", "arbitrary", "arbitrary"),
                                             vmem_limit_bytes=VMEM_LIMIT),
        name="nsa_attention",
    )(qt, ksw, vt, vt, kc, vct, gt, gain, ovt)


def _ffn_body(x_ref, oh_ref, on_ref, woh_ref, won_ref, g2_ref, wg_ref, wu_ref, wd_ref, cw_ref, gf_ref,
              out_ref, halo_ref, act_ref, *, tiles_per_seq):
    tm = x_ref.shape[0]
    x1 = x_ref[...] + _dot(oh_ref[...], woh_ref[...]) + _dot(on_ref[...], won_ref[...])
    hb = _rms(x1, g2_ref[...]).astype(BF16)
    row = lax.broadcasted_iota(jnp.int32, (tm, FFN_TC), 0)

    @pl.when((pl.program_id(0) % tiles_per_seq) == 0)
    def _sequence_start():
        halo_ref[...] = jnp.zeros_like(halo_ref)

    def activation(c, gate, up):
        cols = slice(c * FFN_TC, (c + 1) * FFN_TC)
        halo = halo_ref[:, cols]
        halo_ref[:, cols] = gate[tm - 8:tm, :]
        prev1 = jnp.where(row == 0, halo[7:8, :], pltpu.roll(gate, 1, 0))
        prev2 = jnp.where(row == 0, halo[6:7, :], jnp.where(row == 1, halo[7:8, :], pltpu.roll(gate, 2, 0)))
        cw = cw_ref[:, cols]
        y = cw[0:1, :] * prev2 + cw[1:2, :] * prev1 + cw[2:3, :] * gate + cw[3:4, :]
        return (jax.nn.silu(y) * up).astype(BF16)

    chunk = lambda w_ref, c: _dot(hb, w_ref[:, c * FFN_TC:(c + 1) * FFN_TC])
    gate_up = (chunk(wg_ref, 0), chunk(wu_ref, 0))
    for c in range(FFN_NC):
        cur = gate_up
        if c + 1 < FFN_NC:
            gate_up = (chunk(wg_ref, c + 1), chunk(wu_ref, c + 1))
        act_ref[:, c * FFN_TC:(c + 1) * FFN_TC] = activation(c, *cur)
    acc = _dot(act_ref[...], wd_ref[...])
    out_ref[...] = _rms(x1 + acc, gf_ref[...])


def _ffn_call(x2, oh, on, woh, won, g2, wg, wu, wd, cw, gf, tiles_per_seq):
    n = x2.shape[0]
    tm = FFN_TM
    row = lambda w: pl.BlockSpec((tm, w), lambda i: (i, 0))
    full = lambda a: pl.BlockSpec(a.shape, lambda i: (0,) * a.ndim, pipeline_mode=pl.Buffered(1))
    return pl.pallas_call(
        functools.partial(_ffn_body, tiles_per_seq=tiles_per_seq),
        grid=(n // tm,),
        in_specs=[row(D_MODEL), row(HG_WIDTH), row(NSA_WIDTH), full(woh), full(won), full(g2),
                  full(wg), full(wu), full(wd), full(cw), full(gf)],
        out_specs=row(D_MODEL),
        out_shape=jax.ShapeDtypeStruct((n, D_MODEL), F32),
        scratch_shapes=[pltpu.VMEM((8, D_FF), F32), pltpu.VMEM((tm, D_FF), BF16)],
        compiler_params=pltpu.CompilerParams(dimension_semantics=("arbitrary",),
                                             vmem_limit_bytes=VMEM_LIMIT),
        name="outproj_convffn",
    )(x2, oh, on, woh, won, g2, wg, wu, wd, cw, gf)


def _rope_angles(positions):
    inv_freq = ROPE_THETA ** (-jnp.arange(ROPE_HALF, dtype=F32) * 2.0 / ROPE_DIM)
    ang = positions.astype(F32)[..., None] * inv_freq
    return jnp.concatenate([jnp.cos(ang), jnp.sin(ang)], axis=-1).transpose(0, 2, 1)


def _layer(x, positions, ln1, w_in, lb, hg_gain, pe_k, pe_v, k_w1, k_w2, v_w1, v_w2, nsa_gain, w_o, ln2,
           w_gate, w_up, conv_w, conv_b, w_down, final_gain):
    b, t, d = x.shape
    n = b * t
    assert d == D_MODEL and t % FFN_TM == 0 and t % PROJ_TM == 0 and t % ATT_TQ == 0 and t % HG_TT == 0
    n_grp = t // CMP_STRIDE
    assert n_grp == LANES, "compressed-block axis is laid out on exactly one lane tile"
    n_sel = t // SLC_BLOCK
    assert n_sel % 8 == 0 and n_sel <= LANES and ATT_TK % SLC_BLOCK == 0
    x2 = x.reshape(n, d)

    splits = np.cumsum([0, 4 * HG_WIDTH, NSA_WIDTH] + [KV_WIDTH] * 6 + [N_GATES])
    seg = lambda i: w_in[:, splits[i]:splits[i + 1]]
    wh = seg(0).astype(BF16)
    wk = jnp.concatenate([seg(2), seg(3), seg(4), seg(6)], axis=1).astype(BF16)
    wgate = seg(8).reshape(d, 3, NSA_KV_HEADS, HPG).transpose(0, 2, 1, 3).reshape(d, NSA_KV_HEADS, 3 * HPG)
    wgate = jnp.pad(wgate, ((0, 0), (0, 0), (0, LANES - 3 * HPG))).reshape(d, NSA_KV_HEADS * LANES)
    wt = jnp.concatenate([seg(1), seg(5), seg(7), wgate], axis=1).T.astype(BF16)
    cs = _rope_angles(positions)

    hg, kcn, vcn, ksw, qt, vt, gt = _inproj_call(x2, ln1.reshape(1, d), wh, wk, wt, cs, t // PROJ_TM)

    mst, lvl = _hgrn_tables()
    o_hg = _hgrn_call(hg.reshape(b, t, 4 * HG_WIDTH), lb.reshape(1, HG_WIDTH).astype(F32),
                      hg_gain.reshape(1, HG_WIDTH), mst, lvl)

    per_lane = lambda a: jnp.broadcast_to(a.reshape(2, CMP_STRIDE, 1, NSA_HEAD_DIM, -1),
                                          (2, CMP_STRIDE, NSA_KV_HEADS, NSA_HEAD_DIM, a.shape[-1]))
    w1_rows = lambda w1: per_lane(w1).reshape(2, CMP_STRIDE * LANES, CMP_HIDDEN).astype(BF16)
    pe_rows = lambda pe: per_lane(pe[..., None]).reshape(2, 1, CMP_STRIDE * LANES)
    zeros_w2 = jnp.zeros((CMP_HIDDEN, NSA_HEAD_DIM), F32)
    place = lambda w2: jnp.stack([jnp.concatenate([w2, zeros_w2], 1), jnp.concatenate([zeros_w2, w2], 1)])
    kc, vct = _cmp_call(kcn.reshape(b, t, KV_WIDTH), vcn.reshape(b, t, KV_WIDTH), pe_rows(pe_k), pe_rows(pe_v),
                        w1_rows(k_w1), w1_rows(v_w1),
                        place(k_w2).astype(BF16), place(v_w2).transpose(0, 2, 1).astype(BF16))

    cmp_start = np.arange(n_grp) * CMP_STRIDE
    cmp_end = cmp_start + CMP_BLOCK - 1
    sel_start = np.arange(LANES) * SLC_BLOCK
    overlap = ((cmp_start[:, None] <= sel_start[None, :] + SLC_BLOCK - 1) & (cmp_end[:, None] >= sel_start[None, :])
               & (np.arange(LANES)[None, :] < n_sel) & (np.arange(n_grp)[:, None] < n_grp - 1))
    ovt = jnp.asarray(overlap.T.astype(np.float32), BF16)
    o_nsa = _nsa_call(qt, ksw.reshape(b, t, 2 * KV_WIDTH), vt, kc, vct, gt, nsa_gain.reshape(1, NSA_WIDTH), ovt)

    cw = jnp.concatenate([conv_w, conv_b[None, :], jnp.zeros((4, D_FF), F32)], axis=0)
    out = _ffn_call(x2, o_hg.reshape(n, HG_WIDTH), o_nsa.reshape(n, NSA_WIDTH),
                    w_o[:HG_WIDTH].astype(BF16), w_o[HG_WIDTH:].astype(BF16), ln2.reshape(1, d),
                    w_gate.astype(BF16), w_up.astype(BF16), w_down.astype(BF16), cw,
                    final_gain.reshape(1, d), t // FFN_TM)
    return out.reshape(b, t, d)


def kernel(x, positions, ln1_gain, w_in, hgrn_lb_param, hgrn_out_gain, cmp_pe_k, cmp_pe_v, cmp_k_w1, cmp_k_w2,
           cmp_v_w1, cmp_v_w2, nsa_out_gain, w_o, ln2_gain, ffn_w_gate, ffn_w_up, ffn_conv_w, ffn_conv_b,
           ffn_w_down, final_gain):
    depth = ln1_gain.shape[0]
    assert depth == 1, "the fused final norm assumes a single layer"
    lower_bounds = jnp.cumsum(jax.nn.softmax(hgrn_lb_param.astype(F32), axis=0), axis=0)
    l = 0
    return _layer(x, positions, ln1_gain[l], w_in[l], lower_bounds[l], hgrn_out_gain[l], cmp_pe_k[l], cmp_pe_v[l],
                  cmp_k_w1[l], cmp_k_w2[l], cmp_v_w1[l], cmp_v_w2[l], nsa_out_gain[l], w_o[l], ln2_gain[l],
                  ffn_w_gate[l], ffn_w_up[l], ffn_conv_w[l], ffn_conv_b[l], ffn_w_down[l], final_gain)
```

```python
import functools

import jax
import jax.numpy as jnp
import numpy as np
from jax import lax
from jax.experimental import pallas as pl
from jax.experimental.pallas import tpu as pltpu

F32 = jnp.float32
BF16 = jnp.bfloat16

D_MODEL = 1024
HG_HEADS = 4
HG_DK = 128
HG_DV = 128
HG_WIDTH = HG_HEADS * HG_DV
NSA_HEADS = 8
NSA_KV_HEADS = 2
NSA_HEAD_DIM = 64
HPG = NSA_HEADS // NSA_KV_HEADS
NSA_WIDTH = NSA_HEADS * NSA_HEAD_DIM
KV_WIDTH = NSA_KV_HEADS * NSA_HEAD_DIM
CMP_BLOCK = 32
CMP_STRIDE = 16
CMP_HIDDEN = 256
SLC_BLOCK = 64
SLC_TOPK = 16
WINDOW = 512
ROPE_THETA = 500000.0
ROPE_DIM = NSA_HEAD_DIM // 4
ROPE_HALF = ROPE_DIM // 2
D_FF = 2816
EPS = 1e-6
NEG_INF = -1e30
FORCE_SCORE = 1e4
N_GATES = 3 * NSA_HEADS
LOG2_E = 1.4426950408889634

LANES = 128
VMEM_LIMIT = 56 * 1024 * 1024

PROJ_TM = 512
HG_CHUNK = 128
HG_LEVELS = (16, 32, 64)
HG_DIAG = 16
HG_TT = 512
ATT_TQ = 256
ATT_TK = 256
FFN_TM = 512
FFN_TC = 256
FFN_NC = D_FF // FFN_TC


def _dot(a, b):
    return jnp.dot(a, b, preferred_element_type=F32)


def _dot_nt(a, b):
    return lax.dot_general(a, b, (((1,), (1,)), ((), ())), preferred_element_type=F32)


def _dot_tn(a, b):
    return lax.dot_general(a, b, (((0,), (0,)), ((), ())), preferred_element_type=F32)


def _split3(x):
    hi = x.astype(BF16)
    r = x - hi.astype(F32)
    mid = r.astype(BF16)
    lo = (r - mid.astype(F32)).astype(BF16)
    return hi, mid, lo


def _rms(x, gain):
    return x * lax.rsqrt(jnp.mean(x * x, axis=-1, keepdims=True) + EPS) * gain


def _inproj_body(x_ref, g_ref, wh_ref, wk_ref, wt_ref, cs_ref,
                 hg_ref, kcn_ref, vcn_ref, ksw_ref, qt_ref, vt_ref, gt_ref):
    hb = _rms(x_ref[...], g_ref[...]).astype(BF16)
    hg_ref[...] = _dot(hb, wh_ref[...])

    def rope(v, axis, cos, sin_hi, sin_lo):
        return (v * cos + pltpu.roll(v, ROPE_HALF, axis) * sin_hi
                + pltpu.roll(v, LANES - ROPE_HALF, axis) * sin_lo)

    cos = cs_ref[0, 0:ROPE_HALF, :]
    sin = cs_ref[0, ROPE_HALF:ROPE_DIM, :]
    tm = cos.shape[1]
    zero_h = jnp.zeros((ROPE_HALF, tm), F32)
    rest = NSA_HEAD_DIM - ROPE_DIM
    slab = lambda lo, hi, fill: jnp.concatenate([lo, hi, jnp.full((rest, tm), fill, F32)] * (LANES // NSA_HEAD_DIM), axis=0)
    tab_t = (slab(cos, cos, 1.0), slab(zero_h, sin, 0.0), slab(-sin, zero_h, 0.0))
    tab = tuple(a.T for a in tab_t)
    kn = _dot(hb, wk_ref[...])
    kcn_ref[...] = rope(kn[:, 0:LANES], 1, *tab)
    vcn_ref[...] = kn[:, LANES:2 * LANES]
    ksw_ref[:, 0:LANES] = rope(kn[:, 2 * LANES:3 * LANES], 1, *tab).astype(BF16)
    ksw_ref[:, LANES:2 * LANES] = rope(kn[:, 3 * LANES:4 * LANES], 1, *tab).astype(BF16)

    rt = _dot_nt(wt_ref[...], hb)
    scale = NSA_HEAD_DIM ** -0.5 * LOG2_E
    for j in range(NSA_WIDTH // LANES):
        sl = slice(j * LANES, (j + 1) * LANES)
        qt_ref[0, sl, :] = (rope(rt[sl], 0, *tab_t) * scale).astype(BF16)
    vt_ref[0] = rt[NSA_WIDTH:NSA_WIDTH + 2 * KV_WIDTH].astype(BF16)
    gt_ref[0] = jax.nn.sigmoid(rt[NSA_WIDTH + 2 * KV_WIDTH:])


def _inproj_call(x2, gain, wh, wk, wt, cs, tiles_per_seq):
    n = x2.shape[0]
    tm = PROJ_TM
    t = tiles_per_seq * tm
    b = n // t
    row = lambda w: pl.BlockSpec((tm, w), lambda i: (i, 0))
    col = lambda h: pl.BlockSpec((1, h, tm), lambda i: (i // tiles_per_seq, 0, i % tiles_per_seq))
    full = lambda a: pl.BlockSpec(a.shape, lambda i: (0, 0))
    gate_rows = NSA_KV_HEADS * LANES
    return pl.pallas_call(
        _inproj_body,
        grid=(n // tm,),
        in_specs=[row(D_MODEL), full(gain), full(wh), full(wk), full(wt),
                  col(ROPE_DIM)],
        out_specs=[row(4 * HG_WIDTH), row(KV_WIDTH), row(KV_WIDTH), row(2 * KV_WIDTH),
                   col(NSA_WIDTH), col(2 * KV_WIDTH), col(gate_rows)],
        out_shape=[jax.ShapeDtypeStruct((n, 4 * HG_WIDTH), F32),
                   jax.ShapeDtypeStruct((n, KV_WIDTH), F32),
                   jax.ShapeDtypeStruct((n, KV_WIDTH), F32),
                   jax.ShapeDtypeStruct((n, 2 * KV_WIDTH), BF16),
                   jax.ShapeDtypeStruct((b, NSA_WIDTH, t), BF16),
                   jax.ShapeDtypeStruct((b, 2 * KV_WIDTH, t), BF16),
                   jax.ShapeDtypeStruct((b, gate_rows, t), F32)],
        compiler_params=pltpu.CompilerParams(dimension_semantics=("arbitrary",),
                                             vmem_limit_bytes=VMEM_LIMIT),
        name="inproj",
    )(x2, gain, wh, wk, wt, cs)


def _hgrn_tables():
    L = HG_CHUNK
    t = np.arange(L)[:, None]
    u = np.arange(L)[None, :]
    mats = [((t // HG_DIAG) == (u // HG_DIAG)) & (u <= t)]
    level = np.where(mats[0], 1, 0)
    for li, s in enumerate(HG_LEVELS):
        same = (t // (2 * s)) == (u // (2 * s))
        mid = (t // (2 * s)) * (2 * s) + s
        right = (t % (2 * s)) >= s
        m = np.where(right, same & (u >= mid) & (u <= t), same & (u > t) & (u < mid))
        mats.append(m)
        level = np.where(same & right & ((u % (2 * s)) < s), li + 2, level)
    mats.append(u <= t)
    stack = np.concatenate([m.astype(np.float32) for m in mats], axis=0)
    return jnp.asarray(stack, BF16), jnp.asarray(level, jnp.int32)


def _hgrn_body(q_ref, f_ref, i_ref, g_ref, lb_ref, gain_ref, mst_ref, lvl_ref, o_ref, st_ref):
    L = HG_CHUNK
    n_chunks = q_ref.shape[1] // L

    @pl.when(pl.program_id(1) == 0)
    def _sequence_start():
        st_ref[...] = jnp.zeros_like(st_ref)

    def chunk(c, carry):
        rows = pl.ds(pl.multiple_of(c * L, L), L)
        heads = range(HG_HEADS)
        cols = [slice(h * HG_DK, (h + 1) * HG_DK) for h in heads]
        mst = mst_ref[...]
        lvl = lvl_ref[...]
        n_lv = len(HG_LEVELS)
        q = [q_ref[0, rows, cols[h]] for h in heads]
        vb = [i_ref[0, rows, cols[h]].astype(BF16) for h in heads]
        f = [lb_ref[:, cols[h]] + (1.0 - lb_ref[:, cols[h]]) * jax.nn.sigmoid(f_ref[0, rows, cols[h]]) for h in heads]
        k = [1.0 - f[h] for h in heads]
        parts = [_split3(jnp.log(f[h])) for h in heads]
        e = [(_dot(mst, parts[h][0]) + _dot(mst, parts[h][1])) + _dot(mst, parts[h][2]) for h in heads]
        e_full = [e[h][(n_lv + 1) * L:(n_lv + 2) * L] for h in heads]
        b_last = [e_full[h][L - 1:L, :] for h in heads]
        wq = [[jnp.exp(e[h][l * L:(l + 1) * L]) for l in range(n_lv + 1)] for h in heads]
        wk = [[jnp.exp(-e[h][0:L])] + wq[h][1:] for h in heads]
        prod = [[_dot_nt((q[h] * wq[h][l]).astype(BF16), (k[h] * wk[h][l]).astype(BF16)) for l in range(n_lv + 1)]
                for h in heads]
        st = [st_ref[h] for h in heads]
        inter = [_dot_nt((q[h] * jnp.exp(e_full[h])).astype(BF16), st[h].astype(BF16)) for h in heads]
        k_dec = [(k[h] * jnp.exp(b_last[h] - e_full[h])).astype(BF16) for h in heads]
        upd = [_dot_tn(vb[h], k_dec[h]) for h in heads]
        for h in heads:
            st_ref[h] = st[h] * jnp.exp(b_last[h]) + upd[h]
        a = []
        for h in heads:
            ah = jnp.where(lvl == 1, prod[h][0], 0.0)
            for l in range(1, n_lv + 1):
                ah = jnp.where(lvl == l + 1, prod[h][l], ah)
            a.append(ah.astype(BF16))
        o = [_dot(a[h], vb[h]) + inter[h] for h in heads]
        for h in heads:
            oh = o[h] * lax.rsqrt(jnp.mean(o[h] * o[h], axis=-1, keepdims=True) + EPS) * gain_ref[:, cols[h]]
            o_ref[0, rows, cols[h]] = (oh * jax.nn.silu(g_ref[0, rows, cols[h]])).astype(o_ref.dtype)
        return carry

    lax.fori_loop(0, n_chunks, chunk, 0)


def _hgrn_call(hg, lb, gain, mst, lvl):
    b, t, _ = hg.shape
    tt = HG_TT
    col = lambda k: pl.BlockSpec((1, tt, HG_WIDTH), lambda bi, ti: (bi, ti, k))
    full = lambda a: pl.BlockSpec(a.shape, lambda bi, ti: (0, 0))
    return pl.pallas_call(
        _hgrn_body,
        grid=(b, t // tt),
        in_specs=[col(0), col(1), col(2), col(3), full(lb), full(gain), full(mst), full(lvl)],
        out_specs=pl.BlockSpec((1, tt, HG_WIDTH), lambda bi, ti: (bi, ti, 0)),
        out_shape=jax.ShapeDtypeStruct((b, t, HG_WIDTH), BF16),
        scratch_shapes=[pltpu.VMEM((HG_HEADS, HG_DV, HG_DK), F32)],
        compiler_params=pltpu.CompilerParams(dimension_semantics=("arbitrary", "arbitrary"),
                                             vmem_limit_bytes=VMEM_LIMIT),
        name="hgrn2",
    )(hg, hg, hg, hg, lb, gain, mst, lvl)


def _cmp_body(kcn_ref, vcn_ref, pek_ref, pev_ref, w1k_ref, w1v_ref, w2k_ref, w2v_ref, kc_ref, vc_ref):
    nb = kcn_ref.shape[1] // CMP_STRIDE
    lane_grp = (lax.broadcasted_iota(jnp.int32, (nb, CMP_STRIDE * LANES), 1) // NSA_HEAD_DIM) % NSA_KV_HEADS

    def hidden(src_ref, pe_ref, w1_ref):
        x = jnp.concatenate([src_ref[0, pl.ds(l, nb, stride=CMP_STRIDE), :]
                             for l in range(CMP_STRIDE)], axis=1)
        halves = [x + pe_ref[i] for i in range(2)]
        out = []
        for g in range(NSA_KV_HEADS):
            u, v = (_dot(jnp.where(lane_grp == g, halves[i], 0.0).astype(BF16), w1_ref[i]) for i in range(2))
            out.append(jax.nn.silu(u + pltpu.roll(v, nb - 1, 0)).astype(BF16))
        return out

    hk = hidden(kcn_ref, pek_ref, w1k_ref)
    hv = hidden(vcn_ref, pev_ref, w1v_ref)
    for g in range(NSA_KV_HEADS):
        kc_ref[0, g, 0:nb, :] = _dot(hk[g], w2k_ref[0]).astype(kc_ref.dtype)
        kc_ref[0, g, nb:2 * nb, :] = _dot(hk[g], w2k_ref[1]).astype(kc_ref.dtype)
        vc_ref[0, g, :, 0:nb] = _dot_nt(w2v_ref[0], hv[g]).astype(vc_ref.dtype)
        vc_ref[0, g, :, nb:2 * nb] = _dot_nt(w2v_ref[1], hv[g]).astype(vc_ref.dtype)


def _cmp_call(kcn, vcn, pek, pev, w1k, w1v, w2k, w2v):
    b, t, w = kcn.shape
    nb = t // CMP_STRIDE
    full = lambda a: pl.BlockSpec(a.shape, lambda bi: (0,) * a.ndim)
    out = lambda r, c: pl.BlockSpec((1, NSA_KV_HEADS, r, c), lambda bi: (bi, 0, 0, 0))
    return pl.pallas_call(
        _cmp_body,
        grid=(b,),
        in_specs=[pl.BlockSpec((1, t, w), lambda bi: (bi, 0, 0)), pl.BlockSpec((1, t, w), lambda bi: (bi, 0, 0)),
                  full(pek), full(pev), full(w1k), full(w1v), full(w2k), full(w2v)],
        out_specs=[out(2 * nb, LANES), out(LANES, 2 * nb)],
        out_shape=[jax.ShapeDtypeStruct((b, NSA_KV_HEADS, 2 * nb, LANES), BF16),
                   jax.ShapeDtypeStruct((b, NSA_KV_HEADS, LANES, 2 * nb), BF16)],
        compiler_params=pltpu.CompilerParams(dimension_semantics=("arbitrary",),
                                             vmem_limit_bytes=VMEM_LIMIT),
        name="nsa_compress",
    )(kcn, vcn, pek, pev, w1k, w1v, w2k, w2v)


def _nsa_body(qt_ref, ksw_ref, vst_ref, vwt_ref, kc_ref, vct_ref, gt_ref, gain_ref, ovt_ref, o_ref,
              ks_ref, kw_ref, vs_ref, vw_ref, m_ref, l_ref, acc_ref, s_ref, ch_ref):
    g = pl.program_id(1)
    qi = pl.program_id(2)
    tq = ATT_TQ
    tk = ATT_TK
    t_len = ksw_ref.shape[1]
    n_kt = t_len // tk
    n_pairs = HPG // 2
    hd = NSA_HEAD_DIM

    @pl.when(qi == 0)
    def _build_kv():
        lane = lax.broadcasted_iota(jnp.int32, (tk, LANES), 1)
        lo_lane = lane < hd
        keep = (lane // hd) == g

        def build_k(src_col, dst_ref):
            def body(j, carry):
                rows = pl.ds(pl.multiple_of(j * tk, tk), tk)
                x = ksw_ref[0, rows, src_col * LANES:(src_col + 1) * LANES].astype(F32)
                dup = jnp.where(keep, x, pltpu.roll(x, hd, 1))
                dst_ref[j, 0:tk, :] = jnp.where(lo_lane, dup, 0.0).astype(BF16)
                dst_ref[j, tk:2 * tk, :] = jnp.where(lo_lane, 0.0, dup).astype(BF16)
                return carry
            lax.fori_loop(0, n_kt, body, 0)

        def build_v(src_ref, dst_ref):
            zero = jnp.zeros((hd, tk), BF16)
            for j in range(n_kt):
                x = src_ref[0, :, j * tk:(j + 1) * tk]
                dst_ref[j, 0:hd, 0:tk] = x
                dst_ref[j, 0:hd, tk:2 * tk] = zero
                dst_ref[j, hd:2 * hd, 0:tk] = zero
                dst_ref[j, hd:2 * hd, tk:2 * tk] = x

        build_k(0, ks_ref)
        build_k(1, kw_ref)
        build_v(vst_ref, vs_ref)
        build_v(vwt_ref, vw_ref)

    t0 = qi * tq
    key_i = lax.broadcasted_iota(jnp.int32, (tk, tq), 0)
    qry_t = t0 + lax.broadcasted_iota(jnp.int32, (tk, tq), 1)
    slab_lo = lax.broadcasted_iota(jnp.int32, (LANES, tq), 0) < hd
    q_pairs = [qt_ref[0, p * LANES:(p + 1) * LANES, :] for p in range(n_pairs)]

    last = (t0 + tq - 1) // tk
    first_w = jnp.maximum(t0 - (WINDOW - 1), 0) // tk

    def scores(k_ref, j):
        kt = k_ref[j]
        return [_dot(kt, q_pairs[p]) for p in range(n_pairs)]

    n_cmp_pad = kc_ref.shape[2] // 2
    blk_i = lax.broadcasted_iota(jnp.int32, (n_cmp_pad, tq), 0)
    blk_t = t0 + lax.broadcasted_iota(jnp.int32, (n_cmp_pad, tq), 1)
    cmp_ok = (blk_i * CMP_STRIDE + (CMP_BLOCK - 1)) <= blk_t
    kc = kc_ref[0, 0]
    vct = vct_ref[0, 0]
    s_cmp = [_dot(kc, q_pairs[p]) for p in range(n_pairs)]
    for br, (k_ref, j0) in enumerate(((ks_ref, 0), (kw_ref, first_w))):
        s_first = scores(k_ref, j0)
        for p in range(n_pairs):
            s_ref[br, p] = s_first[p]
    p_sum = jnp.zeros((n_cmp_pad, tq), F32)
    p_cmp = []
    for p in range(n_pairs):
        probs = []
        for h in range(2):
            sh = jnp.where(cmp_ok, s_cmp[p][h * n_cmp_pad:(h + 1) * n_cmp_pad], NEG_INF)
            mh = jnp.max(sh, axis=0, keepdims=True)
            eh = jnp.where(cmp_ok, jnp.exp2(sh - mh), 0.0)
            den = jnp.sum(eh, axis=0, keepdims=True)
            ph = eh / jnp.where(den > 0.0, den, 1.0)
            p_sum = p_sum + ph
            probs.append(ph.astype(BF16))
        p_cmp.append(jnp.concatenate(probs, axis=0))
    o_cmp = [_dot(vct, p_cmp[p]) for p in range(n_pairs)]

    n_sel = t_len // SLC_BLOCK
    hi, mid, lo = _split3(p_sum)
    ovt = ovt_ref[...]
    p_sel = ((_dot(ovt, hi) + _dot(ovt, mid)) + _dot(ovt, lo))[0:n_sel]
    sel_i = lax.broadcasted_iota(jnp.int32, (n_sel, tq), 0)
    cur = (t0 + lax.broadcasted_iota(jnp.int32, (n_sel, tq), 1)) // SLC_BLOCK
    forced = (sel_i == 0) | (sel_i == cur) | (sel_i == cur - 1)
    score = jnp.where(forced, FORCE_SCORE, p_sel)
    score = jnp.where(sel_i <= cur, score, -jnp.inf)
    rank = jnp.zeros((n_sel, tq), jnp.int32)
    row_grp = 8
    grp_i = lax.broadcasted_iota(jnp.int32, (row_grp, tq), 0)
    for i in range(n_sel):
        ci = score[i:i + 1, :]
        ahead = []
        for r0 in range(0, n_sel, row_grp):
            rows = slice(r0, r0 + row_grp)
            if r0 > i:
                ahead.append(ci >= score[rows])
            elif r0 + row_grp <= i:
                ahead.append(ci > score[rows])
            else:
                ahead.append((ci > score[rows]) | ((ci == score[rows]) & (grp_i > i - r0)))
        rank = rank + jnp.where(jnp.concatenate(ahead, axis=0), 1, 0)
    chosen = jnp.where(rank < min(SLC_TOPK, n_sel), 1.0, 0.0)
    for i in range(n_sel):
        ch_ref[i] = chosen[i:i + 1, :]

    def flash(br, k_ref, v_ref, j_lo, j_hi, allowed_fn):
        m_ref[...] = jnp.full_like(m_ref, -jnp.inf)
        l_ref[...] = jnp.zeros_like(l_ref)
        acc_ref[...] = jnp.zeros_like(acc_ref)

        def body(j, carry):
            s_next = scores(k_ref, jnp.minimum(j + 1, j_hi - 1))
            allowed = allowed_fn(j)
            vt = v_ref[j]
            alphas = {}
            probs = {}
            for hh in range(HPG):
                p, h = divmod(hh, 2)
                a_parts = []
                p_parts = []
                for qh in range(tq // LANES):
                    ql = slice(qh * LANES, (qh + 1) * LANES)
                    sh = jnp.where(allowed[:, ql], s_ref[br, p, h * tk:(h + 1) * tk, ql], NEG_INF)
                    m_prev = m_ref[hh, :, ql]
                    m_new = jnp.maximum(m_prev, jnp.max(sh, axis=0, keepdims=True))
                    alpha = jnp.exp2(m_prev - m_new)
                    ph = jnp.exp2(sh - m_new)
                    l_ref[hh, :, ql] = alpha * l_ref[hh, :, ql] + jnp.sum(ph, axis=0, keepdims=True)
                    m_ref[hh, :, ql] = m_new
                    p_parts.append(ph.astype(BF16))
                    a_parts.append(alpha)
                alphas[hh] = jnp.concatenate(a_parts, axis=1)
                probs[hh] = jnp.concatenate(p_parts, axis=1)
            for p in range(n_pairs):
                a_pair = jnp.where(slab_lo, alphas[2 * p], alphas[2 * p + 1])
                pv = _dot(vt, jnp.concatenate([probs[2 * p], probs[2 * p + 1]], axis=0))
                s_ref[br, p] = s_next[p]
                acc_ref[p] = acc_ref[p] * a_pair + pv
            return carry

        lax.fori_loop(j_lo, j_hi, body, 0)
        outs = []
        for p in range(n_pairs):
            inv = jnp.where(slab_lo, 1.0 / l_ref[2 * p], 1.0 / l_ref[2 * p + 1])
            outs.append(acc_ref[p] * inv)
        return outs

    def slc_allowed(j):
        per_tile = tk // SLC_BLOCK
        picked = jnp.concatenate([jnp.broadcast_to(ch_ref[j * per_tile + i], (SLC_BLOCK, tq)) for i in range(per_tile)],
                                 axis=0) > 0.5
        return picked & ((j * tk + key_i) <= qry_t)

    def win_allowed(j):
        rel = qry_t - (j * tk + key_i)
        return (rel >= 0) & (rel < WINDOW)

    o_slc = flash(0, ks_ref, vs_ref, 0, last + 1, slc_allowed)
    o_win = flash(1, kw_ref, vw_ref, first_w, last + 1, win_allowed)

    gates = gt_ref[0]
    gain = gain_ref[...]
    for p in range(n_pairs):
        o = jnp.zeros((LANES, tq), F32)
        for c, branch in enumerate((o_cmp[p], o_slc[p], o_win[p])):
            r = c * HPG + 2 * p
            o = o + jnp.where(slab_lo, gates[r:r + 1, :], gates[r + 1:r + 2, :]) * branch
        sq = o * o
        ms_a = jnp.sum(sq[0:hd], axis=0, keepdims=True)
        ms_b = jnp.sum(sq[hd:2 * hd], axis=0, keepdims=True)
        ms = jnp.where(slab_lo, ms_a, ms_b) * (1.0 / hd)
        o = o * lax.rsqrt(ms + EPS)
        o_ref[0, :, p * LANES:(p + 1) * LANES] = (o.T * gain[:, p * LANES:(p + 1) * LANES]).astype(o_ref.dtype)


def _nsa_call(qt, ksw, vt, kc, vct, gt, gain, ovt):
    b, _, t = qt.shape
    tq, tk = ATT_TQ, ATT_TK
    n_kt = t // tk
    gw = HPG * NSA_HEAD_DIM
    hd = NSA_HEAD_DIM
    k_scratch = pltpu.VMEM((n_kt, 2 * tk, LANES), BF16)
    v_scratch = pltpu.VMEM((n_kt, LANES, 2 * tk), BF16)
    return pl.pallas_call(
        _nsa_body,
        grid=(b, NSA_KV_HEADS, t // tq),
        in_specs=[
            pl.BlockSpec((1, gw, tq), lambda bi, gi, qi: (bi, gi, qi)),
            pl.BlockSpec((1, t, 2 * KV_WIDTH), lambda bi, gi, qi: (bi, 0, 0)),
            pl.BlockSpec((1, hd, t), lambda bi, gi, qi: (bi, gi, 0)),
            pl.BlockSpec((1, hd, t), lambda bi, gi, qi: (bi, NSA_KV_HEADS + gi, 0)),
            pl.BlockSpec((1, 1) + kc.shape[2:], lambda bi, gi, qi: (bi, gi, 0, 0)),
            pl.BlockSpec((1, 1) + vct.shape[2:], lambda bi, gi, qi: (bi, gi, 0, 0)),
            pl.BlockSpec((1, LANES, tq), lambda bi, gi, qi: (bi, gi, qi)),
            pl.BlockSpec((1, gw), lambda bi, gi, qi: (0, gi)),
            pl.BlockSpec(ovt.shape, lambda bi, gi, qi: (0, 0)),
        ],
        out_specs=pl.BlockSpec((1, tq, gw), lambda bi, gi, qi: (bi, qi, gi)),
        out_shape=jax.ShapeDtypeStruct((b, t, NSA_WIDTH), BF16),
        scratch_shapes=[k_scratch, k_scratch, v_scratch, v_scratch,
                        pltpu.VMEM((HPG, 1, tq), F32), pltpu.VMEM((HPG, 1, tq), F32),
                        pltpu.VMEM((HPG // 2, LANES, tq), F32), pltpu.VMEM((2, HPG // 2, 2 * tk, tq), F32),
                        pltpu.VMEM((t // SLC_BLOCK, 1, tq), F32)],
        compiler_params=pltpu.CompilerParams(dimension_semantics=("arbitrary", "arbitrary", "arbitrary"),
                                             vmem_limit_bytes=VMEM_LIMIT),
        name="nsa_attention",
    )(qt, ksw, vt, vt, kc, vct, gt, gain, ovt)


def _ffn_body(x_ref, oh_ref, on_ref, woh_ref, won_ref, g2_ref, wg_ref, wu_ref, wd_ref, cw_ref, gf_ref,
              out_ref, halo_ref, act_ref, *, tiles_per_seq):
    tm = x_ref.shape[0]
    x1 = x_ref[...] + _dot(oh_ref[...], woh_ref[...]) + _dot(on_ref[...], won_ref[...])
    hb = _rms(x1, g2_ref[...]).astype(BF16)
    row = lax.broadcasted_iota(jnp.int32, (tm, FFN_TC), 0)

    @pl.when((pl.program_id(0) % tiles_per_seq) == 0)
    def _sequence_start():
        halo_ref[...] = jnp.zeros_like(halo_ref)

    def activation(c, gate, up):
        cols = slice(c * FFN_TC, (c + 1) * FFN_TC)
        halo = halo_ref[:, cols]
        halo_ref[:, cols] = gate[tm - 8:tm, :]
        prev1 = jnp.where(row == 0, halo[7:8, :], pltpu.roll(gate, 1, 0))
        prev2 = jnp.where(row == 0, halo[6:7, :], jnp.where(row == 1, halo[7:8, :], pltpu.roll(gate, 2, 0)))
        cw = cw_ref[:, cols]
        y = cw[0:1, :] * prev2 + cw[1:2, :] * prev1 + cw[2:3, :] * gate + cw[3:4, :]
        return (jax.nn.silu(y) * up).astype(BF16)

    chunk = lambda w_ref, c: _dot(hb, w_ref[:, c * FFN_TC:(c + 1) * FFN_TC])
    gate_up = (chunk(wg_ref, 0), chunk(wu_ref, 0))
    for c in range(FFN_NC):
        cur = gate_up
        if c + 1 < FFN_NC:
            gate_up = (chunk(wg_ref, c + 1), chunk(wu_ref, c + 1))
        act_ref[:, c * FFN_TC:(c + 1) * FFN_TC] = activation(c, *cur)
    acc = _dot(act_ref[...], wd_ref[...])
    out_ref[...] = _rms(x1 + acc, gf_ref[...])


def _ffn_call(x2, oh, on, woh, won, g2, wg, wu, wd, cw, gf, tiles_per_seq):
    n = x2.shape[0]
    tm = FFN_TM
    row = lambda w: pl.BlockSpec((tm, w), lambda i: (i, 0))
    full = lambda a: pl.BlockSpec(a.shape, lambda i: (0,) * a.ndim, pipeline_mode=pl.Buffered(1))
    return pl.pallas_call(
        functools.partial(_ffn_body, tiles_per_seq=tiles_per_seq),
        grid=(n // tm,),
        in_specs=[row(D_MODEL), row(HG_WIDTH), row(NSA_WIDTH), full(woh), full(won), full(g2),
                  full(wg), full(wu), full(wd), full(cw), full(gf)],
        out_specs=row(D_MODEL),
        out_shape=jax.ShapeDtypeStruct((n, D_MODEL), F32),
        scratch_shapes=[pltpu.VMEM((8, D_FF), F32), pltpu.VMEM((tm, D_FF), BF16)],
        compiler_params=pltpu.CompilerParams(dimension_semantics=("arbitrary",),
                                             vmem_limit_bytes=VMEM_LIMIT),
        name="outproj_convffn",
    )(x2, oh, on, woh, won, g2, wg, wu, wd, cw, gf)


def _rope_angles(positions):
    inv_freq = ROPE_THETA ** (-jnp.arange(ROPE_HALF, dtype=F32) * 2.0 / ROPE_DIM)
    ang = positions.astype(F32)[..., None] * inv_freq
    return jnp.concatenate([jnp.cos(ang), jnp.sin(ang)], axis=-1).transpose(0, 2, 1)


def _layer(x, positions, ln1, w_in, lb, hg_gain, pe_k, pe_v, k_w1, k_w2, v_w1, v_w2, nsa_gain, w_o, ln2,
           w_gate, w_up, conv_w, conv_b, w_down, final_gain):
    b, t, d = x.shape
    n = b * t
    assert d == D_MODEL and t % FFN_TM == 0 and t % PROJ_TM == 0 and t % ATT_TQ == 0 and t % HG_TT == 0
    n_grp = t // CMP_STRIDE
    assert n_grp == LANES, "compressed-block axis is laid out on exactly one lane tile"
    n_sel = t // SLC_BLOCK
    assert n_sel % 8 == 0 and n_sel <= LANES and ATT_TK % SLC_BLOCK == 0
    x2 = x.reshape(n, d)

    splits = np.cumsum([0, 4 * HG_WIDTH, NSA_WIDTH] + [KV_WIDTH] * 6 + [N_GATES])
    seg = lambda i: w_in[:, splits[i]:splits[i + 1]]
    wh = seg(0).astype(BF16)
    wk = jnp.concatenate([seg(2), seg(3), seg(4), seg(6)], axis=1).astype(BF16)
    wgate = seg(8).reshape(d, 3, NSA_KV_HEADS, HPG).transpose(0, 2, 1, 3).reshape(d, NSA_KV_HEADS, 3 * HPG)
    wgate = jnp.pad(wgate, ((0, 0), (0, 0), (0, LANES - 3 * HPG))).reshape(d, NSA_KV_HEADS * LANES)
    wt = jnp.concatenate([seg(1), seg(5), seg(7), wgate], axis=1).T.astype(BF16)
    cs = _rope_angles(positions)

    hg, kcn, vcn, ksw, qt, vt, gt = _inproj_call(x2, ln1.reshape(1, d), wh, wk, wt, cs, t // PROJ_TM)

    mst, lvl = _hgrn_tables()
    o_hg = _hgrn_call(hg.reshape(b, t, 4 * HG_WIDTH), lb.reshape(1, HG_WIDTH).astype(F32),
                      hg_gain.reshape(1, HG_WIDTH), mst, lvl)

    per_lane = lambda a: jnp.broadcast_to(a.reshape(2, CMP_STRIDE, 1, NSA_HEAD_DIM, -1),
                                          (2, CMP_STRIDE, NSA_KV_HEADS, NSA_HEAD_DIM, a.shape[-1]))
    w1_rows = lambda w1: per_lane(w1).reshape(2, CMP_STRIDE * LANES, CMP_HIDDEN).astype(BF16)
    pe_rows = lambda pe: per_lane(pe[..., None]).reshape(2, 1, CMP_STRIDE * LANES)
    zeros_w2 = jnp.zeros((CMP_HIDDEN, NSA_HEAD_DIM), F32)
    place = lambda w2: jnp.stack([jnp.concatenate([w2, zeros_w2], 1), jnp.concatenate([zeros_w2, w2], 1)])
    kc, vct = _cmp_call(kcn.reshape(b, t, KV_WIDTH), vcn.reshape(b, t, KV_WIDTH), pe_rows(pe_k), pe_rows(pe_v),
                        w1_rows(k_w1), w1_rows(v_w1),
                        place(k_w2).astype(BF16), place(v_w2).transpose(0, 2, 1).astype(BF16))

    cmp_start = np.arange(n_grp) * CMP_STRIDE
    cmp_end = cmp_start + CMP_BLOCK - 1
    sel_start = np.arange(LANES) * SLC_BLOCK
    overlap = ((cmp_start[:, None] <= sel_start[None, :] + SLC_BLOCK - 1) & (cmp_end[:, None] >= sel_start[None, :])
               & (np.arange(LANES)[None, :] < n_sel) & (np.arange(n_grp)[:, None] < n_grp - 1))
    ovt = jnp.asarray(overlap.T.astype(np.float32), BF16)
    o_nsa = _nsa_call(qt, ksw.reshape(b, t, 2 * KV_WIDTH), vt, kc, vct, gt, nsa_gain.reshape(1, NSA_WIDTH), ovt)

    cw = jnp.concatenate([conv_w, conv_b[None, :], jnp.zeros((4, D_FF), F32)], axis=0)
    out = _ffn_call(x2, o_hg.reshape(n, HG_WIDTH), o_nsa.reshape(n, NSA_WIDTH),
                    w_o[:HG_WIDTH].astype(BF16), w_o[HG_WIDTH:].astype(BF16), ln2.reshape(1, d),
                    w_gate.astype(BF16), w_up.astype(BF16), w_down.astype(BF16), cw,
                    final_gain.reshape(1, d), t // FFN_TM)
    return out.reshape(b, t, d)


def kernel(x, positions, ln1_gain, w_in, hgrn_lb_param, hgrn_out_gain, cmp_pe_k, cmp_pe_v, cmp_k_w1, cmp_k_w2,
           cmp_v_w1, cmp_v_w2, nsa_out_gain, w_o, ln2_gain, ffn_w_gate, ffn_w_up, ffn_conv_w, ffn_conv_b,
           ffn_w_down, final_gain):
    depth = ln1_gain.shape[0]
    assert depth == 1, "the fused final norm assumes a single layer"
    lower_bounds = jnp.cumsum(jax.nn.softmax(hgrn_lb_param.astype(F32), axis=0), axis=0)
    l = 0
    return _layer(x, positions, ln1_gain[l], w_in[l], lower_bounds[l], hgrn_out_gain[l], cmp_pe_k[l], cmp_pe_v[l],
                  cmp_k_w1[l], cmp_k_w2[l], cmp_v_w1[l], cmp_v_w2[l], nsa_out_gain[l], w_o[l], ln2_gain[l],
                  ffn_w_gate[l], ffn_w_up[l], ffn_conv_w[l], ffn_conv_b[l], ffn_w_down[l], final_gain)
```

```python
import functools

import jax
import jax.numpy as jnp
import numpy as np
from jax import lax
from jax.experimental import pallas as pl
from jax.experimental.pallas import tpu as pltpu

F32 = jnp.float32
BF16 = jnp.bfloat16

D_MODEL = 1024
HG_HEADS = 4
HG_DK = 128
HG_DV = 128
HG_WIDTH = HG_HEADS * HG_DV
NSA_HEADS = 8
NSA_KV_HEADS = 2
NSA_HEAD_DIM = 64
HPG = NSA_HEADS // NSA_KV_HEADS
NSA_WIDTH = NSA_HEADS * NSA_HEAD_DIM
KV_WIDTH = NSA_KV_HEADS * NSA_HEAD_DIM
CMP_BLOCK = 32
CMP_STRIDE = 16
CMP_HIDDEN = 256
SLC_BLOCK = 64
SLC_TOPK = 16
WINDOW = 512
ROPE_THETA = 500000.0
ROPE_DIM = NSA_HEAD_DIM // 4
ROPE_HALF = ROPE_DIM // 2
D_FF = 2816
EPS = 1e-6
NEG_INF = -1e30
FORCE_SCORE = 1e4
N_GATES = 3 * NSA_HEADS
LOG2_E = 1.4426950408889634

LANES = 128
VMEM_LIMIT = 56 * 1024 * 1024

PROJ_TM = 512
HG_CHUNK = 128
HG_LEVELS = (16, 32, 64)
HG_DIAG = 16
HG_TT = 512
ATT_TQ = 256
ATT_TK = 256
SUM_ROWS = 16
FFN_TM = 512
FFN_TC = 256
FFN_NC = D_FF // FFN_TC


def _dot(a, b):
    return jnp.dot(a, b, preferred_element_type=F32)


def _dot_nt(a, b):
    return lax.dot_general(a, b, (((1,), (1,)), ((), ())), preferred_element_type=F32)


def _dot_tn(a, b):
    return lax.dot_general(a, b, (((0,), (0,)), ((), ())), preferred_element_type=F32)


def _split3(x):
    hi = x.astype(BF16)
    r = x - hi.astype(F32)
    mid = r.astype(BF16)
    lo = (r - mid.astype(F32)).astype(BF16)
    return hi, mid, lo


def _rms(x, gain):
    return x * lax.rsqrt(jnp.mean(x * x, axis=-1, keepdims=True) + EPS) * gain


def _inproj_body(x_ref, g_ref, wh_ref, wk_ref, wt_ref, cs_ref,
                 hg_ref, kcn_ref, vcn_ref, ksw_ref, qt_ref, vt_ref, gt_ref):
    hb = _rms(x_ref[...], g_ref[...]).astype(BF16)
    hg_ref[...] = _dot(hb, wh_ref[...])

    def rope(v, axis, cos, sin_hi, sin_lo):
        return (v * cos + pltpu.roll(v, ROPE_HALF, axis) * sin_hi
                + pltpu.roll(v, LANES - ROPE_HALF, axis) * sin_lo)

    cos = cs_ref[0, 0:ROPE_HALF, :]
    sin = cs_ref[0, ROPE_HALF:ROPE_DIM, :]
    tm = cos.shape[1]
    zero_h = jnp.zeros((ROPE_HALF, tm), F32)
    rest = NSA_HEAD_DIM - ROPE_DIM
    slab = lambda lo, hi, fill: jnp.concatenate([lo, hi, jnp.full((rest, tm), fill, F32)] * (LANES // NSA_HEAD_DIM), axis=0)
    tab_t = (slab(cos, cos, 1.0), slab(zero_h, sin, 0.0), slab(-sin, zero_h, 0.0))
    tab = tuple(a.T for a in tab_t)
    kn = _dot(hb, wk_ref[...])
    kcn_ref[...] = rope(kn[:, 0:LANES], 1, *tab)
    vcn_ref[...] = kn[:, LANES:2 * LANES]
    ksw_ref[:, 0:LANES] = rope(kn[:, 2 * LANES:3 * LANES], 1, *tab).astype(BF16)
    ksw_ref[:, LANES:2 * LANES] = rope(kn[:, 3 * LANES:4 * LANES], 1, *tab).astype(BF16)

    rt = _dot_nt(wt_ref[...], hb)
    scale = NSA_HEAD_DIM ** -0.5 * LOG2_E
    for j in range(NSA_WIDTH // LANES):
        sl = slice(j * LANES, (j + 1) * LANES)
        qt_ref[0, sl, :] = (rope(rt[sl], 0, *tab_t) * scale).astype(BF16)
    vt_ref[0] = rt[NSA_WIDTH:NSA_WIDTH + 2 * KV_WIDTH].astype(BF16)
    gt_ref[0] = jax.nn.sigmoid(rt[NSA_WIDTH + 2 * KV_WIDTH:])


def _inproj_call(x2, gain, wh, wk, wt, cs, tiles_per_seq):
    n = x2.shape[0]
    tm = PROJ_TM
    t = tiles_per_seq * tm
    b = n // t
    row = lambda w: pl.BlockSpec((tm, w), lambda i: (i, 0))
    col = lambda h: pl.BlockSpec((1, h, tm), lambda i: (i // tiles_per_seq, 0, i % tiles_per_seq))
    full = lambda a: pl.BlockSpec(a.shape, lambda i: (0, 0))
    gate_rows = NSA_KV_HEADS * LANES
    return pl.pallas_call(
        _inproj_body,
        grid=(n // tm,),
        in_specs=[row(D_MODEL), full(gain), full(wh), full(wk), full(wt),
                  col(ROPE_DIM)],
        out_specs=[row(4 * HG_WIDTH), row(KV_WIDTH), row(KV_WIDTH), row(2 * KV_WIDTH),
                   col(NSA_WIDTH), col(2 * KV_WIDTH), col(gate_rows)],
        out_shape=[jax.ShapeDtypeStruct((n, 4 * HG_WIDTH), F32),
                   jax.ShapeDtypeStruct((n, KV_WIDTH), F32),
                   jax.ShapeDtypeStruct((n, KV_WIDTH), F32),
                   jax.ShapeDtypeStruct((n, 2 * KV_WIDTH), BF16),
                   jax.ShapeDtypeStruct((b, NSA_WIDTH, t), BF16),
                   jax.ShapeDtypeStruct((b, 2 * KV_WIDTH, t), BF16),
                   jax.ShapeDtypeStruct((b, gate_rows, t), F32)],
        compiler_params=pltpu.CompilerParams(dimension_semantics=("arbitrary",),
                                             vmem_limit_bytes=VMEM_LIMIT),
        name="inproj",
    )(x2, gain, wh, wk, wt, cs)


def _hgrn_tables():
    L = HG_CHUNK
    t = np.arange(L)[:, None]
    u = np.arange(L)[None, :]
    level = np.where(((t // HG_DIAG) == (u // HG_DIAG)) & (u <= t), 1, 0)
    for li, s in enumerate(HG_LEVELS):
        same = (t // (2 * s)) == (u // (2 * s))
        right = (t % (2 * s)) >= s
        level = np.where(same & right & ((u % (2 * s)) < s), li + 2, level)
    return jnp.asarray((u <= t).astype(np.float32), BF16), jnp.asarray(level, jnp.int32)


def _hgrn_body(q_ref, f_ref, i_ref, g_ref, lb_ref, gain_ref, mst_ref, lvl_ref, o_ref, st_ref):
    L = HG_CHUNK
    n_chunks = q_ref.shape[1] // L

    @pl.when(pl.program_id(1) == 0)
    def _sequence_start():
        st_ref[...] = jnp.zeros_like(st_ref)

    def chunk(c, carry):
        rows = pl.ds(pl.multiple_of(c * L, L), L)
        heads = range(HG_HEADS)
        cols = [slice(h * HG_DK, (h + 1) * HG_DK) for h in heads]
        mst = mst_ref[...]
        lvl = lvl_ref[...]
        n_lv = len(HG_LEVELS)
        row_i = lax.broadcasted_iota(jnp.int32, (L, HG_DK), 0)
        q = [q_ref[0, rows, cols[h]] for h in heads]
        vb = [i_ref[0, rows, cols[h]].astype(BF16) for h in heads]
        f = [lb_ref[:, cols[h]] + (1.0 - lb_ref[:, cols[h]]) * jax.nn.sigmoid(f_ref[0, rows, cols[h]]) for h in heads]
        k = [1.0 - f[h] for h in heads]
        parts = [_split3(jnp.log(f[h])) for h in heads]
        e_full = [(_dot(mst, parts[h][0]) + _dot(mst, parts[h][1])) + _dot(mst, parts[h][2]) for h in heads]
        b_last = [e_full[h][L - 1:L, :] for h in heads]

        def rel_to(b, blk, off):
            refs = []
            for r0 in range(0, L, blk):
                r = r0 + off - 1
                ref = b[r:r + 1, :] if r >= 0 else jnp.zeros((1, HG_DK), F32)
                refs.append(jnp.broadcast_to(ref, (blk, HG_DK)))
            return b - jnp.concatenate(refs, axis=0)

        def level_sums(b):
            out = [rel_to(b, HG_DIAG, 0)]
            for s_half in HG_LEVELS:
                d = rel_to(b, 2 * s_half, s_half)
                out.append(jnp.where((row_i % (2 * s_half)) >= s_half, d, -d))
            return out

        e = [level_sums(e_full[h]) for h in heads]
        wq = [[jnp.exp(e[h][l]) for l in range(n_lv + 1)] for h in heads]
        wk = [[jnp.exp(-e[h][0])] + wq[h][1:] for h in heads]
        prod = [[_dot_nt((q[h] * wq[h][l]).astype(BF16), (k[h] * wk[h][l]).astype(BF16)) for l in range(n_lv + 1)]
                for h in heads]
        st = [st_ref[h] for h in heads]
        inter = [_dot_nt((q[h] * jnp.exp(e_full[h])).astype(BF16), st[h].astype(BF16)) for h in heads]
        k_dec = [(k[h] * jnp.exp(b_last[h] - e_full[h])).astype(BF16) for h in heads]
        upd = [_dot_tn(vb[h], k_dec[h]) for h in heads]
        for h in heads:
            st_ref[h] = st[h] * jnp.exp(b_last[h]) + upd[h]
        a = []
        for h in heads:
            ah = jnp.where(lvl == 1, prod[h][0], 0.0)
            for l in range(1, n_lv + 1):
                ah = jnp.where(lvl == l + 1, prod[h][l], ah)
            a.append(ah.astype(BF16))
        o = [_dot(a[h], vb[h]) + inter[h] for h in heads]
        for h in heads:
            oh = o[h] * lax.rsqrt(jnp.mean(o[h] * o[h], axis=-1, keepdims=True) + EPS) * gain_ref[:, cols[h]]
            o_ref[0, rows, cols[h]] = (oh * jax.nn.silu(g_ref[0, rows, cols[h]])).astype(o_ref.dtype)
        return carry

    lax.fori_loop(0, n_chunks, chunk, 0, unroll=2)


def _hgrn_call(hg, lb, gain, mst, lvl):
    b, t, _ = hg.shape
    tt = HG_TT
    col = lambda k: pl.BlockSpec((1, tt, HG_WIDTH), lambda bi, ti: (bi, ti, k))
    full = lambda a: pl.BlockSpec(a.shape, lambda bi, ti: (0, 0))
    return pl.pallas_call(
        _hgrn_body,
        grid=(b, t // tt),
        in_specs=[col(0), col(1), col(2), col(3), full(lb), full(gain), full(mst), full(lvl)],
        out_specs=pl.BlockSpec((1, tt, HG_WIDTH), lambda bi, ti: (bi, ti, 0)),
        out_shape=jax.ShapeDtypeStruct((b, t, HG_WIDTH), BF16),
        scratch_shapes=[pltpu.VMEM((HG_HEADS, HG_DV, HG_DK), F32)],
        compiler_params=pltpu.CompilerParams(dimension_semantics=("arbitrary", "arbitrary"),
                                             vmem_limit_bytes=VMEM_LIMIT),
        name="hgrn2",
    )(hg, hg, hg, hg, lb, gain, mst, lvl)


def _cmp_body(kcn_ref, vcn_ref, pek_ref, pev_ref, w1k_ref, w1v_ref, w2k_ref, w2v_ref, kc_ref, vc_ref):
    nb = kcn_ref.shape[1] // CMP_STRIDE
    lane_grp = (lax.broadcasted_iota(jnp.int32, (nb, CMP_STRIDE * LANES), 1) // NSA_HEAD_DIM) % NSA_KV_HEADS

    def hidden(src_ref, pe_ref, w1_ref):
        x = jnp.concatenate([src_ref[0, pl.ds(l, nb, stride=CMP_STRIDE), :]
                             for l in range(CMP_STRIDE)], axis=1)
        halves = [x + pe_ref[i] for i in range(2)]
        out = []
        for g in range(NSA_KV_HEADS):
            u, v = (_dot(jnp.where(lane_grp == g, halves[i], 0.0).astype(BF16), w1_ref[i]) for i in range(2))
            out.append(jax.nn.silu(u + pltpu.roll(v, nb - 1, 0)).astype(BF16))
        return out

    hk = hidden(kcn_ref, pek_ref, w1k_ref)
    hv = hidden(vcn_ref, pev_ref, w1v_ref)
    for g in range(NSA_KV_HEADS):
        kc_ref[0, g, 0:nb, :] = _dot(hk[g], w2k_ref[0]).astype(kc_ref.dtype)
        kc_ref[0, g, nb:2 * nb, :] = _dot(hk[g], w2k_ref[1]).astype(kc_ref.dtype)
        vc_ref[0, g, :, 0:nb] = _dot_nt(w2v_ref[0], hv[g]).astype(vc_ref.dtype)
        vc_ref[0, g, :, nb:2 * nb] = _dot_nt(w2v_ref[1], hv[g]).astype(vc_ref.dtype)


def _cmp_call(kcn, vcn, pek, pev, w1k, w1v, w2k, w2v):
    b, t, w = kcn.shape
    nb = t // CMP_STRIDE
    full = lambda a: pl.BlockSpec(a.shape, lambda bi: (0,) * a.ndim)
    out = lambda r, c: pl.BlockSpec((1, NSA_KV_HEADS, r, c), lambda bi: (bi, 0, 0, 0))
    return pl.pallas_call(
        _cmp_body,
        grid=(b,),
        in_specs=[pl.BlockSpec((1, t, w), lambda bi: (bi, 0, 0)), pl.BlockSpec((1, t, w), lambda bi: (bi, 0, 0)),
                  full(pek), full(pev), full(w1k), full(w1v), full(w2k), full(w2v)],
        out_specs=[out(2 * nb, LANES), out(LANES, 2 * nb)],
        out_shape=[jax.ShapeDtypeStruct((b, NSA_KV_HEADS, 2 * nb, LANES), BF16),
                   jax.ShapeDtypeStruct((b, NSA_KV_HEADS, LANES, 2 * nb), BF16)],
        compiler_params=pltpu.CompilerParams(dimension_semantics=("arbitrary",),
                                             vmem_limit_bytes=VMEM_LIMIT),
        name="nsa_compress",
    )(kcn, vcn, pek, pev, w1k, w1v, w2k, w2v)


def _nsa_body(qt_ref, ksw_ref, vst_ref, vwt_ref, kc_ref, vct_ref, gt_ref, gain_ref, ovt_ref, o_ref,
              ks_ref, kw_ref, vs_ref, vw_ref, m_ref, acc_ref, s_ref, ch_ref):
    g = pl.program_id(1)
    qi = pl.program_id(2)
    tq = ATT_TQ
    tk = ATT_TK
    t_len = ksw_ref.shape[1]
    n_kt = t_len // tk
    n_pairs = HPG // 2
    hd = NSA_HEAD_DIM

    @pl.when(qi == 0)
    def _build_kv():
        lane = lax.broadcasted_iota(jnp.int32, (tk, LANES), 1)
        lo_lane = lane < hd
        keep = (lane // hd) == g

        def build_k(src_col, dst_ref):
            def body(j, carry):
                rows = pl.ds(pl.multiple_of(j * tk, tk), tk)
                x = ksw_ref[0, rows, src_col * LANES:(src_col + 1) * LANES].astype(F32)
                dup = jnp.where(keep, x, pltpu.roll(x, hd, 1))
                dst_ref[j, 0:tk, :] = jnp.where(lo_lane, dup, 0.0).astype(BF16)
                dst_ref[j, tk:2 * tk, :] = jnp.where(lo_lane, 0.0, dup).astype(BF16)
                return carry
            lax.fori_loop(0, n_kt, body, 0)

        def build_v(src_ref, dst_ref):
            zero = jnp.zeros((hd, tk), BF16)
            row = lax.broadcasted_iota(jnp.int32, (SUM_ROWS, 2 * tk), 0)
            col = lax.broadcasted_iota(jnp.int32, (SUM_ROWS, 2 * tk), 1)
            ones_rows = jnp.where(((row == 0) & (col < tk)) | ((row == 1) & (col >= tk)), 1.0, 0.0).astype(BF16)
            for j in range(n_kt):
                x = src_ref[0, :, j * tk:(j + 1) * tk]
                dst_ref[j, 0:hd, 0:tk] = x
                dst_ref[j, 0:hd, tk:2 * tk] = zero
                dst_ref[j, hd:2 * hd, 0:tk] = zero
                dst_ref[j, hd:2 * hd, tk:2 * tk] = x
                dst_ref[j, 2 * hd:2 * hd + SUM_ROWS, :] = ones_rows

        build_k(0, ks_ref)
        build_k(1, kw_ref)
        build_v(vst_ref, vs_ref)
        build_v(vwt_ref, vw_ref)

    t0 = qi * tq
    key_i = lax.broadcasted_iota(jnp.int32, (tk, tq), 0)
    qry_t = t0 + lax.broadcasted_iota(jnp.int32, (tk, tq), 1)
    slab_lo = lax.broadcasted_iota(jnp.int32, (LANES, tq), 0) < hd
    acc_row = lax.broadcasted_iota(jnp.int32, (LANES + SUM_ROWS, tq), 0)
    slab_a = (acc_row < hd) | (acc_row == LANES)
    q_pairs = [qt_ref[0, p * LANES:(p + 1) * LANES, :] for p in range(n_pairs)]

    last = (t0 + tq - 1) // tk
    first_w = jnp.maximum(t0 - (WINDOW - 1), 0) // tk

    def scores(k_ref, j):
        kt = k_ref[j]
        return [_dot(kt, q_pairs[p]) for p in range(n_pairs)]

    n_cmp_pad = kc_ref.shape[2] // 2
    blk_i = lax.broadcasted_iota(jnp.int32, (n_cmp_pad, tq), 0)
    blk_t = t0 + lax.broadcasted_iota(jnp.int32, (n_cmp_pad, tq), 1)
    cmp_ok = (blk_i * CMP_STRIDE + (CMP_BLOCK - 1)) <= blk_t
    kc = kc_ref[0, 0]
    vct = vct_ref[0, 0]
    s_cmp = [_dot(kc, q_pairs[p]) for p in range(n_pairs)]
    for br, (k_ref, j0) in enumerate(((ks_ref, 0), (kw_ref, first_w))):
        s_first = scores(k_ref, j0)
        for p in range(n_pairs):
            s_ref[br, p] = s_first[p]
    p_sum = jnp.zeros((n_cmp_pad, tq), F32)
    p_cmp = []
    for p in range(n_pairs):
        probs = []
        for h in range(2):
            sh = jnp.where(cmp_ok, s_cmp[p][h * n_cmp_pad:(h + 1) * n_cmp_pad], NEG_INF)
            mh = jnp.max(sh, axis=0, keepdims=True)
            eh = jnp.where(cmp_ok, jnp.exp2(sh - mh), 0.0)
            den = jnp.sum(eh, axis=0, keepdims=True)
            ph = eh / jnp.where(den > 0.0, den, 1.0)
            p_sum = p_sum + ph
            probs.append(ph.astype(BF16))
        p_cmp.append(jnp.concatenate(probs, axis=0))
    o_cmp = [_dot(vct, p_cmp[p]) for p in range(n_pairs)]

    n_sel = t_len // SLC_BLOCK
    hi, mid, lo = _split3(p_sum)
    ovt = ovt_ref[...]
    p_sel = ((_dot(ovt, hi) + _dot(ovt, mid)) + _dot(ovt, lo))[0:n_sel]
    sel_i = lax.broadcasted_iota(jnp.int32, (n_sel, tq), 0)
    cur = (t0 + lax.broadcasted_iota(jnp.int32, (n_sel, tq), 1)) // SLC_BLOCK
    forced = (sel_i == 0) | (sel_i == cur) | (sel_i == cur - 1)
    score = jnp.where(forced, FORCE_SCORE, p_sel)
    score = jnp.where(sel_i <= cur, score, -jnp.inf)
    rank = jnp.zeros((n_sel, tq), jnp.int32)
    row_grp = 8
    grp_i = lax.broadcasted_iota(jnp.int32, (row_grp, tq), 0)
    for i in range(n_sel):
        ci = score[i:i + 1, :]
        ahead = []
        for r0 in range(0, n_sel, row_grp):
            rows = slice(r0, r0 + row_grp)
            if r0 > i:
                ahead.append(ci >= score[rows])
            elif r0 + row_grp <= i:
                ahead.append(ci > score[rows])
            else:
                ahead.append((ci > score[rows]) | ((ci == score[rows]) & (grp_i > i - r0)))
        rank = rank + jnp.where(jnp.concatenate(ahead, axis=0), 1, 0)
    chosen = jnp.where(rank < min(SLC_TOPK, n_sel), 1.0, 0.0)
    for i in range(n_sel):
        ch_ref[i] = chosen[i:i + 1, :]

    def flash(br, k_ref, v_ref, j_lo, j_hi, bias_fn):
        m_ref[...] = jnp.full_like(m_ref, -jnp.inf)
        acc_ref[...] = jnp.zeros_like(acc_ref)

        def body(j, carry):
            s_next = scores(k_ref, jnp.minimum(j + 1, j_hi - 1))
            bias = bias_fn(j)
            vt = v_ref[j]
            alphas = {}
            probs = {}
            for hh in range(HPG):
                p, h = divmod(hh, 2)
                a_parts = []
                p_parts = []
                for qh in range(tq // LANES):
                    ql = slice(qh * LANES, (qh + 1) * LANES)
                    sh = s_ref[br, p, h * tk:(h + 1) * tk, ql] + bias[:, ql]
                    m_prev = m_ref[hh, :, ql]
                    m_new = jnp.maximum(m_prev, jnp.max(sh, axis=0, keepdims=True))
                    m_ref[hh, :, ql] = m_new
                    p_parts.append(jnp.exp2(sh - m_new).astype(BF16))
                    a_parts.append(jnp.exp2(m_prev - m_new))
                alphas[hh] = jnp.concatenate(a_parts, axis=1)
                probs[hh] = jnp.concatenate(p_parts, axis=1)
            for p in range(n_pairs):
                a_rows = jnp.where(slab_a, alphas[2 * p], alphas[2 * p + 1])
                pv = _dot(vt, jnp.concatenate([probs[2 * p], probs[2 * p + 1]], axis=0))
                s_ref[br, p] = s_next[p]
                acc_ref[p] = acc_ref[p] * a_rows + pv
            return carry

        lax.fori_loop(j_lo, j_hi, body, 0)
        outs = []
        for p in range(n_pairs):
            acc = acc_ref[p]
            inv = jnp.where(slab_lo, 1.0 / acc[LANES:LANES + 1, :], 1.0 / acc[LANES + 1:LANES + 2, :])
            outs.append(acc[0:LANES] * inv)
        return outs

    def slc_bias(j):
        per_tile = tk // SLC_BLOCK
        picked = jnp.concatenate([jnp.broadcast_to(ch_ref[j * per_tile + i], (SLC_BLOCK, tq)) for i in range(per_tile)],
                                 axis=0) > 0.5
        return jnp.where(picked & ((j * tk + key_i) <= qry_t), 0.0, NEG_INF)

    def win_bias(j):
        rel = qry_t - (j * tk + key_i)
        return jnp.where((rel >= 0) & (rel < WINDOW), 0.0, NEG_INF)

    o_slc = flash(0, ks_ref, vs_ref, 0, last + 1, slc_bias)
    o_win = flash(1, kw_ref, vw_ref, first_w, last + 1, win_bias)

    gates = gt_ref[0]
    gain = gain_ref[...]
    for p in range(n_pairs):
        o = jnp.zeros((LANES, tq), F32)
        for c, branch in enumerate((o_cmp[p], o_slc[p], o_win[p])):
            r = c * HPG + 2 * p
            o = o + jnp.where(slab_lo, gates[r:r + 1, :], gates[r + 1:r + 2, :]) * branch
        sq = o * o
        ms_a = jnp.sum(sq[0:hd], axis=0, keepdims=True)
        ms_b = jnp.sum(sq[hd:2 * hd], axis=0, keepdims=True)
        ms = jnp.where(slab_lo, ms_a, ms_b) * (1.0 / hd)
        o = o * lax.rsqrt(ms + EPS)
        o_ref[0, :, p * LANES:(p + 1) * LANES] = (o.T * gain[:, p * LANES:(p + 1) * LANES]).astype(o_ref.dtype)


def _nsa_call(qt, ksw, vt, kc, vct, gt, gain, ovt):
    b, _, t = qt.shape
    tq, tk = ATT_TQ, ATT_TK
    n_kt = t // tk
    gw = HPG * NSA_HEAD_DIM
    hd = NSA_HEAD_DIM
    k_scratch = pltpu.VMEM((n_kt, 2 * tk, LANES), BF16)
    v_scratch = pltpu.VMEM((n_kt, LANES + SUM_ROWS, 2 * tk), BF16)
    return pl.pallas_call(
        _nsa_body,
        grid=(b, NSA_KV_HEADS, t // tq),
        in_specs=[
            pl.BlockSpec((1, gw, tq), lambda bi, gi, qi: (bi, gi, qi)),
            pl.BlockSpec((1, t, 2 * KV_WIDTH), lambda bi, gi, qi: (bi, 0, 0)),
            pl.BlockSpec((1, hd, t), lambda bi, gi, qi: (bi, gi, 0)),
            pl.BlockSpec((1, hd, t), lambda bi, gi, qi: (bi, NSA_KV_HEADS + gi, 0)),
            pl.BlockSpec((1, 1) + kc.shape[2:], lambda bi, gi, qi: (bi, gi, 0, 0)),
            pl.BlockSpec((1, 1) + vct.shape[2:], lambda bi, gi, qi: (bi, gi, 0, 0)),
            pl.BlockSpec((1, LANES, tq), lambda bi, gi, qi: (bi, gi, qi)),
            pl.BlockSpec((1, gw), lambda bi, gi, qi: (0, gi)),
            pl.BlockSpec(ovt.shape, lambda bi, gi, qi: (0, 0)),
        ],
        out_specs=pl.BlockSpec((1, tq, gw), lambda bi, gi, qi: (bi, qi, gi)),
        out_shape=jax.ShapeDtypeStruct((b, t, NSA_WIDTH), BF16),
        scratch_shapes=[k_scratch, k_scratch, v_scratch, v_scratch,
                        pltpu.VMEM((HPG, 1, tq), F32),
                        pltpu.VMEM((HPG // 2, LANES + SUM_ROWS, tq), F32), pltpu.VMEM((2, HPG // 2, 2 * tk, tq), F32),
                        pltpu.VMEM((t // SLC_BLOCK, 1, tq), F32)],
        compiler_params=pltpu.CompilerParams(dimension_semantics=("arbitrary", "arbitrary", "arbitrary"),
                                             vmem_limit_bytes=VMEM_LIMIT),
        name="nsa_attention",
    )(qt, ksw, vt, vt, kc, vct, gt, gain, ovt)


def _ffn_body(x_ref, oh_ref, on_ref, woh_ref, won_ref, g2_ref, wg_ref, wu_ref, wd_ref, cw_ref, gf_ref,
              out_ref, halo_ref, act_ref, *, tiles_per_seq):
    tm = x_ref.shape[0]
    x1 = x_ref[...] + _dot(oh_ref[...], woh_ref[...]) + _dot(on_ref[...], won_ref[...])
    hb = _rms(x1, g2_ref[...]).astype(BF16)
    row = lax.broadcasted_iota(jnp.int32, (tm, FFN_TC), 0)

    @pl.when((pl.program_id(0) % tiles_per_seq) == 0)
    def _sequence_start():
        halo_ref[...] = jnp.zeros_like(halo_ref)

    def activation(c, gate, up):
        cols = slice(c * FFN_TC, (c + 1) * FFN_TC)
        halo = halo_ref[:, cols]
        halo_ref[:, cols] = gate[tm - 8:tm, :]
        prev1 = jnp.where(row == 0, halo[7:8, :], pltpu.roll(gate, 1, 0))
        prev2 = jnp.where(row == 0, halo[6:7, :], jnp.where(row == 1, halo[7:8, :], pltpu.roll(gate, 2, 0)))
        cw = cw_ref[:, cols]
        y = cw[0:1, :] * prev2 + cw[1:2, :] * prev1 + cw[2:3, :] * gate + cw[3:4, :]
        return (jax.nn.silu(y) * up).astype(BF16)

    chunk = lambda w_ref, c: _dot(hb, w_ref[:, c * FFN_TC:(c + 1) * FFN_TC])
    gate_up = (chunk(wg_ref, 0), chunk(wu_ref, 0))
    for c in range(FFN_NC):
        cur = gate_up
        if c + 1 < FFN_NC:
            gate_up = (chunk(wg_ref, c + 1), chunk(wu_ref, c + 1))
        act_ref[:, c * FFN_TC:(c + 1) * FFN_TC] = activation(c, *cur)
    acc = _dot(act_ref[...], wd_ref[...])
    out_ref[...] = _rms(x1 + acc, gf_ref[...])


def _ffn_call(x2, oh, on, woh, won, g2, wg, wu, wd, cw, gf, tiles_per_seq):
    n = x2.shape[0]
    tm = FFN_TM
    row = lambda w: pl.BlockSpec((tm, w), lambda i: (i, 0))
    full = lambda a: pl.BlockSpec(a.shape, lambda i: (0,) * a.ndim, pipeline_mode=pl.Buffered(1))
    return pl.pallas_call(
        functools.partial(_ffn_body, tiles_per_seq=tiles_per_seq),
        grid=(n // tm,),
        in_specs=[row(D_MODEL), row(HG_WIDTH), row(NSA_WIDTH), full(woh), full(won), full(g2),
                  full(wg), full(wu), full(wd), full(cw), full(gf)],
        out_specs=row(D_MODEL),
        out_shape=jax.ShapeDtypeStruct((n, D_MODEL), F32),
        scratch_shapes=[pltpu.VMEM((8, D_FF), F32), pltpu.VMEM((tm, D_FF), BF16)],
        compiler_params=pltpu.CompilerParams(dimension_semantics=("arbitrary",),
                                             vmem_limit_bytes=VMEM_LIMIT),
        name="outproj_convffn",
    )(x2, oh, on, woh, won, g2, wg, wu, wd, cw, gf)


def _rope_angles(positions):
    inv_freq = ROPE_THETA ** (-jnp.arange(ROPE_HALF, dtype=F32) * 2.0 / ROPE_DIM)
    ang = positions.astype(F32)[..., None] * inv_freq
    return jnp.concatenate([jnp.cos(ang), jnp.sin(ang)], axis=-1).transpose(0, 2, 1)


def _layer(x, positions, ln1, w_in, lb, hg_gain, pe_k, pe_v, k_w1, k_w2, v_w1, v_w2, nsa_gain, w_o, ln2,
           w_gate, w_up, conv_w, conv_b, w_down, final_gain):
    b, t, d = x.shape
    n = b * t
    assert d == D_MODEL and t % FFN_TM == 0 and t % PROJ_TM == 0 and t % ATT_TQ == 0 and t % HG_TT == 0
    n_grp = t // CMP_STRIDE
    assert n_grp == LANES, "compressed-block axis is laid out on exactly one lane tile"
    n_sel = t // SLC_BLOCK
    assert n_sel % 8 == 0 and n_sel <= LANES and ATT_TK % SLC_BLOCK == 0
    x2 = x.reshape(n, d)

    splits = np.cumsum([0, 4 * HG_WIDTH, NSA_WIDTH] + [KV_WIDTH] * 6 + [N_GATES])
    seg = lambda i: w_in[:, splits[i]:splits[i + 1]]
    wh = seg(0).astype(BF16)
    wk = jnp.concatenate([seg(2), seg(3), seg(4), seg(6)], axis=1).astype(BF16)
    wgate = seg(8).reshape(d, 3, NSA_KV_HEADS, HPG).transpose(0, 2, 1, 3).reshape(d, NSA_KV_HEADS, 3 * HPG)
    wgate = jnp.pad(wgate, ((0, 0), (0, 0), (0, LANES - 3 * HPG))).reshape(d, NSA_KV_HEADS * LANES)
    wt = jnp.concatenate([seg(1), seg(5), seg(7), wgate], axis=1).T.astype(BF16)
    cs = _rope_angles(positions)

    hg, kcn, vcn, ksw, qt, vt, gt = _inproj_call(x2, ln1.reshape(1, d), wh, wk, wt, cs, t // PROJ_TM)

    mst, lvl = _hgrn_tables()
    o_hg = _hgrn_call(hg.reshape(b, t, 4 * HG_WIDTH), lb.reshape(1, HG_WIDTH).astype(F32),
                      hg_gain.reshape(1, HG_WIDTH), mst, lvl)

    per_lane = lambda a: jnp.broadcast_to(a.reshape(2, CMP_STRIDE, 1, NSA_HEAD_DIM, -1),
                                          (2, CMP_STRIDE, NSA_KV_HEADS, NSA_HEAD_DIM, a.shape[-1]))
    w1_rows = lambda w1: per_lane(w1).reshape(2, CMP_STRIDE * LANES, CMP_HIDDEN).astype(BF16)
    pe_rows = lambda pe: per_lane(pe[..., None]).reshape(2, 1, CMP_STRIDE * LANES)
    zeros_w2 = jnp.zeros((CMP_HIDDEN, NSA_HEAD_DIM), F32)
    place = lambda w2: jnp.stack([jnp.concatenate([w2, zeros_w2], 1), jnp.concatenate([zeros_w2, w2], 1)])
    kc, vct = _cmp_call(kcn.reshape(b, t, KV_WIDTH), vcn.reshape(b, t, KV_WIDTH), pe_rows(pe_k), pe_rows(pe_v),
                        w1_rows(k_w1), w1_rows(v_w1),
                        place(k_w2).astype(BF16), place(v_w2).transpose(0, 2, 1).astype(BF16))

    cmp_start = np.arange(n_grp) * CMP_STRIDE
    cmp_end = cmp_start + CMP_BLOCK - 1
    sel_start = np.arange(LANES) * SLC_BLOCK
    overlap = ((cmp_start[:, None] <= sel_start[None, :] + SLC_BLOCK - 1) & (cmp_end[:, None] >= sel_start[None, :])
               & (np.arange(LANES)[None, :] < n_sel) & (np.arange(n_grp)[:, None] < n_grp - 1))
    ovt = jnp.asarray(overlap.T.astype(np.float32), BF16)
    o_nsa = _nsa_call(qt, ksw.reshape(b, t, 2 * KV_WIDTH), vt, kc, vct, gt, nsa_gain.reshape(1, NSA_WIDTH), ovt)

    cw = jnp.concatenate([conv_w, conv_b[None, :], jnp.zeros((4, D_FF), F32)], axis=0)
    out = _ffn_call(x2, o_hg.reshape(n, HG_WIDTH), o_nsa.reshape(n, NSA_WIDTH),
                    w_o[:HG_WIDTH].astype(BF16), w_o[HG_WIDTH:].astype(BF16), ln2.reshape(1, d),
                    w_gate.astype(BF16), w_up.astype(BF16), w_down.astype(BF16), cw,
                    final_gain.reshape(1, d), t // FFN_TM)
    return out.reshape(b, t, d)


def kernel(x, positions, ln1_gain, w_in, hgrn_lb_param, hgrn_out_gain, cmp_pe_k, cmp_pe_v, cmp_k_w1, cmp_k_w2,
           cmp_v_w1, cmp_v_w2, nsa_out_gain, w_o, ln2_gain, ffn_w_gate, ffn_w_up, ffn_conv_w, ffn_conv_b,
           ffn_w_down, final_gain):
    depth = ln1_gain.shape[0]
    assert depth == 1, "the fused final norm assumes a single layer"
    lower_bounds = jnp.cumsum(jax.nn.softmax(hgrn_lb_param.astype(F32), axis=0), axis=0)
    l = 0
    return _layer(x, positions, ln1_gain[l], w_in[l], lower_bounds[l], hgrn_out_gain[l], cmp_pe_k[l], cmp_pe_v[l],
                  cmp_k_w1[l], cmp_k_w2[l], cmp_v_w1[l], cmp_v_w2[l], nsa_out_gain[l], w_o[l], ln2_gain[l],
                  ffn_w_gate[l], ffn_w_up[l], ffn_conv_w[l], ffn_conv_b[l], ffn_w_down[l], final_gain)
```

```python
import functools

import jax
import jax.numpy as jnp
import numpy as np
from jax import lax
from jax.experimental import pallas as pl
from jax.experimental.pallas import tpu as pltpu

F32 = jnp.float32
BF16 = jnp.bfloat16

D_MODEL = 1024
HG_HEADS = 4
HG_DK = 128
HG_DV = 128
HG_WIDTH = HG_HEADS * HG_DV
NSA_HEADS = 8
NSA_KV_HEADS = 2
NSA_HEAD_DIM = 64
HPG = NSA_HEADS // NSA_KV_HEADS
NSA_WIDTH = NSA_HEADS * NSA_HEAD_DIM
KV_WIDTH = NSA_KV_HEADS * NSA_HEAD_DIM
CMP_BLOCK = 32
CMP_STRIDE = 16
CMP_HIDDEN = 256
SLC_BLOCK = 64
SLC_TOPK = 16
WINDOW = 512
ROPE_THETA = 500000.0
ROPE_DIM = NSA_HEAD_DIM // 4
ROPE_HALF = ROPE_DIM // 2
D_FF = 2816
EPS = 1e-6
NEG_INF = -1e30
FORCE_SCORE = 1e4
N_GATES = 3 * NSA_HEADS
LOG2_E = 1.4426950408889634

LANES = 128
VMEM_LIMIT = 56 * 1024 * 1024

PROJ_TM = 512
HG_CHUNK = 128
HG_LEVELS = (16, 32, 64)
HG_DIAG = 16
HG_TT = 512
ATT_TQ = 256
ATT_TK = 256
SUM_ROWS = 16
FFN_TM = 512
FFN_TC = 256
FFN_NC = D_FF // FFN_TC


def _dot(a, b):
    return jnp.dot(a, b, preferred_element_type=F32)


def _dot_nt(a, b):
    return lax.dot_general(a, b, (((1,), (1,)), ((), ())), preferred_element_type=F32)


def _dot_tn(a, b):
    return lax.dot_general(a, b, (((0,), (0,)), ((), ())), preferred_element_type=F32)


def _split3(x):
    hi = x.astype(BF16)
    r = x - hi.astype(F32)
    mid = r.astype(BF16)
    lo = (r - mid.astype(F32)).astype(BF16)
    return hi, mid, lo


def _rms(x, gain):
    return x * lax.rsqrt(jnp.mean(x * x, axis=-1, keepdims=True) + EPS) * gain


def _inproj_body(x_ref, g_ref, wh_ref, wk_ref, wt_ref, cs_ref,
                 hg_ref, kcn_ref, vcn_ref, ksw_ref, qt_ref, vt_ref, gt_ref):
    hb = _rms(x_ref[...], g_ref[...]).astype(BF16)
    hg_ref[...] = _dot(hb, wh_ref[...])

    def rope(v, axis, cos, sin_hi, sin_lo):
        return (v * cos + pltpu.roll(v, ROPE_HALF, axis) * sin_hi
                + pltpu.roll(v, LANES - ROPE_HALF, axis) * sin_lo)

    cos = cs_ref[0, 0:ROPE_HALF, :]
    sin = cs_ref[0, ROPE_HALF:ROPE_DIM, :]
    tm = cos.shape[1]
    zero_h = jnp.zeros((ROPE_HALF, tm), F32)
    rest = NSA_HEAD_DIM - ROPE_DIM
    slab = lambda lo, hi, fill: jnp.concatenate([lo, hi, jnp.full((rest, tm), fill, F32)] * (LANES // NSA_HEAD_DIM), axis=0)
    tab_t = (slab(cos, cos, 1.0), slab(zero_h, sin, 0.0), slab(-sin, zero_h, 0.0))
    tab = tuple(a.T for a in tab_t)
    kn = _dot(hb, wk_ref[...])
    kcn_ref[...] = rope(kn[:, 0:LANES], 1, *tab)
    vcn_ref[...] = kn[:, LANES:2 * LANES]
    ksw_ref[:, 0:LANES] = rope(kn[:, 2 * LANES:3 * LANES], 1, *tab).astype(BF16)
    ksw_ref[:, LANES:2 * LANES] = rope(kn[:, 3 * LANES:4 * LANES], 1, *tab).astype(BF16)

    rt = _dot_nt(wt_ref[...], hb)
    scale = NSA_HEAD_DIM ** -0.5 * LOG2_E
    for j in range(NSA_WIDTH // LANES):
        sl = slice(j * LANES, (j + 1) * LANES)
        qt_ref[0, sl, :] = (rope(rt[sl], 0, *tab_t) * scale).astype(BF16)
    vt_ref[0] = rt[NSA_WIDTH:NSA_WIDTH + 2 * KV_WIDTH].astype(BF16)
    gt_ref[0] = jax.nn.sigmoid(rt[NSA_WIDTH + 2 * KV_WIDTH:])


def _inproj_call(x2, gain, wh, wk, wt, cs, tiles_per_seq):
    n = x2.shape[0]
    tm = PROJ_TM
    t = tiles_per_seq * tm
    b = n // t
    row = lambda w: pl.BlockSpec((tm, w), lambda i: (i, 0))
    col = lambda h: pl.BlockSpec((1, h, tm), lambda i: (i // tiles_per_seq, 0, i % tiles_per_seq))
    full = lambda a: pl.BlockSpec(a.shape, lambda i: (0, 0))
    gate_rows = NSA_KV_HEADS * LANES
    return pl.pallas_call(
        _inproj_body,
        grid=(n // tm,),
        in_specs=[row(D_MODEL), full(gain), full(wh), full(wk), full(wt),
                  col(ROPE_DIM)],
        out_specs=[row(4 * HG_WIDTH), row(KV_WIDTH), row(KV_WIDTH), row(2 * KV_WIDTH),
                   col(NSA_WIDTH), col(2 * KV_WIDTH), col(gate_rows)],
        out_shape=[jax.ShapeDtypeStruct((n, 4 * HG_WIDTH), F32),
                   jax.ShapeDtypeStruct((n, KV_WIDTH), F32),
                   jax.ShapeDtypeStruct((n, KV_WIDTH), F32),
                   jax.ShapeDtypeStruct((n, 2 * KV_WIDTH), BF16),
                   jax.ShapeDtypeStruct((b, NSA_WIDTH, t), BF16),
                   jax.ShapeDtypeStruct((b, 2 * KV_WIDTH, t), BF16),
                   jax.ShapeDtypeStruct((b, gate_rows, t), F32)],
        compiler_params=pltpu.CompilerParams(dimension_semantics=("arbitrary",),
                                             vmem_limit_bytes=VMEM_LIMIT),
        name="inproj",
    )(x2, gain, wh, wk, wt, cs)


def _hgrn_tables():
    L = HG_CHUNK
    t = np.arange(L)[:, None]
    u = np.arange(L)[None, :]
    level = np.where(((t // HG_DIAG) == (u // HG_DIAG)) & (u <= t), 1, 0)
    for li, s in enumerate(HG_LEVELS):
        same = (t // (2 * s)) == (u // (2 * s))
        right = (t % (2 * s)) >= s
        level = np.where(same & right & ((u % (2 * s)) < s), li + 2, level)
    return jnp.asarray((u <= t).astype(np.float32), BF16), jnp.asarray(level, jnp.int32)


def _hgrn_body(q_ref, f_ref, i_ref, g_ref, lb_ref, gain_ref, mst_ref, lvl_ref, o_ref, st_ref):
    L = HG_CHUNK
    n_chunks = q_ref.shape[1] // L

    @pl.when(pl.program_id(1) == 0)
    def _sequence_start():
        st_ref[...] = jnp.zeros_like(st_ref)

    def chunk(c, carry):
        rows = pl.ds(pl.multiple_of(c * L, L), L)
        heads = range(HG_HEADS)
        cols = [slice(h * HG_DK, (h + 1) * HG_DK) for h in heads]
        mst = mst_ref[...]
        lvl = lvl_ref[...]
        n_lv = len(HG_LEVELS)
        row_i = lax.broadcasted_iota(jnp.int32, (L, HG_DK), 0)
        q = [q_ref[0, rows, cols[h]] for h in heads]
        vb = [i_ref[0, rows, cols[h]].astype(BF16) for h in heads]
        f = [lb_ref[:, cols[h]] + (1.0 - lb_ref[:, cols[h]]) * jax.nn.sigmoid(f_ref[0, rows, cols[h]]) for h in heads]
        k = [1.0 - f[h] for h in heads]
        parts = [_split3(jnp.log(f[h])) for h in heads]
        e_full = [(_dot(mst, parts[h][0]) + _dot(mst, parts[h][1])) + _dot(mst, parts[h][2]) for h in heads]
        b_last = [e_full[h][L - 1:L, :] for h in heads]

        def rel_to(b, blk, off):
            refs = []
            for r0 in range(0, L, blk):
                r = r0 + off - 1
                ref = b[r:r + 1, :] if r >= 0 else jnp.zeros((1, HG_DK), F32)
                refs.append(jnp.broadcast_to(ref, (blk, HG_DK)))
            return b - jnp.concatenate(refs, axis=0)

        def level_sums(b):
            out = [rel_to(b, HG_DIAG, 0)]
            for s_half in HG_LEVELS:
                d = rel_to(b, 2 * s_half, s_half)
                out.append(jnp.where((row_i % (2 * s_half)) >= s_half, d, -d))
            return out

        e = [level_sums(e_full[h]) for h in heads]
        wq = [[jnp.exp(e[h][l]) for l in range(n_lv + 1)] for h in heads]
        wk = [[jnp.exp(-e[h][0])] + wq[h][1:] for h in heads]
        prod = [[_dot_nt((q[h] * wq[h][l]).astype(BF16), (k[h] * wk[h][l]).astype(BF16)) for l in range(n_lv + 1)]
                for h in heads]
        st = [st_ref[h] for h in heads]
        inter = [_dot_nt((q[h] * jnp.exp(e_full[h])).astype(BF16), st[h].astype(BF16)) for h in heads]
        k_dec = [(k[h] * jnp.exp(b_last[h] - e_full[h])).astype(BF16) for h in heads]
        upd = [_dot_tn(vb[h], k_dec[h]) for h in heads]
        for h in heads:
            st_ref[h] = st[h] * jnp.exp(b_last[h]) + upd[h]
        a = []
        for h in heads:
            ah = jnp.where(lvl == 1, prod[h][0], 0.0)
            for l in range(1, n_lv + 1):
                ah = jnp.where(lvl == l + 1, prod[h][l], ah)
            a.append(ah.astype(BF16))
        o = [_dot(a[h], vb[h]) + inter[h] for h in heads]
        for h in heads:
            oh = o[h] * lax.rsqrt(jnp.mean(o[h] * o[h], axis=-1, keepdims=True) + EPS) * gain_ref[:, cols[h]]
            o_ref[0, rows, cols[h]] = (oh * jax.nn.silu(g_ref[0, rows, cols[h]])).astype(o_ref.dtype)
        return carry

    lax.fori_loop(0, n_chunks, chunk, 0, unroll=2)


def _hgrn_call(hg, lb, gain, mst, lvl):
    b, t, _ = hg.shape
    tt = HG_TT
    col = lambda k: pl.BlockSpec((1, tt, HG_WIDTH), lambda bi, ti: (bi, ti, k))
    full = lambda a: pl.BlockSpec(a.shape, lambda bi, ti: (0, 0))
    return pl.pallas_call(
        _hgrn_body,
        grid=(b, t // tt),
        in_specs=[col(0), col(1), col(2), col(3), full(lb), full(gain), full(mst), full(lvl)],
        out_specs=pl.BlockSpec((1, tt, HG_WIDTH), lambda bi, ti: (bi, ti, 0)),
        out_shape=jax.ShapeDtypeStruct((b, t, HG_WIDTH), BF16),
        scratch_shapes=[pltpu.VMEM((HG_HEADS, HG_DV, HG_DK), F32)],
        compiler_params=pltpu.CompilerParams(dimension_semantics=("arbitrary", "arbitrary"),
                                             vmem_limit_bytes=VMEM_LIMIT),
        name="hgrn2",
    )(hg, hg, hg, hg, lb, gain, mst, lvl)


def _cmp_body(kcn_ref, vcn_ref, pek_ref, pev_ref, w1k_ref, w1v_ref, w2k_ref, w2v_ref, kc_ref, vc_ref):
    nb = kcn_ref.shape[1] // CMP_STRIDE
    lane_grp = (lax.broadcasted_iota(jnp.int32, (nb, CMP_STRIDE * LANES), 1) // NSA_HEAD_DIM) % NSA_KV_HEADS

    def hidden(src_ref, pe_ref, w1_ref):
        x = jnp.concatenate([src_ref[0, pl.ds(l, nb, stride=CMP_STRIDE), :]
                             for l in range(CMP_STRIDE)], axis=1)
        halves = [x + pe_ref[i] for i in range(2)]
        out = []
        for g in range(NSA_KV_HEADS):
            u, v = (_dot(jnp.where(lane_grp == g, halves[i], 0.0).astype(BF16), w1_ref[i]) for i in range(2))
            out.append(jax.nn.silu(u + pltpu.roll(v, nb - 1, 0)).astype(BF16))
        return out

    hk = hidden(kcn_ref, pek_ref, w1k_ref)
    hv = hidden(vcn_ref, pev_ref, w1v_ref)
    for g in range(NSA_KV_HEADS):
        kc_ref[0, g, 0:nb, :] = _dot(hk[g], w2k_ref[0]).astype(kc_ref.dtype)
        kc_ref[0, g, nb:2 * nb, :] = _dot(hk[g], w2k_ref[1]).astype(kc_ref.dtype)
        vc_ref[0, g, :, 0:nb] = _dot_nt(w2v_ref[0], hv[g]).astype(vc_ref.dtype)
        vc_ref[0, g, :, nb:2 * nb] = _dot_nt(w2v_ref[1], hv[g]).astype(vc_ref.dtype)


def _cmp_call(kcn, vcn, pek, pev, w1k, w1v, w2k, w2v):
    b, t, w = kcn.shape
    nb = t // CMP_STRIDE
    full = lambda a: pl.BlockSpec(a.shape, lambda bi: (0,) * a.ndim)
    out = lambda r, c: pl.BlockSpec((1, NSA_KV_HEADS, r, c), lambda bi: (bi, 0, 0, 0))
    return pl.pallas_call(
        _cmp_body,
        grid=(b,),
        in_specs=[pl.BlockSpec((1, t, w), lambda bi: (bi, 0, 0)), pl.BlockSpec((1, t, w), lambda bi: (bi, 0, 0)),
                  full(pek), full(pev), full(w1k), full(w1v), full(w2k), full(w2v)],
        out_specs=[out(2 * nb, LANES), out(LANES, 2 * nb)],
        out_shape=[jax.ShapeDtypeStruct((b, NSA_KV_HEADS, 2 * nb, LANES), BF16),
                   jax.ShapeDtypeStruct((b, NSA_KV_HEADS, LANES, 2 * nb), BF16)],
        compiler_params=pltpu.CompilerParams(dimension_semantics=("arbitrary",),
                                             vmem_limit_bytes=VMEM_LIMIT),
        name="nsa_compress",
    )(kcn, vcn, pek, pev, w1k, w1v, w2k, w2v)


def _nsa_body(qt_ref, ksw_ref, vst_ref, vwt_ref, kc_ref, vct_ref, gt_ref, gain_ref, ovt_ref, o_ref,
              ks_ref, kw_ref, vs_ref, vw_ref, m_ref, acc_ref, s_ref, ch_ref):
    g = pl.program_id(1)
    qi = pl.program_id(2)
    tq = ATT_TQ
    tk = ATT_TK
    t_len = ksw_ref.shape[1]
    n_kt = t_len // tk
    n_pairs = HPG // 2
    hd = NSA_HEAD_DIM

    @pl.when(qi == 0)
    def _build_kv():
        lane = lax.broadcasted_iota(jnp.int32, (tk, LANES), 1)
        lo_lane = lane < hd
        keep = (lane // hd) == g

        def build_k(src_col, dst_ref):
            def body(j, carry):
                rows = pl.ds(pl.multiple_of(j * tk, tk), tk)
                x = ksw_ref[0, rows, src_col * LANES:(src_col + 1) * LANES].astype(F32)
                dup = jnp.where(keep, x, pltpu.roll(x, hd, 1))
                dst_ref[j, 0:tk, :] = jnp.where(lo_lane, dup, 0.0).astype(BF16)
                dst_ref[j, tk:2 * tk, :] = jnp.where(lo_lane, 0.0, dup).astype(BF16)
                return carry
            lax.fori_loop(0, n_kt, body, 0)

        def build_v(src_ref, dst_ref):
            zero = jnp.zeros((hd, tk), BF16)
            row = lax.broadcasted_iota(jnp.int32, (SUM_ROWS, 2 * tk), 0)
            col = lax.broadcasted_iota(jnp.int32, (SUM_ROWS, 2 * tk), 1)
            ones_rows = jnp.where(((row == 0) & (col < tk)) | ((row == 1) & (col >= tk)), 1.0, 0.0).astype(BF16)
            for j in range(n_kt):
                x = src_ref[0, :, j * tk:(j + 1) * tk]
                dst_ref[j, 0:hd, 0:tk] = x
                dst_ref[j, 0:hd, tk:2 * tk] = zero
                dst_ref[j, hd:2 * hd, 0:tk] = zero
                dst_ref[j, hd:2 * hd, tk:2 * tk] = x
                dst_ref[j, 2 * hd:2 * hd + SUM_ROWS, :] = ones_rows

        build_k(0, ks_ref)
        build_k(1, kw_ref)
        build_v(vst_ref, vs_ref)
        build_v(vwt_ref, vw_ref)

    t0 = qi * tq
    key_i = lax.broadcasted_iota(jnp.int32, (tk, tq), 0)
    qry_t = t0 + lax.broadcasted_iota(jnp.int32, (tk, tq), 1)
    slab_lo = lax.broadcasted_iota(jnp.int32, (LANES, tq), 0) < hd
    acc_row = lax.broadcasted_iota(jnp.int32, (LANES + SUM_ROWS, tq), 0)
    slab_a = (acc_row < hd) | (acc_row == LANES)
    q_pairs = [qt_ref[0, p * LANES:(p + 1) * LANES, :] for p in range(n_pairs)]

    last = (t0 + tq - 1) // tk

    def scores(k_ref, j):
        kt = k_ref[j]
        return [_dot(kt, q_pairs[p]) for p in range(n_pairs)]

    n_cmp_pad = kc_ref.shape[2] // 2
    blk_i = lax.broadcasted_iota(jnp.int32, (n_cmp_pad, tq), 0)
    blk_t = t0 + lax.broadcasted_iota(jnp.int32, (n_cmp_pad, tq), 1)
    cmp_ok = (blk_i * CMP_STRIDE + (CMP_BLOCK - 1)) <= blk_t
    kc = kc_ref[0, 0]
    vct = vct_ref[0, 0]
    s_cmp = [_dot(kc, q_pairs[p]) for p in range(n_pairs)]
    n_win = (WINDOW + tq) // tk
    win_tiles = [last - (n_win - 1) + k for k in range(n_win)]
    for slot, (k_ref, j0) in enumerate([(ks_ref, 0)] + [(kw_ref, jnp.maximum(jw, 0)) for jw in win_tiles]):
        s_first = scores(k_ref, j0)
        for p in range(n_pairs):
            s_ref[slot, p] = s_first[p]
    p_sum = jnp.zeros((n_cmp_pad, tq), F32)
    p_cmp = []
    for p in range(n_pairs):
        probs = []
        for h in range(2):
            sh = jnp.where(cmp_ok, s_cmp[p][h * n_cmp_pad:(h + 1) * n_cmp_pad], NEG_INF)
            mh = jnp.max(sh, axis=0, keepdims=True)
            eh = jnp.where(cmp_ok, jnp.exp2(sh - mh), 0.0)
            den = jnp.sum(eh, axis=0, keepdims=True)
            ph = eh / jnp.where(den > 0.0, den, 1.0)
            p_sum = p_sum + ph
            probs.append(ph.astype(BF16))
        p_cmp.append(jnp.concatenate(probs, axis=0))
    o_cmp = [_dot(vct, p_cmp[p]) for p in range(n_pairs)]

    n_sel = t_len // SLC_BLOCK
    hi, mid, lo = _split3(p_sum)
    ovt = ovt_ref[...]
    p_sel = ((_dot(ovt, hi) + _dot(ovt, mid)) + _dot(ovt, lo))[0:n_sel]
    sel_i = lax.broadcasted_iota(jnp.int32, (n_sel, tq), 0)
    cur = (t0 + lax.broadcasted_iota(jnp.int32, (n_sel, tq), 1)) // SLC_BLOCK
    forced = (sel_i == 0) | (sel_i == cur) | (sel_i == cur - 1)
    score = jnp.where(forced, FORCE_SCORE, p_sel)
    score = jnp.where(sel_i <= cur, score, -jnp.inf)
    rank = jnp.zeros((n_sel, tq), jnp.int32)
    row_grp = 8
    grp_i = lax.broadcasted_iota(jnp.int32, (row_grp, tq), 0)
    for i in range(n_sel):
        ci = score[i:i + 1, :]
        ahead = []
        for r0 in range(0, n_sel, row_grp):
            rows = slice(r0, r0 + row_grp)
            if r0 > i:
                ahead.append(ci >= score[rows])
            elif r0 + row_grp <= i:
                ahead.append(ci > score[rows])
            else:
                ahead.append((ci > score[rows]) | ((ci == score[rows]) & (grp_i > i - r0)))
        rank = rank + jnp.where(jnp.concatenate(ahead, axis=0), 1, 0)
    chosen = jnp.where(rank < min(SLC_TOPK, n_sel), 1.0, 0.0)
    for i in range(n_sel):
        ch_ref[i] = chosen[i:i + 1, :]

    def tile_softmax(slot, bias, m_get, m_put):
        alphas = {}
        probs = {}
        for hh in range(HPG):
            p, h = divmod(hh, 2)
            a_parts = []
            p_parts = []
            for qh in range(tq // LANES):
                ql = slice(qh * LANES, (qh + 1) * LANES)
                sh = s_ref[slot, p, h * tk:(h + 1) * tk, ql] + bias[:, ql]
                m_prev = m_get(hh, qh)
                m_new = jnp.maximum(m_prev, jnp.max(sh, axis=0, keepdims=True))
                m_put(hh, qh, m_new)
                p_parts.append(jnp.exp2(sh - m_new).astype(BF16))
                a_parts.append(jnp.exp2(m_prev - m_new))
            alphas[hh] = jnp.concatenate(a_parts, axis=1)
            probs[hh] = jnp.concatenate(p_parts, axis=1)
        return ([jnp.concatenate([probs[2 * p], probs[2 * p + 1]], axis=0) for p in range(n_pairs)],
                [jnp.where(slab_a, alphas[2 * p], alphas[2 * p + 1]) for p in range(n_pairs)])

    def normalised(acc):
        inv = jnp.where(slab_lo, 1.0 / acc[LANES:LANES + 1, :], 1.0 / acc[LANES + 1:LANES + 2, :])
        return acc[0:LANES] * inv

    m_win = {}
    acc_win = [jnp.zeros((LANES + SUM_ROWS, tq), F32) for _ in range(n_pairs)]
    for k, jw in enumerate(win_tiles):
        key_t = jw * tk + key_i
        rel = qry_t - key_t
        bias = jnp.where((key_t >= 0) & (rel >= 0) & (rel < WINDOW), 0.0, NEG_INF)
        probs, a_rows = tile_softmax(1 + k, bias,
                                     lambda hh, qh: m_win.get((hh, qh), jnp.full((1, LANES), -jnp.inf, F32)),
                                     lambda hh, qh, v: m_win.__setitem__((hh, qh), v))
        vt = vw_ref[jnp.maximum(jw, 0)]
        acc_win = [acc_win[p] * a_rows[p] + _dot(vt, probs[p]) for p in range(n_pairs)]
    o_win = [normalised(acc_win[p]) for p in range(n_pairs)]

    m_ref[...] = jnp.full_like(m_ref, -jnp.inf)
    acc_ref[...] = jnp.zeros_like(acc_ref)

    def m_put(hh, qh, v):
        m_ref[hh, :, qh * LANES:(qh + 1) * LANES] = v

    def slc_step(j, carry):
        s_next = scores(ks_ref, jnp.minimum(j + 1, last))
        per_tile = tk // SLC_BLOCK
        picked = jnp.concatenate([jnp.broadcast_to(ch_ref[j * per_tile + i], (SLC_BLOCK, tq)) for i in range(per_tile)],
                                 axis=0) > 0.5
        bias = jnp.where(picked & ((j * tk + key_i) <= qry_t), 0.0, NEG_INF)
        probs, a_rows = tile_softmax(0, bias, lambda hh, qh: m_ref[hh, :, qh * LANES:(qh + 1) * LANES], m_put)
        vt = vs_ref[j]
        for p in range(n_pairs):
            pv = _dot(vt, probs[p])
            s_ref[0, p] = s_next[p]
            acc_ref[p] = acc_ref[p] * a_rows[p] + pv
        return carry

    lax.fori_loop(0, last + 1, slc_step, 0)
    o_slc = [normalised(acc_ref[p]) for p in range(n_pairs)]


    gates = gt_ref[0]
    gain = gain_ref[...]
    for p in range(n_pairs):
        o = jnp.zeros((LANES, tq), F32)
        for c, branch in enumerate((o_cmp[p], o_slc[p], o_win[p])):
            r = c * HPG + 2 * p
            o = o + jnp.where(slab_lo, gates[r:r + 1, :], gates[r + 1:r + 2, :]) * branch
        sq = o * o
        ms_a = jnp.sum(sq[0:hd], axis=0, keepdims=True)
        ms_b = jnp.sum(sq[hd:2 * hd], axis=0, keepdims=True)
        ms = jnp.where(slab_lo, ms_a, ms_b) * (1.0 / hd)
        o = o * lax.rsqrt(ms + EPS)
        o_ref[0, :, p * LANES:(p + 1) * LANES] = (o.T * gain[:, p * LANES:(p + 1) * LANES]).astype(o_ref.dtype)


def _nsa_call(qt, ksw, vt, kc, vct, gt, gain, ovt):
    b, _, t = qt.shape
    tq, tk = ATT_TQ, ATT_TK
    n_kt = t // tk
    gw = HPG * NSA_HEAD_DIM
    hd = NSA_HEAD_DIM
    k_scratch = pltpu.VMEM((n_kt, 2 * tk, LANES), BF16)
    v_scratch = pltpu.VMEM((n_kt, LANES + SUM_ROWS, 2 * tk), BF16)
    return pl.pallas_call(
        _nsa_body,
        grid=(b, NSA_KV_HEADS, t // tq),
        in_specs=[
            pl.BlockSpec((1, gw, tq), lambda bi, gi, qi: (bi, gi, qi)),
            pl.BlockSpec((1, t, 2 * KV_WIDTH), lambda bi, gi, qi: (bi, 0, 0)),
            pl.BlockSpec((1, hd, t), lambda bi, gi, qi: (bi, gi, 0)),
            pl.BlockSpec((1, hd, t), lambda bi, gi, qi: (bi, NSA_KV_HEADS + gi, 0)),
            pl.BlockSpec((1, 1) + kc.shape[2:], lambda bi, gi, qi: (bi, gi, 0, 0)),
            pl.BlockSpec((1, 1) + vct.shape[2:], lambda bi, gi, qi: (bi, gi, 0, 0)),
            pl.BlockSpec((1, LANES, tq), lambda bi, gi, qi: (bi, gi, qi)),
            pl.BlockSpec((1, gw), lambda bi, gi, qi: (0, gi)),
            pl.BlockSpec(ovt.shape, lambda bi, gi, qi: (0, 0)),
        ],
        out_specs=pl.BlockSpec((1, tq, gw), lambda bi, gi, qi: (bi, qi, gi)),
        out_shape=jax.ShapeDtypeStruct((b, t, NSA_WIDTH), BF16),
        scratch_shapes=[k_scratch, k_scratch, v_scratch, v_scratch,
                        pltpu.VMEM((HPG, 1, tq), F32),
                        pltpu.VMEM((HPG // 2, LANES + SUM_ROWS, tq), F32), pltpu.VMEM((1 + (WINDOW + tq) // tk, HPG // 2, 2 * tk, tq), F32),
                        pltpu.VMEM((t // SLC_BLOCK, 1, tq), F32)],
        compiler_params=pltpu.CompilerParams(dimension_semantics=("arbitrary", "arbitrary", "arbitrary"),
                                             vmem_limit_bytes=VMEM_LIMIT),
        name="nsa_attention",
    )(qt, ksw, vt, vt, kc, vct, gt, gain, ovt)


def _ffn_body(x_ref, oh_ref, on_ref, woh_ref, won_ref, g2_ref, wg_ref, wu_ref, wd_ref, cw_ref, gf_ref,
              out_ref, halo_ref, act_ref, *, tiles_per_seq):
    tm = x_ref.shape[0]
    x1 = x_ref[...] + _dot(oh_ref[...], woh_ref[...]) + _dot(on_ref[...], won_ref[...])
    hb = _rms(x1, g2_ref[...]).astype(BF16)
    row = lax.broadcasted_iota(jnp.int32, (tm, FFN_TC), 0)

    @pl.when((pl.program_id(0) % tiles_per_seq) == 0)
    def _sequence_start():
        halo_ref[...] = jnp.zeros_like(halo_ref)

    def activation(c, gate, up):
        cols = slice(c * FFN_TC, (c + 1) * FFN_TC)
        halo = halo_ref[:, cols]
        halo_ref[:, cols] = gate[tm - 8:tm, :]
        prev1 = jnp.where(row == 0, halo[7:8, :], pltpu.roll(gate, 1, 0))
        prev2 = jnp.where(row == 0, halo[6:7, :], jnp.where(row == 1, halo[7:8, :], pltpu.roll(gate, 2, 0)))
        cw = cw_ref[:, cols]
        y = cw[0:1, :] * prev2 + cw[1:2, :] * prev1 + cw[2:3, :] * gate + cw[3:4, :]
        return (jax.nn.silu(y) * up).astype(BF16)

    chunk = lambda w_ref, c: _dot(hb, w_ref[:, c * FFN_TC:(c + 1) * FFN_TC])
    gate_up = (chunk(wg_ref, 0), chunk(wu_ref, 0))
    for c in range(FFN_NC):
        cur = gate_up
        if c + 1 < FFN_NC:
            gate_up = (chunk(wg_ref, c + 1), chunk(wu_ref, c + 1))
        act_ref[:, c * FFN_TC:(c + 1) * FFN_TC] = activation(c, *cur)
    acc = _dot(act_ref[...], wd_ref[...])
    out_ref[...] = _rms(x1 + acc, gf_ref[...])


def _ffn_call(x2, oh, on, woh, won, g2, wg, wu, wd, cw, gf, tiles_per_seq):
    n = x2.shape[0]
    tm = FFN_TM
    row = lambda w: pl.BlockSpec((tm, w), lambda i: (i, 0))
    full = lambda a: pl.BlockSpec(a.shape, lambda i: (0,) * a.ndim, pipeline_mode=pl.Buffered(1))
    return pl.pallas_call(
        functools.partial(_ffn_body, tiles_per_seq=tiles_per_seq),
        grid=(n // tm,),
        in_specs=[row(D_MODEL), row(HG_WIDTH), row(NSA_WIDTH), full(woh), full(won), full(g2),
                  full(wg), full(wu), full(wd), full(cw), full(gf)],
        out_specs=row(D_MODEL),
        out_shape=jax.ShapeDtypeStruct((n, D_MODEL), F32),
        scratch_shapes=[pltpu.VMEM((8, D_FF), F32), pltpu.VMEM((tm, D_FF), BF16)],
        compiler_params=pltpu.CompilerParams(dimension_semantics=("arbitrary",),
                                             vmem_limit_bytes=VMEM_LIMIT),
        name="outproj_convffn",
    )(x2, oh, on, woh, won, g2, wg, wu, wd, cw, gf)


def _rope_angles(positions):
    inv_freq = ROPE_THETA ** (-jnp.arange(ROPE_HALF, dtype=F32) * 2.0 / ROPE_DIM)
    ang = positions.astype(F32)[..., None] * inv_freq
    return jnp.concatenate([jnp.cos(ang), jnp.sin(ang)], axis=-1).transpose(0, 2, 1)


def _layer(x, positions, ln1, w_in, lb, hg_gain, pe_k, pe_v, k_w1, k_w2, v_w1, v_w2, nsa_gain, w_o, ln2,
           w_gate, w_up, conv_w, conv_b, w_down, final_gain):
    b, t, d = x.shape
    n = b * t
    assert d == D_MODEL and t % FFN_TM == 0 and t % PROJ_TM == 0 and t % ATT_TQ == 0 and t % HG_TT == 0
    n_grp = t // CMP_STRIDE
    assert n_grp == LANES, "compressed-block axis is laid out on exactly one lane tile"
    n_sel = t // SLC_BLOCK
    assert n_sel % 8 == 0 and n_sel <= LANES and ATT_TK % SLC_BLOCK == 0
    x2 = x.reshape(n, d)

    splits = np.cumsum([0, 4 * HG_WIDTH, NSA_WIDTH] + [KV_WIDTH] * 6 + [N_GATES])
    seg = lambda i: w_in[:, splits[i]:splits[i + 1]]
    wh = seg(0).astype(BF16)
    wk = jnp.concatenate([seg(2), seg(3), seg(4), seg(6)], axis=1).astype(BF16)
    wgate = seg(8).reshape(d, 3, NSA_KV_HEADS, HPG).transpose(0, 2, 1, 3).reshape(d, NSA_KV_HEADS, 3 * HPG)
    wgate = jnp.pad(wgate, ((0, 0), (0, 0), (0, LANES - 3 * HPG))).reshape(d, NSA_KV_HEADS * LANES)
    wt = jnp.concatenate([seg(1), seg(5), seg(7), wgate], axis=1).T.astype(BF16)
    cs = _rope_angles(positions)

    hg, kcn, vcn, ksw, qt, vt, gt = _inproj_call(x2, ln1.reshape(1, d), wh, wk, wt, cs, t // PROJ_TM)

    mst, lvl = _hgrn_tables()
    o_hg = _hgrn_call(hg.reshape(b, t, 4 * HG_WIDTH), lb.reshape(1, HG_WIDTH).astype(F32),
                      hg_gain.reshape(1, HG_WIDTH), mst, lvl)

    per_lane = lambda a: jnp.broadcast_to(a.reshape(2, CMP_STRIDE, 1, NSA_HEAD_DIM, -1),
                                          (2, CMP_STRIDE, NSA_KV_HEADS, NSA_HEAD_DIM, a.shape[-1]))
    w1_rows = lambda w1: per_lane(w1).reshape(2, CMP_STRIDE * LANES, CMP_HIDDEN).astype(BF16)
    pe_rows = lambda pe: per_lane(pe[..., None]).reshape(2, 1, CMP_STRIDE * LANES)
    zeros_w2 = jnp.zeros((CMP_HIDDEN, NSA_HEAD_DIM), F32)
    place = lambda w2: jnp.stack([jnp.concatenate([w2, zeros_w2], 1), jnp.concatenate([zeros_w2, w2], 1)])
    kc, vct = _cmp_call(kcn.reshape(b, t, KV_WIDTH), vcn.reshape(b, t, KV_WIDTH), pe_rows(pe_k), pe_rows(pe_v),
                        w1_rows(k_w1), w1_rows(v_w1),
                        place(k_w2).astype(BF16), place(v_w2).transpose(0, 2, 1).astype(BF16))

    cmp_start = np.arange(n_grp) * CMP_STRIDE
    cmp_end = cmp_start + CMP_BLOCK - 1
    sel_start = np.arange(LANES) * SLC_BLOCK
    overlap = ((cmp_start[:, None] <= sel_start[None, :] + SLC_BLOCK - 1) & (cmp_end[:, None] >= sel_start[None, :])
               & (np.arange(LANES)[None, :] < n_sel) & (np.arange(n_grp)[:, None] < n_grp - 1))
    ovt = jnp.asarray(overlap.T.astype(np.float32), BF16)
    o_nsa = _nsa_call(qt, ksw.reshape(b, t, 2 * KV_WIDTH), vt, kc, vct, gt, nsa_gain.reshape(1, NSA_WIDTH), ovt)

    cw = jnp.concatenate([conv_w, conv_b[None, :], jnp.zeros((4, D_FF), F32)], axis=0)
    out = _ffn_call(x2, o_hg.reshape(n, HG_WIDTH), o_nsa.reshape(n, NSA_WIDTH),
                    w_o[:HG_WIDTH].astype(BF16), w_o[HG_WIDTH:].astype(BF16), ln2.reshape(1, d),
                    w_gate.astype(BF16), w_up.astype(BF16), w_down.astype(BF16), cw,
                    final_gain.reshape(1, d), t // FFN_TM)
    return out.reshape(b, t, d)


def kernel(x, positions, ln1_gain, w_in, hgrn_lb_param, hgrn_out_gain, cmp_pe_k, cmp_pe_v, cmp_k_w1, cmp_k_w2,
           cmp_v_w1, cmp_v_w2, nsa_out_gain, w_o, ln2_gain, ffn_w_gate, ffn_w_up, ffn_conv_w, ffn_conv_b,
           ffn_w_down, final_gain):
    depth = ln1_gain.shape[0]
    assert depth == 1, "the fused final norm assumes a single layer"
    lower_bounds = jnp.cumsum(jax.nn.softmax(hgrn_lb_param.astype(F32), axis=0), axis=0)
    l = 0
    return _layer(x, positions, ln1_gain[l], w_in[l], lower_bounds[l], hgrn_out_gain[l], cmp_pe_k[l], cmp_pe_v[l],
                  cmp_k_w1[l], cmp_k_w2[l], cmp_v_w1[l], cmp_v_w2[l], nsa_out_gain[l], w_o[l], ln2_gain[l],
                  ffn_w_gate[l], ffn_w_up[l], ffn_conv_w[l], ffn_conv_b[l], ffn_w_down[l], final_gain)
```

```python
import functools

import jax
import jax.numpy as jnp
import numpy as np
from jax import lax
from jax.experimental import pallas as pl
from jax.experimental.pallas import tpu as pltpu

F32 = jnp.float32
BF16 = jnp.bfloat16

D_MODEL = 1024
HG_HEADS = 4
HG_DK = 128
HG_DV = 128
HG_WIDTH = HG_HEADS * HG_DV
NSA_HEADS = 8
NSA_KV_HEADS = 2
NSA_HEAD_DIM = 64
HPG = NSA_HEADS // NSA_KV_HEADS
NSA_WIDTH = NSA_HEADS * NSA_HEAD_DIM
KV_WIDTH = NSA_KV_HEADS * NSA_HEAD_DIM
CMP_BLOCK = 32
CMP_STRIDE = 16
CMP_HIDDEN = 256
SLC_BLOCK = 64
SLC_TOPK = 16
WINDOW = 512
ROPE_THETA = 500000.0
ROPE_DIM = NSA_HEAD_DIM // 4
ROPE_HALF = ROPE_DIM // 2
D_FF = 2816
EPS = 1e-6
NEG_INF = -1e30
FORCE_SCORE = 1e4
N_GATES = 3 * NSA_HEADS
LOG2_E = 1.4426950408889634

LANES = 128
VMEM_LIMIT = 56 * 1024 * 1024

PROJ_TM = 512
HG_CHUNK = 128
HG_LEVELS = (16, 32, 64)
HG_DIAG = 16
HG_TT = 512
ATT_TQ = 256
ATT_TK = 256
SLC_NEAR_TILES = 1
SUM_ROWS = 16
FFN_TM = 512
FFN_TC = 256
FFN_NC = D_FF // FFN_TC


def _dot(a, b):
    return jnp.dot(a, b, preferred_element_type=F32)


def _dot_nt(a, b):
    return lax.dot_general(a, b, (((1,), (1,)), ((), ())), preferred_element_type=F32)


def _dot_tn(a, b):
    return lax.dot_general(a, b, (((0,), (0,)), ((), ())), preferred_element_type=F32)


def _split3(x):
    hi = x.astype(BF16)
    r = x - hi.astype(F32)
    mid = r.astype(BF16)
    lo = (r - mid.astype(F32)).astype(BF16)
    return hi, mid, lo


def _rms(x, gain):
    return x * lax.rsqrt(jnp.mean(x * x, axis=-1, keepdims=True) + EPS) * gain


def _inproj_body(x_ref, g_ref, wh_ref, wk_ref, wt_ref, cs_ref,
                 hg_ref, kcn_ref, vcn_ref, ksw_ref, qt_ref, vt_ref, gt_ref):
    hb = _rms(x_ref[...], g_ref[...]).astype(BF16)
    hg_ref[...] = _dot(hb, wh_ref[...])

    def rope(v, axis, cos, sin_hi, sin_lo):
        return (v * cos + pltpu.roll(v, ROPE_HALF, axis) * sin_hi
                + pltpu.roll(v, LANES - ROPE_HALF, axis) * sin_lo)

    cos = cs_ref[0, 0:ROPE_HALF, :]
    sin = cs_ref[0, ROPE_HALF:ROPE_DIM, :]
    tm = cos.shape[1]
    zero_h = jnp.zeros((ROPE_HALF, tm), F32)
    rest = NSA_HEAD_DIM - ROPE_DIM
    slab = lambda lo, hi, fill: jnp.concatenate([lo, hi, jnp.full((rest, tm), fill, F32)] * (LANES // NSA_HEAD_DIM), axis=0)
    tab_t = (slab(cos, cos, 1.0), slab(zero_h, sin, 0.0), slab(-sin, zero_h, 0.0))
    tab = tuple(a.T for a in tab_t)
    kn = _dot(hb, wk_ref[...])
    kcn_ref[...] = rope(kn[:, 0:LANES], 1, *tab)
    vcn_ref[...] = kn[:, LANES:2 * LANES]
    ksw_ref[:, 0:LANES] = rope(kn[:, 2 * LANES:3 * LANES], 1, *tab).astype(BF16)
    ksw_ref[:, LANES:2 * LANES] = rope(kn[:, 3 * LANES:4 * LANES], 1, *tab).astype(BF16)

    rt = _dot_nt(wt_ref[...], hb)
    scale = NSA_HEAD_DIM ** -0.5 * LOG2_E
    for j in range(NSA_WIDTH // LANES):
        sl = slice(j * LANES, (j + 1) * LANES)
        qt_ref[0, sl, :] = (rope(rt[sl], 0, *tab_t) * scale).astype(BF16)
    vt_ref[0] = rt[NSA_WIDTH:NSA_WIDTH + 2 * KV_WIDTH].astype(BF16)
    gt_ref[0] = jax.nn.sigmoid(rt[NSA_WIDTH + 2 * KV_WIDTH:])


def _inproj_call(x2, gain, wh, wk, wt, cs, tiles_per_seq):
    n = x2.shape[0]
    tm = PROJ_TM
    t = tiles_per_seq * tm
    b = n // t
    row = lambda w: pl.BlockSpec((tm, w), lambda i: (i, 0))
    col = lambda h: pl.BlockSpec((1, h, tm), lambda i: (i // tiles_per_seq, 0, i % tiles_per_seq))
    full = lambda a: pl.BlockSpec(a.shape, lambda i: (0, 0))
    gate_rows = NSA_KV_HEADS * LANES
    return pl.pallas_call(
        _inproj_body,
        grid=(n // tm,),
        in_specs=[row(D_MODEL), full(gain), full(wh), full(wk), full(wt),
                  col(ROPE_DIM)],
        out_specs=[row(4 * HG_WIDTH), row(KV_WIDTH), row(KV_WIDTH), row(2 * KV_WIDTH),
                   col(NSA_WIDTH), col(2 * KV_WIDTH), col(gate_rows)],
        out_shape=[jax.ShapeDtypeStruct((n, 4 * HG_WIDTH), F32),
                   jax.ShapeDtypeStruct((n, KV_WIDTH), F32),
                   jax.ShapeDtypeStruct((n, KV_WIDTH), F32),
                   jax.ShapeDtypeStruct((n, 2 * KV_WIDTH), BF16),
                   jax.ShapeDtypeStruct((b, NSA_WIDTH, t), BF16),
                   jax.ShapeDtypeStruct((b, 2 * KV_WIDTH, t), BF16),
                   jax.ShapeDtypeStruct((b, gate_rows, t), F32)],
        compiler_params=pltpu.CompilerParams(dimension_semantics=("arbitrary",),
                                             vmem_limit_bytes=VMEM_LIMIT),
        name="inproj",
    )(x2, gain, wh, wk, wt, cs)


def _hgrn_tables():
    L = HG_CHUNK
    t = np.arange(L)[:, None]
    u = np.arange(L)[None, :]
    level = np.where(((t // HG_DIAG) == (u // HG_DIAG)) & (u <= t), 1, 0)
    for li, s in enumerate(HG_LEVELS):
        same = (t // (2 * s)) == (u // (2 * s))
        right = (t % (2 * s)) >= s
        level = np.where(same & right & ((u % (2 * s)) < s), li + 2, level)
    return jnp.asarray((u <= t).astype(np.float32), BF16), jnp.asarray(level, jnp.int32)


def _hgrn_body(q_ref, f_ref, i_ref, g_ref, lb_ref, gain_ref, mst_ref, lvl_ref, o_ref, st_ref):
    L = HG_CHUNK
    n_chunks = q_ref.shape[1] // L

    @pl.when(pl.program_id(1) == 0)
    def _sequence_start():
        st_ref[...] = jnp.zeros_like(st_ref)

    def chunk(c, carry):
        rows = pl.ds(pl.multiple_of(c * L, L), L)
        heads = range(HG_HEADS)
        cols = [slice(h * HG_DK, (h + 1) * HG_DK) for h in heads]
        mst = mst_ref[...]
        lvl = lvl_ref[...]
        n_lv = len(HG_LEVELS)
        row_i = lax.broadcasted_iota(jnp.int32, (L, HG_DK), 0)
        q = [q_ref[0, rows, cols[h]] for h in heads]
        vb = [i_ref[0, rows, cols[h]].astype(BF16) for h in heads]
        f = [lb_ref[:, cols[h]] + (1.0 - lb_ref[:, cols[h]]) * jax.nn.sigmoid(f_ref[0, rows, cols[h]]) for h in heads]
        k = [1.0 - f[h] for h in heads]
        parts = [_split3(jnp.log(f[h])) for h in heads]
        e_full = [(_dot(mst, parts[h][0]) + _dot(mst, parts[h][1])) + _dot(mst, parts[h][2]) for h in heads]
        b_last = [e_full[h][L - 1:L, :] for h in heads]

        def rel_to(b, blk, off):
            refs = []
            for r0 in range(0, L, blk):
                r = r0 + off - 1
                ref = b[r:r + 1, :] if r >= 0 else jnp.zeros((1, HG_DK), F32)
                refs.append(jnp.broadcast_to(ref, (blk, HG_DK)))
            return b - jnp.concatenate(refs, axis=0)

        def level_sums(b):
            out = [rel_to(b, HG_DIAG, 0)]
            for s_half in HG_LEVELS:
                d = rel_to(b, 2 * s_half, s_half)
                out.append(jnp.where((row_i % (2 * s_half)) >= s_half, d, -d))
            return out

        e = [level_sums(e_full[h]) for h in heads]
        wq = [[jnp.exp(e[h][l]) for l in range(n_lv + 1)] for h in heads]
        wk = [[jnp.exp(-e[h][0])] + wq[h][1:] for h in heads]
        prod = [[_dot_nt((q[h] * wq[h][l]).astype(BF16), (k[h] * wk[h][l]).astype(BF16)) for l in range(n_lv + 1)]
                for h in heads]
        st = [st_ref[h] for h in heads]
        inter = [_dot_nt((q[h] * jnp.exp(e_full[h])).astype(BF16), st[h].astype(BF16)) for h in heads]
        k_dec = [(k[h] * jnp.exp(b_last[h] - e_full[h])).astype(BF16) for h in heads]
        upd = [_dot_tn(vb[h], k_dec[h]) for h in heads]
        for h in heads:
            st_ref[h] = st[h] * jnp.exp(b_last[h]) + upd[h]
        a = []
        for h in heads:
            ah = jnp.where(lvl == 1, prod[h][0], 0.0)
            for l in range(1, n_lv + 1):
                ah = jnp.where(lvl == l + 1, prod[h][l], ah)
            a.append(ah.astype(BF16))
        o = [_dot(a[h], vb[h]) + inter[h] for h in heads]
        for h in heads:
            oh = o[h] * lax.rsqrt(jnp.mean(o[h] * o[h], axis=-1, keepdims=True) + EPS) * gain_ref[:, cols[h]]
            o_ref[0, rows, cols[h]] = (oh * jax.nn.silu(g_ref[0, rows, cols[h]])).astype(o_ref.dtype)
        return carry

    lax.fori_loop(0, n_chunks, chunk, 0, unroll=2)


def _hgrn_call(hg, lb, gain, mst, lvl):
    b, t, _ = hg.shape
    tt = HG_TT
    col = lambda k: pl.BlockSpec((1, tt, HG_WIDTH), lambda bi, ti: (bi, ti, k))
    full = lambda a: pl.BlockSpec(a.shape, lambda bi, ti: (0, 0))
    return pl.pallas_call(
        _hgrn_body,
        grid=(b, t // tt),
        in_specs=[col(0), col(1), col(2), col(3), full(lb), full(gain), full(mst), full(lvl)],
        out_specs=pl.BlockSpec((1, tt, HG_WIDTH), lambda bi, ti: (bi, ti, 0)),
        out_shape=jax.ShapeDtypeStruct((b, t, HG_WIDTH), BF16),
        scratch_shapes=[pltpu.VMEM((HG_HEADS, HG_DV, HG_DK), F32)],
        compiler_params=pltpu.CompilerParams(dimension_semantics=("arbitrary", "arbitrary"),
                                             vmem_limit_bytes=VMEM_LIMIT),
        name="hgrn2",
    )(hg, hg, hg, hg, lb, gain, mst, lvl)


def _cmp_body(kcn_ref, vcn_ref, pek_ref, pev_ref, w1k_ref, w1v_ref, w2k_ref, w2v_ref, kc_ref, vc_ref):
    nb = kcn_ref.shape[1] // CMP_STRIDE
    lane_grp = (lax.broadcasted_iota(jnp.int32, (nb, CMP_STRIDE * LANES), 1) // NSA_HEAD_DIM) % NSA_KV_HEADS

    def hidden(src_ref, pe_ref, w1_ref):
        x = jnp.concatenate([src_ref[0, pl.ds(l, nb, stride=CMP_STRIDE), :]
                             for l in range(CMP_STRIDE)], axis=1)
        halves = [x + pe_ref[i] for i in range(2)]
        out = []
        for g in range(NSA_KV_HEADS):
            u, v = (_dot(jnp.where(lane_grp == g, halves[i], 0.0).astype(BF16), w1_ref[i]) for i in range(2))
            out.append(jax.nn.silu(u + pltpu.roll(v, nb - 1, 0)).astype(BF16))
        return out

    hk = hidden(kcn_ref, pek_ref, w1k_ref)
    hv = hidden(vcn_ref, pev_ref, w1v_ref)
    for g in range(NSA_KV_HEADS):
        kc_ref[0, g, 0:nb, :] = _dot(hk[g], w2k_ref[0]).astype(kc_ref.dtype)
        kc_ref[0, g, nb:2 * nb, :] = _dot(hk[g], w2k_ref[1]).astype(kc_ref.dtype)
        vc_ref[0, g, :, 0:nb] = _dot_nt(w2v_ref[0], hv[g]).astype(vc_ref.dtype)
        vc_ref[0, g, :, nb:2 * nb] = _dot_nt(w2v_ref[1], hv[g]).astype(vc_ref.dtype)


def _cmp_call(kcn, vcn, pek, pev, w1k, w1v, w2k, w2v):
    b, t, w = kcn.shape
    nb = t // CMP_STRIDE
    full = lambda a: pl.BlockSpec(a.shape, lambda bi: (0,) * a.ndim)
    out = lambda r, c: pl.BlockSpec((1, NSA_KV_HEADS, r, c), lambda bi: (bi, 0, 0, 0))
    return pl.pallas_call(
        _cmp_body,
        grid=(b,),
        in_specs=[pl.BlockSpec((1, t, w), lambda bi: (bi, 0, 0)), pl.BlockSpec((1, t, w), lambda bi: (bi, 0, 0)),
                  full(pek), full(pev), full(w1k), full(w1v), full(w2k), full(w2v)],
        out_specs=[out(2 * nb, LANES), out(LANES, 2 * nb)],
        out_shape=[jax.ShapeDtypeStruct((b, NSA_KV_HEADS, 2 * nb, LANES), BF16),
                   jax.ShapeDtypeStruct((b, NSA_KV_HEADS, LANES, 2 * nb), BF16)],
        compiler_params=pltpu.CompilerParams(dimension_semantics=("arbitrary",),
                                             vmem_limit_bytes=VMEM_LIMIT),
        name="nsa_compress",
    )(kcn, vcn, pek, pev, w1k, w1v, w2k, w2v)


def _nsa_body(qt_ref, ksw_ref, vst_ref, vwt_ref, kc_ref, vct_ref, gt_ref, gain_ref, ovt_ref, o_ref,
              ks_ref, kw_ref, vs_ref, vw_ref, m_ref, acc_ref, s_ref, ch_ref):
    g = pl.program_id(1)
    qi = pl.program_id(2)
    tq = ATT_TQ
    tk = ATT_TK
    t_len = ksw_ref.shape[1]
    n_kt = t_len // tk
    n_pairs = HPG // 2
    hd = NSA_HEAD_DIM

    @pl.when(qi == 0)
    def _build_kv():
        lane = lax.broadcasted_iota(jnp.int32, (tk, LANES), 1)
        lo_lane = lane < hd
        keep = (lane // hd) == g

        def build_k(src_col, dst_ref):
            def body(j, carry):
                rows = pl.ds(pl.multiple_of(j * tk, tk), tk)
                x = ksw_ref[0, rows, src_col * LANES:(src_col + 1) * LANES].astype(F32)
                dup = jnp.where(keep, x, pltpu.roll(x, hd, 1))
                dst_ref[j, 0:tk, :] = jnp.where(lo_lane, dup, 0.0).astype(BF16)
                dst_ref[j, tk:2 * tk, :] = jnp.where(lo_lane, 0.0, dup).astype(BF16)
                return carry
            lax.fori_loop(0, n_kt, body, 0)

        def build_v(src_ref, dst_ref):
            zero = jnp.zeros((hd, tk), BF16)
            row = lax.broadcasted_iota(jnp.int32, (SUM_ROWS, 2 * tk), 0)
            col = lax.broadcasted_iota(jnp.int32, (SUM_ROWS, 2 * tk), 1)
            ones_rows = jnp.where(((row == 0) & (col < tk)) | ((row == 1) & (col >= tk)), 1.0, 0.0).astype(BF16)
            for j in range(n_kt):
                x = src_ref[0, :, j * tk:(j + 1) * tk]
                dst_ref[j, 0:hd, 0:tk] = x
                dst_ref[j, 0:hd, tk:2 * tk] = zero
                dst_ref[j, hd:2 * hd, 0:tk] = zero
                dst_ref[j, hd:2 * hd, tk:2 * tk] = x
                dst_ref[j, 2 * hd:2 * hd + SUM_ROWS, :] = ones_rows

        build_k(0, ks_ref)
        build_k(1, kw_ref)
        build_v(vst_ref, vs_ref)
        build_v(vwt_ref, vw_ref)

    t0 = qi * tq
    key_i = lax.broadcasted_iota(jnp.int32, (tk, tq), 0)
    qry_t = t0 + lax.broadcasted_iota(jnp.int32, (tk, tq), 1)
    slab_lo = lax.broadcasted_iota(jnp.int32, (LANES, tq), 0) < hd
    acc_row = lax.broadcasted_iota(jnp.int32, (LANES + SUM_ROWS, tq), 0)
    slab_a = (acc_row < hd) | (acc_row == LANES)
    q_pairs = [qt_ref[0, p * LANES:(p + 1) * LANES, :] for p in range(n_pairs)]

    last = (t0 + tq - 1) // tk

    def scores(k_ref, j):
        kt = k_ref[j]
        return [_dot(kt, q_pairs[p]) for p in range(n_pairs)]

    n_cmp_pad = kc_ref.shape[2] // 2
    blk_i = lax.broadcasted_iota(jnp.int32, (n_cmp_pad, tq), 0)
    blk_t = t0 + lax.broadcasted_iota(jnp.int32, (n_cmp_pad, tq), 1)
    cmp_ok = (blk_i * CMP_STRIDE + (CMP_BLOCK - 1)) <= blk_t
    kc = kc_ref[0, 0]
    vct = vct_ref[0, 0]
    s_cmp = [_dot(kc, q_pairs[p]) for p in range(n_pairs)]
    n_win = (WINDOW + tq) // tk
    win_tiles = [last - (n_win - 1) + k for k in range(n_win)]
    n_near = SLC_NEAR_TILES
    near_tiles = [last - k for k in range(n_near)]
    up_front = ([(ks_ref, 0)] + [(kw_ref, jnp.maximum(jw, 0)) for jw in win_tiles]
                + [(ks_ref, jnp.maximum(jn, 0)) for jn in near_tiles])
    for slot, (k_ref, j0) in enumerate(up_front):
        s_first = scores(k_ref, j0)
        for p in range(n_pairs):
            s_ref[slot, p] = s_first[p]
    p_sum = jnp.zeros((n_cmp_pad, tq), F32)
    p_cmp = []
    for p in range(n_pairs):
        probs = []
        for h in range(2):
            sh = jnp.where(cmp_ok, s_cmp[p][h * n_cmp_pad:(h + 1) * n_cmp_pad], NEG_INF)
            mh = jnp.max(sh, axis=0, keepdims=True)
            eh = jnp.where(cmp_ok, jnp.exp2(sh - mh), 0.0)
            den = jnp.sum(eh, axis=0, keepdims=True)
            ph = eh / jnp.where(den > 0.0, den, 1.0)
            p_sum = p_sum + ph
            probs.append(ph.astype(BF16))
        p_cmp.append(jnp.concatenate(probs, axis=0))
    o_cmp = [_dot(vct, p_cmp[p]) for p in range(n_pairs)]

    n_sel = t_len // SLC_BLOCK
    hi, mid, lo = _split3(p_sum)
    ovt = ovt_ref[...]
    p_sel = ((_dot(ovt, hi) + _dot(ovt, mid)) + _dot(ovt, lo))[0:n_sel]
    sel_i = lax.broadcasted_iota(jnp.int32, (n_sel, tq), 0)
    cur = (t0 + lax.broadcasted_iota(jnp.int32, (n_sel, tq), 1)) // SLC_BLOCK
    forced = (sel_i == 0) | (sel_i == cur) | (sel_i == cur - 1)
    score = jnp.where(forced, FORCE_SCORE, p_sel)
    score = jnp.where(sel_i <= cur, score, -jnp.inf)
    rank = jnp.zeros((n_sel, tq), jnp.int32)
    row_grp = 8
    grp_i = lax.broadcasted_iota(jnp.int32, (row_grp, tq), 0)
    for i in range(n_sel):
        ci = score[i:i + 1, :]
        ahead = []
        for r0 in range(0, n_sel, row_grp):
            rows = slice(r0, r0 + row_grp)
            if r0 > i:
                ahead.append(ci >= score[rows])
            elif r0 + row_grp <= i:
                ahead.append(ci > score[rows])
            else:
                ahead.append((ci > score[rows]) | ((ci == score[rows]) & (grp_i > i - r0)))
        rank = rank + jnp.where(jnp.concatenate(ahead, axis=0), 1, 0)
    chosen = jnp.where(rank < min(SLC_TOPK, n_sel), 1.0, 0.0)
    for i in range(n_sel):
        ch_ref[i] = chosen[i:i + 1, :]

    def tile_softmax(slot, bias, m_get, m_put):
        alphas = {}
        probs = {}
        for hh in range(HPG):
            p, h = divmod(hh, 2)
            a_parts = []
            p_parts = []
            for qh in range(tq // LANES):
                ql = slice(qh * LANES, (qh + 1) * LANES)
                sh = s_ref[slot, p, h * tk:(h + 1) * tk, ql] + bias[:, ql]
                m_prev = m_get(hh, qh)
                m_new = jnp.maximum(m_prev, jnp.max(sh, axis=0, keepdims=True))
                m_put(hh, qh, m_new)
                p_parts.append(jnp.exp2(sh - m_new).astype(BF16))
                a_parts.append(jnp.exp2(m_prev - m_new))
            alphas[hh] = jnp.concatenate(a_parts, axis=1)
            probs[hh] = jnp.concatenate(p_parts, axis=1)
        return ([jnp.concatenate([probs[2 * p], probs[2 * p + 1]], axis=0) for p in range(n_pairs)],
                [jnp.where(slab_a, alphas[2 * p], alphas[2 * p + 1]) for p in range(n_pairs)])

    def normalised(acc):
        inv = jnp.where(slab_lo, 1.0 / acc[LANES:LANES + 1, :], 1.0 / acc[LANES + 1:LANES + 2, :])
        return acc[0:LANES] * inv

    m_win = {}
    acc_win = [jnp.zeros((LANES + SUM_ROWS, tq), F32) for _ in range(n_pairs)]
    for k, jw in enumerate(win_tiles):
        key_t = jw * tk + key_i
        rel = qry_t - key_t
        bias = jnp.where((key_t >= 0) & (rel >= 0) & (rel < WINDOW), 0.0, NEG_INF)
        probs, a_rows = tile_softmax(1 + k, bias,
                                     lambda hh, qh: m_win.get((hh, qh), jnp.full((1, LANES), -jnp.inf, F32)),
                                     lambda hh, qh, v: m_win.__setitem__((hh, qh), v))
        vt = vw_ref[jnp.maximum(jw, 0)]
        acc_win = [acc_win[p] * a_rows[p] + _dot(vt, probs[p]) for p in range(n_pairs)]
    o_win = [normalised(acc_win[p]) for p in range(n_pairs)]

    def picked_bias(j, also=None):
        per_tile = tk // SLC_BLOCK
        picked = jnp.concatenate([jnp.broadcast_to(ch_ref[j * per_tile + i], (SLC_BLOCK, tq)) for i in range(per_tile)],
                                 axis=0) > 0.5
        return jnp.where(picked if also is None else picked & also, 0.0, NEG_INF)

    m_near = {}
    acc_near = [jnp.zeros((LANES + SUM_ROWS, tq), F32) for _ in range(n_pairs)]
    for k, jn in enumerate(near_tiles):
        key_t = jn * tk + key_i
        jc = jnp.maximum(jn, 0)
        probs, a_rows = tile_softmax(1 + n_win + k, picked_bias(jc, (key_t >= 0) & (key_t <= qry_t)),
                                     lambda hh, qh: m_near.get((hh, qh), jnp.full((1, LANES), -jnp.inf, F32)),
                                     lambda hh, qh, v: m_near.__setitem__((hh, qh), v))
        vt = vs_ref[jc]
        acc_near = [acc_near[p] * a_rows[p] + _dot(vt, probs[p]) for p in range(n_pairs)]

    def m_put(hh, qh, v):
        m_ref[hh, :, qh * LANES:(qh + 1) * LANES] = v

    for (hh, qh), v in m_near.items():
        m_put(hh, qh, v)
    for p in range(n_pairs):
        acc_ref[p] = acc_near[p]
    n_far = jnp.maximum(last + 1 - n_near, 0)

    def slc_step(j, carry):
        s_next = scores(ks_ref, jnp.minimum(j + 1, n_far - 1))
        probs, a_rows = tile_softmax(0, picked_bias(j), lambda hh, qh: m_ref[hh, :, qh * LANES:(qh + 1) * LANES], m_put)
        vt = vs_ref[j]
        for p in range(n_pairs):
            pv = _dot(vt, probs[p])
            s_ref[0, p] = s_next[p]
            acc_ref[p] = acc_ref[p] * a_rows[p] + pv
        return carry

    lax.fori_loop(0, n_far, slc_step, 0)
    o_slc = [normalised(acc_ref[p]) for p in range(n_pairs)]

    gates = gt_ref[0]
    gain = gain_ref[...]
    for p in range(n_pairs):
        o = jnp.zeros((LANES, tq), F32)
        for c, branch in enumerate((o_cmp[p], o_slc[p], o_win[p])):
            r = c * HPG + 2 * p
            o = o + jnp.where(slab_lo, gates[r:r + 1, :], gates[r + 1:r + 2, :]) * branch
        sq = o * o
        ms_a = jnp.sum(sq[0:hd], axis=0, keepdims=True)
        ms_b = jnp.sum(sq[hd:2 * hd], axis=0, keepdims=True)
        ms = jnp.where(slab_lo, ms_a, ms_b) * (1.0 / hd)
        o = o * lax.rsqrt(ms + EPS)
        o_ref[0, :, p * LANES:(p + 1) * LANES] = (o.T * gain[:, p * LANES:(p + 1) * LANES]).astype(o_ref.dtype)


def _nsa_call(qt, ksw, vt, kc, vct, gt, gain, ovt):
    b, _, t = qt.shape
    tq, tk = ATT_TQ, ATT_TK
    n_kt = t // tk
    gw = HPG * NSA_HEAD_DIM
    hd = NSA_HEAD_DIM
    k_scratch = pltpu.VMEM((n_kt, 2 * tk, LANES), BF16)
    v_scratch = pltpu.VMEM((n_kt, LANES + SUM_ROWS, 2 * tk), BF16)
    return pl.pallas_call(
        _nsa_body,
        grid=(b, NSA_KV_HEADS, t // tq),
        in_specs=[
            pl.BlockSpec((1, gw, tq), lambda bi, gi, qi: (bi, gi, qi)),
            pl.BlockSpec((1, t, 2 * KV_WIDTH), lambda bi, gi, qi: (bi, 0, 0)),
            pl.BlockSpec((1, hd, t), lambda bi, gi, qi: (bi, gi, 0)),
            pl.BlockSpec((1, hd, t), lambda bi, gi, qi: (bi, NSA_KV_HEADS + gi, 0)),
            pl.BlockSpec((1, 1) + kc.shape[2:], lambda bi, gi, qi: (bi, gi, 0, 0)),
            pl.BlockSpec((1, 1) + vct.shape[2:], lambda bi, gi, qi: (bi, gi, 0, 0)),
            pl.BlockSpec((1, LANES, tq), lambda bi, gi, qi: (bi, gi, qi)),
            pl.BlockSpec((1, gw), lambda bi, gi, qi: (0, gi)),
            pl.BlockSpec(ovt.shape, lambda bi, gi, qi: (0, 0)),
        ],
        out_specs=pl.BlockSpec((1, tq, gw), lambda bi, gi, qi: (bi, qi, gi)),
        out_shape=jax.ShapeDtypeStruct((b, t, NSA_WIDTH), BF16),
        scratch_shapes=[k_scratch, k_scratch, v_scratch, v_scratch,
                        pltpu.VMEM((HPG, 1, tq), F32),
                        pltpu.VMEM((HPG // 2, LANES + SUM_ROWS, tq), F32), pltpu.VMEM((1 + (WINDOW + tq) // tk + SLC_NEAR_TILES, HPG // 2, 2 * tk, tq), F32),
                        pltpu.VMEM((t // SLC_BLOCK, 1, tq), F32)],
        compiler_params=pltpu.CompilerParams(dimension_semantics=("arbitrary", "arbitrary", "arbitrary"),
                                             vmem_limit_bytes=VMEM_LIMIT),
        name="nsa_attention",
    )(qt, ksw, vt, vt, kc, vct, gt, gain, ovt)


def _ffn_body(x_ref, oh_ref, on_ref, woh_ref, won_ref, g2_ref, wg_ref, wu_ref, wd_ref, cw_ref, gf_ref,
              out_ref, halo_ref, act_ref, *, tiles_per_seq):
    tm = x_ref.shape[0]
    x1 = x_ref[...] + _dot(oh_ref[...], woh_ref[...]) + _dot(on_ref[...], won_ref[...])
    hb = _rms(x1, g2_ref[...]).astype(BF16)
    row = lax.broadcasted_iota(jnp.int32, (tm, FFN_TC), 0)

    @pl.when((pl.program_id(0) % tiles_per_seq) == 0)
    def _sequence_start():
        halo_ref[...] = jnp.zeros_like(halo_ref)

    def activation(c, gate, up):
        cols = slice(c * FFN_TC, (c + 1) * FFN_TC)
        halo = halo_ref[:, cols]
        halo_ref[:, cols] = gate[tm - 8:tm, :]
        prev1 = jnp.where(row == 0, halo[7:8, :], pltpu.roll(gate, 1, 0))
        prev2 = jnp.where(row == 0, halo[6:7, :], jnp.where(row == 1, halo[7:8, :], pltpu.roll(gate, 2, 0)))
        cw = cw_ref[:, cols]
        y = cw[0:1, :] * prev2 + cw[1:2, :] * prev1 + cw[2:3, :] * gate + cw[3:4, :]
        return (jax.nn.silu(y) * up).astype(BF16)

    chunk = lambda w_ref, c: _dot(hb, w_ref[:, c * FFN_TC:(c + 1) * FFN_TC])
    gate_up = (chunk(wg_ref, 0), chunk(wu_ref, 0))
    for c in range(FFN_NC):
        cur = gate_up
        if c + 1 < FFN_NC:
            gate_up = (chunk(wg_ref, c + 1), chunk(wu_ref, c + 1))
        act_ref[:, c * FFN_TC:(c + 1) * FFN_TC] = activation(c, *cur)
    acc = _dot(act_ref[...], wd_ref[...])
    out_ref[...] = _rms(x1 + acc, gf_ref[...])


def _ffn_call(x2, oh, on, woh, won, g2, wg, wu, wd, cw, gf, tiles_per_seq):
    n = x2.shape[0]
    tm = FFN_TM
    row = lambda w: pl.BlockSpec((tm, w), lambda i: (i, 0))
    full = lambda a: pl.BlockSpec(a.shape, lambda i: (0,) * a.ndim, pipeline_mode=pl.Buffered(1))
    return pl.pallas_call(
        functools.partial(_ffn_body, tiles_per_seq=tiles_per_seq),
        grid=(n // tm,),
        in_specs=[row(D_MODEL), row(HG_WIDTH), row(NSA_WIDTH), full(woh), full(won), full(g2),
                  full(wg), full(wu), full(wd), full(cw), full(gf)],
        out_specs=row(D_MODEL),
        out_shape=jax.ShapeDtypeStruct((n, D_MODEL), F32),
        scratch_shapes=[pltpu.VMEM((8, D_FF), F32), pltpu.VMEM((tm, D_FF), BF16)],
        compiler_params=pltpu.CompilerParams(dimension_semantics=("arbitrary",),
                                             vmem_limit_bytes=VMEM_LIMIT),
        name="outproj_convffn",
    )(x2, oh, on, woh, won, g2, wg, wu, wd, cw, gf)


def _rope_angles(positions):
    inv_freq = ROPE_THETA ** (-jnp.arange(ROPE_HALF, dtype=F32) * 2.0 / ROPE_DIM)
    ang = positions.astype(F32)[..., None] * inv_freq
    return jnp.concatenate([jnp.cos(ang), jnp.sin(ang)], axis=-1).transpose(0, 2, 1)


def _layer(x, positions, ln1, w_in, lb, hg_gain, pe_k, pe_v, k_w1, k_w2, v_w1, v_w2, nsa_gain, w_o, ln2,
           w_gate, w_up, conv_w, conv_b, w_down, final_gain):
    b, t, d = x.shape
    n = b * t
    assert d == D_MODEL and t % FFN_TM == 0 and t % PROJ_TM == 0 and t % ATT_TQ == 0 and t % HG_TT == 0
    n_grp = t // CMP_STRIDE
    assert n_grp == LANES, "compressed-block axis is laid out on exactly one lane tile"
    n_sel = t // SLC_BLOCK
    assert n_sel % 8 == 0 and n_sel <= LANES and ATT_TK % SLC_BLOCK == 0
    x2 = x.reshape(n, d)

    splits = np.cumsum([0, 4 * HG_WIDTH, NSA_WIDTH] + [KV_WIDTH] * 6 + [N_GATES])
    seg = lambda i: w_in[:, splits[i]:splits[i + 1]]
    wh = seg(0).astype(BF16)
    wk = jnp.concatenate([seg(2), seg(3), seg(4), seg(6)], axis=1).astype(BF16)
    wgate = seg(8).reshape(d, 3, NSA_KV_HEADS, HPG).transpose(0, 2, 1, 3).reshape(d, NSA_KV_HEADS, 3 * HPG)
    wgate = jnp.pad(wgate, ((0, 0), (0, 0), (0, LANES - 3 * HPG))).reshape(d, NSA_KV_HEADS * LANES)
    wt = jnp.concatenate([seg(1), seg(5), seg(7), wgate], axis=1).T.astype(BF16)
    cs = _rope_angles(positions)

    hg, kcn, vcn, ksw, qt, vt, gt = _inproj_call(x2, ln1.reshape(1, d), wh, wk, wt, cs, t // PROJ_TM)

    mst, lvl = _hgrn_tables()
    o_hg = _hgrn_call(hg.reshape(b, t, 4 * HG_WIDTH), lb.reshape(1, HG_WIDTH).astype(F32),
                      hg_gain.reshape(1, HG_WIDTH), mst, lvl)

    per_lane = lambda a: jnp.broadcast_to(a.reshape(2, CMP_STRIDE, 1, NSA_HEAD_DIM, -1),
                                          (2, CMP_STRIDE, NSA_KV_HEADS, NSA_HEAD_DIM, a.shape[-1]))
    w1_rows = lambda w1: per_lane(w1).reshape(2, CMP_STRIDE * LANES, CMP_HIDDEN).astype(BF16)
    pe_rows = lambda pe: per_lane(pe[..., None]).reshape(2, 1, CMP_STRIDE * LANES)
    zeros_w2 = jnp.zeros((CMP_HIDDEN, NSA_HEAD_DIM), F32)
    place = lambda w2: jnp.stack([jnp.concatenate([w2, zeros_w2], 1), jnp.concatenate([zeros_w2, w2], 1)])
    kc, vct = _cmp_call(kcn.reshape(b, t, KV_WIDTH), vcn.reshape(b, t, KV_WIDTH), pe_rows(pe_k), pe_rows(pe_v),
                        w1_rows(k_w1), w1_rows(v_w1),
                        place(k_w2).astype(BF16), place(v_w2).transpose(0, 2, 1).astype(BF16))

    cmp_start = np.arange(n_grp) * CMP_STRIDE
    cmp_end = cmp_start + CMP_BLOCK - 1
    sel_start = np.arange(LANES) * SLC_BLOCK
    overlap = ((cmp_start[:, None] <= sel_start[None, :] + SLC_BLOCK - 1) & (cmp_end[:, None] >= sel_start[None, :])
               & (np.arange(LANES)[None, :] < n_sel) & (np.arange(n_grp)[:, None] < n_grp - 1))
    ovt = jnp.asarray(overlap.T.astype(np.float32), BF16)
    o_nsa = _nsa_call(qt, ksw.reshape(b, t, 2 * KV_WIDTH), vt, kc, vct, gt, nsa_gain.reshape(1, NSA_WIDTH), ovt)

    cw = jnp.concatenate([conv_w, conv_b[None, :], jnp.zeros((4, D_FF), F32)], axis=0)
    out = _ffn_call(x2, o_hg.reshape(n, HG_WIDTH), o_nsa.reshape(n, NSA_WIDTH),
                    w_o[:HG_WIDTH].astype(BF16), w_o[HG_WIDTH:].astype(BF16), ln2.reshape(1, d),
                    w_gate.astype(BF16), w_up.astype(BF16), w_down.astype(BF16), cw,
                    final_gain.reshape(1, d), t // FFN_TM)
    return out.reshape(b, t, d)


def kernel(x, positions, ln1_gain, w_in, hgrn_lb_param, hgrn_out_gain, cmp_pe_k, cmp_pe_v, cmp_k_w1, cmp_k_w2,
           cmp_v_w1, cmp_v_w2, nsa_out_gain, w_o, ln2_gain, ffn_w_gate, ffn_w_up, ffn_conv_w, ffn_conv_b,
           ffn_w_down, final_gain):
    depth = ln1_gain.shape[0]
    assert depth == 1, "the fused final norm assumes a single layer"
    lower_bounds = jnp.cumsum(jax.nn.softmax(hgrn_lb_param.astype(F32), axis=0), axis=0)
    l = 0
    return _layer(x, positions, ln1_gain[l], w_in[l], lower_bounds[l], hgrn_out_gain[l], cmp_pe_k[l], cmp_pe_v[l],
                  cmp_k_w1[l], cmp_k_w2[l], cmp_v_w1[l], cmp_v_w2[l], nsa_out_gain[l], w_o[l], ln2_gain[l],
                  ffn_w_gate[l], ffn_w_up[l], ffn_conv_w[l], ffn_conv_b[l], ffn_w_down[l], final_gain)
```

```python
import functools

import jax
import jax.numpy as jnp
import numpy as np
from jax import lax
from jax.experimental import pallas as pl
from jax.experimental.pallas import tpu as pltpu

F32 = jnp.float32
BF16 = jnp.bfloat16

D_MODEL = 1024
HG_HEADS = 4
HG_DK = 128
HG_DV = 128
HG_WIDTH = HG_HEADS * HG_DV
NSA_HEADS = 8
NSA_KV_HEADS = 2
NSA_HEAD_DIM = 64
HPG = NSA_HEADS // NSA_KV_HEADS
NSA_WIDTH = NSA_HEADS * NSA_HEAD_DIM
KV_WIDTH = NSA_KV_HEADS * NSA_HEAD_DIM
CMP_BLOCK = 32
CMP_STRIDE = 16
CMP_HIDDEN = 256
SLC_BLOCK = 64
SLC_TOPK = 16
WINDOW = 512
ROPE_THETA = 500000.0
ROPE_DIM = NSA_HEAD_DIM // 4
ROPE_HALF = ROPE_DIM // 2
D_FF = 2816
CONV_WIDTH = 3
EPS = 1e-6
NEG_INF = -1e30
FORCE_SCORE = 1e4
N_GATES = 3 * NSA_HEADS
LOG2_E = 1.4426950408889634

LANES = 128
SUBLANES = 8
VMEM_LIMIT = 56 * 1024 * 1024

PROJ_TM = 512
HG_CHUNK = 128
HG_LEVELS = (16, 32, 64)
HG_DIAG = 16
HG_TT = 512
ATT_TQ = 256
ATT_TK = 256
SLC_NEAR_TILES = 1
SUM_ROWS = 16
FFN_TM = 512
FFN_TC = 256
FFN_NC = D_FF // FFN_TC


def _dot(a, b):
    return jnp.dot(a, b, preferred_element_type=F32)


def _dot_nt(a, b):
    return lax.dot_general(a, b, (((1,), (1,)), ((), ())), preferred_element_type=F32)


def _dot_tn(a, b):
    return lax.dot_general(a, b, (((0,), (0,)), ((), ())), preferred_element_type=F32)


def _split3(x):
    hi = x.astype(BF16)
    r = x - hi.astype(F32)
    mid = r.astype(BF16)
    lo = (r - mid.astype(F32)).astype(BF16)
    return hi, mid, lo


def _rms(x, gain):
    return x * lax.rsqrt(jnp.mean(x * x, axis=-1, keepdims=True) + EPS) * gain


def _inproj_body(x_ref, g_ref, wh_ref, wk_ref, wt_ref, cs_ref,
                 hg_ref, kcn_ref, vcn_ref, ksw_ref, qt_ref, vt_ref, gt_ref):
    hb = _rms(x_ref[...], g_ref[...]).astype(BF16)
    hg_ref[...] = _dot(hb, wh_ref[...])

    def rope(v, axis, cos, sin_hi, sin_lo):
        return (v * cos + pltpu.roll(v, ROPE_HALF, axis) * sin_hi
                + pltpu.roll(v, LANES - ROPE_HALF, axis) * sin_lo)

    cos = cs_ref[0, 0:ROPE_HALF, :]
    sin = cs_ref[0, ROPE_HALF:ROPE_DIM, :]
    tm = cos.shape[1]
    zero_h = jnp.zeros((ROPE_HALF, tm), F32)
    rest = NSA_HEAD_DIM - ROPE_DIM
    slab = lambda lo, hi, fill: jnp.concatenate([lo, hi, jnp.full((rest, tm), fill, F32)] * (LANES // NSA_HEAD_DIM), axis=0)
    tab_t = (slab(cos, cos, 1.0), slab(zero_h, sin, 0.0), slab(-sin, zero_h, 0.0))
    tab = tuple(a.T for a in tab_t)
    kn = _dot(hb, wk_ref[...])
    kcn_ref[...] = rope(kn[:, 0:LANES], 1, *tab)
    vcn_ref[...] = kn[:, LANES:2 * LANES]
    ksw_ref[:, 0:LANES] = rope(kn[:, 2 * LANES:3 * LANES], 1, *tab).astype(BF16)
    ksw_ref[:, LANES:2 * LANES] = rope(kn[:, 3 * LANES:4 * LANES], 1, *tab).astype(BF16)

    rt = _dot_nt(wt_ref[...], hb)
    scale = NSA_HEAD_DIM ** -0.5 * LOG2_E
    for j in range(NSA_WIDTH // LANES):
        sl = slice(j * LANES, (j + 1) * LANES)
        qt_ref[0, sl, :] = (rope(rt[sl], 0, *tab_t) * scale).astype(BF16)
    vt_ref[0] = rt[NSA_WIDTH:NSA_WIDTH + 2 * KV_WIDTH].astype(BF16)
    gt_ref[0] = jax.nn.sigmoid(rt[NSA_WIDTH + 2 * KV_WIDTH:])


def _inproj_call(x2, gain, wh, wk, wt, cs, tiles_per_seq):
    n = x2.shape[0]
    tm = PROJ_TM
    t = tiles_per_seq * tm
    b = n // t
    row = lambda w: pl.BlockSpec((tm, w), lambda i: (i, 0))
    col = lambda h: pl.BlockSpec((1, h, tm), lambda i: (i // tiles_per_seq, 0, i % tiles_per_seq))
    full = lambda a: pl.BlockSpec(a.shape, lambda i: (0, 0))
    gate_rows = NSA_KV_HEADS * LANES
    return pl.pallas_call(
        _inproj_body,
        grid=(n // tm,),
        in_specs=[row(D_MODEL), full(gain), full(wh), full(wk), full(wt),
                  col(ROPE_DIM)],
        out_specs=[row(4 * HG_WIDTH), row(KV_WIDTH), row(KV_WIDTH), row(2 * KV_WIDTH),
                   col(NSA_WIDTH), col(2 * KV_WIDTH), col(gate_rows)],
        out_shape=[jax.ShapeDtypeStruct((n, 4 * HG_WIDTH), F32),
                   jax.ShapeDtypeStruct((n, KV_WIDTH), F32),
                   jax.ShapeDtypeStruct((n, KV_WIDTH), F32),
                   jax.ShapeDtypeStruct((n, 2 * KV_WIDTH), BF16),
                   jax.ShapeDtypeStruct((b, NSA_WIDTH, t), BF16),
                   jax.ShapeDtypeStruct((b, 2 * KV_WIDTH, t), BF16),
                   jax.ShapeDtypeStruct((b, gate_rows, t), F32)],
        compiler_params=pltpu.CompilerParams(dimension_semantics=("arbitrary",),
                                             vmem_limit_bytes=VMEM_LIMIT),
        name="inproj",
    )(x2, gain, wh, wk, wt, cs)


def _hgrn_tables():
    L = HG_CHUNK
    t = np.arange(L)[:, None]
    u = np.arange(L)[None, :]
    level = np.where(((t // HG_DIAG) == (u // HG_DIAG)) & (u <= t), 1, 0)
    for li, s in enumerate(HG_LEVELS):
        same = (t // (2 * s)) == (u // (2 * s))
        right = (t % (2 * s)) >= s
        level = np.where(same & right & ((u % (2 * s)) < s), li + 2, level)
    return jnp.asarray((u <= t).astype(np.float32), BF16), jnp.asarray(level, jnp.int32)


def _hgrn_body(q_ref, f_ref, i_ref, g_ref, lb_ref, gain_ref, mst_ref, lvl_ref, o_ref, st_ref):
    L = HG_CHUNK
    n_chunks = q_ref.shape[1] // L

    @pl.when(pl.program_id(1) == 0)
    def _sequence_start():
        st_ref[...] = jnp.zeros_like(st_ref)

    def chunk(c, carry):
        rows = pl.ds(pl.multiple_of(c * L, L), L)
        heads = range(HG_HEADS)
        cols = [slice(h * HG_DK, (h + 1) * HG_DK) for h in heads]
        mst = mst_ref[...]
        lvl = lvl_ref[...]
        n_lv = len(HG_LEVELS)
        row_i = lax.broadcasted_iota(jnp.int32, (L, HG_DK), 0)
        q = [q_ref[0, rows, cols[h]] for h in heads]
        vb = [i_ref[0, rows, cols[h]].astype(BF16) for h in heads]
        f = [lb_ref[:, cols[h]] + (1.0 - lb_ref[:, cols[h]]) * jax.nn.sigmoid(f_ref[0, rows, cols[h]]) for h in heads]
        k = [1.0 - f[h] for h in heads]
        parts = [_split3(jnp.log(f[h])) for h in heads]
        e_full = [(_dot(mst, parts[h][0]) + _dot(mst, parts[h][1])) + _dot(mst, parts[h][2]) for h in heads]
        b_last = [e_full[h][L - 1:L, :] for h in heads]

        def rel_to(b, blk, off):
            refs = []
            for r0 in range(0, L, blk):
                r = r0 + off - 1
                ref = b[r:r + 1, :] if r >= 0 else jnp.zeros((1, HG_DK), F32)
                refs.append(jnp.broadcast_to(ref, (blk, HG_DK)))
            return b - jnp.concatenate(refs, axis=0)

        def level_sums(b):
            out = [rel_to(b, HG_DIAG, 0)]
            for s_half in HG_LEVELS:
                d = rel_to(b, 2 * s_half, s_half)
                out.append(jnp.where((row_i % (2 * s_half)) >= s_half, d, -d))
            return out

        e = [level_sums(e_full[h]) for h in heads]
        wq = [[jnp.exp(e[h][l]) for l in range(n_lv + 1)] for h in heads]
        wk = [[jnp.exp(-e[h][0])] + wq[h][1:] for h in heads]
        prod = [[_dot_nt((q[h] * wq[h][l]).astype(BF16), (k[h] * wk[h][l]).astype(BF16)) for l in range(n_lv + 1)]
                for h in heads]
        st = [st_ref[h] for h in heads]
        inter = [_dot_nt((q[h] * jnp.exp(e_full[h])).astype(BF16), st[h].astype(BF16)) for h in heads]
        k_dec = [(k[h] * jnp.exp(b_last[h] - e_full[h])).astype(BF16) for h in heads]
        upd = [_dot_tn(vb[h], k_dec[h]) for h in heads]
        for h in heads:
            st_ref[h] = st[h] * jnp.exp(b_last[h]) + upd[h]
        a = []
        for h in heads:
            ah = jnp.where(lvl == 1, prod[h][0], 0.0)
            for l in range(1, n_lv + 1):
                ah = jnp.where(lvl == l + 1, prod[h][l], ah)
            a.append(ah.astype(BF16))
        o = [_dot(a[h], vb[h]) + inter[h] for h in heads]
        for h in heads:
            oh = o[h] * lax.rsqrt(jnp.mean(o[h] * o[h], axis=-1, keepdims=True) + EPS) * gain_ref[:, cols[h]]
            o_ref[0, rows, cols[h]] = (oh * jax.nn.silu(g_ref[0, rows, cols[h]])).astype(o_ref.dtype)
        return carry

    lax.fori_loop(0, n_chunks, chunk, 0, unroll=2)


def _hgrn_call(hg, lb, gain, mst, lvl):
    b, t, _ = hg.shape
    tt = HG_TT
    col = lambda k: pl.BlockSpec((1, tt, HG_WIDTH), lambda bi, ti: (bi, ti, k))
    full = lambda a: pl.BlockSpec(a.shape, lambda bi, ti: (0, 0))
    return pl.pallas_call(
        _hgrn_body,
        grid=(b, t // tt),
        in_specs=[col(0), col(1), col(2), col(3), full(lb), full(gain), full(mst), full(lvl)],
        out_specs=pl.BlockSpec((1, tt, HG_WIDTH), lambda bi, ti: (bi, ti, 0)),
        out_shape=jax.ShapeDtypeStruct((b, t, HG_WIDTH), BF16),
        scratch_shapes=[pltpu.VMEM((HG_HEADS, HG_DV, HG_DK), F32)],
        compiler_params=pltpu.CompilerParams(dimension_semantics=("arbitrary", "arbitrary"),
                                             vmem_limit_bytes=VMEM_LIMIT),
        name="hgrn2",
    )(hg, hg, hg, hg, lb, gain, mst, lvl)


def _cmp_body(kcn_ref, vcn_ref, pek_ref, pev_ref, w1k_ref, w1v_ref, w2k_ref, w2v_ref, kc_ref, vc_ref):
    nb = kcn_ref.shape[1] // CMP_STRIDE
    lane_grp = (lax.broadcasted_iota(jnp.int32, (nb, CMP_STRIDE * LANES), 1) // NSA_HEAD_DIM) % NSA_KV_HEADS

    def hidden(src_ref, pe_ref, w1_ref):
        x = jnp.concatenate([src_ref[0, pl.ds(l, nb, stride=CMP_STRIDE), :]
                             for l in range(CMP_STRIDE)], axis=1)
        halves = [x + pe_ref[i] for i in range(2)]
        out = []
        for g in range(NSA_KV_HEADS):
            u, v = (_dot(jnp.where(lane_grp == g, halves[i], 0.0).astype(BF16), w1_ref[i]) for i in range(2))
            out.append(jax.nn.silu(u + pltpu.roll(v, nb - 1, 0)).astype(BF16))
        return out

    hk = hidden(kcn_ref, pek_ref, w1k_ref)
    hv = hidden(vcn_ref, pev_ref, w1v_ref)
    for g in range(NSA_KV_HEADS):
        kc_ref[0, g, 0:nb, :] = _dot(hk[g], w2k_ref[0]).astype(kc_ref.dtype)
        kc_ref[0, g, nb:2 * nb, :] = _dot(hk[g], w2k_ref[1]).astype(kc_ref.dtype)
        vc_ref[0, g, :, 0:nb] = _dot_nt(w2v_ref[0], hv[g]).astype(vc_ref.dtype)
        vc_ref[0, g, :, nb:2 * nb] = _dot_nt(w2v_ref[1], hv[g]).astype(vc_ref.dtype)


def _cmp_call(kcn, vcn, pek, pev, w1k, w1v, w2k, w2v):
    b, t, w = kcn.shape
    nb = t // CMP_STRIDE
    full = lambda a: pl.BlockSpec(a.shape, lambda bi: (0,) * a.ndim)
    out = lambda r, c: pl.BlockSpec((1, NSA_KV_HEADS, r, c), lambda bi: (bi, 0, 0, 0))
    return pl.pallas_call(
        _cmp_body,
        grid=(b,),
        in_specs=[pl.BlockSpec((1, t, w), lambda bi: (bi, 0, 0)), pl.BlockSpec((1, t, w), lambda bi: (bi, 0, 0)),
                  full(pek), full(pev), full(w1k), full(w1v), full(w2k), full(w2v)],
        out_specs=[out(2 * nb, LANES), out(LANES, 2 * nb)],
        out_shape=[jax.ShapeDtypeStruct((b, NSA_KV_HEADS, 2 * nb, LANES), BF16),
                   jax.ShapeDtypeStruct((b, NSA_KV_HEADS, LANES, 2 * nb), BF16)],
        compiler_params=pltpu.CompilerParams(dimension_semantics=("arbitrary",),
                                             vmem_limit_bytes=VMEM_LIMIT),
        name="nsa_compress",
    )(kcn, vcn, pek, pev, w1k, w1v, w2k, w2v)


def _nsa_body(qt_ref, ksw_ref, vst_ref, vwt_ref, kc_ref, vct_ref, gt_ref, gain_ref, ovt_ref, o_ref,
              ks_ref, kw_ref, vs_ref, vw_ref, m_ref, acc_ref, s_ref, ch_ref):
    g = pl.program_id(1)
    qi = pl.program_id(2)
    tq = ATT_TQ
    tk = ATT_TK
    t_len = ksw_ref.shape[1]
    n_kt = t_len // tk
    n_pairs = HPG // 2
    hd = NSA_HEAD_DIM

    @pl.when(qi == 0)
    def _build_kv():
        lane = lax.broadcasted_iota(jnp.int32, (tk, LANES), 1)
        lo_lane = lane < hd
        keep = (lane // hd) == g

        def build_k(src_col, dst_ref):
            def body(j, carry):
                rows = pl.ds(pl.multiple_of(j * tk, tk), tk)
                x = ksw_ref[0, rows, src_col * LANES:(src_col + 1) * LANES].astype(F32)
                dup = jnp.where(keep, x, pltpu.roll(x, hd, 1))
                dst_ref[j, 0:tk, :] = jnp.where(lo_lane, dup, 0.0).astype(BF16)
                dst_ref[j, tk:2 * tk, :] = jnp.where(lo_lane, 0.0, dup).astype(BF16)
                return carry
            lax.fori_loop(0, n_kt, body, 0)

        def build_v(src_ref, dst_ref):
            zero = jnp.zeros((hd, tk), BF16)
            row = lax.broadcasted_iota(jnp.int32, (SUM_ROWS, 2 * tk), 0)
            col = lax.broadcasted_iota(jnp.int32, (SUM_ROWS, 2 * tk), 1)
            ones_rows = jnp.where(((row == 0) & (col < tk)) | ((row == 1) & (col >= tk)), 1.0, 0.0).astype(BF16)
            for j in range(n_kt):
                x = src_ref[0, :, j * tk:(j + 1) * tk]
                dst_ref[j, 0:hd, 0:tk] = x
                dst_ref[j, 0:hd, tk:2 * tk] = zero
                dst_ref[j, hd:2 * hd, 0:tk] = zero
                dst_ref[j, hd:2 * hd, tk:2 * tk] = x
                dst_ref[j, 2 * hd:2 * hd + SUM_ROWS, :] = ones_rows

        build_k(0, ks_ref)
        build_k(1, kw_ref)
        build_v(vst_ref, vs_ref)
        build_v(vwt_ref, vw_ref)

    t0 = qi * tq
    key_i = lax.broadcasted_iota(jnp.int32, (tk, tq), 0)
    qry_t = t0 + lax.broadcasted_iota(jnp.int32, (tk, tq), 1)
    slab_lo = lax.broadcasted_iota(jnp.int32, (LANES, tq), 0) < hd
    acc_row = lax.broadcasted_iota(jnp.int32, (LANES + SUM_ROWS, tq), 0)
    slab_a = (acc_row < hd) | (acc_row == LANES)
    q_pairs = [qt_ref[0, p * LANES:(p + 1) * LANES, :] for p in range(n_pairs)]

    last = (t0 + tq - 1) // tk

    def scores(k_ref, j):
        kt = k_ref[j]
        return [_dot(kt, q_pairs[p]) for p in range(n_pairs)]

    n_cmp_pad = kc_ref.shape[2] // 2
    blk_i = lax.broadcasted_iota(jnp.int32, (n_cmp_pad, tq), 0)
    blk_t = t0 + lax.broadcasted_iota(jnp.int32, (n_cmp_pad, tq), 1)
    cmp_ok = (blk_i * CMP_STRIDE + (CMP_BLOCK - 1)) <= blk_t
    kc = kc_ref[0, 0]
    vct = vct_ref[0, 0]
    s_cmp = [_dot(kc, q_pairs[p]) for p in range(n_pairs)]
    n_win = (WINDOW + tq) // tk
    win_tiles = [last - (n_win - 1) + k for k in range(n_win)]
    n_near = SLC_NEAR_TILES
    near_tiles = [last - k for k in range(n_near)]
    up_front = ([(ks_ref, 0)] + [(kw_ref, jnp.maximum(jw, 0)) for jw in win_tiles]
                + [(ks_ref, jnp.maximum(jn, 0)) for jn in near_tiles])
    for slot, (k_ref, j0) in enumerate(up_front):
        s_first = scores(k_ref, j0)
        for p in range(n_pairs):
            s_ref[slot, p] = s_first[p]
    p_sum = jnp.zeros((n_cmp_pad, tq), F32)
    p_cmp = []
    for p in range(n_pairs):
        probs = []
        for h in range(2):
            sh = jnp.where(cmp_ok, s_cmp[p][h * n_cmp_pad:(h + 1) * n_cmp_pad], NEG_INF)
            mh = jnp.max(sh, axis=0, keepdims=True)
            eh = jnp.where(cmp_ok, jnp.exp2(sh - mh), 0.0)
            den = jnp.sum(eh, axis=0, keepdims=True)
            ph = eh / jnp.where(den > 0.0, den, 1.0)
            p_sum = p_sum + ph
            probs.append(ph.astype(BF16))
        p_cmp.append(jnp.concatenate(probs, axis=0))
    o_cmp = [_dot(vct, p_cmp[p]) for p in range(n_pairs)]

    n_sel = t_len // SLC_BLOCK
    hi, mid, lo = _split3(p_sum)
    ovt = ovt_ref[...]
    p_sel = ((_dot(ovt, hi) + _dot(ovt, mid)) + _dot(ovt, lo))[0:n_sel]
    sel_i = lax.broadcasted_iota(jnp.int32, (n_sel, tq), 0)
    cur = (t0 + lax.broadcasted_iota(jnp.int32, (n_sel, tq), 1)) // SLC_BLOCK
    forced = (sel_i == 0) | (sel_i == cur) | (sel_i == cur - 1)
    score = jnp.where(forced, FORCE_SCORE, p_sel)
    score = jnp.where(sel_i <= cur, score, -jnp.inf)
    rank = jnp.zeros((n_sel, tq), jnp.int32)
    row_grp = SUBLANES
    grp_i = lax.broadcasted_iota(jnp.int32, (row_grp, tq), 0)
    for i in range(n_sel):
        ci = score[i:i + 1, :]
        ahead = []
        for r0 in range(0, n_sel, row_grp):
            rows = slice(r0, r0 + row_grp)
            if r0 > i:
                ahead.append(ci >= score[rows])
            elif r0 + row_grp <= i:
                ahead.append(ci > score[rows])
            else:
                ahead.append((ci > score[rows]) | ((ci == score[rows]) & (grp_i > i - r0)))
        rank = rank + jnp.where(jnp.concatenate(ahead, axis=0), 1, 0)
    chosen = jnp.where(rank < min(SLC_TOPK, n_sel), 1.0, 0.0)
    for i in range(n_sel):
        ch_ref[i] = chosen[i:i + 1, :]

    def tile_softmax(slot, bias, m_get, m_put):
        alphas = {}
        probs = {}
        for hh in range(HPG):
            p, h = divmod(hh, 2)
            a_parts = []
            p_parts = []
            for qh in range(tq // LANES):
                ql = slice(qh * LANES, (qh + 1) * LANES)
                sh = s_ref[slot, p, h * tk:(h + 1) * tk, ql] + (bias if bias.shape == (1, 1) else bias[:, ql])
                m_prev = m_get(hh, qh)
                m_new = jnp.maximum(m_prev, jnp.max(sh, axis=0, keepdims=True))
                m_put(hh, qh, m_new)
                p_parts.append(jnp.exp2(sh - m_new).astype(BF16))
                a_parts.append(jnp.exp2(m_prev - m_new))
            alphas[hh] = jnp.concatenate(a_parts, axis=1)
            probs[hh] = jnp.concatenate(p_parts, axis=1)
        return ([jnp.concatenate([probs[2 * p], probs[2 * p + 1]], axis=0) for p in range(n_pairs)],
                [jnp.where(slab_a, alphas[2 * p], alphas[2 * p + 1]) for p in range(n_pairs)])

    def normalised(acc):
        inv = jnp.where(slab_lo, 1.0 / acc[LANES:LANES + 1, :], 1.0 / acc[LANES + 1:LANES + 2, :])
        return acc[0:LANES] * inv

    m_win = {}
    acc_win = [jnp.zeros((LANES + SUM_ROWS, tq), F32) for _ in range(n_pairs)]
    for k, jw in enumerate(win_tiles):
        rel_hi = (n_win - k) * tk - 1
        rel_lo = rel_hi - (tq - 1) - (tk - 1)
        if rel_lo >= 0 and rel_hi < WINDOW:
            bias = jnp.where(jw >= 0, 0.0, NEG_INF).astype(F32).reshape(1, 1)
        else:
            key_t = jw * tk + key_i
            rel = qry_t - key_t
            bias = jnp.where((key_t >= 0) & (rel >= 0) & (rel < WINDOW), 0.0, NEG_INF)
        probs, a_rows = tile_softmax(1 + k, bias,
                                     lambda hh, qh: m_win.get((hh, qh), jnp.full((1, LANES), -jnp.inf, F32)),
                                     lambda hh, qh, v: m_win.__setitem__((hh, qh), v))
        vt = vw_ref[jnp.maximum(jw, 0)]
        acc_win = [acc_win[p] * a_rows[p] + _dot(vt, probs[p]) for p in range(n_pairs)]
    o_win = [normalised(acc_win[p]) for p in range(n_pairs)]

    def picked_bias(j, also=None):
        per_tile = tk // SLC_BLOCK
        picked = jnp.concatenate([jnp.broadcast_to(ch_ref[j * per_tile + i], (SLC_BLOCK, tq)) for i in range(per_tile)],
                                 axis=0) > 0.5
        return jnp.where(picked if also is None else picked & also, 0.0, NEG_INF)

    m_near = {}
    acc_near = [jnp.zeros((LANES + SUM_ROWS, tq), F32) for _ in range(n_pairs)]
    for k, jn in enumerate(near_tiles):
        key_t = jn * tk + key_i
        jc = jnp.maximum(jn, 0)
        probs, a_rows = tile_softmax(1 + n_win + k, picked_bias(jc, (key_t >= 0) & (key_t <= qry_t)),
                                     lambda hh, qh: m_near.get((hh, qh), jnp.full((1, LANES), -jnp.inf, F32)),
                                     lambda hh, qh, v: m_near.__setitem__((hh, qh), v))
        vt = vs_ref[jc]
        acc_near = [acc_near[p] * a_rows[p] + _dot(vt, probs[p]) for p in range(n_pairs)]

    def m_put(hh, qh, v):
        m_ref[hh, :, qh * LANES:(qh + 1) * LANES] = v

    for (hh, qh), v in m_near.items():
        m_put(hh, qh, v)
    for p in range(n_pairs):
        acc_ref[p] = acc_near[p]
    n_far = jnp.maximum(last + 1 - n_near, 0)

    def slc_step(j, carry):
        s_next = scores(ks_ref, jnp.minimum(j + 1, n_far - 1))
        probs, a_rows = tile_softmax(0, picked_bias(j), lambda hh, qh: m_ref[hh, :, qh * LANES:(qh + 1) * LANES], m_put)
        vt = vs_ref[j]
        for p in range(n_pairs):
            pv = _dot(vt, probs[p])
            s_ref[0, p] = s_next[p]
            acc_ref[p] = acc_ref[p] * a_rows[p] + pv
        return carry

    lax.fori_loop(0, n_far, slc_step, 0)
    o_slc = [normalised(acc_ref[p]) for p in range(n_pairs)]

    gates = gt_ref[0]
    gain = gain_ref[...]
    for p in range(n_pairs):
        o = jnp.zeros((LANES, tq), F32)
        for c, branch in enumerate((o_cmp[p], o_slc[p], o_win[p])):
            r = c * HPG + 2 * p
            o = o + jnp.where(slab_lo, gates[r:r + 1, :], gates[r + 1:r + 2, :]) * branch
        sq = o * o
        ms_a = jnp.sum(sq[0:hd], axis=0, keepdims=True)
        ms_b = jnp.sum(sq[hd:2 * hd], axis=0, keepdims=True)
        ms = jnp.where(slab_lo, ms_a, ms_b) * (1.0 / hd)
        o = o * lax.rsqrt(ms + EPS)
        o_ref[0, :, p * LANES:(p + 1) * LANES] = (o.T * gain[:, p * LANES:(p + 1) * LANES]).astype(o_ref.dtype)


def _nsa_call(qt, ksw, vt, kc, vct, gt, gain, ovt):
    b, _, t = qt.shape
    tq, tk = ATT_TQ, ATT_TK
    n_kt = t // tk
    gw = HPG * NSA_HEAD_DIM
    hd = NSA_HEAD_DIM
    k_scratch = pltpu.VMEM((n_kt, 2 * tk, LANES), BF16)
    v_scratch = pltpu.VMEM((n_kt, LANES + SUM_ROWS, 2 * tk), BF16)
    return pl.pallas_call(
        _nsa_body,
        grid=(b, NSA_KV_HEADS, t // tq),
        in_specs=[
            pl.BlockSpec((1, gw, tq), lambda bi, gi, qi: (bi, gi, qi)),
            pl.BlockSpec((1, t, 2 * KV_WIDTH), lambda bi, gi, qi: (bi, 0, 0)),
            pl.BlockSpec((1, hd, t), lambda bi, gi, qi: (bi, gi, 0)),
            pl.BlockSpec((1, hd, t), lambda bi, gi, qi: (bi, NSA_KV_HEADS + gi, 0)),
            pl.BlockSpec((1, 1) + kc.shape[2:], lambda bi, gi, qi: (bi, gi, 0, 0)),
            pl.BlockSpec((1, 1) + vct.shape[2:], lambda bi, gi, qi: (bi, gi, 0, 0)),
            pl.BlockSpec((1, LANES, tq), lambda bi, gi, qi: (bi, gi, qi)),
            pl.BlockSpec((1, gw), lambda bi, gi, qi: (0, gi)),
            pl.BlockSpec(ovt.shape, lambda bi, gi, qi: (0, 0)),
        ],
        out_specs=pl.BlockSpec((1, tq, gw), lambda bi, gi, qi: (bi, qi, gi)),
        out_shape=jax.ShapeDtypeStruct((b, t, NSA_WIDTH), BF16),
        scratch_shapes=[k_scratch, k_scratch, v_scratch, v_scratch,
                        pltpu.VMEM((HPG, 1, tq), F32),
                        pltpu.VMEM((HPG // 2, LANES + SUM_ROWS, tq), F32), pltpu.VMEM((1 + (WINDOW + tq) // tk + SLC_NEAR_TILES, HPG // 2, 2 * tk, tq), F32),
                        pltpu.VMEM((t // SLC_BLOCK, 1, tq), F32)],
        compiler_params=pltpu.CompilerParams(dimension_semantics=("arbitrary", "arbitrary", "arbitrary"),
                                             vmem_limit_bytes=VMEM_LIMIT),
        name="nsa_attention",
    )(qt, ksw, vt, vt, kc, vct, gt, gain, ovt)


def _ffn_body(x_ref, oh_ref, on_ref, woh_ref, won_ref, g2_ref, wg_ref, wu_ref, wd_ref, cw_ref, gf_ref,
              out_ref, halo_ref, act_ref, *, tiles_per_seq):
    tm = x_ref.shape[0]
    x1 = x_ref[...] + _dot(oh_ref[...], woh_ref[...]) + _dot(on_ref[...], won_ref[...])
    hb = _rms(x1, g2_ref[...]).astype(BF16)
    row = lax.broadcasted_iota(jnp.int32, (tm, FFN_TC), 0)

    @pl.when((pl.program_id(0) % tiles_per_seq) == 0)
    def _sequence_start():
        halo_ref[...] = jnp.zeros_like(halo_ref)

    def activation(c, gate, up):
        cols = slice(c * FFN_TC, (c + 1) * FFN_TC)
        halo = halo_ref[:, cols]
        halo_ref[:, cols] = gate[tm - SUBLANES:tm, :]
        last1 = halo[SUBLANES - 1:SUBLANES, :]
        last2 = halo[SUBLANES - 2:SUBLANES - 1, :]
        prev1 = jnp.where(row == 0, last1, pltpu.roll(gate, 1, 0))
        prev2 = jnp.where(row == 0, last2, jnp.where(row == 1, last1, pltpu.roll(gate, 2, 0)))
        cw = cw_ref[:, cols]
        y = cw[0:1, :] * prev2 + cw[1:2, :] * prev1 + cw[2:3, :] * gate + cw[3:4, :]
        return (jax.nn.silu(y) * up).astype(BF16)

    chunk = lambda w_ref, c: _dot(hb, w_ref[:, c * FFN_TC:(c + 1) * FFN_TC])
    gate_up = (chunk(wg_ref, 0), chunk(wu_ref, 0))
    for c in range(FFN_NC):
        cur = gate_up
        if c + 1 < FFN_NC:
            gate_up = (chunk(wg_ref, c + 1), chunk(wu_ref, c + 1))
        act_ref[:, c * FFN_TC:(c + 1) * FFN_TC] = activation(c, *cur)
    acc = _dot(act_ref[...], wd_ref[...])
    out_ref[...] = _rms(x1 + acc, gf_ref[...])


def _ffn_call(x2, oh, on, woh, won, g2, wg, wu, wd, cw, gf, tiles_per_seq):
    n = x2.shape[0]
    tm = FFN_TM
    row = lambda w: pl.BlockSpec((tm, w), lambda i: (i, 0))
    full = lambda a: pl.BlockSpec(a.shape, lambda i: (0,) * a.ndim, pipeline_mode=pl.Buffered(1))
    return pl.pallas_call(
        functools.partial(_ffn_body, tiles_per_seq=tiles_per_seq),
        grid=(n // tm,),
        in_specs=[row(D_MODEL), row(HG_WIDTH), row(NSA_WIDTH), full(woh), full(won), full(g2),
                  full(wg), full(wu), full(wd), full(cw), full(gf)],
        out_specs=row(D_MODEL),
        out_shape=jax.ShapeDtypeStruct((n, D_MODEL), F32),
        scratch_shapes=[pltpu.VMEM((SUBLANES, D_FF), F32), pltpu.VMEM((tm, D_FF), BF16)],
        compiler_params=pltpu.CompilerParams(dimension_semantics=("arbitrary",),
                                             vmem_limit_bytes=VMEM_LIMIT),
        name="outproj_convffn",
    )(x2, oh, on, woh, won, g2, wg, wu, wd, cw, gf)


def _rope_angles(positions):
    inv_freq = ROPE_THETA ** (-jnp.arange(ROPE_HALF, dtype=F32) * 2.0 / ROPE_DIM)
    ang = positions.astype(F32)[..., None] * inv_freq
    return jnp.concatenate([jnp.cos(ang), jnp.sin(ang)], axis=-1).transpose(0, 2, 1)


def _layer(x, positions, ln1, w_in, lb, hg_gain, pe_k, pe_v, k_w1, k_w2, v_w1, v_w2, nsa_gain, w_o, ln2,
           w_gate, w_up, conv_w, conv_b, w_down, final_gain):
    b, t, d = x.shape
    n = b * t
    assert d == D_MODEL and t % FFN_TM == 0 and t % PROJ_TM == 0 and t % ATT_TQ == 0 and t % HG_TT == 0
    n_grp = t // CMP_STRIDE
    assert n_grp == LANES, "compressed-block axis is laid out on exactly one lane tile"
    n_sel = t // SLC_BLOCK
    assert n_sel % 8 == 0 and n_sel <= LANES and ATT_TK % SLC_BLOCK == 0
    x2 = x.reshape(n, d)

    splits = np.cumsum([0, 4 * HG_WIDTH, NSA_WIDTH] + [KV_WIDTH] * 6 + [N_GATES])
    seg = lambda i: w_in[:, splits[i]:splits[i + 1]]
    wh = seg(0).astype(BF16)
    wk = jnp.concatenate([seg(2), seg(3), seg(4), seg(6)], axis=1).astype(BF16)
    wgate = seg(8).reshape(d, 3, NSA_KV_HEADS, HPG).transpose(0, 2, 1, 3).reshape(d, NSA_KV_HEADS, 3 * HPG)
    wgate = jnp.pad(wgate, ((0, 0), (0, 0), (0, LANES - 3 * HPG))).reshape(d, NSA_KV_HEADS * LANES)
    wt = jnp.concatenate([seg(1), seg(5), seg(7), wgate], axis=1).T.astype(BF16)
    cs = _rope_angles(positions)

    hg, kcn, vcn, ksw, qt, vt, gt = _inproj_call(x2, ln1.reshape(1, d), wh, wk, wt, cs, t // PROJ_TM)

    mst, lvl = _hgrn_tables()
    o_hg = _hgrn_call(hg.reshape(b, t, 4 * HG_WIDTH), lb.reshape(1, HG_WIDTH).astype(F32),
                      hg_gain.reshape(1, HG_WIDTH), mst, lvl)

    per_lane = lambda a: jnp.broadcast_to(a.reshape(2, CMP_STRIDE, 1, NSA_HEAD_DIM, -1),
                                          (2, CMP_STRIDE, NSA_KV_HEADS, NSA_HEAD_DIM, a.shape[-1]))
    w1_rows = lambda w1: per_lane(w1).reshape(2, CMP_STRIDE * LANES, CMP_HIDDEN).astype(BF16)
    pe_rows = lambda pe: per_lane(pe[..., None]).reshape(2, 1, CMP_STRIDE * LANES)
    zeros_w2 = jnp.zeros((CMP_HIDDEN, NSA_HEAD_DIM), F32)
    place = lambda w2: jnp.stack([jnp.concatenate([w2, zeros_w2], 1), jnp.concatenate([zeros_w2, w2], 1)])
    kc, vct = _cmp_call(kcn.reshape(b, t, KV_WIDTH), vcn.reshape(b, t, KV_WIDTH), pe_rows(pe_k), pe_rows(pe_v),
                        w1_rows(k_w1), w1_rows(v_w1),
                        place(k_w2).astype(BF16), place(v_w2).transpose(0, 2, 1).astype(BF16))

    cmp_start = np.arange(n_grp) * CMP_STRIDE
    cmp_end = cmp_start + CMP_BLOCK - 1
    sel_start = np.arange(LANES) * SLC_BLOCK
    overlap = ((cmp_start[:, None] <= sel_start[None, :] + SLC_BLOCK - 1) & (cmp_end[:, None] >= sel_start[None, :])
               & (np.arange(LANES)[None, :] < n_sel) & (np.arange(n_grp)[:, None] < n_grp - 1))
    ovt = jnp.asarray(overlap.T.astype(np.float32), BF16)
    o_nsa = _nsa_call(qt, ksw.reshape(b, t, 2 * KV_WIDTH), vt, kc, vct, gt, nsa_gain.reshape(1, NSA_WIDTH), ovt)

    cw = jnp.concatenate([conv_w, conv_b[None, :], jnp.zeros((SUBLANES - CONV_WIDTH - 1, D_FF), F32)], axis=0)
    out = _ffn_call(x2, o_hg.reshape(n, HG_WIDTH), o_nsa.reshape(n, NSA_WIDTH),
                    w_o[:HG_WIDTH].astype(BF16), w_o[HG_WIDTH:].astype(BF16), ln2.reshape(1, d),
                    w_gate.astype(BF16), w_up.astype(BF16), w_down.astype(BF16), cw,
                    final_gain.reshape(1, d), t // FFN_TM)
    return out.reshape(b, t, d)


def kernel(x, positions, ln1_gain, w_in, hgrn_lb_param, hgrn_out_gain, cmp_pe_k, cmp_pe_v, cmp_k_w1, cmp_k_w2,
           cmp_v_w1, cmp_v_w2, nsa_out_gain, w_o, ln2_gain, ffn_w_gate, ffn_w_up, ffn_conv_w, ffn_conv_b,
           ffn_w_down, final_gain):
    depth = ln1_gain.shape[0]
    assert depth == 1, "the fused final norm assumes a single layer"
    lower_bounds = jnp.cumsum(jax.nn.softmax(hgrn_lb_param.astype(F32), axis=0), axis=0)
    l = 0
    return _layer(x, positions, ln1_gain[l], w_in[l], lower_bounds[l], hgrn_out_gain[l], cmp_pe_k[l], cmp_pe_v[l],
                  cmp_k_w1[l], cmp_k_w2[l], cmp_v_w1[l], cmp_v_w2[l], nsa_out_gain[l], w_o[l], ln2_gain[l],
                  ffn_w_gate[l], ffn_w_up[l], ffn_conv_w[l], ffn_conv_b[l], ffn_w_down[l], final_gain)
```

```python
import functools

import jax
import jax.numpy as jnp
import numpy as np
from jax import lax
from jax.experimental import pallas as pl
from jax.experimental.pallas import tpu as pltpu

F32 = jnp.float32
BF16 = jnp.bfloat16

D_MODEL = 1024
HG_HEADS = 4
HG_DK = 128
HG_DV = 128
HG_WIDTH = HG_HEADS * HG_DV
NSA_HEADS = 8
NSA_KV_HEADS = 2
NSA_HEAD_DIM = 64
HPG = NSA_HEADS // NSA_KV_HEADS
NSA_WIDTH = NSA_HEADS * NSA_HEAD_DIM
KV_WIDTH = NSA_KV_HEADS * NSA_HEAD_DIM
CMP_BLOCK = 32
CMP_STRIDE = 16
CMP_HIDDEN = 256
SLC_BLOCK = 64
SLC_TOPK = 16
WINDOW = 512
ROPE_THETA = 500000.0
ROPE_DIM = NSA_HEAD_DIM // 4
ROPE_HALF = ROPE_DIM // 2
D_FF = 2816
CONV_WIDTH = 3
EPS = 1e-6
NEG_INF = -1e30
FORCE_SCORE = 1e4
N_GATES = 3 * NSA_HEADS
LOG2_E = 1.4426950408889634

LANES = 128
SUBLANES = 8
VMEM_LIMIT = 56 * 1024 * 1024

PROJ_TM = 512
HG_CHUNK = 128
HG_LEVELS = (16, 32, 64)
HG_DIAG = 16
HG_TT = 512
ATT_TQ = 256
ATT_TK = 256
SLC_NEAR_TILES = 1
SUM_ROWS = 16
FFN_TM = 512
FFN_TC = 256
FFN_NC = D_FF // FFN_TC


def _dot(a, b):
    return jnp.dot(a, b, preferred_element_type=F32)


def _dot_nt(a, b):
    return lax.dot_general(a, b, (((1,), (1,)), ((), ())), preferred_element_type=F32)


def _dot_tn(a, b):
    return lax.dot_general(a, b, (((0,), (0,)), ((), ())), preferred_element_type=F32)


def _split3(x):
    hi = x.astype(BF16)
    r = x - hi.astype(F32)
    mid = r.astype(BF16)
    lo = (r - mid.astype(F32)).astype(BF16)
    return hi, mid, lo


def _rms(x, gain):
    return x * lax.rsqrt(jnp.mean(x * x, axis=-1, keepdims=True) + EPS) * gain


def _inproj_body(x_ref, g_ref, wh_ref, wk_ref, wt_ref, cs_ref,
                 hg_ref, kcn_ref, vcn_ref, ksw_ref, qt_ref, vt_ref, gt_ref):
    hb = _rms(x_ref[...], g_ref[...]).astype(BF16)
    hg_ref[...] = _dot(hb, wh_ref[...])

    def rope(v, axis, cos, sin_hi, sin_lo):
        return (v * cos + pltpu.roll(v, ROPE_HALF, axis) * sin_hi
                + pltpu.roll(v, LANES - ROPE_HALF, axis) * sin_lo)

    cos = cs_ref[0, 0:ROPE_HALF, :]
    sin = cs_ref[0, ROPE_HALF:ROPE_DIM, :]
    tm = cos.shape[1]
    zero_h = jnp.zeros((ROPE_HALF, tm), F32)
    rest = NSA_HEAD_DIM - ROPE_DIM
    slab = lambda lo, hi, fill: jnp.concatenate([lo, hi, jnp.full((rest, tm), fill, F32)] * (LANES // NSA_HEAD_DIM), axis=0)
    tab_t = (slab(cos, cos, 1.0), slab(zero_h, sin, 0.0), slab(-sin, zero_h, 0.0))
    tab = tuple(a.T for a in tab_t)
    kn = _dot(hb, wk_ref[...])
    kcn_ref[...] = rope(kn[:, 0:LANES], 1, *tab)
    vcn_ref[...] = kn[:, LANES:2 * LANES]
    ksw_ref[:, 0:LANES] = rope(kn[:, 2 * LANES:3 * LANES], 1, *tab).astype(BF16)
    ksw_ref[:, LANES:2 * LANES] = rope(kn[:, 3 * LANES:4 * LANES], 1, *tab).astype(BF16)

    rt = _dot_nt(wt_ref[...], hb)
    scale = NSA_HEAD_DIM ** -0.5 * LOG2_E
    for j in range(NSA_WIDTH // LANES):
        sl = slice(j * LANES, (j + 1) * LANES)
        qt_ref[0, sl, :] = (rope(rt[sl], 0, *tab_t) * scale).astype(BF16)
    vt_ref[0] = rt[NSA_WIDTH:NSA_WIDTH + 2 * KV_WIDTH].astype(BF16)
    gt_ref[0] = jax.nn.sigmoid(rt[NSA_WIDTH + 2 * KV_WIDTH:])


def _inproj_call(x2, gain, wh, wk, wt, cs, tiles_per_seq):
    n = x2.shape[0]
    tm = PROJ_TM
    t = tiles_per_seq * tm
    b = n // t
    row = lambda w: pl.BlockSpec((tm, w), lambda i: (i, 0))
    col = lambda h: pl.BlockSpec((1, h, tm), lambda i: (i // tiles_per_seq, 0, i % tiles_per_seq))
    full = lambda a: pl.BlockSpec(a.shape, lambda i: (0, 0))
    gate_rows = NSA_KV_HEADS * LANES
    return pl.pallas_call(
        _inproj_body,
        grid=(n // tm,),
        in_specs=[row(D_MODEL), full(gain), full(wh), full(wk), full(wt),
                  col(ROPE_DIM)],
        out_specs=[row(4 * HG_WIDTH), row(KV_WIDTH), row(KV_WIDTH), row(2 * KV_WIDTH),
                   col(NSA_WIDTH), col(2 * KV_WIDTH), col(gate_rows)],
        out_shape=[jax.ShapeDtypeStruct((n, 4 * HG_WIDTH), F32),
                   jax.ShapeDtypeStruct((n, KV_WIDTH), F32),
                   jax.ShapeDtypeStruct((n, KV_WIDTH), F32),
                   jax.ShapeDtypeStruct((n, 2 * KV_WIDTH), BF16),
                   jax.ShapeDtypeStruct((b, NSA_WIDTH, t), BF16),
                   jax.ShapeDtypeStruct((b, 2 * KV_WIDTH, t), BF16),
                   jax.ShapeDtypeStruct((b, gate_rows, t), F32)],
        compiler_params=pltpu.CompilerParams(dimension_semantics=("arbitrary",),
                                             vmem_limit_bytes=VMEM_LIMIT),
        name="inproj",
    )(x2, gain, wh, wk, wt, cs)


def _hgrn_tables():
    L = HG_CHUNK
    t = np.arange(L)[:, None]
    u = np.arange(L)[None, :]
    level = np.where(((t // HG_DIAG) == (u // HG_DIAG)) & (u <= t), 1, 0)
    for li, s in enumerate(HG_LEVELS):
        same = (t // (2 * s)) == (u // (2 * s))
        right = (t % (2 * s)) >= s
        level = np.where(same & right & ((u % (2 * s)) < s), li + 2, level)
    return jnp.asarray((u <= t).astype(np.float32), BF16), jnp.asarray(level, jnp.int32)


def _hgrn_body(q_ref, f_ref, i_ref, g_ref, lb_ref, gain_ref, mst_ref, lvl_ref, o_ref, st_ref):
    L = HG_CHUNK
    n_chunks = q_ref.shape[1] // L

    @pl.when(pl.program_id(1) == 0)
    def _sequence_start():
        st_ref[...] = jnp.zeros_like(st_ref)

    def chunk(c, carry):
        rows = pl.ds(pl.multiple_of(c * L, L), L)
        heads = range(HG_HEADS)
        cols = [slice(h * HG_DK, (h + 1) * HG_DK) for h in heads]
        mst = mst_ref[...]
        lvl = lvl_ref[...]
        n_lv = len(HG_LEVELS)
        row_i = lax.broadcasted_iota(jnp.int32, (L, HG_DK), 0)
        q = [q_ref[0, rows, cols[h]] for h in heads]
        vb = [i_ref[0, rows, cols[h]].astype(BF16) for h in heads]
        f = [lb_ref[:, cols[h]] + (1.0 - lb_ref[:, cols[h]]) * jax.nn.sigmoid(f_ref[0, rows, cols[h]]) for h in heads]
        k = [1.0 - f[h] for h in heads]
        parts = [_split3(jnp.log(f[h])) for h in heads]
        e_full = [(_dot(mst, parts[h][0]) + _dot(mst, parts[h][1])) + _dot(mst, parts[h][2]) for h in heads]
        b_last = [e_full[h][L - 1:L, :] for h in heads]

        def rel_to(b, blk, off):
            refs = []
            for r0 in range(0, L, blk):
                r = r0 + off - 1
                ref = b[r:r + 1, :] if r >= 0 else jnp.zeros((1, HG_DK), F32)
                refs.append(jnp.broadcast_to(ref, (blk, HG_DK)))
            return b - jnp.concatenate(refs, axis=0)

        def level_sums(b):
            out = [rel_to(b, HG_DIAG, 0)]
            for s_half in HG_LEVELS:
                d = rel_to(b, 2 * s_half, s_half)
                out.append(jnp.where((row_i % (2 * s_half)) >= s_half, d, -d))
            return out

        e = [level_sums(e_full[h]) for h in heads]
        wq = [[jnp.exp(e[h][l]) for l in range(n_lv + 1)] for h in heads]
        wk = [[jnp.exp(-e[h][0])] + wq[h][1:] for h in heads]
        prod = [[_dot_nt((q[h] * wq[h][l]).astype(BF16), (k[h] * wk[h][l]).astype(BF16)) for l in range(n_lv + 1)]
                for h in heads]
        st = [st_ref[h] for h in heads]
        inter = [_dot_nt((q[h] * jnp.exp(e_full[h])).astype(BF16), st[h].astype(BF16)) for h in heads]
        k_dec = [(k[h] * jnp.exp(b_last[h] - e_full[h])).astype(BF16) for h in heads]
        upd = [_dot_tn(vb[h], k_dec[h]) for h in heads]
        for h in heads:
            st_ref[h] = st[h] * jnp.exp(b_last[h]) + upd[h]
        a = []
        for h in heads:
            ah = jnp.where(lvl == 1, prod[h][0], 0.0)
            for l in range(1, n_lv + 1):
                ah = jnp.where(lvl == l + 1, prod[h][l], ah)
            a.append(ah.astype(BF16))
        o = [_dot(a[h], vb[h]) + inter[h] for h in heads]
        for h in heads:
            oh = o[h] * lax.rsqrt(jnp.mean(o[h] * o[h], axis=-1, keepdims=True) + EPS) * gain_ref[:, cols[h]]
            o_ref[0, rows, cols[h]] = (oh * jax.nn.silu(g_ref[0, rows, cols[h]])).astype(o_ref.dtype)
        return carry

    lax.fori_loop(0, n_chunks, chunk, 0, unroll=True)


def _hgrn_call(hg, lb, gain, mst, lvl):
    b, t, _ = hg.shape
    tt = HG_TT
    col = lambda k: pl.BlockSpec((1, tt, HG_WIDTH), lambda bi, ti: (bi, ti, k))
    full = lambda a: pl.BlockSpec(a.shape, lambda bi, ti: (0, 0))
    return pl.pallas_call(
        _hgrn_body,
        grid=(b, t // tt),
        in_specs=[col(0), col(1), col(2), col(3), full(lb), full(gain), full(mst), full(lvl)],
        out_specs=pl.BlockSpec((1, tt, HG_WIDTH), lambda bi, ti: (bi, ti, 0)),
        out_shape=jax.ShapeDtypeStruct((b, t, HG_WIDTH), BF16),
        scratch_shapes=[pltpu.VMEM((HG_HEADS, HG_DV, HG_DK), F32)],
        compiler_params=pltpu.CompilerParams(dimension_semantics=("arbitrary", "arbitrary"),
                                             vmem_limit_bytes=VMEM_LIMIT),
        name="hgrn2",
    )(hg, hg, hg, hg, lb, gain, mst, lvl)


def _cmp_body(kcn_ref, vcn_ref, pek_ref, pev_ref, w1k_ref, w1v_ref, w2k_ref, w2v_ref, kc_ref, vc_ref):
    nb = kcn_ref.shape[1] // CMP_STRIDE
    lane_grp = (lax.broadcasted_iota(jnp.int32, (nb, CMP_STRIDE * LANES), 1) // NSA_HEAD_DIM) % NSA_KV_HEADS

    def hidden(src_ref, pe_ref, w1_ref):
        x = jnp.concatenate([src_ref[0, pl.ds(l, nb, stride=CMP_STRIDE), :]
                             for l in range(CMP_STRIDE)], axis=1)
        halves = [x + pe_ref[i] for i in range(2)]
        out = []
        for g in range(NSA_KV_HEADS):
            u, v = (_dot(jnp.where(lane_grp == g, halves[i], 0.0).astype(BF16), w1_ref[i]) for i in range(2))
            out.append(jax.nn.silu(u + pltpu.roll(v, nb - 1, 0)).astype(BF16))
        return out

    hk = hidden(kcn_ref, pek_ref, w1k_ref)
    hv = hidden(vcn_ref, pev_ref, w1v_ref)
    for g in range(NSA_KV_HEADS):
        kc_ref[0, g, 0:nb, :] = _dot(hk[g], w2k_ref[0]).astype(kc_ref.dtype)
        kc_ref[0, g, nb:2 * nb, :] = _dot(hk[g], w2k_ref[1]).astype(kc_ref.dtype)
        vc_ref[0, g, :, 0:nb] = _dot_nt(w2v_ref[0], hv[g]).astype(vc_ref.dtype)
        vc_ref[0, g, :, nb:2 * nb] = _dot_nt(w2v_ref[1], hv[g]).astype(vc_ref.dtype)


def _cmp_call(kcn, vcn, pek, pev, w1k, w1v, w2k, w2v):
    b, t, w = kcn.shape
    nb = t // CMP_STRIDE
    full = lambda a: pl.BlockSpec(a.shape, lambda bi: (0,) * a.ndim)
    out = lambda r, c: pl.BlockSpec((1, NSA_KV_HEADS, r, c), lambda bi: (bi, 0, 0, 0))
    return pl.pallas_call(
        _cmp_body,
        grid=(b,),
        in_specs=[pl.BlockSpec((1, t, w), lambda bi: (bi, 0, 0)), pl.BlockSpec((1, t, w), lambda bi: (bi, 0, 0)),
                  full(pek), full(pev), full(w1k), full(w1v), full(w2k), full(w2v)],
        out_specs=[out(2 * nb, LANES), out(LANES, 2 * nb)],
        out_shape=[jax.ShapeDtypeStruct((b, NSA_KV_HEADS, 2 * nb, LANES), BF16),
                   jax.ShapeDtypeStruct((b, NSA_KV_HEADS, LANES, 2 * nb), BF16)],
        compiler_params=pltpu.CompilerParams(dimension_semantics=("arbitrary",),
                                             vmem_limit_bytes=VMEM_LIMIT),
        name="nsa_compress",
    )(kcn, vcn, pek, pev, w1k, w1v, w2k, w2v)


def _nsa_body(qt_ref, ksw_ref, vst_ref, vwt_ref, kc_ref, vct_ref, gt_ref, gain_ref, ovt_ref, o_ref,
              ks_ref, kw_ref, vs_ref, vw_ref, m_ref, acc_ref, s_ref, ch_ref):
    g = pl.program_id(1)
    qi = pl.program_id(2)
    tq = ATT_TQ
    tk = ATT_TK
    t_len = ksw_ref.shape[1]
    n_kt = t_len // tk
    n_pairs = HPG // 2
    hd = NSA_HEAD_DIM

    @pl.when(qi == 0)
    def _build_kv():
        lane = lax.broadcasted_iota(jnp.int32, (tk, LANES), 1)
        lo_lane = lane < hd
        keep = (lane // hd) == g

        def build_k(src_col, dst_ref):
            def body(j, carry):
                rows = pl.ds(pl.multiple_of(j * tk, tk), tk)
                x = ksw_ref[0, rows, src_col * LANES:(src_col + 1) * LANES].astype(F32)
                dup = jnp.where(keep, x, pltpu.roll(x, hd, 1))
                dst_ref[j, 0:tk, :] = jnp.where(lo_lane, dup, 0.0).astype(BF16)
                dst_ref[j, tk:2 * tk, :] = jnp.where(lo_lane, 0.0, dup).astype(BF16)
                return carry
            lax.fori_loop(0, n_kt, body, 0)

        def build_v(src_ref, dst_ref):
            zero = jnp.zeros((hd, tk), BF16)
            row = lax.broadcasted_iota(jnp.int32, (SUM_ROWS, 2 * tk), 0)
            col = lax.broadcasted_iota(jnp.int32, (SUM_ROWS, 2 * tk), 1)
            ones_rows = jnp.where(((row == 0) & (col < tk)) | ((row == 1) & (col >= tk)), 1.0, 0.0).astype(BF16)
            for j in range(n_kt):
                x = src_ref[0, :, j * tk:(j + 1) * tk]
                dst_ref[j, 0:hd, 0:tk] = x
                dst_ref[j, 0:hd, tk:2 * tk] = zero
                dst_ref[j, hd:2 * hd, 0:tk] = zero
                dst_ref[j, hd:2 * hd, tk:2 * tk] = x
                dst_ref[j, 2 * hd:2 * hd + SUM_ROWS, :] = ones_rows

        build_k(0, ks_ref)
        build_k(1, kw_ref)
        build_v(vst_ref, vs_ref)
        build_v(vwt_ref, vw_ref)

    t0 = qi * tq
    key_i = lax.broadcasted_iota(jnp.int32, (tk, tq), 0)
    qry_t = t0 + lax.broadcasted_iota(jnp.int32, (tk, tq), 1)
    slab_lo = lax.broadcasted_iota(jnp.int32, (LANES, tq), 0) < hd
    acc_row = lax.broadcasted_iota(jnp.int32, (LANES + SUM_ROWS, tq), 0)
    slab_a = (acc_row < hd) | (acc_row == LANES)
    q_pairs = [qt_ref[0, p * LANES:(p + 1) * LANES, :] for p in range(n_pairs)]

    last = (t0 + tq - 1) // tk

    def scores(k_ref, j):
        kt = k_ref[j]
        return [_dot(kt, q_pairs[p]) for p in range(n_pairs)]

    n_cmp_pad = kc_ref.shape[2] // 2
    blk_i = lax.broadcasted_iota(jnp.int32, (n_cmp_pad, tq), 0)
    blk_t = t0 + lax.broadcasted_iota(jnp.int32, (n_cmp_pad, tq), 1)
    cmp_ok = (blk_i * CMP_STRIDE + (CMP_BLOCK - 1)) <= blk_t
    kc = kc_ref[0, 0]
    vct = vct_ref[0, 0]
    s_cmp = [_dot(kc, q_pairs[p]) for p in range(n_pairs)]
    n_win = (WINDOW + tq) // tk
    win_tiles = [last - (n_win - 1) + k for k in range(n_win)]
    n_near = SLC_NEAR_TILES
    near_tiles = [last - k for k in range(n_near)]
    up_front = ([(ks_ref, 0)] + [(kw_ref, jnp.maximum(jw, 0)) for jw in win_tiles]
                + [(ks_ref, jnp.maximum(jn, 0)) for jn in near_tiles])
    for slot, (k_ref, j0) in enumerate(up_front):
        s_first = scores(k_ref, j0)
        for p in range(n_pairs):
            s_ref[slot, p] = s_first[p]
    p_sum = jnp.zeros((n_cmp_pad, tq), F32)
    p_cmp = []
    for p in range(n_pairs):
        probs = []
        for h in range(2):
            sh = jnp.where(cmp_ok, s_cmp[p][h * n_cmp_pad:(h + 1) * n_cmp_pad], NEG_INF)
            mh = jnp.max(sh, axis=0, keepdims=True)
            eh = jnp.where(cmp_ok, jnp.exp2(sh - mh), 0.0)
            den = jnp.sum(eh, axis=0, keepdims=True)
            ph = eh / jnp.where(den > 0.0, den, 1.0)
            p_sum = p_sum + ph
            probs.append(ph.astype(BF16))
        p_cmp.append(jnp.concatenate(probs, axis=0))
    o_cmp = [_dot(vct, p_cmp[p]) for p in range(n_pairs)]

    n_sel = t_len // SLC_BLOCK
    hi, mid, lo = _split3(p_sum)
    ovt = ovt_ref[...]
    p_sel = ((_dot(ovt, hi) + _dot(ovt, mid)) + _dot(ovt, lo))[0:n_sel]
    sel_i = lax.broadcasted_iota(jnp.int32, (n_sel, tq), 0)
    cur = (t0 + lax.broadcasted_iota(jnp.int32, (n_sel, tq), 1)) // SLC_BLOCK
    forced = (sel_i == 0) | (sel_i == cur) | (sel_i == cur - 1)
    score = jnp.where(forced, FORCE_SCORE, p_sel)
    score = jnp.where(sel_i <= cur, score, -jnp.inf)
    rank = jnp.zeros((n_sel, tq), jnp.int32)
    row_grp = SUBLANES
    grp_i = lax.broadcasted_iota(jnp.int32, (row_grp, tq), 0)
    for i in range(n_sel):
        ci = score[i:i + 1, :]
        ahead = []
        for r0 in range(0, n_sel, row_grp):
            rows = slice(r0, r0 + row_grp)
            if r0 > i:
                ahead.append(ci >= score[rows])
            elif r0 + row_grp <= i:
                ahead.append(ci > score[rows])
            else:
                ahead.append((ci > score[rows]) | ((ci == score[rows]) & (grp_i > i - r0)))
        rank = rank + jnp.where(jnp.concatenate(ahead, axis=0), 1, 0)
    chosen = jnp.where(rank < min(SLC_TOPK, n_sel), 1.0, 0.0)
    for i in range(n_sel):
        ch_ref[i] = chosen[i:i + 1, :]

    def tile_softmax(slot, bias, m_get, m_put):
        alphas = {}
        probs = {}
        for hh in range(HPG):
            p, h = divmod(hh, 2)
            a_parts = []
            p_parts = []
            for qh in range(tq // LANES):
                ql = slice(qh * LANES, (qh + 1) * LANES)
                sh = s_ref[slot, p, h * tk:(h + 1) * tk, ql] + (bias if bias.shape == (1, 1) else bias[:, ql])
                m_prev = m_get(hh, qh)
                m_new = jnp.maximum(m_prev, jnp.max(sh, axis=0, keepdims=True))
                m_put(hh, qh, m_new)
                p_parts.append(jnp.exp2(sh - m_new).astype(BF16))
                a_parts.append(jnp.exp2(m_prev - m_new))
            alphas[hh] = jnp.concatenate(a_parts, axis=1)
            probs[hh] = jnp.concatenate(p_parts, axis=1)
        return ([jnp.concatenate([probs[2 * p], probs[2 * p + 1]], axis=0) for p in range(n_pairs)],
                [jnp.where(slab_a, alphas[2 * p], alphas[2 * p + 1]) for p in range(n_pairs)])

    def normalised(acc):
        inv = jnp.where(slab_lo, 1.0 / acc[LANES:LANES + 1, :], 1.0 / acc[LANES + 1:LANES + 2, :])
        return acc[0:LANES] * inv

    m_win = {}
    acc_win = [jnp.zeros((LANES + SUM_ROWS, tq), F32) for _ in range(n_pairs)]
    for k, jw in enumerate(win_tiles):
        rel_hi = (n_win - k) * tk - 1
        rel_lo = rel_hi - (tq - 1) - (tk - 1)
        if rel_lo >= 0 and rel_hi < WINDOW:
            bias = jnp.where(jw >= 0, 0.0, NEG_INF).astype(F32).reshape(1, 1)
        else:
            key_t = jw * tk + key_i
            rel = qry_t - key_t
            bias = jnp.where((key_t >= 0) & (rel >= 0) & (rel < WINDOW), 0.0, NEG_INF)
        probs, a_rows = tile_softmax(1 + k, bias,
                                     lambda hh, qh: m_win.get((hh, qh), jnp.full((1, LANES), -jnp.inf, F32)),
                                     lambda hh, qh, v: m_win.__setitem__((hh, qh), v))
        vt = vw_ref[jnp.maximum(jw, 0)]
        acc_win = [acc_win[p] * a_rows[p] + _dot(vt, probs[p]) for p in range(n_pairs)]
    o_win = [normalised(acc_win[p]) for p in range(n_pairs)]

    def picked_bias(j, also=None):
        per_tile = tk // SLC_BLOCK
        picked = jnp.concatenate([jnp.broadcast_to(ch_ref[j * per_tile + i], (SLC_BLOCK, tq)) for i in range(per_tile)],
                                 axis=0) > 0.5
        return jnp.where(picked if also is None else picked & also, 0.0, NEG_INF)

    m_near = {}
    acc_near = [jnp.zeros((LANES + SUM_ROWS, tq), F32) for _ in range(n_pairs)]
    for k, jn in enumerate(near_tiles):
        key_t = jn * tk + key_i
        jc = jnp.maximum(jn, 0)
        probs, a_rows = tile_softmax(1 + n_win + k, picked_bias(jc, (key_t >= 0) & (key_t <= qry_t)),
                                     lambda hh, qh: m_near.get((hh, qh), jnp.full((1, LANES), -jnp.inf, F32)),
                                     lambda hh, qh, v: m_near.__setitem__((hh, qh), v))
        vt = vs_ref[jc]
        acc_near = [acc_near[p] * a_rows[p] + _dot(vt, probs[p]) for p in range(n_pairs)]

    def m_put(hh, qh, v):
        m_ref[hh, :, qh * LANES:(qh + 1) * LANES] = v

    for (hh, qh), v in m_near.items():
        m_put(hh, qh, v)
    for p in range(n_pairs):
        acc_ref[p] = acc_near[p]
    n_far = jnp.maximum(last + 1 - n_near, 0)

    def slc_step(j, carry):
        s_next = scores(ks_ref, jnp.minimum(j + 1, n_far - 1))
        probs, a_rows = tile_softmax(0, picked_bias(j), lambda hh, qh: m_ref[hh, :, qh * LANES:(qh + 1) * LANES], m_put)
        vt = vs_ref[j]
        for p in range(n_pairs):
            pv = _dot(vt, probs[p])
            s_ref[0, p] = s_next[p]
            acc_ref[p] = acc_ref[p] * a_rows[p] + pv
        return carry

    lax.fori_loop(0, n_far, slc_step, 0)
    o_slc = [normalised(acc_ref[p]) for p in range(n_pairs)]

    gates = gt_ref[0]
    gain = gain_ref[...]
    for p in range(n_pairs):
        o = jnp.zeros((LANES, tq), F32)
        for c, branch in enumerate((o_cmp[p], o_slc[p], o_win[p])):
            r = c * HPG + 2 * p
            o = o + jnp.where(slab_lo, gates[r:r + 1, :], gates[r + 1:r + 2, :]) * branch
        sq = o * o
        ms_a = jnp.sum(sq[0:hd], axis=0, keepdims=True)
        ms_b = jnp.sum(sq[hd:2 * hd], axis=0, keepdims=True)
        ms = jnp.where(slab_lo, ms_a, ms_b) * (1.0 / hd)
        o = o * lax.rsqrt(ms + EPS)
        o_ref[0, p * LANES:(p + 1) * LANES, :] = (o * gain[p * LANES:(p + 1) * LANES, :]).astype(o_ref.dtype)


def _nsa_call(qt, ksw, vt, kc, vct, gt, gain, ovt):
    b, _, t = qt.shape
    tq, tk = ATT_TQ, ATT_TK
    n_kt = t // tk
    gw = HPG * NSA_HEAD_DIM
    hd = NSA_HEAD_DIM
    k_scratch = pltpu.VMEM((n_kt, 2 * tk, LANES), BF16)
    v_scratch = pltpu.VMEM((n_kt, LANES + SUM_ROWS, 2 * tk), BF16)
    return pl.pallas_call(
        _nsa_body,
        grid=(b, NSA_KV_HEADS, t // tq),
        in_specs=[
            pl.BlockSpec((1, gw, tq), lambda bi, gi, qi: (bi, gi, qi)),
            pl.BlockSpec((1, t, 2 * KV_WIDTH), lambda bi, gi, qi: (bi, 0, 0)),
            pl.BlockSpec((1, hd, t), lambda bi, gi, qi: (bi, gi, 0)),
            pl.BlockSpec((1, hd, t), lambda bi, gi, qi: (bi, NSA_KV_HEADS + gi, 0)),
            pl.BlockSpec((1, 1) + kc.shape[2:], lambda bi, gi, qi: (bi, gi, 0, 0)),
            pl.BlockSpec((1, 1) + vct.shape[2:], lambda bi, gi, qi: (bi, gi, 0, 0)),
            pl.BlockSpec((1, LANES, tq), lambda bi, gi, qi: (bi, gi, qi)),
            pl.BlockSpec((gw, 1), lambda bi, gi, qi: (gi, 0)),
            pl.BlockSpec(ovt.shape, lambda bi, gi, qi: (0, 0)),
        ],
        out_specs=pl.BlockSpec((1, gw, tq), lambda bi, gi, qi: (bi, gi, qi)),
        out_shape=jax.ShapeDtypeStruct((b, NSA_WIDTH, t), BF16),
        scratch_shapes=[k_scratch, k_scratch, v_scratch, v_scratch,
                        pltpu.VMEM((HPG, 1, tq), F32),
                        pltpu.VMEM((HPG // 2, LANES + SUM_ROWS, tq), F32), pltpu.VMEM((1 + (WINDOW + tq) // tk + SLC_NEAR_TILES, HPG // 2, 2 * tk, tq), F32),
                        pltpu.VMEM((t // SLC_BLOCK, 1, tq), F32)],
        compiler_params=pltpu.CompilerParams(dimension_semantics=("arbitrary", "arbitrary", "arbitrary"),
                                             vmem_limit_bytes=VMEM_LIMIT),
        name="nsa_attention",
    )(qt, ksw, vt, vt, kc, vct, gt, gain, ovt)


def _ffn_body(x_ref, oh_ref, on_ref, woh_ref, won_ref, g2_ref, wg_ref, wu_ref, wd_ref, cw_ref, gf_ref,
              out_ref, halo_ref, act_ref, *, tiles_per_seq):
    tm = x_ref.shape[0]
    x1 = x_ref[...] + _dot(oh_ref[...], woh_ref[...]) + _dot_tn(on_ref[0], won_ref[...])
    hb = _rms(x1, g2_ref[...]).astype(BF16)
    row = lax.broadcasted_iota(jnp.int32, (tm, FFN_TC), 0)

    @pl.when((pl.program_id(0) % tiles_per_seq) == 0)
    def _sequence_start():
        halo_ref[...] = jnp.zeros_like(halo_ref)

    def activation(c, gate, up):
        cols = slice(c * FFN_TC, (c + 1) * FFN_TC)
        halo = halo_ref[:, cols]
        halo_ref[:, cols] = gate[tm - SUBLANES:tm, :]
        last1 = halo[SUBLANES - 1:SUBLANES, :]
        last2 = halo[SUBLANES - 2:SUBLANES - 1, :]
        prev1 = jnp.where(row == 0, last1, pltpu.roll(gate, 1, 0))
        prev2 = jnp.where(row == 0, last2, jnp.where(row == 1, last1, pltpu.roll(gate, 2, 0)))
        cw = cw_ref[:, cols]
        y = cw[0:1, :] * prev2 + cw[1:2, :] * prev1 + cw[2:3, :] * gate + cw[3:4, :]
        return (jax.nn.silu(y) * up).astype(BF16)

    chunk = lambda w_ref, c: _dot(hb, w_ref[:, c * FFN_TC:(c + 1) * FFN_TC])
    gate_up = (chunk(wg_ref, 0), chunk(wu_ref, 0))
    for c in range(FFN_NC):
        cur = gate_up
        if c + 1 < FFN_NC:
            gate_up = (chunk(wg_ref, c + 1), chunk(wu_ref, c + 1))
        act_ref[:, c * FFN_TC:(c + 1) * FFN_TC] = activation(c, *cur)
    acc = _dot(act_ref[...], wd_ref[...])
    out_ref[...] = _rms(x1 + acc, gf_ref[...])


def _ffn_call(x2, oh, on, woh, won, g2, wg, wu, wd, cw, gf, tiles_per_seq):
    n = x2.shape[0]
    tm = FFN_TM
    row = lambda w: pl.BlockSpec((tm, w), lambda i: (i, 0))
    full = lambda a: pl.BlockSpec(a.shape, lambda i: (0,) * a.ndim, pipeline_mode=pl.Buffered(1))
    return pl.pallas_call(
        functools.partial(_ffn_body, tiles_per_seq=tiles_per_seq),
        grid=(n // tm,),
        in_specs=[row(D_MODEL), row(HG_WIDTH),
                  pl.BlockSpec((1, NSA_WIDTH, tm), lambda i: (i // tiles_per_seq, 0, i % tiles_per_seq)),
                  full(woh), full(won), full(g2),
                  full(wg), full(wu), full(wd), full(cw), full(gf)],
        out_specs=row(D_MODEL),
        out_shape=jax.ShapeDtypeStruct((n, D_MODEL), F32),
        scratch_shapes=[pltpu.VMEM((SUBLANES, D_FF), F32), pltpu.VMEM((tm, D_FF), BF16)],
        compiler_params=pltpu.CompilerParams(dimension_semantics=("arbitrary",),
                                             vmem_limit_bytes=VMEM_LIMIT),
        name="outproj_convffn",
    )(x2, oh, on, woh, won, g2, wg, wu, wd, cw, gf)


def _rope_angles(positions):
    inv_freq = ROPE_THETA ** (-jnp.arange(ROPE_HALF, dtype=F32) * 2.0 / ROPE_DIM)
    ang = positions.astype(F32)[..., None] * inv_freq
    return jnp.concatenate([jnp.cos(ang), jnp.sin(ang)], axis=-1).transpose(0, 2, 1)


def _layer(x, positions, ln1, w_in, lb, hg_gain, pe_k, pe_v, k_w1, k_w2, v_w1, v_w2, nsa_gain, w_o, ln2,
           w_gate, w_up, conv_w, conv_b, w_down, final_gain):
    b, t, d = x.shape
    n = b * t
    assert d == D_MODEL and t % FFN_TM == 0 and t % PROJ_TM == 0 and t % ATT_TQ == 0 and t % HG_TT == 0
    n_grp = t // CMP_STRIDE
    assert n_grp == LANES, "compressed-block axis is laid out on exactly one lane tile"
    n_sel = t // SLC_BLOCK
    assert n_sel % 8 == 0 and n_sel <= LANES and ATT_TK % SLC_BLOCK == 0
    x2 = x.reshape(n, d)

    splits = np.cumsum([0, 4 * HG_WIDTH, NSA_WIDTH] + [KV_WIDTH] * 6 + [N_GATES])
    seg = lambda i: w_in[:, splits[i]:splits[i + 1]]
    wh = seg(0).astype(BF16)
    wk = jnp.concatenate([seg(2), seg(3), seg(4), seg(6)], axis=1).astype(BF16)
    wgate = seg(8).reshape(d, 3, NSA_KV_HEADS, HPG).transpose(0, 2, 1, 3).reshape(d, NSA_KV_HEADS, 3 * HPG)
    wgate = jnp.pad(wgate, ((0, 0), (0, 0), (0, LANES - 3 * HPG))).reshape(d, NSA_KV_HEADS * LANES)
    wt = jnp.concatenate([seg(1), seg(5), seg(7), wgate], axis=1).T.astype(BF16)
    cs = _rope_angles(positions)

    hg, kcn, vcn, ksw, qt, vt, gt = _inproj_call(x2, ln1.reshape(1, d), wh, wk, wt, cs, t // PROJ_TM)

    mst, lvl = _hgrn_tables()
    o_hg = _hgrn_call(hg.reshape(b, t, 4 * HG_WIDTH), lb.reshape(1, HG_WIDTH).astype(F32),
                      hg_gain.reshape(1, HG_WIDTH), mst, lvl)

    per_lane = lambda a: jnp.broadcast_to(a.reshape(2, CMP_STRIDE, 1, NSA_HEAD_DIM, -1),
                                          (2, CMP_STRIDE, NSA_KV_HEADS, NSA_HEAD_DIM, a.shape[-1]))
    w1_rows = lambda w1: per_lane(w1).reshape(2, CMP_STRIDE * LANES, CMP_HIDDEN).astype(BF16)
    pe_rows = lambda pe: per_lane(pe[..., None]).reshape(2, 1, CMP_STRIDE * LANES)
    zeros_w2 = jnp.zeros((CMP_HIDDEN, NSA_HEAD_DIM), F32)
    place = lambda w2: jnp.stack([jnp.concatenate([w2, zeros_w2], 1), jnp.concatenate([zeros_w2, w2], 1)])
    kc, vct = _cmp_call(kcn.reshape(b, t, KV_WIDTH), vcn.reshape(b, t, KV_WIDTH), pe_rows(pe_k), pe_rows(pe_v),
                        w1_rows(k_w1), w1_rows(v_w1),
                        place(k_w2).astype(BF16), place(v_w2).transpose(0, 2, 1).astype(BF16))

    cmp_start = np.arange(n_grp) * CMP_STRIDE
    cmp_end = cmp_start + CMP_BLOCK - 1
    sel_start = np.arange(LANES) * SLC_BLOCK
    overlap = ((cmp_start[:, None] <= sel_start[None, :] + SLC_BLOCK - 1) & (cmp_end[:, None] >= sel_start[None, :])
               & (np.arange(LANES)[None, :] < n_sel) & (np.arange(n_grp)[:, None] < n_grp - 1))
    ovt = jnp.asarray(overlap.T.astype(np.float32), BF16)
    o_nsa = _nsa_call(qt, ksw.reshape(b, t, 2 * KV_WIDTH), vt, kc, vct, gt, nsa_gain.reshape(NSA_WIDTH, 1), ovt)

    cw = jnp.concatenate([conv_w, conv_b[None, :], jnp.zeros((SUBLANES - CONV_WIDTH - 1, D_FF), F32)], axis=0)
    out = _ffn_call(x2, o_hg.reshape(n, HG_WIDTH), o_nsa,
                    w_o[:HG_WIDTH].astype(BF16), w_o[HG_WIDTH:].astype(BF16), ln2.reshape(1, d),
                    w_gate.astype(BF16), w_up.astype(BF16), w_down.astype(BF16), cw,
                    final_gain.reshape(1, d), t // FFN_TM)
    return out.reshape(b, t, d)


def kernel(x, positions, ln1_gain, w_in, hgrn_lb_param, hgrn_out_gain, cmp_pe_k, cmp_pe_v, cmp_k_w1, cmp_k_w2,
           cmp_v_w1, cmp_v_w2, nsa_out_gain, w_o, ln2_gain, ffn_w_gate, ffn_w_up, ffn_conv_w, ffn_conv_b,
           ffn_w_down, final_gain):
    depth = ln1_gain.shape[0]
    assert depth == 1, "the fused final norm assumes a single layer"
    lower_bounds = jnp.cumsum(jax.nn.softmax(hgrn_lb_param.astype(F32), axis=0), axis=0)
    l = 0
    return _layer(x, positions, ln1_gain[l], w_in[l], lower_bounds[l], hgrn_out_gain[l], cmp_pe_k[l], cmp_pe_v[l],
                  cmp_k_w1[l], cmp_k_w2[l], cmp_v_w1[l], cmp_v_w2[l], nsa_out_gain[l], w_o[l], ln2_gain[l],
                  ffn_w_gate[l], ffn_w_up[l], ffn_conv_w[l], ffn_conv_b[l], ffn_w_down[l], final_gain)
```

```python
import functools

import jax
import jax.numpy as jnp
import numpy as np
from jax import lax
from jax.experimental import pallas as pl
from jax.experimental.pallas import tpu as pltpu

F32 = jnp.float32
BF16 = jnp.bfloat16

D_MODEL = 1024
HG_HEADS = 4
HG_DK = 128
HG_DV = 128
HG_WIDTH = HG_HEADS * HG_DV
NSA_HEADS = 8
NSA_KV_HEADS = 2
NSA_HEAD_DIM = 64
HPG = NSA_HEADS // NSA_KV_HEADS
NSA_WIDTH = NSA_HEADS * NSA_HEAD_DIM
KV_WIDTH = NSA_KV_HEADS * NSA_HEAD_DIM
CMP_BLOCK = 32
CMP_STRIDE = 16
CMP_HIDDEN = 256
SLC_BLOCK = 64
SLC_TOPK = 16
WINDOW = 512
ROPE_THETA = 500000.0
ROPE_DIM = NSA_HEAD_DIM // 4
ROPE_HALF = ROPE_DIM // 2
D_FF = 2816
CONV_WIDTH = 3
EPS = 1e-6
NEG_INF = -1e30
FORCE_SCORE = 1e4
N_GATES = 3 * NSA_HEADS
LOG2_E = 1.4426950408889634

LANES = 128
SUBLANES = 8
VMEM_LIMIT = 56 * 1024 * 1024

PROJ_TM = 512
HG_CHUNK = 128
HG_LEVELS = (16, 32, 64)
HG_DIAG = 16
HG_TT = 512
ATT_TQ = 256
ATT_TK = 256
SLC_NEAR_TILES = 1
SUM_ROWS = 16
FFN_TM = 512
FFN_TC = 256
FFN_NC = D_FF // FFN_TC


def _dot(a, b):
    return jnp.dot(a, b, preferred_element_type=F32)


def _dot_nt(a, b):
    return lax.dot_general(a, b, (((1,), (1,)), ((), ())), preferred_element_type=F32)


def _dot_tn(a, b):
    return lax.dot_general(a, b, (((0,), (0,)), ((), ())), preferred_element_type=F32)


def _split3(x):
    hi = x.astype(BF16)
    r = x - hi.astype(F32)
    mid = r.astype(BF16)
    lo = (r - mid.astype(F32)).astype(BF16)
    return hi, mid, lo


def _rms(x, gain):
    return x * lax.rsqrt(jnp.mean(x * x, axis=-1, keepdims=True) + EPS) * gain


def _inproj_body(x_ref, g_ref, wh_ref, wk_ref, wt_ref, cs_ref,
                 hg_ref, kcn_ref, vcn_ref, ksw_ref, qt_ref, vt_ref, gt_ref):
    hb = _rms(x_ref[...], g_ref[...]).astype(BF16)
    hg_ref[...] = _dot(hb, wh_ref[...])

    def rope(v, axis, cos, sin_hi, sin_lo):
        return (v * cos + pltpu.roll(v, ROPE_HALF, axis) * sin_hi
                + pltpu.roll(v, LANES - ROPE_HALF, axis) * sin_lo)

    cos = cs_ref[0, 0:ROPE_HALF, :]
    sin = cs_ref[0, ROPE_HALF:ROPE_DIM, :]
    tm = cos.shape[1]
    zero_h = jnp.zeros((ROPE_HALF, tm), F32)
    rest = NSA_HEAD_DIM - ROPE_DIM
    slab = lambda lo, hi, fill: jnp.concatenate([lo, hi, jnp.full((rest, tm), fill, F32)] * (LANES // NSA_HEAD_DIM), axis=0)
    tab_t = (slab(cos, cos, 1.0), slab(zero_h, sin, 0.0), slab(-sin, zero_h, 0.0))
    tab = tuple(a.T for a in tab_t)
    kn = _dot(hb, wk_ref[...])
    kcn_ref[...] = rope(kn[:, 0:LANES], 1, *tab)
    vcn_ref[...] = kn[:, LANES:2 * LANES]
    ksw_ref[:, 0:LANES] = rope(kn[:, 2 * LANES:3 * LANES], 1, *tab).astype(BF16)
    ksw_ref[:, LANES:2 * LANES] = rope(kn[:, 3 * LANES:4 * LANES], 1, *tab).astype(BF16)

    rt = _dot_nt(wt_ref[...], hb)
    scale = NSA_HEAD_DIM ** -0.5 * LOG2_E
    for j in range(NSA_WIDTH // LANES):
        sl = slice(j * LANES, (j + 1) * LANES)
        qt_ref[0, sl, :] = (rope(rt[sl], 0, *tab_t) * scale).astype(BF16)
    vt_ref[0] = rt[NSA_WIDTH:NSA_WIDTH + 2 * KV_WIDTH].astype(BF16)
    gt_ref[0] = jax.nn.sigmoid(rt[NSA_WIDTH + 2 * KV_WIDTH:])


def _inproj_call(x2, gain, wh, wk, wt, cs, tiles_per_seq):
    n = x2.shape[0]
    tm = PROJ_TM
    t = tiles_per_seq * tm
    b = n // t
    row = lambda w: pl.BlockSpec((tm, w), lambda i: (i, 0))
    col = lambda h: pl.BlockSpec((1, h, tm), lambda i: (i // tiles_per_seq, 0, i % tiles_per_seq))
    full = lambda a: pl.BlockSpec(a.shape, lambda i: (0, 0))
    gate_rows = NSA_KV_HEADS * LANES
    return pl.pallas_call(
        _inproj_body,
        grid=(n // tm,),
        in_specs=[row(D_MODEL), full(gain), full(wh), full(wk), full(wt),
                  col(ROPE_DIM)],
        out_specs=[row(4 * HG_WIDTH), row(KV_WIDTH), row(KV_WIDTH), row(2 * KV_WIDTH),
                   col(NSA_WIDTH), col(2 * KV_WIDTH), col(gate_rows)],
        out_shape=[jax.ShapeDtypeStruct((n, 4 * HG_WIDTH), F32),
                   jax.ShapeDtypeStruct((n, KV_WIDTH), F32),
                   jax.ShapeDtypeStruct((n, KV_WIDTH), F32),
                   jax.ShapeDtypeStruct((n, 2 * KV_WIDTH), BF16),
                   jax.ShapeDtypeStruct((b, NSA_WIDTH, t), BF16),
                   jax.ShapeDtypeStruct((b, 2 * KV_WIDTH, t), BF16),
                   jax.ShapeDtypeStruct((b, gate_rows, t), F32)],
        compiler_params=pltpu.CompilerParams(dimension_semantics=("arbitrary",),
                                             vmem_limit_bytes=VMEM_LIMIT),
        name="inproj",
    )(x2, gain, wh, wk, wt, cs)


def _hgrn_tables():
    L = HG_CHUNK
    t = np.arange(L)[:, None]
    u = np.arange(L)[None, :]
    level = np.where(((t // HG_DIAG) == (u // HG_DIAG)) & (u <= t), 1, 0)
    for li, s in enumerate(HG_LEVELS):
        same = (t // (2 * s)) == (u // (2 * s))
        right = (t % (2 * s)) >= s
        level = np.where(same & right & ((u % (2 * s)) < s), li + 2, level)
    return jnp.asarray((u <= t).astype(np.float32), BF16), jnp.asarray(level, jnp.int32)


def _hgrn_body(q_ref, f_ref, i_ref, g_ref, lb_ref, gain_ref, mst_ref, lvl_ref, o_ref, st_ref):
    L = HG_CHUNK
    n_chunks = q_ref.shape[1] // L

    @pl.when(pl.program_id(1) == 0)
    def _sequence_start():
        st_ref[...] = jnp.zeros_like(st_ref)

    def chunk(c, carry):
        rows = pl.ds(pl.multiple_of(c * L, L), L)
        heads = range(HG_HEADS)
        cols = [slice(h * HG_DK, (h + 1) * HG_DK) for h in heads]
        mst = mst_ref[...]
        lvl = lvl_ref[...]
        n_lv = len(HG_LEVELS)
        row_i = lax.broadcasted_iota(jnp.int32, (L, HG_DK), 0)
        q = [q_ref[0, rows, cols[h]] for h in heads]
        vb = [i_ref[0, rows, cols[h]].astype(BF16) for h in heads]
        f = [lb_ref[:, cols[h]] + (1.0 - lb_ref[:, cols[h]]) * jax.nn.sigmoid(f_ref[0, rows, cols[h]]) for h in heads]
        k = [1.0 - f[h] for h in heads]
        parts = [_split3(jnp.log(f[h])) for h in heads]
        e_full = [(_dot(mst, parts[h][0]) + _dot(mst, parts[h][1])) + _dot(mst, parts[h][2]) for h in heads]
        b_last = [e_full[h][L - 1:L, :] for h in heads]

        def rel_to(b, blk, off):
            refs = []
            for r0 in range(0, L, blk):
                r = r0 + off - 1
                ref = b[r:r + 1, :] if r >= 0 else jnp.zeros((1, HG_DK), F32)
                refs.append(jnp.broadcast_to(ref, (blk, HG_DK)))
            return b - jnp.concatenate(refs, axis=0)

        def level_sums(b):
            out = [rel_to(b, HG_DIAG, 0)]
            for s_half in HG_LEVELS:
                d = rel_to(b, 2 * s_half, s_half)
                out.append(jnp.where((row_i % (2 * s_half)) >= s_half, d, -d))
            return out

        e = [level_sums(e_full[h]) for h in heads]
        wq = [[jnp.exp(e[h][l]) for l in range(n_lv + 1)] for h in heads]
        wk = [[jnp.exp(-e[h][0])] + wq[h][1:] for h in heads]
        prod = [[_dot_nt((q[h] * wq[h][l]).astype(BF16), (k[h] * wk[h][l]).astype(BF16)) for l in range(n_lv + 1)]
                for h in heads]
        st = [st_ref[h] for h in heads]
        inter = [_dot_nt((q[h] * jnp.exp(e_full[h])).astype(BF16), st[h].astype(BF16)) for h in heads]
        k_dec = [(k[h] * jnp.exp(b_last[h] - e_full[h])).astype(BF16) for h in heads]
        upd = [_dot_tn(vb[h], k_dec[h]) for h in heads]
        for h in heads:
            st_ref[h] = st[h] * jnp.exp(b_last[h]) + upd[h]
        a = []
        for h in heads:
            ah = jnp.where(lvl == 1, prod[h][0], 0.0)
            for l in range(1, n_lv + 1):
                ah = jnp.where(lvl == l + 1, prod[h][l], ah)
            a.append(ah.astype(BF16))
        o = [_dot(a[h], vb[h]) + inter[h] for h in heads]
        for h in heads:
            oh = o[h] * lax.rsqrt(jnp.mean(o[h] * o[h], axis=-1, keepdims=True) + EPS) * gain_ref[:, cols[h]]
            o_ref[0, rows, cols[h]] = (oh * jax.nn.silu(g_ref[0, rows, cols[h]])).astype(o_ref.dtype)
        return carry

    lax.fori_loop(0, n_chunks, chunk, 0, unroll=True)


def _hgrn_call(hg, lb, gain, mst, lvl):
    b, t, _ = hg.shape
    tt = HG_TT
    col = lambda k: pl.BlockSpec((1, tt, HG_WIDTH), lambda bi, ti: (bi, ti, k))
    full = lambda a: pl.BlockSpec(a.shape, lambda bi, ti: (0, 0))
    return pl.pallas_call(
        _hgrn_body,
        grid=(b, t // tt),
        in_specs=[col(0), col(1), col(2), col(3), full(lb), full(gain), full(mst), full(lvl)],
        out_specs=pl.BlockSpec((1, tt, HG_WIDTH), lambda bi, ti: (bi, ti, 0)),
        out_shape=jax.ShapeDtypeStruct((b, t, HG_WIDTH), BF16),
        scratch_shapes=[pltpu.VMEM((HG_HEADS, HG_DV, HG_DK), F32)],
        compiler_params=pltpu.CompilerParams(dimension_semantics=("arbitrary", "arbitrary"),
                                             vmem_limit_bytes=VMEM_LIMIT),
        name="hgrn2",
    )(hg, hg, hg, hg, lb, gain, mst, lvl)


def _cmp_body(kcn_ref, vcn_ref, pek_ref, pev_ref, w1k_ref, w1v_ref, w2k_ref, w2v_ref, kc_ref, vc_ref):
    nb = kcn_ref.shape[1] // CMP_STRIDE
    lane_grp = (lax.broadcasted_iota(jnp.int32, (nb, CMP_STRIDE * LANES), 1) // NSA_HEAD_DIM) % NSA_KV_HEADS

    def hidden(src_ref, pe_ref, w1_ref):
        x = jnp.concatenate([src_ref[0, pl.ds(l, nb, stride=CMP_STRIDE), :]
                             for l in range(CMP_STRIDE)], axis=1)
        halves = [x + pe_ref[i] for i in range(2)]
        out = []
        for g in range(NSA_KV_HEADS):
            u, v = (_dot(jnp.where(lane_grp == g, halves[i], 0.0).astype(BF16), w1_ref[i]) for i in range(2))
            out.append(jax.nn.silu(u + pltpu.roll(v, nb - 1, 0)).astype(BF16))
        return out

    hk = hidden(kcn_ref, pek_ref, w1k_ref)
    hv = hidden(vcn_ref, pev_ref, w1v_ref)
    for g in range(NSA_KV_HEADS):
        kc_ref[0, g, 0:nb, :] = _dot(hk[g], w2k_ref[0]).astype(kc_ref.dtype)
        kc_ref[0, g, nb:2 * nb, :] = _dot(hk[g], w2k_ref[1]).astype(kc_ref.dtype)
        vc_ref[0, g, :, 0:nb] = _dot_nt(w2v_ref[0], hv[g]).astype(vc_ref.dtype)
        vc_ref[0, g, :, nb:2 * nb] = _dot_nt(w2v_ref[1], hv[g]).astype(vc_ref.dtype)


def _cmp_call(kcn, vcn, pek, pev, w1k, w1v, w2k, w2v):
    b, t, w = kcn.shape
    nb = t // CMP_STRIDE
    full = lambda a: pl.BlockSpec(a.shape, lambda bi: (0,) * a.ndim)
    out = lambda r, c: pl.BlockSpec((1, NSA_KV_HEADS, r, c), lambda bi: (bi, 0, 0, 0))
    return pl.pallas_call(
        _cmp_body,
        grid=(b,),
        in_specs=[pl.BlockSpec((1, t, w), lambda bi: (bi, 0, 0)), pl.BlockSpec((1, t, w), lambda bi: (bi, 0, 0)),
                  full(pek), full(pev), full(w1k), full(w1v), full(w2k), full(w2v)],
        out_specs=[out(2 * nb, LANES), out(LANES, 2 * nb)],
        out_shape=[jax.ShapeDtypeStruct((b, NSA_KV_HEADS, 2 * nb, LANES), BF16),
                   jax.ShapeDtypeStruct((b, NSA_KV_HEADS, LANES, 2 * nb), BF16)],
        compiler_params=pltpu.CompilerParams(dimension_semantics=("arbitrary",),
                                             vmem_limit_bytes=VMEM_LIMIT),
        name="nsa_compress",
    )(kcn, vcn, pek, pev, w1k, w1v, w2k, w2v)


def _nsa_body(qt_ref, ksw_ref, vst_ref, vwt_ref, kc_ref, vct_ref, gt_ref, gain_ref, ovt_ref, o_ref,
              ks_ref, kw_ref, vs_ref, vw_ref, m_ref, acc_ref, s_ref, ch_ref):
    g = pl.program_id(1)
    qi = pl.program_id(2)
    tq = ATT_TQ
    tk = ATT_TK
    t_len = ksw_ref.shape[1]
    n_kt = t_len // tk
    n_pairs = HPG // 2
    hd = NSA_HEAD_DIM

    @pl.when(qi == 0)
    def _build_kv():
        lane = lax.broadcasted_iota(jnp.int32, (tk, LANES), 1)
        lo_lane = lane < hd
        keep = (lane // hd) == g

        def build_k(src_col, dst_ref):
            def body(j, carry):
                rows = pl.ds(pl.multiple_of(j * tk, tk), tk)
                x = ksw_ref[0, rows, src_col * LANES:(src_col + 1) * LANES].astype(F32)
                dup = jnp.where(keep, x, pltpu.roll(x, hd, 1))
                dst_ref[j, 0:tk, :] = jnp.where(lo_lane, dup, 0.0).astype(BF16)
                dst_ref[j, tk:2 * tk, :] = jnp.where(lo_lane, 0.0, dup).astype(BF16)
                return carry
            lax.fori_loop(0, n_kt, body, 0)

        def build_v(src_ref, dst_ref):
            zero = jnp.zeros((hd, tk), BF16)
            row = lax.broadcasted_iota(jnp.int32, (SUM_ROWS, 2 * tk), 0)
            col = lax.broadcasted_iota(jnp.int32, (SUM_ROWS, 2 * tk), 1)
            ones_rows = jnp.where(((row == 0) & (col < tk)) | ((row == 1) & (col >= tk)), 1.0, 0.0).astype(BF16)
            for j in range(n_kt):
                x = src_ref[0, :, j * tk:(j + 1) * tk]
                dst_ref[j, 0:hd, 0:tk] = x
                dst_ref[j, 0:hd, tk:2 * tk] = zero
                dst_ref[j, hd:2 * hd, 0:tk] = zero
                dst_ref[j, hd:2 * hd, tk:2 * tk] = x
                dst_ref[j, 2 * hd:2 * hd + SUM_ROWS, :] = ones_rows

        build_k(0, ks_ref)
        build_k(1, kw_ref)
        build_v(vst_ref, vs_ref)
        build_v(vwt_ref, vw_ref)

    t0 = qi * tq
    key_i = lax.broadcasted_iota(jnp.int32, (tk, tq), 0)
    qry_t = t0 + lax.broadcasted_iota(jnp.int32, (tk, tq), 1)
    slab_lo = lax.broadcasted_iota(jnp.int32, (LANES, tq), 0) < hd
    acc_row = lax.broadcasted_iota(jnp.int32, (LANES + SUM_ROWS, tq), 0)
    slab_a = (acc_row < hd) | (acc_row == LANES)
    q_pairs = [qt_ref[0, p * LANES:(p + 1) * LANES, :] for p in range(n_pairs)]

    last = (t0 + tq - 1) // tk

    def scores(k_ref, j):
        kt = k_ref[j]
        return [_dot(kt, q_pairs[p]) for p in range(n_pairs)]

    n_cmp_pad = kc_ref.shape[2] // 2
    blk_i = lax.broadcasted_iota(jnp.int32, (n_cmp_pad, tq), 0)
    blk_t = t0 + lax.broadcasted_iota(jnp.int32, (n_cmp_pad, tq), 1)
    cmp_ok = (blk_i * CMP_STRIDE + (CMP_BLOCK - 1)) <= blk_t
    kc = kc_ref[0, 0]
    vct = vct_ref[0, 0]
    s_cmp = [_dot(kc, q_pairs[p]) for p in range(n_pairs)]
    n_win = (WINDOW + tq) // tk
    win_tiles = [last - (n_win - 1) + k for k in range(n_win)]
    n_near = SLC_NEAR_TILES
    near_tiles = [last - k for k in range(n_near)]
    up_front = ([(ks_ref, 0)] + [(kw_ref, jnp.maximum(jw, 0)) for jw in win_tiles]
                + [(ks_ref, jnp.maximum(jn, 0)) for jn in near_tiles])
    for slot, (k_ref, j0) in enumerate(up_front):
        s_first = scores(k_ref, j0)
        for p in range(n_pairs):
            s_ref[slot, p] = s_first[p]
    p_sum = jnp.zeros((n_cmp_pad, tq), F32)
    p_cmp = []
    for p in range(n_pairs):
        probs = []
        for h in range(2):
            sh = jnp.where(cmp_ok, s_cmp[p][h * n_cmp_pad:(h + 1) * n_cmp_pad], NEG_INF)
            mh = jnp.max(sh, axis=0, keepdims=True)
            eh = jnp.where(cmp_ok, jnp.exp2(sh - mh), 0.0)
            den = jnp.sum(eh, axis=0, keepdims=True)
            ph = eh / jnp.where(den > 0.0, den, 1.0)
            p_sum = p_sum + ph
            probs.append(ph.astype(BF16))
        p_cmp.append(jnp.concatenate(probs, axis=0))
    o_cmp = [_dot(vct, p_cmp[p]) for p in range(n_pairs)]

    n_sel = t_len // SLC_BLOCK
    hi, mid, lo = _split3(p_sum)
    ovt = ovt_ref[...]
    p_sel = ((_dot(ovt, hi) + _dot(ovt, mid)) + _dot(ovt, lo))[0:n_sel]
    sel_i = lax.broadcasted_iota(jnp.int32, (n_sel, tq), 0)
    cur = (t0 + lax.broadcasted_iota(jnp.int32, (n_sel, tq), 1)) // SLC_BLOCK
    forced = (sel_i == 0) | (sel_i == cur) | (sel_i == cur - 1)
    score = jnp.where(forced, FORCE_SCORE, p_sel)
    score = jnp.where(sel_i <= cur, score, -jnp.inf)
    rank = jnp.zeros((n_sel, tq), jnp.int32)
    row_grp = SUBLANES
    grp_i = lax.broadcasted_iota(jnp.int32, (row_grp, tq), 0)
    for i in range(n_sel):
        ci = score[i:i + 1, :]
        ahead = []
        for r0 in range(0, n_sel, row_grp):
            rows = slice(r0, r0 + row_grp)
            if r0 > i:
                ahead.append(ci >= score[rows])
            elif r0 + row_grp <= i:
                ahead.append(ci > score[rows])
            else:
                ahead.append((ci > score[rows]) | ((ci == score[rows]) & (grp_i > i - r0)))
        rank = rank + jnp.where(jnp.concatenate(ahead, axis=0), 1, 0)
    chosen = jnp.where(rank < min(SLC_TOPK, n_sel), 1.0, 0.0)
    for i in range(n_sel):
        ch_ref[i] = chosen[i:i + 1, :]

    def tile_softmax(slot, bias, m_get, m_put):
        alphas = {}
        probs = {}
        for hh in range(HPG):
            p, h = divmod(hh, 2)
            a_parts = []
            p_parts = []
            for qh in range(tq // LANES):
                ql = slice(qh * LANES, (qh + 1) * LANES)
                sh = s_ref[slot, p, h * tk:(h + 1) * tk, ql] + (bias if bias.shape == (1, 1) else bias[:, ql])
                m_prev = m_get(hh, qh)
                m_new = jnp.maximum(m_prev, jnp.max(sh, axis=0, keepdims=True))
                m_put(hh, qh, m_new)
                p_parts.append(jnp.exp2(sh - m_new).astype(BF16))
                a_parts.append(jnp.exp2(m_prev - m_new))
            alphas[hh] = jnp.concatenate(a_parts, axis=1)
            probs[hh] = jnp.concatenate(p_parts, axis=1)
        return ([jnp.concatenate([probs[2 * p], probs[2 * p + 1]], axis=0) for p in range(n_pairs)],
                [jnp.where(slab_a, alphas[2 * p], alphas[2 * p + 1]) for p in range(n_pairs)])

    def normalised(acc):
        inv = jnp.where(slab_lo, 1.0 / acc[LANES:LANES + 1, :], 1.0 / acc[LANES + 1:LANES + 2, :])
        return acc[0:LANES] * inv

    m_win = {}
    acc_win = [jnp.zeros((LANES + SUM_ROWS, tq), F32) for _ in range(n_pairs)]
    for k, jw in enumerate(win_tiles):
        rel_hi = (n_win - k) * tk - 1
        rel_lo = rel_hi - (tq - 1) - (tk - 1)
        exists = jnp.where(jw >= 0, 0.0, NEG_INF).astype(F32).reshape(1, 1)
        if rel_lo >= 0 and rel_hi < WINDOW:
            bias = exists
        else:
            rel = qry_t - (jw * tk + key_i)
            inside = (rel < WINDOW) if rel_lo >= 0 else (rel >= 0) if rel_hi < WINDOW else (rel >= 0) & (rel < WINDOW)
            bias = jnp.where(inside, exists, NEG_INF)
        probs, a_rows = tile_softmax(1 + k, bias,
                                     lambda hh, qh: m_win.get((hh, qh), jnp.full((1, LANES), -jnp.inf, F32)),
                                     lambda hh, qh, v: m_win.__setitem__((hh, qh), v))
        vt = vw_ref[jnp.maximum(jw, 0)]
        acc_win = [acc_win[p] * a_rows[p] + _dot(vt, probs[p]) for p in range(n_pairs)]
    o_win = [normalised(acc_win[p]) for p in range(n_pairs)]

    def picked_bias(j, also=None):
        per_tile = tk // SLC_BLOCK
        picked = jnp.concatenate([jnp.broadcast_to(ch_ref[j * per_tile + i], (SLC_BLOCK, tq)) for i in range(per_tile)],
                                 axis=0) > 0.5
        return jnp.where(picked if also is None else picked & also, 0.0, NEG_INF)

    m_near = {}
    acc_near = [jnp.zeros((LANES + SUM_ROWS, tq), F32) for _ in range(n_pairs)]
    for k, jn in enumerate(near_tiles):
        jc = jnp.maximum(jn, 0)
        causal = (jn * tk + key_i) <= qry_t
        if k > 0:
            causal = causal & (jn >= 0)
        probs, a_rows = tile_softmax(1 + n_win + k, picked_bias(jc, causal),
                                     lambda hh, qh: m_near.get((hh, qh), jnp.full((1, LANES), -jnp.inf, F32)),
                                     lambda hh, qh, v: m_near.__setitem__((hh, qh), v))
        vt = vs_ref[jc]
        acc_near = [acc_near[p] * a_rows[p] + _dot(vt, probs[p]) for p in range(n_pairs)]

    def m_put(hh, qh, v):
        m_ref[hh, :, qh * LANES:(qh + 1) * LANES] = v

    for (hh, qh), v in m_near.items():
        m_put(hh, qh, v)
    for p in range(n_pairs):
        acc_ref[p] = acc_near[p]
    n_far = jnp.maximum(last + 1 - n_near, 0)

    def slc_step(j, carry):
        s_next = scores(ks_ref, jnp.minimum(j + 1, n_far - 1))
        probs, a_rows = tile_softmax(0, picked_bias(j), lambda hh, qh: m_ref[hh, :, qh * LANES:(qh + 1) * LANES], m_put)
        vt = vs_ref[j]
        for p in range(n_pairs):
            pv = _dot(vt, probs[p])
            s_ref[0, p] = s_next[p]
            acc_ref[p] = acc_ref[p] * a_rows[p] + pv
        return carry

    lax.fori_loop(0, n_far, slc_step, 0)
    o_slc = [normalised(acc_ref[p]) for p in range(n_pairs)]

    gates = gt_ref[0]
    gain = gain_ref[...]
    for p in range(n_pairs):
        o = jnp.zeros((LANES, tq), F32)
        for c, branch in enumerate((o_cmp[p], o_slc[p], o_win[p])):
            r = c * HPG + 2 * p
            o = o + jnp.where(slab_lo, gates[r:r + 1, :], gates[r + 1:r + 2, :]) * branch
        sq = o * o
        ms_a = jnp.sum(sq[0:hd], axis=0, keepdims=True)
        ms_b = jnp.sum(sq[hd:2 * hd], axis=0, keepdims=True)
        ms = jnp.where(slab_lo, ms_a, ms_b) * (1.0 / hd)
        o = o * lax.rsqrt(ms + EPS)
        o_ref[0, p * LANES:(p + 1) * LANES, :] = (o * gain[p * LANES:(p + 1) * LANES, :]).astype(o_ref.dtype)


def _nsa_call(qt, ksw, vt, kc, vct, gt, gain, ovt):
    b, _, t = qt.shape
    tq, tk = ATT_TQ, ATT_TK
    n_kt = t // tk
    gw = HPG * NSA_HEAD_DIM
    hd = NSA_HEAD_DIM
    k_scratch = pltpu.VMEM((n_kt, 2 * tk, LANES), BF16)
    v_scratch = pltpu.VMEM((n_kt, LANES + SUM_ROWS, 2 * tk), BF16)
    return pl.pallas_call(
        _nsa_body,
        grid=(b, NSA_KV_HEADS, t // tq),
        in_specs=[
            pl.BlockSpec((1, gw, tq), lambda bi, gi, qi: (bi, gi, qi)),
            pl.BlockSpec((1, t, 2 * KV_WIDTH), lambda bi, gi, qi: (bi, 0, 0)),
            pl.BlockSpec((1, hd, t), lambda bi, gi, qi: (bi, gi, 0)),
            pl.BlockSpec((1, hd, t), lambda bi, gi, qi: (bi, NSA_KV_HEADS + gi, 0)),
            pl.BlockSpec((1, 1) + kc.shape[2:], lambda bi, gi, qi: (bi, gi, 0, 0)),
            pl.BlockSpec((1, 1) + vct.shape[2:], lambda bi, gi, qi: (bi, gi, 0, 0)),
            pl.BlockSpec((1, LANES, tq), lambda bi, gi, qi: (bi, gi, qi)),
            pl.BlockSpec((gw, 1), lambda bi, gi, qi: (gi, 0)),
            pl.BlockSpec(ovt.shape, lambda bi, gi, qi: (0, 0)),
        ],
        out_specs=pl.BlockSpec((1, gw, tq), lambda bi, gi, qi: (bi, gi, qi)),
        out_shape=jax.ShapeDtypeStruct((b, NSA_WIDTH, t), BF16),
        scratch_shapes=[k_scratch, k_scratch, v_scratch, v_scratch,
                        pltpu.VMEM((HPG, 1, tq), F32),
                        pltpu.VMEM((HPG // 2, LANES + SUM_ROWS, tq), F32), pltpu.VMEM((1 + (WINDOW + tq) // tk + SLC_NEAR_TILES, HPG // 2, 2 * tk, tq), F32),
                        pltpu.VMEM((t // SLC_BLOCK, 1, tq), F32)],
        compiler_params=pltpu.CompilerParams(dimension_semantics=("arbitrary", "arbitrary", "arbitrary"),
                                             vmem_limit_bytes=VMEM_LIMIT),
        name="nsa_attention",
    )(qt, ksw, vt, vt, kc, vct, gt, gain, ovt)


def _ffn_body(x_ref, oh_ref, on_ref, woh_ref, won_ref, g2_ref, wg_ref, wu_ref, wd_ref, cw_ref, gf_ref,
              out_ref, halo_ref, act_ref, *, tiles_per_seq):
    tm = x_ref.shape[0]
    x1 = x_ref[...] + _dot(oh_ref[...], woh_ref[...]) + _dot_tn(on_ref[0], won_ref[...])
    hb = _rms(x1, g2_ref[...]).astype(BF16)
    row = lax.broadcasted_iota(jnp.int32, (tm, FFN_TC), 0)

    @pl.when((pl.program_id(0) % tiles_per_seq) == 0)
    def _sequence_start():
        halo_ref[...] = jnp.zeros_like(halo_ref)

    def activation(c, gate, up):
        cols = slice(c * FFN_TC, (c + 1) * FFN_TC)
        halo = halo_ref[:, cols]
        halo_ref[:, cols] = gate[tm - SUBLANES:tm, :]
        last1 = halo[SUBLANES - 1:SUBLANES, :]
        last2 = halo[SUBLANES - 2:SUBLANES - 1, :]
        prev1 = jnp.where(row == 0, last1, pltpu.roll(gate, 1, 0))
        prev2 = jnp.where(row == 0, last2, jnp.where(row == 1, last1, pltpu.roll(gate, 2, 0)))
        cw = cw_ref[:, cols]
        y = cw[0:1, :] * prev2 + cw[1:2, :] * prev1 + cw[2:3, :] * gate + cw[3:4, :]
        return (jax.nn.silu(y) * up).astype(BF16)

    chunk = lambda w_ref, c: _dot(hb, w_ref[:, c * FFN_TC:(c + 1) * FFN_TC])
    gate_up = (chunk(wg_ref, 0), chunk(wu_ref, 0))
    for c in range(FFN_NC):
        cur = gate_up
        if c + 1 < FFN_NC:
            gate_up = (chunk(wg_ref, c + 1), chunk(wu_ref, c + 1))
        act_ref[:, c * FFN_TC:(c + 1) * FFN_TC] = activation(c, *cur)
    acc = _dot(act_ref[...], wd_ref[...])
    out_ref[...] = _rms(x1 + acc, gf_ref[...])


def _ffn_call(x2, oh, on, woh, won, g2, wg, wu, wd, cw, gf, tiles_per_seq):
    n = x2.shape[0]
    tm = FFN_TM
    row = lambda w: pl.BlockSpec((tm, w), lambda i: (i, 0))
    full = lambda a: pl.BlockSpec(a.shape, lambda i: (0,) * a.ndim, pipeline_mode=pl.Buffered(1))
    return pl.pallas_call(
        functools.partial(_ffn_body, tiles_per_seq=tiles_per_seq),
        grid=(n // tm,),
        in_specs=[row(D_MODEL), row(HG_WIDTH),
                  pl.BlockSpec((1, NSA_WIDTH, tm), lambda i: (i // tiles_per_seq, 0, i % tiles_per_seq)),
                  full(woh), full(won), full(g2),
                  full(wg), full(wu), full(wd), full(cw), full(gf)],
        out_specs=row(D_MODEL),
        out_shape=jax.ShapeDtypeStruct((n, D_MODEL), F32),
        scratch_shapes=[pltpu.VMEM((SUBLANES, D_FF), F32), pltpu.VMEM((tm, D_FF), BF16)],
        compiler_params=pltpu.CompilerParams(dimension_semantics=("arbitrary",),
                                             vmem_limit_bytes=VMEM_LIMIT),
        name="outproj_convffn",
    )(x2, oh, on, woh, won, g2, wg, wu, wd, cw, gf)


def _rope_angles(positions):
    inv_freq = ROPE_THETA ** (-jnp.arange(ROPE_HALF, dtype=F32) * 2.0 / ROPE_DIM)
    ang = positions.astype(F32)[..., None] * inv_freq
    return jnp.concatenate([jnp.cos(ang), jnp.sin(ang)], axis=-1).transpose(0, 2, 1)


def _layer(x, positions, ln1, w_in, lb, hg_gain, pe_k, pe_v, k_w1, k_w2, v_w1, v_w2, nsa_gain, w_o, ln2,
           w_gate, w_up, conv_w, conv_b, w_down, final_gain):
    b, t, d = x.shape
    n = b * t
    assert d == D_MODEL and t % FFN_TM == 0 and t % PROJ_TM == 0 and t % ATT_TQ == 0 and t % HG_TT == 0
    n_grp = t // CMP_STRIDE
    assert n_grp == LANES, "compressed-block axis is laid out on exactly one lane tile"
    n_sel = t // SLC_BLOCK
    assert n_sel % 8 == 0 and n_sel <= LANES and ATT_TK % SLC_BLOCK == 0
    x2 = x.reshape(n, d)

    splits = np.cumsum([0, 4 * HG_WIDTH, NSA_WIDTH] + [KV_WIDTH] * 6 + [N_GATES])
    seg = lambda i: w_in[:, splits[i]:splits[i + 1]]
    wh = seg(0).astype(BF16)
    wk = jnp.concatenate([seg(2), seg(3), seg(4), seg(6)], axis=1).astype(BF16)
    wgate = seg(8).reshape(d, 3, NSA_KV_HEADS, HPG).transpose(0, 2, 1, 3).reshape(d, NSA_KV_HEADS, 3 * HPG)
    wgate = jnp.pad(wgate, ((0, 0), (0, 0), (0, LANES - 3 * HPG))).reshape(d, NSA_KV_HEADS * LANES)
    wt = jnp.concatenate([seg(1), seg(5), seg(7), wgate], axis=1).T.astype(BF16)
    cs = _rope_angles(positions)

    hg, kcn, vcn, ksw, qt, vt, gt = _inproj_call(x2, ln1.reshape(1, d), wh, wk, wt, cs, t // PROJ_TM)

    mst, lvl = _hgrn_tables()
    o_hg = _hgrn_call(hg.reshape(b, t, 4 * HG_WIDTH), lb.reshape(1, HG_WIDTH).astype(F32),
                      hg_gain.reshape(1, HG_WIDTH), mst, lvl)

    per_lane = lambda a: jnp.broadcast_to(a.reshape(2, CMP_STRIDE, 1, NSA_HEAD_DIM, -1),
                                          (2, CMP_STRIDE, NSA_KV_HEADS, NSA_HEAD_DIM, a.shape[-1]))
    w1_rows = lambda w1: per_lane(w1).reshape(2, CMP_STRIDE * LANES, CMP_HIDDEN).astype(BF16)
    pe_rows = lambda pe: per_lane(pe[..., None]).reshape(2, 1, CMP_STRIDE * LANES)
    zeros_w2 = jnp.zeros((CMP_HIDDEN, NSA_HEAD_DIM), F32)
    place = lambda w2: jnp.stack([jnp.concatenate([w2, zeros_w2], 1), jnp.concatenate([zeros_w2, w2], 1)])
    kc, vct = _cmp_call(kcn.reshape(b, t, KV_WIDTH), vcn.reshape(b, t, KV_WIDTH), pe_rows(pe_k), pe_rows(pe_v),
                        w1_rows(k_w1), w1_rows(v_w1),
                        place(k_w2).astype(BF16), place(v_w2).transpose(0, 2, 1).astype(BF16))

    cmp_start = np.arange(n_grp) * CMP_STRIDE
    cmp_end = cmp_start + CMP_BLOCK - 1
    sel_start = np.arange(LANES) * SLC_BLOCK
    overlap = ((cmp_start[:, None] <= sel_start[None, :] + SLC_BLOCK - 1) & (cmp_end[:, None] >= sel_start[None, :])
               & (np.arange(LANES)[None, :] < n_sel) & (np.arange(n_grp)[:, None] < n_grp - 1))
    ovt = jnp.asarray(overlap.T.astype(np.float32), BF16)
    o_nsa = _nsa_call(qt, ksw.reshape(b, t, 2 * KV_WIDTH), vt, kc, vct, gt, nsa_gain.reshape(NSA_WIDTH, 1), ovt)

    cw = jnp.concatenate([conv_w, conv_b[None, :], jnp.zeros((SUBLANES - CONV_WIDTH - 1, D_FF), F32)], axis=0)
    out = _ffn_call(x2, o_hg.reshape(n, HG_WIDTH), o_nsa,
                    w_o[:HG_WIDTH].astype(BF16), w_o[HG_WIDTH:].astype(BF16), ln2.reshape(1, d),
                    w_gate.astype(BF16), w_up.astype(BF16), w_down.astype(BF16), cw,
                    final_gain.reshape(1, d), t // FFN_TM)
    return out.reshape(b, t, d)


def kernel(x, positions, ln1_gain, w_in, hgrn_lb_param, hgrn_out_gain, cmp_pe_k, cmp_pe_v, cmp_k_w1, cmp_k_w2,
           cmp_v_w1, cmp_v_w2, nsa_out_gain, w_o, ln2_gain, ffn_w_gate, ffn_w_up, ffn_conv_w, ffn_conv_b,
           ffn_w_down, final_gain):
    depth = ln1_gain.shape[0]
    assert depth == 1, "the fused final norm assumes a single layer"
    lower_bounds = jnp.cumsum(jax.nn.softmax(hgrn_lb_param.astype(F32), axis=0), axis=0)
    l = 0
    return _layer(x, positions, ln1_gain[l], w_in[l], lower_bounds[l], hgrn_out_gain[l], cmp_pe_k[l], cmp_pe_v[l],
                  cmp_k_w1[l], cmp_k_w2[l], cmp_v_w1[l], cmp_v_w2[l], nsa_out_gain[l], w_o[l], ln2_gain[l],
                  ffn_w_gate[l], ffn_w_up[l], ffn_conv_w[l], ffn_conv_b[l], ffn_w_down[l], final_gain)
```

```python
import functools

import jax
import jax.numpy as jnp
import numpy as np
from jax import lax
from jax.experimental import pallas as pl
from jax.experimental.pallas import tpu as pltpu

F32 = jnp.float32
BF16 = jnp.bfloat16

D_MODEL = 1024
HG_HEADS = 4
HG_DK = 128
HG_DV = 128
HG_WIDTH = HG_HEADS * HG_DV
NSA_HEADS = 8
NSA_KV_HEADS = 2
NSA_HEAD_DIM = 64
HPG = NSA_HEADS // NSA_KV_HEADS
NSA_WIDTH = NSA_HEADS * NSA_HEAD_DIM
KV_WIDTH = NSA_KV_HEADS * NSA_HEAD_DIM
CMP_BLOCK = 32
CMP_STRIDE = 16
CMP_HIDDEN = 256
SLC_BLOCK = 64
SLC_TOPK = 16
WINDOW = 512
ROPE_THETA = 500000.0
ROPE_DIM = NSA_HEAD_DIM // 4
ROPE_HALF = ROPE_DIM // 2
D_FF = 2816
CONV_WIDTH = 3
EPS = 1e-6
NEG_INF = -1e30
FORCE_SCORE = 1e4
N_GATES = 3 * NSA_HEADS
LOG2_E = 1.4426950408889634

LANES = 128
SUBLANES = 8
VMEM_LIMIT = 56 * 1024 * 1024

PROJ_TM = 512
HG_CHUNK = 128
HG_LEVELS = (16, 32, 64)
HG_DIAG = 16
HG_TT = 1024
ATT_TQ = 256
ATT_TK = 256
SLC_NEAR_TILES = 1
SUM_ROWS = 16
FFN_TM = 512
FFN_TC = 256
FFN_NC = D_FF // FFN_TC


def _dot(a, b):
    return jnp.dot(a, b, preferred_element_type=F32)


def _dot_nt(a, b):
    return lax.dot_general(a, b, (((1,), (1,)), ((), ())), preferred_element_type=F32)


def _dot_tn(a, b):
    return lax.dot_general(a, b, (((0,), (0,)), ((), ())), preferred_element_type=F32)


def _split3(x):
    hi = x.astype(BF16)
    r = x - hi.astype(F32)
    mid = r.astype(BF16)
    lo = (r - mid.astype(F32)).astype(BF16)
    return hi, mid, lo


def _rms(x, gain):
    return x * lax.rsqrt(jnp.mean(x * x, axis=-1, keepdims=True) + EPS) * gain


def _inproj_body(x_ref, g_ref, wh_ref, wk_ref, wt_ref, cs_ref,
                 hg_ref, kcn_ref, vcn_ref, ksw_ref, qt_ref, vt_ref, gt_ref):
    hb = _rms(x_ref[...], g_ref[...]).astype(BF16)
    hg_ref[...] = _dot(hb, wh_ref[...])

    def rope(v, axis, cos, sin_hi, sin_lo):
        return (v * cos + pltpu.roll(v, ROPE_HALF, axis) * sin_hi
                + pltpu.roll(v, LANES - ROPE_HALF, axis) * sin_lo)

    cos = cs_ref[0, 0:ROPE_HALF, :]
    sin = cs_ref[0, ROPE_HALF:ROPE_DIM, :]
    tm = cos.shape[1]
    zero_h = jnp.zeros((ROPE_HALF, tm), F32)
    rest = NSA_HEAD_DIM - ROPE_DIM
    slab = lambda lo, hi, fill: jnp.concatenate([lo, hi, jnp.full((rest, tm), fill, F32)] * (LANES // NSA_HEAD_DIM), axis=0)
    tab_t = (slab(cos, cos, 1.0), slab(zero_h, sin, 0.0), slab(-sin, zero_h, 0.0))
    tab = tuple(a.T for a in tab_t)
    kn = _dot(hb, wk_ref[...])
    kcn_ref[...] = rope(kn[:, 0:LANES], 1, *tab)
    vcn_ref[...] = kn[:, LANES:2 * LANES]
    ksw_ref[:, 0:LANES] = rope(kn[:, 2 * LANES:3 * LANES], 1, *tab).astype(BF16)
    ksw_ref[:, LANES:2 * LANES] = rope(kn[:, 3 * LANES:4 * LANES], 1, *tab).astype(BF16)

    rt = _dot_nt(wt_ref[...], hb)
    scale = NSA_HEAD_DIM ** -0.5 * LOG2_E
    for j in range(NSA_WIDTH // LANES):
        sl = slice(j * LANES, (j + 1) * LANES)
        qt_ref[0, sl, :] = (rope(rt[sl], 0, *tab_t) * scale).astype(BF16)
    vt_ref[0] = rt[NSA_WIDTH:NSA_WIDTH + 2 * KV_WIDTH].astype(BF16)
    gt_ref[0] = jax.nn.sigmoid(rt[NSA_WIDTH + 2 * KV_WIDTH:])


def _inproj_call(x2, gain, wh, wk, wt, cs, tiles_per_seq):
    n = x2.shape[0]
    tm = PROJ_TM
    t = tiles_per_seq * tm
    b = n // t
    row = lambda w: pl.BlockSpec((tm, w), lambda i: (i, 0))
    col = lambda h: pl.BlockSpec((1, h, tm), lambda i: (i // tiles_per_seq, 0, i % tiles_per_seq))
    full = lambda a: pl.BlockSpec(a.shape, lambda i: (0, 0))
    gate_rows = NSA_KV_HEADS * LANES
    return pl.pallas_call(
        _inproj_body,
        grid=(n // tm,),
        in_specs=[row(D_MODEL), full(gain), full(wh), full(wk), full(wt),
                  col(ROPE_DIM)],
        out_specs=[row(4 * HG_WIDTH), row(KV_WIDTH), row(KV_WIDTH), row(2 * KV_WIDTH),
                   col(NSA_WIDTH), col(2 * KV_WIDTH), col(gate_rows)],
        out_shape=[jax.ShapeDtypeStruct((n, 4 * HG_WIDTH), F32),
                   jax.ShapeDtypeStruct((n, KV_WIDTH), F32),
                   jax.ShapeDtypeStruct((n, KV_WIDTH), F32),
                   jax.ShapeDtypeStruct((n, 2 * KV_WIDTH), BF16),
                   jax.ShapeDtypeStruct((b, NSA_WIDTH, t), BF16),
                   jax.ShapeDtypeStruct((b, 2 * KV_WIDTH, t), BF16),
                   jax.ShapeDtypeStruct((b, gate_rows, t), F32)],
        compiler_params=pltpu.CompilerParams(dimension_semantics=("arbitrary",),
                                             vmem_limit_bytes=VMEM_LIMIT),
        name="inproj",
    )(x2, gain, wh, wk, wt, cs)


def _hgrn_tables():
    L = HG_CHUNK
    t = np.arange(L)[:, None]
    u = np.arange(L)[None, :]
    level = np.where(((t // HG_DIAG) == (u // HG_DIAG)) & (u <= t), 1, 0)
    for li, s in enumerate(HG_LEVELS):
        same = (t // (2 * s)) == (u // (2 * s))
        right = (t % (2 * s)) >= s
        level = np.where(same & right & ((u % (2 * s)) < s), li + 2, level)
    return jnp.asarray((u <= t).astype(np.float32), BF16), jnp.asarray(level, jnp.int32)


def _hgrn_body(q_ref, f_ref, i_ref, g_ref, lb_ref, gain_ref, mst_ref, lvl_ref, o_ref, st_ref):
    L = HG_CHUNK
    n_chunks = q_ref.shape[1] // L

    @pl.when(pl.program_id(1) == 0)
    def _sequence_start():
        st_ref[...] = jnp.zeros_like(st_ref)

    def chunk(c, carry):
        rows = pl.ds(pl.multiple_of(c * L, L), L)
        heads = range(HG_HEADS)
        cols = [slice(h * HG_DK, (h + 1) * HG_DK) for h in heads]
        mst = mst_ref[...]
        lvl = lvl_ref[...]
        n_lv = len(HG_LEVELS)
        row_i = lax.broadcasted_iota(jnp.int32, (L, HG_DK), 0)
        q = [q_ref[0, rows, cols[h]] for h in heads]
        vb = [i_ref[0, rows, cols[h]].astype(BF16) for h in heads]
        f = [lb_ref[:, cols[h]] + (1.0 - lb_ref[:, cols[h]]) * jax.nn.sigmoid(f_ref[0, rows, cols[h]]) for h in heads]
        k = [1.0 - f[h] for h in heads]
        parts = [_split3(jnp.log2(f[h])) for h in heads]
        e_full = [(_dot(mst, parts[h][0]) + _dot(mst, parts[h][1])) + _dot(mst, parts[h][2]) for h in heads]
        b_last = [e_full[h][L - 1:L, :] for h in heads]

        def rel_to(b, blk, off):
            refs = []
            for r0 in range(0, L, blk):
                r = r0 + off - 1
                ref = b[r:r + 1, :] if r >= 0 else jnp.zeros((1, HG_DK), F32)
                refs.append(jnp.broadcast_to(ref, (blk, HG_DK)))
            return b - jnp.concatenate(refs, axis=0)

        def level_sums(b):
            out = [rel_to(b, HG_DIAG, 0)]
            for s_half in HG_LEVELS:
                d = rel_to(b, 2 * s_half, s_half)
                out.append(jnp.where((row_i % (2 * s_half)) >= s_half, d, -d))
            return out

        e = [level_sums(e_full[h]) for h in heads]
        wq = [[jnp.exp2(e[h][l]) for l in range(n_lv + 1)] for h in heads]
        wk = [[jnp.exp2(-e[h][0])] + wq[h][1:] for h in heads]
        prod = [[_dot_nt((q[h] * wq[h][l]).astype(BF16), (k[h] * wk[h][l]).astype(BF16)) for l in range(n_lv + 1)]
                for h in heads]
        st = [st_ref[h] for h in heads]
        inter = [_dot_nt((q[h] * jnp.exp2(e_full[h])).astype(BF16), st[h].astype(BF16)) for h in heads]
        k_dec = [(k[h] * jnp.exp2(b_last[h] - e_full[h])).astype(BF16) for h in heads]
        upd = [_dot_tn(vb[h], k_dec[h]) for h in heads]
        for h in heads:
            st_ref[h] = st[h] * jnp.exp2(b_last[h]) + upd[h]
        a = []
        for h in heads:
            ah = jnp.where(lvl == 1, prod[h][0], 0.0)
            for l in range(1, n_lv + 1):
                ah = jnp.where(lvl == l + 1, prod[h][l], ah)
            a.append(ah.astype(BF16))
        o = [_dot(a[h], vb[h]) + inter[h] for h in heads]
        for h in heads:
            oh = o[h] * lax.rsqrt(jnp.mean(o[h] * o[h], axis=-1, keepdims=True) + EPS) * gain_ref[:, cols[h]]
            o_ref[0, rows, cols[h]] = (oh * jax.nn.silu(g_ref[0, rows, cols[h]])).astype(o_ref.dtype)
        return carry

    lax.fori_loop(0, n_chunks, chunk, 0, unroll=True)


def _hgrn_call(hg, lb, gain, mst, lvl):
    b, t, _ = hg.shape
    tt = HG_TT
    col = lambda k: pl.BlockSpec((1, tt, HG_WIDTH), lambda bi, ti: (bi, ti, k))
    full = lambda a: pl.BlockSpec(a.shape, lambda bi, ti: (0, 0))
    return pl.pallas_call(
        _hgrn_body,
        grid=(b, t // tt),
        in_specs=[col(0), col(1), col(2), col(3), full(lb), full(gain), full(mst), full(lvl)],
        out_specs=pl.BlockSpec((1, tt, HG_WIDTH), lambda bi, ti: (bi, ti, 0)),
        out_shape=jax.ShapeDtypeStruct((b, t, HG_WIDTH), BF16),
        scratch_shapes=[pltpu.VMEM((HG_HEADS, HG_DV, HG_DK), F32)],
        compiler_params=pltpu.CompilerParams(dimension_semantics=("arbitrary", "arbitrary"),
                                             vmem_limit_bytes=VMEM_LIMIT),
        name="hgrn2",
    )(hg, hg, hg, hg, lb, gain, mst, lvl)


def _cmp_body(kcn_ref, vcn_ref, pek_ref, pev_ref, w1k_ref, w1v_ref, w2k_ref, w2v_ref, kc_ref, vc_ref):
    nb = kcn_ref.shape[1] // CMP_STRIDE
    lane_grp = (lax.broadcasted_iota(jnp.int32, (nb, CMP_STRIDE * LANES), 1) // NSA_HEAD_DIM) % NSA_KV_HEADS

    def hidden(src_ref, pe_ref, w1_ref):
        x = jnp.concatenate([src_ref[0, pl.ds(l, nb, stride=CMP_STRIDE), :]
                             for l in range(CMP_STRIDE)], axis=1)
        halves = [x + pe_ref[i] for i in range(2)]
        out = []
        for g in range(NSA_KV_HEADS):
            u, v = (_dot(jnp.where(lane_grp == g, halves[i], 0.0).astype(BF16), w1_ref[i]) for i in range(2))
            out.append(jax.nn.silu(u + pltpu.roll(v, nb - 1, 0)).astype(BF16))
        return out

    hk = hidden(kcn_ref, pek_ref, w1k_ref)
    hv = hidden(vcn_ref, pev_ref, w1v_ref)
    for g in range(NSA_KV_HEADS):
        kc_ref[0, g, 0:nb, :] = _dot(hk[g], w2k_ref[0]).astype(kc_ref.dtype)
        kc_ref[0, g, nb:2 * nb, :] = _dot(hk[g], w2k_ref[1]).astype(kc_ref.dtype)
        vc_ref[0, g, :, 0:nb] = _dot_nt(w2v_ref[0], hv[g]).astype(vc_ref.dtype)
        vc_ref[0, g, :, nb:2 * nb] = _dot_nt(w2v_ref[1], hv[g]).astype(vc_ref.dtype)


def _cmp_call(kcn, vcn, pek, pev, w1k, w1v, w2k, w2v):
    b, t, w = kcn.shape
    nb = t // CMP_STRIDE
    full = lambda a: pl.BlockSpec(a.shape, lambda bi: (0,) * a.ndim)
    out = lambda r, c: pl.BlockSpec((1, NSA_KV_HEADS, r, c), lambda bi: (bi, 0, 0, 0))
    return pl.pallas_call(
        _cmp_body,
        grid=(b,),
        in_specs=[pl.BlockSpec((1, t, w), lambda bi: (bi, 0, 0)), pl.BlockSpec((1, t, w), lambda bi: (bi, 0, 0)),
                  full(pek), full(pev), full(w1k), full(w1v), full(w2k), full(w2v)],
        out_specs=[out(2 * nb, LANES), out(LANES, 2 * nb)],
        out_shape=[jax.ShapeDtypeStruct((b, NSA_KV_HEADS, 2 * nb, LANES), BF16),
                   jax.ShapeDtypeStruct((b, NSA_KV_HEADS, LANES, 2 * nb), BF16)],
        compiler_params=pltpu.CompilerParams(dimension_semantics=("arbitrary",),
                                             vmem_limit_bytes=VMEM_LIMIT),
        name="nsa_compress",
    )(kcn, vcn, pek, pev, w1k, w1v, w2k, w2v)


def _nsa_body(qt_ref, ksw_ref, vst_ref, vwt_ref, kc_ref, vct_ref, gt_ref, gain_ref, ovt_ref, o_ref,
              ks_ref, kw_ref, vs_ref, vw_ref, m_ref, acc_ref, s_ref, ch_ref):
    g = pl.program_id(1)
    qi = pl.program_id(2)
    tq = ATT_TQ
    tk = ATT_TK
    t_len = ksw_ref.shape[1]
    n_kt = t_len // tk
    n_pairs = HPG // 2
    hd = NSA_HEAD_DIM

    @pl.when(qi == 0)
    def _build_kv():
        lane = lax.broadcasted_iota(jnp.int32, (tk, LANES), 1)
        lo_lane = lane < hd
        keep = (lane // hd) == g

        def build_k(src_col, dst_ref):
            def body(j, carry):
                rows = pl.ds(pl.multiple_of(j * tk, tk), tk)
                x = ksw_ref[0, rows, src_col * LANES:(src_col + 1) * LANES].astype(F32)
                dup = jnp.where(keep, x, pltpu.roll(x, hd, 1))
                dst_ref[j, 0:tk, :] = jnp.where(lo_lane, dup, 0.0).astype(BF16)
                dst_ref[j, tk:2 * tk, :] = jnp.where(lo_lane, 0.0, dup).astype(BF16)
                return carry
            lax.fori_loop(0, n_kt, body, 0)

        def build_v(src_ref, dst_ref):
            zero = jnp.zeros((hd, tk), BF16)
            row = lax.broadcasted_iota(jnp.int32, (SUM_ROWS, 2 * tk), 0)
            col = lax.broadcasted_iota(jnp.int32, (SUM_ROWS, 2 * tk), 1)
            ones_rows = jnp.where(((row == 0) & (col < tk)) | ((row == 1) & (col >= tk)), 1.0, 0.0).astype(BF16)
            for j in range(n_kt):
                x = src_ref[0, :, j * tk:(j + 1) * tk]
                dst_ref[j, 0:hd, 0:tk] = x
                dst_ref[j, 0:hd, tk:2 * tk] = zero
                dst_ref[j, hd:2 * hd, 0:tk] = zero
                dst_ref[j, hd:2 * hd, tk:2 * tk] = x
                dst_ref[j, 2 * hd:2 * hd + SUM_ROWS, :] = ones_rows

        build_k(0, ks_ref)
        build_k(1, kw_ref)
        build_v(vst_ref, vs_ref)
        build_v(vwt_ref, vw_ref)

    t0 = qi * tq
    key_i = lax.broadcasted_iota(jnp.int32, (tk, tq), 0)
    qry_t = t0 + lax.broadcasted_iota(jnp.int32, (tk, tq), 1)
    slab_lo = lax.broadcasted_iota(jnp.int32, (LANES, tq), 0) < hd
    acc_row = lax.broadcasted_iota(jnp.int32, (LANES + SUM_ROWS, tq), 0)
    slab_a = (acc_row < hd) | (acc_row == LANES)
    q_pairs = [qt_ref[0, p * LANES:(p + 1) * LANES, :] for p in range(n_pairs)]

    last = (t0 + tq - 1) // tk

    def scores(k_ref, j):
        kt = k_ref[j]
        return [_dot(kt, q_pairs[p]) for p in range(n_pairs)]

    n_cmp_pad = kc_ref.shape[2] // 2
    blk_i = lax.broadcasted_iota(jnp.int32, (n_cmp_pad, tq), 0)
    blk_t = t0 + lax.broadcasted_iota(jnp.int32, (n_cmp_pad, tq), 1)
    cmp_ok = (blk_i * CMP_STRIDE + (CMP_BLOCK - 1)) <= blk_t
    kc = kc_ref[0, 0]
    vct = vct_ref[0, 0]
    s_cmp = [_dot(kc, q_pairs[p]) for p in range(n_pairs)]
    n_win = (WINDOW + tq) // tk
    win_tiles = [last - (n_win - 1) + k for k in range(n_win)]
    n_near = SLC_NEAR_TILES
    near_tiles = [last - k for k in range(n_near)]
    up_front = ([(ks_ref, 0)] + [(kw_ref, jnp.maximum(jw, 0)) for jw in win_tiles]
                + [(ks_ref, jnp.maximum(jn, 0)) for jn in near_tiles])
    for slot, (k_ref, j0) in enumerate(up_front):
        s_first = scores(k_ref, j0)
        for p in range(n_pairs):
            s_ref[slot, p] = s_first[p]
    p_sum = jnp.zeros((n_cmp_pad, tq), F32)
    p_cmp = []
    for p in range(n_pairs):
        probs = []
        for h in range(2):
            sh = jnp.where(cmp_ok, s_cmp[p][h * n_cmp_pad:(h + 1) * n_cmp_pad], NEG_INF)
            mh = jnp.max(sh, axis=0, keepdims=True)
            eh = jnp.where(cmp_ok, jnp.exp2(sh - mh), 0.0)
            den = jnp.sum(eh, axis=0, keepdims=True)
            ph = eh / jnp.where(den > 0.0, den, 1.0)
            p_sum = p_sum + ph
            probs.append(ph.astype(BF16))
        p_cmp.append(jnp.concatenate(probs, axis=0))
    o_cmp = [_dot(vct, p_cmp[p]) for p in range(n_pairs)]

    n_sel = t_len // SLC_BLOCK
    hi, mid, lo = _split3(p_sum)
    ovt = ovt_ref[...]
    p_sel = ((_dot(ovt, hi) + _dot(ovt, mid)) + _dot(ovt, lo))[0:n_sel]
    sel_i = lax.broadcasted_iota(jnp.int32, (n_sel, tq), 0)
    cur = (t0 + lax.broadcasted_iota(jnp.int32, (n_sel, tq), 1)) // SLC_BLOCK
    forced = (sel_i == 0) | (sel_i == cur) | (sel_i == cur - 1)
    score = jnp.where(forced, FORCE_SCORE, p_sel)
    score = jnp.where(sel_i <= cur, score, -jnp.inf)
    rank = jnp.zeros((n_sel, tq), jnp.int32)
    row_grp = SUBLANES
    grp_i = lax.broadcasted_iota(jnp.int32, (row_grp, tq), 0)
    for i in range(n_sel):
        ci = score[i:i + 1, :]
        ahead = []
        for r0 in range(0, n_sel, row_grp):
            rows = slice(r0, r0 + row_grp)
            if r0 > i:
                ahead.append(ci >= score[rows])
            elif r0 + row_grp <= i:
                ahead.append(ci > score[rows])
            else:
                ahead.append((ci > score[rows]) | ((ci == score[rows]) & (grp_i > i - r0)))
        rank = rank + jnp.where(jnp.concatenate(ahead, axis=0), 1, 0)
    chosen = jnp.where(rank < min(SLC_TOPK, n_sel), 1.0, 0.0)
    for i in range(n_sel):
        ch_ref[i] = chosen[i:i + 1, :]

    def tile_softmax(slot, bias, m_get, m_put):
        alphas = {}
        probs = {}
        for hh in range(HPG):
            p, h = divmod(hh, 2)
            a_parts = []
            p_parts = []
            for qh in range(tq // LANES):
                ql = slice(qh * LANES, (qh + 1) * LANES)
                sh = s_ref[slot, p, h * tk:(h + 1) * tk, ql] + (bias if bias.shape == (1, 1) else bias[:, ql])
                m_prev = m_get(hh, qh)
                m_new = jnp.maximum(m_prev, jnp.max(sh, axis=0, keepdims=True))
                m_put(hh, qh, m_new)
                p_parts.append(jnp.exp2(sh - m_new).astype(BF16))
                a_parts.append(jnp.exp2(m_prev - m_new))
            alphas[hh] = jnp.concatenate(a_parts, axis=1)
            probs[hh] = jnp.concatenate(p_parts, axis=1)
        return ([jnp.concatenate([probs[2 * p], probs[2 * p + 1]], axis=0) for p in range(n_pairs)],
                [jnp.where(slab_a, alphas[2 * p], alphas[2 * p + 1]) for p in range(n_pairs)])

    def normalised(acc):
        inv = jnp.where(slab_lo, 1.0 / acc[LANES:LANES + 1, :], 1.0 / acc[LANES + 1:LANES + 2, :])
        return acc[0:LANES] * inv

    m_win = {}
    acc_win = [jnp.zeros((LANES + SUM_ROWS, tq), F32) for _ in range(n_pairs)]
    for k, jw in enumerate(win_tiles):
        rel_hi = (n_win - k) * tk - 1
        rel_lo = rel_hi - (tq - 1) - (tk - 1)
        exists = jnp.where(jw >= 0, 0.0, NEG_INF).astype(F32).reshape(1, 1)
        if rel_lo >= 0 and rel_hi < WINDOW:
            bias = exists
        else:
            rel = qry_t - (jw * tk + key_i)
            inside = (rel < WINDOW) if rel_lo >= 0 else (rel >= 0) if rel_hi < WINDOW else (rel >= 0) & (rel < WINDOW)
            bias = jnp.where(inside, exists, NEG_INF)
        probs, a_rows = tile_softmax(1 + k, bias,
                                     lambda hh, qh: m_win.get((hh, qh), jnp.full((1, LANES), -jnp.inf, F32)),
                                     lambda hh, qh, v: m_win.__setitem__((hh, qh), v))
        vt = vw_ref[jnp.maximum(jw, 0)]
        acc_win = [acc_win[p] * a_rows[p] + _dot(vt, probs[p]) for p in range(n_pairs)]
    o_win = [normalised(acc_win[p]) for p in range(n_pairs)]

    def picked_bias(j, also=None):
        per_tile = tk // SLC_BLOCK
        picked = jnp.concatenate([jnp.broadcast_to(ch_ref[j * per_tile + i], (SLC_BLOCK, tq)) for i in range(per_tile)],
                                 axis=0) > 0.5
        return jnp.where(picked if also is None else picked & also, 0.0, NEG_INF)

    m_near = {}
    acc_near = [jnp.zeros((LANES + SUM_ROWS, tq), F32) for _ in range(n_pairs)]
    for k, jn in enumerate(near_tiles):
        jc = jnp.maximum(jn, 0)
        causal = (jn * tk + key_i) <= qry_t
        if k > 0:
            causal = causal & (jn >= 0)
        probs, a_rows = tile_softmax(1 + n_win + k, picked_bias(jc, causal),
                                     lambda hh, qh: m_near.get((hh, qh), jnp.full((1, LANES), -jnp.inf, F32)),
                                     lambda hh, qh, v: m_near.__setitem__((hh, qh), v))
        vt = vs_ref[jc]
        acc_near = [acc_near[p] * a_rows[p] + _dot(vt, probs[p]) for p in range(n_pairs)]

    def m_put(hh, qh, v):
        m_ref[hh, :, qh * LANES:(qh + 1) * LANES] = v

    for (hh, qh), v in m_near.items():
        m_put(hh, qh, v)
    for p in range(n_pairs):
        acc_ref[p] = acc_near[p]
    n_far = jnp.maximum(last + 1 - n_near, 0)

    def slc_step(j, carry):
        s_next = scores(ks_ref, jnp.minimum(j + 1, n_far - 1))
        probs, a_rows = tile_softmax(0, picked_bias(j), lambda hh, qh: m_ref[hh, :, qh * LANES:(qh + 1) * LANES], m_put)
        vt = vs_ref[j]
        for p in range(n_pairs):
            pv = _dot(vt, probs[p])
            s_ref[0, p] = s_next[p]
            acc_ref[p] = acc_ref[p] * a_rows[p] + pv
        return carry

    lax.fori_loop(0, n_far, slc_step, 0)
    o_slc = [normalised(acc_ref[p]) for p in range(n_pairs)]

    gates = gt_ref[0]
    gain = gain_ref[...]
    for p in range(n_pairs):
        o = jnp.zeros((LANES, tq), F32)
        for c, branch in enumerate((o_cmp[p], o_slc[p], o_win[p])):
            r = c * HPG + 2 * p
            o = o + jnp.where(slab_lo, gates[r:r + 1, :], gates[r + 1:r + 2, :]) * branch
        sq = o * o
        ms_a = jnp.sum(sq[0:hd], axis=0, keepdims=True)
        ms_b = jnp.sum(sq[hd:2 * hd], axis=0, keepdims=True)
        ms = jnp.where(slab_lo, ms_a, ms_b) * (1.0 / hd)
        o = o * lax.rsqrt(ms + EPS)
        o_ref[0, p * LANES:(p + 1) * LANES, :] = (o * gain[p * LANES:(p + 1) * LANES, :]).astype(o_ref.dtype)


def _nsa_call(qt, ksw, vt, kc, vct, gt, gain, ovt):
    b, _, t = qt.shape
    tq, tk = ATT_TQ, ATT_TK
    n_kt = t // tk
    gw = HPG * NSA_HEAD_DIM
    hd = NSA_HEAD_DIM
    k_scratch = pltpu.VMEM((n_kt, 2 * tk, LANES), BF16)
    v_scratch = pltpu.VMEM((n_kt, LANES + SUM_ROWS, 2 * tk), BF16)
    return pl.pallas_call(
        _nsa_body,
        grid=(b, NSA_KV_HEADS, t // tq),
        in_specs=[
            pl.BlockSpec((1, gw, tq), lambda bi, gi, qi: (bi, gi, qi)),
            pl.BlockSpec((1, t, 2 * KV_WIDTH), lambda bi, gi, qi: (bi, 0, 0)),
            pl.BlockSpec((1, hd, t), lambda bi, gi, qi: (bi, gi, 0)),
            pl.BlockSpec((1, hd, t), lambda bi, gi, qi: (bi, NSA_KV_HEADS + gi, 0)),
            pl.BlockSpec((1, 1) + kc.shape[2:], lambda bi, gi, qi: (bi, gi, 0, 0)),
            pl.BlockSpec((1, 1) + vct.shape[2:], lambda bi, gi, qi: (bi, gi, 0, 0)),
            pl.BlockSpec((1, LANES, tq), lambda bi, gi, qi: (bi, gi, qi)),
            pl.BlockSpec((gw, 1), lambda bi, gi, qi: (gi, 0)),
            pl.BlockSpec(ovt.shape, lambda bi, gi, qi: (0, 0)),
        ],
        out_specs=pl.BlockSpec((1, gw, tq), lambda bi, gi, qi: (bi, gi, qi)),
        out_shape=jax.ShapeDtypeStruct((b, NSA_WIDTH, t), BF16),
        scratch_shapes=[k_scratch, k_scratch, v_scratch, v_scratch,
                        pltpu.VMEM((HPG, 1, tq), F32),
                        pltpu.VMEM((HPG // 2, LANES + SUM_ROWS, tq), F32), pltpu.VMEM((1 + (WINDOW + tq) // tk + SLC_NEAR_TILES, HPG // 2, 2 * tk, tq), F32),
                        pltpu.VMEM((t // SLC_BLOCK, 1, tq), F32)],
        compiler_params=pltpu.CompilerParams(dimension_semantics=("arbitrary", "arbitrary", "arbitrary"),
                                             vmem_limit_bytes=VMEM_LIMIT),
        name="nsa_attention",
    )(qt, ksw, vt, vt, kc, vct, gt, gain, ovt)


def _ffn_body(x_ref, oh_ref, on_ref, woh_ref, won_ref, g2_ref, wg_ref, wu_ref, wd_ref, cw_ref, gf_ref,
              out_ref, halo_ref, act_ref, *, tiles_per_seq):
    tm = x_ref.shape[0]
    x1 = x_ref[...] + _dot(oh_ref[...], woh_ref[...]) + _dot_tn(on_ref[0], won_ref[...])
    hb = _rms(x1, g2_ref[...]).astype(BF16)
    row = lax.broadcasted_iota(jnp.int32, (tm, FFN_TC), 0)

    @pl.when((pl.program_id(0) % tiles_per_seq) == 0)
    def _sequence_start():
        halo_ref[...] = jnp.zeros_like(halo_ref)

    def activation(c, gate, up):
        cols = slice(c * FFN_TC, (c + 1) * FFN_TC)
        halo = halo_ref[:, cols]
        halo_ref[:, cols] = gate[tm - SUBLANES:tm, :]
        last1 = halo[SUBLANES - 1:SUBLANES, :]
        last2 = halo[SUBLANES - 2:SUBLANES - 1, :]
        prev1 = jnp.where(row == 0, last1, pltpu.roll(gate, 1, 0))
        prev2 = jnp.where(row == 0, last2, jnp.where(row == 1, last1, pltpu.roll(gate, 2, 0)))
        cw = cw_ref[:, cols]
        y = cw[0:1, :] * prev2 + cw[1:2, :] * prev1 + cw[2:3, :] * gate + cw[3:4, :]
        return (jax.nn.silu(y) * up).astype(BF16)

    chunk = lambda w_ref, c: _dot(hb, w_ref[:, c * FFN_TC:(c + 1) * FFN_TC])
    gate_up = (chunk(wg_ref, 0), chunk(wu_ref, 0))
    for c in range(FFN_NC):
        cur = gate_up
        if c + 1 < FFN_NC:
            gate_up = (chunk(wg_ref, c + 1), chunk(wu_ref, c + 1))
        act_ref[:, c * FFN_TC:(c + 1) * FFN_TC] = activation(c, *cur)
    acc = _dot(act_ref[...], wd_ref[...])
    out_ref[...] = _rms(x1 + acc, gf_ref[...])


def _ffn_call(x2, oh, on, woh, won, g2, wg, wu, wd, cw, gf, tiles_per_seq):
    n = x2.shape[0]
    tm = FFN_TM
    row = lambda w: pl.BlockSpec((tm, w), lambda i: (i, 0))
    full = lambda a: pl.BlockSpec(a.shape, lambda i: (0,) * a.ndim, pipeline_mode=pl.Buffered(1))
    return pl.pallas_call(
        functools.partial(_ffn_body, tiles_per_seq=tiles_per_seq),
        grid=(n // tm,),
        in_specs=[row(D_MODEL), row(HG_WIDTH),
                  pl.BlockSpec((1, NSA_WIDTH, tm), lambda i: (i // tiles_per_seq, 0, i % tiles_per_seq)),
                  full(woh), full(won), full(g2),
                  full(wg), full(wu), full(wd), full(cw), full(gf)],
        out_specs=row(D_MODEL),
        out_shape=jax.ShapeDtypeStruct((n, D_MODEL), F32),
        scratch_shapes=[pltpu.VMEM((SUBLANES, D_FF), F32), pltpu.VMEM((tm, D_FF), BF16)],
        compiler_params=pltpu.CompilerParams(dimension_semantics=("arbitrary",),
                                             vmem_limit_bytes=VMEM_LIMIT),
        name="outproj_convffn",
    )(x2, oh, on, woh, won, g2, wg, wu, wd, cw, gf)


def _rope_angles(positions):
    inv_freq = ROPE_THETA ** (-jnp.arange(ROPE_HALF, dtype=F32) * 2.0 / ROPE_DIM)
    ang = positions.astype(F32)[..., None] * inv_freq
    return jnp.concatenate([jnp.cos(ang), jnp.sin(ang)], axis=-1).transpose(0, 2, 1)


def _layer(x, positions, ln1, w_in, lb, hg_gain, pe_k, pe_v, k_w1, k_w2, v_w1, v_w2, nsa_gain, w_o, ln2,
           w_gate, w_up, conv_w, conv_b, w_down, final_gain):
    b, t, d = x.shape
    n = b * t
    assert d == D_MODEL and t % FFN_TM == 0 and t % PROJ_TM == 0 and t % ATT_TQ == 0 and t % HG_TT == 0
    n_grp = t // CMP_STRIDE
    assert n_grp == LANES, "compressed-block axis is laid out on exactly one lane tile"
    n_sel = t // SLC_BLOCK
    assert n_sel % 8 == 0 and n_sel <= LANES and ATT_TK % SLC_BLOCK == 0
    x2 = x.reshape(n, d)

    splits = np.cumsum([0, 4 * HG_WIDTH, NSA_WIDTH] + [KV_WIDTH] * 6 + [N_GATES])
    seg = lambda i: w_in[:, splits[i]:splits[i + 1]]
    wh = seg(0).astype(BF16)
    wk = jnp.concatenate([seg(2), seg(3), seg(4), seg(6)], axis=1).astype(BF16)
    wgate = seg(8).reshape(d, 3, NSA_KV_HEADS, HPG).transpose(0, 2, 1, 3).reshape(d, NSA_KV_HEADS, 3 * HPG)
    wgate = jnp.pad(wgate, ((0, 0), (0, 0), (0, LANES - 3 * HPG))).reshape(d, NSA_KV_HEADS * LANES)
    wt = jnp.concatenate([seg(1), seg(5), seg(7), wgate], axis=1).T.astype(BF16)
    cs = _rope_angles(positions)

    hg, kcn, vcn, ksw, qt, vt, gt = _inproj_call(x2, ln1.reshape(1, d), wh, wk, wt, cs, t // PROJ_TM)

    mst, lvl = _hgrn_tables()
    o_hg = _hgrn_call(hg.reshape(b, t, 4 * HG_WIDTH), lb.reshape(1, HG_WIDTH).astype(F32),
                      hg_gain.reshape(1, HG_WIDTH), mst, lvl)

    per_lane = lambda a: jnp.broadcast_to(a.reshape(2, CMP_STRIDE, 1, NSA_HEAD_DIM, -1),
                                          (2, CMP_STRIDE, NSA_KV_HEADS, NSA_HEAD_DIM, a.shape[-1]))
    w1_rows = lambda w1: per_lane(w1).reshape(2, CMP_STRIDE * LANES, CMP_HIDDEN).astype(BF16)
    pe_rows = lambda pe: per_lane(pe[..., None]).reshape(2, 1, CMP_STRIDE * LANES)
    zeros_w2 = jnp.zeros((CMP_HIDDEN, NSA_HEAD_DIM), F32)
    place = lambda w2: jnp.stack([jnp.concatenate([w2, zeros_w2], 1), jnp.concatenate([zeros_w2, w2], 1)])
    kc, vct = _cmp_call(kcn.reshape(b, t, KV_WIDTH), vcn.reshape(b, t, KV_WIDTH), pe_rows(pe_k), pe_rows(pe_v),
                        w1_rows(k_w1), w1_rows(v_w1),
                        place(k_w2).astype(BF16), place(v_w2).transpose(0, 2, 1).astype(BF16))

    cmp_start = np.arange(n_grp) * CMP_STRIDE
    cmp_end = cmp_start + CMP_BLOCK - 1
    sel_start = np.arange(LANES) * SLC_BLOCK
    overlap = ((cmp_start[:, None] <= sel_start[None, :] + SLC_BLOCK - 1) & (cmp_end[:, None] >= sel_start[None, :])
               & (np.arange(LANES)[None, :] < n_sel) & (np.arange(n_grp)[:, None] < n_grp - 1))
    ovt = jnp.asarray(overlap.T.astype(np.float32), BF16)
    o_nsa = _nsa_call(qt, ksw.reshape(b, t, 2 * KV_WIDTH), vt, kc, vct, gt, nsa_gain.reshape(NSA_WIDTH, 1), ovt)

    cw = jnp.concatenate([conv_w, conv_b[None, :], jnp.zeros((SUBLANES - CONV_WIDTH - 1, D_FF), F32)], axis=0)
    out = _ffn_call(x2, o_hg.reshape(n, HG_WIDTH), o_nsa,
                    w_o[:HG_WIDTH].astype(BF16), w_o[HG_WIDTH:].astype(BF16), ln2.reshape(1, d),
                    w_gate.astype(BF16), w_up.astype(BF16), w_down.astype(BF16), cw,
                    final_gain.reshape(1, d), t // FFN_TM)
    return out.reshape(b, t, d)


def kernel(x, positions, ln1_gain, w_in, hgrn_lb_param, hgrn_out_gain, cmp_pe_k, cmp_pe_v, cmp_k_w1, cmp_k_w2,
           cmp_v_w1, cmp_v_w2, nsa_out_gain, w_o, ln2_gain, ffn_w_gate, ffn_w_up, ffn_conv_w, ffn_conv_b,
           ffn_w_down, final_gain):
    depth = ln1_gain.shape[0]
    assert depth == 1, "the fused final norm assumes a single layer"
    lower_bounds = jnp.cumsum(jax.nn.softmax(hgrn_lb_param.astype(F32), axis=0), axis=0)
    l = 0
    return _layer(x, positions, ln1_gain[l], w_in[l], lower_bounds[l], hgrn_out_gain[l], cmp_pe_k[l], cmp_pe_v[l],
                  cmp_k_w1[l], cmp_k_w2[l], cmp_v_w1[l], cmp_v_w2[l], nsa_out_gain[l], w_o[l], ln2_gain[l],
                  ffn_w_gate[l], ffn_w_up[l], ffn_conv_w[l], ffn_conv_b[l], ffn_w_down[l], final_gain)
```

```python
import functools

import jax
import jax.numpy as jnp
import numpy as np
from jax import lax
from jax.experimental import pallas as pl
from jax.experimental.pallas import tpu as pltpu

F32 = jnp.float32
BF16 = jnp.bfloat16

D_MODEL = 1024
HG_HEADS = 4
HG_DK = 128
HG_DV = 128
HG_WIDTH = HG_HEADS * HG_DV
NSA_HEADS = 8
NSA_KV_HEADS = 2
NSA_HEAD_DIM = 64
HPG = NSA_HEADS // NSA_KV_HEADS
NSA_WIDTH = NSA_HEADS * NSA_HEAD_DIM
KV_WIDTH = NSA_KV_HEADS * NSA_HEAD_DIM
CMP_BLOCK = 32
CMP_STRIDE = 16
CMP_HIDDEN = 256
SLC_BLOCK = 64
SLC_TOPK = 16
WINDOW = 512
ROPE_THETA = 500000.0
ROPE_DIM = NSA_HEAD_DIM // 4
ROPE_HALF = ROPE_DIM // 2
D_FF = 2816
CONV_WIDTH = 3
EPS = 1e-6
NEG_INF = -1e30
FORCE_SCORE = 1e4
N_GATES = 3 * NSA_HEADS
LOG2_E = 1.4426950408889634

LANES = 128
SUBLANES = 8
VMEM_LIMIT = 56 * 1024 * 1024

PROJ_TM = 512
HG_CHUNK = 128
HG_LEVELS = (16, 32, 64)
HG_DIAG = 16
HG_TT = 1024
ATT_TQ = 256
ATT_TK = 256
KEY_BLK = 128
SLC_NEAR_TILES = 1
SUM_ROWS = 16
FFN_TM = 512
FFN_TC = 256
FFN_NC = D_FF // FFN_TC


def _dot(a, b):
    return jnp.dot(a, b, preferred_element_type=F32)


def _dot_nt(a, b):
    return lax.dot_general(a, b, (((1,), (1,)), ((), ())), preferred_element_type=F32)


def _dot_tn(a, b):
    return lax.dot_general(a, b, (((0,), (0,)), ((), ())), preferred_element_type=F32)


def _split3(x):
    hi = x.astype(BF16)
    r = x - hi.astype(F32)
    mid = r.astype(BF16)
    lo = (r - mid.astype(F32)).astype(BF16)
    return hi, mid, lo


def _rms(x, gain):
    return x * lax.rsqrt(jnp.mean(x * x, axis=-1, keepdims=True) + EPS) * gain


def _inproj_body(x_ref, g_ref, wh_ref, wk_ref, wt_ref, cs_ref,
                 hg_ref, kcn_ref, vcn_ref, ksw_ref, qt_ref, vt_ref, gt_ref):
    hb = _rms(x_ref[...], g_ref[...]).astype(BF16)
    hg_ref[...] = _dot(hb, wh_ref[...])

    def rope(v, axis, cos, sin_hi, sin_lo):
        return (v * cos + pltpu.roll(v, ROPE_HALF, axis) * sin_hi
                + pltpu.roll(v, LANES - ROPE_HALF, axis) * sin_lo)

    cos = cs_ref[0, 0:ROPE_HALF, :]
    sin = cs_ref[0, ROPE_HALF:ROPE_DIM, :]
    tm = cos.shape[1]
    zero_h = jnp.zeros((ROPE_HALF, tm), F32)
    rest = NSA_HEAD_DIM - ROPE_DIM
    slab = lambda lo, hi, fill: jnp.concatenate([lo, hi, jnp.full((rest, tm), fill, F32)] * (LANES // NSA_HEAD_DIM), axis=0)
    tab_t = (slab(cos, cos, 1.0), slab(zero_h, sin, 0.0), slab(-sin, zero_h, 0.0))
    tab = tuple(a.T for a in tab_t)
    kn = _dot(hb, wk_ref[...])
    kcn_ref[...] = rope(kn[:, 0:LANES], 1, *tab)
    vcn_ref[...] = kn[:, LANES:2 * LANES]
    ksw_ref[:, 0:LANES] = rope(kn[:, 2 * LANES:3 * LANES], 1, *tab).astype(BF16)
    ksw_ref[:, LANES:2 * LANES] = rope(kn[:, 3 * LANES:4 * LANES], 1, *tab).astype(BF16)

    rt = _dot_nt(wt_ref[...], hb)
    scale = NSA_HEAD_DIM ** -0.5 * LOG2_E
    for j in range(NSA_WIDTH // LANES):
        sl = slice(j * LANES, (j + 1) * LANES)
        qt_ref[0, sl, :] = (rope(rt[sl], 0, *tab_t) * scale).astype(BF16)
    vt_ref[0] = rt[NSA_WIDTH:NSA_WIDTH + 2 * KV_WIDTH].astype(BF16)
    gt_ref[0] = jax.nn.sigmoid(rt[NSA_WIDTH + 2 * KV_WIDTH:])


def _inproj_call(x2, gain, wh, wk, wt, cs, tiles_per_seq):
    n = x2.shape[0]
    tm = PROJ_TM
    t = tiles_per_seq * tm
    b = n // t
    row = lambda w: pl.BlockSpec((tm, w), lambda i: (i, 0))
    col = lambda h: pl.BlockSpec((1, h, tm), lambda i: (i // tiles_per_seq, 0, i % tiles_per_seq))
    full = lambda a: pl.BlockSpec(a.shape, lambda i: (0, 0))
    gate_rows = NSA_KV_HEADS * LANES
    return pl.pallas_call(
        _inproj_body,
        grid=(n // tm,),
        in_specs=[row(D_MODEL), full(gain), full(wh), full(wk), full(wt),
                  col(ROPE_DIM)],
        out_specs=[row(4 * HG_WIDTH), row(KV_WIDTH), row(KV_WIDTH), row(2 * KV_WIDTH),
                   col(NSA_WIDTH), col(2 * KV_WIDTH), col(gate_rows)],
        out_shape=[jax.ShapeDtypeStruct((n, 4 * HG_WIDTH), F32),
                   jax.ShapeDtypeStruct((n, KV_WIDTH), F32),
                   jax.ShapeDtypeStruct((n, KV_WIDTH), F32),
                   jax.ShapeDtypeStruct((n, 2 * KV_WIDTH), BF16),
                   jax.ShapeDtypeStruct((b, NSA_WIDTH, t), BF16),
                   jax.ShapeDtypeStruct((b, 2 * KV_WIDTH, t), BF16),
                   jax.ShapeDtypeStruct((b, gate_rows, t), F32)],
        compiler_params=pltpu.CompilerParams(dimension_semantics=("arbitrary",),
                                             vmem_limit_bytes=VMEM_LIMIT),
        name="inproj",
    )(x2, gain, wh, wk, wt, cs)


def _hgrn_tables():
    L = HG_CHUNK
    t = np.arange(L)[:, None]
    u = np.arange(L)[None, :]
    level = np.where(((t // HG_DIAG) == (u // HG_DIAG)) & (u <= t), 1, 0)
    for li, s in enumerate(HG_LEVELS):
        same = (t // (2 * s)) == (u // (2 * s))
        right = (t % (2 * s)) >= s
        level = np.where(same & right & ((u % (2 * s)) < s), li + 2, level)
    return jnp.asarray((u <= t).astype(np.float32), BF16), jnp.asarray(level, jnp.int32)


def _hgrn_body(q_ref, f_ref, i_ref, g_ref, lb_ref, gain_ref, mst_ref, lvl_ref, o_ref, st_ref):
    L = HG_CHUNK
    n_chunks = q_ref.shape[1] // L

    @pl.when(pl.program_id(1) == 0)
    def _sequence_start():
        st_ref[...] = jnp.zeros_like(st_ref)

    def chunk(c, carry):
        rows = pl.ds(pl.multiple_of(c * L, L), L)
        heads = range(HG_HEADS)
        cols = [slice(h * HG_DK, (h + 1) * HG_DK) for h in heads]
        mst = mst_ref[...]
        lvl = lvl_ref[...]
        n_lv = len(HG_LEVELS)
        row_i = lax.broadcasted_iota(jnp.int32, (L, HG_DK), 0)
        q = [q_ref[0, rows, cols[h]] for h in heads]
        vb = [i_ref[0, rows, cols[h]].astype(BF16) for h in heads]
        f = [lb_ref[:, cols[h]] + (1.0 - lb_ref[:, cols[h]]) * jax.nn.sigmoid(f_ref[0, rows, cols[h]]) for h in heads]
        k = [1.0 - f[h] for h in heads]
        parts = [_split3(jnp.log2(f[h])) for h in heads]
        e_full = [(_dot(mst, parts[h][0]) + _dot(mst, parts[h][1])) + _dot(mst, parts[h][2]) for h in heads]
        b_last = [e_full[h][L - 1:L, :] for h in heads]

        def rel_to(b, blk, off):
            refs = []
            for r0 in range(0, L, blk):
                r = r0 + off - 1
                ref = b[r:r + 1, :] if r >= 0 else jnp.zeros((1, HG_DK), F32)
                refs.append(jnp.broadcast_to(ref, (blk, HG_DK)))
            return b - jnp.concatenate(refs, axis=0)

        def level_sums(b):
            out = [rel_to(b, HG_DIAG, 0)]
            for s_half in HG_LEVELS:
                d = rel_to(b, 2 * s_half, s_half)
                out.append(jnp.where((row_i % (2 * s_half)) >= s_half, d, -d))
            return out

        e = [level_sums(e_full[h]) for h in heads]
        wq = [[jnp.exp2(e[h][l]) for l in range(n_lv + 1)] for h in heads]
        wk = [[jnp.exp2(-e[h][0])] + wq[h][1:] for h in heads]
        prod = [[_dot_nt((q[h] * wq[h][l]).astype(BF16), (k[h] * wk[h][l]).astype(BF16)) for l in range(n_lv + 1)]
                for h in heads]
        st = [st_ref[h] for h in heads]
        inter = [_dot_nt((q[h] * jnp.exp2(e_full[h])).astype(BF16), st[h].astype(BF16)) for h in heads]
        k_dec = [(k[h] * jnp.exp2(b_last[h] - e_full[h])).astype(BF16) for h in heads]
        upd = [_dot_tn(vb[h], k_dec[h]) for h in heads]
        for h in heads:
            st_ref[h] = st[h] * jnp.exp2(b_last[h]) + upd[h]
        a = []
        for h in heads:
            ah = jnp.where(lvl == 1, prod[h][0], 0.0)
            for l in range(1, n_lv + 1):
                ah = jnp.where(lvl == l + 1, prod[h][l], ah)
            a.append(ah.astype(BF16))
        o = [_dot(a[h], vb[h]) + inter[h] for h in heads]
        for h in heads:
            oh = o[h] * lax.rsqrt(jnp.mean(o[h] * o[h], axis=-1, keepdims=True) + EPS) * gain_ref[:, cols[h]]
            o_ref[0, rows, cols[h]] = (oh * jax.nn.silu(g_ref[0, rows, cols[h]])).astype(o_ref.dtype)
        return carry

    lax.fori_loop(0, n_chunks, chunk, 0, unroll=True)


def _hgrn_call(hg, lb, gain, mst, lvl):
    b, t, _ = hg.shape
    tt = HG_TT
    col = lambda k: pl.BlockSpec((1, tt, HG_WIDTH), lambda bi, ti: (bi, ti, k))
    full = lambda a: pl.BlockSpec(a.shape, lambda bi, ti: (0, 0))
    return pl.pallas_call(
        _hgrn_body,
        grid=(b, t // tt),
        in_specs=[col(0), col(1), col(2), col(3), full(lb), full(gain), full(mst), full(lvl)],
        out_specs=pl.BlockSpec((1, tt, HG_WIDTH), lambda bi, ti: (bi, ti, 0)),
        out_shape=jax.ShapeDtypeStruct((b, t, HG_WIDTH), BF16),
        scratch_shapes=[pltpu.VMEM((HG_HEADS, HG_DV, HG_DK), F32)],
        compiler_params=pltpu.CompilerParams(dimension_semantics=("arbitrary", "arbitrary"),
                                             vmem_limit_bytes=VMEM_LIMIT),
        name="hgrn2",
    )(hg, hg, hg, hg, lb, gain, mst, lvl)


def _cmp_body(kcn_ref, vcn_ref, pek_ref, pev_ref, w1k_ref, w1v_ref, w2k_ref, w2v_ref, kc_ref, vc_ref):
    nb = kcn_ref.shape[1] // CMP_STRIDE
    lane_grp = (lax.broadcasted_iota(jnp.int32, (nb, CMP_STRIDE * LANES), 1) // NSA_HEAD_DIM) % NSA_KV_HEADS

    def hidden(src_ref, pe_ref, w1_ref):
        x = jnp.concatenate([src_ref[0, pl.ds(l, nb, stride=CMP_STRIDE), :]
                             for l in range(CMP_STRIDE)], axis=1)
        halves = [x + pe_ref[i] for i in range(2)]
        out = []
        for g in range(NSA_KV_HEADS):
            u, v = (_dot(jnp.where(lane_grp == g, halves[i], 0.0).astype(BF16), w1_ref[i]) for i in range(2))
            out.append(jax.nn.silu(u + pltpu.roll(v, nb - 1, 0)).astype(BF16))
        return out

    hk = hidden(kcn_ref, pek_ref, w1k_ref)
    hv = hidden(vcn_ref, pev_ref, w1v_ref)
    for g in range(NSA_KV_HEADS):
        kc_ref[0, g, 0:nb, :] = _dot(hk[g], w2k_ref[0]).astype(kc_ref.dtype)
        kc_ref[0, g, nb:2 * nb, :] = _dot(hk[g], w2k_ref[1]).astype(kc_ref.dtype)
        vc_ref[0, g, :, 0:nb] = _dot_nt(w2v_ref[0], hv[g]).astype(vc_ref.dtype)
        vc_ref[0, g, :, nb:2 * nb] = _dot_nt(w2v_ref[1], hv[g]).astype(vc_ref.dtype)


def _cmp_call(kcn, vcn, pek, pev, w1k, w1v, w2k, w2v):
    b, t, w = kcn.shape
    nb = t // CMP_STRIDE
    full = lambda a: pl.BlockSpec(a.shape, lambda bi: (0,) * a.ndim)
    out = lambda r, c: pl.BlockSpec((1, NSA_KV_HEADS, r, c), lambda bi: (bi, 0, 0, 0))
    return pl.pallas_call(
        _cmp_body,
        grid=(b,),
        in_specs=[pl.BlockSpec((1, t, w), lambda bi: (bi, 0, 0)), pl.BlockSpec((1, t, w), lambda bi: (bi, 0, 0)),
                  full(pek), full(pev), full(w1k), full(w1v), full(w2k), full(w2v)],
        out_specs=[out(2 * nb, LANES), out(LANES, 2 * nb)],
        out_shape=[jax.ShapeDtypeStruct((b, NSA_KV_HEADS, 2 * nb, LANES), BF16),
                   jax.ShapeDtypeStruct((b, NSA_KV_HEADS, LANES, 2 * nb), BF16)],
        compiler_params=pltpu.CompilerParams(dimension_semantics=("arbitrary",),
                                             vmem_limit_bytes=VMEM_LIMIT),
        name="nsa_compress",
    )(kcn, vcn, pek, pev, w1k, w1v, w2k, w2v)


def _nsa_body(qt_ref, ksw_ref, vst_ref, vwt_ref, kc_ref, vct_ref, gt_ref, gain_ref, ovt_ref, o_ref,
              ks_ref, kw_ref, vs_ref, vw_ref, m_ref, acc_ref, s_ref, ch_ref):
    g = pl.program_id(1)
    qi = pl.program_id(2)
    tq = ATT_TQ
    tk = ATT_TK
    t_len = ksw_ref.shape[1]
    n_kt = t_len // tk
    n_pairs = HPG // 2
    hd = NSA_HEAD_DIM

    @pl.when(qi == 0)
    def _build_kv():
        lane = lax.broadcasted_iota(jnp.int32, (tk, LANES), 1)
        lo_lane = lane < hd
        keep = (lane // hd) == g

        def build_k(src_col, dst_ref):
            def body(j, carry):
                rows = pl.ds(pl.multiple_of(j * tk, tk), tk)
                x = ksw_ref[0, rows, src_col * LANES:(src_col + 1) * LANES].astype(F32)
                dup = jnp.where(keep, x, pltpu.roll(x, hd, 1))
                dst_ref[j, 0:tk, :] = jnp.where(lo_lane, dup, 0.0).astype(BF16)
                dst_ref[j, tk:2 * tk, :] = jnp.where(lo_lane, 0.0, dup).astype(BF16)
                return carry
            lax.fori_loop(0, n_kt, body, 0)

        def build_v(src_ref, dst_ref):
            zero = jnp.zeros((hd, tk), BF16)
            row = lax.broadcasted_iota(jnp.int32, (SUM_ROWS, 2 * tk), 0)
            col = lax.broadcasted_iota(jnp.int32, (SUM_ROWS, 2 * tk), 1)
            ones_rows = jnp.where(((row == 0) & (col < tk)) | ((row == 1) & (col >= tk)), 1.0, 0.0).astype(BF16)
            for j in range(n_kt):
                x = src_ref[0, :, j * tk:(j + 1) * tk]
                dst_ref[j, 0:hd, 0:tk] = x
                dst_ref[j, 0:hd, tk:2 * tk] = zero
                dst_ref[j, hd:2 * hd, 0:tk] = zero
                dst_ref[j, hd:2 * hd, tk:2 * tk] = x
                dst_ref[j, 2 * hd:2 * hd + SUM_ROWS, :] = ones_rows

        build_k(0, ks_ref)
        build_k(1, kw_ref)
        build_v(vst_ref, vs_ref)
        build_v(vwt_ref, vw_ref)

    t0 = qi * tq
    key_i = lax.broadcasted_iota(jnp.int32, (tk, tq), 0)
    qry_t = t0 + lax.broadcasted_iota(jnp.int32, (tk, tq), 1)
    slab_lo = lax.broadcasted_iota(jnp.int32, (LANES, tq), 0) < hd
    acc_row = lax.broadcasted_iota(jnp.int32, (LANES + SUM_ROWS, tq), 0)
    slab_a = (acc_row < hd) | (acc_row == LANES)
    q_pairs = [qt_ref[0, p * LANES:(p + 1) * LANES, :] for p in range(n_pairs)]

    last = (t0 + tq - 1) // tk

    def scores(k_ref, j):
        kt = k_ref[j]
        return [_dot(kt, q_pairs[p]) for p in range(n_pairs)]

    n_cmp_pad = kc_ref.shape[2] // 2
    blk_i = lax.broadcasted_iota(jnp.int32, (n_cmp_pad, tq), 0)
    blk_t = t0 + lax.broadcasted_iota(jnp.int32, (n_cmp_pad, tq), 1)
    cmp_ok = (blk_i * CMP_STRIDE + (CMP_BLOCK - 1)) <= blk_t
    kc = kc_ref[0, 0]
    vct = vct_ref[0, 0]
    s_cmp = [_dot(kc, q_pairs[p]) for p in range(n_pairs)]
    n_win = (WINDOW + tq) // tk
    win_tiles = [last - (n_win - 1) + k for k in range(n_win)]
    n_near = SLC_NEAR_TILES
    near_tiles = [last - k for k in range(n_near)]
    up_front = ([(ks_ref, 0)] + [(kw_ref, jnp.maximum(jw, 0)) for jw in win_tiles]
                + [(ks_ref, jnp.maximum(jn, 0)) for jn in near_tiles])
    for slot, (k_ref, j0) in enumerate(up_front):
        s_first = scores(k_ref, j0)
        for p in range(n_pairs):
            s_ref[slot, p] = s_first[p]
    p_sum = jnp.zeros((n_cmp_pad, tq), F32)
    p_cmp = []
    for p in range(n_pairs):
        probs = []
        for h in range(2):
            sh = jnp.where(cmp_ok, s_cmp[p][h * n_cmp_pad:(h + 1) * n_cmp_pad], NEG_INF)
            mh = jnp.max(sh, axis=0, keepdims=True)
            eh = jnp.where(cmp_ok, jnp.exp2(sh - mh), 0.0)
            den = jnp.sum(eh, axis=0, keepdims=True)
            ph = eh / jnp.where(den > 0.0, den, 1.0)
            p_sum = p_sum + ph
            probs.append(ph.astype(BF16))
        p_cmp.append(jnp.concatenate(probs, axis=0))
    o_cmp = [_dot(vct, p_cmp[p]) for p in range(n_pairs)]

    n_sel = t_len // SLC_BLOCK
    hi, mid, lo = _split3(p_sum)
    ovt = ovt_ref[...]
    p_sel = ((_dot(ovt, hi) + _dot(ovt, mid)) + _dot(ovt, lo))[0:n_sel]
    sel_i = lax.broadcasted_iota(jnp.int32, (n_sel, tq), 0)
    cur = (t0 + lax.broadcasted_iota(jnp.int32, (n_sel, tq), 1)) // SLC_BLOCK
    forced = (sel_i == 0) | (sel_i == cur) | (sel_i == cur - 1)
    score = jnp.where(forced, FORCE_SCORE, p_sel)
    score = jnp.where(sel_i <= cur, score, -jnp.inf)
    rank = jnp.zeros((n_sel, tq), jnp.int32)
    row_grp = SUBLANES
    grp_i = lax.broadcasted_iota(jnp.int32, (row_grp, tq), 0)
    for i in range(n_sel):
        ci = score[i:i + 1, :]
        ahead = []
        for r0 in range(0, n_sel, row_grp):
            rows = slice(r0, r0 + row_grp)
            if r0 > i:
                ahead.append(ci >= score[rows])
            elif r0 + row_grp <= i:
                ahead.append(ci > score[rows])
            else:
                ahead.append((ci > score[rows]) | ((ci == score[rows]) & (grp_i > i - r0)))
        rank = rank + jnp.where(jnp.concatenate(ahead, axis=0), 1, 0)
    chosen = jnp.where(rank < min(SLC_TOPK, n_sel), 1.0, 0.0)
    for i in range(n_sel):
        ch_ref[i] = chosen[i:i + 1, :]

    def live_keys(rel_at_origin, lower, upper):
        out = []
        for qh in range(tq // LANES):
            live = [kb for kb in range(tk // KEY_BLK)
                    if rel_at_origin + qh * LANES + LANES - 1 - kb * KEY_BLK >= lower
                    and rel_at_origin + qh * LANES - (kb * KEY_BLK + KEY_BLK - 1) < upper]
            out.append((min(live) * KEY_BLK, (max(live) + 1) * KEY_BLK))
        return out

    def tile_softmax(slot, bias, m_get, m_put, keys=None):
        alphas = {}
        probs = {}
        for hh in range(HPG):
            p, h = divmod(hh, 2)
            a_parts = []
            p_parts = []
            for qh in range(tq // LANES):
                ql = slice(qh * LANES, (qh + 1) * LANES)
                k_lo, k_hi = keys[qh] if keys is not None else (0, tk)
                sh = (s_ref[slot, p, h * tk + k_lo:h * tk + k_hi, ql]
                      + (bias if bias.shape == (1, 1) else bias[k_lo:k_hi, ql]))
                m_prev = m_get(hh, qh)
                m_new = jnp.maximum(m_prev, jnp.max(sh, axis=0, keepdims=True))
                m_put(hh, qh, m_new)
                piece = [jnp.zeros((k_lo, LANES), BF16)] if k_lo else []
                piece.append(jnp.exp2(sh - m_new).astype(BF16))
                if k_hi < tk:
                    piece.append(jnp.zeros((tk - k_hi, LANES), BF16))
                p_parts.append(jnp.concatenate(piece, axis=0) if len(piece) > 1 else piece[0])
                a_parts.append(jnp.exp2(m_prev - m_new))
            alphas[hh] = jnp.concatenate(a_parts, axis=1)
            probs[hh] = jnp.concatenate(p_parts, axis=1)
        return ([jnp.concatenate([probs[2 * p], probs[2 * p + 1]], axis=0) for p in range(n_pairs)],
                [jnp.where(slab_a, alphas[2 * p], alphas[2 * p + 1]) for p in range(n_pairs)])

    def normalised(acc):
        inv = jnp.where(slab_lo, 1.0 / acc[LANES:LANES + 1, :], 1.0 / acc[LANES + 1:LANES + 2, :])
        return acc[0:LANES] * inv

    m_win = {}
    acc_win = [jnp.zeros((LANES + SUM_ROWS, tq), F32) for _ in range(n_pairs)]
    for k, jw in enumerate(win_tiles):
        rel_hi = (n_win - k) * tk - 1
        rel_lo = rel_hi - (tq - 1) - (tk - 1)
        exists = jnp.where(jw >= 0, 0.0, NEG_INF).astype(F32).reshape(1, 1)
        if rel_lo >= 0 and rel_hi < WINDOW:
            bias = exists
        else:
            rel = qry_t - (jw * tk + key_i)
            inside = (rel < WINDOW) if rel_lo >= 0 else (rel >= 0) if rel_hi < WINDOW else (rel >= 0) & (rel < WINDOW)
            bias = jnp.where(inside, exists, NEG_INF)
        probs, a_rows = tile_softmax(1 + k, bias,
                                     lambda hh, qh: m_win.get((hh, qh), jnp.full((1, LANES), -jnp.inf, F32)),
                                     lambda hh, qh, v: m_win.__setitem__((hh, qh), v),
                                     keys=live_keys(rel_lo + tk - 1, 0, WINDOW))
        vt = vw_ref[jnp.maximum(jw, 0)]
        acc_win = [acc_win[p] * a_rows[p] + _dot(vt, probs[p]) for p in range(n_pairs)]
    o_win = [normalised(acc_win[p]) for p in range(n_pairs)]

    def picked_bias(j, also=None):
        per_tile = tk // SLC_BLOCK
        picked = jnp.concatenate([jnp.broadcast_to(ch_ref[j * per_tile + i], (SLC_BLOCK, tq)) for i in range(per_tile)],
                                 axis=0) > 0.5
        return jnp.where(picked if also is None else picked & also, 0.0, NEG_INF)

    m_near = {}
    acc_near = [jnp.zeros((LANES + SUM_ROWS, tq), F32) for _ in range(n_pairs)]
    for k, jn in enumerate(near_tiles):
        jc = jnp.maximum(jn, 0)
        causal = (jn * tk + key_i) <= qry_t
        if k > 0:
            causal = causal & (jn >= 0)
        probs, a_rows = tile_softmax(1 + n_win + k, picked_bias(jc, causal),
                                     lambda hh, qh: m_near.get((hh, qh), jnp.full((1, LANES), -jnp.inf, F32)),
                                     lambda hh, qh, v: m_near.__setitem__((hh, qh), v),
                                     keys=live_keys(tk - tq + k * tk, 0, t_len))
        vt = vs_ref[jc]
        acc_near = [acc_near[p] * a_rows[p] + _dot(vt, probs[p]) for p in range(n_pairs)]

    def m_put(hh, qh, v):
        m_ref[hh, :, qh * LANES:(qh + 1) * LANES] = v

    for (hh, qh), v in m_near.items():
        m_put(hh, qh, v)
    for p in range(n_pairs):
        acc_ref[p] = acc_near[p]
    n_far = jnp.maximum(last + 1 - n_near, 0)

    def slc_step(j, carry):
        s_next = scores(ks_ref, jnp.minimum(j + 1, n_far - 1))
        probs, a_rows = tile_softmax(0, picked_bias(j), lambda hh, qh: m_ref[hh, :, qh * LANES:(qh + 1) * LANES], m_put)
        vt = vs_ref[j]
        for p in range(n_pairs):
            pv = _dot(vt, probs[p])
            s_ref[0, p] = s_next[p]
            acc_ref[p] = acc_ref[p] * a_rows[p] + pv
        return carry

    lax.fori_loop(0, n_far, slc_step, 0)
    o_slc = [normalised(acc_ref[p]) for p in range(n_pairs)]

    gates = gt_ref[0]
    gain = gain_ref[...]
    for p in range(n_pairs):
        o = jnp.zeros((LANES, tq), F32)
        for c, branch in enumerate((o_cmp[p], o_slc[p], o_win[p])):
            r = c * HPG + 2 * p
            o = o + jnp.where(slab_lo, gates[r:r + 1, :], gates[r + 1:r + 2, :]) * branch
        sq = o * o
        ms_a = jnp.sum(sq[0:hd], axis=0, keepdims=True)
        ms_b = jnp.sum(sq[hd:2 * hd], axis=0, keepdims=True)
        ms = jnp.where(slab_lo, ms_a, ms_b) * (1.0 / hd)
        o = o * lax.rsqrt(ms + EPS)
        o_ref[0, p * LANES:(p + 1) * LANES, :] = (o * gain[p * LANES:(p + 1) * LANES, :]).astype(o_ref.dtype)


def _nsa_call(qt, ksw, vt, kc, vct, gt, gain, ovt):
    b, _, t = qt.shape
    tq, tk = ATT_TQ, ATT_TK
    n_kt = t // tk
    gw = HPG * NSA_HEAD_DIM
    hd = NSA_HEAD_DIM
    k_scratch = pltpu.VMEM((n_kt, 2 * tk, LANES), BF16)
    v_scratch = pltpu.VMEM((n_kt, LANES + SUM_ROWS, 2 * tk), BF16)
    return pl.pallas_call(
        _nsa_body,
        grid=(b, NSA_KV_HEADS, t // tq),
        in_specs=[
            pl.BlockSpec((1, gw, tq), lambda bi, gi, qi: (bi, gi, qi)),
            pl.BlockSpec((1, t, 2 * KV_WIDTH), lambda bi, gi, qi: (bi, 0, 0)),
            pl.BlockSpec((1, hd, t), lambda bi, gi, qi: (bi, gi, 0)),
            pl.BlockSpec((1, hd, t), lambda bi, gi, qi: (bi, NSA_KV_HEADS + gi, 0)),
            pl.BlockSpec((1, 1) + kc.shape[2:], lambda bi, gi, qi: (bi, gi, 0, 0)),
            pl.BlockSpec((1, 1) + vct.shape[2:], lambda bi, gi, qi: (bi, gi, 0, 0)),
            pl.BlockSpec((1, LANES, tq), lambda bi, gi, qi: (bi, gi, qi)),
            pl.BlockSpec((gw, 1), lambda bi, gi, qi: (gi, 0)),
            pl.BlockSpec(ovt.shape, lambda bi, gi, qi: (0, 0)),
        ],
        out_specs=pl.BlockSpec((1, gw, tq), lambda bi, gi, qi: (bi, gi, qi)),
        out_shape=jax.ShapeDtypeStruct((b, NSA_WIDTH, t), BF16),
        scratch_shapes=[k_scratch, k_scratch, v_scratch, v_scratch,
                        pltpu.VMEM((HPG, 1, tq), F32),
                        pltpu.VMEM((HPG // 2, LANES + SUM_ROWS, tq), F32), pltpu.VMEM((1 + (WINDOW + tq) // tk + SLC_NEAR_TILES, HPG // 2, 2 * tk, tq), F32),
                        pltpu.VMEM((t // SLC_BLOCK, 1, tq), F32)],
        compiler_params=pltpu.CompilerParams(dimension_semantics=("arbitrary", "arbitrary", "arbitrary"),
                                             vmem_limit_bytes=VMEM_LIMIT),
        name="nsa_attention",
    )(qt, ksw, vt, vt, kc, vct, gt, gain, ovt)


def _ffn_body(x_ref, oh_ref, on_ref, woh_ref, won_ref, g2_ref, wg_ref, wu_ref, wd_ref, cw_ref, gf_ref,
              out_ref, halo_ref, act_ref, *, tiles_per_seq):
    tm = x_ref.shape[0]
    x1 = x_ref[...] + _dot(oh_ref[...], woh_ref[...]) + _dot_tn(on_ref[0], won_ref[...])
    hb = _rms(x1, g2_ref[...]).astype(BF16)
    row = lax.broadcasted_iota(jnp.int32, (tm, FFN_TC), 0)

    @pl.when((pl.program_id(0) % tiles_per_seq) == 0)
    def _sequence_start():
        halo_ref[...] = jnp.zeros_like(halo_ref)

    def activation(c, gate, up):
        cols = slice(c * FFN_TC, (c + 1) * FFN_TC)
        halo = halo_ref[:, cols]
        halo_ref[:, cols] = gate[tm - SUBLANES:tm, :]
        last1 = halo[SUBLANES - 1:SUBLANES, :]
        last2 = halo[SUBLANES - 2:SUBLANES - 1, :]
        prev1 = jnp.where(row == 0, last1, pltpu.roll(gate, 1, 0))
        prev2 = jnp.where(row == 0, last2, jnp.where(row == 1, last1, pltpu.roll(gate, 2, 0)))
        cw = cw_ref[:, cols]
        y = cw[0:1, :] * prev2 + cw[1:2, :] * prev1 + cw[2:3, :] * gate + cw[3:4, :]
        return (jax.nn.silu(y) * up).astype(BF16)

    chunk = lambda w_ref, c: _dot(hb, w_ref[:, c * FFN_TC:(c + 1) * FFN_TC])
    gate_up = (chunk(wg_ref, 0), chunk(wu_ref, 0))
    for c in range(FFN_NC):
        cur = gate_up
        if c + 1 < FFN_NC:
            gate_up = (chunk(wg_ref, c + 1), chunk(wu_ref, c + 1))
        act_ref[:, c * FFN_TC:(c + 1) * FFN_TC] = activation(c, *cur)
    acc = _dot(act_ref[...], wd_ref[...])
    out_ref[...] = _rms(x1 + acc, gf_ref[...])


def _ffn_call(x2, oh, on, woh, won, g2, wg, wu, wd, cw, gf, tiles_per_seq):
    n = x2.shape[0]
    tm = FFN_TM
    row = lambda w: pl.BlockSpec((tm, w), lambda i: (i, 0))
    full = lambda a: pl.BlockSpec(a.shape, lambda i: (0,) * a.ndim, pipeline_mode=pl.Buffered(1))
    return pl.pallas_call(
        functools.partial(_ffn_body, tiles_per_seq=tiles_per_seq),
        grid=(n // tm,),
        in_specs=[row(D_MODEL), row(HG_WIDTH),
                  pl.BlockSpec((1, NSA_WIDTH, tm), lambda i: (i // tiles_per_seq, 0, i % tiles_per_seq)),
                  full(woh), full(won), full(g2),
                  full(wg), full(wu), full(wd), full(cw), full(gf)],
        out_specs=row(D_MODEL),
        out_shape=jax.ShapeDtypeStruct((n, D_MODEL), F32),
        scratch_shapes=[pltpu.VMEM((SUBLANES, D_FF), F32), pltpu.VMEM((tm, D_FF), BF16)],
        compiler_params=pltpu.CompilerParams(dimension_semantics=("arbitrary",),
                                             vmem_limit_bytes=VMEM_LIMIT),
        name="outproj_convffn",
    )(x2, oh, on, woh, won, g2, wg, wu, wd, cw, gf)


def _rope_angles(positions):
    inv_freq = ROPE_THETA ** (-jnp.arange(ROPE_HALF, dtype=F32) * 2.0 / ROPE_DIM)
    ang = positions.astype(F32)[..., None] * inv_freq
    return jnp.concatenate([jnp.cos(ang), jnp.sin(ang)], axis=-1).transpose(0, 2, 1)


def _layer(x, positions, ln1, w_in, lb, hg_gain, pe_k, pe_v, k_w1, k_w2, v_w1, v_w2, nsa_gain, w_o, ln2,
           w_gate, w_up, conv_w, conv_b, w_down, final_gain):
    b, t, d = x.shape
    n = b * t
    assert d == D_MODEL and t % FFN_TM == 0 and t % PROJ_TM == 0 and t % ATT_TQ == 0 and t % HG_TT == 0
    n_grp = t // CMP_STRIDE
    assert n_grp == LANES, "compressed-block axis is laid out on exactly one lane tile"
    n_sel = t // SLC_BLOCK
    assert n_sel % 8 == 0 and n_sel <= LANES and ATT_TK % SLC_BLOCK == 0 and ATT_TQ % ATT_TK == 0
    x2 = x.reshape(n, d)

    splits = np.cumsum([0, 4 * HG_WIDTH, NSA_WIDTH] + [KV_WIDTH] * 6 + [N_GATES])
    seg = lambda i: w_in[:, splits[i]:splits[i + 1]]
    wh = seg(0).astype(BF16)
    wk = jnp.concatenate([seg(2), seg(3), seg(4), seg(6)], axis=1).astype(BF16)
    wgate = seg(8).reshape(d, 3, NSA_KV_HEADS, HPG).transpose(0, 2, 1, 3).reshape(d, NSA_KV_HEADS, 3 * HPG)
    wgate = jnp.pad(wgate, ((0, 0), (0, 0), (0, LANES - 3 * HPG))).reshape(d, NSA_KV_HEADS * LANES)
    wt = jnp.concatenate([seg(1), seg(5), seg(7), wgate], axis=1).T.astype(BF16)
    cs = _rope_angles(positions)

    hg, kcn, vcn, ksw, qt, vt, gt = _inproj_call(x2, ln1.reshape(1, d), wh, wk, wt, cs, t // PROJ_TM)

    mst, lvl = _hgrn_tables()
    o_hg = _hgrn_call(hg.reshape(b, t, 4 * HG_WIDTH), lb.reshape(1, HG_WIDTH).astype(F32),
                      hg_gain.reshape(1, HG_WIDTH), mst, lvl)

    per_lane = lambda a: jnp.broadcast_to(a.reshape(2, CMP_STRIDE, 1, NSA_HEAD_DIM, -1),
                                          (2, CMP_STRIDE, NSA_KV_HEADS, NSA_HEAD_DIM, a.shape[-1]))
    w1_rows = lambda w1: per_lane(w1).reshape(2, CMP_STRIDE * LANES, CMP_HIDDEN).astype(BF16)
    pe_rows = lambda pe: per_lane(pe[..., None]).reshape(2, 1, CMP_STRIDE * LANES)
    zeros_w2 = jnp.zeros((CMP_HIDDEN, NSA_HEAD_DIM), F32)
    place = lambda w2: jnp.stack([jnp.concatenate([w2, zeros_w2], 1), jnp.concatenate([zeros_w2, w2], 1)])
    kc, vct = _cmp_call(kcn.reshape(b, t, KV_WIDTH), vcn.reshape(b, t, KV_WIDTH), pe_rows(pe_k), pe_rows(pe_v),
                        w1_rows(k_w1), w1_rows(v_w1),
                        place(k_w2).astype(BF16), place(v_w2).transpose(0, 2, 1).astype(BF16))

    cmp_start = np.arange(n_grp) * CMP_STRIDE
    cmp_end = cmp_start + CMP_BLOCK - 1
    sel_start = np.arange(LANES) * SLC_BLOCK
    overlap = ((cmp_start[:, None] <= sel_start[None, :] + SLC_BLOCK - 1) & (cmp_end[:, None] >= sel_start[None, :])
               & (np.arange(LANES)[None, :] < n_sel) & (np.arange(n_grp)[:, None] < n_grp - 1))
    ovt = jnp.asarray(overlap.T.astype(np.float32), BF16)
    o_nsa = _nsa_call(qt, ksw.reshape(b, t, 2 * KV_WIDTH), vt, kc, vct, gt, nsa_gain.reshape(NSA_WIDTH, 1), ovt)

    cw = jnp.concatenate([conv_w, conv_b[None, :], jnp.zeros((SUBLANES - CONV_WIDTH - 1, D_FF), F32)], axis=0)
    out = _ffn_call(x2, o_hg.reshape(n, HG_WIDTH), o_nsa,
                    w_o[:HG_WIDTH].astype(BF16), w_o[HG_WIDTH:].astype(BF16), ln2.reshape(1, d),
                    w_gate.astype(BF16), w_up.astype(BF16), w_down.astype(BF16), cw,
                    final_gain.reshape(1, d), t // FFN_TM)
    return out.reshape(b, t, d)


def kernel(x, positions, ln1_gain, w_in, hgrn_lb_param, hgrn_out_gain, cmp_pe_k, cmp_pe_v, cmp_k_w1, cmp_k_w2,
           cmp_v_w1, cmp_v_w2, nsa_out_gain, w_o, ln2_gain, ffn_w_gate, ffn_w_up, ffn_conv_w, ffn_conv_b,
           ffn_w_down, final_gain):
    depth = ln1_gain.shape[0]
    assert depth == 1, "the fused final norm assumes a single layer"
    lower_bounds = jnp.cumsum(jax.nn.softmax(hgrn_lb_param.astype(F32), axis=0), axis=0)
    l = 0
    return _layer(x, positions, ln1_gain[l], w_in[l], lower_bounds[l], hgrn_out_gain[l], cmp_pe_k[l], cmp_pe_v[l],
                  cmp_k_w1[l], cmp_k_w2[l], cmp_v_w1[l], cmp_v_w2[l], nsa_out_gain[l], w_o[l], ln2_gain[l],
                  ffn_w_gate[l], ffn_w_up[l], ffn_conv_w[l], ffn_conv_b[l], ffn_w_down[l], final_gain)
```

```python
import functools

import jax
import jax.numpy as jnp
import numpy as np
from jax import lax
from jax.experimental import pallas as pl
from jax.experimental.pallas import tpu as pltpu

F32 = jnp.float32
BF16 = jnp.bfloat16

D_MODEL = 1024
HG_HEADS = 4
HG_DK = 128
HG_DV = 128
HG_WIDTH = HG_HEADS * HG_DV
NSA_HEADS = 8
NSA_KV_HEADS = 2
NSA_HEAD_DIM = 64
HPG = NSA_HEADS // NSA_KV_HEADS
NSA_WIDTH = NSA_HEADS * NSA_HEAD_DIM
KV_WIDTH = NSA_KV_HEADS * NSA_HEAD_DIM
CMP_BLOCK = 32
CMP_STRIDE = 16
CMP_HIDDEN = 256
SLC_BLOCK = 64
SLC_TOPK = 16
WINDOW = 512
ROPE_THETA = 500000.0
ROPE_DIM = NSA_HEAD_DIM // 4
ROPE_HALF = ROPE_DIM // 2
D_FF = 2816
CONV_WIDTH = 3
EPS = 1e-6
NEG_INF = -1e30
FORCE_SCORE = 1e4
N_GATES = 3 * NSA_HEADS
LOG2_E = 1.4426950408889634

LANES = 128
SUBLANES = 8
VMEM_LIMIT = 56 * 1024 * 1024

PROJ_TM = 1024
HG_CHUNK = 128
HG_LEVELS = (16, 32, 64)
HG_DIAG = 16
HG_TT = 1024
ATT_TQ = 256
ATT_TK = 256
KEY_BLK = 128
SLC_NEAR_TILES = 1
SUM_ROWS = 16
FFN_TM = 512
FFN_TC = 256
FFN_NC = D_FF // FFN_TC


def _dot(a, b):
    return jnp.dot(a, b, preferred_element_type=F32)


def _dot_nt(a, b):
    return lax.dot_general(a, b, (((1,), (1,)), ((), ())), preferred_element_type=F32)


def _dot_tn(a, b):
    return lax.dot_general(a, b, (((0,), (0,)), ((), ())), preferred_element_type=F32)


def _split3(x):
    hi = x.astype(BF16)
    r = x - hi.astype(F32)
    mid = r.astype(BF16)
    lo = (r - mid.astype(F32)).astype(BF16)
    return hi, mid, lo


def _rms(x, gain):
    return x * lax.rsqrt(jnp.mean(x * x, axis=-1, keepdims=True) + EPS) * gain


def _inproj_body(x_ref, g_ref, wh_ref, wk_ref, wt_ref, cs_ref,
                 hg_ref, kcn_ref, vcn_ref, ksw_ref, qt_ref, vt_ref, gt_ref):
    hb = _rms(x_ref[...], g_ref[...]).astype(BF16)
    hg_ref[...] = _dot(hb, wh_ref[...])

    def rope(v, axis, cos, sin_hi, sin_lo):
        return (v * cos + pltpu.roll(v, ROPE_HALF, axis) * sin_hi
                + pltpu.roll(v, LANES - ROPE_HALF, axis) * sin_lo)

    cos = cs_ref[0, 0:ROPE_HALF, :]
    sin = cs_ref[0, ROPE_HALF:ROPE_DIM, :]
    tm = cos.shape[1]
    zero_h = jnp.zeros((ROPE_HALF, tm), F32)
    rest = NSA_HEAD_DIM - ROPE_DIM
    slab = lambda lo, hi, fill: jnp.concatenate([lo, hi, jnp.full((rest, tm), fill, F32)] * (LANES // NSA_HEAD_DIM), axis=0)
    tab_t = (slab(cos, cos, 1.0), slab(zero_h, sin, 0.0), slab(-sin, zero_h, 0.0))
    tab = tuple(a.T for a in tab_t)
    kn = _dot(hb, wk_ref[...])
    kcn_ref[...] = rope(kn[:, 0:LANES], 1, *tab)
    vcn_ref[...] = kn[:, LANES:2 * LANES]
    ksw_ref[:, 0:LANES] = rope(kn[:, 2 * LANES:3 * LANES], 1, *tab).astype(BF16)
    ksw_ref[:, LANES:2 * LANES] = rope(kn[:, 3 * LANES:4 * LANES], 1, *tab).astype(BF16)

    rt = _dot_nt(wt_ref[...], hb)
    scale = NSA_HEAD_DIM ** -0.5 * LOG2_E
    for j in range(NSA_WIDTH // LANES):
        sl = slice(j * LANES, (j + 1) * LANES)
        qt_ref[0, sl, :] = (rope(rt[sl], 0, *tab_t) * scale).astype(BF16)
    vt_ref[0] = rt[NSA_WIDTH:NSA_WIDTH + 2 * KV_WIDTH].astype(BF16)
    gt_ref[0] = jax.nn.sigmoid(rt[NSA_WIDTH + 2 * KV_WIDTH:])


def _inproj_call(x2, gain, wh, wk, wt, cs, tiles_per_seq):
    n = x2.shape[0]
    tm = PROJ_TM
    t = tiles_per_seq * tm
    b = n // t
    row = lambda w: pl.BlockSpec((tm, w), lambda i: (i, 0))
    col = lambda h: pl.BlockSpec((1, h, tm), lambda i: (i // tiles_per_seq, 0, i % tiles_per_seq))
    full = lambda a: pl.BlockSpec(a.shape, lambda i: (0, 0))
    gate_rows = NSA_KV_HEADS * LANES
    return pl.pallas_call(
        _inproj_body,
        grid=(n // tm,),
        in_specs=[row(D_MODEL), full(gain), full(wh), full(wk), full(wt),
                  col(ROPE_DIM)],
        out_specs=[row(4 * HG_WIDTH), row(KV_WIDTH), row(KV_WIDTH), row(2 * KV_WIDTH),
                   col(NSA_WIDTH), col(2 * KV_WIDTH), col(gate_rows)],
        out_shape=[jax.ShapeDtypeStruct((n, 4 * HG_WIDTH), F32),
                   jax.ShapeDtypeStruct((n, KV_WIDTH), F32),
                   jax.ShapeDtypeStruct((n, KV_WIDTH), F32),
                   jax.ShapeDtypeStruct((n, 2 * KV_WIDTH), BF16),
                   jax.ShapeDtypeStruct((b, NSA_WIDTH, t), BF16),
                   jax.ShapeDtypeStruct((b, 2 * KV_WIDTH, t), BF16),
                   jax.ShapeDtypeStruct((b, gate_rows, t), F32)],
        compiler_params=pltpu.CompilerParams(dimension_semantics=("arbitrary",),
                                             vmem_limit_bytes=VMEM_LIMIT),
        name="inproj",
    )(x2, gain, wh, wk, wt, cs)


def _hgrn_tables():
    L = HG_CHUNK
    t = np.arange(L)[:, None]
    u = np.arange(L)[None, :]
    level = np.where(((t // HG_DIAG) == (u // HG_DIAG)) & (u <= t), 1, 0)
    for li, s in enumerate(HG_LEVELS):
        same = (t // (2 * s)) == (u // (2 * s))
        right = (t % (2 * s)) >= s
        level = np.where(same & right & ((u % (2 * s)) < s), li + 2, level)
    return jnp.asarray((u <= t).astype(np.float32), BF16), jnp.asarray(level, jnp.int32)


def _hgrn_body(q_ref, f_ref, i_ref, g_ref, lb_ref, gain_ref, mst_ref, lvl_ref, o_ref, st_ref):
    L = HG_CHUNK
    n_chunks = q_ref.shape[1] // L

    @pl.when(pl.program_id(1) == 0)
    def _sequence_start():
        st_ref[...] = jnp.zeros_like(st_ref)

    def chunk(c, carry):
        rows = pl.ds(pl.multiple_of(c * L, L), L)
        heads = range(HG_HEADS)
        cols = [slice(h * HG_DK, (h + 1) * HG_DK) for h in heads]
        mst = mst_ref[...]
        lvl = lvl_ref[...]
        n_lv = len(HG_LEVELS)
        row_i = lax.broadcasted_iota(jnp.int32, (L, HG_DK), 0)
        q = [q_ref[0, rows, cols[h]] for h in heads]
        vb = [i_ref[0, rows, cols[h]].astype(BF16) for h in heads]
        f = [lb_ref[:, cols[h]] + (1.0 - lb_ref[:, cols[h]]) * jax.nn.sigmoid(f_ref[0, rows, cols[h]]) for h in heads]
        k = [1.0 - f[h] for h in heads]
        parts = [_split3(jnp.log2(f[h])) for h in heads]
        e_full = [(_dot(mst, parts[h][0]) + _dot(mst, parts[h][1])) + _dot(mst, parts[h][2]) for h in heads]
        b_last = [e_full[h][L - 1:L, :] for h in heads]

        def rel_to(b, blk, off):
            refs = []
            for r0 in range(0, L, blk):
                r = r0 + off - 1
                ref = b[r:r + 1, :] if r >= 0 else jnp.zeros((1, HG_DK), F32)
                refs.append(jnp.broadcast_to(ref, (blk, HG_DK)))
            return b - jnp.concatenate(refs, axis=0)

        def level_sums(b):
            out = [rel_to(b, HG_DIAG, 0)]
            for s_half in HG_LEVELS:
                d = rel_to(b, 2 * s_half, s_half)
                out.append(jnp.where((row_i % (2 * s_half)) >= s_half, d, -d))
            return out

        e = [level_sums(e_full[h]) for h in heads]
        wq = [[jnp.exp2(e[h][l]) for l in range(n_lv + 1)] for h in heads]
        wk = [[jnp.exp2(-e[h][0])] + wq[h][1:] for h in heads]
        prod = [[_dot_nt((q[h] * wq[h][l]).astype(BF16), (k[h] * wk[h][l]).astype(BF16)) for l in range(n_lv + 1)]
                for h in heads]
        st = [st_ref[h] for h in heads]
        inter = [_dot_nt((q[h] * jnp.exp2(e_full[h])).astype(BF16), st[h].astype(BF16)) for h in heads]
        k_dec = [(k[h] * jnp.exp2(b_last[h] - e_full[h])).astype(BF16) for h in heads]
        upd = [_dot_tn(vb[h], k_dec[h]) for h in heads]
        for h in heads:
            st_ref[h] = st[h] * jnp.exp2(b_last[h]) + upd[h]
        a = []
        for h in heads:
            ah = jnp.where(lvl == 1, prod[h][0], 0.0)
            for l in range(1, n_lv + 1):
                ah = jnp.where(lvl == l + 1, prod[h][l], ah)
            a.append(ah.astype(BF16))
        o = [_dot(a[h], vb[h]) + inter[h] for h in heads]
        for h in heads:
            oh = o[h] * lax.rsqrt(jnp.mean(o[h] * o[h], axis=-1, keepdims=True) + EPS) * gain_ref[:, cols[h]]
            o_ref[0, rows, cols[h]] = (oh * jax.nn.silu(g_ref[0, rows, cols[h]])).astype(o_ref.dtype)
        return carry

    lax.fori_loop(0, n_chunks, chunk, 0, unroll=True)


def _hgrn_call(hg, lb, gain, mst, lvl):
    b, t, _ = hg.shape
    tt = HG_TT
    col = lambda k: pl.BlockSpec((1, tt, HG_WIDTH), lambda bi, ti: (bi, ti, k))
    full = lambda a: pl.BlockSpec(a.shape, lambda bi, ti: (0, 0))
    return pl.pallas_call(
        _hgrn_body,
        grid=(b, t // tt),
        in_specs=[col(0), col(1), col(2), col(3), full(lb), full(gain), full(mst), full(lvl)],
        out_specs=pl.BlockSpec((1, tt, HG_WIDTH), lambda bi, ti: (bi, ti, 0)),
        out_shape=jax.ShapeDtypeStruct((b, t, HG_WIDTH), BF16),
        scratch_shapes=[pltpu.VMEM((HG_HEADS, HG_DV, HG_DK), F32)],
        compiler_params=pltpu.CompilerParams(dimension_semantics=("arbitrary", "arbitrary"),
                                             vmem_limit_bytes=VMEM_LIMIT),
        name="hgrn2",
    )(hg, hg, hg, hg, lb, gain, mst, lvl)


def _cmp_body(kcn_ref, vcn_ref, pek_ref, pev_ref, w1k_ref, w1v_ref, w2k_ref, w2v_ref, kc_ref, vc_ref):
    nb = kcn_ref.shape[1] // CMP_STRIDE
    lane_grp = (lax.broadcasted_iota(jnp.int32, (nb, CMP_STRIDE * LANES), 1) // NSA_HEAD_DIM) % NSA_KV_HEADS

    def hidden(src_ref, pe_ref, w1_ref):
        x = jnp.concatenate([src_ref[0, pl.ds(l, nb, stride=CMP_STRIDE), :]
                             for l in range(CMP_STRIDE)], axis=1)
        halves = [x + pe_ref[i] for i in range(2)]
        out = []
        for g in range(NSA_KV_HEADS):
            u, v = (_dot(jnp.where(lane_grp == g, halves[i], 0.0).astype(BF16), w1_ref[i]) for i in range(2))
            out.append(jax.nn.silu(u + pltpu.roll(v, nb - 1, 0)).astype(BF16))
        return out

    hk = hidden(kcn_ref, pek_ref, w1k_ref)
    hv = hidden(vcn_ref, pev_ref, w1v_ref)
    for g in range(NSA_KV_HEADS):
        kc_ref[0, g, 0:nb, :] = _dot(hk[g], w2k_ref[0]).astype(kc_ref.dtype)
        kc_ref[0, g, nb:2 * nb, :] = _dot(hk[g], w2k_ref[1]).astype(kc_ref.dtype)
        vc_ref[0, g, :, 0:nb] = _dot_nt(w2v_ref[0], hv[g]).astype(vc_ref.dtype)
        vc_ref[0, g, :, nb:2 * nb] = _dot_nt(w2v_ref[1], hv[g]).astype(vc_ref.dtype)


def _cmp_call(kcn, vcn, pek, pev, w1k, w1v, w2k, w2v):
    b, t, w = kcn.shape
    nb = t // CMP_STRIDE
    full = lambda a: pl.BlockSpec(a.shape, lambda bi: (0,) * a.ndim)
    out = lambda r, c: pl.BlockSpec((1, NSA_KV_HEADS, r, c), lambda bi: (bi, 0, 0, 0))
    return pl.pallas_call(
        _cmp_body,
        grid=(b,),
        in_specs=[pl.BlockSpec((1, t, w), lambda bi: (bi, 0, 0)), pl.BlockSpec((1, t, w), lambda bi: (bi, 0, 0)),
                  full(pek), full(pev), full(w1k), full(w1v), full(w2k), full(w2v)],
        out_specs=[out(2 * nb, LANES), out(LANES, 2 * nb)],
        out_shape=[jax.ShapeDtypeStruct((b, NSA_KV_HEADS, 2 * nb, LANES), BF16),
                   jax.ShapeDtypeStruct((b, NSA_KV_HEADS, LANES, 2 * nb), BF16)],
        compiler_params=pltpu.CompilerParams(dimension_semantics=("arbitrary",),
                                             vmem_limit_bytes=VMEM_LIMIT),
        name="nsa_compress",
    )(kcn, vcn, pek, pev, w1k, w1v, w2k, w2v)


def _nsa_body(qt_ref, ksw_ref, vst_ref, vwt_ref, kc_ref, vct_ref, gt_ref, gain_ref, ovt_ref, o_ref,
              ks_ref, kw_ref, vs_ref, vw_ref, m_ref, acc_ref, s_ref, ch_ref):
    g = pl.program_id(1)
    qi = pl.program_id(2)
    tq = ATT_TQ
    tk = ATT_TK
    t_len = ksw_ref.shape[1]
    n_kt = t_len // tk
    n_pairs = HPG // 2
    hd = NSA_HEAD_DIM

    @pl.when(qi == 0)
    def _build_kv():
        lane = lax.broadcasted_iota(jnp.int32, (tk, LANES), 1)
        lo_lane = lane < hd
        keep = (lane // hd) == g

        def build_k(src_col, dst_ref):
            def body(j, carry):
                rows = pl.ds(pl.multiple_of(j * tk, tk), tk)
                x = ksw_ref[0, rows, src_col * LANES:(src_col + 1) * LANES].astype(F32)
                dup = jnp.where(keep, x, pltpu.roll(x, hd, 1))
                dst_ref[j, 0:tk, :] = jnp.where(lo_lane, dup, 0.0).astype(BF16)
                dst_ref[j, tk:2 * tk, :] = jnp.where(lo_lane, 0.0, dup).astype(BF16)
                return carry
            lax.fori_loop(0, n_kt, body, 0)

        def build_v(src_ref, dst_ref):
            zero = jnp.zeros((hd, tk), BF16)
            row = lax.broadcasted_iota(jnp.int32, (SUM_ROWS, 2 * tk), 0)
            col = lax.broadcasted_iota(jnp.int32, (SUM_ROWS, 2 * tk), 1)
            ones_rows = jnp.where(((row == 0) & (col < tk)) | ((row == 1) & (col >= tk)), 1.0, 0.0).astype(BF16)
            for j in range(n_kt):
                x = src_ref[0, :, j * tk:(j + 1) * tk]
                dst_ref[j, 0:hd, 0:tk] = x
                dst_ref[j, 0:hd, tk:2 * tk] = zero
                dst_ref[j, hd:2 * hd, 0:tk] = zero
                dst_ref[j, hd:2 * hd, tk:2 * tk] = x
                dst_ref[j, 2 * hd:2 * hd + SUM_ROWS, :] = ones_rows

        build_k(0, ks_ref)
        build_k(1, kw_ref)
        build_v(vst_ref, vs_ref)
        build_v(vwt_ref, vw_ref)

    t0 = qi * tq
    key_i = lax.broadcasted_iota(jnp.int32, (tk, tq), 0)
    qry_t = t0 + lax.broadcasted_iota(jnp.int32, (tk, tq), 1)
    slab_lo = lax.broadcasted_iota(jnp.int32, (LANES, tq), 0) < hd
    acc_row = lax.broadcasted_iota(jnp.int32, (LANES + SUM_ROWS, tq), 0)
    slab_a = (acc_row < hd) | (acc_row == LANES)
    q_pairs = [qt_ref[0, p * LANES:(p + 1) * LANES, :] for p in range(n_pairs)]

    last = (t0 + tq - 1) // tk

    def scores(k_ref, j):
        kt = k_ref[j]
        return [_dot(kt, q_pairs[p]) for p in range(n_pairs)]

    n_cmp_pad = kc_ref.shape[2] // 2
    blk_i = lax.broadcasted_iota(jnp.int32, (n_cmp_pad, tq), 0)
    blk_t = t0 + lax.broadcasted_iota(jnp.int32, (n_cmp_pad, tq), 1)
    cmp_ok = (blk_i * CMP_STRIDE + (CMP_BLOCK - 1)) <= blk_t
    kc = kc_ref[0, 0]
    vct = vct_ref[0, 0]
    s_cmp = [_dot(kc, q_pairs[p]) for p in range(n_pairs)]
    n_win = (WINDOW + tq) // tk
    win_tiles = [last - (n_win - 1) + k for k in range(n_win)]
    n_near = SLC_NEAR_TILES
    near_tiles = [last - k for k in range(n_near)]
    up_front = ([(ks_ref, 0)] + [(kw_ref, jnp.maximum(jw, 0)) for jw in win_tiles]
                + [(ks_ref, jnp.maximum(jn, 0)) for jn in near_tiles])
    for slot, (k_ref, j0) in enumerate(up_front):
        s_first = scores(k_ref, j0)
        for p in range(n_pairs):
            s_ref[slot, p] = s_first[p]
    p_sum = jnp.zeros((n_cmp_pad, tq), F32)
    p_cmp = []
    for p in range(n_pairs):
        probs = []
        for h in range(2):
            sh = jnp.where(cmp_ok, s_cmp[p][h * n_cmp_pad:(h + 1) * n_cmp_pad], NEG_INF)
            mh = jnp.max(sh, axis=0, keepdims=True)
            eh = jnp.where(cmp_ok, jnp.exp2(sh - mh), 0.0)
            den = jnp.sum(eh, axis=0, keepdims=True)
            ph = eh / jnp.where(den > 0.0, den, 1.0)
            p_sum = p_sum + ph
            probs.append(ph.astype(BF16))
        p_cmp.append(jnp.concatenate(probs, axis=0))
    o_cmp = [_dot(vct, p_cmp[p]) for p in range(n_pairs)]

    n_sel = t_len // SLC_BLOCK
    hi, mid, lo = _split3(p_sum)
    ovt = ovt_ref[...]
    p_sel = ((_dot(ovt, hi) + _dot(ovt, mid)) + _dot(ovt, lo))[0:n_sel]
    sel_i = lax.broadcasted_iota(jnp.int32, (n_sel, tq), 0)
    cur = (t0 + lax.broadcasted_iota(jnp.int32, (n_sel, tq), 1)) // SLC_BLOCK
    forced = (sel_i == 0) | (sel_i == cur) | (sel_i == cur - 1)
    score = jnp.where(forced, FORCE_SCORE, p_sel)
    score = jnp.where(sel_i <= cur, score, -jnp.inf)
    rank = jnp.zeros((n_sel, tq), jnp.int32)
    row_grp = SUBLANES
    grp_i = lax.broadcasted_iota(jnp.int32, (row_grp, tq), 0)
    for i in range(n_sel):
        ci = score[i:i + 1, :]
        ahead = []
        for r0 in range(0, n_sel, row_grp):
            rows = slice(r0, r0 + row_grp)
            if r0 > i:
                ahead.append(ci >= score[rows])
            elif r0 + row_grp <= i:
                ahead.append(ci > score[rows])
            else:
                ahead.append((ci > score[rows]) | ((ci == score[rows]) & (grp_i > i - r0)))
        rank = rank + jnp.where(jnp.concatenate(ahead, axis=0), 1, 0)
    chosen = jnp.where(rank < min(SLC_TOPK, n_sel), 1.0, 0.0)
    for i in range(n_sel):
        ch_ref[i] = chosen[i:i + 1, :]

    def live_keys(rel_at_origin, lower, upper):
        out = []
        for qh in range(tq // LANES):
            live = [kb for kb in range(tk // KEY_BLK)
                    if rel_at_origin + qh * LANES + LANES - 1 - kb * KEY_BLK >= lower
                    and rel_at_origin + qh * LANES - (kb * KEY_BLK + KEY_BLK - 1) < upper]
            out.append((min(live) * KEY_BLK, (max(live) + 1) * KEY_BLK))
        return out

    def tile_softmax(slot, bias, m_get, m_put, keys=None):
        alphas = {}
        probs = {}
        for hh in range(HPG):
            p, h = divmod(hh, 2)
            a_parts = []
            p_parts = []
            for qh in range(tq // LANES):
                ql = slice(qh * LANES, (qh + 1) * LANES)
                k_lo, k_hi = keys[qh] if keys is not None else (0, tk)
                sh = (s_ref[slot, p, h * tk + k_lo:h * tk + k_hi, ql]
                      + (bias if bias.shape == (1, 1) else bias[k_lo:k_hi, ql]))
                m_prev = m_get(hh, qh)
                m_new = jnp.maximum(m_prev, jnp.max(sh, axis=0, keepdims=True))
                m_put(hh, qh, m_new)
                piece = [jnp.zeros((k_lo, LANES), BF16)] if k_lo else []
                piece.append(jnp.exp2(sh - m_new).astype(BF16))
                if k_hi < tk:
                    piece.append(jnp.zeros((tk - k_hi, LANES), BF16))
                p_parts.append(jnp.concatenate(piece, axis=0) if len(piece) > 1 else piece[0])
                a_parts.append(jnp.exp2(m_prev - m_new))
            alphas[hh] = jnp.concatenate(a_parts, axis=1)
            probs[hh] = jnp.concatenate(p_parts, axis=1)
        return ([jnp.concatenate([probs[2 * p], probs[2 * p + 1]], axis=0) for p in range(n_pairs)],
                [jnp.where(slab_a, alphas[2 * p], alphas[2 * p + 1]) for p in range(n_pairs)])

    def normalised(acc):
        inv = jnp.where(slab_lo, 1.0 / acc[LANES:LANES + 1, :], 1.0 / acc[LANES + 1:LANES + 2, :])
        return acc[0:LANES] * inv

    m_win = {}
    acc_win = [jnp.zeros((LANES + SUM_ROWS, tq), F32) for _ in range(n_pairs)]
    for k, jw in enumerate(win_tiles):
        rel_hi = (n_win - k) * tk - 1
        rel_lo = rel_hi - (tq - 1) - (tk - 1)
        exists = jnp.where(jw >= 0, 0.0, NEG_INF).astype(F32).reshape(1, 1)
        if rel_lo >= 0 and rel_hi < WINDOW:
            bias = exists
        else:
            rel = qry_t - (jw * tk + key_i)
            inside = (rel < WINDOW) if rel_lo >= 0 else (rel >= 0) if rel_hi < WINDOW else (rel >= 0) & (rel < WINDOW)
            bias = jnp.where(inside, exists, NEG_INF)
        probs, a_rows = tile_softmax(1 + k, bias,
                                     lambda hh, qh: m_win.get((hh, qh), jnp.full((1, LANES), -jnp.inf, F32)),
                                     lambda hh, qh, v: m_win.__setitem__((hh, qh), v),
                                     keys=live_keys(rel_lo + tk - 1, 0, WINDOW))
        vt = vw_ref[jnp.maximum(jw, 0)]
        acc_win = [acc_win[p] * a_rows[p] + _dot(vt, probs[p]) for p in range(n_pairs)]
    o_win = [normalised(acc_win[p]) for p in range(n_pairs)]

    def picked_bias(j, also=None):
        per_tile = tk // SLC_BLOCK
        picked = jnp.concatenate([jnp.broadcast_to(ch_ref[j * per_tile + i], (SLC_BLOCK, tq)) for i in range(per_tile)],
                                 axis=0) > 0.5
        return jnp.where(picked if also is None else picked & also, 0.0, NEG_INF)

    m_near = {}
    acc_near = [jnp.zeros((LANES + SUM_ROWS, tq), F32) for _ in range(n_pairs)]
    for k, jn in enumerate(near_tiles):
        jc = jnp.maximum(jn, 0)
        causal = (jn * tk + key_i) <= qry_t
        if k > 0:
            causal = causal & (jn >= 0)
        probs, a_rows = tile_softmax(1 + n_win + k, picked_bias(jc, causal),
                                     lambda hh, qh: m_near.get((hh, qh), jnp.full((1, LANES), -jnp.inf, F32)),
                                     lambda hh, qh, v: m_near.__setitem__((hh, qh), v),
                                     keys=live_keys(tk - tq + k * tk, 0, t_len))
        vt = vs_ref[jc]
        acc_near = [acc_near[p] * a_rows[p] + _dot(vt, probs[p]) for p in range(n_pairs)]

    def m_put(hh, qh, v):
        m_ref[hh, :, qh * LANES:(qh + 1) * LANES] = v

    for (hh, qh), v in m_near.items():
        m_put(hh, qh, v)
    for p in range(n_pairs):
        acc_ref[p] = acc_near[p]
    n_far = jnp.maximum(last + 1 - n_near, 0)

    def slc_step(j, carry):
        s_next = scores(ks_ref, jnp.minimum(j + 1, n_far - 1))
        probs, a_rows = tile_softmax(0, picked_bias(j), lambda hh, qh: m_ref[hh, :, qh * LANES:(qh + 1) * LANES], m_put)
        vt = vs_ref[j]
        for p in range(n_pairs):
            pv = _dot(vt, probs[p])
            s_ref[0, p] = s_next[p]
            acc_ref[p] = acc_ref[p] * a_rows[p] + pv
        return carry

    lax.fori_loop(0, n_far, slc_step, 0)
    o_slc = [normalised(acc_ref[p]) for p in range(n_pairs)]

    gates = gt_ref[0]
    gain = gain_ref[...]
    for p in range(n_pairs):
        o = jnp.zeros((LANES, tq), F32)
        for c, branch in enumerate((o_cmp[p], o_slc[p], o_win[p])):
            r = c * HPG + 2 * p
            o = o + jnp.where(slab_lo, gates[r:r + 1, :], gates[r + 1:r + 2, :]) * branch
        sq = o * o
        ms_a = jnp.sum(sq[0:hd], axis=0, keepdims=True)
        ms_b = jnp.sum(sq[hd:2 * hd], axis=0, keepdims=True)
        ms = jnp.where(slab_lo, ms_a, ms_b) * (1.0 / hd)
        o = o * lax.rsqrt(ms + EPS)
        o_ref[0, p * LANES:(p + 1) * LANES, :] = (o * gain[p * LANES:(p + 1) * LANES, :]).astype(o_ref.dtype)


def _nsa_call(qt, ksw, vt, kc, vct, gt, gain, ovt):
    b, _, t = qt.shape
    tq, tk = ATT_TQ, ATT_TK
    n_kt = t // tk
    gw = HPG * NSA_HEAD_DIM
    hd = NSA_HEAD_DIM
    k_scratch = pltpu.VMEM((n_kt, 2 * tk, LANES), BF16)
    v_scratch = pltpu.VMEM((n_kt, LANES + SUM_ROWS, 2 * tk), BF16)
    return pl.pallas_call(
        _nsa_body,
        grid=(b, NSA_KV_HEADS, t // tq),
        in_specs=[
            pl.BlockSpec((1, gw, tq), lambda bi, gi, qi: (bi, gi, qi)),
            pl.BlockSpec((1, t, 2 * KV_WIDTH), lambda bi, gi, qi: (bi, 0, 0)),
            pl.BlockSpec((1, hd, t), lambda bi, gi, qi: (bi, gi, 0)),
            pl.BlockSpec((1, hd, t), lambda bi, gi, qi: (bi, NSA_KV_HEADS + gi, 0)),
            pl.BlockSpec((1, 1) + kc.shape[2:], lambda bi, gi, qi: (bi, gi, 0, 0)),
            pl.BlockSpec((1, 1) + vct.shape[2:], lambda bi, gi, qi: (bi, gi, 0, 0)),
            pl.BlockSpec((1, LANES, tq), lambda bi, gi, qi: (bi, gi, qi)),
            pl.BlockSpec((gw, 1), lambda bi, gi, qi: (gi, 0)),
            pl.BlockSpec(ovt.shape, lambda bi, gi, qi: (0, 0)),
        ],
        out_specs=pl.BlockSpec((1, gw, tq), lambda bi, gi, qi: (bi, gi, qi)),
        out_shape=jax.ShapeDtypeStruct((b, NSA_WIDTH, t), BF16),
        scratch_shapes=[k_scratch, k_scratch, v_scratch, v_scratch,
                        pltpu.VMEM((HPG, 1, tq), F32),
                        pltpu.VMEM((HPG // 2, LANES + SUM_ROWS, tq), F32), pltpu.VMEM((1 + (WINDOW + tq) // tk + SLC_NEAR_TILES, HPG // 2, 2 * tk, tq), F32),
                        pltpu.VMEM((t // SLC_BLOCK, 1, tq), F32)],
        compiler_params=pltpu.CompilerParams(dimension_semantics=("arbitrary", "arbitrary", "arbitrary"),
                                             vmem_limit_bytes=VMEM_LIMIT),
        name="nsa_attention",
    )(qt, ksw, vt, vt, kc, vct, gt, gain, ovt)


def _ffn_body(x_ref, oh_ref, on_ref, woh_ref, won_ref, g2_ref, wg_ref, wu_ref, wd_ref, cw_ref, gf_ref,
              out_ref, halo_ref, act_ref, *, tiles_per_seq):
    tm = x_ref.shape[0]
    x1 = x_ref[...] + _dot(oh_ref[...], woh_ref[...]) + _dot_tn(on_ref[0], won_ref[...])
    hb = _rms(x1, g2_ref[...]).astype(BF16)
    row = lax.broadcasted_iota(jnp.int32, (tm, FFN_TC), 0)

    @pl.when((pl.program_id(0) % tiles_per_seq) == 0)
    def _sequence_start():
        halo_ref[...] = jnp.zeros_like(halo_ref)

    def activation(c, gate, up):
        cols = slice(c * FFN_TC, (c + 1) * FFN_TC)
        halo = halo_ref[:, cols]
        halo_ref[:, cols] = gate[tm - SUBLANES:tm, :]
        last1 = halo[SUBLANES - 1:SUBLANES, :]
        last2 = halo[SUBLANES - 2:SUBLANES - 1, :]
        prev1 = jnp.where(row == 0, last1, pltpu.roll(gate, 1, 0))
        prev2 = jnp.where(row == 0, last2, jnp.where(row == 1, last1, pltpu.roll(gate, 2, 0)))
        cw = cw_ref[:, cols]
        y = cw[0:1, :] * prev2 + cw[1:2, :] * prev1 + cw[2:3, :] * gate + cw[3:4, :]
        return (jax.nn.silu(y) * up).astype(BF16)

    chunk = lambda w_ref, c: _dot(hb, w_ref[:, c * FFN_TC:(c + 1) * FFN_TC])
    gate_up = (chunk(wg_ref, 0), chunk(wu_ref, 0))
    for c in range(FFN_NC):
        cur = gate_up
        if c + 1 < FFN_NC:
            gate_up = (chunk(wg_ref, c + 1), chunk(wu_ref, c + 1))
        act_ref[:, c * FFN_TC:(c + 1) * FFN_TC] = activation(c, *cur)
    acc = _dot(act_ref[...], wd_ref[...])
    out_ref[...] = _rms(x1 + acc, gf_ref[...])


def _ffn_call(x2, oh, on, woh, won, g2, wg, wu, wd, cw, gf, tiles_per_seq):
    n = x2.shape[0]
    tm = FFN_TM
    row = lambda w: pl.BlockSpec((tm, w), lambda i: (i, 0))
    full = lambda a: pl.BlockSpec(a.shape, lambda i: (0,) * a.ndim, pipeline_mode=pl.Buffered(1))
    return pl.pallas_call(
        functools.partial(_ffn_body, tiles_per_seq=tiles_per_seq),
        grid=(n // tm,),
        in_specs=[row(D_MODEL), row(HG_WIDTH),
                  pl.BlockSpec((1, NSA_WIDTH, tm), lambda i: (i // tiles_per_seq, 0, i % tiles_per_seq)),
                  full(woh), full(won), full(g2),
                  full(wg), full(wu), full(wd), full(cw), full(gf)],
        out_specs=row(D_MODEL),
        out_shape=jax.ShapeDtypeStruct((n, D_MODEL), F32),
        scratch_shapes=[pltpu.VMEM((SUBLANES, D_FF), F32), pltpu.VMEM((tm, D_FF), BF16)],
        compiler_params=pltpu.CompilerParams(dimension_semantics=("arbitrary",),
                                             vmem_limit_bytes=VMEM_LIMIT),
        name="outproj_convffn",
    )(x2, oh, on, woh, won, g2, wg, wu, wd, cw, gf)


def _rope_angles(positions):
    inv_freq = ROPE_THETA ** (-jnp.arange(ROPE_HALF, dtype=F32) * 2.0 / ROPE_DIM)
    ang = positions.astype(F32)[..., None] * inv_freq
    return jnp.concatenate([jnp.cos(ang), jnp.sin(ang)], axis=-1).transpose(0, 2, 1)


def _layer(x, positions, ln1, w_in, lb, hg_gain, pe_k, pe_v, k_w1, k_w2, v_w1, v_w2, nsa_gain, w_o, ln2,
           w_gate, w_up, conv_w, conv_b, w_down, final_gain):
    b, t, d = x.shape
    n = b * t
    assert d == D_MODEL and t % FFN_TM == 0 and t % PROJ_TM == 0 and t % ATT_TQ == 0 and t % HG_TT == 0
    n_grp = t // CMP_STRIDE
    assert n_grp == LANES, "compressed-block axis is laid out on exactly one lane tile"
    n_sel = t // SLC_BLOCK
    assert n_sel % 8 == 0 and n_sel <= LANES and ATT_TK % SLC_BLOCK == 0 and ATT_TQ % ATT_TK == 0
    x2 = x.reshape(n, d)

    splits = np.cumsum([0, 4 * HG_WIDTH, NSA_WIDTH] + [KV_WIDTH] * 6 + [N_GATES])
    seg = lambda i: w_in[:, splits[i]:splits[i + 1]]
    wh = seg(0).astype(BF16)
    wk = jnp.concatenate([seg(2), seg(3), seg(4), seg(6)], axis=1).astype(BF16)
    wgate = seg(8).reshape(d, 3, NSA_KV_HEADS, HPG).transpose(0, 2, 1, 3).reshape(d, NSA_KV_HEADS, 3 * HPG)
    wgate = jnp.pad(wgate, ((0, 0), (0, 0), (0, LANES - 3 * HPG))).reshape(d, NSA_KV_HEADS * LANES)
    wt = jnp.concatenate([seg(1), seg(5), seg(7), wgate], axis=1).T.astype(BF16)
    cs = _rope_angles(positions)

    hg, kcn, vcn, ksw, qt, vt, gt = _inproj_call(x2, ln1.reshape(1, d), wh, wk, wt, cs, t // PROJ_TM)

    mst, lvl = _hgrn_tables()
    o_hg = _hgrn_call(hg.reshape(b, t, 4 * HG_WIDTH), lb.reshape(1, HG_WIDTH).astype(F32),
                      hg_gain.reshape(1, HG_WIDTH), mst, lvl)

    per_lane = lambda a: jnp.broadcast_to(a.reshape(2, CMP_STRIDE, 1, NSA_HEAD_DIM, -1),
                                          (2, CMP_STRIDE, NSA_KV_HEADS, NSA_HEAD_DIM, a.shape[-1]))
    w1_rows = lambda w1: per_lane(w1).reshape(2, CMP_STRIDE * LANES, CMP_HIDDEN).astype(BF16)
    pe_rows = lambda pe: per_lane(pe[..., None]).reshape(2, 1, CMP_STRIDE * LANES)
    zeros_w2 = jnp.zeros((CMP_HIDDEN, NSA_HEAD_DIM), F32)
    place = lambda w2: jnp.stack([jnp.concatenate([w2, zeros_w2], 1), jnp.concatenate([zeros_w2, w2], 1)])
    kc, vct = _cmp_call(kcn.reshape(b, t, KV_WIDTH), vcn.reshape(b, t, KV_WIDTH), pe_rows(pe_k), pe_rows(pe_v),
                        w1_rows(k_w1), w1_rows(v_w1),
                        place(k_w2).astype(BF16), place(v_w2).transpose(0, 2, 1).astype(BF16))

    cmp_start = np.arange(n_grp) * CMP_STRIDE
    cmp_end = cmp_start + CMP_BLOCK - 1
    sel_start = np.arange(LANES) * SLC_BLOCK
    overlap = ((cmp_start[:, None] <= sel_start[None, :] + SLC_BLOCK - 1) & (cmp_end[:, None] >= sel_start[None, :])
               & (np.arange(LANES)[None, :] < n_sel) & (np.arange(n_grp)[:, None] < n_grp - 1))
    ovt = jnp.asarray(overlap.T.astype(np.float32), BF16)
    o_nsa = _nsa_call(qt, ksw.reshape(b, t, 2 * KV_WIDTH), vt, kc, vct, gt, nsa_gain.reshape(NSA_WIDTH, 1), ovt)

    cw = jnp.concatenate([conv_w, conv_b[None, :], jnp.zeros((SUBLANES - CONV_WIDTH - 1, D_FF), F32)], axis=0)
    out = _ffn_call(x2, o_hg.reshape(n, HG_WIDTH), o_nsa,
                    w_o[:HG_WIDTH].astype(BF16), w_o[HG_WIDTH:].astype(BF16), ln2.reshape(1, d),
                    w_gate.astype(BF16), w_up.astype(BF16), w_down.astype(BF16), cw,
                    final_gain.reshape(1, d), t // FFN_TM)
    return out.reshape(b, t, d)


def kernel(x, positions, ln1_gain, w_in, hgrn_lb_param, hgrn_out_gain, cmp_pe_k, cmp_pe_v, cmp_k_w1, cmp_k_w2,
           cmp_v_w1, cmp_v_w2, nsa_out_gain, w_o, ln2_gain, ffn_w_gate, ffn_w_up, ffn_conv_w, ffn_conv_b,
           ffn_w_down, final_gain):
    depth = ln1_gain.shape[0]
    assert depth == 1, "the fused final norm assumes a single layer"
    lower_bounds = jnp.cumsum(jax.nn.softmax(hgrn_lb_param.astype(F32), axis=0), axis=0)
    l = 0
    return _layer(x, positions, ln1_gain[l], w_in[l], lower_bounds[l], hgrn_out_gain[l], cmp_pe_k[l], cmp_pe_v[l],
                  cmp_k_w1[l], cmp_k_w2[l], cmp_v_w1[l], cmp_v_w2[l], nsa_out_gain[l], w_o[l], ln2_gain[l],
                  ffn_w_gate[l], ffn_w_up[l], ffn_conv_w[l], ffn_conv_b[l], ffn_w_down[l], final_gain)
```

```python
import functools

import jax
import jax.numpy as jnp
import numpy as np
from jax import lax
from jax.experimental import pallas as pl
from jax.experimental.pallas import tpu as pltpu

F32 = jnp.float32
BF16 = jnp.bfloat16

D_MODEL = 1024
HG_HEADS = 4
HG_DK = 128
HG_DV = 128
HG_WIDTH = HG_HEADS * HG_DV
NSA_HEADS = 8
NSA_KV_HEADS = 2
NSA_HEAD_DIM = 64
HPG = NSA_HEADS // NSA_KV_HEADS
NSA_WIDTH = NSA_HEADS * NSA_HEAD_DIM
KV_WIDTH = NSA_KV_HEADS * NSA_HEAD_DIM
CMP_BLOCK = 32
CMP_STRIDE = 16
CMP_HIDDEN = 256
SLC_BLOCK = 64
SLC_TOPK = 16
WINDOW = 512
ROPE_THETA = 500000.0
ROPE_DIM = NSA_HEAD_DIM // 4
ROPE_HALF = ROPE_DIM // 2
D_FF = 2816
CONV_WIDTH = 3
EPS = 1e-6
NEG_INF = -1e30
FORCE_SCORE = 1e4
N_GATES = 3 * NSA_HEADS
LOG2_E = 1.4426950408889634

LANES = 128
SUBLANES = 8
VMEM_LIMIT = 56 * 1024 * 1024

PROJ_TM = 1024
HG_CHUNK = 128
HG_LEVELS = (16, 32, 64)
HG_DIAG = 16
HG_TT = 1024
ATT_TQ = 256
ATT_TK = 256
KEY_BLK = 128
SLC_NEAR_TILES = 1
SUM_ROWS = 16
FFN_TM = 512
FFN_TC = 256
FFN_NC = D_FF // FFN_TC


def _dot(a, b):
    return jnp.dot(a, b, preferred_element_type=F32)


def _dot_nt(a, b):
    return lax.dot_general(a, b, (((1,), (1,)), ((), ())), preferred_element_type=F32)


def _dot_tn(a, b):
    return lax.dot_general(a, b, (((0,), (0,)), ((), ())), preferred_element_type=F32)


def _split3(x):
    hi = x.astype(BF16)
    r = x - hi.astype(F32)
    mid = r.astype(BF16)
    lo = (r - mid.astype(F32)).astype(BF16)
    return hi, mid, lo


def _rms(x, gain):
    return x * lax.rsqrt(jnp.mean(x * x, axis=-1, keepdims=True) + EPS) * gain


def _inproj_body(x_ref, g_ref, wh_ref, wk_ref, wt_ref, cs_ref,
                 hg_ref, kcn_ref, vcn_ref, ksw_ref, qt_ref, vt_ref, gt_ref):
    hb = _rms(x_ref[...], g_ref[...]).astype(BF16)
    hg_ref[...] = _dot(hb, wh_ref[...])

    def rope(v, axis, cos, sin_hi, sin_lo):
        return (v * cos + pltpu.roll(v, ROPE_HALF, axis) * sin_hi
                + pltpu.roll(v, LANES - ROPE_HALF, axis) * sin_lo)

    cos = cs_ref[0, 0:ROPE_HALF, :]
    sin = cs_ref[0, ROPE_HALF:ROPE_DIM, :]
    tm = cos.shape[1]
    zero_h = jnp.zeros((ROPE_HALF, tm), F32)
    rest = NSA_HEAD_DIM - ROPE_DIM
    slab = lambda lo, hi, fill: jnp.concatenate([lo, hi, jnp.full((rest, tm), fill, F32)] * (LANES // NSA_HEAD_DIM), axis=0)
    tab_t = (slab(cos, cos, 1.0), slab(zero_h, sin, 0.0), slab(-sin, zero_h, 0.0))
    tab = tuple(a.T for a in tab_t)
    kn = _dot(hb, wk_ref[...])
    kcn_ref[...] = rope(kn[:, 0:LANES], 1, *tab)
    vcn_ref[...] = kn[:, LANES:2 * LANES]
    ksw_ref[:, 0:LANES] = rope(kn[:, 2 * LANES:3 * LANES], 1, *tab).astype(BF16)
    ksw_ref[:, LANES:2 * LANES] = rope(kn[:, 3 * LANES:4 * LANES], 1, *tab).astype(BF16)

    rt = _dot_nt(wt_ref[...], hb)
    scale = NSA_HEAD_DIM ** -0.5 * LOG2_E
    for j in range(NSA_WIDTH // LANES):
        sl = slice(j * LANES, (j + 1) * LANES)
        qt_ref[0, sl, :] = (rope(rt[sl], 0, *tab_t) * scale).astype(BF16)
    vt_ref[0] = rt[NSA_WIDTH:NSA_WIDTH + 2 * KV_WIDTH].astype(BF16)
    gt_ref[0] = jax.nn.sigmoid(rt[NSA_WIDTH + 2 * KV_WIDTH:])


def _inproj_call(x2, gain, wh, wk, wt, cs, tiles_per_seq):
    n = x2.shape[0]
    tm = PROJ_TM
    t = tiles_per_seq * tm
    b = n // t
    row = lambda w: pl.BlockSpec((tm, w), lambda i: (i, 0))
    col = lambda h: pl.BlockSpec((1, h, tm), lambda i: (i // tiles_per_seq, 0, i % tiles_per_seq))
    full = lambda a: pl.BlockSpec(a.shape, lambda i: (0, 0))
    gate_rows = NSA_KV_HEADS * LANES
    return pl.pallas_call(
        _inproj_body,
        grid=(n // tm,),
        in_specs=[row(D_MODEL), full(gain), full(wh), full(wk), full(wt),
                  col(ROPE_DIM)],
        out_specs=[row(4 * HG_WIDTH), row(KV_WIDTH), row(KV_WIDTH), row(2 * KV_WIDTH),
                   col(NSA_WIDTH), col(2 * KV_WIDTH), col(gate_rows)],
        out_shape=[jax.ShapeDtypeStruct((n, 4 * HG_WIDTH), F32),
                   jax.ShapeDtypeStruct((n, KV_WIDTH), F32),
                   jax.ShapeDtypeStruct((n, KV_WIDTH), F32),
                   jax.ShapeDtypeStruct((n, 2 * KV_WIDTH), BF16),
                   jax.ShapeDtypeStruct((b, NSA_WIDTH, t), BF16),
                   jax.ShapeDtypeStruct((b, 2 * KV_WIDTH, t), BF16),
                   jax.ShapeDtypeStruct((b, gate_rows, t), F32)],
        compiler_params=pltpu.CompilerParams(dimension_semantics=("arbitrary",),
                                             vmem_limit_bytes=VMEM_LIMIT),
        name="inproj",
    )(x2, gain, wh, wk, wt, cs)


def _hgrn_tables():
    L = HG_CHUNK
    t = np.arange(L)[:, None]
    u = np.arange(L)[None, :]
    level = np.where(((t // HG_DIAG) == (u // HG_DIAG)) & (u <= t), 1, 0)
    for li, s in enumerate(HG_LEVELS):
        same = (t // (2 * s)) == (u // (2 * s))
        right = (t % (2 * s)) >= s
        level = np.where(same & right & ((u % (2 * s)) < s), li + 2, level)
    return jnp.asarray((u <= t).astype(np.float32), BF16), jnp.asarray(level, jnp.int32)


def _hgrn_body(q_ref, f_ref, i_ref, g_ref, lb_ref, gain_ref, mst_ref, lvl_ref, o_ref, st_ref):
    L = HG_CHUNK
    n_chunks = q_ref.shape[1] // L

    @pl.when(pl.program_id(1) == 0)
    def _sequence_start():
        st_ref[...] = jnp.zeros_like(st_ref)

    def chunk(c, carry):
        rows = pl.ds(pl.multiple_of(c * L, L), L)
        heads = range(HG_HEADS)
        cols = [slice(h * HG_DK, (h + 1) * HG_DK) for h in heads]
        mst = mst_ref[...]
        lvl = lvl_ref[...]
        n_lv = len(HG_LEVELS)
        row_i = lax.broadcasted_iota(jnp.int32, (L, HG_DK), 0)
        q = [q_ref[0, rows, cols[h]] for h in heads]
        vb = [i_ref[0, rows, cols[h]].astype(BF16) for h in heads]
        f = [lb_ref[:, cols[h]] + (1.0 - lb_ref[:, cols[h]]) * jax.nn.sigmoid(f_ref[0, rows, cols[h]]) for h in heads]
        k = [1.0 - f[h] for h in heads]
        parts = [_split3(jnp.log2(f[h])) for h in heads]
        e_full = [(_dot(mst, parts[h][0]) + _dot(mst, parts[h][1])) + _dot(mst, parts[h][2]) for h in heads]
        b_last = [e_full[h][L - 1:L, :] for h in heads]

        def rel_to(b, blk, off):
            refs = []
            for r0 in range(0, L, blk):
                r = r0 + off - 1
                ref = b[r:r + 1, :] if r >= 0 else jnp.zeros((1, HG_DK), F32)
                refs.append(jnp.broadcast_to(ref, (blk, HG_DK)))
            return b - jnp.concatenate(refs, axis=0)

        def level_sums(b):
            out = [rel_to(b, HG_DIAG, 0)]
            for s_half in HG_LEVELS:
                d = rel_to(b, 2 * s_half, s_half)
                out.append(jnp.where((row_i % (2 * s_half)) >= s_half, d, -d))
            return out

        e = [level_sums(e_full[h]) for h in heads]
        wq = [[jnp.exp2(e[h][l]) for l in range(n_lv + 1)] for h in heads]
        wk = [[jnp.exp2(-e[h][0])] + wq[h][1:] for h in heads]
        prod = [[_dot_nt((q[h] * wq[h][l]).astype(BF16), (k[h] * wk[h][l]).astype(BF16)) for l in range(n_lv + 1)]
                for h in heads]
        st = [st_ref[h] for h in heads]
        inter = [_dot_nt((q[h] * jnp.exp2(e_full[h])).astype(BF16), st[h].astype(BF16)) for h in heads]
        k_dec = [(k[h] * jnp.exp2(b_last[h] - e_full[h])).astype(BF16) for h in heads]
        upd = [_dot_tn(vb[h], k_dec[h]) for h in heads]
        for h in heads:
            st_ref[h] = st[h] * jnp.exp2(b_last[h]) + upd[h]
        a = []
        for h in heads:
            ah = jnp.where(lvl == 1, prod[h][0], 0.0)
            for l in range(1, n_lv + 1):
                ah = jnp.where(lvl == l + 1, prod[h][l], ah)
            a.append(ah.astype(BF16))
        o = [_dot(a[h], vb[h]) + inter[h] for h in heads]
        for h in heads:
            oh = o[h] * lax.rsqrt(jnp.mean(o[h] * o[h], axis=-1, keepdims=True) + EPS) * gain_ref[:, cols[h]]
            o_ref[0, rows, cols[h]] = (oh * jax.nn.silu(g_ref[0, rows, cols[h]])).astype(o_ref.dtype)
        return carry

    lax.fori_loop(0, n_chunks, chunk, 0, unroll=True)


def _hgrn_call(hg, lb, gain, mst, lvl):
    b, t, _ = hg.shape
    tt = HG_TT
    col = lambda k: pl.BlockSpec((1, tt, HG_WIDTH), lambda bi, ti: (bi, ti, k))
    full = lambda a: pl.BlockSpec(a.shape, lambda bi, ti: (0, 0))
    return pl.pallas_call(
        _hgrn_body,
        grid=(b, t // tt),
        in_specs=[col(0), col(1), col(2), col(3), full(lb), full(gain), full(mst), full(lvl)],
        out_specs=pl.BlockSpec((1, tt, HG_WIDTH), lambda bi, ti: (bi, ti, 0)),
        out_shape=jax.ShapeDtypeStruct((b, t, HG_WIDTH), BF16),
        scratch_shapes=[pltpu.VMEM((HG_HEADS, HG_DV, HG_DK), F32)],
        compiler_params=pltpu.CompilerParams(dimension_semantics=("arbitrary", "arbitrary"),
                                             vmem_limit_bytes=VMEM_LIMIT),
        name="hgrn2",
    )(hg, hg, hg, hg, lb, gain, mst, lvl)


def _cmp_body(kcn_ref, vcn_ref, pek_ref, pev_ref, w1k_ref, w1v_ref, w2k_ref, w2v_ref, kc_ref, vc_ref):
    nb = kcn_ref.shape[1] // CMP_STRIDE
    lane_grp = (lax.broadcasted_iota(jnp.int32, (nb, CMP_STRIDE * LANES), 1) // NSA_HEAD_DIM) % NSA_KV_HEADS

    def hidden(src_ref, pe_ref, w1_ref):
        x = jnp.concatenate([src_ref[0, pl.ds(l, nb, stride=CMP_STRIDE), :]
                             for l in range(CMP_STRIDE)], axis=1)
        halves = [x + pe_ref[i] for i in range(2)]
        out = []
        for g in range(NSA_KV_HEADS):
            u, v = (_dot(jnp.where(lane_grp == g, halves[i], 0.0).astype(BF16), w1_ref[i]) for i in range(2))
            out.append(jax.nn.silu(u + pltpu.roll(v, nb - 1, 0)).astype(BF16))
        return out

    hk = hidden(kcn_ref, pek_ref, w1k_ref)
    hv = hidden(vcn_ref, pev_ref, w1v_ref)
    for g in range(NSA_KV_HEADS):
        kc_ref[0, g, 0:nb, :] = _dot(hk[g], w2k_ref[0]).astype(kc_ref.dtype)
        kc_ref[0, g, nb:2 * nb, :] = _dot(hk[g], w2k_ref[1]).astype(kc_ref.dtype)
        vc_ref[0, g, :, 0:nb] = _dot_nt(w2v_ref[0], hv[g]).astype(vc_ref.dtype)
        vc_ref[0, g, :, nb:2 * nb] = _dot_nt(w2v_ref[1], hv[g]).astype(vc_ref.dtype)


def _cmp_call(kcn, vcn, pek, pev, w1k, w1v, w2k, w2v):
    b, t, w = kcn.shape
    nb = t // CMP_STRIDE
    full = lambda a: pl.BlockSpec(a.shape, lambda bi: (0,) * a.ndim)
    out = lambda r, c: pl.BlockSpec((1, NSA_KV_HEADS, r, c), lambda bi: (bi, 0, 0, 0))
    return pl.pallas_call(
        _cmp_body,
        grid=(b,),
        in_specs=[pl.BlockSpec((1, t, w), lambda bi: (bi, 0, 0)), pl.BlockSpec((1, t, w), lambda bi: (bi, 0, 0)),
                  full(pek), full(pev), full(w1k), full(w1v), full(w2k), full(w2v)],
        out_specs=[out(2 * nb, LANES), out(LANES, 2 * nb)],
        out_shape=[jax.ShapeDtypeStruct((b, NSA_KV_HEADS, 2 * nb, LANES), BF16),
                   jax.ShapeDtypeStruct((b, NSA_KV_HEADS, LANES, 2 * nb), BF16)],
        compiler_params=pltpu.CompilerParams(dimension_semantics=("arbitrary",),
                                             vmem_limit_bytes=VMEM_LIMIT),
        name="nsa_compress",
    )(kcn, vcn, pek, pev, w1k, w1v, w2k, w2v)


def _nsa_body(qt_ref, ksw_ref, vst_ref, vwt_ref, kc_ref, vct_ref, gt_ref, gain_ref, ovt_ref, o_ref,
              ks_ref, kw_ref, vs_ref, vw_ref, m_ref, acc_ref, s_ref, ch_ref):
    g = pl.program_id(1)
    qi = pl.program_id(2)
    tq = ATT_TQ
    tk = ATT_TK
    t_len = ksw_ref.shape[1]
    n_kt = t_len // tk
    n_pairs = HPG // 2
    hd = NSA_HEAD_DIM

    @pl.when(qi == 0)
    def _build_kv():
        lane = lax.broadcasted_iota(jnp.int32, (tk, LANES), 1)
        lo_lane = lane < hd
        keep = (lane // hd) == g

        def build_k(src_col, dst_ref):
            for j in range(n_kt):
                x = ksw_ref[0, j * tk:(j + 1) * tk, src_col * LANES:(src_col + 1) * LANES].astype(F32)
                dup = jnp.where(keep, x, pltpu.roll(x, hd, 1))
                dst_ref[j, 0:tk, :] = jnp.where(lo_lane, dup, 0.0).astype(BF16)
                dst_ref[j, tk:2 * tk, :] = jnp.where(lo_lane, 0.0, dup).astype(BF16)

        def build_v(src_ref, dst_ref):
            zero = jnp.zeros((hd, tk), BF16)
            row = lax.broadcasted_iota(jnp.int32, (SUM_ROWS, 2 * tk), 0)
            col = lax.broadcasted_iota(jnp.int32, (SUM_ROWS, 2 * tk), 1)
            ones_rows = jnp.where(((row == 0) & (col < tk)) | ((row == 1) & (col >= tk)), 1.0, 0.0).astype(BF16)
            for j in range(n_kt):
                x = src_ref[0, :, j * tk:(j + 1) * tk]
                dst_ref[j, 0:hd, 0:tk] = x
                dst_ref[j, 0:hd, tk:2 * tk] = zero
                dst_ref[j, hd:2 * hd, 0:tk] = zero
                dst_ref[j, hd:2 * hd, tk:2 * tk] = x
                dst_ref[j, 2 * hd:2 * hd + SUM_ROWS, :] = ones_rows

        build_k(0, ks_ref)
        build_k(1, kw_ref)
        build_v(vst_ref, vs_ref)
        build_v(vwt_ref, vw_ref)

    t0 = qi * tq
    key_i = lax.broadcasted_iota(jnp.int32, (tk, tq), 0)
    qry_t = t0 + lax.broadcasted_iota(jnp.int32, (tk, tq), 1)
    slab_lo = lax.broadcasted_iota(jnp.int32, (LANES, tq), 0) < hd
    acc_row = lax.broadcasted_iota(jnp.int32, (LANES + SUM_ROWS, tq), 0)
    slab_a = (acc_row < hd) | (acc_row == LANES)
    q_pairs = [qt_ref[0, p * LANES:(p + 1) * LANES, :] for p in range(n_pairs)]

    last = (t0 + tq - 1) // tk

    def scores(k_ref, j):
        kt = k_ref[j]
        return [_dot(kt, q_pairs[p]) for p in range(n_pairs)]

    n_cmp_pad = kc_ref.shape[2] // 2
    blk_i = lax.broadcasted_iota(jnp.int32, (n_cmp_pad, tq), 0)
    blk_t = t0 + lax.broadcasted_iota(jnp.int32, (n_cmp_pad, tq), 1)
    cmp_ok = (blk_i * CMP_STRIDE + (CMP_BLOCK - 1)) <= blk_t
    kc = kc_ref[0, 0]
    vct = vct_ref[0, 0]
    s_cmp = [_dot(kc, q_pairs[p]) for p in range(n_pairs)]
    n_win = (WINDOW + tq) // tk
    win_tiles = [last - (n_win - 1) + k for k in range(n_win)]
    n_near = SLC_NEAR_TILES
    near_tiles = [last - k for k in range(n_near)]
    up_front = ([(ks_ref, 0)] + [(kw_ref, jnp.maximum(jw, 0)) for jw in win_tiles]
                + [(ks_ref, jnp.maximum(jn, 0)) for jn in near_tiles])
    for slot, (k_ref, j0) in enumerate(up_front):
        s_first = scores(k_ref, j0)
        for p in range(n_pairs):
            s_ref[slot, p] = s_first[p]
    p_sum = jnp.zeros((n_cmp_pad, tq), F32)
    p_cmp = []
    for p in range(n_pairs):
        probs = []
        for h in range(2):
            sh = jnp.where(cmp_ok, s_cmp[p][h * n_cmp_pad:(h + 1) * n_cmp_pad], NEG_INF)
            mh = jnp.max(sh, axis=0, keepdims=True)
            eh = jnp.where(cmp_ok, jnp.exp2(sh - mh), 0.0)
            den = jnp.sum(eh, axis=0, keepdims=True)
            ph = eh / jnp.where(den > 0.0, den, 1.0)
            p_sum = p_sum + ph
            probs.append(ph.astype(BF16))
        p_cmp.append(jnp.concatenate(probs, axis=0))
    o_cmp = [_dot(vct, p_cmp[p]) for p in range(n_pairs)]

    n_sel = t_len // SLC_BLOCK
    hi, mid, lo = _split3(p_sum)
    ovt = ovt_ref[...]
    p_sel = ((_dot(ovt, hi) + _dot(ovt, mid)) + _dot(ovt, lo))[0:n_sel]
    sel_i = lax.broadcasted_iota(jnp.int32, (n_sel, tq), 0)
    cur = (t0 + lax.broadcasted_iota(jnp.int32, (n_sel, tq), 1)) // SLC_BLOCK
    forced = (sel_i == 0) | (sel_i == cur) | (sel_i == cur - 1)
    score = jnp.where(forced, FORCE_SCORE, p_sel)
    score = jnp.where(sel_i <= cur, score, -jnp.inf)
    rank = jnp.zeros((n_sel, tq), jnp.int32)
    row_grp = SUBLANES
    grp_i = lax.broadcasted_iota(jnp.int32, (row_grp, tq), 0)
    for i in range(n_sel):
        ci = score[i:i + 1, :]
        ahead = []
        for r0 in range(0, n_sel, row_grp):
            rows = slice(r0, r0 + row_grp)
            if r0 > i:
                ahead.append(ci >= score[rows])
            elif r0 + row_grp <= i:
                ahead.append(ci > score[rows])
            else:
                ahead.append((ci > score[rows]) | ((ci == score[rows]) & (grp_i > i - r0)))
        rank = rank + jnp.where(jnp.concatenate(ahead, axis=0), 1, 0)
    chosen = jnp.where(rank < min(SLC_TOPK, n_sel), 1.0, 0.0)
    for i in range(n_sel):
        ch_ref[i] = chosen[i:i + 1, :]

    def live_keys(rel_at_origin, lower, upper):
        out = []
        for qh in range(tq // LANES):
            live = [kb for kb in range(tk // KEY_BLK)
                    if rel_at_origin + qh * LANES + LANES - 1 - kb * KEY_BLK >= lower
                    and rel_at_origin + qh * LANES - (kb * KEY_BLK + KEY_BLK - 1) < upper]
            out.append((min(live) * KEY_BLK, (max(live) + 1) * KEY_BLK))
        return out

    def tile_softmax(slot, bias, m_get, m_put, keys=None):
        alphas = {}
        probs = {}
        for hh in range(HPG):
            p, h = divmod(hh, 2)
            a_parts = []
            p_parts = []
            for qh in range(tq // LANES):
                ql = slice(qh * LANES, (qh + 1) * LANES)
                k_lo, k_hi = keys[qh] if keys is not None else (0, tk)
                sh = (s_ref[slot, p, h * tk + k_lo:h * tk + k_hi, ql]
                      + (bias if bias.shape == (1, 1) else bias[k_lo:k_hi, ql]))
                m_prev = m_get(hh, qh)
                m_new = jnp.maximum(m_prev, jnp.max(sh, axis=0, keepdims=True))
                m_put(hh, qh, m_new)
                piece = [jnp.zeros((k_lo, LANES), BF16)] if k_lo else []
                piece.append(jnp.exp2(sh - m_new).astype(BF16))
                if k_hi < tk:
                    piece.append(jnp.zeros((tk - k_hi, LANES), BF16))
                p_parts.append(jnp.concatenate(piece, axis=0) if len(piece) > 1 else piece[0])
                a_parts.append(jnp.exp2(m_prev - m_new))
            alphas[hh] = jnp.concatenate(a_parts, axis=1)
            probs[hh] = jnp.concatenate(p_parts, axis=1)
        return ([jnp.concatenate([probs[2 * p], probs[2 * p + 1]], axis=0) for p in range(n_pairs)],
                [jnp.where(slab_a, alphas[2 * p], alphas[2 * p + 1]) for p in range(n_pairs)])

    def normalised(acc):
        inv = jnp.where(slab_lo, 1.0 / acc[LANES:LANES + 1, :], 1.0 / acc[LANES + 1:LANES + 2, :])
        return acc[0:LANES] * inv

    m_win = {}
    acc_win = [jnp.zeros((LANES + SUM_ROWS, tq), F32) for _ in range(n_pairs)]
    for k, jw in enumerate(win_tiles):
        rel_hi = (n_win - k) * tk - 1
        rel_lo = rel_hi - (tq - 1) - (tk - 1)
        exists = jnp.where(jw >= 0, 0.0, NEG_INF).astype(F32).reshape(1, 1)
        if rel_lo >= 0 and rel_hi < WINDOW:
            bias = exists
        else:
            rel = qry_t - (jw * tk + key_i)
            inside = (rel < WINDOW) if rel_lo >= 0 else (rel >= 0) if rel_hi < WINDOW else (rel >= 0) & (rel < WINDOW)
            bias = jnp.where(inside, exists, NEG_INF)
        probs, a_rows = tile_softmax(1 + k, bias,
                                     lambda hh, qh: m_win.get((hh, qh), jnp.full((1, LANES), -jnp.inf, F32)),
                                     lambda hh, qh, v: m_win.__setitem__((hh, qh), v),
                                     keys=live_keys(rel_lo + tk - 1, 0, WINDOW))
        vt = vw_ref[jnp.maximum(jw, 0)]
        acc_win = [acc_win[p] * a_rows[p] + _dot(vt, probs[p]) for p in range(n_pairs)]
    o_win = [normalised(acc_win[p]) for p in range(n_pairs)]

    def picked_bias(j, also=None):
        per_tile = tk // SLC_BLOCK
        picked = jnp.concatenate([jnp.broadcast_to(ch_ref[j * per_tile + i], (SLC_BLOCK, tq)) for i in range(per_tile)],
                                 axis=0) > 0.5
        return jnp.where(picked if also is None else picked & also, 0.0, NEG_INF)

    m_near = {}
    acc_near = [jnp.zeros((LANES + SUM_ROWS, tq), F32) for _ in range(n_pairs)]
    for k, jn in enumerate(near_tiles):
        jc = jnp.maximum(jn, 0)
        causal = (jn * tk + key_i) <= qry_t
        if k > 0:
            causal = causal & (jn >= 0)
        probs, a_rows = tile_softmax(1 + n_win + k, picked_bias(jc, causal),
                                     lambda hh, qh: m_near.get((hh, qh), jnp.full((1, LANES), -jnp.inf, F32)),
                                     lambda hh, qh, v: m_near.__setitem__((hh, qh), v),
                                     keys=live_keys(tk - tq + k * tk, 0, t_len))
        vt = vs_ref[jc]
        acc_near = [acc_near[p] * a_rows[p] + _dot(vt, probs[p]) for p in range(n_pairs)]

    def m_put(hh, qh, v):
        m_ref[hh, :, qh * LANES:(qh + 1) * LANES] = v

    for (hh, qh), v in m_near.items():
        m_put(hh, qh, v)
    for p in range(n_pairs):
        acc_ref[p] = acc_near[p]
    n_far = jnp.maximum(last + 1 - n_near, 0)

    def slc_step(j, carry):
        s_next = scores(ks_ref, jnp.minimum(j + 1, n_far - 1))
        probs, a_rows = tile_softmax(0, picked_bias(j), lambda hh, qh: m_ref[hh, :, qh * LANES:(qh + 1) * LANES], m_put)
        vt = vs_ref[j]
        for p in range(n_pairs):
            pv = _dot(vt, probs[p])
            s_ref[0, p] = s_next[p]
            acc_ref[p] = acc_ref[p] * a_rows[p] + pv
        return carry

    lax.fori_loop(0, n_far, slc_step, 0)
    o_slc = [normalised(acc_ref[p]) for p in range(n_pairs)]

    gates = gt_ref[0]
    gain = gain_ref[...]
    for p in range(n_pairs):
        o = jnp.zeros((LANES, tq), F32)
        for c, branch in enumerate((o_cmp[p], o_slc[p], o_win[p])):
            r = c * HPG + 2 * p
            o = o + jnp.where(slab_lo, gates[r:r + 1, :], gates[r + 1:r + 2, :]) * branch
        sq = o * o
        ms_a = jnp.sum(sq[0:hd], axis=0, keepdims=True)
        ms_b = jnp.sum(sq[hd:2 * hd], axis=0, keepdims=True)
        ms = jnp.where(slab_lo, ms_a, ms_b) * (1.0 / hd)
        o = o * lax.rsqrt(ms + EPS)
        o_ref[0, p * LANES:(p + 1) * LANES, :] = (o * gain[p * LANES:(p + 1) * LANES, :]).astype(o_ref.dtype)


def _nsa_call(qt, ksw, vt, kc, vct, gt, gain, ovt):
    b, _, t = qt.shape
    tq, tk = ATT_TQ, ATT_TK
    n_kt = t // tk
    gw = HPG * NSA_HEAD_DIM
    hd = NSA_HEAD_DIM
    k_scratch = pltpu.VMEM((n_kt, 2 * tk, LANES), BF16)
    v_scratch = pltpu.VMEM((n_kt, LANES + SUM_ROWS, 2 * tk), BF16)
    return pl.pallas_call(
        _nsa_body,
        grid=(b, NSA_KV_HEADS, t // tq),
        in_specs=[
            pl.BlockSpec((1, gw, tq), lambda bi, gi, qi: (bi, gi, qi)),
            pl.BlockSpec((1, t, 2 * KV_WIDTH), lambda bi, gi, qi: (bi, 0, 0)),
            pl.BlockSpec((1, hd, t), lambda bi, gi, qi: (bi, gi, 0)),
            pl.BlockSpec((1, hd, t), lambda bi, gi, qi: (bi, NSA_KV_HEADS + gi, 0)),
            pl.BlockSpec((1, 1) + kc.shape[2:], lambda bi, gi, qi: (bi, gi, 0, 0)),
            pl.BlockSpec((1, 1) + vct.shape[2:], lambda bi, gi, qi: (bi, gi, 0, 0)),
            pl.BlockSpec((1, LANES, tq), lambda bi, gi, qi: (bi, gi, qi)),
            pl.BlockSpec((gw, 1), lambda bi, gi, qi: (gi, 0)),
            pl.BlockSpec(ovt.shape, lambda bi, gi, qi: (0, 0)),
        ],
        out_specs=pl.BlockSpec((1, gw, tq), lambda bi, gi, qi: (bi, gi, qi)),
        out_shape=jax.ShapeDtypeStruct((b, NSA_WIDTH, t), BF16),
        scratch_shapes=[k_scratch, k_scratch, v_scratch, v_scratch,
                        pltpu.VMEM((HPG, 1, tq), F32),
                        pltpu.VMEM((HPG // 2, LANES + SUM_ROWS, tq), F32), pltpu.VMEM((1 + (WINDOW + tq) // tk + SLC_NEAR_TILES, HPG // 2, 2 * tk, tq), F32),
                        pltpu.VMEM((t // SLC_BLOCK, 1, tq), F32)],
        compiler_params=pltpu.CompilerParams(dimension_semantics=("arbitrary", "arbitrary", "arbitrary"),
                                             vmem_limit_bytes=VMEM_LIMIT),
        name="nsa_attention",
    )(qt, ksw, vt, vt, kc, vct, gt, gain, ovt)


def _ffn_body(x_ref, oh_ref, on_ref, woh_ref, won_ref, g2_ref, wg_ref, wu_ref, wd_ref, cw_ref, gf_ref,
              out_ref, halo_ref, act_ref, *, tiles_per_seq):
    tm = x_ref.shape[0]
    x1 = x_ref[...] + _dot(oh_ref[...], woh_ref[...]) + _dot_tn(on_ref[0], won_ref[...])
    hb = _rms(x1, g2_ref[...]).astype(BF16)
    row = lax.broadcasted_iota(jnp.int32, (tm, FFN_TC), 0)

    @pl.when((pl.program_id(0) % tiles_per_seq) == 0)
    def _sequence_start():
        halo_ref[...] = jnp.zeros_like(halo_ref)

    def activation(c, gate, up):
        cols = slice(c * FFN_TC, (c + 1) * FFN_TC)
        halo = halo_ref[:, cols]
        halo_ref[:, cols] = gate[tm - SUBLANES:tm, :]
        last1 = halo[SUBLANES - 1:SUBLANES, :]
        last2 = halo[SUBLANES - 2:SUBLANES - 1, :]
        prev1 = jnp.where(row == 0, last1, pltpu.roll(gate, 1, 0))
        prev2 = jnp.where(row == 0, last2, jnp.where(row == 1, last1, pltpu.roll(gate, 2, 0)))
        cw = cw_ref[:, cols]
        y = cw[0:1, :] * prev2 + cw[1:2, :] * prev1 + cw[2:3, :] * gate + cw[3:4, :]
        return (jax.nn.silu(y) * up).astype(BF16)

    chunk = lambda w_ref, c: _dot(hb, w_ref[:, c * FFN_TC:(c + 1) * FFN_TC])
    gate_up = (chunk(wg_ref, 0), chunk(wu_ref, 0))
    for c in range(FFN_NC):
        cur = gate_up
        if c + 1 < FFN_NC:
            gate_up = (chunk(wg_ref, c + 1), chunk(wu_ref, c + 1))
        act_ref[:, c * FFN_TC:(c + 1) * FFN_TC] = activation(c, *cur)
    acc = _dot(act_ref[...], wd_ref[...])
    out_ref[...] = _rms(x1 + acc, gf_ref[...])


def _ffn_call(x2, oh, on, woh, won, g2, wg, wu, wd, cw, gf, tiles_per_seq):
    n = x2.shape[0]
    tm = FFN_TM
    row = lambda w: pl.BlockSpec((tm, w), lambda i: (i, 0))
    full = lambda a: pl.BlockSpec(a.shape, lambda i: (0,) * a.ndim, pipeline_mode=pl.Buffered(1))
    return pl.pallas_call(
        functools.partial(_ffn_body, tiles_per_seq=tiles_per_seq),
        grid=(n // tm,),
        in_specs=[row(D_MODEL), row(HG_WIDTH),
                  pl.BlockSpec((1, NSA_WIDTH, tm), lambda i: (i // tiles_per_seq, 0, i % tiles_per_seq)),
                  full(woh), full(won), full(g2),
                  full(wg), full(wu), full(wd), full(cw), full(gf)],
        out_specs=row(D_MODEL),
        out_shape=jax.ShapeDtypeStruct((n, D_MODEL), F32),
        scratch_shapes=[pltpu.VMEM((SUBLANES, D_FF), F32), pltpu.VMEM((tm, D_FF), BF16)],
        compiler_params=pltpu.CompilerParams(dimension_semantics=("arbitrary",),
                                             vmem_limit_bytes=VMEM_LIMIT),
        name="outproj_convffn",
    )(x2, oh, on, woh, won, g2, wg, wu, wd, cw, gf)


def _rope_angles(positions):
    inv_freq = ROPE_THETA ** (-jnp.arange(ROPE_HALF, dtype=F32) * 2.0 / ROPE_DIM)
    ang = positions.astype(F32)[..., None] * inv_freq
    return jnp.concatenate([jnp.cos(ang), jnp.sin(ang)], axis=-1).transpose(0, 2, 1)


def _layer(x, positions, ln1, w_in, lb, hg_gain, pe_k, pe_v, k_w1, k_w2, v_w1, v_w2, nsa_gain, w_o, ln2,
           w_gate, w_up, conv_w, conv_b, w_down, final_gain):
    b, t, d = x.shape
    n = b * t
    assert d == D_MODEL and t % FFN_TM == 0 and t % PROJ_TM == 0 and t % ATT_TQ == 0 and t % HG_TT == 0
    n_grp = t // CMP_STRIDE
    assert n_grp == LANES, "compressed-block axis is laid out on exactly one lane tile"
    n_sel = t // SLC_BLOCK
    assert n_sel % 8 == 0 and n_sel <= LANES and ATT_TK % SLC_BLOCK == 0 and ATT_TQ % ATT_TK == 0
    x2 = x.reshape(n, d)

    splits = np.cumsum([0, 4 * HG_WIDTH, NSA_WIDTH] + [KV_WIDTH] * 6 + [N_GATES])
    seg = lambda i: w_in[:, splits[i]:splits[i + 1]]
    wh = seg(0).astype(BF16)
    wk = jnp.concatenate([seg(2), seg(3), seg(4), seg(6)], axis=1).astype(BF16)
    wgate = seg(8).reshape(d, 3, NSA_KV_HEADS, HPG).transpose(0, 2, 1, 3).reshape(d, NSA_KV_HEADS, 3 * HPG)
    wgate = jnp.pad(wgate, ((0, 0), (0, 0), (0, LANES - 3 * HPG))).reshape(d, NSA_KV_HEADS * LANES)
    wt = jnp.concatenate([seg(1), seg(5), seg(7), wgate], axis=1).T.astype(BF16)
    cs = _rope_angles(positions)

    hg, kcn, vcn, ksw, qt, vt, gt = _inproj_call(x2, ln1.reshape(1, d), wh, wk, wt, cs, t // PROJ_TM)

    mst, lvl = _hgrn_tables()
    o_hg = _hgrn_call(hg.reshape(b, t, 4 * HG_WIDTH), lb.reshape(1, HG_WIDTH).astype(F32),
                      hg_gain.reshape(1, HG_WIDTH), mst, lvl)

    per_lane = lambda a: jnp.broadcast_to(a.reshape(2, CMP_STRIDE, 1, NSA_HEAD_DIM, -1),
                                          (2, CMP_STRIDE, NSA_KV_HEADS, NSA_HEAD_DIM, a.shape[-1]))
    w1_rows = lambda w1: per_lane(w1).reshape(2, CMP_STRIDE * LANES, CMP_HIDDEN).astype(BF16)
    pe_rows = lambda pe: per_lane(pe[..., None]).reshape(2, 1, CMP_STRIDE * LANES)
    zeros_w2 = jnp.zeros((CMP_HIDDEN, NSA_HEAD_DIM), F32)
    place = lambda w2: jnp.stack([jnp.concatenate([w2, zeros_w2], 1), jnp.concatenate([zeros_w2, w2], 1)])
    kc, vct = _cmp_call(kcn.reshape(b, t, KV_WIDTH), vcn.reshape(b, t, KV_WIDTH), pe_rows(pe_k), pe_rows(pe_v),
                        w1_rows(k_w1), w1_rows(v_w1),
                        place(k_w2).astype(BF16), place(v_w2).transpose(0, 2, 1).astype(BF16))

    cmp_start = np.arange(n_grp) * CMP_STRIDE
    cmp_end = cmp_start + CMP_BLOCK - 1
    sel_start = np.arange(LANES) * SLC_BLOCK
    overlap = ((cmp_start[:, None] <= sel_start[None, :] + SLC_BLOCK - 1) & (cmp_end[:, None] >= sel_start[None, :])
               & (np.arange(LANES)[None, :] < n_sel) & (np.arange(n_grp)[:, None] < n_grp - 1))
    ovt = jnp.asarray(overlap.T.astype(np.float32), BF16)
    o_nsa = _nsa_call(qt, ksw.reshape(b, t, 2 * KV_WIDTH), vt, kc, vct, gt, nsa_gain.reshape(NSA_WIDTH, 1), ovt)

    cw = jnp.concatenate([conv_w, conv_b[None, :], jnp.zeros((SUBLANES - CONV_WIDTH - 1, D_FF), F32)], axis=0)
    out = _ffn_call(x2, o_hg.reshape(n, HG_WIDTH), o_nsa,
                    w_o[:HG_WIDTH].astype(BF16), w_o[HG_WIDTH:].astype(BF16), ln2.reshape(1, d),
                    w_gate.astype(BF16), w_up.astype(BF16), w_down.astype(BF16), cw,
                    final_gain.reshape(1, d), t // FFN_TM)
    return out.reshape(b, t, d)


def kernel(x, positions, ln1_gain, w_in, hgrn_lb_param, hgrn_out_gain, cmp_pe_k, cmp_pe_v, cmp_k_w1, cmp_k_w2,
           cmp_v_w1, cmp_v_w2, nsa_out_gain, w_o, ln2_gain, ffn_w_gate, ffn_w_up, ffn_conv_w, ffn_conv_b,
           ffn_w_down, final_gain):
    depth = ln1_gain.shape[0]
    assert depth == 1, "the fused final norm assumes a single layer"
    lower_bounds = jnp.cumsum(jax.nn.softmax(hgrn_lb_param.astype(F32), axis=0), axis=0)
    l = 0
    return _layer(x, positions, ln1_gain[l], w_in[l], lower_bounds[l], hgrn_out_gain[l], cmp_pe_k[l], cmp_pe_v[l],
                  cmp_k_w1[l], cmp_k_w2[l], cmp_v_w1[l], cmp_v_w2[l], nsa_out_gain[l], w_o[l], ln2_gain[l],
                  ffn_w_gate[l], ffn_w_up[l], ffn_conv_w[l], ffn_conv_b[l], ffn_w_down[l], final_gain)
```

```python
import functools

import jax
import jax.numpy as jnp
import numpy as np
from jax import lax
from jax.experimental import pallas as pl
from jax.experimental.pallas import tpu as pltpu

F32 = jnp.float32
BF16 = jnp.bfloat16

D_MODEL = 1024
HG_HEADS = 4
HG_DK = 128
HG_DV = 128
HG_WIDTH = HG_HEADS * HG_DV
NSA_HEADS = 8
NSA_KV_HEADS = 2
NSA_HEAD_DIM = 64
HPG = NSA_HEADS // NSA_KV_HEADS
NSA_WIDTH = NSA_HEADS * NSA_HEAD_DIM
KV_WIDTH = NSA_KV_HEADS * NSA_HEAD_DIM
CMP_BLOCK = 32
CMP_STRIDE = 16
CMP_HIDDEN = 256
SLC_BLOCK = 64
SLC_TOPK = 16
WINDOW = 512
ROPE_THETA = 500000.0
ROPE_DIM = NSA_HEAD_DIM // 4
ROPE_HALF = ROPE_DIM // 2
D_FF = 2816
CONV_WIDTH = 3
EPS = 1e-6
NEG_INF = -1e30
FORCE_SCORE = 1e4
N_GATES = 3 * NSA_HEADS
LOG2_E = 1.4426950408889634

LANES = 128
SUBLANES = 8
VMEM_LIMIT = 56 * 1024 * 1024

PROJ_TM = 1024
HG_CHUNK = 128
HG_LEVELS = (16, 32, 64)
HG_DIAG = 16
HG_TT = 1024
ATT_TQ = 256
ATT_TK = 256
KEY_BLK = 128
SLC_NEAR_TILES = 1
SUM_ROWS = 16
FFN_TM = 512
FFN_TC = 256
FFN_NC = D_FF // FFN_TC


def _dot(a, b):
    return jnp.dot(a, b, preferred_element_type=F32)


def _dot_nt(a, b):
    return lax.dot_general(a, b, (((1,), (1,)), ((), ())), preferred_element_type=F32)


def _dot_tn(a, b):
    return lax.dot_general(a, b, (((0,), (0,)), ((), ())), preferred_element_type=F32)


def _split3(x):
    hi = x.astype(BF16)
    r = x - hi.astype(F32)
    mid = r.astype(BF16)
    lo = (r - mid.astype(F32)).astype(BF16)
    return hi, mid, lo


def _rms(x, gain):
    return x * lax.rsqrt(jnp.mean(x * x, axis=-1, keepdims=True) + EPS) * gain


def _inproj_body(x_ref, g_ref, wh_ref, wk_ref, wt_ref, cs_ref,
                 hg_ref, kcn_ref, vcn_ref, ksw_ref, qt_ref, vt_ref, gt_ref):
    hb = _rms(x_ref[...], g_ref[...]).astype(BF16)
    hg_ref[...] = _dot(hb, wh_ref[...])

    def rope(v, axis, cos, sin_hi, sin_lo):
        return (v * cos + pltpu.roll(v, ROPE_HALF, axis) * sin_hi
                + pltpu.roll(v, LANES - ROPE_HALF, axis) * sin_lo)

    cos = cs_ref[0, 0:ROPE_HALF, :]
    sin = cs_ref[0, ROPE_HALF:ROPE_DIM, :]
    tm = cos.shape[1]
    zero_h = jnp.zeros((ROPE_HALF, tm), F32)
    rest = NSA_HEAD_DIM - ROPE_DIM
    slab = lambda lo, hi, fill: jnp.concatenate([lo, hi, jnp.full((rest, tm), fill, F32)] * (LANES // NSA_HEAD_DIM), axis=0)
    tab_t = (slab(cos, cos, 1.0), slab(zero_h, sin, 0.0), slab(-sin, zero_h, 0.0))
    tab = tuple(a.T for a in tab_t)
    kn = _dot(hb, wk_ref[...])
    kcn_ref[...] = rope(kn[:, 0:LANES], 1, *tab)
    vcn_ref[...] = kn[:, LANES:2 * LANES]
    ksw_ref[:, 0:LANES] = rope(kn[:, 2 * LANES:3 * LANES], 1, *tab).astype(BF16)
    ksw_ref[:, LANES:2 * LANES] = rope(kn[:, 3 * LANES:4 * LANES], 1, *tab).astype(BF16)

    rt = _dot_nt(wt_ref[...], hb)
    scale = NSA_HEAD_DIM ** -0.5 * LOG2_E
    for j in range(NSA_WIDTH // LANES):
        sl = slice(j * LANES, (j + 1) * LANES)
        qt_ref[0, sl, :] = (rope(rt[sl], 0, *tab_t) * scale).astype(BF16)
    vt_ref[0] = rt[NSA_WIDTH:NSA_WIDTH + 2 * KV_WIDTH].astype(BF16)
    gt_ref[0] = jax.nn.sigmoid(rt[NSA_WIDTH + 2 * KV_WIDTH:])


def _inproj_call(x2, gain, wh, wk, wt, cs, tiles_per_seq):
    n = x2.shape[0]
    tm = PROJ_TM
    t = tiles_per_seq * tm
    b = n // t
    row = lambda w: pl.BlockSpec((tm, w), lambda i: (i, 0))
    col = lambda h: pl.BlockSpec((1, h, tm), lambda i: (i // tiles_per_seq, 0, i % tiles_per_seq))
    full = lambda a: pl.BlockSpec(a.shape, lambda i: (0, 0))
    gate_rows = NSA_KV_HEADS * LANES
    return pl.pallas_call(
        _inproj_body,
        grid=(n // tm,),
        in_specs=[row(D_MODEL), full(gain), full(wh), full(wk), full(wt),
                  col(ROPE_DIM)],
        out_specs=[row(4 * HG_WIDTH), row(KV_WIDTH), row(KV_WIDTH), row(2 * KV_WIDTH),
                   col(NSA_WIDTH), col(2 * KV_WIDTH), col(gate_rows)],
        out_shape=[jax.ShapeDtypeStruct((n, 4 * HG_WIDTH), F32),
                   jax.ShapeDtypeStruct((n, KV_WIDTH), F32),
                   jax.ShapeDtypeStruct((n, KV_WIDTH), F32),
                   jax.ShapeDtypeStruct((n, 2 * KV_WIDTH), BF16),
                   jax.ShapeDtypeStruct((b, NSA_WIDTH, t), BF16),
                   jax.ShapeDtypeStruct((b, 2 * KV_WIDTH, t), BF16),
                   jax.ShapeDtypeStruct((b, gate_rows, t), F32)],
        compiler_params=pltpu.CompilerParams(dimension_semantics=("arbitrary",),
                                             vmem_limit_bytes=VMEM_LIMIT),
        name="inproj",
    )(x2, gain, wh, wk, wt, cs)


def _hgrn_tables():
    L = HG_CHUNK
    t = np.arange(L)[:, None]
    u = np.arange(L)[None, :]
    level = np.where(((t // HG_DIAG) == (u // HG_DIAG)) & (u <= t), 1, 0)
    for li, s in enumerate(HG_LEVELS):
        same = (t // (2 * s)) == (u // (2 * s))
        right = (t % (2 * s)) >= s
        level = np.where(same & right & ((u % (2 * s)) < s), li + 2, level)
    return jnp.asarray((u <= t).astype(np.float32), BF16), jnp.asarray(level, jnp.int32)


def _hgrn_body(q_ref, f_ref, i_ref, g_ref, lb_ref, gain_ref, mst_ref, lvl_ref, o_ref, st_ref):
    L = HG_CHUNK
    n_chunks = q_ref.shape[1] // L

    @pl.when(pl.program_id(1) == 0)
    def _sequence_start():
        st_ref[...] = jnp.zeros_like(st_ref)

    def chunk(c, carry):
        rows = pl.ds(pl.multiple_of(c * L, L), L)
        heads = range(HG_HEADS)
        cols = [slice(h * HG_DK, (h + 1) * HG_DK) for h in heads]
        mst = mst_ref[...]
        lvl = lvl_ref[...]
        n_lv = len(HG_LEVELS)
        row_i = lax.broadcasted_iota(jnp.int32, (L, HG_DK), 0)
        q = [q_ref[0, rows, cols[h]] for h in heads]
        vb = [i_ref[0, rows, cols[h]].astype(BF16) for h in heads]
        f = [lb_ref[:, cols[h]] + (1.0 - lb_ref[:, cols[h]]) * jax.nn.sigmoid(f_ref[0, rows, cols[h]]) for h in heads]
        k = [1.0 - f[h] for h in heads]
        parts = [_split3(jnp.log2(f[h])) for h in heads]
        e_full = [(_dot(mst, parts[h][0]) + _dot(mst, parts[h][1])) + _dot(mst, parts[h][2]) for h in heads]
        b_last = [e_full[h][L - 1:L, :] for h in heads]

        def rel_to(b, blk, off):
            refs = []
            for r0 in range(0, L, blk):
                r = r0 + off - 1
                ref = b[r:r + 1, :] if r >= 0 else jnp.zeros((1, HG_DK), F32)
                refs.append(jnp.broadcast_to(ref, (blk, HG_DK)))
            return b - jnp.concatenate(refs, axis=0)

        def level_sums(b):
            out = [rel_to(b, HG_DIAG, 0)]
            for s_half in HG_LEVELS:
                d = rel_to(b, 2 * s_half, s_half)
                out.append(jnp.where((row_i % (2 * s_half)) >= s_half, d, -d))
            return out

        e = [level_sums(e_full[h]) for h in heads]
        wq = [[jnp.exp2(e[h][l]) for l in range(n_lv + 1)] for h in heads]
        wk = [[jnp.exp2(-e[h][0])] + wq[h][1:] for h in heads]
        prod = [[_dot_nt((q[h] * wq[h][l]).astype(BF16), (k[h] * wk[h][l]).astype(BF16)) for l in range(n_lv + 1)]
                for h in heads]
        st = [st_ref[h] for h in heads]
        inter = [_dot_nt((q[h] * jnp.exp2(e_full[h])).astype(BF16), st[h].astype(BF16)) for h in heads]
        k_dec = [(k[h] * jnp.exp2(b_last[h] - e_full[h])).astype(BF16) for h in heads]
        upd = [_dot_tn(vb[h], k_dec[h]) for h in heads]
        for h in heads:
            st_ref[h] = st[h] * jnp.exp2(b_last[h]) + upd[h]
        a = []
        for h in heads:
            ah = jnp.where(lvl == 1, prod[h][0], 0.0)
            for l in range(1, n_lv + 1):
                ah = jnp.where(lvl == l + 1, prod[h][l], ah)
            a.append(ah.astype(BF16))
        o = [_dot(a[h], vb[h]) + inter[h] for h in heads]
        for h in heads:
            oh = o[h] * lax.rsqrt(jnp.mean(o[h] * o[h], axis=-1, keepdims=True) + EPS) * gain_ref[:, cols[h]]
            o_ref[0, rows, cols[h]] = (oh * jax.nn.silu(g_ref[0, rows, cols[h]])).astype(o_ref.dtype)
        return carry

    lax.fori_loop(0, n_chunks, chunk, 0, unroll=True)


def _hgrn_call(hg, lb, gain, mst, lvl):
    b, t, _ = hg.shape
    tt = HG_TT
    col = lambda k: pl.BlockSpec((1, tt, HG_WIDTH), lambda bi, ti: (bi, ti, k))
    full = lambda a: pl.BlockSpec(a.shape, lambda bi, ti: (0, 0))
    return pl.pallas_call(
        _hgrn_body,
        grid=(b, t // tt),
        in_specs=[col(0), col(1), col(2), col(3), full(lb), full(gain), full(mst), full(lvl)],
        out_specs=pl.BlockSpec((1, tt, HG_WIDTH), lambda bi, ti: (bi, ti, 0)),
        out_shape=jax.ShapeDtypeStruct((b, t, HG_WIDTH), BF16),
        scratch_shapes=[pltpu.VMEM((HG_HEADS, HG_DV, HG_DK), F32)],
        compiler_params=pltpu.CompilerParams(dimension_semantics=("arbitrary", "arbitrary"),
                                             vmem_limit_bytes=VMEM_LIMIT),
        name="hgrn2",
    )(hg, hg, hg, hg, lb, gain, mst, lvl)


def _cmp_body(kcn_ref, vcn_ref, pek_ref, pev_ref, w1k_ref, w1v_ref, w2k_ref, w2v_ref, kc_ref, vc_ref):
    nb = kcn_ref.shape[1] // CMP_STRIDE
    lane_grp = (lax.broadcasted_iota(jnp.int32, (nb, CMP_STRIDE * LANES), 1) // NSA_HEAD_DIM) % NSA_KV_HEADS

    def hidden(src_ref, pe_ref, w1_ref):
        x = jnp.concatenate([src_ref[0, pl.ds(l, nb, stride=CMP_STRIDE), :]
                             for l in range(CMP_STRIDE)], axis=1)
        halves = [x + pe_ref[i] for i in range(2)]
        out = []
        for g in range(NSA_KV_HEADS):
            u, v = (_dot(jnp.where(lane_grp == g, halves[i], 0.0).astype(BF16), w1_ref[i]) for i in range(2))
            out.append(jax.nn.silu(u + pltpu.roll(v, nb - 1, 0)).astype(BF16))
        return out

    hk = hidden(kcn_ref, pek_ref, w1k_ref)
    hv = hidden(vcn_ref, pev_ref, w1v_ref)
    for g in range(NSA_KV_HEADS):
        kc_ref[0, g, 0:nb, :] = _dot(hk[g], w2k_ref[0]).astype(kc_ref.dtype)
        kc_ref[0, g, nb:2 * nb, :] = _dot(hk[g], w2k_ref[1]).astype(kc_ref.dtype)
        vc_ref[0, g, :, 0:nb] = _dot_nt(w2v_ref[0], hv[g]).astype(vc_ref.dtype)
        vc_ref[0, g, :, nb:2 * nb] = _dot_nt(w2v_ref[1], hv[g]).astype(vc_ref.dtype)


def _cmp_call(kcn, vcn, pek, pev, w1k, w1v, w2k, w2v):
    b, t, w = kcn.shape
    nb = t // CMP_STRIDE
    full = lambda a: pl.BlockSpec(a.shape, lambda bi: (0,) * a.ndim)
    out = lambda r, c: pl.BlockSpec((1, NSA_KV_HEADS, r, c), lambda bi: (bi, 0, 0, 0))
    return pl.pallas_call(
        _cmp_body,
        grid=(b,),
        in_specs=[pl.BlockSpec((1, t, w), lambda bi: (bi, 0, 0)), pl.BlockSpec((1, t, w), lambda bi: (bi, 0, 0)),
                  full(pek), full(pev), full(w1k), full(w1v), full(w2k), full(w2v)],
        out_specs=[out(2 * nb, LANES), out(LANES, 2 * nb)],
        out_shape=[jax.ShapeDtypeStruct((b, NSA_KV_HEADS, 2 * nb, LANES), BF16),
                   jax.ShapeDtypeStruct((b, NSA_KV_HEADS, LANES, 2 * nb), BF16)],
        compiler_params=pltpu.CompilerParams(dimension_semantics=("arbitrary",),
                                             vmem_limit_bytes=VMEM_LIMIT),
        name="nsa_compress",
    )(kcn, vcn, pek, pev, w1k, w1v, w2k, w2v)


def _nsa_body(qt_ref, ksw_ref, vst_ref, vwt_ref, kc_ref, vct_ref, gt_ref, gain_ref, ovt_ref, o_ref,
              ks_ref, kw_ref, vs_ref, vw_ref, m_ref, acc_ref, s_ref, ch_ref, ocw_ref):
    g = pl.program_id(1)
    qi = pl.program_id(2)
    tq = ATT_TQ
    tk = ATT_TK
    t_len = ksw_ref.shape[1]
    n_kt = t_len // tk
    n_pairs = HPG // 2
    hd = NSA_HEAD_DIM

    @pl.when(qi == 0)
    def _build_kv():
        lane = lax.broadcasted_iota(jnp.int32, (tk, LANES), 1)
        lo_lane = lane < hd
        keep = (lane // hd) == g

        def build_k(src_col, dst_ref):
            for j in range(n_kt):
                x = ksw_ref[0, j * tk:(j + 1) * tk, src_col * LANES:(src_col + 1) * LANES].astype(F32)
                dup = jnp.where(keep, x, pltpu.roll(x, hd, 1))
                dst_ref[j, 0:tk, :] = jnp.where(lo_lane, dup, 0.0).astype(BF16)
                dst_ref[j, tk:2 * tk, :] = jnp.where(lo_lane, 0.0, dup).astype(BF16)

        def build_v(src_ref, dst_ref):
            zero = jnp.zeros((hd, tk), BF16)
            row = lax.broadcasted_iota(jnp.int32, (SUM_ROWS, 2 * tk), 0)
            col = lax.broadcasted_iota(jnp.int32, (SUM_ROWS, 2 * tk), 1)
            ones_rows = jnp.where(((row == 0) & (col < tk)) | ((row == 1) & (col >= tk)), 1.0, 0.0).astype(BF16)
            for j in range(n_kt):
                x = src_ref[0, :, j * tk:(j + 1) * tk]
                dst_ref[j, 0:hd, 0:tk] = x
                dst_ref[j, 0:hd, tk:2 * tk] = zero
                dst_ref[j, hd:2 * hd, 0:tk] = zero
                dst_ref[j, hd:2 * hd, tk:2 * tk] = x
                dst_ref[j, 2 * hd:2 * hd + SUM_ROWS, :] = ones_rows

        build_k(0, ks_ref)
        build_k(1, kw_ref)
        build_v(vst_ref, vs_ref)
        build_v(vwt_ref, vw_ref)

    t0 = qi * tq
    key_i = lax.broadcasted_iota(jnp.int32, (tk, tq), 0)
    qry_t = t0 + lax.broadcasted_iota(jnp.int32, (tk, tq), 1)
    slab_lo = lax.broadcasted_iota(jnp.int32, (LANES, tq), 0) < hd
    acc_row = lax.broadcasted_iota(jnp.int32, (LANES + SUM_ROWS, tq), 0)
    slab_a = (acc_row < hd) | (acc_row == LANES)
    q_pairs = [qt_ref[0, p * LANES:(p + 1) * LANES, :] for p in range(n_pairs)]

    last = (t0 + tq - 1) // tk

    def scores(k_ref, j):
        kt = k_ref[j]
        return [_dot(kt, q_pairs[p]) for p in range(n_pairs)]

    def live_keys(rel_at_origin, lower, upper):
        out = []
        for qh in range(tq // LANES):
            live = [kb for kb in range(tk // KEY_BLK)
                    if rel_at_origin + qh * LANES + LANES - 1 - kb * KEY_BLK >= lower
                    and rel_at_origin + qh * LANES - (kb * KEY_BLK + KEY_BLK - 1) < upper]
            out.append((min(live) * KEY_BLK, (max(live) + 1) * KEY_BLK))
        return out

    def tile_softmax(slot, bias, m_get, m_put, keys=None):
        alphas = {}
        probs = {}
        for hh in range(HPG):
            p, h = divmod(hh, 2)
            a_parts = []
            p_parts = []
            for qh in range(tq // LANES):
                ql = slice(qh * LANES, (qh + 1) * LANES)
                k_lo, k_hi = keys[qh] if keys is not None else (0, tk)
                sh = (s_ref[slot, p, h * tk + k_lo:h * tk + k_hi, ql]
                      + (bias if bias.shape == (1, 1) else bias[k_lo:k_hi, ql]))
                m_prev = m_get(hh, qh)
                m_new = jnp.maximum(m_prev, jnp.max(sh, axis=0, keepdims=True))
                m_put(hh, qh, m_new)
                piece = [jnp.zeros((k_lo, LANES), BF16)] if k_lo else []
                piece.append(jnp.exp2(sh - m_new).astype(BF16))
                if k_hi < tk:
                    piece.append(jnp.zeros((tk - k_hi, LANES), BF16))
                p_parts.append(jnp.concatenate(piece, axis=0) if len(piece) > 1 else piece[0])
                a_parts.append(jnp.exp2(m_prev - m_new))
            alphas[hh] = jnp.concatenate(a_parts, axis=1)
            probs[hh] = jnp.concatenate(p_parts, axis=1)
        return ([jnp.concatenate([probs[2 * p], probs[2 * p + 1]], axis=0) for p in range(n_pairs)],
                [jnp.where(slab_a, alphas[2 * p], alphas[2 * p + 1]) for p in range(n_pairs)])

    def normalised(acc):
        inv = jnp.where(slab_lo, 1.0 / acc[LANES:LANES + 1, :], 1.0 / acc[LANES + 1:LANES + 2, :])
        return acc[0:LANES] * inv

    def picked_bias(j, also=None):
        per_tile = tk // SLC_BLOCK
        picked = jnp.concatenate([jnp.broadcast_to(ch_ref[j * per_tile + i], (SLC_BLOCK, tq)) for i in range(per_tile)],
                                 axis=0) > 0.5
        return jnp.where(picked if also is None else picked & also, 0.0, NEG_INF)

    def m_put(hh, qh, v):
        m_ref[hh, :, qh * LANES:(qh + 1) * LANES] = v

    n_win = (WINDOW + tq) // tk
    n_near = SLC_NEAR_TILES
    n_far = jnp.maximum(last + 1 - n_near, 0)

    def static_part(first_k, has_far, need_topk):
        n_cmp_pad = kc_ref.shape[2] // 2
        blk_i = lax.broadcasted_iota(jnp.int32, (n_cmp_pad, tq), 0)
        blk_t = t0 + lax.broadcasted_iota(jnp.int32, (n_cmp_pad, tq), 1)
        cmp_ok = (blk_i * CMP_STRIDE + (CMP_BLOCK - 1)) <= blk_t
        kc = kc_ref[0, 0]
        vct = vct_ref[0, 0]
        s_cmp = [_dot(kc, q_pairs[p]) for p in range(n_pairs)]
        win_tiles = [(k, last - (n_win - 1) + k) for k in range(first_k, n_win)]
        near_tiles = [last - k for k in range(n_near)]
        up_front = ([(0, ks_ref, 0)] if has_far else []) + [(1 + k, kw_ref, jw) for k, jw in win_tiles]
        up_front += [(1 + n_win + k, ks_ref, jnp.maximum(jn, 0)) for k, jn in enumerate(near_tiles)]
        for slot, k_ref, j0 in up_front:
            s_first = scores(k_ref, j0)
            for p in range(n_pairs):
                s_ref[slot, p] = s_first[p]
        p_sum = jnp.zeros((n_cmp_pad, tq), F32)
        p_cmp = []
        for p in range(n_pairs):
            probs = []
            for h in range(2):
                sh = jnp.where(cmp_ok, s_cmp[p][h * n_cmp_pad:(h + 1) * n_cmp_pad], NEG_INF)
                mh = jnp.max(sh, axis=0, keepdims=True)
                eh = jnp.where(cmp_ok, jnp.exp2(sh - mh), 0.0)
                den = jnp.sum(eh, axis=0, keepdims=True)
                ph = eh / jnp.where(den > 0.0, den, 1.0)
                p_sum = p_sum + ph
                probs.append(ph.astype(BF16))
            p_cmp.append(jnp.concatenate(probs, axis=0))
        o_cmp = [_dot(vct, p_cmp[p]) for p in range(n_pairs)]

        n_sel = t_len // SLC_BLOCK
        if need_topk:
            hi, mid, lo = _split3(p_sum)
            ovt = ovt_ref[...]
            p_sel = ((_dot(ovt, hi) + _dot(ovt, mid)) + _dot(ovt, lo))[0:n_sel]
            sel_i = lax.broadcasted_iota(jnp.int32, (n_sel, tq), 0)
            cur = (t0 + lax.broadcasted_iota(jnp.int32, (n_sel, tq), 1)) // SLC_BLOCK
            forced = (sel_i == 0) | (sel_i == cur) | (sel_i == cur - 1)
            score = jnp.where(forced, FORCE_SCORE, p_sel)
            score = jnp.where(sel_i <= cur, score, -jnp.inf)
            rank = jnp.zeros((n_sel, tq), jnp.int32)
            row_grp = SUBLANES
            grp_i = lax.broadcasted_iota(jnp.int32, (row_grp, tq), 0)
            for i in range(n_sel):
                ci = score[i:i + 1, :]
                ahead = []
                for r0 in range(0, n_sel, row_grp):
                    rows = slice(r0, r0 + row_grp)
                    if r0 > i:
                        ahead.append(ci >= score[rows])
                    elif r0 + row_grp <= i:
                        ahead.append(ci > score[rows])
                    else:
                        ahead.append((ci > score[rows]) | ((ci == score[rows]) & (grp_i > i - r0)))
                rank = rank + jnp.where(jnp.concatenate(ahead, axis=0), 1, 0)
            chosen = jnp.where(rank < min(SLC_TOPK, n_sel), 1.0, 0.0)
        else:
            sel_i = lax.broadcasted_iota(jnp.int32, (n_sel, tq), 0)
            cur = (t0 + lax.broadcasted_iota(jnp.int32, (n_sel, tq), 1)) // SLC_BLOCK
            chosen = jnp.where(sel_i <= cur, 1.0, 0.0)
        for i in range(n_sel):
            ch_ref[i] = chosen[i:i + 1, :]

        m_win = {}
        acc_win = [jnp.zeros((LANES + SUM_ROWS, tq), F32) for _ in range(n_pairs)]
        for k, jw in win_tiles:
            rel_hi = (n_win - k) * tk - 1
            rel_lo = rel_hi - (tq - 1) - (tk - 1)
            exists = jnp.zeros((1, 1), F32)
            if rel_lo >= 0 and rel_hi < WINDOW:
                bias = exists
            else:
                rel = qry_t - (jw * tk + key_i)
                inside = (rel < WINDOW) if rel_lo >= 0 else (rel >= 0) if rel_hi < WINDOW else (rel >= 0) & (rel < WINDOW)
                bias = jnp.where(inside, exists, NEG_INF)
            probs, a_rows = tile_softmax(1 + k, bias,
                                         lambda hh, qh: m_win.get((hh, qh), jnp.full((1, LANES), -jnp.inf, F32)),
                                         lambda hh, qh, v: m_win.__setitem__((hh, qh), v),
                                         keys=live_keys(rel_lo + tk - 1, 0, WINDOW))
            vt = vw_ref[jw]
            acc_win = [acc_win[p] * a_rows[p] + _dot(vt, probs[p]) for p in range(n_pairs)]
        for p in range(n_pairs):
            ocw_ref[0, p] = o_cmp[p]
            ocw_ref[1, p] = normalised(acc_win[p])

        m_near = {}
        acc_near = [jnp.zeros((LANES + SUM_ROWS, tq), F32) for _ in range(n_pairs)]
        for k, jn in enumerate(near_tiles):
            jc = jnp.maximum(jn, 0)
            causal = (jn * tk + key_i) <= qry_t
            if k > 0:
                causal = causal & (jn >= 0)
            probs, a_rows = tile_softmax(1 + n_win + k, picked_bias(jc, causal),
                                         lambda hh, qh: m_near.get((hh, qh), jnp.full((1, LANES), -jnp.inf, F32)),
                                         lambda hh, qh, v: m_near.__setitem__((hh, qh), v),
                                         keys=live_keys(tk - tq + k * tk, 0, t_len))
            vt = vs_ref[jc]
            acc_near = [acc_near[p] * a_rows[p] + _dot(vt, probs[p]) for p in range(n_pairs)]

        for (hh, qh), v in m_near.items():
            m_put(hh, qh, v)
        for p in range(n_pairs):
            acc_ref[p] = acc_near[p]

    def variant(e):
        tiles_upto = (e + 1) * tq // tk
        return (max(0, n_win - tiles_upto), tiles_upto > n_near, (e + 1) * tq > SLC_TOPK * SLC_BLOCK)

    n_q = t_len // tq
    start = 0
    for e in range(1, n_q + 1):
        if e == n_q or variant(e) != variant(start):
            pl.when((qi >= start) & (qi < e))(functools.partial(static_part, *variant(start)))
            start = e
    o_cmp = [ocw_ref[0, p] for p in range(n_pairs)]
    o_win = [ocw_ref[1, p] for p in range(n_pairs)]

    def slc_step(j, carry):
        s_next = scores(ks_ref, jnp.minimum(j + 1, n_far - 1))
        probs, a_rows = tile_softmax(0, picked_bias(j), lambda hh, qh: m_ref[hh, :, qh * LANES:(qh + 1) * LANES], m_put)
        vt = vs_ref[j]
        for p in range(n_pairs):
            pv = _dot(vt, probs[p])
            s_ref[0, p] = s_next[p]
            acc_ref[p] = acc_ref[p] * a_rows[p] + pv
        return carry

    lax.fori_loop(0, n_far, slc_step, 0)
    o_slc = [normalised(acc_ref[p]) for p in range(n_pairs)]

    gates = gt_ref[0]
    gain = gain_ref[...]
    for p in range(n_pairs):
        o = jnp.zeros((LANES, tq), F32)
        for c, branch in enumerate((o_cmp[p], o_slc[p], o_win[p])):
            r = c * HPG + 2 * p
            o = o + jnp.where(slab_lo, gates[r:r + 1, :], gates[r + 1:r + 2, :]) * branch
        sq = o * o
        ms_a = jnp.sum(sq[0:hd], axis=0, keepdims=True)
        ms_b = jnp.sum(sq[hd:2 * hd], axis=0, keepdims=True)
        ms = jnp.where(slab_lo, ms_a, ms_b) * (1.0 / hd)
        o = o * lax.rsqrt(ms + EPS)
        o_ref[0, p * LANES:(p + 1) * LANES, :] = (o * gain[p * LANES:(p + 1) * LANES, :]).astype(o_ref.dtype)


def _nsa_call(qt, ksw, vt, kc, vct, gt, gain, ovt):
    b, _, t = qt.shape
    tq, tk = ATT_TQ, ATT_TK
    n_kt = t // tk
    gw = HPG * NSA_HEAD_DIM
    hd = NSA_HEAD_DIM
    k_scratch = pltpu.VMEM((n_kt, 2 * tk, LANES), BF16)
    v_scratch = pltpu.VMEM((n_kt, LANES + SUM_ROWS, 2 * tk), BF16)
    return pl.pallas_call(
        _nsa_body,
        grid=(b, NSA_KV_HEADS, t // tq),
        in_specs=[
            pl.BlockSpec((1, gw, tq), lambda bi, gi, qi: (bi, gi, qi)),
            pl.BlockSpec((1, t, 2 * KV_WIDTH), lambda bi, gi, qi: (bi, 0, 0)),
            pl.BlockSpec((1, hd, t), lambda bi, gi, qi: (bi, gi, 0)),
            pl.BlockSpec((1, hd, t), lambda bi, gi, qi: (bi, NSA_KV_HEADS + gi, 0)),
            pl.BlockSpec((1, 1) + kc.shape[2:], lambda bi, gi, qi: (bi, gi, 0, 0)),
            pl.BlockSpec((1, 1) + vct.shape[2:], lambda bi, gi, qi: (bi, gi, 0, 0)),
            pl.BlockSpec((1, LANES, tq), lambda bi, gi, qi: (bi, gi, qi)),
            pl.BlockSpec((gw, 1), lambda bi, gi, qi: (gi, 0)),
            pl.BlockSpec(ovt.shape, lambda bi, gi, qi: (0, 0)),
        ],
        out_specs=pl.BlockSpec((1, gw, tq), lambda bi, gi, qi: (bi, gi, qi)),
        out_shape=jax.ShapeDtypeStruct((b, NSA_WIDTH, t), BF16),
        scratch_shapes=[k_scratch, k_scratch, v_scratch, v_scratch,
                        pltpu.VMEM((HPG, 1, tq), F32),
                        pltpu.VMEM((HPG // 2, LANES + SUM_ROWS, tq), F32), pltpu.VMEM((1 + (WINDOW + tq) // tk + SLC_NEAR_TILES, HPG // 2, 2 * tk, tq), F32),
                        pltpu.VMEM((t // SLC_BLOCK, 1, tq), F32), pltpu.VMEM((2, HPG // 2, LANES, tq), F32)],
        compiler_params=pltpu.CompilerParams(dimension_semantics=("arbitrary", "arbitrary", "arbitrary"),
                                             vmem_limit_bytes=VMEM_LIMIT),
        name="nsa_attention",
    )(qt, ksw, vt, vt, kc, vct, gt, gain, ovt)


def _ffn_body(x_ref, oh_ref, on_ref, woh_ref, won_ref, g2_ref, wg_ref, wu_ref, wd_ref, cw_ref, gf_ref,
              out_ref, halo_ref, act_ref, *, tiles_per_seq):
    tm = x_ref.shape[0]
    x1 = x_ref[...] + _dot(oh_ref[...], woh_ref[...]) + _dot_tn(on_ref[0], won_ref[...])
    hb = _rms(x1, g2_ref[...]).astype(BF16)
    row = lax.broadcasted_iota(jnp.int32, (tm, FFN_TC), 0)

    @pl.when((pl.program_id(0) % tiles_per_seq) == 0)
    def _sequence_start():
        halo_ref[...] = jnp.zeros_like(halo_ref)

    def activation(c, gate, up):
        cols = slice(c * FFN_TC, (c + 1) * FFN_TC)
        halo = halo_ref[:, cols]
        halo_ref[:, cols] = gate[tm - SUBLANES:tm, :]
        last1 = halo[SUBLANES - 1:SUBLANES, :]
        last2 = halo[SUBLANES - 2:SUBLANES - 1, :]
        prev1 = jnp.where(row == 0, last1, pltpu.roll(gate, 1, 0))
        prev2 = jnp.where(row == 0, last2, jnp.where(row == 1, last1, pltpu.roll(gate, 2, 0)))
        cw = cw_ref[:, cols]
        y = cw[0:1, :] * prev2 + cw[1:2, :] * prev1 + cw[2:3, :] * gate + cw[3:4, :]
        return (jax.nn.silu(y) * up).astype(BF16)

    chunk = lambda w_ref, c: _dot(hb, w_ref[:, c * FFN_TC:(c + 1) * FFN_TC])
    gate_up = (chunk(wg_ref, 0), chunk(wu_ref, 0))
    for c in range(FFN_NC):
        cur = gate_up
        if c + 1 < FFN_NC:
            gate_up = (chunk(wg_ref, c + 1), chunk(wu_ref, c + 1))
        act_ref[:, c * FFN_TC:(c + 1) * FFN_TC] = activation(c, *cur)
    acc = _dot(act_ref[...], wd_ref[...])
    out_ref[...] = _rms(x1 + acc, gf_ref[...])


def _ffn_call(x2, oh, on, woh, won, g2, wg, wu, wd, cw, gf, tiles_per_seq):
    n = x2.shape[0]
    tm = FFN_TM
    row = lambda w: pl.BlockSpec((tm, w), lambda i: (i, 0))
    full = lambda a: pl.BlockSpec(a.shape, lambda i: (0,) * a.ndim, pipeline_mode=pl.Buffered(1))
    return pl.pallas_call(
        functools.partial(_ffn_body, tiles_per_seq=tiles_per_seq),
        grid=(n // tm,),
        in_specs=[row(D_MODEL), row(HG_WIDTH),
                  pl.BlockSpec((1, NSA_WIDTH, tm), lambda i: (i // tiles_per_seq, 0, i % tiles_per_seq)),
                  full(woh), full(won), full(g2),
                  full(wg), full(wu), full(wd), full(cw), full(gf)],
        out_specs=row(D_MODEL),
        out_shape=jax.ShapeDtypeStruct((n, D_MODEL), F32),
        scratch_shapes=[pltpu.VMEM((SUBLANES, D_FF), F32), pltpu.VMEM((tm, D_FF), BF16)],
        compiler_params=pltpu.CompilerParams(dimension_semantics=("arbitrary",),
                                             vmem_limit_bytes=VMEM_LIMIT),
        name="outproj_convffn",
    )(x2, oh, on, woh, won, g2, wg, wu, wd, cw, gf)


def _rope_angles(positions):
    inv_freq = ROPE_THETA ** (-jnp.arange(ROPE_HALF, dtype=F32) * 2.0 / ROPE_DIM)
    ang = positions.astype(F32)[..., None] * inv_freq
    return jnp.concatenate([jnp.cos(ang), jnp.sin(ang)], axis=-1).transpose(0, 2, 1)


def _layer(x, positions, ln1, w_in, lb, hg_gain, pe_k, pe_v, k_w1, k_w2, v_w1, v_w2, nsa_gain, w_o, ln2,
           w_gate, w_up, conv_w, conv_b, w_down, final_gain):
    b, t, d = x.shape
    n = b * t
    assert d == D_MODEL and t % FFN_TM == 0 and t % PROJ_TM == 0 and t % ATT_TQ == 0 and t % HG_TT == 0
    n_grp = t // CMP_STRIDE
    assert n_grp == LANES, "compressed-block axis is laid out on exactly one lane tile"
    n_sel = t // SLC_BLOCK
    assert n_sel % 8 == 0 and n_sel <= LANES and ATT_TK % SLC_BLOCK == 0 and ATT_TQ % ATT_TK == 0
    x2 = x.reshape(n, d)

    splits = np.cumsum([0, 4 * HG_WIDTH, NSA_WIDTH] + [KV_WIDTH] * 6 + [N_GATES])
    seg = lambda i: w_in[:, splits[i]:splits[i + 1]]
    wh = seg(0).astype(BF16)
    wk = jnp.concatenate([seg(2), seg(3), seg(4), seg(6)], axis=1).astype(BF16)
    wgate = seg(8).reshape(d, 3, NSA_KV_HEADS, HPG).transpose(0, 2, 1, 3).reshape(d, NSA_KV_HEADS, 3 * HPG)
    wgate = jnp.pad(wgate, ((0, 0), (0, 0), (0, LANES - 3 * HPG))).reshape(d, NSA_KV_HEADS * LANES)
    wt = jnp.concatenate([seg(1), seg(5), seg(7), wgate], axis=1).T.astype(BF16)
    cs = _rope_angles(positions)

    hg, kcn, vcn, ksw, qt, vt, gt = _inproj_call(x2, ln1.reshape(1, d), wh, wk, wt, cs, t // PROJ_TM)

    mst, lvl = _hgrn_tables()
    o_hg = _hgrn_call(hg.reshape(b, t, 4 * HG_WIDTH), lb.reshape(1, HG_WIDTH).astype(F32),
                      hg_gain.reshape(1, HG_WIDTH), mst, lvl)

    per_lane = lambda a: jnp.broadcast_to(a.reshape(2, CMP_STRIDE, 1, NSA_HEAD_DIM, -1),
                                          (2, CMP_STRIDE, NSA_KV_HEADS, NSA_HEAD_DIM, a.shape[-1]))
    w1_rows = lambda w1: per_lane(w1).reshape(2, CMP_STRIDE * LANES, CMP_HIDDEN).astype(BF16)
    pe_rows = lambda pe: per_lane(pe[..., None]).reshape(2, 1, CMP_STRIDE * LANES)
    zeros_w2 = jnp.zeros((CMP_HIDDEN, NSA_HEAD_DIM), F32)
    place = lambda w2: jnp.stack([jnp.concatenate([w2, zeros_w2], 1), jnp.concatenate([zeros_w2, w2], 1)])
    kc, vct = _cmp_call(kcn.reshape(b, t, KV_WIDTH), vcn.reshape(b, t, KV_WIDTH), pe_rows(pe_k), pe_rows(pe_v),
                        w1_rows(k_w1), w1_rows(v_w1),
                        place(k_w2).astype(BF16), place(v_w2).transpose(0, 2, 1).astype(BF16))

    cmp_start = np.arange(n_grp) * CMP_STRIDE
    cmp_end = cmp_start + CMP_BLOCK - 1
    sel_start = np.arange(LANES) * SLC_BLOCK
    overlap = ((cmp_start[:, None] <= sel_start[None, :] + SLC_BLOCK - 1) & (cmp_end[:, None] >= sel_start[None, :])
               & (np.arange(LANES)[None, :] < n_sel) & (np.arange(n_grp)[:, None] < n_grp - 1))
    ovt = jnp.asarray(overlap.T.astype(np.float32), BF16)
    o_nsa = _nsa_call(qt, ksw.reshape(b, t, 2 * KV_WIDTH), vt, kc, vct, gt, nsa_gain.reshape(NSA_WIDTH, 1), ovt)

    cw = jnp.concatenate([conv_w, conv_b[None, :], jnp.zeros((SUBLANES - CONV_WIDTH - 1, D_FF), F32)], axis=0)
    out = _ffn_call(x2, o_hg.reshape(n, HG_WIDTH), o_nsa,
                    w_o[:HG_WIDTH].astype(BF16), w_o[HG_WIDTH:].astype(BF16), ln2.reshape(1, d),
                    w_gate.astype(BF16), w_up.astype(BF16), w_down.astype(BF16), cw,
                    final_gain.reshape(1, d), t // FFN_TM)
    return out.reshape(b, t, d)


def kernel(x, positions, ln1_gain, w_in, hgrn_lb_param, hgrn_out_gain, cmp_pe_k, cmp_pe_v, cmp_k_w1, cmp_k_w2,
           cmp_v_w1, cmp_v_w2, nsa_out_gain, w_o, ln2_gain, ffn_w_gate, ffn_w_up, ffn_conv_w, ffn_conv_b,
           ffn_w_down, final_gain):
    depth = ln1_gain.shape[0]
    assert depth == 1, "the fused final norm assumes a single layer"
    lower_bounds = jnp.cumsum(jax.nn.softmax(hgrn_lb_param.astype(F32), axis=0), axis=0)
    l = 0
    return _layer(x, positions, ln1_gain[l], w_in[l], lower_bounds[l], hgrn_out_gain[l], cmp_pe_k[l], cmp_pe_v[l],
                  cmp_k_w1[l], cmp_k_w2[l], cmp_v_w1[l], cmp_v_w2[l], nsa_out_gain[l], w_o[l], ln2_gain[l],
                  ffn_w_gate[l], ffn_w_up[l], ffn_conv_w[l], ffn_conv_b[l], ffn_w_down[l], final_gain)
```

```python
import functools

import jax
import jax.numpy as jnp
import numpy as np
from jax import lax
from jax.experimental import pallas as pl
from jax.experimental.pallas import tpu as pltpu

F32 = jnp.float32
BF16 = jnp.bfloat16

D_MODEL = 1024
HG_HEADS = 4
HG_DK = 128
HG_DV = 128
HG_WIDTH = HG_HEADS * HG_DV
NSA_HEADS = 8
NSA_KV_HEADS = 2
NSA_HEAD_DIM = 64
HPG = NSA_HEADS // NSA_KV_HEADS
NSA_WIDTH = NSA_HEADS * NSA_HEAD_DIM
KV_WIDTH = NSA_KV_HEADS * NSA_HEAD_DIM
CMP_BLOCK = 32
CMP_STRIDE = 16
CMP_HIDDEN = 256
SLC_BLOCK = 64
SLC_TOPK = 16
WINDOW = 512
ROPE_THETA = 500000.0
ROPE_DIM = NSA_HEAD_DIM // 4
ROPE_HALF = ROPE_DIM // 2
D_FF = 2816
CONV_WIDTH = 3
EPS = 1e-6
NEG_INF = -1e30
FORCE_SCORE = 1e4
N_GATES = 3 * NSA_HEADS
LOG2_E = 1.4426950408889634

LANES = 128
SUBLANES = 8
VMEM_LIMIT = 56 * 1024 * 1024

PROJ_TM = 1024
HG_CHUNK = 128
HG_LEVELS = (16, 32, 64)
HG_DIAG = 16
HG_TT = 1024
ATT_TQ = 256
ATT_TK = 256
KEY_BLK = 128
SLC_NEAR_TILES = 1
SUM_ROWS = 16
FFN_TM = 512
FFN_TC = 256
FFN_NC = D_FF // FFN_TC


def _dot(a, b):
    return jnp.dot(a, b, preferred_element_type=F32)


def _dot_nt(a, b):
    return lax.dot_general(a, b, (((1,), (1,)), ((), ())), preferred_element_type=F32)


def _dot_tn(a, b):
    return lax.dot_general(a, b, (((0,), (0,)), ((), ())), preferred_element_type=F32)


def _split3(x):
    hi = x.astype(BF16)
    r = x - hi.astype(F32)
    mid = r.astype(BF16)
    lo = (r - mid.astype(F32)).astype(BF16)
    return hi, mid, lo


def _rms(x, gain):
    return x * lax.rsqrt(jnp.mean(x * x, axis=-1, keepdims=True) + EPS) * gain


def _inproj_body(x_ref, g_ref, wh_ref, wk_ref, wt_ref, cs_ref,
                 hq_ref, hf_ref, hi_ref, hgate_ref, kcn_ref, vcn_ref, ksw_ref, qt_ref, vt_ref, gt_ref):
    hb = _rms(x_ref[...], g_ref[...]).astype(BF16)
    hg = _dot(hb, wh_ref[...])
    for k, ref in enumerate((hq_ref, hf_ref, hi_ref, hgate_ref)):
        ref[...] = hg[:, k * HG_WIDTH:(k + 1) * HG_WIDTH]

    def rope(v, axis, cos, sin_hi, sin_lo):
        return (v * cos + pltpu.roll(v, ROPE_HALF, axis) * sin_hi
                + pltpu.roll(v, LANES - ROPE_HALF, axis) * sin_lo)

    cos = cs_ref[0, 0:ROPE_HALF, :]
    sin = cs_ref[0, ROPE_HALF:ROPE_DIM, :]
    tm = cos.shape[1]
    zero_h = jnp.zeros((ROPE_HALF, tm), F32)
    rest = NSA_HEAD_DIM - ROPE_DIM
    slab = lambda lo, hi, fill: jnp.concatenate([lo, hi, jnp.full((rest, tm), fill, F32)] * (LANES // NSA_HEAD_DIM), axis=0)
    tab_t = (slab(cos, cos, 1.0), slab(zero_h, sin, 0.0), slab(-sin, zero_h, 0.0))
    tab = tuple(a.T for a in tab_t)
    kn = _dot(hb, wk_ref[...])
    kcn_ref[...] = rope(kn[:, 0:LANES], 1, *tab)
    vcn_ref[...] = kn[:, LANES:2 * LANES]
    ksw_ref[:, 0:LANES] = rope(kn[:, 2 * LANES:3 * LANES], 1, *tab).astype(BF16)
    ksw_ref[:, LANES:2 * LANES] = rope(kn[:, 3 * LANES:4 * LANES], 1, *tab).astype(BF16)

    rt = _dot_nt(wt_ref[...], hb)
    scale = NSA_HEAD_DIM ** -0.5 * LOG2_E
    for j in range(NSA_WIDTH // LANES):
        sl = slice(j * LANES, (j + 1) * LANES)
        qt_ref[0, sl, :] = (rope(rt[sl], 0, *tab_t) * scale).astype(BF16)
    vt_ref[0] = rt[NSA_WIDTH:NSA_WIDTH + 2 * KV_WIDTH].astype(BF16)
    gt_ref[0] = jax.nn.sigmoid(rt[NSA_WIDTH + 2 * KV_WIDTH:])


def _inproj_call(x2, gain, wh, wk, wt, cs, tiles_per_seq):
    n = x2.shape[0]
    tm = PROJ_TM
    t = tiles_per_seq * tm
    b = n // t
    row = lambda w: pl.BlockSpec((tm, w), lambda i: (i, 0))
    col = lambda h: pl.BlockSpec((1, h, tm), lambda i: (i // tiles_per_seq, 0, i % tiles_per_seq))
    full = lambda a: pl.BlockSpec(a.shape, lambda i: (0, 0))
    gate_rows = NSA_KV_HEADS * LANES
    return pl.pallas_call(
        _inproj_body,
        grid=(n // tm,),
        in_specs=[row(D_MODEL), full(gain), full(wh), full(wk), full(wt),
                  col(ROPE_DIM)],
        out_specs=[row(HG_WIDTH)] * 4 + [row(KV_WIDTH), row(KV_WIDTH), row(2 * KV_WIDTH),
                   col(NSA_WIDTH), col(2 * KV_WIDTH), col(gate_rows)],
        out_shape=[jax.ShapeDtypeStruct((n, HG_WIDTH), F32)] * 4 + [
                   jax.ShapeDtypeStruct((n, KV_WIDTH), F32),
                   jax.ShapeDtypeStruct((n, KV_WIDTH), F32),
                   jax.ShapeDtypeStruct((n, 2 * KV_WIDTH), BF16),
                   jax.ShapeDtypeStruct((b, NSA_WIDTH, t), BF16),
                   jax.ShapeDtypeStruct((b, 2 * KV_WIDTH, t), BF16),
                   jax.ShapeDtypeStruct((b, gate_rows, t), F32)],
        compiler_params=pltpu.CompilerParams(dimension_semantics=("arbitrary",),
                                             vmem_limit_bytes=VMEM_LIMIT),
        name="inproj",
    )(x2, gain, wh, wk, wt, cs)


def _hgrn_tables():
    L = HG_CHUNK
    t = np.arange(L)[:, None]
    u = np.arange(L)[None, :]
    level = np.where(((t // HG_DIAG) == (u // HG_DIAG)) & (u <= t), 1, 0)
    for li, s in enumerate(HG_LEVELS):
        same = (t // (2 * s)) == (u // (2 * s))
        right = (t % (2 * s)) >= s
        level = np.where(same & right & ((u % (2 * s)) < s), li + 2, level)
    return jnp.asarray((u <= t).astype(np.float32), BF16), jnp.asarray(level, jnp.int32)


def _hgrn_body(q_ref, f_ref, i_ref, g_ref, lb_ref, gain_ref, mst_ref, lvl_ref, o_ref, st_ref):
    L = HG_CHUNK
    n_chunks = q_ref.shape[1] // L

    @pl.when(pl.program_id(1) == 0)
    def _sequence_start():
        st_ref[...] = jnp.zeros_like(st_ref)

    def chunk(c, carry):
        rows = pl.ds(pl.multiple_of(c * L, L), L)
        heads = range(HG_HEADS)
        cols = [slice(h * HG_DK, (h + 1) * HG_DK) for h in heads]
        mst = mst_ref[...]
        lvl = lvl_ref[...]
        n_lv = len(HG_LEVELS)
        row_i = lax.broadcasted_iota(jnp.int32, (L, HG_DK), 0)
        q = [q_ref[0, rows, cols[h]] for h in heads]
        vb = [i_ref[0, rows, cols[h]].astype(BF16) for h in heads]
        f = [lb_ref[:, cols[h]] + (1.0 - lb_ref[:, cols[h]]) * jax.nn.sigmoid(f_ref[0, rows, cols[h]]) for h in heads]
        k = [1.0 - f[h] for h in heads]
        parts = [_split3(jnp.log2(f[h])) for h in heads]
        e_full = [(_dot(mst, parts[h][0]) + _dot(mst, parts[h][1])) + _dot(mst, parts[h][2]) for h in heads]
        b_last = [e_full[h][L - 1:L, :] for h in heads]

        def rel_to(b, blk, off):
            refs = []
            for r0 in range(0, L, blk):
                r = r0 + off - 1
                ref = b[r:r + 1, :] if r >= 0 else jnp.zeros((1, HG_DK), F32)
                refs.append(jnp.broadcast_to(ref, (blk, HG_DK)))
            return b - jnp.concatenate(refs, axis=0)

        def level_sums(b):
            out = [rel_to(b, HG_DIAG, 0)]
            for s_half in HG_LEVELS:
                d = rel_to(b, 2 * s_half, s_half)
                out.append(jnp.where((row_i % (2 * s_half)) >= s_half, d, -d))
            return out

        e = [level_sums(e_full[h]) for h in heads]
        wq = [[jnp.exp2(e[h][l]) for l in range(n_lv + 1)] for h in heads]
        wk = [[jnp.exp2(-e[h][0])] + wq[h][1:] for h in heads]
        prod = [[_dot_nt((q[h] * wq[h][l]).astype(BF16), (k[h] * wk[h][l]).astype(BF16)) for l in range(n_lv + 1)]
                for h in heads]
        st = [st_ref[h] for h in heads]
        inter = [_dot_nt((q[h] * jnp.exp2(e_full[h])).astype(BF16), st[h].astype(BF16)) for h in heads]
        k_dec = [(k[h] * jnp.exp2(b_last[h] - e_full[h])).astype(BF16) for h in heads]
        upd = [_dot_tn(vb[h], k_dec[h]) for h in heads]
        for h in heads:
            st_ref[h] = st[h] * jnp.exp2(b_last[h]) + upd[h]
        a = []
        for h in heads:
            ah = jnp.where(lvl == 1, prod[h][0], 0.0)
            for l in range(1, n_lv + 1):
                ah = jnp.where(lvl == l + 1, prod[h][l], ah)
            a.append(ah.astype(BF16))
        o = [_dot(a[h], vb[h]) + inter[h] for h in heads]
        for h in heads:
            oh = o[h] * lax.rsqrt(jnp.mean(o[h] * o[h], axis=-1, keepdims=True) + EPS) * gain_ref[:, cols[h]]
            o_ref[0, rows, cols[h]] = (oh * jax.nn.silu(g_ref[0, rows, cols[h]])).astype(o_ref.dtype)
        return carry

    lax.fori_loop(0, n_chunks, chunk, 0, unroll=True)


def _hgrn_call(hq, hf, hi, hgate, lb, gain, mst, lvl):
    b, t, _ = hq.shape
    tt = HG_TT
    blk = pl.BlockSpec((1, tt, HG_WIDTH), lambda bi, ti: (bi, ti, 0))
    full = lambda a: pl.BlockSpec(a.shape, lambda bi, ti: (0, 0))
    return pl.pallas_call(
        _hgrn_body,
        grid=(b, t // tt),
        in_specs=[blk, blk, blk, blk, full(lb), full(gain), full(mst), full(lvl)],
        out_specs=pl.BlockSpec((1, tt, HG_WIDTH), lambda bi, ti: (bi, ti, 0)),
        out_shape=jax.ShapeDtypeStruct((b, t, HG_WIDTH), BF16),
        scratch_shapes=[pltpu.VMEM((HG_HEADS, HG_DV, HG_DK), F32)],
        compiler_params=pltpu.CompilerParams(dimension_semantics=("arbitrary", "arbitrary"),
                                             vmem_limit_bytes=VMEM_LIMIT),
        name="hgrn2",
    )(hq, hf, hi, hgate, lb, gain, mst, lvl)


def _cmp_body(kcn_ref, vcn_ref, pek_ref, pev_ref, w1k_ref, w1v_ref, w2k_ref, w2v_ref, kc_ref, vc_ref):
    nb = kcn_ref.shape[1] // CMP_STRIDE
    lane_grp = (lax.broadcasted_iota(jnp.int32, (nb, CMP_STRIDE * LANES), 1) // NSA_HEAD_DIM) % NSA_KV_HEADS

    def hidden(src_ref, pe_ref, w1_ref):
        x = jnp.concatenate([src_ref[0, pl.ds(l, nb, stride=CMP_STRIDE), :]
                             for l in range(CMP_STRIDE)], axis=1)
        halves = [x + pe_ref[i] for i in range(2)]
        out = []
        for g in range(NSA_KV_HEADS):
            u, v = (_dot(jnp.where(lane_grp == g, halves[i], 0.0).astype(BF16), w1_ref[i]) for i in range(2))
            out.append(jax.nn.silu(u + pltpu.roll(v, nb - 1, 0)).astype(BF16))
        return out

    hk = hidden(kcn_ref, pek_ref, w1k_ref)
    hv = hidden(vcn_ref, pev_ref, w1v_ref)
    for g in range(NSA_KV_HEADS):
        kc_ref[0, g, 0:nb, :] = _dot(hk[g], w2k_ref[0]).astype(kc_ref.dtype)
        kc_ref[0, g, nb:2 * nb, :] = _dot(hk[g], w2k_ref[1]).astype(kc_ref.dtype)
        vc_ref[0, g, :, 0:nb] = _dot_nt(w2v_ref[0], hv[g]).astype(vc_ref.dtype)
        vc_ref[0, g, :, nb:2 * nb] = _dot_nt(w2v_ref[1], hv[g]).astype(vc_ref.dtype)


def _cmp_call(kcn, vcn, pek, pev, w1k, w1v, w2k, w2v):
    b, t, w = kcn.shape
    nb = t // CMP_STRIDE
    full = lambda a: pl.BlockSpec(a.shape, lambda bi: (0,) * a.ndim)
    out = lambda r, c: pl.BlockSpec((1, NSA_KV_HEADS, r, c), lambda bi: (bi, 0, 0, 0))
    return pl.pallas_call(
        _cmp_body,
        grid=(b,),
        in_specs=[pl.BlockSpec((1, t, w), lambda bi: (bi, 0, 0)), pl.BlockSpec((1, t, w), lambda bi: (bi, 0, 0)),
                  full(pek), full(pev), full(w1k), full(w1v), full(w2k), full(w2v)],
        out_specs=[out(2 * nb, LANES), out(LANES, 2 * nb)],
        out_shape=[jax.ShapeDtypeStruct((b, NSA_KV_HEADS, 2 * nb, LANES), BF16),
                   jax.ShapeDtypeStruct((b, NSA_KV_HEADS, LANES, 2 * nb), BF16)],
        compiler_params=pltpu.CompilerParams(dimension_semantics=("arbitrary",),
                                             vmem_limit_bytes=VMEM_LIMIT),
        name="nsa_compress",
    )(kcn, vcn, pek, pev, w1k, w1v, w2k, w2v)


def _nsa_body(qt_ref, ksw_ref, vst_ref, vwt_ref, kc_ref, vct_ref, gt_ref, gain_ref, ovt_ref, o_ref,
              ks_ref, kw_ref, vs_ref, vw_ref, m_ref, acc_ref, s_ref, ch_ref, ocw_ref):
    g = pl.program_id(1)
    qi = pl.program_id(2)
    tq = ATT_TQ
    tk = ATT_TK
    t_len = ksw_ref.shape[1]
    n_kt = t_len // tk
    n_pairs = HPG // 2
    hd = NSA_HEAD_DIM

    @pl.when(qi == 0)
    def _build_kv():
        lane = lax.broadcasted_iota(jnp.int32, (tk, LANES), 1)
        lo_lane = lane < hd
        keep = (lane // hd) == g

        def build_k(src_col, dst_ref):
            for j in range(n_kt):
                x = ksw_ref[0, j * tk:(j + 1) * tk, src_col * LANES:(src_col + 1) * LANES].astype(F32)
                dup = jnp.where(keep, x, pltpu.roll(x, hd, 1))
                dst_ref[j, 0:tk, :] = jnp.where(lo_lane, dup, 0.0).astype(BF16)
                dst_ref[j, tk:2 * tk, :] = jnp.where(lo_lane, 0.0, dup).astype(BF16)

        def build_v(src_ref, dst_ref):
            zero = jnp.zeros((hd, tk), BF16)
            row = lax.broadcasted_iota(jnp.int32, (SUM_ROWS, 2 * tk), 0)
            col = lax.broadcasted_iota(jnp.int32, (SUM_ROWS, 2 * tk), 1)
            ones_rows = jnp.where(((row == 0) & (col < tk)) | ((row == 1) & (col >= tk)), 1.0, 0.0).astype(BF16)
            for j in range(n_kt):
                x = src_ref[0, :, j * tk:(j + 1) * tk]
                dst_ref[j, 0:hd, 0:tk] = x
                dst_ref[j, 0:hd, tk:2 * tk] = zero
                dst_ref[j, hd:2 * hd, 0:tk] = zero
                dst_ref[j, hd:2 * hd, tk:2 * tk] = x
                dst_ref[j, 2 * hd:2 * hd + SUM_ROWS, :] = ones_rows

        build_k(0, ks_ref)
        build_k(1, kw_ref)
        build_v(vst_ref, vs_ref)
        build_v(vwt_ref, vw_ref)

    t0 = qi * tq
    key_i = lax.broadcasted_iota(jnp.int32, (tk, tq), 0)
    qry_t = t0 + lax.broadcasted_iota(jnp.int32, (tk, tq), 1)
    slab_lo = lax.broadcasted_iota(jnp.int32, (LANES, tq), 0) < hd
    acc_row = lax.broadcasted_iota(jnp.int32, (LANES + SUM_ROWS, tq), 0)
    slab_a = (acc_row < hd) | (acc_row == LANES)
    q_pairs = [qt_ref[0, p * LANES:(p + 1) * LANES, :] for p in range(n_pairs)]

    last = (t0 + tq - 1) // tk

    def scores(k_ref, j):
        kt = k_ref[j]
        return [_dot(kt, q_pairs[p]) for p in range(n_pairs)]

    def live_keys(rel_at_origin, lower, upper):
        out = []
        for qh in range(tq // LANES):
            live = [kb for kb in range(tk // KEY_BLK)
                    if rel_at_origin + qh * LANES + LANES - 1 - kb * KEY_BLK >= lower
                    and rel_at_origin + qh * LANES - (kb * KEY_BLK + KEY_BLK - 1) < upper]
            out.append((min(live) * KEY_BLK, (max(live) + 1) * KEY_BLK))
        return out

    def tile_softmax(slot, bias, m_get, m_put, keys=None):
        alphas = {}
        probs = {}
        for hh in range(HPG):
            p, h = divmod(hh, 2)
            a_parts = []
            p_parts = []
            for qh in range(tq // LANES):
                ql = slice(qh * LANES, (qh + 1) * LANES)
                k_lo, k_hi = keys[qh] if keys is not None else (0, tk)
                sh = (s_ref[slot, p, h * tk + k_lo:h * tk + k_hi, ql]
                      + (bias if bias.shape == (1, 1) else bias[k_lo:k_hi, ql]))
                m_prev = m_get(hh, qh)
                m_new = jnp.maximum(m_prev, jnp.max(sh, axis=0, keepdims=True))
                m_put(hh, qh, m_new)
                piece = [jnp.zeros((k_lo, LANES), BF16)] if k_lo else []
                piece.append(jnp.exp2(sh - m_new).astype(BF16))
                if k_hi < tk:
                    piece.append(jnp.zeros((tk - k_hi, LANES), BF16))
                p_parts.append(jnp.concatenate(piece, axis=0) if len(piece) > 1 else piece[0])
                a_parts.append(jnp.exp2(m_prev - m_new))
            alphas[hh] = jnp.concatenate(a_parts, axis=1)
            probs[hh] = jnp.concatenate(p_parts, axis=1)
        return ([jnp.concatenate([probs[2 * p], probs[2 * p + 1]], axis=0) for p in range(n_pairs)],
                [jnp.where(slab_a, alphas[2 * p], alphas[2 * p + 1]) for p in range(n_pairs)])

    def normalised(acc):
        inv = jnp.where(slab_lo, 1.0 / acc[LANES:LANES + 1, :], 1.0 / acc[LANES + 1:LANES + 2, :])
        return acc[0:LANES] * inv

    def picked_bias(j, also=None):
        per_tile = tk // SLC_BLOCK
        picked = jnp.concatenate([jnp.broadcast_to(ch_ref[j * per_tile + i], (SLC_BLOCK, tq)) for i in range(per_tile)],
                                 axis=0) > 0.5
        return jnp.where(picked if also is None else picked & also, 0.0, NEG_INF)

    def m_put(hh, qh, v):
        m_ref[hh, :, qh * LANES:(qh + 1) * LANES] = v

    n_win = (WINDOW + tq) // tk
    n_near = SLC_NEAR_TILES
    n_far = jnp.maximum(last + 1 - n_near, 0)

    def static_part(first_k, has_far, need_topk):
        n_cmp_pad = kc_ref.shape[2] // 2
        blk_i = lax.broadcasted_iota(jnp.int32, (n_cmp_pad, tq), 0)
        blk_t = t0 + lax.broadcasted_iota(jnp.int32, (n_cmp_pad, tq), 1)
        cmp_ok = (blk_i * CMP_STRIDE + (CMP_BLOCK - 1)) <= blk_t
        kc = kc_ref[0, 0]
        vct = vct_ref[0, 0]
        s_cmp = [_dot(kc, q_pairs[p]) for p in range(n_pairs)]
        win_tiles = [(k, last - (n_win - 1) + k) for k in range(first_k, n_win)]
        near_tiles = [last - k for k in range(n_near)]
        up_front = ([(0, ks_ref, 0)] if has_far else []) + [(1 + k, kw_ref, jw) for k, jw in win_tiles]
        up_front += [(1 + n_win + k, ks_ref, jnp.maximum(jn, 0)) for k, jn in enumerate(near_tiles)]
        for slot, k_ref, j0 in up_front:
            s_first = scores(k_ref, j0)
            for p in range(n_pairs):
                s_ref[slot, p] = s_first[p]
        p_sum = jnp.zeros((n_cmp_pad, tq), F32)
        p_cmp = []
        for p in range(n_pairs):
            probs = []
            for h in range(2):
                sh = jnp.where(cmp_ok, s_cmp[p][h * n_cmp_pad:(h + 1) * n_cmp_pad], NEG_INF)
                mh = jnp.max(sh, axis=0, keepdims=True)
                eh = jnp.where(cmp_ok, jnp.exp2(sh - mh), 0.0)
                den = jnp.sum(eh, axis=0, keepdims=True)
                ph = eh / jnp.where(den > 0.0, den, 1.0)
                p_sum = p_sum + ph
                probs.append(ph.astype(BF16))
            p_cmp.append(jnp.concatenate(probs, axis=0))
        o_cmp = [_dot(vct, p_cmp[p]) for p in range(n_pairs)]

        n_sel = t_len // SLC_BLOCK
        if need_topk:
            hi, mid, lo = _split3(p_sum)
            ovt = ovt_ref[...]
            p_sel = ((_dot(ovt, hi) + _dot(ovt, mid)) + _dot(ovt, lo))[0:n_sel]
            sel_i = lax.broadcasted_iota(jnp.int32, (n_sel, tq), 0)
            cur = (t0 + lax.broadcasted_iota(jnp.int32, (n_sel, tq), 1)) // SLC_BLOCK
            forced = (sel_i == 0) | (sel_i == cur) | (sel_i == cur - 1)
            score = jnp.where(forced, FORCE_SCORE, p_sel)
            score = jnp.where(sel_i <= cur, score, -jnp.inf)
            rank = jnp.zeros((n_sel, tq), jnp.int32)
            row_grp = SUBLANES
            grp_i = lax.broadcasted_iota(jnp.int32, (row_grp, tq), 0)
            for i in range(n_sel):
                ci = score[i:i + 1, :]
                ahead = []
                for r0 in range(0, n_sel, row_grp):
                    rows = slice(r0, r0 + row_grp)
                    if r0 > i:
                        ahead.append(ci >= score[rows])
                    elif r0 + row_grp <= i:
                        ahead.append(ci > score[rows])
                    else:
                        ahead.append((ci > score[rows]) | ((ci == score[rows]) & (grp_i > i - r0)))
                rank = rank + jnp.where(jnp.concatenate(ahead, axis=0), 1, 0)
            chosen = jnp.where(rank < min(SLC_TOPK, n_sel), 1.0, 0.0)
        else:
            sel_i = lax.broadcasted_iota(jnp.int32, (n_sel, tq), 0)
            cur = (t0 + lax.broadcasted_iota(jnp.int32, (n_sel, tq), 1)) // SLC_BLOCK
            chosen = jnp.where(sel_i <= cur, 1.0, 0.0)
        for i in range(n_sel):
            ch_ref[i] = chosen[i:i + 1, :]

        m_win = {}
        acc_win = [jnp.zeros((LANES + SUM_ROWS, tq), F32) for _ in range(n_pairs)]
        for k, jw in win_tiles:
            rel_hi = (n_win - k) * tk - 1
            rel_lo = rel_hi - (tq - 1) - (tk - 1)
            exists = jnp.zeros((1, 1), F32)
            if rel_lo >= 0 and rel_hi < WINDOW:
                bias = exists
            else:
                rel = qry_t - (jw * tk + key_i)
                inside = (rel < WINDOW) if rel_lo >= 0 else (rel >= 0) if rel_hi < WINDOW else (rel >= 0) & (rel < WINDOW)
                bias = jnp.where(inside, exists, NEG_INF)
            probs, a_rows = tile_softmax(1 + k, bias,
                                         lambda hh, qh: m_win.get((hh, qh), jnp.full((1, LANES), -jnp.inf, F32)),
                                         lambda hh, qh, v: m_win.__setitem__((hh, qh), v),
                                         keys=live_keys(rel_lo + tk - 1, 0, WINDOW))
            vt = vw_ref[jw]
            acc_win = [acc_win[p] * a_rows[p] + _dot(vt, probs[p]) for p in range(n_pairs)]
        for p in range(n_pairs):
            ocw_ref[0, p] = o_cmp[p]
            ocw_ref[1, p] = normalised(acc_win[p])

        m_near = {}
        acc_near = [jnp.zeros((LANES + SUM_ROWS, tq), F32) for _ in range(n_pairs)]
        for k, jn in enumerate(near_tiles):
            jc = jnp.maximum(jn, 0)
            causal = (jn * tk + key_i) <= qry_t
            if k > 0:
                causal = causal & (jn >= 0)
            probs, a_rows = tile_softmax(1 + n_win + k, picked_bias(jc, causal),
                                         lambda hh, qh: m_near.get((hh, qh), jnp.full((1, LANES), -jnp.inf, F32)),
                                         lambda hh, qh, v: m_near.__setitem__((hh, qh), v),
                                         keys=live_keys(tk - tq + k * tk, 0, t_len))
            vt = vs_ref[jc]
            acc_near = [acc_near[p] * a_rows[p] + _dot(vt, probs[p]) for p in range(n_pairs)]

        for (hh, qh), v in m_near.items():
            m_put(hh, qh, v)
        for p in range(n_pairs):
            acc_ref[p] = acc_near[p]

    def variant(e):
        tiles_upto = (e + 1) * tq // tk
        return (max(0, n_win - tiles_upto), tiles_upto > n_near, (e + 1) * tq > SLC_TOPK * SLC_BLOCK)

    n_q = t_len // tq
    start = 0
    for e in range(1, n_q + 1):
        if e == n_q or variant(e) != variant(start):
            pl.when((qi >= start) & (qi < e))(functools.partial(static_part, *variant(start)))
            start = e
    o_cmp = [ocw_ref[0, p] for p in range(n_pairs)]
    o_win = [ocw_ref[1, p] for p in range(n_pairs)]

    def slc_step(j, carry):
        s_next = scores(ks_ref, jnp.minimum(j + 1, n_far - 1))
        probs, a_rows = tile_softmax(0, picked_bias(j), lambda hh, qh: m_ref[hh, :, qh * LANES:(qh + 1) * LANES], m_put)
        vt = vs_ref[j]
        for p in range(n_pairs):
            pv = _dot(vt, probs[p])
            s_ref[0, p] = s_next[p]
            acc_ref[p] = acc_ref[p] * a_rows[p] + pv
        return carry

    lax.fori_loop(0, n_far, slc_step, 0)
    o_slc = [normalised(acc_ref[p]) for p in range(n_pairs)]

    gates = gt_ref[0]
    gain = gain_ref[...]
    for p in range(n_pairs):
        o = jnp.zeros((LANES, tq), F32)
        for c, branch in enumerate((o_cmp[p], o_slc[p], o_win[p])):
            r = c * HPG + 2 * p
            o = o + jnp.where(slab_lo, gates[r:r + 1, :], gates[r + 1:r + 2, :]) * branch
        sq = o * o
        ms_a = jnp.sum(sq[0:hd], axis=0, keepdims=True)
        ms_b = jnp.sum(sq[hd:2 * hd], axis=0, keepdims=True)
        ms = jnp.where(slab_lo, ms_a, ms_b) * (1.0 / hd)
        o = o * lax.rsqrt(ms + EPS)
        o_ref[0, p * LANES:(p + 1) * LANES, :] = (o * gain[p * LANES:(p + 1) * LANES, :]).astype(o_ref.dtype)


def _nsa_call(qt, ksw, vt, kc, vct, gt, gain, ovt):
    b, _, t = qt.shape
    tq, tk = ATT_TQ, ATT_TK
    n_kt = t // tk
    gw = HPG * NSA_HEAD_DIM
    hd = NSA_HEAD_DIM
    k_scratch = pltpu.VMEM((n_kt, 2 * tk, LANES), BF16)
    v_scratch = pltpu.VMEM((n_kt, LANES + SUM_ROWS, 2 * tk), BF16)
    return pl.pallas_call(
        _nsa_body,
        grid=(b, NSA_KV_HEADS, t // tq),
        in_specs=[
            pl.BlockSpec((1, gw, tq), lambda bi, gi, qi: (bi, gi, qi)),
            pl.BlockSpec((1, t, 2 * KV_WIDTH), lambda bi, gi, qi: (bi, 0, 0)),
            pl.BlockSpec((1, hd, t), lambda bi, gi, qi: (bi, gi, 0)),
            pl.BlockSpec((1, hd, t), lambda bi, gi, qi: (bi, NSA_KV_HEADS + gi, 0)),
            pl.BlockSpec((1, 1) + kc.shape[2:], lambda bi, gi, qi: (bi, gi, 0, 0)),
            pl.BlockSpec((1, 1) + vct.shape[2:], lambda bi, gi, qi: (bi, gi, 0, 0)),
            pl.BlockSpec((1, LANES, tq), lambda bi, gi, qi: (bi, gi, qi)),
            pl.BlockSpec((gw, 1), lambda bi, gi, qi: (gi, 0)),
            pl.BlockSpec(ovt.shape, lambda bi, gi, qi: (0, 0)),
        ],
        out_specs=pl.BlockSpec((1, gw, tq), lambda bi, gi, qi: (bi, gi, qi)),
        out_shape=jax.ShapeDtypeStruct((b, NSA_WIDTH, t), BF16),
        scratch_shapes=[k_scratch, k_scratch, v_scratch, v_scratch,
                        pltpu.VMEM((HPG, 1, tq), F32),
                        pltpu.VMEM((HPG // 2, LANES + SUM_ROWS, tq), F32), pltpu.VMEM((1 + (WINDOW + tq) // tk + SLC_NEAR_TILES, HPG // 2, 2 * tk, tq), F32),
                        pltpu.VMEM((t // SLC_BLOCK, 1, tq), F32), pltpu.VMEM((2, HPG // 2, LANES, tq), F32)],
        compiler_params=pltpu.CompilerParams(dimension_semantics=("arbitrary", "arbitrary", "arbitrary"),
                                             vmem_limit_bytes=VMEM_LIMIT),
        name="nsa_attention",
    )(qt, ksw, vt, vt, kc, vct, gt, gain, ovt)


def _ffn_body(x_ref, oh_ref, on_ref, woh_ref, won_ref, g2_ref, wg_ref, wu_ref, wd_ref, cw_ref, gf_ref,
              out_ref, halo_ref, act_ref, *, tiles_per_seq):
    tm = x_ref.shape[0]
    x1 = x_ref[...] + _dot(oh_ref[...], woh_ref[...]) + _dot_tn(on_ref[0], won_ref[...])
    hb = _rms(x1, g2_ref[...]).astype(BF16)
    row = lax.broadcasted_iota(jnp.int32, (tm, FFN_TC), 0)

    @pl.when((pl.program_id(0) % tiles_per_seq) == 0)
    def _sequence_start():
        halo_ref[...] = jnp.zeros_like(halo_ref)

    def activation(c, gate, up):
        cols = slice(c * FFN_TC, (c + 1) * FFN_TC)
        halo = halo_ref[:, cols]
        halo_ref[:, cols] = gate[tm - SUBLANES:tm, :]
        last1 = halo[SUBLANES - 1:SUBLANES, :]
        last2 = halo[SUBLANES - 2:SUBLANES - 1, :]
        prev1 = jnp.where(row == 0, last1, pltpu.roll(gate, 1, 0))
        prev2 = jnp.where(row == 0, last2, jnp.where(row == 1, last1, pltpu.roll(gate, 2, 0)))
        cw = cw_ref[:, cols]
        y = cw[0:1, :] * prev2 + cw[1:2, :] * prev1 + cw[2:3, :] * gate + cw[3:4, :]
        return (jax.nn.silu(y) * up).astype(BF16)

    chunk = lambda w_ref, c: _dot(hb, w_ref[:, c * FFN_TC:(c + 1) * FFN_TC])
    gate_up = (chunk(wg_ref, 0), chunk(wu_ref, 0))
    for c in range(FFN_NC):
        cur = gate_up
        if c + 1 < FFN_NC:
            gate_up = (chunk(wg_ref, c + 1), chunk(wu_ref, c + 1))
        act_ref[:, c * FFN_TC:(c + 1) * FFN_TC] = activation(c, *cur)
    acc = _dot(act_ref[...], wd_ref[...])
    out_ref[...] = _rms(x1 + acc, gf_ref[...])


def _ffn_call(x2, oh, on, woh, won, g2, wg, wu, wd, cw, gf, tiles_per_seq):
    n = x2.shape[0]
    tm = FFN_TM
    row = lambda w: pl.BlockSpec((tm, w), lambda i: (i, 0))
    full = lambda a: pl.BlockSpec(a.shape, lambda i: (0,) * a.ndim, pipeline_mode=pl.Buffered(1))
    return pl.pallas_call(
        functools.partial(_ffn_body, tiles_per_seq=tiles_per_seq),
        grid=(n // tm,),
        in_specs=[row(D_MODEL), row(HG_WIDTH),
                  pl.BlockSpec((1, NSA_WIDTH, tm), lambda i: (i // tiles_per_seq, 0, i % tiles_per_seq)),
                  full(woh), full(won), full(g2),
                  full(wg), full(wu), full(wd), full(cw), full(gf)],
        out_specs=row(D_MODEL),
        out_shape=jax.ShapeDtypeStruct((n, D_MODEL), F32),
        scratch_shapes=[pltpu.VMEM((SUBLANES, D_FF), F32), pltpu.VMEM((tm, D_FF), BF16)],
        compiler_params=pltpu.CompilerParams(dimension_semantics=("arbitrary",),
                                             vmem_limit_bytes=VMEM_LIMIT),
        name="outproj_convffn",
    )(x2, oh, on, woh, won, g2, wg, wu, wd, cw, gf)


def _rope_angles(positions):
    inv_freq = ROPE_THETA ** (-jnp.arange(ROPE_HALF, dtype=F32) * 2.0 / ROPE_DIM)
    ang = positions.astype(F32)[..., None] * inv_freq
    return jnp.concatenate([jnp.cos(ang), jnp.sin(ang)], axis=-1).transpose(0, 2, 1)


def _layer(x, positions, ln1, w_in, lb, hg_gain, pe_k, pe_v, k_w1, k_w2, v_w1, v_w2, nsa_gain, w_o, ln2,
           w_gate, w_up, conv_w, conv_b, w_down, final_gain):
    b, t, d = x.shape
    n = b * t
    assert d == D_MODEL and t % FFN_TM == 0 and t % PROJ_TM == 0 and t % ATT_TQ == 0 and t % HG_TT == 0
    n_grp = t // CMP_STRIDE
    assert n_grp == LANES, "compressed-block axis is laid out on exactly one lane tile"
    n_sel = t // SLC_BLOCK
    assert n_sel % 8 == 0 and n_sel <= LANES and ATT_TK % SLC_BLOCK == 0 and ATT_TQ % ATT_TK == 0
    x2 = x.reshape(n, d)

    splits = np.cumsum([0, 4 * HG_WIDTH, NSA_WIDTH] + [KV_WIDTH] * 6 + [N_GATES])
    seg = lambda i: w_in[:, splits[i]:splits[i + 1]]
    wh = seg(0).astype(BF16)
    wk = jnp.concatenate([seg(2), seg(3), seg(4), seg(6)], axis=1).astype(BF16)
    wgate = seg(8).reshape(d, 3, NSA_KV_HEADS, HPG).transpose(0, 2, 1, 3).reshape(d, NSA_KV_HEADS, 3 * HPG)
    wgate = jnp.pad(wgate, ((0, 0), (0, 0), (0, LANES - 3 * HPG))).reshape(d, NSA_KV_HEADS * LANES)
    wt = jnp.concatenate([seg(1), seg(5), seg(7), wgate], axis=1).T.astype(BF16)
    cs = _rope_angles(positions)

    hq, hf, hi, hgate, kcn, vcn, ksw, qt, vt, gt = _inproj_call(x2, ln1.reshape(1, d), wh, wk, wt, cs, t // PROJ_TM)

    mst, lvl = _hgrn_tables()
    o_hg = _hgrn_call(*(a.reshape(b, t, HG_WIDTH) for a in (hq, hf, hi, hgate)), lb.reshape(1, HG_WIDTH).astype(F32),
                      hg_gain.reshape(1, HG_WIDTH), mst, lvl)

    per_lane = lambda a: jnp.broadcast_to(a.reshape(2, CMP_STRIDE, 1, NSA_HEAD_DIM, -1),
                                          (2, CMP_STRIDE, NSA_KV_HEADS, NSA_HEAD_DIM, a.shape[-1]))
    w1_rows = lambda w1: per_lane(w1).reshape(2, CMP_STRIDE * LANES, CMP_HIDDEN).astype(BF16)
    pe_rows = lambda pe: per_lane(pe[..., None]).reshape(2, 1, CMP_STRIDE * LANES)
    zeros_w2 = jnp.zeros((CMP_HIDDEN, NSA_HEAD_DIM), F32)
    place = lambda w2: jnp.stack([jnp.concatenate([w2, zeros_w2], 1), jnp.concatenate([zeros_w2, w2], 1)])
    kc, vct = _cmp_call(kcn.reshape(b, t, KV_WIDTH), vcn.reshape(b, t, KV_WIDTH), pe_rows(pe_k), pe_rows(pe_v),
                        w1_rows(k_w1), w1_rows(v_w1),
                        place(k_w2).astype(BF16), place(v_w2).transpose(0, 2, 1).astype(BF16))

    cmp_start = np.arange(n_grp) * CMP_STRIDE
    cmp_end = cmp_start + CMP_BLOCK - 1
    sel_start = np.arange(LANES) * SLC_BLOCK
    overlap = ((cmp_start[:, None] <= sel_start[None, :] + SLC_BLOCK - 1) & (cmp_end[:, None] >= sel_start[None, :])
               & (np.arange(LANES)[None, :] < n_sel) & (np.arange(n_grp)[:, None] < n_grp - 1))
    ovt = jnp.asarray(overlap.T.astype(np.float32), BF16)
    o_nsa = _nsa_call(qt, ksw.reshape(b, t, 2 * KV_WIDTH), vt, kc, vct, gt, nsa_gain.reshape(NSA_WIDTH, 1), ovt)

    cw = jnp.concatenate([conv_w, conv_b[None, :], jnp.zeros((SUBLANES - CONV_WIDTH - 1, D_FF), F32)], axis=0)
    out = _ffn_call(x2, o_hg.reshape(n, HG_WIDTH), o_nsa,
                    w_o[:HG_WIDTH].astype(BF16), w_o[HG_WIDTH:].astype(BF16), ln2.reshape(1, d),
                    w_gate.astype(BF16), w_up.astype(BF16), w_down.astype(BF16), cw,
                    final_gain.reshape(1, d), t // FFN_TM)
    return out.reshape(b, t, d)


def kernel(x, positions, ln1_gain, w_in, hgrn_lb_param, hgrn_out_gain, cmp_pe_k, cmp_pe_v, cmp_k_w1, cmp_k_w2,
           cmp_v_w1, cmp_v_w2, nsa_out_gain, w_o, ln2_gain, ffn_w_gate, ffn_w_up, ffn_conv_w, ffn_conv_b,
           ffn_w_down, final_gain):
    depth = ln1_gain.shape[0]
    assert depth == 1, "the fused final norm assumes a single layer"
    lower_bounds = jnp.cumsum(jax.nn.softmax(hgrn_lb_param.astype(F32), axis=0), axis=0)
    l = 0
    return _layer(x, positions, ln1_gain[l], w_in[l], lower_bounds[l], hgrn_out_gain[l], cmp_pe_k[l], cmp_pe_v[l],
                  cmp_k_w1[l], cmp_k_w2[l], cmp_v_w1[l], cmp_v_w2[l], nsa_out_gain[l], w_o[l], ln2_gain[l],
                  ffn_w_gate[l], ffn_w_up[l], ffn_conv_w[l], ffn_conv_b[l], ffn_w_down[l], final_gain)
```

```python
import functools

import jax
import jax.numpy as jnp
import numpy as np
from jax import lax
from jax.experimental import pallas as pl
from jax.experimental.pallas import tpu as pltpu

F32 = jnp.float32
BF16 = jnp.bfloat16

D_MODEL = 1024
HG_HEADS = 4
HG_DK = 128
HG_DV = 128
HG_WIDTH = HG_HEADS * HG_DV
NSA_HEADS = 8
NSA_KV_HEADS = 2
NSA_HEAD_DIM = 64
HPG = NSA_HEADS // NSA_KV_HEADS
NSA_WIDTH = NSA_HEADS * NSA_HEAD_DIM
KV_WIDTH = NSA_KV_HEADS * NSA_HEAD_DIM
CMP_BLOCK = 32
CMP_STRIDE = 16
CMP_HIDDEN = 256
SLC_BLOCK = 64
SLC_TOPK = 16
WINDOW = 512
ROPE_THETA = 500000.0
ROPE_DIM = NSA_HEAD_DIM // 4
ROPE_HALF = ROPE_DIM // 2
D_FF = 2816
CONV_WIDTH = 3
EPS = 1e-6
NEG_INF = -1e30
FORCE_SCORE = 1e4
N_GATES = 3 * NSA_HEADS
LOG2_E = 1.4426950408889634

LANES = 128
SUBLANES = 8
VMEM_LIMIT = 56 * 1024 * 1024

PROJ_TM = 1024
HG_CHUNK = 128
HG_LEVELS = (16, 32, 64)
HG_DIAG = 16
HG_TT = 1024
ATT_TQ = 256
ATT_TK = 256
KEY_BLK = 128
SLC_NEAR_TILES = 1
SUM_ROWS = 16
FFN_TM = 512
FFN_TC = 256
FFN_NC = D_FF // FFN_TC


def _dot(a, b):
    return jnp.dot(a, b, preferred_element_type=F32)


def _dot_nt(a, b):
    return lax.dot_general(a, b, (((1,), (1,)), ((), ())), preferred_element_type=F32)


def _dot_tn(a, b):
    return lax.dot_general(a, b, (((0,), (0,)), ((), ())), preferred_element_type=F32)


def _split3(x):
    hi = x.astype(BF16)
    r = x - hi.astype(F32)
    mid = r.astype(BF16)
    lo = (r - mid.astype(F32)).astype(BF16)
    return hi, mid, lo


def _rms(x, gain):
    return x * lax.rsqrt(jnp.mean(x * x, axis=-1, keepdims=True) + EPS) * gain


def _inproj_body(x_ref, g_ref, wh_ref, wk_ref, wt_ref, cs_ref,
                 hg_ref, kcn_ref, vcn_ref, ksw_ref, qt_ref, vt_ref, gt_ref):
    hb = _rms(x_ref[...], g_ref[...]).astype(BF16)
    hg_ref[...] = _dot(hb, wh_ref[...])

    def rope(v, axis, cos, sin_hi, sin_lo):
        return (v * cos + pltpu.roll(v, ROPE_HALF, axis) * sin_hi
                + pltpu.roll(v, LANES - ROPE_HALF, axis) * sin_lo)

    cos = cs_ref[0, 0:ROPE_HALF, :]
    sin = cs_ref[0, ROPE_HALF:ROPE_DIM, :]
    tm = cos.shape[1]
    zero_h = jnp.zeros((ROPE_HALF, tm), F32)
    rest = NSA_HEAD_DIM - ROPE_DIM
    slab = lambda lo, hi, fill: jnp.concatenate([lo, hi, jnp.full((rest, tm), fill, F32)] * (LANES // NSA_HEAD_DIM), axis=0)
    tab_t = (slab(cos, cos, 1.0), slab(zero_h, sin, 0.0), slab(-sin, zero_h, 0.0))
    tab = tuple(a.T for a in tab_t)
    kn = _dot(hb, wk_ref[...])
    kcn_ref[...] = rope(kn[:, 0:LANES], 1, *tab)
    vcn_ref[...] = kn[:, LANES:2 * LANES]
    ksw_ref[:, 0:LANES] = rope(kn[:, 2 * LANES:3 * LANES], 1, *tab).astype(BF16)
    ksw_ref[:, LANES:2 * LANES] = rope(kn[:, 3 * LANES:4 * LANES], 1, *tab).astype(BF16)

    rt = _dot_nt(wt_ref[...], hb)
    scale = NSA_HEAD_DIM ** -0.5 * LOG2_E
    for j in range(NSA_WIDTH // LANES):
        sl = slice(j * LANES, (j + 1) * LANES)
        qt_ref[0, sl, :] = (rope(rt[sl], 0, *tab_t) * scale).astype(BF16)
    vt_ref[0] = rt[NSA_WIDTH:NSA_WIDTH + 2 * KV_WIDTH].astype(BF16)
    gt_ref[0] = jax.nn.sigmoid(rt[NSA_WIDTH + 2 * KV_WIDTH:])


def _inproj_call(x2, gain, wh, wk, wt, cs, tiles_per_seq):
    n = x2.shape[0]
    tm = PROJ_TM
    t = tiles_per_seq * tm
    b = n // t
    row = lambda w: pl.BlockSpec((tm, w), lambda i: (i, 0))
    col = lambda h: pl.BlockSpec((1, h, tm), lambda i: (i // tiles_per_seq, 0, i % tiles_per_seq))
    full = lambda a: pl.BlockSpec(a.shape, lambda i: (0, 0))
    gate_rows = NSA_KV_HEADS * LANES
    return pl.pallas_call(
        _inproj_body,
        grid=(n // tm,),
        in_specs=[row(D_MODEL), full(gain), full(wh), full(wk), full(wt),
                  col(ROPE_DIM)],
        out_specs=[row(4 * HG_WIDTH), row(KV_WIDTH), row(KV_WIDTH), row(2 * KV_WIDTH),
                   col(NSA_WIDTH), col(2 * KV_WIDTH), col(gate_rows)],
        out_shape=[jax.ShapeDtypeStruct((n, 4 * HG_WIDTH), F32),
                   jax.ShapeDtypeStruct((n, KV_WIDTH), F32),
                   jax.ShapeDtypeStruct((n, KV_WIDTH), F32),
                   jax.ShapeDtypeStruct((n, 2 * KV_WIDTH), BF16),
                   jax.ShapeDtypeStruct((b, NSA_WIDTH, t), BF16),
                   jax.ShapeDtypeStruct((b, 2 * KV_WIDTH, t), BF16),
                   jax.ShapeDtypeStruct((b, gate_rows, t), F32)],
        compiler_params=pltpu.CompilerParams(dimension_semantics=("arbitrary",),
                                             vmem_limit_bytes=VMEM_LIMIT),
        name="inproj",
    )(x2, gain, wh, wk, wt, cs)


def _hgrn_tables():
    L = HG_CHUNK
    t = np.arange(L)[:, None]
    u = np.arange(L)[None, :]
    level = np.where(((t // HG_DIAG) == (u // HG_DIAG)) & (u <= t), 1, 0)
    for li, s in enumerate(HG_LEVELS):
        same = (t // (2 * s)) == (u // (2 * s))
        right = (t % (2 * s)) >= s
        level = np.where(same & right & ((u % (2 * s)) < s), li + 2, level)
    return jnp.asarray((u <= t).astype(np.float32), BF16), jnp.asarray(level, jnp.int32)


def _hgrn_body(q_ref, f_ref, i_ref, g_ref, lb_ref, gain_ref, mst_ref, lvl_ref, o_ref, st_ref):
    L = HG_CHUNK
    n_chunks = q_ref.shape[1] // L

    @pl.when(pl.program_id(1) == 0)
    def _sequence_start():
        st_ref[...] = jnp.zeros_like(st_ref)

    def chunk(c, carry):
        rows = pl.ds(pl.multiple_of(c * L, L), L)
        heads = range(HG_HEADS)
        cols = [slice(h * HG_DK, (h + 1) * HG_DK) for h in heads]
        mst = mst_ref[...]
        lvl = lvl_ref[...]
        n_lv = len(HG_LEVELS)
        row_i = lax.broadcasted_iota(jnp.int32, (L, HG_DK), 0)
        q = [q_ref[0, rows, cols[h]] for h in heads]
        vb = [i_ref[0, rows, cols[h]].astype(BF16) for h in heads]
        f = [lb_ref[:, cols[h]] + (1.0 - lb_ref[:, cols[h]]) * jax.nn.sigmoid(f_ref[0, rows, cols[h]]) for h in heads]
        k = [1.0 - f[h] for h in heads]
        parts = [_split3(jnp.log2(f[h])) for h in heads]
        e_full = [(_dot(mst, parts[h][0]) + _dot(mst, parts[h][1])) + _dot(mst, parts[h][2]) for h in heads]
        b_last = [e_full[h][L - 1:L, :] for h in heads]

        def rel_to(b, blk, off):
            refs = []
            for r0 in range(0, L, blk):
                r = r0 + off - 1
                ref = b[r:r + 1, :] if r >= 0 else jnp.zeros((1, HG_DK), F32)
                refs.append(jnp.broadcast_to(ref, (blk, HG_DK)))
            return b - jnp.concatenate(refs, axis=0)

        def level_sums(b):
            out = [rel_to(b, HG_DIAG, 0)]
            for s_half in HG_LEVELS:
                d = rel_to(b, 2 * s_half, s_half)
                out.append(jnp.where((row_i % (2 * s_half)) >= s_half, d, -d))
            return out

        e = [level_sums(e_full[h]) for h in heads]
        wq = [[jnp.exp2(e[h][l]) for l in range(n_lv + 1)] for h in heads]
        wk = [[jnp.exp2(-e[h][0])] + wq[h][1:] for h in heads]
        prod = [[_dot_nt((q[h] * wq[h][l]).astype(BF16), (k[h] * wk[h][l]).astype(BF16)) for l in range(n_lv + 1)]
                for h in heads]
        st = [st_ref[h] for h in heads]
        inter = [_dot_nt((q[h] * jnp.exp2(e_full[h])).astype(BF16), st[h].astype(BF16)) for h in heads]
        k_dec = [(k[h] * jnp.exp2(b_last[h] - e_full[h])).astype(BF16) for h in heads]
        upd = [_dot_tn(vb[h], k_dec[h]) for h in heads]
        for h in heads:
            st_ref[h] = st[h] * jnp.exp2(b_last[h]) + upd[h]
        a = []
        for h in heads:
            ah = jnp.where(lvl == 1, prod[h][0], 0.0)
            for l in range(1, n_lv + 1):
                ah = jnp.where(lvl == l + 1, prod[h][l], ah)
            a.append(ah.astype(BF16))
        o = [_dot(a[h], vb[h]) + inter[h] for h in heads]
        for h in heads:
            oh = o[h] * lax.rsqrt(jnp.mean(o[h] * o[h], axis=-1, keepdims=True) + EPS) * gain_ref[:, cols[h]]
            o_ref[0, rows, cols[h]] = (oh * jax.nn.silu(g_ref[0, rows, cols[h]])).astype(o_ref.dtype)
        return carry

    lax.fori_loop(0, n_chunks, chunk, 0, unroll=True)


def _hgrn_call(hg, lb, gain, mst, lvl):
    b, t, _ = hg.shape
    tt = HG_TT
    col = lambda k: pl.BlockSpec((1, tt, HG_WIDTH), lambda bi, ti: (bi, ti, k))
    full = lambda a: pl.BlockSpec(a.shape, lambda bi, ti: (0, 0))
    return pl.pallas_call(
        _hgrn_body,
        grid=(b, t // tt),
        in_specs=[col(0), col(1), col(2), col(3), full(lb), full(gain), full(mst), full(lvl)],
        out_specs=pl.BlockSpec((1, tt, HG_WIDTH), lambda bi, ti: (bi, ti, 0)),
        out_shape=jax.ShapeDtypeStruct((b, t, HG_WIDTH), BF16),
        scratch_shapes=[pltpu.VMEM((HG_HEADS, HG_DV, HG_DK), F32)],
        compiler_params=pltpu.CompilerParams(dimension_semantics=("arbitrary", "arbitrary"),
                                             vmem_limit_bytes=VMEM_LIMIT),
        name="hgrn2",
    )(hg, hg, hg, hg, lb, gain, mst, lvl)


def _cmp_body(kcn_ref, vcn_ref, pek_ref, pev_ref, w1k_ref, w1v_ref, w2k_ref, w2v_ref, kc_ref, vc_ref):
    nb = kcn_ref.shape[1] // CMP_STRIDE
    lane_grp = (lax.broadcasted_iota(jnp.int32, (nb, CMP_STRIDE * LANES), 1) // NSA_HEAD_DIM) % NSA_KV_HEADS

    def hidden(src_ref, pe_ref, w1_ref):
        x = jnp.concatenate([src_ref[0, pl.ds(l, nb, stride=CMP_STRIDE), :]
                             for l in range(CMP_STRIDE)], axis=1)
        halves = [x + pe_ref[i] for i in range(2)]
        out = []
        for g in range(NSA_KV_HEADS):
            u, v = (_dot(jnp.where(lane_grp == g, halves[i], 0.0).astype(BF16), w1_ref[i]) for i in range(2))
            out.append(jax.nn.silu(u + pltpu.roll(v, nb - 1, 0)).astype(BF16))
        return out

    hk = hidden(kcn_ref, pek_ref, w1k_ref)
    hv = hidden(vcn_ref, pev_ref, w1v_ref)
    for g in range(NSA_KV_HEADS):
        kc_ref[0, g, 0:nb, :] = _dot(hk[g], w2k_ref[0]).astype(kc_ref.dtype)
        kc_ref[0, g, nb:2 * nb, :] = _dot(hk[g], w2k_ref[1]).astype(kc_ref.dtype)
        vc_ref[0, g, :, 0:nb] = _dot_nt(w2v_ref[0], hv[g]).astype(vc_ref.dtype)
        vc_ref[0, g, :, nb:2 * nb] = _dot_nt(w2v_ref[1], hv[g]).astype(vc_ref.dtype)


def _cmp_call(kcn, vcn, pek, pev, w1k, w1v, w2k, w2v):
    b, t, w = kcn.shape
    nb = t // CMP_STRIDE
    full = lambda a: pl.BlockSpec(a.shape, lambda bi: (0,) * a.ndim)
    out = lambda r, c: pl.BlockSpec((1, NSA_KV_HEADS, r, c), lambda bi: (bi, 0, 0, 0))
    return pl.pallas_call(
        _cmp_body,
        grid=(b,),
        in_specs=[pl.BlockSpec((1, t, w), lambda bi: (bi, 0, 0)), pl.BlockSpec((1, t, w), lambda bi: (bi, 0, 0)),
                  full(pek), full(pev), full(w1k), full(w1v), full(w2k), full(w2v)],
        out_specs=[out(2 * nb, LANES), out(LANES, 2 * nb)],
        out_shape=[jax.ShapeDtypeStruct((b, NSA_KV_HEADS, 2 * nb, LANES), BF16),
                   jax.ShapeDtypeStruct((b, NSA_KV_HEADS, LANES, 2 * nb), BF16)],
        compiler_params=pltpu.CompilerParams(dimension_semantics=("arbitrary",),
                                             vmem_limit_bytes=VMEM_LIMIT),
        name="nsa_compress",
    )(kcn, vcn, pek, pev, w1k, w1v, w2k, w2v)


def _nsa_body(qt_ref, ksw_ref, vst_ref, vwt_ref, kc_ref, vct_ref, gt_ref, gain_ref, ovt_ref, o_ref,
              ks_ref, kw_ref, vs_ref, vw_ref, m_ref, acc_ref, s_ref, ch_ref, ocw_ref):
    g = pl.program_id(1)
    qi = pl.program_id(2)
    tq = ATT_TQ
    tk = ATT_TK
    t_len = ksw_ref.shape[1]
    n_kt = t_len // tk
    n_pairs = HPG // 2
    hd = NSA_HEAD_DIM

    @pl.when(qi == 0)
    def _build_kv():
        lane = lax.broadcasted_iota(jnp.int32, (tk, LANES), 1)
        lo_lane = lane < hd
        keep = (lane // hd) == g

        def build_k(src_col, dst_ref):
            for j in range(n_kt):
                x = ksw_ref[0, j * tk:(j + 1) * tk, src_col * LANES:(src_col + 1) * LANES].astype(F32)
                dup = jnp.where(keep, x, pltpu.roll(x, hd, 1))
                dst_ref[j, 0:tk, :] = jnp.where(lo_lane, dup, 0.0).astype(BF16)
                dst_ref[j, tk:2 * tk, :] = jnp.where(lo_lane, 0.0, dup).astype(BF16)

        def build_v(src_ref, dst_ref):
            zero = jnp.zeros((hd, tk), BF16)
            row = lax.broadcasted_iota(jnp.int32, (SUM_ROWS, 2 * tk), 0)
            col = lax.broadcasted_iota(jnp.int32, (SUM_ROWS, 2 * tk), 1)
            ones_rows = jnp.where(((row == 0) & (col < tk)) | ((row == 1) & (col >= tk)), 1.0, 0.0).astype(BF16)
            for j in range(n_kt):
                x = src_ref[0, :, j * tk:(j + 1) * tk]
                dst_ref[j, 0:hd, 0:tk] = x
                dst_ref[j, 0:hd, tk:2 * tk] = zero
                dst_ref[j, hd:2 * hd, 0:tk] = zero
                dst_ref[j, hd:2 * hd, tk:2 * tk] = x
                dst_ref[j, 2 * hd:2 * hd + SUM_ROWS, :] = ones_rows

        build_k(0, ks_ref)
        build_k(1, kw_ref)
        build_v(vst_ref, vs_ref)
        build_v(vwt_ref, vw_ref)

    t0 = qi * tq
    key_i = lax.broadcasted_iota(jnp.int32, (tk, tq), 0)
    qry_t = t0 + lax.broadcasted_iota(jnp.int32, (tk, tq), 1)
    slab_lo = lax.broadcasted_iota(jnp.int32, (LANES, tq), 0) < hd
    acc_row = lax.broadcasted_iota(jnp.int32, (LANES + SUM_ROWS, tq), 0)
    slab_a = (acc_row < hd) | (acc_row == LANES)
    q_pairs = [qt_ref[0, p * LANES:(p + 1) * LANES, :] for p in range(n_pairs)]

    last = (t0 + tq - 1) // tk

    def scores(k_ref, j):
        kt = k_ref[j]
        return [_dot(kt, q_pairs[p]) for p in range(n_pairs)]

    def live_keys(rel_at_origin, lower, upper):
        out = []
        for qh in range(tq // LANES):
            live = [kb for kb in range(tk // KEY_BLK)
                    if rel_at_origin + qh * LANES + LANES - 1 - kb * KEY_BLK >= lower
                    and rel_at_origin + qh * LANES - (kb * KEY_BLK + KEY_BLK - 1) < upper]
            out.append((min(live) * KEY_BLK, (max(live) + 1) * KEY_BLK))
        return out

    def tile_softmax(slot, bias, m_get, m_put, keys=None):
        alphas = {}
        probs = {}
        for hh in range(HPG):
            p, h = divmod(hh, 2)
            a_parts = []
            p_parts = []
            for qh in range(tq // LANES):
                ql = slice(qh * LANES, (qh + 1) * LANES)
                k_lo, k_hi = keys[qh] if keys is not None else (0, tk)
                sh = (s_ref[slot, p, h * tk + k_lo:h * tk + k_hi, ql]
                      + (bias if bias.shape == (1, 1) else bias[k_lo:k_hi, ql]))
                m_prev = m_get(hh, qh)
                m_new = jnp.maximum(m_prev, jnp.max(sh, axis=0, keepdims=True))
                m_put(hh, qh, m_new)
                piece = [jnp.zeros((k_lo, LANES), BF16)] if k_lo else []
                piece.append(jnp.exp2(sh - m_new).astype(BF16))
                if k_hi < tk:
                    piece.append(jnp.zeros((tk - k_hi, LANES), BF16))
                p_parts.append(jnp.concatenate(piece, axis=0) if len(piece) > 1 else piece[0])
                a_parts.append(jnp.exp2(m_prev - m_new))
            alphas[hh] = jnp.concatenate(a_parts, axis=1)
            probs[hh] = jnp.concatenate(p_parts, axis=1)
        return ([jnp.concatenate([probs[2 * p], probs[2 * p + 1]], axis=0) for p in range(n_pairs)],
                [jnp.where(slab_a, alphas[2 * p], alphas[2 * p + 1]) for p in range(n_pairs)])

    def normalised(acc):
        inv = jnp.where(slab_lo, 1.0 / acc[LANES:LANES + 1, :], 1.0 / acc[LANES + 1:LANES + 2, :])
        return acc[0:LANES] * inv

    def picked_bias(j, also=None):
        per_tile = tk // SLC_BLOCK
        picked = jnp.concatenate([jnp.broadcast_to(ch_ref[j * per_tile + i], (SLC_BLOCK, tq)) for i in range(per_tile)],
                                 axis=0) > 0.5
        return jnp.where(picked if also is None else picked & also, 0.0, NEG_INF)

    def m_put(hh, qh, v):
        m_ref[hh, :, qh * LANES:(qh + 1) * LANES] = v

    n_win = (WINDOW + tq) // tk
    n_near = SLC_NEAR_TILES
    n_far = jnp.maximum(last + 1 - n_near, 0)

    def static_part(first_k, has_far, need_topk):
        n_cmp_pad = kc_ref.shape[2] // 2
        blk_i = lax.broadcasted_iota(jnp.int32, (n_cmp_pad, tq), 0)
        blk_t = t0 + lax.broadcasted_iota(jnp.int32, (n_cmp_pad, tq), 1)
        cmp_ok = (blk_i * CMP_STRIDE + (CMP_BLOCK - 1)) <= blk_t
        kc = kc_ref[0, 0]
        vct = vct_ref[0, 0]
        s_cmp = [_dot(kc, q_pairs[p]) for p in range(n_pairs)]
        win_tiles = [(k, last - (n_win - 1) + k) for k in range(first_k, n_win)]
        near_tiles = [last - k for k in range(n_near)]
        up_front = ([(0, ks_ref, 0)] if has_far else []) + [(1 + k, kw_ref, jw) for k, jw in win_tiles]
        up_front += [(1 + n_win + k, ks_ref, jnp.maximum(jn, 0)) for k, jn in enumerate(near_tiles)]
        for slot, k_ref, j0 in up_front:
            s_first = scores(k_ref, j0)
            for p in range(n_pairs):
                s_ref[slot, p] = s_first[p]
        p_sum = jnp.zeros((n_cmp_pad, tq), F32)
        p_cmp = []
        for p in range(n_pairs):
            probs = []
            for h in range(2):
                sh = jnp.where(cmp_ok, s_cmp[p][h * n_cmp_pad:(h + 1) * n_cmp_pad], NEG_INF)
                mh = jnp.max(sh, axis=0, keepdims=True)
                eh = jnp.where(cmp_ok, jnp.exp2(sh - mh), 0.0)
                den = jnp.sum(eh, axis=0, keepdims=True)
                ph = eh / jnp.where(den > 0.0, den, 1.0)
                p_sum = p_sum + ph
                probs.append(ph.astype(BF16))
            p_cmp.append(jnp.concatenate(probs, axis=0))
        o_cmp = [_dot(vct, p_cmp[p]) for p in range(n_pairs)]

        n_sel = t_len // SLC_BLOCK
        if need_topk:
            hi, mid, lo = _split3(p_sum)
            ovt = ovt_ref[...]
            p_sel = ((_dot(ovt, hi) + _dot(ovt, mid)) + _dot(ovt, lo))[0:n_sel]
            sel_i = lax.broadcasted_iota(jnp.int32, (n_sel, tq), 0)
            cur = (t0 + lax.broadcasted_iota(jnp.int32, (n_sel, tq), 1)) // SLC_BLOCK
            forced = (sel_i == 0) | (sel_i == cur) | (sel_i == cur - 1)
            score = jnp.where(forced, FORCE_SCORE, p_sel)
            score = jnp.where(sel_i <= cur, score, -jnp.inf)
            rank = jnp.zeros((n_sel, tq), jnp.int32)
            row_grp = SUBLANES
            grp_i = lax.broadcasted_iota(jnp.int32, (row_grp, tq), 0)
            for i in range(n_sel):
                ci = score[i:i + 1, :]
                ahead = []
                for r0 in range(0, n_sel, row_grp):
                    rows = slice(r0, r0 + row_grp)
                    if r0 > i:
                        ahead.append(ci >= score[rows])
                    elif r0 + row_grp <= i:
                        ahead.append(ci > score[rows])
                    else:
                        ahead.append((ci > score[rows]) | ((ci == score[rows]) & (grp_i > i - r0)))
                rank = rank + jnp.where(jnp.concatenate(ahead, axis=0), 1, 0)
            chosen = jnp.where(rank < min(SLC_TOPK, n_sel), 1.0, 0.0)
        else:
            sel_i = lax.broadcasted_iota(jnp.int32, (n_sel, tq), 0)
            cur = (t0 + lax.broadcasted_iota(jnp.int32, (n_sel, tq), 1)) // SLC_BLOCK
            chosen = jnp.where(sel_i <= cur, 1.0, 0.0)
        for i in range(n_sel):
            ch_ref[i] = chosen[i:i + 1, :]

        m_win = {}
        acc_win = [jnp.zeros((LANES + SUM_ROWS, tq), F32) for _ in range(n_pairs)]
        for k, jw in win_tiles:
            rel_hi = (n_win - k) * tk - 1
            rel_lo = rel_hi - (tq - 1) - (tk - 1)
            exists = jnp.zeros((1, 1), F32)
            if rel_lo >= 0 and rel_hi < WINDOW:
                bias = exists
            else:
                rel = qry_t - (jw * tk + key_i)
                inside = (rel < WINDOW) if rel_lo >= 0 else (rel >= 0) if rel_hi < WINDOW else (rel >= 0) & (rel < WINDOW)
                bias = jnp.where(inside, exists, NEG_INF)
            probs, a_rows = tile_softmax(1 + k, bias,
                                         lambda hh, qh: m_win.get((hh, qh), jnp.full((1, LANES), -jnp.inf, F32)),
                                         lambda hh, qh, v: m_win.__setitem__((hh, qh), v),
                                         keys=live_keys(rel_lo + tk - 1, 0, WINDOW))
            vt = vw_ref[jw]
            acc_win = [acc_win[p] * a_rows[p] + _dot(vt, probs[p]) for p in range(n_pairs)]
        for p in range(n_pairs):
            ocw_ref[0, p] = o_cmp[p]
            ocw_ref[1, p] = normalised(acc_win[p])

        m_near = {}
        acc_near = [jnp.zeros((LANES + SUM_ROWS, tq), F32) for _ in range(n_pairs)]
        for k, jn in enumerate(near_tiles):
            jc = jnp.maximum(jn, 0)
            causal = (jn * tk + key_i) <= qry_t
            if k > 0:
                causal = causal & (jn >= 0)
            probs, a_rows = tile_softmax(1 + n_win + k, picked_bias(jc, causal),
                                         lambda hh, qh: m_near.get((hh, qh), jnp.full((1, LANES), -jnp.inf, F32)),
                                         lambda hh, qh, v: m_near.__setitem__((hh, qh), v),
                                         keys=live_keys(tk - tq + k * tk, 0, t_len))
            vt = vs_ref[jc]
            acc_near = [acc_near[p] * a_rows[p] + _dot(vt, probs[p]) for p in range(n_pairs)]

        for (hh, qh), v in m_near.items():
            m_put(hh, qh, v)
        for p in range(n_pairs):
            acc_ref[p] = acc_near[p]

    def variant(e):
        tiles_upto = (e + 1) * tq // tk
        return (max(0, n_win - tiles_upto), tiles_upto > n_near, (e + 1) * tq > SLC_TOPK * SLC_BLOCK)

    n_q = t_len // tq
    start = 0
    for e in range(1, n_q + 1):
        if e == n_q or variant(e) != variant(start):
            pl.when((qi >= start) & (qi < e))(functools.partial(static_part, *variant(start)))
            start = e
    o_cmp = [ocw_ref[0, p] for p in range(n_pairs)]
    o_win = [ocw_ref[1, p] for p in range(n_pairs)]

    def slc_step(j, carry):
        s_next = scores(ks_ref, jnp.minimum(j + 1, n_far - 1))
        probs, a_rows = tile_softmax(0, picked_bias(j), lambda hh, qh: m_ref[hh, :, qh * LANES:(qh + 1) * LANES], m_put)
        vt = vs_ref[j]
        for p in range(n_pairs):
            pv = _dot(vt, probs[p])
            s_ref[0, p] = s_next[p]
            acc_ref[p] = acc_ref[p] * a_rows[p] + pv
        return carry

    lax.fori_loop(0, n_far, slc_step, 0)
    o_slc = [normalised(acc_ref[p]) for p in range(n_pairs)]

    gates = gt_ref[0]
    gain = gain_ref[...]
    for p in range(n_pairs):
        o = jnp.zeros((LANES, tq), F32)
        for c, branch in enumerate((o_cmp[p], o_slc[p], o_win[p])):
            r = c * HPG + 2 * p
            o = o + jnp.where(slab_lo, gates[r:r + 1, :], gates[r + 1:r + 2, :]) * branch
        sq = o * o
        ms_a = jnp.sum(sq[0:hd], axis=0, keepdims=True)
        ms_b = jnp.sum(sq[hd:2 * hd], axis=0, keepdims=True)
        ms = jnp.where(slab_lo, ms_a, ms_b) * (1.0 / hd)
        o = o * lax.rsqrt(ms + EPS)
        o_ref[0, p * LANES:(p + 1) * LANES, :] = (o * gain[p * LANES:(p + 1) * LANES, :]).astype(o_ref.dtype)


def _nsa_call(qt, ksw, vt, kc, vct, gt, gain, ovt):
    b, _, t = qt.shape
    tq, tk = ATT_TQ, ATT_TK
    n_kt = t // tk
    gw = HPG * NSA_HEAD_DIM
    hd = NSA_HEAD_DIM
    k_scratch = pltpu.VMEM((n_kt, 2 * tk, LANES), BF16)
    v_scratch = pltpu.VMEM((n_kt, LANES + SUM_ROWS, 2 * tk), BF16)
    return pl.pallas_call(
        _nsa_body,
        grid=(b, NSA_KV_HEADS, t // tq),
        in_specs=[
            pl.BlockSpec((1, gw, tq), lambda bi, gi, qi: (bi, gi, qi)),
            pl.BlockSpec((1, t, 2 * KV_WIDTH), lambda bi, gi, qi: (bi, 0, 0)),
            pl.BlockSpec((1, hd, t), lambda bi, gi, qi: (bi, gi, 0)),
            pl.BlockSpec((1, hd, t), lambda bi, gi, qi: (bi, NSA_KV_HEADS + gi, 0)),
            pl.BlockSpec((1, 1) + kc.shape[2:], lambda bi, gi, qi: (bi, gi, 0, 0)),
            pl.BlockSpec((1, 1) + vct.shape[2:], lambda bi, gi, qi: (bi, gi, 0, 0)),
            pl.BlockSpec((1, LANES, tq), lambda bi, gi, qi: (bi, gi, qi)),
            pl.BlockSpec((gw, 1), lambda bi, gi, qi: (gi, 0)),
            pl.BlockSpec(ovt.shape, lambda bi, gi, qi: (0, 0)),
        ],
        out_specs=pl.BlockSpec((1, gw, tq), lambda bi, gi, qi: (bi, gi, qi)),
        out_shape=jax.ShapeDtypeStruct((b, NSA_WIDTH, t), BF16),
        scratch_shapes=[k_scratch, k_scratch, v_scratch, v_scratch,
                        pltpu.VMEM((HPG, 1, tq), F32),
                        pltpu.VMEM((HPG // 2, LANES + SUM_ROWS, tq), F32), pltpu.VMEM((1 + (WINDOW + tq) // tk + SLC_NEAR_TILES, HPG // 2, 2 * tk, tq), F32),
                        pltpu.VMEM((t // SLC_BLOCK, 1, tq), F32), pltpu.VMEM((2, HPG // 2, LANES, tq), F32)],
        compiler_params=pltpu.CompilerParams(dimension_semantics=("arbitrary", "arbitrary", "arbitrary"),
                                             vmem_limit_bytes=VMEM_LIMIT),
        name="nsa_attention",
    )(qt, ksw, vt, vt, kc, vct, gt, gain, ovt)


def _ffn_body(x_ref, oh_ref, on_ref, woh_ref, won_ref, g2_ref, wg_ref, wu_ref, wd_ref, cw_ref, gf_ref,
              out_ref, halo_ref, act_ref, *, tiles_per_seq):
    tm = x_ref.shape[0]
    x1 = x_ref[...] + _dot(oh_ref[...], woh_ref[...]) + _dot_tn(on_ref[0], won_ref[...])
    hb = _rms(x1, g2_ref[...]).astype(BF16)
    row = lax.broadcasted_iota(jnp.int32, (tm, FFN_TC), 0)

    @pl.when((pl.program_id(0) % tiles_per_seq) == 0)
    def _sequence_start():
        halo_ref[...] = jnp.zeros_like(halo_ref)

    def activation(c, gate, up):
        cols = slice(c * FFN_TC, (c + 1) * FFN_TC)
        halo = halo_ref[:, cols]
        halo_ref[:, cols] = gate[tm - SUBLANES:tm, :]
        last1 = halo[SUBLANES - 1:SUBLANES, :]
        last2 = halo[SUBLANES - 2:SUBLANES - 1, :]
        prev1 = jnp.where(row == 0, last1, pltpu.roll(gate, 1, 0))
        prev2 = jnp.where(row == 0, last2, jnp.where(row == 1, last1, pltpu.roll(gate, 2, 0)))
        cw = cw_ref[:, cols]
        y = cw[0:1, :] * prev2 + cw[1:2, :] * prev1 + cw[2:3, :] * gate + cw[3:4, :]
        return (jax.nn.silu(y) * up).astype(BF16)

    chunk = lambda w_ref, c: _dot(hb, w_ref[:, c * FFN_TC:(c + 1) * FFN_TC].astype(BF16))
    gate_up = (chunk(wg_ref, 0), chunk(wu_ref, 0))
    for c in range(FFN_NC):
        cur = gate_up
        if c + 1 < FFN_NC:
            gate_up = (chunk(wg_ref, c + 1), chunk(wu_ref, c + 1))
        act_ref[:, c * FFN_TC:(c + 1) * FFN_TC] = activation(c, *cur)
    acc = _dot(act_ref[...], wd_ref[...].astype(BF16))
    out_ref[...] = _rms(x1 + acc, gf_ref[...])


def _ffn_call(x2, oh, on, woh, won, g2, wg, wu, wd, cw, gf, tiles_per_seq):
    n = x2.shape[0]
    tm = FFN_TM
    row = lambda w: pl.BlockSpec((tm, w), lambda i: (i, 0))
    full = lambda a: pl.BlockSpec(a.shape, lambda i: (0,) * a.ndim, pipeline_mode=pl.Buffered(1))
    return pl.pallas_call(
        functools.partial(_ffn_body, tiles_per_seq=tiles_per_seq),
        grid=(n // tm,),
        in_specs=[row(D_MODEL), row(HG_WIDTH),
                  pl.BlockSpec((1, NSA_WIDTH, tm), lambda i: (i // tiles_per_seq, 0, i % tiles_per_seq)),
                  full(woh), full(won), full(g2),
                  full(wg), full(wu), full(wd), full(cw), full(gf)],
        out_specs=row(D_MODEL),
        out_shape=jax.ShapeDtypeStruct((n, D_MODEL), F32),
        scratch_shapes=[pltpu.VMEM((SUBLANES, D_FF), F32), pltpu.VMEM((tm, D_FF), BF16)],
        compiler_params=pltpu.CompilerParams(dimension_semantics=("arbitrary",),
                                             vmem_limit_bytes=VMEM_LIMIT),
        name="outproj_convffn",
    )(x2, oh, on, woh, won, g2, wg, wu, wd, cw, gf)


def _rope_angles(positions):
    inv_freq = ROPE_THETA ** (-jnp.arange(ROPE_HALF, dtype=F32) * 2.0 / ROPE_DIM)
    ang = positions.astype(F32)[..., None] * inv_freq
    return jnp.concatenate([jnp.cos(ang), jnp.sin(ang)], axis=-1).transpose(0, 2, 1)


def _layer(x, positions, ln1, w_in, lb, hg_gain, pe_k, pe_v, k_w1, k_w2, v_w1, v_w2, nsa_gain, w_o, ln2,
           w_gate, w_up, conv_w, conv_b, w_down, final_gain):
    b, t, d = x.shape
    n = b * t
    assert d == D_MODEL and t % FFN_TM == 0 and t % PROJ_TM == 0 and t % ATT_TQ == 0 and t % HG_TT == 0
    n_grp = t // CMP_STRIDE
    assert n_grp == LANES, "compressed-block axis is laid out on exactly one lane tile"
    n_sel = t // SLC_BLOCK
    assert n_sel % 8 == 0 and n_sel <= LANES and ATT_TK % SLC_BLOCK == 0 and ATT_TQ % ATT_TK == 0
    x2 = x.reshape(n, d)

    splits = np.cumsum([0, 4 * HG_WIDTH, NSA_WIDTH] + [KV_WIDTH] * 6 + [N_GATES])
    seg = lambda i: w_in[:, splits[i]:splits[i + 1]]
    wh = seg(0).astype(BF16)
    wk = jnp.concatenate([seg(2), seg(3), seg(4), seg(6)], axis=1).astype(BF16)
    wgate = seg(8).reshape(d, 3, NSA_KV_HEADS, HPG).transpose(0, 2, 1, 3).reshape(d, NSA_KV_HEADS, 3 * HPG)
    wgate = jnp.pad(wgate, ((0, 0), (0, 0), (0, LANES - 3 * HPG))).reshape(d, NSA_KV_HEADS * LANES)
    wt = jnp.concatenate([seg(1), seg(5), seg(7), wgate], axis=1).T.astype(BF16)
    cs = _rope_angles(positions)

    hg, kcn, vcn, ksw, qt, vt, gt = _inproj_call(x2, ln1.reshape(1, d), wh, wk, wt, cs, t // PROJ_TM)

    mst, lvl = _hgrn_tables()
    o_hg = _hgrn_call(hg.reshape(b, t, 4 * HG_WIDTH), lb.reshape(1, HG_WIDTH).astype(F32),
                      hg_gain.reshape(1, HG_WIDTH), mst, lvl)

    per_lane = lambda a: jnp.broadcast_to(a.reshape(2, CMP_STRIDE, 1, NSA_HEAD_DIM, -1),
                                          (2, CMP_STRIDE, NSA_KV_HEADS, NSA_HEAD_DIM, a.shape[-1]))
    w1_rows = lambda w1: per_lane(w1).reshape(2, CMP_STRIDE * LANES, CMP_HIDDEN).astype(BF16)
    pe_rows = lambda pe: per_lane(pe[..., None]).reshape(2, 1, CMP_STRIDE * LANES)
    zeros_w2 = jnp.zeros((CMP_HIDDEN, NSA_HEAD_DIM), F32)
    place = lambda w2: jnp.stack([jnp.concatenate([w2, zeros_w2], 1), jnp.concatenate([zeros_w2, w2], 1)])
    kc, vct = _cmp_call(kcn.reshape(b, t, KV_WIDTH), vcn.reshape(b, t, KV_WIDTH), pe_rows(pe_k), pe_rows(pe_v),
                        w1_rows(k_w1), w1_rows(v_w1),
                        place(k_w2).astype(BF16), place(v_w2).transpose(0, 2, 1).astype(BF16))

    cmp_start = np.arange(n_grp) * CMP_STRIDE
    cmp_end = cmp_start + CMP_BLOCK - 1
    sel_start = np.arange(LANES) * SLC_BLOCK
    overlap = ((cmp_start[:, None] <= sel_start[None, :] + SLC_BLOCK - 1) & (cmp_end[:, None] >= sel_start[None, :])
               & (np.arange(LANES)[None, :] < n_sel) & (np.arange(n_grp)[:, None] < n_grp - 1))
    ovt = jnp.asarray(overlap.T.astype(np.float32), BF16)
    o_nsa = _nsa_call(qt, ksw.reshape(b, t, 2 * KV_WIDTH), vt, kc, vct, gt, nsa_gain.reshape(NSA_WIDTH, 1), ovt)

    cw = jnp.concatenate([conv_w, conv_b[None, :], jnp.zeros((SUBLANES - CONV_WIDTH - 1, D_FF), F32)], axis=0)
    out = _ffn_call(x2, o_hg.reshape(n, HG_WIDTH), o_nsa,
                    w_o[:HG_WIDTH].astype(BF16), w_o[HG_WIDTH:].astype(BF16), ln2.reshape(1, d),
                    w_gate, w_up, w_down, cw,
                    final_gain.reshape(1, d), t // FFN_TM)
    return out.reshape(b, t, d)


def kernel(x, positions, ln1_gain, w_in, hgrn_lb_param, hgrn_out_gain, cmp_pe_k, cmp_pe_v, cmp_k_w1, cmp_k_w2,
           cmp_v_w1, cmp_v_w2, nsa_out_gain, w_o, ln2_gain, ffn_w_gate, ffn_w_up, ffn_conv_w, ffn_conv_b,
           ffn_w_down, final_gain):
    depth = ln1_gain.shape[0]
    assert depth == 1, "the fused final norm assumes a single layer"
    lower_bounds = jnp.cumsum(jax.nn.softmax(hgrn_lb_param.astype(F32), axis=0), axis=0)
    l = 0
    return _layer(x, positions, ln1_gain[l], w_in[l], lower_bounds[l], hgrn_out_gain[l], cmp_pe_k[l], cmp_pe_v[l],
                  cmp_k_w1[l], cmp_k_w2[l], cmp_v_w1[l], cmp_v_w2[l], nsa_out_gain[l], w_o[l], ln2_gain[l],
                  ffn_w_gate[l], ffn_w_up[l], ffn_conv_w[l], ffn_conv_b[l], ffn_w_down[l], final_gain)
```

```python
import functools

import jax
import jax.numpy as jnp
import numpy as np
from jax import lax
from jax.experimental import pallas as pl
from jax.experimental.pallas import tpu as pltpu

F32 = jnp.float32
BF16 = jnp.bfloat16

D_MODEL = 1024
HG_HEADS = 4
HG_DK = 128
HG_DV = 128
HG_WIDTH = HG_HEADS * HG_DV
NSA_HEADS = 8
NSA_KV_HEADS = 2
NSA_HEAD_DIM = 64
HPG = NSA_HEADS // NSA_KV_HEADS
NSA_WIDTH = NSA_HEADS * NSA_HEAD_DIM
KV_WIDTH = NSA_KV_HEADS * NSA_HEAD_DIM
CMP_BLOCK = 32
CMP_STRIDE = 16
CMP_HIDDEN = 256
SLC_BLOCK = 64
SLC_TOPK = 16
WINDOW = 512
ROPE_THETA = 500000.0
ROPE_DIM = NSA_HEAD_DIM // 4
ROPE_HALF = ROPE_DIM // 2
D_FF = 2816
CONV_WIDTH = 3
EPS = 1e-6
NEG_INF = -1e30
FORCE_SCORE = 1e4
N_GATES = 3 * NSA_HEADS
LOG2_E = 1.4426950408889634

LANES = 128
SUBLANES = 8
VMEM_LIMIT = 56 * 1024 * 1024

PROJ_TM = 1024
HG_CHUNK = 128
HG_LEVELS = (16, 32, 64)
HG_DIAG = 16
HG_TT = 1024
ATT_TQ = 256
ATT_TK = 256
KEY_BLK = 128
SLC_NEAR_TILES = 1
SUM_ROWS = 16
FFN_TM = 512
FFN_TC = 256
FFN_NC = D_FF // FFN_TC


def _dot(a, b):
    return jnp.dot(a, b, preferred_element_type=F32)


def _dot_nt(a, b):
    return lax.dot_general(a, b, (((1,), (1,)), ((), ())), preferred_element_type=F32)


def _dot_tn(a, b):
    return lax.dot_general(a, b, (((0,), (0,)), ((), ())), preferred_element_type=F32)


def _split3(x):
    hi = x.astype(BF16)
    r = x - hi.astype(F32)
    mid = r.astype(BF16)
    lo = (r - mid.astype(F32)).astype(BF16)
    return hi, mid, lo


def _rms(x, gain):
    return x * lax.rsqrt(jnp.mean(x * x, axis=-1, keepdims=True) + EPS) * gain


def _inproj_body(x_ref, g_ref, wh_ref, wk_ref, wt_ref, cs_ref,
                 hg_ref, kcn_ref, vcn_ref, ksw_ref, qt_ref, vt_ref, gt_ref):
    hb = _rms(x_ref[...], g_ref[...]).astype(BF16)
    hg_ref[...] = _dot(hb, wh_ref[...].astype(BF16))

    def rope(v, axis, cos, sin_hi, sin_lo):
        return (v * cos + pltpu.roll(v, ROPE_HALF, axis) * sin_hi
                + pltpu.roll(v, LANES - ROPE_HALF, axis) * sin_lo)

    cos = cs_ref[0, 0:ROPE_HALF, :]
    sin = cs_ref[0, ROPE_HALF:ROPE_DIM, :]
    tm = cos.shape[1]
    zero_h = jnp.zeros((ROPE_HALF, tm), F32)
    rest = NSA_HEAD_DIM - ROPE_DIM
    slab = lambda lo, hi, fill: jnp.concatenate([lo, hi, jnp.full((rest, tm), fill, F32)] * (LANES // NSA_HEAD_DIM), axis=0)
    tab_t = (slab(cos, cos, 1.0), slab(zero_h, sin, 0.0), slab(-sin, zero_h, 0.0))
    tab = tuple(a.T for a in tab_t)
    kn = _dot(hb, wk_ref[...])
    kcn_ref[...] = rope(kn[:, 0:LANES], 1, *tab)
    vcn_ref[...] = kn[:, LANES:2 * LANES]
    ksw_ref[:, 0:LANES] = rope(kn[:, 2 * LANES:3 * LANES], 1, *tab).astype(BF16)
    ksw_ref[:, LANES:2 * LANES] = rope(kn[:, 3 * LANES:4 * LANES], 1, *tab).astype(BF16)

    rt = _dot_nt(wt_ref[...], hb)
    scale = NSA_HEAD_DIM ** -0.5 * LOG2_E
    for j in range(NSA_WIDTH // LANES):
        sl = slice(j * LANES, (j + 1) * LANES)
        qt_ref[0, sl, :] = (rope(rt[sl], 0, *tab_t) * scale).astype(BF16)
    vt_ref[0] = rt[NSA_WIDTH:NSA_WIDTH + 2 * KV_WIDTH].astype(BF16)
    gt_ref[0] = jax.nn.sigmoid(rt[NSA_WIDTH + 2 * KV_WIDTH:])


def _inproj_call(x2, gain, w_in, wk, wt, cs, tiles_per_seq):
    n = x2.shape[0]
    tm = PROJ_TM
    t = tiles_per_seq * tm
    b = n // t
    row = lambda w: pl.BlockSpec((tm, w), lambda i: (i, 0))
    col = lambda h: pl.BlockSpec((1, h, tm), lambda i: (i // tiles_per_seq, 0, i % tiles_per_seq))
    full = lambda a: pl.BlockSpec(a.shape, lambda i: (0, 0))
    gate_rows = NSA_KV_HEADS * LANES
    return pl.pallas_call(
        _inproj_body,
        grid=(n // tm,),
        in_specs=[row(D_MODEL), full(gain),
                  pl.BlockSpec((w_in.shape[0], 4 * HG_WIDTH), lambda i: (0, 0), pipeline_mode=pl.Buffered(1)),
                  full(wk), full(wt),
                  col(ROPE_DIM)],
        out_specs=[row(4 * HG_WIDTH), row(KV_WIDTH), row(KV_WIDTH), row(2 * KV_WIDTH),
                   col(NSA_WIDTH), col(2 * KV_WIDTH), col(gate_rows)],
        out_shape=[jax.ShapeDtypeStruct((n, 4 * HG_WIDTH), F32),
                   jax.ShapeDtypeStruct((n, KV_WIDTH), F32),
                   jax.ShapeDtypeStruct((n, KV_WIDTH), F32),
                   jax.ShapeDtypeStruct((n, 2 * KV_WIDTH), BF16),
                   jax.ShapeDtypeStruct((b, NSA_WIDTH, t), BF16),
                   jax.ShapeDtypeStruct((b, 2 * KV_WIDTH, t), BF16),
                   jax.ShapeDtypeStruct((b, gate_rows, t), F32)],
        compiler_params=pltpu.CompilerParams(dimension_semantics=("arbitrary",),
                                             vmem_limit_bytes=VMEM_LIMIT),
        name="inproj",
    )(x2, gain, w_in, wk, wt, cs)


def _hgrn_tables():
    L = HG_CHUNK
    t = np.arange(L)[:, None]
    u = np.arange(L)[None, :]
    level = np.where(((t // HG_DIAG) == (u // HG_DIAG)) & (u <= t), 1, 0)
    for li, s in enumerate(HG_LEVELS):
        same = (t // (2 * s)) == (u // (2 * s))
        right = (t % (2 * s)) >= s
        level = np.where(same & right & ((u % (2 * s)) < s), li + 2, level)
    return jnp.asarray((u <= t).astype(np.float32), BF16), jnp.asarray(level, jnp.int32)


def _hgrn_body(q_ref, f_ref, i_ref, g_ref, lb_ref, gain_ref, mst_ref, lvl_ref, o_ref, st_ref):
    L = HG_CHUNK
    n_chunks = q_ref.shape[1] // L

    @pl.when(pl.program_id(1) == 0)
    def _sequence_start():
        st_ref[...] = jnp.zeros_like(st_ref)

    def chunk(c, carry):
        rows = pl.ds(pl.multiple_of(c * L, L), L)
        heads = range(HG_HEADS)
        cols = [slice(h * HG_DK, (h + 1) * HG_DK) for h in heads]
        mst = mst_ref[...]
        lvl = lvl_ref[...]
        n_lv = len(HG_LEVELS)
        row_i = lax.broadcasted_iota(jnp.int32, (L, HG_DK), 0)
        q = [q_ref[0, rows, cols[h]] for h in heads]
        vb = [i_ref[0, rows, cols[h]].astype(BF16) for h in heads]
        f = [lb_ref[:, cols[h]] + (1.0 - lb_ref[:, cols[h]]) * jax.nn.sigmoid(f_ref[0, rows, cols[h]]) for h in heads]
        k = [1.0 - f[h] for h in heads]
        parts = [_split3(jnp.log2(f[h])) for h in heads]
        e_full = [(_dot(mst, parts[h][0]) + _dot(mst, parts[h][1])) + _dot(mst, parts[h][2]) for h in heads]
        b_last = [e_full[h][L - 1:L, :] for h in heads]

        def rel_to(b, blk, off):
            refs = []
            for r0 in range(0, L, blk):
                r = r0 + off - 1
                ref = b[r:r + 1, :] if r >= 0 else jnp.zeros((1, HG_DK), F32)
                refs.append(jnp.broadcast_to(ref, (blk, HG_DK)))
            return b - jnp.concatenate(refs, axis=0)

        def level_sums(b):
            out = [rel_to(b, HG_DIAG, 0)]
            for s_half in HG_LEVELS:
                d = rel_to(b, 2 * s_half, s_half)
                out.append(jnp.where((row_i % (2 * s_half)) >= s_half, d, -d))
            return out

        e = [level_sums(e_full[h]) for h in heads]
        wq = [[jnp.exp2(e[h][l]) for l in range(n_lv + 1)] for h in heads]
        wk = [[jnp.exp2(-e[h][0])] + wq[h][1:] for h in heads]
        prod = [[_dot_nt((q[h] * wq[h][l]).astype(BF16), (k[h] * wk[h][l]).astype(BF16)) for l in range(n_lv + 1)]
                for h in heads]
        st = [st_ref[h] for h in heads]
        inter = [_dot_nt((q[h] * jnp.exp2(e_full[h])).astype(BF16), st[h].astype(BF16)) for h in heads]
        k_dec = [(k[h] * jnp.exp2(b_last[h] - e_full[h])).astype(BF16) for h in heads]
        upd = [_dot_tn(vb[h], k_dec[h]) for h in heads]
        for h in heads:
            st_ref[h] = st[h] * jnp.exp2(b_last[h]) + upd[h]
        a = []
        for h in heads:
            ah = jnp.where(lvl == 1, prod[h][0], 0.0)
            for l in range(1, n_lv + 1):
                ah = jnp.where(lvl == l + 1, prod[h][l], ah)
            a.append(ah.astype(BF16))
        o = [_dot(a[h], vb[h]) + inter[h] for h in heads]
        for h in heads:
            oh = o[h] * lax.rsqrt(jnp.mean(o[h] * o[h], axis=-1, keepdims=True) + EPS) * gain_ref[:, cols[h]]
            o_ref[0, rows, cols[h]] = (oh * jax.nn.silu(g_ref[0, rows, cols[h]])).astype(o_ref.dtype)
        return carry

    lax.fori_loop(0, n_chunks, chunk, 0, unroll=True)


def _hgrn_call(hg, lb, gain, mst, lvl):
    b, t, _ = hg.shape
    tt = HG_TT
    col = lambda k: pl.BlockSpec((1, tt, HG_WIDTH), lambda bi, ti: (bi, ti, k))
    full = lambda a: pl.BlockSpec(a.shape, lambda bi, ti: (0, 0))
    return pl.pallas_call(
        _hgrn_body,
        grid=(b, t // tt),
        in_specs=[col(0), col(1), col(2), col(3), full(lb), full(gain), full(mst), full(lvl)],
        out_specs=pl.BlockSpec((1, tt, HG_WIDTH), lambda bi, ti: (bi, ti, 0)),
        out_shape=jax.ShapeDtypeStruct((b, t, HG_WIDTH), BF16),
        scratch_shapes=[pltpu.VMEM((HG_HEADS, HG_DV, HG_DK), F32)],
        compiler_params=pltpu.CompilerParams(dimension_semantics=("arbitrary", "arbitrary"),
                                             vmem_limit_bytes=VMEM_LIMIT),
        name="hgrn2",
    )(hg, hg, hg, hg, lb, gain, mst, lvl)


def _cmp_body(kcn_ref, vcn_ref, pek_ref, pev_ref, w1k_ref, w1v_ref, w2k_ref, w2v_ref, kc_ref, vc_ref):
    nb = kcn_ref.shape[1] // CMP_STRIDE
    lane_grp = (lax.broadcasted_iota(jnp.int32, (nb, CMP_STRIDE * LANES), 1) // NSA_HEAD_DIM) % NSA_KV_HEADS

    def hidden(src_ref, pe_ref, w1_ref):
        x = jnp.concatenate([src_ref[0, pl.ds(l, nb, stride=CMP_STRIDE), :]
                             for l in range(CMP_STRIDE)], axis=1)
        halves = [x + pe_ref[i] for i in range(2)]
        out = []
        for g in range(NSA_KV_HEADS):
            u, v = (_dot(jnp.where(lane_grp == g, halves[i], 0.0).astype(BF16), w1_ref[i]) for i in range(2))
            out.append(jax.nn.silu(u + pltpu.roll(v, nb - 1, 0)).astype(BF16))
        return out

    hk = hidden(kcn_ref, pek_ref, w1k_ref)
    hv = hidden(vcn_ref, pev_ref, w1v_ref)
    for g in range(NSA_KV_HEADS):
        kc_ref[0, g, 0:nb, :] = _dot(hk[g], w2k_ref[0]).astype(kc_ref.dtype)
        kc_ref[0, g, nb:2 * nb, :] = _dot(hk[g], w2k_ref[1]).astype(kc_ref.dtype)
        vc_ref[0, g, :, 0:nb] = _dot_nt(w2v_ref[0], hv[g]).astype(vc_ref.dtype)
        vc_ref[0, g, :, nb:2 * nb] = _dot_nt(w2v_ref[1], hv[g]).astype(vc_ref.dtype)


def _cmp_call(kcn, vcn, pek, pev, w1k, w1v, w2k, w2v):
    b, t, w = kcn.shape
    nb = t // CMP_STRIDE
    full = lambda a: pl.BlockSpec(a.shape, lambda bi: (0,) * a.ndim)
    out = lambda r, c: pl.BlockSpec((1, NSA_KV_HEADS, r, c), lambda bi: (bi, 0, 0, 0))
    return pl.pallas_call(
        _cmp_body,
        grid=(b,),
        in_specs=[pl.BlockSpec((1, t, w), lambda bi: (bi, 0, 0)), pl.BlockSpec((1, t, w), lambda bi: (bi, 0, 0)),
                  full(pek), full(pev), full(w1k), full(w1v), full(w2k), full(w2v)],
        out_specs=[out(2 * nb, LANES), out(LANES, 2 * nb)],
        out_shape=[jax.ShapeDtypeStruct((b, NSA_KV_HEADS, 2 * nb, LANES), BF16),
                   jax.ShapeDtypeStruct((b, NSA_KV_HEADS, LANES, 2 * nb), BF16)],
        compiler_params=pltpu.CompilerParams(dimension_semantics=("arbitrary",),
                                             vmem_limit_bytes=VMEM_LIMIT),
        name="nsa_compress",
    )(kcn, vcn, pek, pev, w1k, w1v, w2k, w2v)


def _nsa_body(qt_ref, ksw_ref, vst_ref, vwt_ref, kc_ref, vct_ref, gt_ref, gain_ref, ovt_ref, o_ref,
              ks_ref, kw_ref, vs_ref, vw_ref, m_ref, acc_ref, s_ref, ch_ref, ocw_ref):
    g = pl.program_id(1)
    qi = pl.program_id(2)
    tq = ATT_TQ
    tk = ATT_TK
    t_len = ksw_ref.shape[1]
    n_kt = t_len // tk
    n_pairs = HPG // 2
    hd = NSA_HEAD_DIM

    @pl.when(qi == 0)
    def _build_kv():
        lane = lax.broadcasted_iota(jnp.int32, (tk, LANES), 1)
        lo_lane = lane < hd
        keep = (lane // hd) == g

        def build_k(src_col, dst_ref):
            for j in range(n_kt):
                x = ksw_ref[0, j * tk:(j + 1) * tk, src_col * LANES:(src_col + 1) * LANES].astype(F32)
                dup = jnp.where(keep, x, pltpu.roll(x, hd, 1))
                dst_ref[j, 0:tk, :] = jnp.where(lo_lane, dup, 0.0).astype(BF16)
                dst_ref[j, tk:2 * tk, :] = jnp.where(lo_lane, 0.0, dup).astype(BF16)

        def build_v(src_ref, dst_ref):
            zero = jnp.zeros((hd, tk), BF16)
            row = lax.broadcasted_iota(jnp.int32, (SUM_ROWS, 2 * tk), 0)
            col = lax.broadcasted_iota(jnp.int32, (SUM_ROWS, 2 * tk), 1)
            ones_rows = jnp.where(((row == 0) & (col < tk)) | ((row == 1) & (col >= tk)), 1.0, 0.0).astype(BF16)
            for j in range(n_kt):
                x = src_ref[0, :, j * tk:(j + 1) * tk]
                dst_ref[j, 0:hd, 0:tk] = x
                dst_ref[j, 0:hd, tk:2 * tk] = zero
                dst_ref[j, hd:2 * hd, 0:tk] = zero
                dst_ref[j, hd:2 * hd, tk:2 * tk] = x
                dst_ref[j, 2 * hd:2 * hd + SUM_ROWS, :] = ones_rows

        build_k(0, ks_ref)
        build_k(1, kw_ref)
        build_v(vst_ref, vs_ref)
        build_v(vwt_ref, vw_ref)

    t0 = qi * tq
    key_i = lax.broadcasted_iota(jnp.int32, (tk, tq), 0)
    qry_t = t0 + lax.broadcasted_iota(jnp.int32, (tk, tq), 1)
    slab_lo = lax.broadcasted_iota(jnp.int32, (LANES, tq), 0) < hd
    acc_row = lax.broadcasted_iota(jnp.int32, (LANES + SUM_ROWS, tq), 0)
    slab_a = (acc_row < hd) | (acc_row == LANES)
    q_pairs = [qt_ref[0, p * LANES:(p + 1) * LANES, :] for p in range(n_pairs)]

    last = (t0 + tq - 1) // tk

    def scores(k_ref, j):
        kt = k_ref[j]
        return [_dot(kt, q_pairs[p]) for p in range(n_pairs)]

    def live_keys(rel_at_origin, lower, upper):
        out = []
        for qh in range(tq // LANES):
            live = [kb for kb in range(tk // KEY_BLK)
                    if rel_at_origin + qh * LANES + LANES - 1 - kb * KEY_BLK >= lower
                    and rel_at_origin + qh * LANES - (kb * KEY_BLK + KEY_BLK - 1) < upper]
            out.append((min(live) * KEY_BLK, (max(live) + 1) * KEY_BLK))
        return out

    def tile_softmax(slot, bias, m_get, m_put, keys=None):
        alphas = {}
        probs = {}
        for hh in range(HPG):
            p, h = divmod(hh, 2)
            a_parts = []
            p_parts = []
            for qh in range(tq // LANES):
                ql = slice(qh * LANES, (qh + 1) * LANES)
                k_lo, k_hi = keys[qh] if keys is not None else (0, tk)
                sh = (s_ref[slot, p, h * tk + k_lo:h * tk + k_hi, ql]
                      + (bias if bias.shape == (1, 1) else bias[k_lo:k_hi, ql]))
                m_prev = m_get(hh, qh)
                m_new = jnp.maximum(m_prev, jnp.max(sh, axis=0, keepdims=True))
                m_put(hh, qh, m_new)
                piece = [jnp.zeros((k_lo, LANES), BF16)] if k_lo else []
                piece.append(jnp.exp2(sh - m_new).astype(BF16))
                if k_hi < tk:
                    piece.append(jnp.zeros((tk - k_hi, LANES), BF16))
                p_parts.append(jnp.concatenate(piece, axis=0) if len(piece) > 1 else piece[0])
                a_parts.append(jnp.exp2(m_prev - m_new))
            alphas[hh] = jnp.concatenate(a_parts, axis=1)
            probs[hh] = jnp.concatenate(p_parts, axis=1)
        return ([jnp.concatenate([probs[2 * p], probs[2 * p + 1]], axis=0) for p in range(n_pairs)],
                [jnp.where(slab_a, alphas[2 * p], alphas[2 * p + 1]) for p in range(n_pairs)])

    def normalised(acc):
        inv = jnp.where(slab_lo, 1.0 / acc[LANES:LANES + 1, :], 1.0 / acc[LANES + 1:LANES + 2, :])
        return acc[0:LANES] * inv

    def picked_bias(j, also=None):
        per_tile = tk // SLC_BLOCK
        picked = jnp.concatenate([jnp.broadcast_to(ch_ref[j * per_tile + i], (SLC_BLOCK, tq)) for i in range(per_tile)],
                                 axis=0) > 0.5
        return jnp.where(picked if also is None else picked & also, 0.0, NEG_INF)

    def m_put(hh, qh, v):
        m_ref[hh, :, qh * LANES:(qh + 1) * LANES] = v

    n_win = (WINDOW + tq) // tk
    n_near = SLC_NEAR_TILES
    n_far = jnp.maximum(last + 1 - n_near, 0)

    def static_part(first_k, has_far, need_topk):
        n_cmp_pad = kc_ref.shape[2] // 2
        blk_i = lax.broadcasted_iota(jnp.int32, (n_cmp_pad, tq), 0)
        blk_t = t0 + lax.broadcasted_iota(jnp.int32, (n_cmp_pad, tq), 1)
        cmp_ok = (blk_i * CMP_STRIDE + (CMP_BLOCK - 1)) <= blk_t
        kc = kc_ref[0, 0]
        vct = vct_ref[0, 0]
        s_cmp = [_dot(kc, q_pairs[p]) for p in range(n_pairs)]
        win_tiles = [(k, last - (n_win - 1) + k) for k in range(first_k, n_win)]
        near_tiles = [last - k for k in range(n_near)]
        up_front = ([(0, ks_ref, 0)] if has_far else []) + [(1 + k, kw_ref, jw) for k, jw in win_tiles]
        up_front += [(1 + n_win + k, ks_ref, jnp.maximum(jn, 0)) for k, jn in enumerate(near_tiles)]
        for slot, k_ref, j0 in up_front:
            s_first = scores(k_ref, j0)
            for p in range(n_pairs):
                s_ref[slot, p] = s_first[p]
        p_sum = jnp.zeros((n_cmp_pad, tq), F32)
        p_cmp = []
        for p in range(n_pairs):
            probs = []
            for h in range(2):
                sh = jnp.where(cmp_ok, s_cmp[p][h * n_cmp_pad:(h + 1) * n_cmp_pad], NEG_INF)
                mh = jnp.max(sh, axis=0, keepdims=True)
                eh = jnp.where(cmp_ok, jnp.exp2(sh - mh), 0.0)
                den = jnp.sum(eh, axis=0, keepdims=True)
                ph = eh / jnp.where(den > 0.0, den, 1.0)
                p_sum = p_sum + ph
                probs.append(ph.astype(BF16))
            p_cmp.append(jnp.concatenate(probs, axis=0))
        o_cmp = [_dot(vct, p_cmp[p]) for p in range(n_pairs)]

        n_sel = t_len // SLC_BLOCK
        if need_topk:
            hi, mid, lo = _split3(p_sum)
            ovt = ovt_ref[...]
            p_sel = ((_dot(ovt, hi) + _dot(ovt, mid)) + _dot(ovt, lo))[0:n_sel]
            sel_i = lax.broadcasted_iota(jnp.int32, (n_sel, tq), 0)
            cur = (t0 + lax.broadcasted_iota(jnp.int32, (n_sel, tq), 1)) // SLC_BLOCK
            forced = (sel_i == 0) | (sel_i == cur) | (sel_i == cur - 1)
            score = jnp.where(forced, FORCE_SCORE, p_sel)
            score = jnp.where(sel_i <= cur, score, -jnp.inf)
            rank = jnp.zeros((n_sel, tq), jnp.int32)
            row_grp = SUBLANES
            grp_i = lax.broadcasted_iota(jnp.int32, (row_grp, tq), 0)
            for i in range(n_sel):
                ci = score[i:i + 1, :]
                ahead = []
                for r0 in range(0, n_sel, row_grp):
                    rows = slice(r0, r0 + row_grp)
                    if r0 > i:
                        ahead.append(ci >= score[rows])
                    elif r0 + row_grp <= i:
                        ahead.append(ci > score[rows])
                    else:
                        ahead.append((ci > score[rows]) | ((ci == score[rows]) & (grp_i > i - r0)))
                rank = rank + jnp.where(jnp.concatenate(ahead, axis=0), 1, 0)
            chosen = jnp.where(rank < min(SLC_TOPK, n_sel), 1.0, 0.0)
        else:
            sel_i = lax.broadcasted_iota(jnp.int32, (n_sel, tq), 0)
            cur = (t0 + lax.broadcasted_iota(jnp.int32, (n_sel, tq), 1)) // SLC_BLOCK
            chosen = jnp.where(sel_i <= cur, 1.0, 0.0)
        for i in range(n_sel):
            ch_ref[i] = chosen[i:i + 1, :]

        m_win = {}
        acc_win = [jnp.zeros((LANES + SUM_ROWS, tq), F32) for _ in range(n_pairs)]
        for k, jw in win_tiles:
            rel_hi = (n_win - k) * tk - 1
            rel_lo = rel_hi - (tq - 1) - (tk - 1)
            exists = jnp.zeros((1, 1), F32)
            if rel_lo >= 0 and rel_hi < WINDOW:
                bias = exists
            else:
                rel = qry_t - (jw * tk + key_i)
                inside = (rel < WINDOW) if rel_lo >= 0 else (rel >= 0) if rel_hi < WINDOW else (rel >= 0) & (rel < WINDOW)
                bias = jnp.where(inside, exists, NEG_INF)
            probs, a_rows = tile_softmax(1 + k, bias,
                                         lambda hh, qh: m_win.get((hh, qh), jnp.full((1, LANES), -jnp.inf, F32)),
                                         lambda hh, qh, v: m_win.__setitem__((hh, qh), v),
                                         keys=live_keys(rel_lo + tk - 1, 0, WINDOW))
            vt = vw_ref[jw]
            acc_win = [acc_win[p] * a_rows[p] + _dot(vt, probs[p]) for p in range(n_pairs)]
        for p in range(n_pairs):
            ocw_ref[0, p] = o_cmp[p]
            ocw_ref[1, p] = normalised(acc_win[p])

        m_near = {}
        acc_near = [jnp.zeros((LANES + SUM_ROWS, tq), F32) for _ in range(n_pairs)]
        for k, jn in enumerate(near_tiles):
            jc = jnp.maximum(jn, 0)
            causal = (jn * tk + key_i) <= qry_t
            if k > 0:
                causal = causal & (jn >= 0)
            probs, a_rows = tile_softmax(1 + n_win + k, picked_bias(jc, causal),
                                         lambda hh, qh: m_near.get((hh, qh), jnp.full((1, LANES), -jnp.inf, F32)),
                                         lambda hh, qh, v: m_near.__setitem__((hh, qh), v),
                                         keys=live_keys(tk - tq + k * tk, 0, t_len))
            vt = vs_ref[jc]
            acc_near = [acc_near[p] * a_rows[p] + _dot(vt, probs[p]) for p in range(n_pairs)]

        for (hh, qh), v in m_near.items():
            m_put(hh, qh, v)
        for p in range(n_pairs):
            acc_ref[p] = acc_near[p]

    def variant(e):
        tiles_upto = (e + 1) * tq // tk
        return (max(0, n_win - tiles_upto), tiles_upto > n_near, (e + 1) * tq > SLC_TOPK * SLC_BLOCK)

    n_q = t_len // tq
    start = 0
    for e in range(1, n_q + 1):
        if e == n_q or variant(e) != variant(start):
            pl.when((qi >= start) & (qi < e))(functools.partial(static_part, *variant(start)))
            start = e
    o_cmp = [ocw_ref[0, p] for p in range(n_pairs)]
    o_win = [ocw_ref[1, p] for p in range(n_pairs)]

    def slc_step(j, carry):
        s_next = scores(ks_ref, jnp.minimum(j + 1, n_far - 1))
        probs, a_rows = tile_softmax(0, picked_bias(j), lambda hh, qh: m_ref[hh, :, qh * LANES:(qh + 1) * LANES], m_put)
        vt = vs_ref[j]
        for p in range(n_pairs):
            pv = _dot(vt, probs[p])
            s_ref[0, p] = s_next[p]
            acc_ref[p] = acc_ref[p] * a_rows[p] + pv
        return carry

    lax.fori_loop(0, n_far, slc_step, 0)
    o_slc = [normalised(acc_ref[p]) for p in range(n_pairs)]

    gates = gt_ref[0]
    gain = gain_ref[...]
    for p in range(n_pairs):
        o = jnp.zeros((LANES, tq), F32)
        for c, branch in enumerate((o_cmp[p], o_slc[p], o_win[p])):
            r = c * HPG + 2 * p
            o = o + jnp.where(slab_lo, gates[r:r + 1, :], gates[r + 1:r + 2, :]) * branch
        sq = o * o
        ms_a = jnp.sum(sq[0:hd], axis=0, keepdims=True)
        ms_b = jnp.sum(sq[hd:2 * hd], axis=0, keepdims=True)
        ms = jnp.where(slab_lo, ms_a, ms_b) * (1.0 / hd)
        o = o * lax.rsqrt(ms + EPS)
        o_ref[0, p * LANES:(p + 1) * LANES, :] = (o * gain[p * LANES:(p + 1) * LANES, :]).astype(o_ref.dtype)


def _nsa_call(qt, ksw, vt, kc, vct, gt, gain, ovt):
    b, _, t = qt.shape
    tq, tk = ATT_TQ, ATT_TK
    n_kt = t // tk
    gw = HPG * NSA_HEAD_DIM
    hd = NSA_HEAD_DIM
    k_scratch = pltpu.VMEM((n_kt, 2 * tk, LANES), BF16)
    v_scratch = pltpu.VMEM((n_kt, LANES + SUM_ROWS, 2 * tk), BF16)
    return pl.pallas_call(
        _nsa_body,
        grid=(b, NSA_KV_HEADS, t // tq),
        in_specs=[
            pl.BlockSpec((1, gw, tq), lambda bi, gi, qi: (bi, gi, qi)),
            pl.BlockSpec((1, t, 2 * KV_WIDTH), lambda bi, gi, qi: (bi, 0, 0)),
            pl.BlockSpec((1, hd, t), lambda bi, gi, qi: (bi, gi, 0)),
            pl.BlockSpec((1, hd, t), lambda bi, gi, qi: (bi, NSA_KV_HEADS + gi, 0)),
            pl.BlockSpec((1, 1) + kc.shape[2:], lambda bi, gi, qi: (bi, gi, 0, 0)),
            pl.BlockSpec((1, 1) + vct.shape[2:], lambda bi, gi, qi: (bi, gi, 0, 0)),
            pl.BlockSpec((1, LANES, tq), lambda bi, gi, qi: (bi, gi, qi)),
            pl.BlockSpec((gw, 1), lambda bi, gi, qi: (gi, 0)),
            pl.BlockSpec(ovt.shape, lambda bi, gi, qi: (0, 0)),
        ],
        out_specs=pl.BlockSpec((1, gw, tq), lambda bi, gi, qi: (bi, gi, qi)),
        out_shape=jax.ShapeDtypeStruct((b, NSA_WIDTH, t), BF16),
        scratch_shapes=[k_scratch, k_scratch, v_scratch, v_scratch,
                        pltpu.VMEM((HPG, 1, tq), F32),
                        pltpu.VMEM((HPG // 2, LANES + SUM_ROWS, tq), F32), pltpu.VMEM((1 + (WINDOW + tq) // tk + SLC_NEAR_TILES, HPG // 2, 2 * tk, tq), F32),
                        pltpu.VMEM((t // SLC_BLOCK, 1, tq), F32), pltpu.VMEM((2, HPG // 2, LANES, tq), F32)],
        compiler_params=pltpu.CompilerParams(dimension_semantics=("arbitrary", "arbitrary", "arbitrary"),
                                             vmem_limit_bytes=VMEM_LIMIT),
        name="nsa_attention",
    )(qt, ksw, vt, vt, kc, vct, gt, gain, ovt)


def _ffn_body(x_ref, oh_ref, on_ref, woh_ref, won_ref, g2_ref, wg_ref, wu_ref, wd_ref, cw_ref, gf_ref,
              out_ref, halo_ref, act_ref, *, tiles_per_seq):
    tm = x_ref.shape[0]
    x1 = x_ref[...] + _dot(oh_ref[...], woh_ref[...]) + _dot_tn(on_ref[0], won_ref[...])
    hb = _rms(x1, g2_ref[...]).astype(BF16)
    row = lax.broadcasted_iota(jnp.int32, (tm, FFN_TC), 0)

    @pl.when((pl.program_id(0) % tiles_per_seq) == 0)
    def _sequence_start():
        halo_ref[...] = jnp.zeros_like(halo_ref)

    def activation(c, gate, up):
        cols = slice(c * FFN_TC, (c + 1) * FFN_TC)
        halo = halo_ref[:, cols]
        halo_ref[:, cols] = gate[tm - SUBLANES:tm, :]
        last1 = halo[SUBLANES - 1:SUBLANES, :]
        last2 = halo[SUBLANES - 2:SUBLANES - 1, :]
        prev1 = jnp.where(row == 0, last1, pltpu.roll(gate, 1, 0))
        prev2 = jnp.where(row == 0, last2, jnp.where(row == 1, last1, pltpu.roll(gate, 2, 0)))
        cw = cw_ref[:, cols]
        y = cw[0:1, :] * prev2 + cw[1:2, :] * prev1 + cw[2:3, :] * gate + cw[3:4, :]
        return (jax.nn.silu(y) * up).astype(BF16)

    chunk = lambda w_ref, c: _dot(hb, w_ref[:, c * FFN_TC:(c + 1) * FFN_TC].astype(BF16))
    gate_up = (chunk(wg_ref, 0), chunk(wu_ref, 0))
    for c in range(FFN_NC):
        cur = gate_up
        if c + 1 < FFN_NC:
            gate_up = (chunk(wg_ref, c + 1), chunk(wu_ref, c + 1))
        act_ref[:, c * FFN_TC:(c + 1) * FFN_TC] = activation(c, *cur)
    acc = _dot(act_ref[...], wd_ref[...].astype(BF16))
    out_ref[...] = _rms(x1 + acc, gf_ref[...])


def _ffn_call(x2, oh, on, woh, won, g2, wg, wu, wd, cw, gf, tiles_per_seq):
    n = x2.shape[0]
    tm = FFN_TM
    row = lambda w: pl.BlockSpec((tm, w), lambda i: (i, 0))
    full = lambda a: pl.BlockSpec(a.shape, lambda i: (0,) * a.ndim, pipeline_mode=pl.Buffered(1))
    return pl.pallas_call(
        functools.partial(_ffn_body, tiles_per_seq=tiles_per_seq),
        grid=(n // tm,),
        in_specs=[row(D_MODEL), row(HG_WIDTH),
                  pl.BlockSpec((1, NSA_WIDTH, tm), lambda i: (i // tiles_per_seq, 0, i % tiles_per_seq)),
                  full(woh), full(won), full(g2),
                  full(wg), full(wu), full(wd), full(cw), full(gf)],
        out_specs=row(D_MODEL),
        out_shape=jax.ShapeDtypeStruct((n, D_MODEL), F32),
        scratch_shapes=[pltpu.VMEM((SUBLANES, D_FF), F32), pltpu.VMEM((tm, D_FF), BF16)],
        compiler_params=pltpu.CompilerParams(dimension_semantics=("arbitrary",),
                                             vmem_limit_bytes=VMEM_LIMIT),
        name="outproj_convffn",
    )(x2, oh, on, woh, won, g2, wg, wu, wd, cw, gf)


def _rope_angles(positions):
    inv_freq = ROPE_THETA ** (-jnp.arange(ROPE_HALF, dtype=F32) * 2.0 / ROPE_DIM)
    ang = positions.astype(F32)[..., None] * inv_freq
    return jnp.concatenate([jnp.cos(ang), jnp.sin(ang)], axis=-1).transpose(0, 2, 1)


def _layer(x, positions, ln1, w_in, lb, hg_gain, pe_k, pe_v, k_w1, k_w2, v_w1, v_w2, nsa_gain, w_o, ln2,
           w_gate, w_up, conv_w, conv_b, w_down, final_gain):
    b, t, d = x.shape
    n = b * t
    assert d == D_MODEL and t % FFN_TM == 0 and t % PROJ_TM == 0 and t % ATT_TQ == 0 and t % HG_TT == 0
    n_grp = t // CMP_STRIDE
    assert n_grp == LANES, "compressed-block axis is laid out on exactly one lane tile"
    n_sel = t // SLC_BLOCK
    assert n_sel % 8 == 0 and n_sel <= LANES and ATT_TK % SLC_BLOCK == 0 and ATT_TQ % ATT_TK == 0
    x2 = x.reshape(n, d)

    splits = np.cumsum([0, 4 * HG_WIDTH, NSA_WIDTH] + [KV_WIDTH] * 6 + [N_GATES])
    seg = lambda i: w_in[:, splits[i]:splits[i + 1]]
    wk = jnp.concatenate([seg(2), seg(3), seg(4), seg(6)], axis=1).astype(BF16)
    wgate = seg(8).reshape(d, 3, NSA_KV_HEADS, HPG).transpose(0, 2, 1, 3).reshape(d, NSA_KV_HEADS, 3 * HPG)
    wgate = jnp.pad(wgate, ((0, 0), (0, 0), (0, LANES - 3 * HPG))).reshape(d, NSA_KV_HEADS * LANES)
    wt = jnp.concatenate([seg(1), seg(5), seg(7), wgate], axis=1).T.astype(BF16)
    cs = _rope_angles(positions)

    hg, kcn, vcn, ksw, qt, vt, gt = _inproj_call(x2, ln1.reshape(1, d), w_in, wk, wt, cs, t // PROJ_TM)

    mst, lvl = _hgrn_tables()
    o_hg = _hgrn_call(hg.reshape(b, t, 4 * HG_WIDTH), lb.reshape(1, HG_WIDTH).astype(F32),
                      hg_gain.reshape(1, HG_WIDTH), mst, lvl)

    per_lane = lambda a: jnp.broadcast_to(a.reshape(2, CMP_STRIDE, 1, NSA_HEAD_DIM, -1),
                                          (2, CMP_STRIDE, NSA_KV_HEADS, NSA_HEAD_DIM, a.shape[-1]))
    w1_rows = lambda w1: per_lane(w1).reshape(2, CMP_STRIDE * LANES, CMP_HIDDEN).astype(BF16)
    pe_rows = lambda pe: per_lane(pe[..., None]).reshape(2, 1, CMP_STRIDE * LANES)
    zeros_w2 = jnp.zeros((CMP_HIDDEN, NSA_HEAD_DIM), F32)
    place = lambda w2: jnp.stack([jnp.concatenate([w2, zeros_w2], 1), jnp.concatenate([zeros_w2, w2], 1)])
    kc, vct = _cmp_call(kcn.reshape(b, t, KV_WIDTH), vcn.reshape(b, t, KV_WIDTH), pe_rows(pe_k), pe_rows(pe_v),
                        w1_rows(k_w1), w1_rows(v_w1),
                        place(k_w2).astype(BF16), place(v_w2).transpose(0, 2, 1).astype(BF16))

    cmp_start = np.arange(n_grp) * CMP_STRIDE
    cmp_end = cmp_start + CMP_BLOCK - 1
    sel_start = np.arange(LANES) * SLC_BLOCK
    overlap = ((cmp_start[:, None] <= sel_start[None, :] + SLC_BLOCK - 1) & (cmp_end[:, None] >= sel_start[None, :])
               & (np.arange(LANES)[None, :] < n_sel) & (np.arange(n_grp)[:, None] < n_grp - 1))
    ovt = jnp.asarray(overlap.T.astype(np.float32), BF16)
    o_nsa = _nsa_call(qt, ksw.reshape(b, t, 2 * KV_WIDTH), vt, kc, vct, gt, nsa_gain.reshape(NSA_WIDTH, 1), ovt)

    cw = jnp.concatenate([conv_w, conv_b[None, :], jnp.zeros((SUBLANES - CONV_WIDTH - 1, D_FF), F32)], axis=0)
    out = _ffn_call(x2, o_hg.reshape(n, HG_WIDTH), o_nsa,
                    w_o[:HG_WIDTH].astype(BF16), w_o[HG_WIDTH:].astype(BF16), ln2.reshape(1, d),
                    w_gate, w_up, w_down, cw,
                    final_gain.reshape(1, d), t // FFN_TM)
    return out.reshape(b, t, d)


def kernel(x, positions, ln1_gain, w_in, hgrn_lb_param, hgrn_out_gain, cmp_pe_k, cmp_pe_v, cmp_k_w1, cmp_k_w2,
           cmp_v_w1, cmp_v_w2, nsa_out_gain, w_o, ln2_gain, ffn_w_gate, ffn_w_up, ffn_conv_w, ffn_conv_b,
           ffn_w_down, final_gain):
    depth = ln1_gain.shape[0]
    assert depth == 1, "the fused final norm assumes a single layer"
    lower_bounds = jnp.cumsum(jax.nn.softmax(hgrn_lb_param.astype(F32), axis=0), axis=0)
    l = 0
    return _layer(x, positions, ln1_gain[l], w_in[l], lower_bounds[l], hgrn_out_gain[l], cmp_pe_k[l], cmp_pe_v[l],
                  cmp_k_w1[l], cmp_k_w2[l], cmp_v_w1[l], cmp_v_w2[l], nsa_out_gain[l], w_o[l], ln2_gain[l],
                  ffn_w_gate[l], ffn_w_up[l], ffn_conv_w[l], ffn_conv_b[l], ffn_w_down[l], final_gain)
```

```python
import functools

import jax
import jax.numpy as jnp
import numpy as np
from jax import lax
from jax.experimental import pallas as pl
from jax.experimental.pallas import tpu as pltpu

F32 = jnp.float32
BF16 = jnp.bfloat16

D_MODEL = 1024
HG_HEADS = 4
HG_DK = 128
HG_DV = 128
HG_WIDTH = HG_HEADS * HG_DV
NSA_HEADS = 8
NSA_KV_HEADS = 2
NSA_HEAD_DIM = 64
HPG = NSA_HEADS // NSA_KV_HEADS
NSA_WIDTH = NSA_HEADS * NSA_HEAD_DIM
KV_WIDTH = NSA_KV_HEADS * NSA_HEAD_DIM
CMP_BLOCK = 32
CMP_STRIDE = 16
CMP_HIDDEN = 256
SLC_BLOCK = 64
SLC_TOPK = 16
WINDOW = 512
ROPE_THETA = 500000.0
ROPE_DIM = NSA_HEAD_DIM // 4
ROPE_HALF = ROPE_DIM // 2
D_FF = 2816
CONV_WIDTH = 3
EPS = 1e-6
NEG_INF = -1e30
FORCE_SCORE = 1e4
N_GATES = 3 * NSA_HEADS
LOG2_E = 1.4426950408889634

LANES = 128
SUBLANES = 8
VMEM_LIMIT = 56 * 1024 * 1024

PROJ_TM = 1024
HG_CHUNK = 128
HG_LEVELS = (16, 32, 64)
HG_DIAG = 16
HG_TT = 1024
ATT_TQ = 256
ATT_TK = 256
KEY_BLK = 128
SLC_NEAR_TILES = 1
SUM_ROWS = 16
FFN_TM = 512
FFN_TC = 256
FFN_NC = D_FF // FFN_TC


def _dot(a, b):
    return jnp.dot(a, b, preferred_element_type=F32)


def _dot_nt(a, b):
    return lax.dot_general(a, b, (((1,), (1,)), ((), ())), preferred_element_type=F32)


def _dot_tn(a, b):
    return lax.dot_general(a, b, (((0,), (0,)), ((), ())), preferred_element_type=F32)


def _split3(x):
    hi = x.astype(BF16)
    r = x - hi.astype(F32)
    mid = r.astype(BF16)
    lo = (r - mid.astype(F32)).astype(BF16)
    return hi, mid, lo


def _rms(x, gain):
    return x * lax.rsqrt(jnp.mean(x * x, axis=-1, keepdims=True) + EPS) * gain


def _inproj_body(x_ref, g_ref, wh_ref, wk_ref, wt_ref, cs_ref,
                 hg_ref, kcn_ref, vcn_ref, ksw_ref, qt_ref, vt_ref, gt_ref):
    hb = _rms(x_ref[...], g_ref[...]).astype(BF16)
    hg_ref[...] = _dot(hb, wh_ref[...])

    def rope(v, axis, cos, sin_hi, sin_lo):
        return (v * cos + pltpu.roll(v, ROPE_HALF, axis) * sin_hi
                + pltpu.roll(v, LANES - ROPE_HALF, axis) * sin_lo)

    cos = cs_ref[0, 0:ROPE_HALF, :]
    sin = cs_ref[0, ROPE_HALF:ROPE_DIM, :]
    tm = cos.shape[1]
    zero_h = jnp.zeros((ROPE_HALF, tm), F32)
    rest = NSA_HEAD_DIM - ROPE_DIM
    slab = lambda lo, hi, fill: jnp.concatenate([lo, hi, jnp.full((rest, tm), fill, F32)] * (LANES // NSA_HEAD_DIM), axis=0)
    tab_t = (slab(cos, cos, 1.0), slab(zero_h, sin, 0.0), slab(-sin, zero_h, 0.0))
    tab = tuple(a.T for a in tab_t)
    kn = _dot(hb, wk_ref[...])
    kcn_ref[...] = rope(kn[:, 0:LANES], 1, *tab)
    vcn_ref[...] = kn[:, LANES:2 * LANES]
    ksw_ref[:, 0:LANES] = rope(kn[:, 2 * LANES:3 * LANES], 1, *tab).astype(BF16)
    ksw_ref[:, LANES:2 * LANES] = rope(kn[:, 3 * LANES:4 * LANES], 1, *tab).astype(BF16)

    rt = _dot_nt(wt_ref[...], hb)
    scale = NSA_HEAD_DIM ** -0.5 * LOG2_E
    for j in range(NSA_WIDTH // LANES):
        sl = slice(j * LANES, (j + 1) * LANES)
        qt_ref[0, sl, :] = (rope(rt[sl], 0, *tab_t) * scale).astype(BF16)
    vt_ref[0] = rt[NSA_WIDTH:NSA_WIDTH + 2 * KV_WIDTH].astype(BF16)
    gt_ref[0] = jax.nn.sigmoid(rt[NSA_WIDTH + 2 * KV_WIDTH:])


def _inproj_call(x2, gain, wh, wk, wt, cs, tiles_per_seq):
    n = x2.shape[0]
    tm = PROJ_TM
    t = tiles_per_seq * tm
    b = n // t
    row = lambda w: pl.BlockSpec((tm, w), lambda i: (i, 0))
    col = lambda h: pl.BlockSpec((1, h, tm), lambda i: (i // tiles_per_seq, 0, i % tiles_per_seq))
    full = lambda a: pl.BlockSpec(a.shape, lambda i: (0, 0))
    gate_rows = NSA_KV_HEADS * LANES
    return pl.pallas_call(
        _inproj_body,
        grid=(n // tm,),
        in_specs=[row(D_MODEL), full(gain), full(wh), full(wk), full(wt),
                  col(ROPE_DIM)],
        out_specs=[row(4 * HG_WIDTH), row(KV_WIDTH), row(KV_WIDTH), row(2 * KV_WIDTH),
                   col(NSA_WIDTH), col(2 * KV_WIDTH), col(gate_rows)],
        out_shape=[jax.ShapeDtypeStruct((n, 4 * HG_WIDTH), F32),
                   jax.ShapeDtypeStruct((n, KV_WIDTH), F32),
                   jax.ShapeDtypeStruct((n, KV_WIDTH), F32),
                   jax.ShapeDtypeStruct((n, 2 * KV_WIDTH), BF16),
                   jax.ShapeDtypeStruct((b, NSA_WIDTH, t), BF16),
                   jax.ShapeDtypeStruct((b, 2 * KV_WIDTH, t), BF16),
                   jax.ShapeDtypeStruct((b, gate_rows, t), F32)],
        compiler_params=pltpu.CompilerParams(dimension_semantics=("arbitrary",),
                                             vmem_limit_bytes=VMEM_LIMIT),
        name="inproj",
    )(x2, gain, wh, wk, wt, cs)


def _hgrn_tables():
    L = HG_CHUNK
    t = np.arange(L)[:, None]
    u = np.arange(L)[None, :]
    level = np.where(((t // HG_DIAG) == (u // HG_DIAG)) & (u <= t), 1, 0)
    for li, s in enumerate(HG_LEVELS):
        same = (t // (2 * s)) == (u // (2 * s))
        right = (t % (2 * s)) >= s
        level = np.where(same & right & ((u % (2 * s)) < s), li + 2, level)
    return jnp.asarray((u <= t).astype(np.float32), BF16), jnp.asarray(level, jnp.int32)


def _hgrn_body(q_ref, f_ref, i_ref, g_ref, lb_ref, gain_ref, mst_ref, lvl_ref, o_ref, st_ref):
    L = HG_CHUNK
    n_chunks = q_ref.shape[1] // L

    @pl.when(pl.program_id(1) == 0)
    def _sequence_start():
        st_ref[...] = jnp.zeros_like(st_ref)

    def chunk(c, carry):
        rows = pl.ds(pl.multiple_of(c * L, L), L)
        heads = range(HG_HEADS)
        cols = [slice(h * HG_DK, (h + 1) * HG_DK) for h in heads]
        mst = mst_ref[...]
        lvl = lvl_ref[...]
        n_lv = len(HG_LEVELS)
        row_i = lax.broadcasted_iota(jnp.int32, (L, HG_DK), 0)
        q = [q_ref[0, rows, cols[h]] for h in heads]
        vb = [i_ref[0, rows, cols[h]].astype(BF16) for h in heads]
        f = [lb_ref[:, cols[h]] + (1.0 - lb_ref[:, cols[h]]) * jax.nn.sigmoid(f_ref[0, rows, cols[h]]) for h in heads]
        k = [1.0 - f[h] for h in heads]
        parts = [_split3(jnp.log2(f[h])) for h in heads]
        e_full = [(_dot(mst, parts[h][0]) + _dot(mst, parts[h][1])) + _dot(mst, parts[h][2]) for h in heads]
        b_last = [e_full[h][L - 1:L, :] for h in heads]

        def rel_to(b, blk, off):
            refs = []
            for r0 in range(0, L, blk):
                r = r0 + off - 1
                ref = b[r:r + 1, :] if r >= 0 else jnp.zeros((1, HG_DK), F32)
                refs.append(jnp.broadcast_to(ref, (blk, HG_DK)))
            return b - jnp.concatenate(refs, axis=0)

        def level_sums(b):
            out = [rel_to(b, HG_DIAG, 0)]
            for s_half in HG_LEVELS:
                d = rel_to(b, 2 * s_half, s_half)
                out.append(jnp.where((row_i % (2 * s_half)) >= s_half, d, -d))
            return out

        e = [level_sums(e_full[h]) for h in heads]
        wq = [[jnp.exp2(e[h][l]) for l in range(n_lv + 1)] for h in heads]
        wk = [[jnp.exp2(-e[h][0])] + wq[h][1:] for h in heads]
        prod = [[_dot_nt((q[h] * wq[h][l]).astype(BF16), (k[h] * wk[h][l]).astype(BF16)) for l in range(n_lv + 1)]
                for h in heads]
        st = [st_ref[h] for h in heads]
        inter = [_dot_nt((q[h] * jnp.exp2(e_full[h])).astype(BF16), st[h].astype(BF16)) for h in heads]
        k_dec = [(k[h] * jnp.exp2(b_last[h] - e_full[h])).astype(BF16) for h in heads]
        upd = [_dot_tn(vb[h], k_dec[h]) for h in heads]
        for h in heads:
            st_ref[h] = st[h] * jnp.exp2(b_last[h]) + upd[h]
        a = []
        for h in heads:
            ah = jnp.where(lvl == 1, prod[h][0], 0.0)
            for l in range(1, n_lv + 1):
                ah = jnp.where(lvl == l + 1, prod[h][l], ah)
            a.append(ah.astype(BF16))
        o = [_dot(a[h], vb[h]) + inter[h] for h in heads]
        for h in heads:
            oh = o[h] * lax.rsqrt(jnp.mean(o[h] * o[h], axis=-1, keepdims=True) + EPS) * gain_ref[:, cols[h]]
            o_ref[0, rows, cols[h]] = (oh * jax.nn.silu(g_ref[0, rows, cols[h]])).astype(o_ref.dtype)
        return carry

    lax.fori_loop(0, n_chunks, chunk, 0, unroll=True)


def _hgrn_call(hg, lb, gain, mst, lvl):
    b, t, _ = hg.shape
    tt = HG_TT
    col = lambda k: pl.BlockSpec((1, tt, HG_WIDTH), lambda bi, ti: (bi, ti, k))
    full = lambda a: pl.BlockSpec(a.shape, lambda bi, ti: (0, 0))
    return pl.pallas_call(
        _hgrn_body,
        grid=(b, t // tt),
        in_specs=[col(0), col(1), col(2), col(3), full(lb), full(gain), full(mst), full(lvl)],
        out_specs=pl.BlockSpec((1, tt, HG_WIDTH), lambda bi, ti: (bi, ti, 0)),
        out_shape=jax.ShapeDtypeStruct((b, t, HG_WIDTH), BF16),
        scratch_shapes=[pltpu.VMEM((HG_HEADS, HG_DV, HG_DK), F32)],
        compiler_params=pltpu.CompilerParams(dimension_semantics=("arbitrary", "arbitrary"),
                                             vmem_limit_bytes=VMEM_LIMIT),
        name="hgrn2",
    )(hg, hg, hg, hg, lb, gain, mst, lvl)


def _cmp_body(kcn_ref, vcn_ref, pek_ref, pev_ref, w1k_ref, w1v_ref, w2k_ref, w2v_ref, kc_ref, vc_ref):
    nb = kcn_ref.shape[1] // CMP_STRIDE
    lane_grp = (lax.broadcasted_iota(jnp.int32, (nb, CMP_STRIDE * LANES), 1) // NSA_HEAD_DIM) % NSA_KV_HEADS

    def hidden(src_ref, pe_ref, w1_ref):
        x = jnp.concatenate([src_ref[0, pl.ds(l, nb, stride=CMP_STRIDE), :]
                             for l in range(CMP_STRIDE)], axis=1)
        halves = [x + pe_ref[i] for i in range(2)]
        out = []
        for g in range(NSA_KV_HEADS):
            u, v = (_dot(jnp.where(lane_grp == g, halves[i], 0.0).astype(BF16), w1_ref[i]) for i in range(2))
            out.append(jax.nn.silu(u + pltpu.roll(v, nb - 1, 0)).astype(BF16))
        return out

    hk = hidden(kcn_ref, pek_ref, w1k_ref)
    hv = hidden(vcn_ref, pev_ref, w1v_ref)
    for g in range(NSA_KV_HEADS):
        kc_ref[0, g, 0:nb, :] = _dot(hk[g], w2k_ref[0]).astype(kc_ref.dtype)
        kc_ref[0, g, nb:2 * nb, :] = _dot(hk[g], w2k_ref[1]).astype(kc_ref.dtype)
        vc_ref[0, g, :, 0:nb] = _dot_nt(w2v_ref[0], hv[g]).astype(vc_ref.dtype)
        vc_ref[0, g, :, nb:2 * nb] = _dot_nt(w2v_ref[1], hv[g]).astype(vc_ref.dtype)


def _cmp_call(kcn, vcn, pek, pev, w1k, w1v, w2k, w2v):
    b, t, w = kcn.shape
    nb = t // CMP_STRIDE
    full = lambda a: pl.BlockSpec(a.shape, lambda bi: (0,) * a.ndim)
    out = lambda r, c: pl.BlockSpec((1, NSA_KV_HEADS, r, c), lambda bi: (bi, 0, 0, 0))
    return pl.pallas_call(
        _cmp_body,
        grid=(b,),
        in_specs=[pl.BlockSpec((1, t, w), lambda bi: (bi, 0, 0)), pl.BlockSpec((1, t, w), lambda bi: (bi, 0, 0)),
                  full(pek), full(pev), full(w1k), full(w1v), full(w2k), full(w2v)],
        out_specs=[out(2 * nb, LANES), out(LANES, 2 * nb)],
        out_shape=[jax.ShapeDtypeStruct((b, NSA_KV_HEADS, 2 * nb, LANES), BF16),
                   jax.ShapeDtypeStruct((b, NSA_KV_HEADS, LANES, 2 * nb), BF16)],
        compiler_params=pltpu.CompilerParams(dimension_semantics=("arbitrary",),
                                             vmem_limit_bytes=VMEM_LIMIT),
        name="nsa_compress",
    )(kcn, vcn, pek, pev, w1k, w1v, w2k, w2v)


def _nsa_body(qt_ref, ksw_ref, vst_ref, vwt_ref, kc_ref, vct_ref, gt_ref, gain_ref, ovt_ref, o_ref,
              ks_ref, kw_ref, vs_ref, vw_ref, m_ref, acc_ref, s_ref, ch_ref, ocw_ref):
    g = pl.program_id(1)
    qi = pl.program_id(2)
    tq = ATT_TQ
    tk = ATT_TK
    t_len = ksw_ref.shape[1]
    n_kt = t_len // tk
    n_pairs = HPG // 2
    hd = NSA_HEAD_DIM

    @pl.when(qi == 0)
    def _build_kv():
        lane = lax.broadcasted_iota(jnp.int32, (tk, LANES), 1)
        lo_lane = lane < hd
        keep = (lane // hd) == g

        def build_k(src_col, dst_ref):
            for j in range(n_kt):
                x = ksw_ref[0, j * tk:(j + 1) * tk, src_col * LANES:(src_col + 1) * LANES].astype(F32)
                dup = jnp.where(keep, x, pltpu.roll(x, hd, 1))
                dst_ref[j, 0:tk, :] = jnp.where(lo_lane, dup, 0.0).astype(BF16)
                dst_ref[j, tk:2 * tk, :] = jnp.where(lo_lane, 0.0, dup).astype(BF16)

        def build_v(src_ref, dst_ref):
            zero = jnp.zeros((hd, tk), BF16)
            row = lax.broadcasted_iota(jnp.int32, (SUM_ROWS, 2 * tk), 0)
            col = lax.broadcasted_iota(jnp.int32, (SUM_ROWS, 2 * tk), 1)
            ones_rows = jnp.where(((row == 0) & (col < tk)) | ((row == 1) & (col >= tk)), 1.0, 0.0).astype(BF16)
            for j in range(n_kt):
                x = src_ref[0, :, j * tk:(j + 1) * tk]
                dst_ref[j, 0:hd, 0:tk] = x
                dst_ref[j, 0:hd, tk:2 * tk] = zero
                dst_ref[j, hd:2 * hd, 0:tk] = zero
                dst_ref[j, hd:2 * hd, tk:2 * tk] = x
                dst_ref[j, 2 * hd:2 * hd + SUM_ROWS, :] = ones_rows

        build_k(0, ks_ref)
        build_k(1, kw_ref)
        build_v(vst_ref, vs_ref)
        build_v(vwt_ref, vw_ref)

    t0 = qi * tq
    key_i = lax.broadcasted_iota(jnp.int32, (tk, tq), 0)
    qry_t = t0 + lax.broadcasted_iota(jnp.int32, (tk, tq), 1)
    slab_lo = lax.broadcasted_iota(jnp.int32, (LANES, tq), 0) < hd
    acc_row = lax.broadcasted_iota(jnp.int32, (LANES + SUM_ROWS, tq), 0)
    slab_a = (acc_row < hd) | (acc_row == LANES)
    q_pairs = [qt_ref[0, p * LANES:(p + 1) * LANES, :] for p in range(n_pairs)]

    last = (t0 + tq - 1) // tk

    def scores(k_ref, j):
        kt = k_ref[j]
        return [_dot(kt, q_pairs[p]) for p in range(n_pairs)]

    def live_keys(rel_at_origin, lower, upper):
        out = []
        for qh in range(tq // LANES):
            live = [kb for kb in range(tk // KEY_BLK)
                    if rel_at_origin + qh * LANES + LANES - 1 - kb * KEY_BLK >= lower
                    and rel_at_origin + qh * LANES - (kb * KEY_BLK + KEY_BLK - 1) < upper]
            out.append((min(live) * KEY_BLK, (max(live) + 1) * KEY_BLK))
        return out

    def tile_softmax(slot, bias, m_get, m_put, keys=None):
        alphas = {}
        probs = {}
        for hh in range(HPG):
            p, h = divmod(hh, 2)
            a_parts = []
            p_parts = []
            for qh in range(tq // LANES):
                ql = slice(qh * LANES, (qh + 1) * LANES)
                k_lo, k_hi = keys[qh] if keys is not None else (0, tk)
                sh = (s_ref[slot, p, h * tk + k_lo:h * tk + k_hi, ql]
                      + (bias if bias.shape == (1, 1) else bias[k_lo:k_hi, ql]))
                m_prev = m_get(hh, qh)
                m_new = jnp.maximum(m_prev, jnp.max(sh, axis=0, keepdims=True))
                m_put(hh, qh, m_new)
                piece = [jnp.zeros((k_lo, LANES), BF16)] if k_lo else []
                piece.append(jnp.exp2(sh - m_new).astype(BF16))
                if k_hi < tk:
                    piece.append(jnp.zeros((tk - k_hi, LANES), BF16))
                p_parts.append(jnp.concatenate(piece, axis=0) if len(piece) > 1 else piece[0])
                a_parts.append(jnp.exp2(m_prev - m_new))
            alphas[hh] = jnp.concatenate(a_parts, axis=1)
            probs[hh] = jnp.concatenate(p_parts, axis=1)
        return ([jnp.concatenate([probs[2 * p], probs[2 * p + 1]], axis=0) for p in range(n_pairs)],
                [jnp.where(slab_a, alphas[2 * p], alphas[2 * p + 1]) for p in range(n_pairs)])

    def normalised(acc):
        inv = jnp.where(slab_lo, 1.0 / acc[LANES:LANES + 1, :], 1.0 / acc[LANES + 1:LANES + 2, :])
        return acc[0:LANES] * inv

    def picked_bias(j, also=None):
        per_tile = tk // SLC_BLOCK
        picked = jnp.concatenate([jnp.broadcast_to(ch_ref[j * per_tile + i], (SLC_BLOCK, tq)) for i in range(per_tile)],
                                 axis=0) > 0.5
        return jnp.where(picked if also is None else picked & also, 0.0, NEG_INF)

    def m_put(hh, qh, v):
        m_ref[hh, :, qh * LANES:(qh + 1) * LANES] = v

    n_win = (WINDOW + tq) // tk
    n_near = SLC_NEAR_TILES
    n_far = jnp.maximum(last + 1 - n_near, 0)

    def static_part(first_k, has_far, need_topk):
        n_cmp_pad = kc_ref.shape[2] // 2
        blk_i = lax.broadcasted_iota(jnp.int32, (n_cmp_pad, tq), 0)
        blk_t = t0 + lax.broadcasted_iota(jnp.int32, (n_cmp_pad, tq), 1)
        cmp_ok = (blk_i * CMP_STRIDE + (CMP_BLOCK - 1)) <= blk_t
        kc = kc_ref[0, 0]
        vct = vct_ref[0, 0]
        s_cmp = [_dot(kc, q_pairs[p]) for p in range(n_pairs)]
        win_tiles = [(k, last - (n_win - 1) + k) for k in range(first_k, n_win)]
        near_tiles = [last - k for k in range(n_near)]
        up_front = ([(0, ks_ref, 0)] if has_far else []) + [(1 + k, kw_ref, jw) for k, jw in win_tiles]
        up_front += [(1 + n_win + k, ks_ref, jnp.maximum(jn, 0)) for k, jn in enumerate(near_tiles)]
        for slot, k_ref, j0 in up_front:
            s_first = scores(k_ref, j0)
            for p in range(n_pairs):
                s_ref[slot, p] = s_first[p]
        p_sum = jnp.zeros((n_cmp_pad, tq), F32)
        p_cmp = []
        for p in range(n_pairs):
            probs = []
            for h in range(2):
                sh = jnp.where(cmp_ok, s_cmp[p][h * n_cmp_pad:(h + 1) * n_cmp_pad], NEG_INF)
                mh = jnp.max(sh, axis=0, keepdims=True)
                eh = jnp.where(cmp_ok, jnp.exp2(sh - mh), 0.0)
                den = jnp.sum(eh, axis=0, keepdims=True)
                ph = eh / jnp.where(den > 0.0, den, 1.0)
                p_sum = p_sum + ph
                probs.append(ph.astype(BF16))
            p_cmp.append(jnp.concatenate(probs, axis=0))
        o_cmp = [_dot(vct, p_cmp[p]) for p in range(n_pairs)]

        n_sel = t_len // SLC_BLOCK
        if need_topk:
            hi, mid, lo = _split3(p_sum)
            ovt = ovt_ref[...]
            p_sel = ((_dot(ovt, hi) + _dot(ovt, mid)) + _dot(ovt, lo))[0:n_sel]
            sel_i = lax.broadcasted_iota(jnp.int32, (n_sel, tq), 0)
            cur = (t0 + lax.broadcasted_iota(jnp.int32, (n_sel, tq), 1)) // SLC_BLOCK
            forced = (sel_i == 0) | (sel_i == cur) | (sel_i == cur - 1)
            score = jnp.where(forced, FORCE_SCORE, p_sel)
            score = jnp.where(sel_i <= cur, score, -jnp.inf)
            rank = jnp.zeros((n_sel, tq), jnp.int32)
            row_grp = SUBLANES
            grp_i = lax.broadcasted_iota(jnp.int32, (row_grp, tq), 0)
            for i in range(n_sel):
                ci = score[i:i + 1, :]
                ahead = []
                for r0 in range(0, n_sel, row_grp):
                    rows = slice(r0, r0 + row_grp)
                    if r0 > i:
                        ahead.append(ci >= score[rows])
                    elif r0 + row_grp <= i:
                        ahead.append(ci > score[rows])
                    else:
                        ahead.append((ci > score[rows]) | ((ci == score[rows]) & (grp_i > i - r0)))
                rank = rank + jnp.where(jnp.concatenate(ahead, axis=0), 1, 0)
            chosen = jnp.where(rank < min(SLC_TOPK, n_sel), 1.0, 0.0)
        else:
            sel_i = lax.broadcasted_iota(jnp.int32, (n_sel, tq), 0)
            cur = (t0 + lax.broadcasted_iota(jnp.int32, (n_sel, tq), 1)) // SLC_BLOCK
            chosen = jnp.where(sel_i <= cur, 1.0, 0.0)
        for i in range(n_sel):
            ch_ref[i] = chosen[i:i + 1, :]

        m_win = {}
        acc_win = [jnp.zeros((LANES + SUM_ROWS, tq), F32) for _ in range(n_pairs)]
        for k, jw in win_tiles:
            rel_hi = (n_win - k) * tk - 1
            rel_lo = rel_hi - (tq - 1) - (tk - 1)
            exists = jnp.zeros((1, 1), F32)
            if rel_lo >= 0 and rel_hi < WINDOW:
                bias = exists
            else:
                rel = qry_t - (jw * tk + key_i)
                inside = (rel < WINDOW) if rel_lo >= 0 else (rel >= 0) if rel_hi < WINDOW else (rel >= 0) & (rel < WINDOW)
                bias = jnp.where(inside, exists, NEG_INF)
            probs, a_rows = tile_softmax(1 + k, bias,
                                         lambda hh, qh: m_win.get((hh, qh), jnp.full((1, LANES), -jnp.inf, F32)),
                                         lambda hh, qh, v: m_win.__setitem__((hh, qh), v),
                                         keys=live_keys(rel_lo + tk - 1, 0, WINDOW))
            vt = vw_ref[jw]
            acc_win = [acc_win[p] * a_rows[p] + _dot(vt, probs[p]) for p in range(n_pairs)]
        for p in range(n_pairs):
            ocw_ref[0, p] = o_cmp[p]
            ocw_ref[1, p] = normalised(acc_win[p])

        m_near = {}
        acc_near = [jnp.zeros((LANES + SUM_ROWS, tq), F32) for _ in range(n_pairs)]
        for k, jn in enumerate(near_tiles):
            jc = jnp.maximum(jn, 0)
            causal = (jn * tk + key_i) <= qry_t
            if k > 0:
                causal = causal & (jn >= 0)
            probs, a_rows = tile_softmax(1 + n_win + k, picked_bias(jc, causal),
                                         lambda hh, qh: m_near.get((hh, qh), jnp.full((1, LANES), -jnp.inf, F32)),
                                         lambda hh, qh, v: m_near.__setitem__((hh, qh), v),
                                         keys=live_keys(tk - tq + k * tk, 0, t_len))
            vt = vs_ref[jc]
            acc_near = [acc_near[p] * a_rows[p] + _dot(vt, probs[p]) for p in range(n_pairs)]

        for (hh, qh), v in m_near.items():
            m_put(hh, qh, v)
        for p in range(n_pairs):
            acc_ref[p] = acc_near[p]

    def variant(e):
        tiles_upto = (e + 1) * tq // tk
        return (max(0, n_win - tiles_upto), tiles_upto > n_near, (e + 1) * tq > SLC_TOPK * SLC_BLOCK)

    n_q = t_len // tq
    start = 0
    for e in range(1, n_q + 1):
        if e == n_q or variant(e) != variant(start):
            pl.when((qi >= start) & (qi < e))(functools.partial(static_part, *variant(start)))
            start = e
    o_cmp = [ocw_ref[0, p] for p in range(n_pairs)]
    o_win = [ocw_ref[1, p] for p in range(n_pairs)]

    def slc_step(j, carry):
        s_next = scores(ks_ref, jnp.minimum(j + 1, n_far - 1))
        probs, a_rows = tile_softmax(0, picked_bias(j), lambda hh, qh: m_ref[hh, :, qh * LANES:(qh + 1) * LANES], m_put)
        vt = vs_ref[j]
        for p in range(n_pairs):
            pv = _dot(vt, probs[p])
            s_ref[0, p] = s_next[p]
            acc_ref[p] = acc_ref[p] * a_rows[p] + pv
        return carry

    lax.fori_loop(0, n_far, slc_step, 0)
    o_slc = [normalised(acc_ref[p]) for p in range(n_pairs)]

    gates = gt_ref[0]
    gain = gain_ref[...]
    for p in range(n_pairs):
        o = jnp.zeros((LANES, tq), F32)
        for c, branch in enumerate((o_cmp[p], o_slc[p], o_win[p])):
            r = c * HPG + 2 * p
            o = o + jnp.where(slab_lo, gates[r:r + 1, :], gates[r + 1:r + 2, :]) * branch
        sq = o * o
        ms_a = jnp.sum(sq[0:hd], axis=0, keepdims=True)
        ms_b = jnp.sum(sq[hd:2 * hd], axis=0, keepdims=True)
        ms = jnp.where(slab_lo, ms_a, ms_b) * (1.0 / hd)
        o = o * lax.rsqrt(ms + EPS)
        o_ref[0, p * LANES:(p + 1) * LANES, :] = (o * gain[p * LANES:(p + 1) * LANES, :]).astype(o_ref.dtype)


def _nsa_call(qt, ksw, vt, kc, vct, gt, gain, ovt):
    b, _, t = qt.shape
    tq, tk = ATT_TQ, ATT_TK
    n_kt = t // tk
    gw = HPG * NSA_HEAD_DIM
    hd = NSA_HEAD_DIM
    k_scratch = pltpu.VMEM((n_kt, 2 * tk, LANES), BF16)
    v_scratch = pltpu.VMEM((n_kt, LANES + SUM_ROWS, 2 * tk), BF16)
    return pl.pallas_call(
        _nsa_body,
        grid=(b, NSA_KV_HEADS, t // tq),
        in_specs=[
            pl.BlockSpec((1, gw, tq), lambda bi, gi, qi: (bi, gi, qi)),
            pl.BlockSpec((1, t, 2 * KV_WIDTH), lambda bi, gi, qi: (bi, 0, 0)),
            pl.BlockSpec((1, hd, t), lambda bi, gi, qi: (bi, gi, 0)),
            pl.BlockSpec((1, hd, t), lambda bi, gi, qi: (bi, NSA_KV_HEADS + gi, 0)),
            pl.BlockSpec((1, 1) + kc.shape[2:], lambda bi, gi, qi: (bi, gi, 0, 0)),
            pl.BlockSpec((1, 1) + vct.shape[2:], lambda bi, gi, qi: (bi, gi, 0, 0)),
            pl.BlockSpec((1, LANES, tq), lambda bi, gi, qi: (bi, gi, qi)),
            pl.BlockSpec((gw, 1), lambda bi, gi, qi: (gi, 0)),
            pl.BlockSpec(ovt.shape, lambda bi, gi, qi: (0, 0)),
        ],
        out_specs=pl.BlockSpec((1, gw, tq), lambda bi, gi, qi: (bi, gi, qi)),
        out_shape=jax.ShapeDtypeStruct((b, NSA_WIDTH, t), BF16),
        scratch_shapes=[k_scratch, k_scratch, v_scratch, v_scratch,
                        pltpu.VMEM((HPG, 1, tq), F32),
                        pltpu.VMEM((HPG // 2, LANES + SUM_ROWS, tq), F32), pltpu.VMEM((1 + (WINDOW + tq) // tk + SLC_NEAR_TILES, HPG // 2, 2 * tk, tq), F32),
                        pltpu.VMEM((t // SLC_BLOCK, 1, tq), F32), pltpu.VMEM((2, HPG // 2, LANES, tq), F32)],
        compiler_params=pltpu.CompilerParams(dimension_semantics=("arbitrary", "arbitrary", "arbitrary"),
                                             vmem_limit_bytes=VMEM_LIMIT),
        name="nsa_attention",
    )(qt, ksw, vt, vt, kc, vct, gt, gain, ovt)


def _ffn_body(x_ref, oh_ref, on_ref, woh_ref, won_ref, g2_ref, wg_ref, wu_ref, wd_ref, cw_ref, gf_ref,
              out_ref, halo_ref, act_ref, *, tiles_per_seq):
    tm = x_ref.shape[0]
    x1 = x_ref[...] + _dot(oh_ref[...], woh_ref[...]) + _dot_tn(on_ref[0], won_ref[...])
    hb = _rms(x1, g2_ref[...]).astype(BF16)
    row = lax.broadcasted_iota(jnp.int32, (tm, FFN_TC), 0)

    @pl.when((pl.program_id(0) % tiles_per_seq) == 0)
    def _sequence_start():
        halo_ref[...] = jnp.zeros_like(halo_ref)

    def activation(c, gate, up):
        cols = slice(c * FFN_TC, (c + 1) * FFN_TC)
        halo = halo_ref[:, cols]
        halo_ref[:, cols] = gate[tm - SUBLANES:tm, :]
        last1 = halo[SUBLANES - 1:SUBLANES, :]
        last2 = halo[SUBLANES - 2:SUBLANES - 1, :]
        prev1 = jnp.where(row == 0, last1, pltpu.roll(gate, 1, 0))
        prev2 = jnp.where(row == 0, last2, jnp.where(row == 1, last1, pltpu.roll(gate, 2, 0)))
        cw = cw_ref[:, cols]
        y = cw[0:1, :] * prev2 + cw[1:2, :] * prev1 + cw[2:3, :] * gate + cw[3:4, :]
        return (jax.nn.silu(y) * up).astype(BF16)

    chunk = lambda w_ref, c: _dot(hb, w_ref[:, c * FFN_TC:(c + 1) * FFN_TC].astype(BF16))
    gate_up = (chunk(wg_ref, 0), chunk(wu_ref, 0))
    for c in range(FFN_NC):
        cur = gate_up
        if c + 1 < FFN_NC:
            gate_up = (chunk(wg_ref, c + 1), chunk(wu_ref, c + 1))
        act_ref[:, c * FFN_TC:(c + 1) * FFN_TC] = activation(c, *cur)
    acc = _dot(act_ref[...], wd_ref[...].astype(BF16))
    out_ref[...] = _rms(x1 + acc, gf_ref[...])


def _ffn_call(x2, oh, on, woh, won, g2, wg, wu, wd, cw, gf, tiles_per_seq):
    n = x2.shape[0]
    tm = FFN_TM
    row = lambda w: pl.BlockSpec((tm, w), lambda i: (i, 0))
    full = lambda a: pl.BlockSpec(a.shape, lambda i: (0,) * a.ndim, pipeline_mode=pl.Buffered(1))
    return pl.pallas_call(
        functools.partial(_ffn_body, tiles_per_seq=tiles_per_seq),
        grid=(n // tm,),
        in_specs=[row(D_MODEL), row(HG_WIDTH),
                  pl.BlockSpec((1, NSA_WIDTH, tm), lambda i: (i // tiles_per_seq, 0, i % tiles_per_seq)),
                  full(woh), full(won), full(g2),
                  full(wg), full(wu), full(wd), full(cw), full(gf)],
        out_specs=row(D_MODEL),
        out_shape=jax.ShapeDtypeStruct((n, D_MODEL), F32),
        scratch_shapes=[pltpu.VMEM((SUBLANES, D_FF), F32), pltpu.VMEM((tm, D_FF), BF16)],
        compiler_params=pltpu.CompilerParams(dimension_semantics=("arbitrary",),
                                             vmem_limit_bytes=VMEM_LIMIT),
        name="outproj_convffn",
    )(x2, oh, on, woh, won, g2, wg, wu, wd, cw, gf)


def _rope_angles(positions):
    inv_freq = ROPE_THETA ** (-jnp.arange(ROPE_HALF, dtype=F32) * 2.0 / ROPE_DIM)
    ang = positions.astype(F32)[..., None] * inv_freq
    return jnp.concatenate([jnp.cos(ang), jnp.sin(ang)], axis=-1).transpose(0, 2, 1)


def _layer(x, positions, ln1, w_in, lb, hg_gain, pe_k, pe_v, k_w1, k_w2, v_w1, v_w2, nsa_gain, w_o, ln2,
           w_gate, w_up, conv_w, conv_b, w_down, final_gain):
    b, t, d = x.shape
    n = b * t
    assert d == D_MODEL and t % FFN_TM == 0 and t % PROJ_TM == 0 and t % ATT_TQ == 0 and t % HG_TT == 0
    n_grp = t // CMP_STRIDE
    assert n_grp == LANES, "compressed-block axis is laid out on exactly one lane tile"
    n_sel = t // SLC_BLOCK
    assert n_sel % 8 == 0 and n_sel <= LANES and ATT_TK % SLC_BLOCK == 0 and ATT_TQ % ATT_TK == 0
    x2 = x.reshape(n, d)

    splits = np.cumsum([0, 4 * HG_WIDTH, NSA_WIDTH] + [KV_WIDTH] * 6 + [N_GATES])
    w_in = lax.optimization_barrier(w_in.astype(BF16))
    seg = lambda i: w_in[:, splits[i]:splits[i + 1]]
    wh = seg(0)
    wk = jnp.concatenate([seg(2), seg(3), seg(4), seg(6)], axis=1)
    wgate = seg(8).reshape(d, 3, NSA_KV_HEADS, HPG).transpose(0, 2, 1, 3).reshape(d, NSA_KV_HEADS, 3 * HPG)
    wgate = jnp.pad(wgate, ((0, 0), (0, 0), (0, LANES - 3 * HPG))).reshape(d, NSA_KV_HEADS * LANES)
    wt = jnp.concatenate([seg(1), seg(5), seg(7), wgate], axis=1).T
    cs = _rope_angles(positions)

    hg, kcn, vcn, ksw, qt, vt, gt = _inproj_call(x2, ln1.reshape(1, d), wh, wk, wt, cs, t // PROJ_TM)

    mst, lvl = _hgrn_tables()
    o_hg = _hgrn_call(hg.reshape(b, t, 4 * HG_WIDTH), lb.reshape(1, HG_WIDTH).astype(F32),
                      hg_gain.reshape(1, HG_WIDTH), mst, lvl)

    per_lane = lambda a: jnp.broadcast_to(a.reshape(2, CMP_STRIDE, 1, NSA_HEAD_DIM, -1),
                                          (2, CMP_STRIDE, NSA_KV_HEADS, NSA_HEAD_DIM, a.shape[-1]))
    w1_rows = lambda w1: per_lane(w1).reshape(2, CMP_STRIDE * LANES, CMP_HIDDEN).astype(BF16)
    pe_rows = lambda pe: per_lane(pe[..., None]).reshape(2, 1, CMP_STRIDE * LANES)
    zeros_w2 = jnp.zeros((CMP_HIDDEN, NSA_HEAD_DIM), F32)
    place = lambda w2: jnp.stack([jnp.concatenate([w2, zeros_w2], 1), jnp.concatenate([zeros_w2, w2], 1)])
    kc, vct = _cmp_call(kcn.reshape(b, t, KV_WIDTH), vcn.reshape(b, t, KV_WIDTH), pe_rows(pe_k), pe_rows(pe_v),
                        w1_rows(k_w1), w1_rows(v_w1),
                        place(k_w2).astype(BF16), place(v_w2).transpose(0, 2, 1).astype(BF16))

    cmp_start = np.arange(n_grp) * CMP_STRIDE
    cmp_end = cmp_start + CMP_BLOCK - 1
    sel_start = np.arange(LANES) * SLC_BLOCK
    overlap = ((cmp_start[:, None] <= sel_start[None, :] + SLC_BLOCK - 1) & (cmp_end[:, None] >= sel_start[None, :])
               & (np.arange(LANES)[None, :] < n_sel) & (np.arange(n_grp)[:, None] < n_grp - 1))
    ovt = jnp.asarray(overlap.T.astype(np.float32), BF16)
    o_nsa = _nsa_call(qt, ksw.reshape(b, t, 2 * KV_WIDTH), vt, kc, vct, gt, nsa_gain.reshape(NSA_WIDTH, 1), ovt)

    cw = jnp.concatenate([conv_w, conv_b[None, :], jnp.zeros((SUBLANES - CONV_WIDTH - 1, D_FF), F32)], axis=0)
    out = _ffn_call(x2, o_hg.reshape(n, HG_WIDTH), o_nsa,
                    w_o[:HG_WIDTH].astype(BF16), w_o[HG_WIDTH:].astype(BF16), ln2.reshape(1, d),
                    w_gate, w_up, w_down, cw,
                    final_gain.reshape(1, d), t // FFN_TM)
    return out.reshape(b, t, d)


def kernel(x, positions, ln1_gain, w_in, hgrn_lb_param, hgrn_out_gain, cmp_pe_k, cmp_pe_v, cmp_k_w1, cmp_k_w2,
           cmp_v_w1, cmp_v_w2, nsa_out_gain, w_o, ln2_gain, ffn_w_gate, ffn_w_up, ffn_conv_w, ffn_conv_b,
           ffn_w_down, final_gain):
    depth = ln1_gain.shape[0]
    assert depth == 1, "the fused final norm assumes a single layer"
    lower_bounds = jnp.cumsum(jax.nn.softmax(hgrn_lb_param.astype(F32), axis=0), axis=0)
    l = 0
    return _layer(x, positions, ln1_gain[l], w_in[l], lower_bounds[l], hgrn_out_gain[l], cmp_pe_k[l], cmp_pe_v[l],
                  cmp_k_w1[l], cmp_k_w2[l], cmp_v_w1[l], cmp_v_w2[l], nsa_out_gain[l], w_o[l], ln2_gain[l],
                  ffn_w_gate[l], ffn_w_up[l], ffn_conv_w[l], ffn_conv_b[l], ffn_w_down[l], final_gain)
```

```python
import functools

import jax
import jax.numpy as jnp
import numpy as np
from jax import lax
from jax.experimental import pallas as pl
from jax.experimental.pallas import tpu as pltpu

F32 = jnp.float32
BF16 = jnp.bfloat16

D_MODEL = 1024
HG_HEADS = 4
HG_DK = 128
HG_DV = 128
HG_WIDTH = HG_HEADS * HG_DV
NSA_HEADS = 8
NSA_KV_HEADS = 2
NSA_HEAD_DIM = 64
HPG = NSA_HEADS // NSA_KV_HEADS
NSA_WIDTH = NSA_HEADS * NSA_HEAD_DIM
KV_WIDTH = NSA_KV_HEADS * NSA_HEAD_DIM
CMP_BLOCK = 32
CMP_STRIDE = 16
CMP_HIDDEN = 256
SLC_BLOCK = 64
SLC_TOPK = 16
WINDOW = 512
ROPE_THETA = 500000.0
ROPE_DIM = NSA_HEAD_DIM // 4
ROPE_HALF = ROPE_DIM // 2
D_FF = 2816
CONV_WIDTH = 3
EPS = 1e-6
NEG_INF = -1e30
FORCE_SCORE = 1e4
N_GATES = 3 * NSA_HEADS
LOG2_E = 1.4426950408889634

LANES = 128
SUBLANES = 8
VMEM_LIMIT = 56 * 1024 * 1024

PROJ_TM = 1024
HG_CHUNK = 128
HG_LEVELS = (16, 32, 64)
HG_DIAG = 16
PROJ_PIECE = 256
ATT_TQ = 256
ATT_TK = 256
KEY_BLK = 128
SLC_NEAR_TILES = 1
SUM_ROWS = 16
FFN_TM = 512
FFN_TC = 256
FFN_NC = D_FF // FFN_TC


def _dot(a, b):
    return jnp.dot(a, b, preferred_element_type=F32)


def _dot_nt(a, b):
    return lax.dot_general(a, b, (((1,), (1,)), ((), ())), preferred_element_type=F32)


def _dot_tn(a, b):
    return lax.dot_general(a, b, (((0,), (0,)), ((), ())), preferred_element_type=F32)


def _split3(x):
    hi = x.astype(BF16)
    r = x - hi.astype(F32)
    mid = r.astype(BF16)
    lo = (r - mid.astype(F32)).astype(BF16)
    return hi, mid, lo


def _rms(x, gain):
    return x * lax.rsqrt(jnp.mean(x * x, axis=-1, keepdims=True) + EPS) * gain


def _inproj_body(x_ref, g_ref, wh_ref, wk_ref, wt_ref, cs_ref, lb_ref, gain_ref, mst_ref, lvl_ref,
                 o_ref, kcn_ref, vcn_ref, ksw_ref, qt_ref, vt_ref, gt_ref, hb_ref, hg_ref, st_ref, *, tiles_per_seq):
    hb_ref[...] = _rms(x_ref[...], g_ref[...]).astype(BF16)
    hg_ref[...] = _dot(hb_ref[...], wh_ref[...])

    @pl.when(pl.program_id(0) % tiles_per_seq == 0)
    def _sequence_start():
        st_ref[...] = jnp.zeros_like(st_ref)

    def rope(v, axis, cos, sin_hi, sin_lo):
        return (v * cos + pltpu.roll(v, ROPE_HALF, axis) * sin_hi
                + pltpu.roll(v, LANES - ROPE_HALF, axis) * sin_lo)

    def rope_tables(tok):
        cos = cs_ref[0, 0:ROPE_HALF, tok]
        sin = cs_ref[0, ROPE_HALF:ROPE_DIM, tok]
        width = cos.shape[1]
        zero_h = jnp.zeros((ROPE_HALF, width), F32)
        rest = NSA_HEAD_DIM - ROPE_DIM
        slab = lambda lo, hi, fill: jnp.concatenate([lo, hi, jnp.full((rest, width), fill, F32)]
                                                    * (LANES // NSA_HEAD_DIM), axis=0)
        return slab(cos, cos, 1.0), slab(zero_h, sin, 0.0), slab(-sin, zero_h, 0.0)

    def token_major_keys():
        tab = tuple(a.T for a in rope_tables(slice(None)))
        kn = _dot(hb_ref[...], wk_ref[...])
        kcn_ref[...] = rope(kn[:, 0:LANES], 1, *tab)
        vcn_ref[...] = kn[:, LANES:2 * LANES]
        ksw_ref[:, 0:LANES] = rope(kn[:, 2 * LANES:3 * LANES], 1, *tab).astype(BF16)
        ksw_ref[:, LANES:2 * LANES] = rope(kn[:, 3 * LANES:4 * LANES], 1, *tab).astype(BF16)

    def feature_major(tok):
        tab_t = rope_tables(tok)
        rt = _dot_nt(wt_ref[...], hb_ref[tok, :])
        scale = NSA_HEAD_DIM ** -0.5 * LOG2_E
        for j in range(NSA_WIDTH // LANES):
            sl = slice(j * LANES, (j + 1) * LANES)
            qt_ref[0, sl, tok] = (rope(rt[sl], 0, *tab_t) * scale).astype(BF16)
        vt_ref[0, :, tok] = rt[NSA_WIDTH:NSA_WIDTH + 2 * KV_WIDTH].astype(BF16)
        gt_ref[0, :, tok] = jax.nn.sigmoid(rt[NSA_WIDTH + 2 * KV_WIDTH:])

    token_major_keys()
    n_chunks = x_ref.shape[0] // HG_CHUNK
    chunks_per_piece = PROJ_PIECE // HG_CHUNK
    for c in range(n_chunks):
        if c % chunks_per_piece == 0:
            feature_major(slice(c * HG_CHUNK, c * HG_CHUNK + PROJ_PIECE))
        _hgrn_chunk(c, hg_ref, lb_ref, gain_ref, mst_ref, lvl_ref, st_ref, o_ref)


def _inproj_call(x2, gain, wh, wk, wt, cs, lb, hg_gain, mst, lvl, tiles_per_seq):
    n = x2.shape[0]
    tm = PROJ_TM
    t = tiles_per_seq * tm
    b = n // t
    row = lambda w: pl.BlockSpec((tm, w), lambda i: (i, 0))
    col = lambda h: pl.BlockSpec((1, h, tm), lambda i: (i // tiles_per_seq, 0, i % tiles_per_seq))
    full = lambda a: pl.BlockSpec(a.shape, lambda i: (0, 0))
    gate_rows = NSA_KV_HEADS * LANES
    return pl.pallas_call(
        functools.partial(_inproj_body, tiles_per_seq=tiles_per_seq),
        grid=(n // tm,),
        in_specs=[row(D_MODEL), full(gain), full(wh), full(wk), full(wt),
                  col(ROPE_DIM), full(lb), full(hg_gain), full(mst), full(lvl)],
        out_specs=[row(HG_WIDTH), row(KV_WIDTH), row(KV_WIDTH), row(2 * KV_WIDTH),
                   col(NSA_WIDTH), col(2 * KV_WIDTH), col(gate_rows)],
        out_shape=[jax.ShapeDtypeStruct((n, HG_WIDTH), BF16),
                   jax.ShapeDtypeStruct((n, KV_WIDTH), F32),
                   jax.ShapeDtypeStruct((n, KV_WIDTH), F32),
                   jax.ShapeDtypeStruct((n, 2 * KV_WIDTH), BF16),
                   jax.ShapeDtypeStruct((b, NSA_WIDTH, t), BF16),
                   jax.ShapeDtypeStruct((b, 2 * KV_WIDTH, t), BF16),
                   jax.ShapeDtypeStruct((b, gate_rows, t), F32)],
        scratch_shapes=[pltpu.VMEM((tm, D_MODEL), BF16),
                        pltpu.VMEM((tm, 4 * HG_WIDTH), F32),
                        pltpu.VMEM((HG_HEADS, HG_DV, HG_DK), F32)],
        compiler_params=pltpu.CompilerParams(dimension_semantics=("arbitrary",),
                                             vmem_limit_bytes=VMEM_LIMIT),
        name="inproj_hgrn2",
    )(x2, gain, wh, wk, wt, cs, lb, hg_gain, mst, lvl)


def _hgrn_tables():
    L = HG_CHUNK
    t = np.arange(L)[:, None]
    u = np.arange(L)[None, :]
    level = np.where(((t // HG_DIAG) == (u // HG_DIAG)) & (u <= t), 1, 0)
    for li, s in enumerate(HG_LEVELS):
        same = (t // (2 * s)) == (u // (2 * s))
        right = (t % (2 * s)) >= s
        level = np.where(same & right & ((u % (2 * s)) < s), li + 2, level)
    return jnp.asarray((u <= t).astype(np.float32), BF16), jnp.asarray(level, jnp.int32)


def _hgrn_chunk(c, hg_ref, lb_ref, gain_ref, mst_ref, lvl_ref, st_ref, o_ref):
    L = HG_CHUNK
    rows = slice(c * L, (c + 1) * L)
    heads = range(HG_HEADS)
    cols = [slice(h * HG_DK, (h + 1) * HG_DK) for h in heads]
    part = lambda p, h: hg_ref[rows, p * HG_WIDTH + h * HG_DK:p * HG_WIDTH + (h + 1) * HG_DK]
    mst = mst_ref[...]
    lvl = lvl_ref[...]
    n_lv = len(HG_LEVELS)
    row_i = lax.broadcasted_iota(jnp.int32, (L, HG_DK), 0)
    q = [part(0, h) for h in heads]
    vb = [part(2, h).astype(BF16) for h in heads]
    f = [lb_ref[:, cols[h]] + (1.0 - lb_ref[:, cols[h]]) * jax.nn.sigmoid(part(1, h)) for h in heads]
    k = [1.0 - f[h] for h in heads]
    parts = [_split3(jnp.log2(f[h])) for h in heads]
    e_full = [(_dot(mst, parts[h][0]) + _dot(mst, parts[h][1])) + _dot(mst, parts[h][2]) for h in heads]
    b_last = [e_full[h][L - 1:L, :] for h in heads]

    def rel_to(b, blk, off):
        refs = []
        for r0 in range(0, L, blk):
            r = r0 + off - 1
            ref = b[r:r + 1, :] if r >= 0 else jnp.zeros((1, HG_DK), F32)
            refs.append(jnp.broadcast_to(ref, (blk, HG_DK)))
        return b - jnp.concatenate(refs, axis=0)

    def level_sums(b):
        out = [rel_to(b, HG_DIAG, 0)]
        for s_half in HG_LEVELS:
            d = rel_to(b, 2 * s_half, s_half)
            out.append(jnp.where((row_i % (2 * s_half)) >= s_half, d, -d))
        return out

    e = [level_sums(e_full[h]) for h in heads]
    wq = [[jnp.exp2(e[h][l]) for l in range(n_lv + 1)] for h in heads]
    wk = [[jnp.exp2(-e[h][0])] + wq[h][1:] for h in heads]
    prod = [[_dot_nt((q[h] * wq[h][l]).astype(BF16), (k[h] * wk[h][l]).astype(BF16)) for l in range(n_lv + 1)]
            for h in heads]
    st = [st_ref[h] for h in heads]
    inter = [_dot_nt((q[h] * jnp.exp2(e_full[h])).astype(BF16), st[h].astype(BF16)) for h in heads]
    k_dec = [(k[h] * jnp.exp2(b_last[h] - e_full[h])).astype(BF16) for h in heads]
    upd = [_dot_tn(vb[h], k_dec[h]) for h in heads]
    for h in heads:
        st_ref[h] = st[h] * jnp.exp2(b_last[h]) + upd[h]
    a = []
    for h in heads:
        ah = jnp.where(lvl == 1, prod[h][0], 0.0)
        for l in range(1, n_lv + 1):
            ah = jnp.where(lvl == l + 1, prod[h][l], ah)
        a.append(ah.astype(BF16))
    o = [_dot(a[h], vb[h]) + inter[h] for h in heads]
    for h in heads:
        oh = o[h] * lax.rsqrt(jnp.mean(o[h] * o[h], axis=-1, keepdims=True) + EPS) * gain_ref[:, cols[h]]
        o_ref[rows, cols[h]] = (oh * jax.nn.silu(part(3, h))).astype(o_ref.dtype)


def _cmp_body(kcn_ref, vcn_ref, pek_ref, pev_ref, w1k_ref, w1v_ref, w2k_ref, w2v_ref, kc_ref, vc_ref):
    nb = kcn_ref.shape[1] // CMP_STRIDE
    lane_grp = (lax.broadcasted_iota(jnp.int32, (nb, CMP_STRIDE * LANES), 1) // NSA_HEAD_DIM) % NSA_KV_HEADS

    def hidden(src_ref, pe_ref, w1_ref):
        x = jnp.concatenate([src_ref[0, pl.ds(l, nb, stride=CMP_STRIDE), :]
                             for l in range(CMP_STRIDE)], axis=1)
        halves = [x + pe_ref[i] for i in range(2)]
        out = []
        for g in range(NSA_KV_HEADS):
            u, v = (_dot(jnp.where(lane_grp == g, halves[i], 0.0).astype(BF16), w1_ref[i]) for i in range(2))
            out.append(jax.nn.silu(u + pltpu.roll(v, nb - 1, 0)).astype(BF16))
        return out

    hk = hidden(kcn_ref, pek_ref, w1k_ref)
    hv = hidden(vcn_ref, pev_ref, w1v_ref)
    for g in range(NSA_KV_HEADS):
        kc_ref[0, g, 0:nb, :] = _dot(hk[g], w2k_ref[0]).astype(kc_ref.dtype)
        kc_ref[0, g, nb:2 * nb, :] = _dot(hk[g], w2k_ref[1]).astype(kc_ref.dtype)
        vc_ref[0, g, :, 0:nb] = _dot_nt(w2v_ref[0], hv[g]).astype(vc_ref.dtype)
        vc_ref[0, g, :, nb:2 * nb] = _dot_nt(w2v_ref[1], hv[g]).astype(vc_ref.dtype)


def _cmp_call(kcn, vcn, pek, pev, w1k, w1v, w2k, w2v):
    b, t, w = kcn.shape
    nb = t // CMP_STRIDE
    full = lambda a: pl.BlockSpec(a.shape, lambda bi: (0,) * a.ndim)
    out = lambda r, c: pl.BlockSpec((1, NSA_KV_HEADS, r, c), lambda bi: (bi, 0, 0, 0))
    return pl.pallas_call(
        _cmp_body,
        grid=(b,),
        in_specs=[pl.BlockSpec((1, t, w), lambda bi: (bi, 0, 0)), pl.BlockSpec((1, t, w), lambda bi: (bi, 0, 0)),
                  full(pek), full(pev), full(w1k), full(w1v), full(w2k), full(w2v)],
        out_specs=[out(2 * nb, LANES), out(LANES, 2 * nb)],
        out_shape=[jax.ShapeDtypeStruct((b, NSA_KV_HEADS, 2 * nb, LANES), BF16),
                   jax.ShapeDtypeStruct((b, NSA_KV_HEADS, LANES, 2 * nb), BF16)],
        compiler_params=pltpu.CompilerParams(dimension_semantics=("arbitrary",),
                                             vmem_limit_bytes=VMEM_LIMIT),
        name="nsa_compress",
    )(kcn, vcn, pek, pev, w1k, w1v, w2k, w2v)


def _nsa_body(qt_ref, ksw_ref, vst_ref, vwt_ref, kc_ref, vct_ref, gt_ref, gain_ref, ovt_ref, o_ref,
              ks_ref, kw_ref, vs_ref, vw_ref, m_ref, acc_ref, s_ref, ch_ref, ocw_ref):
    g = pl.program_id(1)
    qi = pl.program_id(2)
    tq = ATT_TQ
    tk = ATT_TK
    t_len = ksw_ref.shape[1]
    n_kt = t_len // tk
    n_pairs = HPG // 2
    hd = NSA_HEAD_DIM

    @pl.when(qi == 0)
    def _build_kv():
        lane = lax.broadcasted_iota(jnp.int32, (tk, LANES), 1)
        lo_lane = lane < hd
        keep = (lane // hd) == g

        def build_k(src_col, dst_ref):
            for j in range(n_kt):
                x = ksw_ref[0, j * tk:(j + 1) * tk, src_col * LANES:(src_col + 1) * LANES].astype(F32)
                dup = jnp.where(keep, x, pltpu.roll(x, hd, 1))
                dst_ref[j, 0:tk, :] = jnp.where(lo_lane, dup, 0.0).astype(BF16)
                dst_ref[j, tk:2 * tk, :] = jnp.where(lo_lane, 0.0, dup).astype(BF16)

        def build_v(src_ref, dst_ref):
            zero = jnp.zeros((hd, tk), BF16)
            row = lax.broadcasted_iota(jnp.int32, (SUM_ROWS, 2 * tk), 0)
            col = lax.broadcasted_iota(jnp.int32, (SUM_ROWS, 2 * tk), 1)
            ones_rows = jnp.where(((row == 0) & (col < tk)) | ((row == 1) & (col >= tk)), 1.0, 0.0).astype(BF16)
            for j in range(n_kt):
                x = src_ref[0, :, j * tk:(j + 1) * tk]
                dst_ref[j, 0:hd, 0:tk] = x
                dst_ref[j, 0:hd, tk:2 * tk] = zero
                dst_ref[j, hd:2 * hd, 0:tk] = zero
                dst_ref[j, hd:2 * hd, tk:2 * tk] = x
                dst_ref[j, 2 * hd:2 * hd + SUM_ROWS, :] = ones_rows

        build_k(0, ks_ref)
        build_k(1, kw_ref)
        build_v(vst_ref, vs_ref)
        build_v(vwt_ref, vw_ref)

    t0 = qi * tq
    key_i = lax.broadcasted_iota(jnp.int32, (tk, tq), 0)
    qry_t = t0 + lax.broadcasted_iota(jnp.int32, (tk, tq), 1)
    slab_lo = lax.broadcasted_iota(jnp.int32, (LANES, tq), 0) < hd
    acc_row = lax.broadcasted_iota(jnp.int32, (LANES + SUM_ROWS, tq), 0)
    slab_a = (acc_row < hd) | (acc_row == LANES)
    q_pairs = [qt_ref[0, p * LANES:(p + 1) * LANES, :] for p in range(n_pairs)]

    last = (t0 + tq - 1) // tk

    def scores(k_ref, j):
        kt = k_ref[j]
        return [_dot(kt, q_pairs[p]) for p in range(n_pairs)]

    def live_keys(rel_at_origin, lower, upper):
        out = []
        for qh in range(tq // LANES):
            live = [kb for kb in range(tk // KEY_BLK)
                    if rel_at_origin + qh * LANES + LANES - 1 - kb * KEY_BLK >= lower
                    and rel_at_origin + qh * LANES - (kb * KEY_BLK + KEY_BLK - 1) < upper]
            out.append((min(live) * KEY_BLK, (max(live) + 1) * KEY_BLK))
        return out

    def tile_softmax(slot, bias, m_get, m_put, keys=None):
        alphas = {}
        probs = {}
        for hh in range(HPG):
            p, h = divmod(hh, 2)
            a_parts = []
            p_parts = []
            for qh in range(tq // LANES):
                ql = slice(qh * LANES, (qh + 1) * LANES)
                k_lo, k_hi = keys[qh] if keys is not None else (0, tk)
                sh = (s_ref[slot, p, h * tk + k_lo:h * tk + k_hi, ql]
                      + (bias if bias.shape == (1, 1) else bias[k_lo:k_hi, ql]))
                m_prev = m_get(hh, qh)
                m_new = jnp.maximum(m_prev, jnp.max(sh, axis=0, keepdims=True))
                m_put(hh, qh, m_new)
                piece = [jnp.zeros((k_lo, LANES), BF16)] if k_lo else []
                piece.append(jnp.exp2(sh - m_new).astype(BF16))
                if k_hi < tk:
                    piece.append(jnp.zeros((tk - k_hi, LANES), BF16))
                p_parts.append(jnp.concatenate(piece, axis=0) if len(piece) > 1 else piece[0])
                a_parts.append(jnp.exp2(m_prev - m_new))
            alphas[hh] = jnp.concatenate(a_parts, axis=1)
            probs[hh] = jnp.concatenate(p_parts, axis=1)
        return ([jnp.concatenate([probs[2 * p], probs[2 * p + 1]], axis=0) for p in range(n_pairs)],
                [jnp.where(slab_a, alphas[2 * p], alphas[2 * p + 1]) for p in range(n_pairs)])

    def normalised(acc):
        inv = jnp.where(slab_lo, 1.0 / acc[LANES:LANES + 1, :], 1.0 / acc[LANES + 1:LANES + 2, :])
        return acc[0:LANES] * inv

    def picked_bias(j, also=None):
        per_tile = tk // SLC_BLOCK
        picked = jnp.concatenate([jnp.broadcast_to(ch_ref[j * per_tile + i], (SLC_BLOCK, tq)) for i in range(per_tile)],
                                 axis=0) > 0.5
        return jnp.where(picked if also is None else picked & also, 0.0, NEG_INF)

    def m_put(hh, qh, v):
        m_ref[hh, :, qh * LANES:(qh + 1) * LANES] = v

    n_win = (WINDOW + tq) // tk
    n_near = SLC_NEAR_TILES
    n_far = jnp.maximum(last + 1 - n_near, 0)

    def static_part(first_k, has_far, need_topk):
        n_cmp_pad = kc_ref.shape[2] // 2
        blk_i = lax.broadcasted_iota(jnp.int32, (n_cmp_pad, tq), 0)
        blk_t = t0 + lax.broadcasted_iota(jnp.int32, (n_cmp_pad, tq), 1)
        cmp_ok = (blk_i * CMP_STRIDE + (CMP_BLOCK - 1)) <= blk_t
        kc = kc_ref[0, 0]
        vct = vct_ref[0, 0]
        s_cmp = [_dot(kc, q_pairs[p]) for p in range(n_pairs)]
        win_tiles = [(k, last - (n_win - 1) + k) for k in range(first_k, n_win)]
        near_tiles = [last - k for k in range(n_near)]
        up_front = ([(0, ks_ref, 0)] if has_far else []) + [(1 + k, kw_ref, jw) for k, jw in win_tiles]
        up_front += [(1 + n_win + k, ks_ref, jnp.maximum(jn, 0)) for k, jn in enumerate(near_tiles)]
        for slot, k_ref, j0 in up_front:
            s_first = scores(k_ref, j0)
            for p in range(n_pairs):
                s_ref[slot, p] = s_first[p]
        p_sum = jnp.zeros((n_cmp_pad, tq), F32)
        p_cmp = []
        for p in range(n_pairs):
            probs = []
            for h in range(2):
                sh = jnp.where(cmp_ok, s_cmp[p][h * n_cmp_pad:(h + 1) * n_cmp_pad], NEG_INF)
                mh = jnp.max(sh, axis=0, keepdims=True)
                eh = jnp.where(cmp_ok, jnp.exp2(sh - mh), 0.0)
                den = jnp.sum(eh, axis=0, keepdims=True)
                ph = eh / jnp.where(den > 0.0, den, 1.0)
                p_sum = p_sum + ph
                probs.append(ph.astype(BF16))
            p_cmp.append(jnp.concatenate(probs, axis=0))
        o_cmp = [_dot(vct, p_cmp[p]) for p in range(n_pairs)]

        n_sel = t_len // SLC_BLOCK
        if need_topk:
            hi, mid, lo = _split3(p_sum)
            ovt = ovt_ref[...]
            p_sel = ((_dot(ovt, hi) + _dot(ovt, mid)) + _dot(ovt, lo))[0:n_sel]
            sel_i = lax.broadcasted_iota(jnp.int32, (n_sel, tq), 0)
            cur = (t0 + lax.broadcasted_iota(jnp.int32, (n_sel, tq), 1)) // SLC_BLOCK
            forced = (sel_i == 0) | (sel_i == cur) | (sel_i == cur - 1)
            score = jnp.where(forced, FORCE_SCORE, p_sel)
            score = jnp.where(sel_i <= cur, score, -jnp.inf)
            rank = jnp.zeros((n_sel, tq), jnp.int32)
            row_grp = SUBLANES
            grp_i = lax.broadcasted_iota(jnp.int32, (row_grp, tq), 0)
            for i in range(n_sel):
                ci = score[i:i + 1, :]
                ahead = []
                for r0 in range(0, n_sel, row_grp):
                    rows = slice(r0, r0 + row_grp)
                    if r0 > i:
                        ahead.append(ci >= score[rows])
                    elif r0 + row_grp <= i:
                        ahead.append(ci > score[rows])
                    else:
                        ahead.append((ci > score[rows]) | ((ci == score[rows]) & (grp_i > i - r0)))
                rank = rank + jnp.where(jnp.concatenate(ahead, axis=0), 1, 0)
            chosen = jnp.where(rank < min(SLC_TOPK, n_sel), 1.0, 0.0)
        else:
            sel_i = lax.broadcasted_iota(jnp.int32, (n_sel, tq), 0)
            cur = (t0 + lax.broadcasted_iota(jnp.int32, (n_sel, tq), 1)) // SLC_BLOCK
            chosen = jnp.where(sel_i <= cur, 1.0, 0.0)
        for i in range(n_sel):
            ch_ref[i] = chosen[i:i + 1, :]

        m_win = {}
        acc_win = [jnp.zeros((LANES + SUM_ROWS, tq), F32) for _ in range(n_pairs)]
        for k, jw in win_tiles:
            rel_hi = (n_win - k) * tk - 1
            rel_lo = rel_hi - (tq - 1) - (tk - 1)
            exists = jnp.zeros((1, 1), F32)
            if rel_lo >= 0 and rel_hi < WINDOW:
                bias = exists
            else:
                rel = qry_t - (jw * tk + key_i)
                inside = (rel < WINDOW) if rel_lo >= 0 else (rel >= 0) if rel_hi < WINDOW else (rel >= 0) & (rel < WINDOW)
                bias = jnp.where(inside, exists, NEG_INF)
            probs, a_rows = tile_softmax(1 + k, bias,
                                         lambda hh, qh: m_win.get((hh, qh), jnp.full((1, LANES), -jnp.inf, F32)),
                                         lambda hh, qh, v: m_win.__setitem__((hh, qh), v),
                                         keys=live_keys(rel_lo + tk - 1, 0, WINDOW))
            vt = vw_ref[jw]
            acc_win = [acc_win[p] * a_rows[p] + _dot(vt, probs[p]) for p in range(n_pairs)]
        for p in range(n_pairs):
            ocw_ref[0, p] = o_cmp[p]
            ocw_ref[1, p] = normalised(acc_win[p])

        m_near = {}
        acc_near = [jnp.zeros((LANES + SUM_ROWS, tq), F32) for _ in range(n_pairs)]
        for k, jn in enumerate(near_tiles):
            jc = jnp.maximum(jn, 0)
            causal = (jn * tk + key_i) <= qry_t
            if k > 0:
                causal = causal & (jn >= 0)
            probs, a_rows = tile_softmax(1 + n_win + k, picked_bias(jc, causal),
                                         lambda hh, qh: m_near.get((hh, qh), jnp.full((1, LANES), -jnp.inf, F32)),
                                         lambda hh, qh, v: m_near.__setitem__((hh, qh), v),
                                         keys=live_keys(tk - tq + k * tk, 0, t_len))
            vt = vs_ref[jc]
            acc_near = [acc_near[p] * a_rows[p] + _dot(vt, probs[p]) for p in range(n_pairs)]

        for (hh, qh), v in m_near.items():
            m_put(hh, qh, v)
        for p in range(n_pairs):
            acc_ref[p] = acc_near[p]

    def variant(e):
        tiles_upto = (e + 1) * tq // tk
        return (max(0, n_win - tiles_upto), tiles_upto > n_near, (e + 1) * tq > SLC_TOPK * SLC_BLOCK)

    n_q = t_len // tq
    start = 0
    for e in range(1, n_q + 1):
        if e == n_q or variant(e) != variant(start):
            pl.when((qi >= start) & (qi < e))(functools.partial(static_part, *variant(start)))
            start = e
    o_cmp = [ocw_ref[0, p] for p in range(n_pairs)]
    o_win = [ocw_ref[1, p] for p in range(n_pairs)]

    def slc_step(j, carry):
        s_next = scores(ks_ref, jnp.minimum(j + 1, n_far - 1))
        probs, a_rows = tile_softmax(0, picked_bias(j), lambda hh, qh: m_ref[hh, :, qh * LANES:(qh + 1) * LANES], m_put)
        vt = vs_ref[j]
        for p in range(n_pairs):
            pv = _dot(vt, probs[p])
            s_ref[0, p] = s_next[p]
            acc_ref[p] = acc_ref[p] * a_rows[p] + pv
        return carry

    lax.fori_loop(0, n_far, slc_step, 0)
    o_slc = [normalised(acc_ref[p]) for p in range(n_pairs)]

    gates = gt_ref[0]
    gain = gain_ref[...]
    for p in range(n_pairs):
        o = jnp.zeros((LANES, tq), F32)
        for c, branch in enumerate((o_cmp[p], o_slc[p], o_win[p])):
            r = c * HPG + 2 * p
            o = o + jnp.where(slab_lo, gates[r:r + 1, :], gates[r + 1:r + 2, :]) * branch
        sq = o * o
        ms_a = jnp.sum(sq[0:hd], axis=0, keepdims=True)
        ms_b = jnp.sum(sq[hd:2 * hd], axis=0, keepdims=True)
        ms = jnp.where(slab_lo, ms_a, ms_b) * (1.0 / hd)
        o = o * lax.rsqrt(ms + EPS)
        o_ref[0, p * LANES:(p + 1) * LANES, :] = (o * gain[p * LANES:(p + 1) * LANES, :]).astype(o_ref.dtype)


def _nsa_call(qt, ksw, vt, kc, vct, gt, gain, ovt):
    b, _, t = qt.shape
    tq, tk = ATT_TQ, ATT_TK
    n_kt = t // tk
    gw = HPG * NSA_HEAD_DIM
    hd = NSA_HEAD_DIM
    k_scratch = pltpu.VMEM((n_kt, 2 * tk, LANES), BF16)
    v_scratch = pltpu.VMEM((n_kt, LANES + SUM_ROWS, 2 * tk), BF16)
    return pl.pallas_call(
        _nsa_body,
        grid=(b, NSA_KV_HEADS, t // tq),
        in_specs=[
            pl.BlockSpec((1, gw, tq), lambda bi, gi, qi: (bi, gi, qi)),
            pl.BlockSpec((1, t, 2 * KV_WIDTH), lambda bi, gi, qi: (bi, 0, 0)),
            pl.BlockSpec((1, hd, t), lambda bi, gi, qi: (bi, gi, 0)),
            pl.BlockSpec((1, hd, t), lambda bi, gi, qi: (bi, NSA_KV_HEADS + gi, 0)),
            pl.BlockSpec((1, 1) + kc.shape[2:], lambda bi, gi, qi: (bi, gi, 0, 0)),
            pl.BlockSpec((1, 1) + vct.shape[2:], lambda bi, gi, qi: (bi, gi, 0, 0)),
            pl.BlockSpec((1, LANES, tq), lambda bi, gi, qi: (bi, gi, qi)),
            pl.BlockSpec((gw, 1), lambda bi, gi, qi: (gi, 0)),
            pl.BlockSpec(ovt.shape, lambda bi, gi, qi: (0, 0)),
        ],
        out_specs=pl.BlockSpec((1, gw, tq), lambda bi, gi, qi: (bi, gi, qi)),
        out_shape=jax.ShapeDtypeStruct((b, NSA_WIDTH, t), BF16),
        scratch_shapes=[k_scratch, k_scratch, v_scratch, v_scratch,
                        pltpu.VMEM((HPG, 1, tq), F32),
                        pltpu.VMEM((HPG // 2, LANES + SUM_ROWS, tq), F32), pltpu.VMEM((1 + (WINDOW + tq) // tk + SLC_NEAR_TILES, HPG // 2, 2 * tk, tq), F32),
                        pltpu.VMEM((t // SLC_BLOCK, 1, tq), F32), pltpu.VMEM((2, HPG // 2, LANES, tq), F32)],
        compiler_params=pltpu.CompilerParams(dimension_semantics=("arbitrary", "arbitrary", "arbitrary"),
                                             vmem_limit_bytes=VMEM_LIMIT),
        name="nsa_attention",
    )(qt, ksw, vt, vt, kc, vct, gt, gain, ovt)


def _ffn_body(x_ref, oh_ref, on_ref, woh_ref, won_ref, g2_ref, wg_ref, wu_ref, wd_ref, cw_ref, gf_ref,
              out_ref, halo_ref, act_ref, *, tiles_per_seq):
    tm = x_ref.shape[0]
    x1 = x_ref[...] + _dot(oh_ref[...], woh_ref[...]) + _dot_tn(on_ref[0], won_ref[...])
    hb = _rms(x1, g2_ref[...]).astype(BF16)
    row = lax.broadcasted_iota(jnp.int32, (tm, FFN_TC), 0)

    @pl.when((pl.program_id(0) % tiles_per_seq) == 0)
    def _sequence_start():
        halo_ref[...] = jnp.zeros_like(halo_ref)

    def activation(c, gate, up):
        cols = slice(c * FFN_TC, (c + 1) * FFN_TC)
        halo = halo_ref[:, cols]
        halo_ref[:, cols] = gate[tm - SUBLANES:tm, :]
        last1 = halo[SUBLANES - 1:SUBLANES, :]
        last2 = halo[SUBLANES - 2:SUBLANES - 1, :]
        prev1 = jnp.where(row == 0, last1, pltpu.roll(gate, 1, 0))
        prev2 = jnp.where(row == 0, last2, jnp.where(row == 1, last1, pltpu.roll(gate, 2, 0)))
        cw = cw_ref[:, cols]
        y = cw[0:1, :] * prev2 + cw[1:2, :] * prev1 + cw[2:3, :] * gate + cw[3:4, :]
        return (jax.nn.silu(y) * up).astype(BF16)

    chunk = lambda w_ref, c: _dot(hb, w_ref[:, c * FFN_TC:(c + 1) * FFN_TC].astype(BF16))
    gate_up = (chunk(wg_ref, 0), chunk(wu_ref, 0))
    for c in range(FFN_NC):
        cur = gate_up
        if c + 1 < FFN_NC:
            gate_up = (chunk(wg_ref, c + 1), chunk(wu_ref, c + 1))
        act_ref[:, c * FFN_TC:(c + 1) * FFN_TC] = activation(c, *cur)
    acc = _dot(act_ref[...], wd_ref[...].astype(BF16))
    out_ref[...] = _rms(x1 + acc, gf_ref[...])


def _ffn_call(x2, oh, on, woh, won, g2, wg, wu, wd, cw, gf, tiles_per_seq):
    n = x2.shape[0]
    tm = FFN_TM
    row = lambda w: pl.BlockSpec((tm, w), lambda i: (i, 0))
    full = lambda a: pl.BlockSpec(a.shape, lambda i: (0,) * a.ndim, pipeline_mode=pl.Buffered(1))
    return pl.pallas_call(
        functools.partial(_ffn_body, tiles_per_seq=tiles_per_seq),
        grid=(n // tm,),
        in_specs=[row(D_MODEL), row(HG_WIDTH),
                  pl.BlockSpec((1, NSA_WIDTH, tm), lambda i: (i // tiles_per_seq, 0, i % tiles_per_seq)),
                  full(woh), full(won), full(g2),
                  full(wg), full(wu), full(wd), full(cw), full(gf)],
        out_specs=row(D_MODEL),
        out_shape=jax.ShapeDtypeStruct((n, D_MODEL), F32),
        scratch_shapes=[pltpu.VMEM((SUBLANES, D_FF), F32), pltpu.VMEM((tm, D_FF), BF16)],
        compiler_params=pltpu.CompilerParams(dimension_semantics=("arbitrary",),
                                             vmem_limit_bytes=VMEM_LIMIT),
        name="outproj_convffn",
    )(x2, oh, on, woh, won, g2, wg, wu, wd, cw, gf)


def _rope_angles(positions):
    inv_freq = ROPE_THETA ** (-jnp.arange(ROPE_HALF, dtype=F32) * 2.0 / ROPE_DIM)
    ang = positions.astype(F32)[..., None] * inv_freq
    return jnp.concatenate([jnp.cos(ang), jnp.sin(ang)], axis=-1).transpose(0, 2, 1)


def _layer(x, positions, ln1, w_in, lb, hg_gain, pe_k, pe_v, k_w1, k_w2, v_w1, v_w2, nsa_gain, w_o, ln2,
           w_gate, w_up, conv_w, conv_b, w_down, final_gain):
    b, t, d = x.shape
    n = b * t
    assert d == D_MODEL and t % FFN_TM == 0 and t % PROJ_TM == 0 and t % ATT_TQ == 0
    assert PROJ_TM % PROJ_PIECE == 0 and PROJ_PIECE % HG_CHUNK == 0
    n_grp = t // CMP_STRIDE
    assert n_grp == LANES, "compressed-block axis is laid out on exactly one lane tile"
    n_sel = t // SLC_BLOCK
    assert n_sel % 8 == 0 and n_sel <= LANES and ATT_TK % SLC_BLOCK == 0 and ATT_TQ % ATT_TK == 0
    x2 = x.reshape(n, d)

    splits = np.cumsum([0, 4 * HG_WIDTH, NSA_WIDTH] + [KV_WIDTH] * 6 + [N_GATES])
    seg = lambda i: w_in[:, splits[i]:splits[i + 1]]
    wh = seg(0).astype(BF16)
    wk = jnp.concatenate([seg(2), seg(3), seg(4), seg(6)], axis=1).astype(BF16)
    wgate = seg(8).reshape(d, 3, NSA_KV_HEADS, HPG).transpose(0, 2, 1, 3).reshape(d, NSA_KV_HEADS, 3 * HPG)
    wgate = jnp.pad(wgate, ((0, 0), (0, 0), (0, LANES - 3 * HPG))).reshape(d, NSA_KV_HEADS * LANES)
    wt = jnp.concatenate([seg(1), seg(5), seg(7), wgate], axis=1).T.astype(BF16)
    cs = _rope_angles(positions)

    mst, lvl = _hgrn_tables()
    o_hg, kcn, vcn, ksw, qt, vt, gt = _inproj_call(x2, ln1.reshape(1, d), wh, wk, wt, cs,
                                                   lb.reshape(1, HG_WIDTH).astype(F32), hg_gain.reshape(1, HG_WIDTH),
                                                   mst, lvl, t // PROJ_TM)

    per_lane = lambda a: jnp.broadcast_to(a.reshape(2, CMP_STRIDE, 1, NSA_HEAD_DIM, -1),
                                          (2, CMP_STRIDE, NSA_KV_HEADS, NSA_HEAD_DIM, a.shape[-1]))
    w1_rows = lambda w1: per_lane(w1).reshape(2, CMP_STRIDE * LANES, CMP_HIDDEN).astype(BF16)
    pe_rows = lambda pe: per_lane(pe[..., None]).reshape(2, 1, CMP_STRIDE * LANES)
    zeros_w2 = jnp.zeros((CMP_HIDDEN, NSA_HEAD_DIM), F32)
    place = lambda w2: jnp.stack([jnp.concatenate([w2, zeros_w2], 1), jnp.concatenate([zeros_w2, w2], 1)])
    kc, vct = _cmp_call(kcn.reshape(b, t, KV_WIDTH), vcn.reshape(b, t, KV_WIDTH), pe_rows(pe_k), pe_rows(pe_v),
                        w1_rows(k_w1), w1_rows(v_w1),
                        place(k_w2).astype(BF16), place(v_w2).transpose(0, 2, 1).astype(BF16))

    cmp_start = np.arange(n_grp) * CMP_STRIDE
    cmp_end = cmp_start + CMP_BLOCK - 1
    sel_start = np.arange(LANES) * SLC_BLOCK
    overlap = ((cmp_start[:, None] <= sel_start[None, :] + SLC_BLOCK - 1) & (cmp_end[:, None] >= sel_start[None, :])
               & (np.arange(LANES)[None, :] < n_sel) & (np.arange(n_grp)[:, None] < n_grp - 1))
    ovt = jnp.asarray(overlap.T.astype(np.float32), BF16)
    o_nsa = _nsa_call(qt, ksw.reshape(b, t, 2 * KV_WIDTH), vt, kc, vct, gt, nsa_gain.reshape(NSA_WIDTH, 1), ovt)

    cw = jnp.concatenate([conv_w, conv_b[None, :], jnp.zeros((SUBLANES - CONV_WIDTH - 1, D_FF), F32)], axis=0)
    out = _ffn_call(x2, o_hg, o_nsa,
                    w_o[:HG_WIDTH].astype(BF16), w_o[HG_WIDTH:].astype(BF16), ln2.reshape(1, d),
                    w_gate, w_up, w_down, cw,
                    final_gain.reshape(1, d), t // FFN_TM)
    return out.reshape(b, t, d)


def kernel(x, positions, ln1_gain, w_in, hgrn_lb_param, hgrn_out_gain, cmp_pe_k, cmp_pe_v, cmp_k_w1, cmp_k_w2,
           cmp_v_w1, cmp_v_w2, nsa_out_gain, w_o, ln2_gain, ffn_w_gate, ffn_w_up, ffn_conv_w, ffn_conv_b,
           ffn_w_down, final_gain):
    depth = ln1_gain.shape[0]
    assert depth == 1, "the fused final norm assumes a single layer"
    lower_bounds = jnp.cumsum(jax.nn.softmax(hgrn_lb_param.astype(F32), axis=0), axis=0)
    l = 0
    return _layer(x, positions, ln1_gain[l], w_in[l], lower_bounds[l], hgrn_out_gain[l], cmp_pe_k[l], cmp_pe_v[l],
                  cmp_k_w1[l], cmp_k_w2[l], cmp_v_w1[l], cmp_v_w2[l], nsa_out_gain[l], w_o[l], ln2_gain[l],
                  ffn_w_gate[l], ffn_w_up[l], ffn_conv_w[l], ffn_conv_b[l], ffn_w_down[l], final_gain)
```

```python
import functools

import jax
import jax.numpy as jnp
import numpy as np
from jax import lax
from jax.experimental import pallas as pl
from jax.experimental.pallas import tpu as pltpu

F32 = jnp.float32
BF16 = jnp.bfloat16

D_MODEL = 1024
HG_HEADS = 4
HG_DK = 128
HG_DV = 128
HG_WIDTH = HG_HEADS * HG_DV
NSA_HEADS = 8
NSA_KV_HEADS = 2
NSA_HEAD_DIM = 64
HPG = NSA_HEADS // NSA_KV_HEADS
NSA_WIDTH = NSA_HEADS * NSA_HEAD_DIM
KV_WIDTH = NSA_KV_HEADS * NSA_HEAD_DIM
CMP_BLOCK = 32
CMP_STRIDE = 16
CMP_HIDDEN = 256
SLC_BLOCK = 64
SLC_TOPK = 16
WINDOW = 512
ROPE_THETA = 500000.0
ROPE_DIM = NSA_HEAD_DIM // 4
ROPE_HALF = ROPE_DIM // 2
D_FF = 2816
CONV_WIDTH = 3
EPS = 1e-6
NEG_INF = -1e30
FORCE_SCORE = 1e4
N_GATES = 3 * NSA_HEADS
LOG2_E = 1.4426950408889634

LANES = 128
SUBLANES = 8
VMEM_LIMIT = 56 * 1024 * 1024

PROJ_TM = 1024
HG_CHUNK = 128
HG_LEVELS = (16, 32, 64)
HG_DIAG = 16
PROJ_PIECE = 256
ATT_TQ = 256
ATT_TK = 256
KEY_BLK = 128
SLC_NEAR_TILES = 1
SUM_ROWS = 16
FFN_TM = 512
FFN_TC = 256
FFN_NC = D_FF // FFN_TC


def _dot(a, b):
    return jnp.dot(a, b, preferred_element_type=F32)


def _dot_nt(a, b):
    return lax.dot_general(a, b, (((1,), (1,)), ((), ())), preferred_element_type=F32)


def _dot_tn(a, b):
    return lax.dot_general(a, b, (((0,), (0,)), ((), ())), preferred_element_type=F32)


def _split3(x):
    hi = x.astype(BF16)
    r = x - hi.astype(F32)
    mid = r.astype(BF16)
    lo = (r - mid.astype(F32)).astype(BF16)
    return hi, mid, lo


def _rms(x, gain):
    return x * lax.rsqrt(jnp.mean(x * x, axis=-1, keepdims=True) + EPS) * gain


def _inproj_body(x_ref, g_ref, wh_ref, wk_ref, wt_ref, cs_ref, lb_ref, gain_ref, mst_ref, lvl_ref,
                 o_ref, kcn_ref, vcn_ref, ksw_ref, qt_ref, vt_ref, gt_ref, hb_ref, hg_ref, st_ref, *, tiles_per_seq):
    hb_ref[...] = _rms(x_ref[...], g_ref[...]).astype(BF16)
    hg_ref[...] = _dot(hb_ref[...], wh_ref[...])

    @pl.when(pl.program_id(0) % tiles_per_seq == 0)
    def _sequence_start():
        st_ref[...] = jnp.zeros_like(st_ref)

    def rope(v, axis, cos, sin_hi, sin_lo):
        return (v * cos + pltpu.roll(v, ROPE_HALF, axis) * sin_hi
                + pltpu.roll(v, LANES - ROPE_HALF, axis) * sin_lo)

    def rope_tables(tok):
        cos = cs_ref[0, 0:ROPE_HALF, tok]
        sin = cs_ref[0, ROPE_HALF:ROPE_DIM, tok]
        width = cos.shape[1]
        zero_h = jnp.zeros((ROPE_HALF, width), F32)
        rest = NSA_HEAD_DIM - ROPE_DIM
        slab = lambda lo, hi, fill: jnp.concatenate([lo, hi, jnp.full((rest, width), fill, F32)]
                                                    * (LANES // NSA_HEAD_DIM), axis=0)
        return slab(cos, cos, 1.0), slab(zero_h, sin, 0.0), slab(-sin, zero_h, 0.0)

    def token_major_keys():
        tab = tuple(a.T for a in rope_tables(slice(None)))
        kn = _dot(hb_ref[...], wk_ref[...])
        kcn_ref[...] = rope(kn[:, 0:LANES], 1, *tab)
        vcn_ref[...] = kn[:, LANES:2 * LANES]
        ksw_ref[:, 0:LANES] = rope(kn[:, 2 * LANES:3 * LANES], 1, *tab).astype(BF16)
        ksw_ref[:, LANES:2 * LANES] = rope(kn[:, 3 * LANES:4 * LANES], 1, *tab).astype(BF16)

    def feature_major(tok):
        tab_t = rope_tables(tok)
        rt = _dot_nt(wt_ref[...], hb_ref[tok, :])
        scale = NSA_HEAD_DIM ** -0.5 * LOG2_E
        for j in range(NSA_WIDTH // LANES):
            sl = slice(j * LANES, (j + 1) * LANES)
            qt_ref[0, sl, tok] = (rope(rt[sl], 0, *tab_t) * scale).astype(BF16)
        vt_ref[0, :, tok] = rt[NSA_WIDTH:NSA_WIDTH + 2 * KV_WIDTH].astype(BF16)
        gt_ref[0, :, tok] = jax.nn.sigmoid(rt[NSA_WIDTH + 2 * KV_WIDTH:])

    token_major_keys()
    n_chunks = x_ref.shape[0] // HG_CHUNK
    chunks_per_piece = PROJ_PIECE // HG_CHUNK
    for c in range(n_chunks):
        if c % chunks_per_piece == 0:
            feature_major(slice(c * HG_CHUNK, c * HG_CHUNK + PROJ_PIECE))
        _hgrn_chunk(c, hg_ref, lb_ref, gain_ref, mst_ref, lvl_ref, st_ref, o_ref)


def _inproj_call(x2, gain, wh, wk, wt, cs, lb, hg_gain, mst, lvl, tiles_per_seq):
    n = x2.shape[0]
    tm = PROJ_TM
    t = tiles_per_seq * tm
    b = n // t
    row = lambda w: pl.BlockSpec((tm, w), lambda i: (i, 0))
    col = lambda h: pl.BlockSpec((1, h, tm), lambda i: (i // tiles_per_seq, 0, i % tiles_per_seq))
    full = lambda a: pl.BlockSpec(a.shape, lambda i: (0, 0))
    gate_rows = NSA_KV_HEADS * LANES
    return pl.pallas_call(
        functools.partial(_inproj_body, tiles_per_seq=tiles_per_seq),
        grid=(n // tm,),
        in_specs=[row(D_MODEL), full(gain), full(wh), full(wk), full(wt),
                  col(ROPE_DIM), full(lb), full(hg_gain), full(mst), full(lvl)],
        out_specs=[row(HG_WIDTH), row(KV_WIDTH), row(KV_WIDTH), row(2 * KV_WIDTH),
                   col(NSA_WIDTH), col(2 * KV_WIDTH), col(gate_rows)],
        out_shape=[jax.ShapeDtypeStruct((n, HG_WIDTH), BF16),
                   jax.ShapeDtypeStruct((n, KV_WIDTH), F32),
                   jax.ShapeDtypeStruct((n, KV_WIDTH), F32),
                   jax.ShapeDtypeStruct((n, 2 * KV_WIDTH), BF16),
                   jax.ShapeDtypeStruct((b, NSA_WIDTH, t), BF16),
                   jax.ShapeDtypeStruct((b, 2 * KV_WIDTH, t), BF16),
                   jax.ShapeDtypeStruct((b, gate_rows, t), F32)],
        scratch_shapes=[pltpu.VMEM((tm, D_MODEL), BF16),
                        pltpu.VMEM((tm, 4 * HG_WIDTH), F32),
                        pltpu.VMEM((HG_HEADS, HG_DV, HG_DK), F32)],
        compiler_params=pltpu.CompilerParams(dimension_semantics=("arbitrary",),
                                             vmem_limit_bytes=VMEM_LIMIT),
        name="inproj_hgrn2",
    )(x2, gain, wh, wk, wt, cs, lb, hg_gain, mst, lvl)


def _hgrn_tables():
    L = HG_CHUNK
    t = np.arange(L)[:, None]
    u = np.arange(L)[None, :]
    level = np.where(((t // HG_DIAG) == (u // HG_DIAG)) & (u <= t), 1, 0)
    for li, s in enumerate(HG_LEVELS):
        same = (t // (2 * s)) == (u // (2 * s))
        right = (t % (2 * s)) >= s
        level = np.where(same & right & ((u % (2 * s)) < s), li + 2, level)
    return jnp.asarray((u <= t).astype(np.float32), BF16), jnp.asarray(level, jnp.int32)


def _hgrn_chunk(c, hg_ref, lb_ref, gain_ref, mst_ref, lvl_ref, st_ref, o_ref):
    L = HG_CHUNK
    rows = slice(c * L, (c + 1) * L)
    heads = range(HG_HEADS)
    cols = [slice(h * HG_DK, (h + 1) * HG_DK) for h in heads]
    part = lambda p, h: hg_ref[rows, p * HG_WIDTH + h * HG_DK:p * HG_WIDTH + (h + 1) * HG_DK]
    mst = mst_ref[...]
    lvl = lvl_ref[...]
    n_lv = len(HG_LEVELS)
    row_i = lax.broadcasted_iota(jnp.int32, (L, HG_DK), 0)
    q = [part(0, h) for h in heads]
    vb = [part(2, h).astype(BF16) for h in heads]
    f = [lb_ref[:, cols[h]] + (1.0 - lb_ref[:, cols[h]]) * jax.nn.sigmoid(part(1, h)) for h in heads]
    k = [1.0 - f[h] for h in heads]
    parts = [_split3(jnp.log2(f[h])) for h in heads]
    e_full = [(_dot(mst, parts[h][0]) + _dot(mst, parts[h][1])) + _dot(mst, parts[h][2]) for h in heads]
    b_last = [e_full[h][L - 1:L, :] for h in heads]

    def rel_to(b, blk, off):
        refs = []
        for r0 in range(0, L, blk):
            r = r0 + off - 1
            ref = b[r:r + 1, :] if r >= 0 else jnp.zeros((1, HG_DK), F32)
            refs.append(jnp.broadcast_to(ref, (blk, HG_DK)))
        return b - jnp.concatenate(refs, axis=0)

    def level_sums(b):
        out = [rel_to(b, HG_DIAG, 0)]
        for s_half in HG_LEVELS:
            d = rel_to(b, 2 * s_half, s_half)
            out.append(jnp.where((row_i % (2 * s_half)) >= s_half, d, -d))
        return out

    e = [level_sums(e_full[h]) for h in heads]
    wq = [[jnp.exp2(e[h][l]) for l in range(n_lv + 1)] for h in heads]
    wk = [[jnp.exp2(-e[h][0])] + wq[h][1:] for h in heads]
    prod = [[_dot_nt((q[h] * wq[h][l]).astype(BF16), (k[h] * wk[h][l]).astype(BF16)) for l in range(n_lv + 1)]
            for h in heads]
    st = [st_ref[h] for h in heads]
    inter = [_dot_nt((q[h] * jnp.exp2(e_full[h])).astype(BF16), st[h].astype(BF16)) for h in heads]
    k_dec = [(k[h] * jnp.exp2(b_last[h] - e_full[h])).astype(BF16) for h in heads]
    upd = [_dot_tn(vb[h], k_dec[h]) for h in heads]
    for h in heads:
        st_ref[h] = st[h] * jnp.exp2(b_last[h]) + upd[h]
    a = []
    for h in heads:
        ah = jnp.where(lvl == 1, prod[h][0], 0.0)
        for l in range(1, n_lv + 1):
            ah = jnp.where(lvl == l + 1, prod[h][l], ah)
        a.append(ah.astype(BF16))
    o = [_dot(a[h], vb[h]) + inter[h] for h in heads]
    for h in heads:
        oh = o[h] * lax.rsqrt(jnp.mean(o[h] * o[h], axis=-1, keepdims=True) + EPS) * gain_ref[:, cols[h]]
        o_ref[rows, cols[h]] = (oh * jax.nn.silu(part(3, h))).astype(o_ref.dtype)


def _cmp_body(kcn_ref, vcn_ref, pek_ref, pev_ref, w1k_ref, w1v_ref, w2k_ref, w2v_ref, kc_ref, vc_ref):
    nb = kcn_ref.shape[1] // CMP_STRIDE
    lane_grp = (lax.broadcasted_iota(jnp.int32, (nb, CMP_STRIDE * LANES), 1) // NSA_HEAD_DIM) % NSA_KV_HEADS

    def hidden(src_ref, pe_ref, w1_ref):
        x = jnp.concatenate([src_ref[0, pl.ds(l, nb, stride=CMP_STRIDE), :]
                             for l in range(CMP_STRIDE)], axis=1)
        halves = [x + pe_ref[i] for i in range(2)]
        out = []
        for g in range(NSA_KV_HEADS):
            u, v = (_dot(jnp.where(lane_grp == g, halves[i], 0.0).astype(BF16), w1_ref[i]) for i in range(2))
            out.append(jax.nn.silu(u + pltpu.roll(v, nb - 1, 0)).astype(BF16))
        return out

    hk = hidden(kcn_ref, pek_ref, w1k_ref)
    hv = hidden(vcn_ref, pev_ref, w1v_ref)
    for g in range(NSA_KV_HEADS):
        kc_ref[0, g, 0:nb, :] = _dot(hk[g], w2k_ref[0]).astype(kc_ref.dtype)
        kc_ref[0, g, nb:2 * nb, :] = _dot(hk[g], w2k_ref[1]).astype(kc_ref.dtype)
        vc_ref[0, g, :, 0:nb] = _dot_nt(w2v_ref[0], hv[g]).astype(vc_ref.dtype)
        vc_ref[0, g, :, nb:2 * nb] = _dot_nt(w2v_ref[1], hv[g]).astype(vc_ref.dtype)


def _cmp_call(kcn, vcn, pek, pev, w1k, w1v, w2k, w2v):
    b, t, w = kcn.shape
    nb = t // CMP_STRIDE
    full = lambda a: pl.BlockSpec(a.shape, lambda bi: (0,) * a.ndim)
    out = lambda r, c: pl.BlockSpec((1, NSA_KV_HEADS, r, c), lambda bi: (bi, 0, 0, 0))
    return pl.pallas_call(
        _cmp_body,
        grid=(b,),
        in_specs=[pl.BlockSpec((1, t, w), lambda bi: (bi, 0, 0)), pl.BlockSpec((1, t, w), lambda bi: (bi, 0, 0)),
                  full(pek), full(pev), full(w1k), full(w1v), full(w2k), full(w2v)],
        out_specs=[out(2 * nb, LANES), out(LANES, 2 * nb)],
        out_shape=[jax.ShapeDtypeStruct((b, NSA_KV_HEADS, 2 * nb, LANES), BF16),
                   jax.ShapeDtypeStruct((b, NSA_KV_HEADS, LANES, 2 * nb), BF16)],
        compiler_params=pltpu.CompilerParams(dimension_semantics=("arbitrary",),
                                             vmem_limit_bytes=VMEM_LIMIT),
        name="nsa_compress",
    )(kcn, vcn, pek, pev, w1k, w1v, w2k, w2v)


def _nsa_body(qt_ref, ksw_ref, vst_ref, vwt_ref, kc_ref, vct_ref, gt_ref, gain_ref, ovt_ref, o_ref,
              ks_ref, kw_ref, vs_ref, vw_ref, m_ref, acc_ref, s_ref, ch_ref, ocw_ref):
    g = pl.program_id(1)
    qi = pl.program_id(2)
    tq = ATT_TQ
    tk = ATT_TK
    t_len = ksw_ref.shape[1]
    n_kt = t_len // tk
    n_pairs = HPG // 2
    hd = NSA_HEAD_DIM

    @pl.when(qi == 0)
    def _build_kv():
        lane = lax.broadcasted_iota(jnp.int32, (tk, LANES), 1)
        lo_lane = lane < hd
        keep = (lane // hd) == g

        def build_k(src_col, dst_ref):
            for j in range(n_kt):
                x = ksw_ref[0, j * tk:(j + 1) * tk, src_col * LANES:(src_col + 1) * LANES].astype(F32)
                dup = jnp.where(keep, x, pltpu.roll(x, hd, 1))
                dst_ref[j, 0:tk, :] = jnp.where(lo_lane, dup, 0.0).astype(BF16)
                dst_ref[j, tk:2 * tk, :] = jnp.where(lo_lane, 0.0, dup).astype(BF16)

        def build_v(src_ref, dst_ref):
            zero = jnp.zeros((hd, tk), BF16)
            row = lax.broadcasted_iota(jnp.int32, (SUM_ROWS, 2 * tk), 0)
            col = lax.broadcasted_iota(jnp.int32, (SUM_ROWS, 2 * tk), 1)
            ones_rows = jnp.where(((row == 0) & (col < tk)) | ((row == 1) & (col >= tk)), 1.0, 0.0).astype(BF16)
            for j in range(n_kt):
                x = src_ref[0, :, j * tk:(j + 1) * tk]
                dst_ref[j, 0:hd, 0:tk] = x
                dst_ref[j, 0:hd, tk:2 * tk] = zero
                dst_ref[j, hd:2 * hd, 0:tk] = zero
                dst_ref[j, hd:2 * hd, tk:2 * tk] = x
                dst_ref[j, 2 * hd:2 * hd + SUM_ROWS, :] = ones_rows

        build_k(0, ks_ref)
        build_k(1, kw_ref)
        build_v(vst_ref, vs_ref)
        build_v(vwt_ref, vw_ref)

    t0 = qi * tq
    key_i = lax.broadcasted_iota(jnp.int32, (tk, tq), 0)
    qry_t = t0 + lax.broadcasted_iota(jnp.int32, (tk, tq), 1)
    slab_lo = lax.broadcasted_iota(jnp.int32, (LANES, tq), 0) < hd
    acc_row = lax.broadcasted_iota(jnp.int32, (LANES + SUM_ROWS, tq), 0)
    slab_a = (acc_row < hd) | (acc_row == LANES)
    q_pairs = [qt_ref[0, p * LANES:(p + 1) * LANES, :] for p in range(n_pairs)]

    last = (t0 + tq - 1) // tk

    def scores(k_ref, j):
        kt = k_ref[j]
        return [_dot(kt, q_pairs[p]) for p in range(n_pairs)]

    def live_keys(rel_at_origin, lower, upper):
        out = []
        for qh in range(tq // LANES):
            live = [kb for kb in range(tk // KEY_BLK)
                    if rel_at_origin + qh * LANES + LANES - 1 - kb * KEY_BLK >= lower
                    and rel_at_origin + qh * LANES - (kb * KEY_BLK + KEY_BLK - 1) < upper]
            out.append((min(live) * KEY_BLK, (max(live) + 1) * KEY_BLK))
        return out

    def tile_softmax(slot, bias, m_get, m_put, keys=None):
        alphas = {}
        probs = {}
        for hh in range(HPG):
            p, h = divmod(hh, 2)
            a_parts = []
            p_parts = []
            for qh in range(tq // LANES):
                ql = slice(qh * LANES, (qh + 1) * LANES)
                k_lo, k_hi = keys[qh] if keys is not None else (0, tk)
                sh = (s_ref[slot, p, h * tk + k_lo:h * tk + k_hi, ql]
                      + (bias if bias.shape == (1, 1) else bias[k_lo:k_hi, ql]))
                m_prev = m_get(hh, qh)
                m_new = jnp.maximum(m_prev, jnp.max(sh, axis=0, keepdims=True))
                m_put(hh, qh, m_new)
                piece = [jnp.zeros((k_lo, LANES), BF16)] if k_lo else []
                piece.append(jnp.exp2(sh - m_new).astype(BF16))
                if k_hi < tk:
                    piece.append(jnp.zeros((tk - k_hi, LANES), BF16))
                p_parts.append(jnp.concatenate(piece, axis=0) if len(piece) > 1 else piece[0])
                a_parts.append(jnp.exp2(m_prev - m_new))
            alphas[hh] = jnp.concatenate(a_parts, axis=1)
            probs[hh] = jnp.concatenate(p_parts, axis=1)
        return ([jnp.concatenate([probs[2 * p], probs[2 * p + 1]], axis=0) for p in range(n_pairs)],
                [jnp.where(slab_a, alphas[2 * p], alphas[2 * p + 1]) for p in range(n_pairs)])

    def normalised(acc):
        inv = jnp.where(slab_lo, 1.0 / acc[LANES:LANES + 1, :], 1.0 / acc[LANES + 1:LANES + 2, :])
        return acc[0:LANES] * inv

    def picked_bias(j, also=None):
        per_tile = tk // SLC_BLOCK
        picked = jnp.concatenate([jnp.broadcast_to(ch_ref[j * per_tile + i], (SLC_BLOCK, tq)) for i in range(per_tile)],
                                 axis=0) > 0.5
        return jnp.where(picked if also is None else picked & also, 0.0, NEG_INF)

    def m_put(hh, qh, v):
        m_ref[hh, :, qh * LANES:(qh + 1) * LANES] = v

    n_win = (WINDOW + tq) // tk
    n_near = SLC_NEAR_TILES
    n_far = jnp.maximum(last + 1 - n_near, 0)

    def static_part(first_k, has_far, need_topk):
        n_cmp_pad = kc_ref.shape[2] // 2
        blk_i = lax.broadcasted_iota(jnp.int32, (n_cmp_pad, tq), 0)
        blk_t = t0 + lax.broadcasted_iota(jnp.int32, (n_cmp_pad, tq), 1)
        cmp_ok = (blk_i * CMP_STRIDE + (CMP_BLOCK - 1)) <= blk_t
        kc = kc_ref[0, 0]
        vct = vct_ref[0, 0]
        s_cmp = [_dot(kc, q_pairs[p]) for p in range(n_pairs)]
        win_tiles = [(k, last - (n_win - 1) + k) for k in range(first_k, n_win)]
        near_tiles = [last - k for k in range(n_near)]
        up_front = ([(0, ks_ref, 0)] if has_far else []) + [(1 + k, kw_ref, jw) for k, jw in win_tiles]
        up_front += [(1 + n_win + k, ks_ref, jnp.maximum(jn, 0)) for k, jn in enumerate(near_tiles)]
        for slot, k_ref, j0 in up_front:
            s_first = scores(k_ref, j0)
            for p in range(n_pairs):
                s_ref[slot, p] = s_first[p]
        p_sum = jnp.zeros((n_cmp_pad, tq), F32)
        p_cmp = []
        for p in range(n_pairs):
            probs = []
            for h in range(2):
                sh = jnp.where(cmp_ok, s_cmp[p][h * n_cmp_pad:(h + 1) * n_cmp_pad], NEG_INF)
                mh = jnp.max(sh, axis=0, keepdims=True)
                eh = jnp.where(cmp_ok, jnp.exp2(sh - mh), 0.0)
                den = jnp.sum(eh, axis=0, keepdims=True)
                ph = eh / jnp.where(den > 0.0, den, 1.0)
                p_sum = p_sum + ph
                probs.append(ph.astype(BF16))
            p_cmp.append(jnp.concatenate(probs, axis=0))
        o_cmp = [_dot(vct, p_cmp[p]) for p in range(n_pairs)]

        n_sel = t_len // SLC_BLOCK
        if need_topk:
            hi, mid, lo = _split3(p_sum)
            ovt = ovt_ref[...]
            p_sel = ((_dot(ovt, hi) + _dot(ovt, mid)) + _dot(ovt, lo))[0:n_sel]
            sel_i = lax.broadcasted_iota(jnp.int32, (n_sel, tq), 0)
            cur = (t0 + lax.broadcasted_iota(jnp.int32, (n_sel, tq), 1)) // SLC_BLOCK
            forced = (sel_i == 0) | (sel_i == cur) | (sel_i == cur - 1)
            score = jnp.where(forced, FORCE_SCORE, p_sel)
            score = jnp.where(sel_i <= cur, score, -jnp.inf)
            rank = jnp.zeros((n_sel, tq), jnp.int32)
            row_grp = SUBLANES
            grp_i = lax.broadcasted_iota(jnp.int32, (row_grp, tq), 0)
            for i in range(n_sel):
                ci = score[i:i + 1, :]
                ahead = []
                for r0 in range(0, n_sel, row_grp):
                    rows = slice(r0, r0 + row_grp)
                    if r0 > i:
                        ahead.append(ci >= score[rows])
                    elif r0 + row_grp <= i:
                        ahead.append(ci > score[rows])
                    else:
                        ahead.append((ci > score[rows]) | ((ci == score[rows]) & (grp_i > i - r0)))
                rank = rank + jnp.where(jnp.concatenate(ahead, axis=0), 1, 0)
            chosen = jnp.where(rank < min(SLC_TOPK, n_sel), 1.0, 0.0)
        else:
            sel_i = lax.broadcasted_iota(jnp.int32, (n_sel, tq), 0)
            cur = (t0 + lax.broadcasted_iota(jnp.int32, (n_sel, tq), 1)) // SLC_BLOCK
            chosen = jnp.where(sel_i <= cur, 1.0, 0.0)
        for i in range(n_sel):
            ch_ref[i] = chosen[i:i + 1, :]

        m_win = {}
        acc_win = [jnp.zeros((LANES + SUM_ROWS, tq), F32) for _ in range(n_pairs)]
        for k, jw in win_tiles:
            rel_hi = (n_win - k) * tk - 1
            rel_lo = rel_hi - (tq - 1) - (tk - 1)
            exists = jnp.zeros((1, 1), F32)
            if rel_lo >= 0 and rel_hi < WINDOW:
                bias = exists
            else:
                rel = qry_t - (jw * tk + key_i)
                inside = (rel < WINDOW) if rel_lo >= 0 else (rel >= 0) if rel_hi < WINDOW else (rel >= 0) & (rel < WINDOW)
                bias = jnp.where(inside, exists, NEG_INF)
            probs, a_rows = tile_softmax(1 + k, bias,
                                         lambda hh, qh: m_win.get((hh, qh), jnp.full((1, LANES), -jnp.inf, F32)),
                                         lambda hh, qh, v: m_win.__setitem__((hh, qh), v),
                                         keys=live_keys(rel_lo + tk - 1, 0, WINDOW))
            vt = vw_ref[jw]
            acc_win = [acc_win[p] * a_rows[p] + _dot(vt, probs[p]) for p in range(n_pairs)]
        for p in range(n_pairs):
            ocw_ref[0, p] = o_cmp[p]
            ocw_ref[1, p] = normalised(acc_win[p])

        m_near = {}
        acc_near = [jnp.zeros((LANES + SUM_ROWS, tq), F32) for _ in range(n_pairs)]
        for k, jn in enumerate(near_tiles):
            jc = jnp.maximum(jn, 0)
            causal = (jn * tk + key_i) <= qry_t
            if k > 0:
                causal = causal & (jn >= 0)
            probs, a_rows = tile_softmax(1 + n_win + k, picked_bias(jc, causal),
                                         lambda hh, qh: m_near.get((hh, qh), jnp.full((1, LANES), -jnp.inf, F32)),
                                         lambda hh, qh, v: m_near.__setitem__((hh, qh), v),
                                         keys=live_keys(tk - tq + k * tk, 0, t_len))
            vt = vs_ref[jc]
            acc_near = [acc_near[p] * a_rows[p] + _dot(vt, probs[p]) for p in range(n_pairs)]

        for (hh, qh), v in m_near.items():
            m_put(hh, qh, v)
        for p in range(n_pairs):
            acc_ref[p] = acc_near[p]

    def variant(e):
        tiles_upto = (e + 1) * tq // tk
        return (max(0, n_win - tiles_upto), tiles_upto > n_near, (e + 1) * tq > SLC_TOPK * SLC_BLOCK)

    n_q = t_len // tq
    start = 0
    for e in range(1, n_q + 1):
        if e == n_q or variant(e) != variant(start):
            pl.when((qi >= start) & (qi < e))(functools.partial(static_part, *variant(start)))
            start = e
    o_cmp = [ocw_ref[0, p] for p in range(n_pairs)]
    o_win = [ocw_ref[1, p] for p in range(n_pairs)]

    def slc_step(j, carry):
        s_next = scores(ks_ref, jnp.minimum(j + 1, n_far - 1))
        probs, a_rows = tile_softmax(0, picked_bias(j), lambda hh, qh: m_ref[hh, :, qh * LANES:(qh + 1) * LANES], m_put)
        vt = vs_ref[j]
        for p in range(n_pairs):
            pv = _dot(vt, probs[p])
            s_ref[0, p] = s_next[p]
            acc_ref[p] = acc_ref[p] * a_rows[p] + pv
        return carry

    lax.fori_loop(0, n_far, slc_step, 0)
    o_slc = [normalised(acc_ref[p]) for p in range(n_pairs)]

    gates = gt_ref[0]
    gain = gain_ref[...]
    for p in range(n_pairs):
        o = jnp.zeros((LANES, tq), F32)
        for c, branch in enumerate((o_cmp[p], o_slc[p], o_win[p])):
            r = c * HPG + 2 * p
            o = o + jnp.where(slab_lo, gates[r:r + 1, :], gates[r + 1:r + 2, :]) * branch
        sq = o * o
        ms_a = jnp.sum(sq[0:hd], axis=0, keepdims=True)
        ms_b = jnp.sum(sq[hd:2 * hd], axis=0, keepdims=True)
        ms = jnp.where(slab_lo, ms_a, ms_b) * (1.0 / hd)
        o = o * lax.rsqrt(ms + EPS)
        o_ref[0, p * LANES:(p + 1) * LANES, :] = (o * gain[p * LANES:(p + 1) * LANES, :]).astype(o_ref.dtype)


def _nsa_call(qt, ksw, vt, kc, vct, gt, gain, ovt):
    b, _, t = qt.shape
    tq, tk = ATT_TQ, ATT_TK
    n_kt = t // tk
    gw = HPG * NSA_HEAD_DIM
    hd = NSA_HEAD_DIM
    k_scratch = pltpu.VMEM((n_kt, 2 * tk, LANES), BF16)
    v_scratch = pltpu.VMEM((n_kt, LANES + SUM_ROWS, 2 * tk), BF16)
    return pl.pallas_call(
        _nsa_body,
        grid=(b, NSA_KV_HEADS, t // tq),
        in_specs=[
            pl.BlockSpec((1, gw, tq), lambda bi, gi, qi: (bi, gi, qi)),
            pl.BlockSpec((1, t, 2 * KV_WIDTH), lambda bi, gi, qi: (bi, 0, 0)),
            pl.BlockSpec((1, hd, t), lambda bi, gi, qi: (bi, gi, 0)),
            pl.BlockSpec((1, hd, t), lambda bi, gi, qi: (bi, NSA_KV_HEADS + gi, 0)),
            pl.BlockSpec((1, 1) + kc.shape[2:], lambda bi, gi, qi: (bi, gi, 0, 0)),
            pl.BlockSpec((1, 1) + vct.shape[2:], lambda bi, gi, qi: (bi, gi, 0, 0)),
            pl.BlockSpec((1, LANES, tq), lambda bi, gi, qi: (bi, gi, qi)),
            pl.BlockSpec((gw, 1), lambda bi, gi, qi: (gi, 0)),
            pl.BlockSpec(ovt.shape, lambda bi, gi, qi: (0, 0)),
        ],
        out_specs=pl.BlockSpec((1, gw, tq), lambda bi, gi, qi: (bi, gi, qi)),
        out_shape=jax.ShapeDtypeStruct((b, NSA_WIDTH, t), BF16),
        scratch_shapes=[k_scratch, k_scratch, v_scratch, v_scratch,
                        pltpu.VMEM((HPG, 1, tq), F32),
                        pltpu.VMEM((HPG // 2, LANES + SUM_ROWS, tq), F32), pltpu.VMEM((1 + (WINDOW + tq) // tk + SLC_NEAR_TILES, HPG // 2, 2 * tk, tq), F32),
                        pltpu.VMEM((t // SLC_BLOCK, 1, tq), F32), pltpu.VMEM((2, HPG // 2, LANES, tq), F32)],
        compiler_params=pltpu.CompilerParams(dimension_semantics=("arbitrary", "arbitrary", "arbitrary"),
                                             vmem_limit_bytes=VMEM_LIMIT),
        name="nsa_attention",
    )(qt, ksw, vt, vt, kc, vct, gt, gain, ovt)


def _ffn_body(x_ref, oh_ref, on_ref, woh_ref, won_ref, g2_ref, wg_ref, wu_ref, wd_ref, cw_ref, gf_ref,
              out_ref, halo_ref, act_ref, *, tiles_per_seq):
    tm = x_ref.shape[0]
    x1 = x_ref[...] + _dot(oh_ref[...], woh_ref[...]) + _dot_tn(on_ref[0], won_ref[...])
    hb = _rms(x1, g2_ref[...]).astype(BF16)
    row = lax.broadcasted_iota(jnp.int32, (tm, FFN_TC), 0)

    @pl.when((pl.program_id(0) % tiles_per_seq) == 0)
    def _sequence_start():
        halo_ref[...] = jnp.zeros_like(halo_ref)

    def activation(c, gate, up):
        cols = slice(c * FFN_TC, (c + 1) * FFN_TC)
        halo = halo_ref[:, cols]
        halo_ref[:, cols] = gate[tm - SUBLANES:tm, :]
        last1 = halo[SUBLANES - 1:SUBLANES, :]
        last2 = halo[SUBLANES - 2:SUBLANES - 1, :]
        prev1 = jnp.where(row == 0, last1, pltpu.roll(gate, 1, 0))
        prev2 = jnp.where(row == 0, last2, jnp.where(row == 1, last1, pltpu.roll(gate, 2, 0)))
        cw = cw_ref[:, cols]
        y = cw[0:1, :] * prev2 + cw[1:2, :] * prev1 + cw[2:3, :] * gate + cw[3:4, :]
        return (jax.nn.silu(y) * up).astype(BF16)

    chunk = lambda w_ref, c: _dot(hb, w_ref[:, c * FFN_TC:(c + 1) * FFN_TC].astype(BF16))
    gate_up = (chunk(wg_ref, 0), chunk(wu_ref, 0))
    for c in range(FFN_NC):
        cur = gate_up
        if c + 1 < FFN_NC:
            gate_up = (chunk(wg_ref, c + 1), chunk(wu_ref, c + 1))
        act_ref[:, c * FFN_TC:(c + 1) * FFN_TC] = activation(c, *cur)
    acc = _dot(act_ref[...], wd_ref[...].astype(BF16))
    out_ref[...] = _rms(x1 + acc, gf_ref[...])


def _ffn_call(x2, oh, on, woh, won, g2, wg, wu, wd, cw, gf, tiles_per_seq):
    n = x2.shape[0]
    tm = FFN_TM
    row = lambda w: pl.BlockSpec((tm, w), lambda i: (i, 0))
    full = lambda a: pl.BlockSpec(a.shape, lambda i: (0,) * a.ndim, pipeline_mode=pl.Buffered(1))
    return pl.pallas_call(
        functools.partial(_ffn_body, tiles_per_seq=tiles_per_seq),
        grid=(n // tm,),
        in_specs=[row(D_MODEL), row(HG_WIDTH),
                  pl.BlockSpec((1, NSA_WIDTH, tm), lambda i: (i // tiles_per_seq, 0, i % tiles_per_seq)),
                  full(woh), full(won), full(g2),
                  full(wg), full(wu), full(wd), full(cw), full(gf)],
        out_specs=row(D_MODEL),
        out_shape=jax.ShapeDtypeStruct((n, D_MODEL), F32),
        scratch_shapes=[pltpu.VMEM((SUBLANES, D_FF), F32), pltpu.VMEM((tm, D_FF), BF16)],
        compiler_params=pltpu.CompilerParams(dimension_semantics=("arbitrary",),
                                             vmem_limit_bytes=VMEM_LIMIT),
        name="outproj_convffn",
    )(x2, oh, on, woh, won, g2, wg, wu, wd, cw, gf)


def _rope_angles(positions):
    inv_freq = ROPE_THETA ** (-jnp.arange(ROPE_HALF, dtype=F32) * 2.0 / ROPE_DIM)
    ang = positions.astype(F32)[..., None] * inv_freq
    return jnp.concatenate([jnp.cos(ang), jnp.sin(ang)], axis=-1).transpose(0, 2, 1)


def _layer(x, positions, ln1, w_in, lb, hg_gain, pe_k, pe_v, k_w1, k_w2, v_w1, v_w2, nsa_gain, w_o, ln2,
           w_gate, w_up, conv_w, conv_b, w_down, final_gain):
    b, t, d = x.shape
    n = b * t
    assert d == D_MODEL and t % FFN_TM == 0 and t % PROJ_TM == 0 and t % ATT_TQ == 0
    assert PROJ_TM % PROJ_PIECE == 0 and PROJ_PIECE % HG_CHUNK == 0
    n_grp = t // CMP_STRIDE
    assert n_grp == LANES, "compressed-block axis is laid out on exactly one lane tile"
    n_sel = t // SLC_BLOCK
    assert n_sel % 8 == 0 and n_sel <= LANES and ATT_TK % SLC_BLOCK == 0 and ATT_TQ % ATT_TK == 0
    x2 = x.reshape(n, d)

    splits = np.cumsum([0, 4 * HG_WIDTH, NSA_WIDTH] + [KV_WIDTH] * 6 + [N_GATES])
    seg = lambda i: w_in[:, splits[i]:splits[i + 1]]
    wh = seg(0).astype(BF16)
    wk = jnp.concatenate([seg(2), seg(3), seg(4), seg(6)], axis=1).astype(BF16)
    wgate = seg(8).reshape(d, 3, NSA_KV_HEADS, HPG).transpose(0, 2, 1, 3).reshape(d, NSA_KV_HEADS, 3 * HPG)
    wgate = jnp.pad(wgate, ((0, 0), (0, 0), (0, LANES - 3 * HPG))).reshape(d, NSA_KV_HEADS * LANES)
    wt = jnp.concatenate([seg(1), seg(5), seg(7), wgate], axis=1).T.astype(BF16)
    cs = _rope_angles(positions)

    mst, lvl = _hgrn_tables()
    o_hg, kcn, vcn, ksw, qt, vt, gt = _inproj_call(x2, ln1.reshape(1, d), wh, wk, wt, cs,
                                                   lb.reshape(1, HG_WIDTH).astype(F32), hg_gain.reshape(1, HG_WIDTH),
                                                   mst, lvl, t // PROJ_TM)
    later = (pe_k, pe_v, k_w1, k_w2, v_w1, v_w2, nsa_gain, w_o, ln2, conv_w, conv_b, final_gain)
    later, kcn = lax.optimization_barrier((later, kcn))
    pe_k, pe_v, k_w1, k_w2, v_w1, v_w2, nsa_gain, w_o, ln2, conv_w, conv_b, final_gain = later

    per_lane = lambda a: jnp.broadcast_to(a.reshape(2, CMP_STRIDE, 1, NSA_HEAD_DIM, -1),
                                          (2, CMP_STRIDE, NSA_KV_HEADS, NSA_HEAD_DIM, a.shape[-1]))
    w1_rows = lambda w1: per_lane(w1).reshape(2, CMP_STRIDE * LANES, CMP_HIDDEN).astype(BF16)
    pe_rows = lambda pe: per_lane(pe[..., None]).reshape(2, 1, CMP_STRIDE * LANES)
    zeros_w2 = jnp.zeros((CMP_HIDDEN, NSA_HEAD_DIM), F32)
    place = lambda w2: jnp.stack([jnp.concatenate([w2, zeros_w2], 1), jnp.concatenate([zeros_w2, w2], 1)])
    kc, vct = _cmp_call(kcn.reshape(b, t, KV_WIDTH), vcn.reshape(b, t, KV_WIDTH), pe_rows(pe_k), pe_rows(pe_v),
                        w1_rows(k_w1), w1_rows(v_w1),
                        place(k_w2).astype(BF16), place(v_w2).transpose(0, 2, 1).astype(BF16))

    cmp_start = np.arange(n_grp) * CMP_STRIDE
    cmp_end = cmp_start + CMP_BLOCK - 1
    sel_start = np.arange(LANES) * SLC_BLOCK
    overlap = ((cmp_start[:, None] <= sel_start[None, :] + SLC_BLOCK - 1) & (cmp_end[:, None] >= sel_start[None, :])
               & (np.arange(LANES)[None, :] < n_sel) & (np.arange(n_grp)[:, None] < n_grp - 1))
    ovt = jnp.asarray(overlap.T.astype(np.float32), BF16)
    o_nsa = _nsa_call(qt, ksw.reshape(b, t, 2 * KV_WIDTH), vt, kc, vct, gt, nsa_gain.reshape(NSA_WIDTH, 1), ovt)

    cw = jnp.concatenate([conv_w, conv_b[None, :], jnp.zeros((SUBLANES - CONV_WIDTH - 1, D_FF), F32)], axis=0)
    out = _ffn_call(x2, o_hg, o_nsa,
                    w_o[:HG_WIDTH].astype(BF16), w_o[HG_WIDTH:].astype(BF16), ln2.reshape(1, d),
                    w_gate, w_up, w_down, cw,
                    final_gain.reshape(1, d), t // FFN_TM)
    return out.reshape(b, t, d)


def kernel(x, positions, ln1_gain, w_in, hgrn_lb_param, hgrn_out_gain, cmp_pe_k, cmp_pe_v, cmp_k_w1, cmp_k_w2,
           cmp_v_w1, cmp_v_w2, nsa_out_gain, w_o, ln2_gain, ffn_w_gate, ffn_w_up, ffn_conv_w, ffn_conv_b,
           ffn_w_down, final_gain):
    depth = ln1_gain.shape[0]
    assert depth == 1, "the fused final norm assumes a single layer"
    lower_bounds = jnp.cumsum(jax.nn.softmax(hgrn_lb_param.astype(F32), axis=0), axis=0)
    l = 0
    return _layer(x, positions, ln1_gain[l], w_in[l], lower_bounds[l], hgrn_out_gain[l], cmp_pe_k[l], cmp_pe_v[l],
                  cmp_k_w1[l], cmp_k_w2[l], cmp_v_w1[l], cmp_v_w2[l], nsa_out_gain[l], w_o[l], ln2_gain[l],
                  ffn_w_gate[l], ffn_w_up[l], ffn_conv_w[l], ffn_conv_b[l], ffn_w_down[l], final_gain)
```
